```python
import jax, jax.numpy as jnp
from jax import lax
import numpy as np

D_MODEL = 1024
BATCH = 8
SEQ = 2048
DEPTH = 2

N_META = 16
D_FF = (11 * D_MODEL) // 4
LRU_WIDTH = D_MODEL // 4
LRU_BLOCKS = 4
LRU_BLOCK = LRU_WIDTH // LRU_BLOCKS
LRU_CONV = 4
LRU_C = 8.0
SC_WIDTH = D_MODEL // 4
SC_GROUPS = 4
SC_CONV = 3
RWKV_WIDTH = D_MODEL // 2
RWKV_HEAD = 64
RWKV_HEADS = RWKV_WIDTH // RWKV_HEAD
DECAY_LORA = 32
ICL_LORA = 32
GATE_LORA = 64
MIX_WIDTH = LRU_WIDTH + SC_WIDTH + RWKV_WIDTH
RWKV_IN = 3 * RWKV_WIDTH + DECAY_LORA + ICL_LORA + GATE_LORA
N_IN = 2 * LRU_WIDTH + 3 * SC_WIDTH + RWKV_IN
IN_SPLIT_IDX = (LRU_WIDTH, 2 * LRU_WIDTH, 2 * LRU_WIDTH + SC_WIDTH,
                2 * LRU_WIDTH + 2 * SC_WIDTH, 2 * LRU_WIDTH + 3 * SC_WIDTH)
RWKV_SPLIT_IDX = (RWKV_WIDTH, 2 * RWKV_WIDTH, 3 * RWKV_WIDTH,
                  3 * RWKV_WIDTH + DECAY_LORA, 3 * RWKV_WIDTH + DECAY_LORA + ICL_LORA)
RMS_EPS = 1e-6
LNX_EPS = 64e-5

kernel_name = "hybrid_rglru_shortconv_rwkv7_macaron"


def rms_norm(x, g):
    xf = x.astype(jnp.float32)
    y = xf * lax.rsqrt(jnp.mean(xf * xf, axis=-1, keepdims=True) + RMS_EPS)
    return (y * g).astype(x.dtype)


def group_rms_norm(x, g, n_groups):
    shp = x.shape
    xf = x.astype(jnp.float32).reshape(shp[:-1] + (n_groups, shp[-1] // n_groups))
    y = xf * lax.rsqrt(jnp.mean(xf * xf, axis=-1, keepdims=True) + RMS_EPS)
    return (y.reshape(shp) * g).astype(x.dtype)


def causal_dwconv(x, w):
    k_w, t = w.shape[0], x.shape[1]
    xp = jnp.pad(x, ((0, 0), (k_w - 1, 0), (0, 0)))
    y = xp[:, 0:t] * w[0]
    for k in range(1, k_w):
        y = y + xp[:, k:k + t] * w[k]
    return y


def token_shift(x):
    return jnp.pad(x, ((0, 0), (1, 0), (0, 0)))[:, :-1]


def swiglu(x, w_in, w_out):
    gate, up = jnp.split(x @ w_in, 2, axis=-1)
    return (jax.nn.silu(gate) * up) @ w_out


def _linear_rec_combine(c1, c2):
    a1, b1 = c1
    a2, b2 = c2
    return a1 * a2, a2 * b1 + b2


def rg_lru_mixer(xb, gb, conv_w, conv_b, wa, ba, wx, bx, lam):
    bsz, t, _ = xb.shape
    u = (causal_dwconv(xb, conv_w) + conv_b).astype(jnp.float32)
    ub = u.reshape(bsz, t, LRU_BLOCKS, LRU_BLOCK)
    r = jax.nn.sigmoid(jnp.einsum('btgi,gij->btgj', ub, wa).reshape(bsz, t, LRU_WIDTH) + ba)
    i = jax.nn.sigmoid(jnp.einsum('btgi,gij->btgj', ub, wx).reshape(bsz, t, LRU_WIDTH) + bx)
    log_a = -LRU_C * r * jax.nn.softplus(-lam)
    a = jnp.exp(log_a)
    b = jnp.sqrt(-jnp.expm1(2.0 * log_a)) * (i * u)
    _, h = lax.associative_scan(_linear_rec_combine, (a, b), axis=1)
    return jax.nn.gelu(gb.astype(jnp.float32)) * h


def short_conv_mixer(sc_b, sc_c, sc_x, conv_w):
    return sc_b * causal_dwconv(sc_c * sc_x, conv_w)


def rwkv7_mixer(z, mu, w0, w2, a0, a2, g2, k_k, k_a, r_k, lnx_w, lnx_b):
    bsz, t, _ = z.shape
    z = (z + (token_shift(z) - z) * mu).astype(jnp.float32)
    r, k, v, w_lo, a_lo, g_lo = jnp.split(z, RWKV_SPLIT_IDX, axis=-1)
    w_log = -jax.nn.softplus(-(w0 + jnp.tanh(w_lo) @ w2)) - 0.5
    decay = jnp.exp(-jnp.exp(w_log))
    a = jax.nn.sigmoid(a0 + a_lo @ a2)
    g = jax.nn.sigmoid(g_lo) @ g2
    kk = k * k_k
    k = k * (1.0 + (a - 1.0) * k_a)

    def heads(q):
        return q.reshape(bsz, t, RWKV_HEADS, RWKV_HEAD)

    r, k, v, kk, a, decay = (heads(q) for q in (r, k, v, kk, a, decay))
    kk = kk * lax.rsqrt(jnp.maximum(jnp.sum(kk * kk, axis=-1, keepdims=True), 1e-24))

    def step(s, inp):
        r_t, w_t, k_t, v_t, kk_t, a_t = inp
        sa = jnp.einsum('bhvk,bhk->bhv', s, -kk_t)
        s = (s * w_t[:, :, None, :] + sa[..., None] * (kk_t * a_t)[:, :, None, :]
             + v_t[..., None] * k_t[:, :, None, :])
        return s, jnp.einsum('bhvk,bhk->bhv', s, r_t)

    seq = tuple(jnp.moveaxis(q, 1, 0) for q in (r, decay, k, v, kk, a))
    s0 = jnp.zeros((bsz, RWKV_HEADS, RWKV_HEAD, RWKV_HEAD), jnp.float32)
    _, y = lax.scan(step, s0, seq)
    y = jnp.moveaxis(y, 0, 1)
    mean = jnp.mean(y, axis=-1, keepdims=True)
    var = jnp.mean(jnp.square(y - mean), axis=-1, keepdims=True)
    y = ((y - mean) * lax.rsqrt(var + LNX_EPS)).reshape(bsz, t, RWKV_WIDTH) * lnx_w + lnx_b
    bonus = jnp.sum(r * k * r_k.reshape(RWKV_HEADS, RWKV_HEAD), axis=-1, keepdims=True) * v
    y = y + bonus.reshape(bsz, t, RWKV_WIDTH)
    return y * g


def hybrid_mixer(u, w_in, w_out, lru_conv_w, lru_conv_b, lru_wa, lru_ba, lru_wx, lru_bx,
                 lru_lambda, lru_norm_g, sc_conv_w, sc_norm_g, rwkv_mu, rwkv_w0, rwkv_w2,
                 rwkv_a0, rwkv_a2, rwkv_g2, rwkv_k_k, rwkv_k_a, rwkv_r_k, rwkv_lnx_w, rwkv_lnx_b):
    p = u @ w_in
    lru_x, lru_g, sc_b, sc_c, sc_x, rw = jnp.split(p, IN_SPLIT_IDX, axis=-1)
    y_lru = group_rms_norm(
        rg_lru_mixer(lru_x, lru_g, lru_conv_w, lru_conv_b, lru_wa, lru_ba, lru_wx, lru_bx, lru_lambda),
        lru_norm_g, LRU_BLOCKS)
    y_sc = group_rms_norm(short_conv_mixer(sc_b, sc_c, sc_x, sc_conv_w), sc_norm_g, SC_GROUPS)
    y_rw = rwkv7_mixer(rw, rwkv_mu, rwkv_w0, rwkv_w2, rwkv_a0, rwkv_a2, rwkv_g2,
                       rwkv_k_k, rwkv_k_a, rwkv_r_k, rwkv_lnx_w, rwkv_lnx_b)
    y = jnp.concatenate([y_lru.astype(u.dtype), y_sc.astype(u.dtype), y_rw.astype(u.dtype)], axis=-1)
    return y @ w_out


def _fwd_setup_inputs(seed: int = 0) -> dict:
    key = jax.random.key(seed)
    ks = jax.random.split(key, 32)
    L, D = DEPTH, D_MODEL
    nrm = jax.random.normal
    u01 = jax.random.uniform(ks[14], (L, LRU_WIDTH), minval=0.9, maxval=0.999)
    a_base = u01 ** (1.0 / LRU_C)
    return {
        'x': nrm(ks[0], (BATCH, SEQ, D), jnp.float32),
        'meta_tokens': nrm(ks[1], (N_META, D), jnp.float32),
        'norm_g': 1.0 + 0.02 * nrm(ks[2], (L, 6, D), jnp.float32),
        'ffn1_w_in': nrm(ks[3], (L, D, 2 * D_FF), jnp.float32) * D ** -0.5,
        'ffn1_w_out': nrm(ks[4], (L, D_FF, D), jnp.float32) * D_FF ** -0.5,
        'ffn2_w_in': nrm(ks[5], (L, D, 2 * D_FF), jnp.float32) * D ** -0.5,
        'ffn2_w_out': nrm(ks[6], (L, D_FF, D), jnp.float32) * D_FF ** -0.5,
        'mix_w_in': nrm(ks[7], (L, D, N_IN), jnp.float32) * D ** -0.5,
        'mix_w_out': nrm(ks[8], (L, MIX_WIDTH, D), jnp.float32) * MIX_WIDTH ** -0.5,
        'lru_conv_w': nrm(ks[9], (L, LRU_CONV, LRU_WIDTH), jnp.float32) * LRU_CONV ** -0.5,
        'lru_conv_b': 0.02 * nrm(ks[10], (L, LRU_WIDTH), jnp.float32),
        'lru_wa': nrm(ks[11], (L, LRU_BLOCKS, LRU_BLOCK, LRU_BLOCK), jnp.float32) * LRU_BLOCK ** -0.5,
        'lru_ba': 0.1 * nrm(ks[12], (L, LRU_WIDTH), jnp.float32),
        'lru_wx': nrm(ks[13], (L, LRU_BLOCKS, LRU_BLOCK, LRU_BLOCK), jnp.float32) * LRU_BLOCK ** -0.5,
        'lru_bx': 0.1 * nrm(ks[15], (L, LRU_WIDTH), jnp.float32),
        'lru_lambda': jnp.log(a_base) - jnp.log1p(-a_base),
        'lru_norm_g': 1.0 + 0.02 * nrm(ks[16], (L, LRU_WIDTH), jnp.float32),
        'sc_conv_w': nrm(ks[17], (L, SC_CONV, SC_WIDTH), jnp.float32) * SC_CONV ** -0.5,
        'sc_norm_g': 1.0 + 0.02 * nrm(ks[18], (L, SC_WIDTH), jnp.float32),
        'rwkv_mu': jax.random.uniform(ks[19], (L, RWKV_IN), jnp.float32),
        'rwkv_w0': jnp.linspace(-6.0, -1.0, RWKV_WIDTH, dtype=jnp.float32)[None, :]
                   + 0.1 * nrm(ks[20], (L, RWKV_WIDTH), jnp.float32),
        'rwkv_w2': nrm(ks[21], (L, DECAY_LORA, RWKV_WIDTH), jnp.float32) * DECAY_LORA ** -0.5,
        'rwkv_a0': 0.1 * nrm(ks[22], (L, RWKV_WIDTH), jnp.float32),
        'rwkv_a2': nrm(ks[23], (L, ICL_LORA, RWKV_WIDTH), jnp.float32) * ICL_LORA ** -0.5,
        'rwkv_g2': nrm(ks[24], (L, GATE_LORA, RWKV_WIDTH), jnp.float32) * GATE_LORA ** -0.5,
        'rwkv_k_k': 0.85 + 0.02 * nrm(ks[25], (L, RWKV_WIDTH), jnp.float32),
        'rwkv_k_a': 1.0 + 0.02 * nrm(ks[26], (L, RWKV_WIDTH), jnp.float32),
        'rwkv_r_k': 0.1 * nrm(ks[27], (L, RWKV_WIDTH), jnp.float32),
        'rwkv_lnx_w': 1.0 + 0.02 * nrm(ks[28], (L, RWKV_WIDTH), jnp.float32),
        'rwkv_lnx_b': 0.02 * nrm(ks[29], (L, RWKV_WIDTH), jnp.float32),
    }


def _fwd_reference(x, meta_tokens, norm_g, ffn1_w_in, ffn1_w_out, ffn2_w_in, ffn2_w_out,
              mix_w_in, mix_w_out, lru_conv_w, lru_conv_b, lru_wa, lru_ba, lru_wx, lru_bx,
              lru_lambda, lru_norm_g, sc_conv_w, sc_norm_g, rwkv_mu, rwkv_w0, rwkv_w2,
              rwkv_a0, rwkv_a2, rwkv_g2, rwkv_k_k, rwkv_k_a, rwkv_r_k, rwkv_lnx_w, rwkv_lnx_b):
    bsz = x.shape[0]
    meta = jnp.broadcast_to(meta_tokens.astype(x.dtype)[None], (bsz, N_META, x.shape[-1]))
    h = jnp.concatenate([meta, x], axis=1)
    for l in range(DEPTH):
        g = norm_g[l]
        h = h + 0.5 * rms_norm(swiglu(rms_norm(h, g[0]), ffn1_w_in[l], ffn1_w_out[l]), g[1])
        m = hybrid_mixer(rms_norm(h, g[2]), mix_w_in[l], mix_w_out[l],
                         lru_conv_w[l], lru_conv_b[l], lru_wa[l], lru_ba[l], lru_wx[l], lru_bx[l],
                         lru_lambda[l], lru_norm_g[l], sc_conv_w[l], sc_norm_g[l],
                         rwkv_mu[l], rwkv_w0[l], rwkv_w2[l], rwkv_a0[l], rwkv_a2[l], rwkv_g2[l],
                         rwkv_k_k[l], rwkv_k_a[l], rwkv_r_k[l], rwkv_lnx_w[l], rwkv_lnx_b[l])
        h = h + rms_norm(m, g[3])
        h = h + 0.5 * rms_norm(swiglu(rms_norm(h, g[4]), ffn2_w_in[l], ffn2_w_out[l]), g[5])
    return h[:, N_META:]


import jax as _jax
import jax.numpy as _jnp

TWIN_FORMAT = 'train_step'
FWD_PARAMS = ['x', 'meta_tokens', 'norm_g', 'ffn1_w_in', 'ffn1_w_out', 'ffn2_w_in', 'ffn2_w_out', 'mix_w_in', 'mix_w_out', 'lru_conv_w', 'lru_conv_b', 'lru_wa', 'lru_ba', 'lru_wx', 'lru_bx', 'lru_lambda', 'lru_norm_g', 'sc_conv_w', 'sc_norm_g', 'rwkv_mu', 'rwkv_w0', 'rwkv_w2', 'rwkv_a0', 'rwkv_a2', 'rwkv_g2', 'rwkv_k_k', 'rwkv_k_a', 'rwkv_r_k', 'rwkv_lnx_w', 'rwkv_lnx_b']
TWIN_WEIGHTS = ['meta_tokens', 'norm_g', 'ffn1_w_in', 'ffn1_w_out', 'ffn2_w_in', 'ffn2_w_out', 'mix_w_in', 'mix_w_out', 'lru_conv_w', 'lru_conv_b', 'lru_wa', 'lru_ba', 'lru_wx', 'lru_bx', 'lru_lambda', 'lru_norm_g', 'sc_conv_w', 'sc_norm_g', 'rwkv_mu', 'rwkv_w0', 'rwkv_w2', 'rwkv_a0', 'rwkv_a2', 'rwkv_g2', 'rwkv_k_k', 'rwkv_k_a', 'rwkv_r_k', 'rwkv_lnx_w', 'rwkv_lnx_b']
TWIN_DIFF_INPUT = 'x'
TWIN_INPUTS = ['x', 'meta_tokens', 'norm_g', 'ffn1_w_in', 'ffn1_w_out', 'ffn2_w_in', 'ffn2_w_out', 'mix_w_in', 'mix_w_out', 'lru_conv_w', 'lru_conv_b', 'lru_wa', 'lru_ba', 'lru_wx', 'lru_bx', 'lru_lambda', 'lru_norm_g', 'sc_conv_w', 'sc_norm_g', 'rwkv_mu', 'rwkv_w0', 'rwkv_w2', 'rwkv_a0', 'rwkv_a2', 'rwkv_g2', 'rwkv_k_k', 'rwkv_k_a', 'rwkv_r_k', 'rwkv_lnx_w', 'rwkv_lnx_b', 'loss_target', 'm_meta_tokens', 'm_norm_g', 'm_ffn1_w_in', 'm_ffn1_w_out', 'm_ffn2_w_in', 'm_ffn2_w_out', 'm_mix_w_in', 'm_mix_w_out', 'm_lru_conv_w', 'm_lru_conv_b', 'm_lru_wa', 'm_lru_ba', 'm_lru_wx', 'm_lru_bx', 'm_lru_lambda', 'm_lru_norm_g', 'm_sc_conv_w', 'm_sc_norm_g', 'm_rwkv_mu', 'm_rwkv_w0', 'm_rwkv_w2', 'm_rwkv_a0', 'm_rwkv_a2', 'm_rwkv_g2', 'm_rwkv_k_k', 'm_rwkv_k_a', 'm_rwkv_r_k', 'm_rwkv_lnx_w', 'm_rwkv_lnx_b', 'v_meta_tokens', 'v_norm_g', 'v_ffn1_w_in', 'v_ffn1_w_out', 'v_ffn2_w_in', 'v_ffn2_w_out', 'v_mix_w_in', 'v_mix_w_out', 'v_lru_conv_w', 'v_lru_conv_b', 'v_lru_wa', 'v_lru_ba', 'v_lru_wx', 'v_lru_bx', 'v_lru_lambda', 'v_lru_norm_g', 'v_sc_conv_w', 'v_sc_norm_g', 'v_rwkv_mu', 'v_rwkv_w0', 'v_rwkv_w2', 'v_rwkv_a0', 'v_rwkv_a2', 'v_rwkv_g2', 'v_rwkv_k_k', 'v_rwkv_k_a', 'v_rwkv_r_k', 'v_rwkv_lnx_w', 'v_rwkv_lnx_b']
TWIN_OUTPUTS = ['loss', 'grad_x', 'grad_meta_tokens', 'grad_norm_g', 'grad_ffn1_w_in', 'grad_ffn1_w_out', 'grad_ffn2_w_in', 'grad_ffn2_w_out', 'grad_mix_w_in', 'grad_mix_w_out', 'grad_lru_conv_w', 'grad_lru_conv_b', 'grad_lru_wa', 'grad_lru_ba', 'grad_lru_wx', 'grad_lru_bx', 'grad_lru_lambda', 'grad_lru_norm_g', 'grad_sc_conv_w', 'grad_sc_norm_g', 'grad_rwkv_mu', 'grad_rwkv_w0', 'grad_rwkv_w2', 'grad_rwkv_a0', 'grad_rwkv_a2', 'grad_rwkv_g2', 'grad_rwkv_k_k', 'grad_rwkv_k_a', 'grad_rwkv_r_k', 'grad_rwkv_lnx_w', 'grad_rwkv_lnx_b', 'delta_meta_tokens', 'delta_norm_g', 'delta_ffn1_w_in', 'delta_ffn1_w_out', 'delta_ffn2_w_in', 'delta_ffn2_w_out', 'delta_mix_w_in', 'delta_mix_w_out', 'delta_lru_conv_w', 'delta_lru_conv_b', 'delta_lru_wa', 'delta_lru_ba', 'delta_lru_wx', 'delta_lru_bx', 'delta_lru_lambda', 'delta_lru_norm_g', 'delta_sc_conv_w', 'delta_sc_norm_g', 'delta_rwkv_mu', 'delta_rwkv_w0', 'delta_rwkv_w2', 'delta_rwkv_a0', 'delta_rwkv_a2', 'delta_rwkv_g2', 'delta_rwkv_k_k', 'delta_rwkv_k_a', 'delta_rwkv_r_k', 'delta_rwkv_lnx_w', 'delta_rwkv_lnx_b', 'new_m_meta_tokens', 'new_m_norm_g', 'new_m_ffn1_w_in', 'new_m_ffn1_w_out', 'new_m_ffn2_w_in', 'new_m_ffn2_w_out', 'new_m_mix_w_in', 'new_m_mix_w_out', 'new_m_lru_conv_w', 'new_m_lru_conv_b', 'new_m_lru_wa', 'new_m_lru_ba', 'new_m_lru_wx', 'new_m_lru_bx', 'new_m_lru_lambda', 'new_m_lru_norm_g', 'new_m_sc_conv_w', 'new_m_sc_norm_g', 'new_m_rwkv_mu', 'new_m_rwkv_w0', 'new_m_rwkv_w2', 'new_m_rwkv_a0', 'new_m_rwkv_a2', 'new_m_rwkv_g2', 'new_m_rwkv_k_k', 'new_m_rwkv_k_a', 'new_m_rwkv_r_k', 'new_m_rwkv_lnx_w', 'new_m_rwkv_lnx_b', 'new_v_meta_tokens', 'new_v_norm_g', 'new_v_ffn1_w_in', 'new_v_ffn1_w_out', 'new_v_ffn2_w_in', 'new_v_ffn2_w_out', 'new_v_mix_w_in', 'new_v_mix_w_out', 'new_v_lru_conv_w', 'new_v_lru_conv_b', 'new_v_lru_wa', 'new_v_lru_ba', 'new_v_lru_wx', 'new_v_lru_bx', 'new_v_lru_lambda', 'new_v_lru_norm_g', 'new_v_sc_conv_w', 'new_v_sc_norm_g', 'new_v_rwkv_mu', 'new_v_rwkv_w0', 'new_v_rwkv_w2', 'new_v_rwkv_a0', 'new_v_rwkv_a2', 'new_v_rwkv_g2', 'new_v_rwkv_k_k', 'new_v_rwkv_k_a', 'new_v_rwkv_r_k', 'new_v_rwkv_lnx_w', 'new_v_rwkv_lnx_b']
TWIN_LEAF_KINDS = {'loss': 'loss', 'grad_x': 'grad_x', 'grad_meta_tokens': 'grad_w', 'grad_norm_g': 'grad_w', 'grad_ffn1_w_in': 'grad_w', 'grad_ffn1_w_out': 'grad_w', 'grad_ffn2_w_in': 'grad_w', 'grad_ffn2_w_out': 'grad_w', 'grad_mix_w_in': 'grad_w', 'grad_mix_w_out': 'grad_w', 'grad_lru_conv_w': 'grad_w', 'grad_lru_conv_b': 'grad_w', 'grad_lru_wa': 'grad_w', 'grad_lru_ba': 'grad_w', 'grad_lru_wx': 'grad_w', 'grad_lru_bx': 'grad_w', 'grad_lru_lambda': 'grad_w', 'grad_lru_norm_g': 'grad_w', 'grad_sc_conv_w': 'grad_w', 'grad_sc_norm_g': 'grad_w', 'grad_rwkv_mu': 'grad_w', 'grad_rwkv_w0': 'grad_w', 'grad_rwkv_w2': 'grad_w', 'grad_rwkv_a0': 'grad_w', 'grad_rwkv_a2': 'grad_w', 'grad_rwkv_g2': 'grad_w', 'grad_rwkv_k_k': 'grad_w', 'grad_rwkv_k_a': 'grad_w', 'grad_rwkv_r_k': 'grad_w', 'grad_rwkv_lnx_w': 'grad_w', 'grad_rwkv_lnx_b': 'grad_w', 'delta_meta_tokens': 'delta_w', 'delta_norm_g': 'delta_w', 'delta_ffn1_w_in': 'delta_w', 'delta_ffn1_w_out': 'delta_w', 'delta_ffn2_w_in': 'delta_w', 'delta_ffn2_w_out': 'delta_w', 'delta_mix_w_in': 'delta_w', 'delta_mix_w_out': 'delta_w', 'delta_lru_conv_w': 'delta_w', 'delta_lru_conv_b': 'delta_w', 'delta_lru_wa': 'delta_w', 'delta_lru_ba': 'delta_w', 'delta_lru_wx': 'delta_w', 'delta_lru_bx': 'delta_w', 'delta_lru_lambda': 'delta_w', 'delta_lru_norm_g': 'delta_w', 'delta_sc_conv_w': 'delta_w', 'delta_sc_norm_g': 'delta_w', 'delta_rwkv_mu': 'delta_w', 'delta_rwkv_w0': 'delta_w', 'delta_rwkv_w2': 'delta_w', 'delta_rwkv_a0': 'delta_w', 'delta_rwkv_a2': 'delta_w', 'delta_rwkv_g2': 'delta_w', 'delta_rwkv_k_k': 'delta_w', 'delta_rwkv_k_a': 'delta_w', 'delta_rwkv_r_k': 'delta_w', 'delta_rwkv_lnx_w': 'delta_w', 'delta_rwkv_lnx_b': 'delta_w', 'new_m_meta_tokens': 'new_m', 'new_m_norm_g': 'new_m', 'new_m_ffn1_w_in': 'new_m', 'new_m_ffn1_w_out': 'new_m', 'new_m_ffn2_w_in': 'new_m', 'new_m_ffn2_w_out': 'new_m', 'new_m_mix_w_in': 'new_m', 'new_m_mix_w_out': 'new_m', 'new_m_lru_conv_w': 'new_m', 'new_m_lru_conv_b': 'new_m', 'new_m_lru_wa': 'new_m', 'new_m_lru_ba': 'new_m', 'new_m_lru_wx': 'new_m', 'new_m_lru_bx': 'new_m', 'new_m_lru_lambda': 'new_m', 'new_m_lru_norm_g': 'new_m', 'new_m_sc_conv_w': 'new_m', 'new_m_sc_norm_g': 'new_m', 'new_m_rwkv_mu': 'new_m', 'new_m_rwkv_w0': 'new_m', 'new_m_rwkv_w2': 'new_m', 'new_m_rwkv_a0': 'new_m', 'new_m_rwkv_a2': 'new_m', 'new_m_rwkv_g2': 'new_m', 'new_m_rwkv_k_k': 'new_m', 'new_m_rwkv_k_a': 'new_m', 'new_m_rwkv_r_k': 'new_m', 'new_m_rwkv_lnx_w': 'new_m', 'new_m_rwkv_lnx_b': 'new_m', 'new_v_meta_tokens': 'new_v', 'new_v_norm_g': 'new_v', 'new_v_ffn1_w_in': 'new_v', 'new_v_ffn1_w_out': 'new_v', 'new_v_ffn2_w_in': 'new_v', 'new_v_ffn2_w_out': 'new_v', 'new_v_mix_w_in': 'new_v', 'new_v_mix_w_out': 'new_v', 'new_v_lru_conv_w': 'new_v', 'new_v_lru_conv_b': 'new_v', 'new_v_lru_wa': 'new_v', 'new_v_lru_ba': 'new_v', 'new_v_lru_wx': 'new_v', 'new_v_lru_bx': 'new_v', 'new_v_lru_lambda': 'new_v', 'new_v_lru_norm_g': 'new_v', 'new_v_sc_conv_w': 'new_v', 'new_v_sc_norm_g': 'new_v', 'new_v_rwkv_mu': 'new_v', 'new_v_rwkv_w0': 'new_v', 'new_v_rwkv_w2': 'new_v', 'new_v_rwkv_a0': 'new_v', 'new_v_rwkv_a2': 'new_v', 'new_v_rwkv_g2': 'new_v', 'new_v_rwkv_k_k': 'new_v', 'new_v_rwkv_k_a': 'new_v', 'new_v_rwkv_r_k': 'new_v', 'new_v_rwkv_lnx_w': 'new_v', 'new_v_rwkv_lnx_b': 'new_v'}


def _forward(args):
    return _fwd_reference(*[args[k] for k in FWD_PARAMS])


def _output_shape():
    out = _jax.eval_shape(lambda: _forward(_fwd_setup_inputs(0)))
    return out.shape, out.dtype

N_MICROBATCH = 1
ADAM_LR = 0.001
ADAM_B1 = 0.9
ADAM_B2 = 0.999
ADAM_EPS = 1e-08
ADAM_WD = 0.01
ADAM_STEP = 10
PER_EXAMPLE_BATCH_AXIS = {'x': 0, 'loss_target': 0}
SHARED_INPUTS = []
_WEIGHT_DTYPES = {'meta_tokens': _jnp.float32, 'norm_g': _jnp.float32, 'ffn1_w_in': _jnp.float32, 'ffn1_w_out': _jnp.float32, 'ffn2_w_in': _jnp.float32, 'ffn2_w_out': _jnp.float32, 'mix_w_in': _jnp.float32, 'mix_w_out': _jnp.float32, 'lru_conv_w': _jnp.float32, 'lru_conv_b': _jnp.float32, 'lru_wa': _jnp.float32, 'lru_ba': _jnp.float32, 'lru_wx': _jnp.float32, 'lru_bx': _jnp.float32, 'lru_lambda': _jnp.float32, 'lru_norm_g': _jnp.float32, 'sc_conv_w': _jnp.float32, 'sc_norm_g': _jnp.float32, 'rwkv_mu': _jnp.float32, 'rwkv_w0': _jnp.float32, 'rwkv_w2': _jnp.float32, 'rwkv_a0': _jnp.float32, 'rwkv_a2': _jnp.float32, 'rwkv_g2': _jnp.float32, 'rwkv_k_k': _jnp.float32, 'rwkv_k_a': _jnp.float32, 'rwkv_r_k': _jnp.float32, 'rwkv_lnx_w': _jnp.float32, 'rwkv_lnx_b': _jnp.float32}
MOMENT_SCALE = {'meta_tokens': 4.782214e-02, 'norm_g': 6.832348e+00, 'ffn1_w_in': 2.183899e-01, 'ffn1_w_out': 3.645990e-01, 'ffn2_w_in': 1.331246e-01, 'ffn2_w_out': 2.414886e-01, 'mix_w_in': 4.297174e-01, 'mix_w_out': 6.055311e-01, 'lru_conv_w': 9.155132e-01, 'lru_conv_b': 1.631705e+01, 'lru_wa': 3.947545e-01, 'lru_ba': 2.655914e-01, 'lru_wx': 7.362347e-01, 'lru_bx': 2.478026e-01, 'lru_lambda': 4.214756e-01, 'lru_norm_g': 1.108256e+00, 'sc_conv_w': 5.244736e-01, 'sc_norm_g': 5.099305e-01, 'rwkv_mu': 4.757600e-01, 'rwkv_w0': 1.236660e-01, 'rwkv_w2': 2.313888e-02, 'rwkv_a0': 1.087895e-01, 'rwkv_a2': 9.793079e-02, 'rwkv_g2': 2.698698e-01, 'rwkv_k_k': 2.431213e-01, 'rwkv_k_a': 3.245980e-01, 'rwkv_r_k': 6.407343e-01, 'rwkv_lnx_w': 2.867397e-01, 'rwkv_lnx_b': 2.025096e+00}


def _to_microbatches(a, axis):
    t = _jnp.moveaxis(a, axis, 0)
    t = t.reshape((N_MICROBATCH, t.shape[0] // N_MICROBATCH) + t.shape[1:])
    return _jnp.moveaxis(t, 1, axis + 1)


def setup_inputs(seed: int = 0) -> dict:
    inp = _fwd_setup_inputs(seed)
    key = _jax.random.fold_in(_jax.random.key(seed), 7919)
    shape, _ = _output_shape()
    out = dict(inp)
    out["loss_target"] = _jax.random.normal(_jax.random.fold_in(key, 0), shape, _jnp.float32)
    for i, name in enumerate(TWIN_WEIGHTS):
        w = inp[name].astype(_jnp.float32)
        if MOMENT_SCALE is None:
            s = _jnp.sqrt(_jnp.mean(_jnp.square(w)) + 1e-30)
        else:
            s = MOMENT_SCALE[name]
        km, kv = _jax.random.split(_jax.random.fold_in(key, i + 1))
        out[name] = w
        out["m_" + name] = s * _jax.random.normal(km, w.shape, _jnp.float32)
        out["v_" + name] = (s * s) * _jax.random.uniform(kv, w.shape, _jnp.float32, 0.5, 1.5)
    if N_MICROBATCH > 1:
        for name, axis in PER_EXAMPLE_BATCH_AXIS.items():
            out[name] = _to_microbatches(out[name], axis)
    return {'x': out['x'], 'meta_tokens': out['meta_tokens'], 'norm_g': out['norm_g'], 'ffn1_w_in': out['ffn1_w_in'], 'ffn1_w_out': out['ffn1_w_out'], 'ffn2_w_in': out['ffn2_w_in'], 'ffn2_w_out': out['ffn2_w_out'], 'mix_w_in': out['mix_w_in'], 'mix_w_out': out['mix_w_out'], 'lru_conv_w': out['lru_conv_w'], 'lru_conv_b': out['lru_conv_b'], 'lru_wa': out['lru_wa'], 'lru_ba': out['lru_ba'], 'lru_wx': out['lru_wx'], 'lru_bx': out['lru_bx'], 'lru_lambda': out['lru_lambda'], 'lru_norm_g': out['lru_norm_g'], 'sc_conv_w': out['sc_conv_w'], 'sc_norm_g': out['sc_norm_g'], 'rwkv_mu': out['rwkv_mu'], 'rwkv_w0': out['rwkv_w0'], 'rwkv_w2': out['rwkv_w2'], 'rwkv_a0': out['rwkv_a0'], 'rwkv_a2': out['rwkv_a2'], 'rwkv_g2': out['rwkv_g2'], 'rwkv_k_k': out['rwkv_k_k'], 'rwkv_k_a': out['rwkv_k_a'], 'rwkv_r_k': out['rwkv_r_k'], 'rwkv_lnx_w': out['rwkv_lnx_w'], 'rwkv_lnx_b': out['rwkv_lnx_b'], 'loss_target': out['loss_target'], 'm_meta_tokens': out['m_meta_tokens'], 'm_norm_g': out['m_norm_g'], 'm_ffn1_w_in': out['m_ffn1_w_in'], 'm_ffn1_w_out': out['m_ffn1_w_out'], 'm_ffn2_w_in': out['m_ffn2_w_in'], 'm_ffn2_w_out': out['m_ffn2_w_out'], 'm_mix_w_in': out['m_mix_w_in'], 'm_mix_w_out': out['m_mix_w_out'], 'm_lru_conv_w': out['m_lru_conv_w'], 'm_lru_conv_b': out['m_lru_conv_b'], 'm_lru_wa': out['m_lru_wa'], 'm_lru_ba': out['m_lru_ba'], 'm_lru_wx': out['m_lru_wx'], 'm_lru_bx': out['m_lru_bx'], 'm_lru_lambda': out['m_lru_lambda'], 'm_lru_norm_g': out['m_lru_norm_g'], 'm_sc_conv_w': out['m_sc_conv_w'], 'm_sc_norm_g': out['m_sc_norm_g'], 'm_rwkv_mu': out['m_rwkv_mu'], 'm_rwkv_w0': out['m_rwkv_w0'], 'm_rwkv_w2': out['m_rwkv_w2'], 'm_rwkv_a0': out['m_rwkv_a0'], 'm_rwkv_a2': out['m_rwkv_a2'], 'm_rwkv_g2': out['m_rwkv_g2'], 'm_rwkv_k_k': out['m_rwkv_k_k'], 'm_rwkv_k_a': out['m_rwkv_k_a'], 'm_rwkv_r_k': out['m_rwkv_r_k'], 'm_rwkv_lnx_w': out['m_rwkv_lnx_w'], 'm_rwkv_lnx_b': out['m_rwkv_lnx_b'], 'v_meta_tokens': out['v_meta_tokens'], 'v_norm_g': out['v_norm_g'], 'v_ffn1_w_in': out['v_ffn1_w_in'], 'v_ffn1_w_out': out['v_ffn1_w_out'], 'v_ffn2_w_in': out['v_ffn2_w_in'], 'v_ffn2_w_out': out['v_ffn2_w_out'], 'v_mix_w_in': out['v_mix_w_in'], 'v_mix_w_out': out['v_mix_w_out'], 'v_lru_conv_w': out['v_lru_conv_w'], 'v_lru_conv_b': out['v_lru_conv_b'], 'v_lru_wa': out['v_lru_wa'], 'v_lru_ba': out['v_lru_ba'], 'v_lru_wx': out['v_lru_wx'], 'v_lru_bx': out['v_lru_bx'], 'v_lru_lambda': out['v_lru_lambda'], 'v_lru_norm_g': out['v_lru_norm_g'], 'v_sc_conv_w': out['v_sc_conv_w'], 'v_sc_norm_g': out['v_sc_norm_g'], 'v_rwkv_mu': out['v_rwkv_mu'], 'v_rwkv_w0': out['v_rwkv_w0'], 'v_rwkv_w2': out['v_rwkv_w2'], 'v_rwkv_a0': out['v_rwkv_a0'], 'v_rwkv_a2': out['v_rwkv_a2'], 'v_rwkv_g2': out['v_rwkv_g2'], 'v_rwkv_k_k': out['v_rwkv_k_k'], 'v_rwkv_k_a': out['v_rwkv_k_a'], 'v_rwkv_r_k': out['v_rwkv_r_k'], 'v_rwkv_lnx_w': out['v_rwkv_lnx_w'], 'v_rwkv_lnx_b': out['v_rwkv_lnx_b']}


def _loss(weights, diff, rest, loss_target):
    with _jax.named_scope("forward"):
        args = {**rest, TWIN_DIFF_INPUT: diff, **{k: w.astype(_WEIGHT_DTYPES[k]) for k, w in weights.items()}}
        y = _forward(args)
    with _jax.named_scope("loss_head"):
        err = _jnp.square(y.astype(_jnp.float32) - loss_target)
        return 0.5 * _jnp.sum(_jnp.mean(err, axis=-1)) if err.ndim else 0.5 * err


def _adamw(w, g, m, v):
    m = ADAM_B1 * m + (1.0 - ADAM_B1) * g
    v = ADAM_B2 * v + (1.0 - ADAM_B2) * _jnp.square(g)
    m_hat = m / (1.0 - ADAM_B1 ** ADAM_STEP)
    v_hat = v / (1.0 - ADAM_B2 ** ADAM_STEP)
    delta = -ADAM_LR * (m_hat / (_jnp.sqrt(v_hat) + ADAM_EPS) + ADAM_WD * w)
    return delta, m, v


def reference(x, meta_tokens, norm_g, ffn1_w_in, ffn1_w_out, ffn2_w_in, ffn2_w_out, mix_w_in, mix_w_out, lru_conv_w, lru_conv_b, lru_wa, lru_ba, lru_wx, lru_bx, lru_lambda, lru_norm_g, sc_conv_w, sc_norm_g, rwkv_mu, rwkv_w0, rwkv_w2, rwkv_a0, rwkv_a2, rwkv_g2, rwkv_k_k, rwkv_k_a, rwkv_r_k, rwkv_lnx_w, rwkv_lnx_b, loss_target, m_meta_tokens, m_norm_g, m_ffn1_w_in, m_ffn1_w_out, m_ffn2_w_in, m_ffn2_w_out, m_mix_w_in, m_mix_w_out, m_lru_conv_w, m_lru_conv_b, m_lru_wa, m_lru_ba, m_lru_wx, m_lru_bx, m_lru_lambda, m_lru_norm_g, m_sc_conv_w, m_sc_norm_g, m_rwkv_mu, m_rwkv_w0, m_rwkv_w2, m_rwkv_a0, m_rwkv_a2, m_rwkv_g2, m_rwkv_k_k, m_rwkv_k_a, m_rwkv_r_k, m_rwkv_lnx_w, m_rwkv_lnx_b, v_meta_tokens, v_norm_g, v_ffn1_w_in, v_ffn1_w_out, v_ffn2_w_in, v_ffn2_w_out, v_mix_w_in, v_mix_w_out, v_lru_conv_w, v_lru_conv_b, v_lru_wa, v_lru_ba, v_lru_wx, v_lru_bx, v_lru_lambda, v_lru_norm_g, v_sc_conv_w, v_sc_norm_g, v_rwkv_mu, v_rwkv_w0, v_rwkv_w2, v_rwkv_a0, v_rwkv_a2, v_rwkv_g2, v_rwkv_k_k, v_rwkv_k_a, v_rwkv_r_k, v_rwkv_lnx_w, v_rwkv_lnx_b):
    given = dict(x=x, meta_tokens=meta_tokens, norm_g=norm_g, ffn1_w_in=ffn1_w_in, ffn1_w_out=ffn1_w_out, ffn2_w_in=ffn2_w_in, ffn2_w_out=ffn2_w_out, mix_w_in=mix_w_in, mix_w_out=mix_w_out, lru_conv_w=lru_conv_w, lru_conv_b=lru_conv_b, lru_wa=lru_wa, lru_ba=lru_ba, lru_wx=lru_wx, lru_bx=lru_bx, lru_lambda=lru_lambda, lru_norm_g=lru_norm_g, sc_conv_w=sc_conv_w, sc_norm_g=sc_norm_g, rwkv_mu=rwkv_mu, rwkv_w0=rwkv_w0, rwkv_w2=rwkv_w2, rwkv_a0=rwkv_a0, rwkv_a2=rwkv_a2, rwkv_g2=rwkv_g2, rwkv_k_k=rwkv_k_k, rwkv_k_a=rwkv_k_a, rwkv_r_k=rwkv_r_k, rwkv_lnx_w=rwkv_lnx_w, rwkv_lnx_b=rwkv_lnx_b, loss_target=loss_target, m_meta_tokens=m_meta_tokens, m_norm_g=m_norm_g, m_ffn1_w_in=m_ffn1_w_in, m_ffn1_w_out=m_ffn1_w_out, m_ffn2_w_in=m_ffn2_w_in, m_ffn2_w_out=m_ffn2_w_out, m_mix_w_in=m_mix_w_in, m_mix_w_out=m_mix_w_out, m_lru_conv_w=m_lru_conv_w, m_lru_conv_b=m_lru_conv_b, m_lru_wa=m_lru_wa, m_lru_ba=m_lru_ba, m_lru_wx=m_lru_wx, m_lru_bx=m_lru_bx, m_lru_lambda=m_lru_lambda, m_lru_norm_g=m_lru_norm_g, m_sc_conv_w=m_sc_conv_w, m_sc_norm_g=m_sc_norm_g, m_rwkv_mu=m_rwkv_mu, m_rwkv_w0=m_rwkv_w0, m_rwkv_w2=m_rwkv_w2, m_rwkv_a0=m_rwkv_a0, m_rwkv_a2=m_rwkv_a2, m_rwkv_g2=m_rwkv_g2, m_rwkv_k_k=m_rwkv_k_k, m_rwkv_k_a=m_rwkv_k_a, m_rwkv_r_k=m_rwkv_r_k, m_rwkv_lnx_w=m_rwkv_lnx_w, m_rwkv_lnx_b=m_rwkv_lnx_b, v_meta_tokens=v_meta_tokens, v_norm_g=v_norm_g, v_ffn1_w_in=v_ffn1_w_in, v_ffn1_w_out=v_ffn1_w_out, v_ffn2_w_in=v_ffn2_w_in, v_ffn2_w_out=v_ffn2_w_out, v_mix_w_in=v_mix_w_in, v_mix_w_out=v_mix_w_out, v_lru_conv_w=v_lru_conv_w, v_lru_conv_b=v_lru_conv_b, v_lru_wa=v_lru_wa, v_lru_ba=v_lru_ba, v_lru_wx=v_lru_wx, v_lru_bx=v_lru_bx, v_lru_lambda=v_lru_lambda, v_lru_norm_g=v_lru_norm_g, v_sc_conv_w=v_sc_conv_w, v_sc_norm_g=v_sc_norm_g, v_rwkv_mu=v_rwkv_mu, v_rwkv_w0=v_rwkv_w0, v_rwkv_w2=v_rwkv_w2, v_rwkv_a0=v_rwkv_a0, v_rwkv_a2=v_rwkv_a2, v_rwkv_g2=v_rwkv_g2, v_rwkv_k_k=v_rwkv_k_k, v_rwkv_k_a=v_rwkv_k_a, v_rwkv_r_k=v_rwkv_r_k, v_rwkv_lnx_w=v_rwkv_lnx_w, v_rwkv_lnx_b=v_rwkv_lnx_b)
    weights = {n: given[n] for n in TWIN_WEIGHTS}
    shared = {n: given[n] for n in SHARED_INPUTS}
    per_example = {n: given[n] for n in ['x']}
    grad_fn = _jax.value_and_grad(_loss, argnums=(0, 1))

    def one_microbatch(ex, loss_target):
        ex = dict(ex)
        diff = ex.pop(TWIN_DIFF_INPUT)
        return grad_fn(weights, diff, {**shared, **ex}, loss_target)

    if N_MICROBATCH == 1:
        loss, (grad_w, grad_x) = one_microbatch(per_example, given["loss_target"])
    else:
        def body(carry, xs):
            loss_sum, grad_sum = carry
            l_k, (gw_k, gx_k) = one_microbatch(xs[0], xs[1])
            with _jax.named_scope("update"):
                return (loss_sum + l_k, _jax.tree.map(_jnp.add, grad_sum, gw_k)), gx_k

        init = (_jnp.zeros((), _jnp.float32), _jax.tree.map(_jnp.zeros_like, weights))
        (loss, grad_w), grad_x = _jax.lax.scan(body, init, (per_example, given["loss_target"]))
    with _jax.named_scope("update"):
        delta_w, new_m, new_v = {}, {}, {}
        for n in TWIN_WEIGHTS:
            delta_w[n], new_m[n], new_v[n] = _adamw(weights[n], grad_w[n], given["m_" + n], given["v_" + n])
    return (loss, grad_x, *[grad_w[n] for n in TWIN_WEIGHTS], *[delta_w[n] for n in TWIN_WEIGHTS],
            *[new_m[n] for n in TWIN_WEIGHTS], *[new_v[n] for n in TWIN_WEIGHTS])
```

```python
import functools

import jax
import jax.numpy as jnp
from jax import lax
from jax.experimental import pallas as pl
from jax.experimental.pallas import tpu as pltpu

F32 = jnp.float32
BF16 = jnp.bfloat16
HIGHEST = lax.Precision.HIGHEST

N_DEV = 8
MESH_AXES = ("x", "y", "c")
N_META = 16
D_MODEL = 1024
LRU_W = 256
SC_W = 256
RW_W = 512
HEAD = 64
LANES = 128
CHUNK = 64
RW_IN = 1664
N_IN = 2944
FFN_BLK = 704
RMS_EPS = 1e-6
LNX_EPS = 64e-5
LRU_C = 8.0
ADAM_LR, ADAM_B1, ADAM_B2, ADAM_EPS, ADAM_WD, ADAM_STEP = 0.001, 0.9, 0.999, 1e-08, 0.01, 10

WEIGHTS = ['meta_tokens', 'norm_g', 'ffn1_w_in', 'ffn1_w_out', 'ffn2_w_in', 'ffn2_w_out', 'mix_w_in', 'mix_w_out',
           'lru_conv_w', 'lru_conv_b', 'lru_wa', 'lru_ba', 'lru_wx', 'lru_bx', 'lru_lambda', 'lru_norm_g',
           'sc_conv_w', 'sc_norm_g', 'rwkv_mu', 'rwkv_w0', 'rwkv_w2', 'rwkv_a0', 'rwkv_a2', 'rwkv_g2', 'rwkv_k_k',
           'rwkv_k_a', 'rwkv_r_k', 'rwkv_lnx_w', 'rwkv_lnx_b']
BIG = ['ffn1_w_in', 'ffn1_w_out', 'ffn2_w_in', 'ffn2_w_out', 'mix_w_in', 'mix_w_out']
SMALL_SHARDED = {'meta_tokens': 1, 'norm_g': 2, 'lru_conv_w': 2, 'sc_conv_w': 2, 'rwkv_w2': 2, 'rwkv_a2': 2, 'rwkv_g2': 2}
SMALL = [n for n in WEIGHTS if n not in BIG]


def _pick(n, cands):
    for c in cands:
        if n % c == 0:
            return c
    raise ValueError(f"no tile for {n}")


def _rowwise(fn, rows, params, row_outs, acc_outs, *, tm, name):
    nr, npar, nro, nao = len(rows), len(params), len(row_outs), len(acc_outs)
    n_rows = rows[0].shape[-2]
    assert n_rows % tm == 0, (name, n_rows, tm)

    def body(*refs):
        vals = [r[...] for r in refs[:nr + npar]]
        outs = fn(*vals)
        if not isinstance(outs, (tuple, list)):
            outs = (outs,)
        assert len(outs) == nro + nao, (name, len(outs))
        for o_ref, o in zip(refs[nr + npar:nr + npar + nro], outs[:nro]):
            o_ref[...] = o.astype(o_ref.dtype)
        step = pl.program_id(0)
        for a_ref, a in zip(refs[nr + npar + nro:], outs[nro:]):
            @pl.when(step == 0)
            def _(a_ref=a_ref, a=a):
                a_ref[...] = a.astype(F32)

            @pl.when(step > 0)
            def _(a_ref=a_ref, a=a):
                a_ref[...] += a.astype(F32)

    def row_spec(shape):
        if len(shape) == 2:
            return pl.BlockSpec((tm, shape[1]), lambda i: (i, 0))
        return pl.BlockSpec((shape[0], tm, shape[2]), lambda i: (0, i, 0))

    def full_spec(shape):
        nd = len(shape)
        return pl.BlockSpec(tuple(shape), lambda i, nd=nd: (0,) * nd)

    in_specs = [row_spec(r.shape) for r in rows] + [full_spec(p.shape) for p in params]
    out_shape = [jax.ShapeDtypeStruct((n_rows, w), dt) for (w, dt) in row_outs]
    out_shape += [jax.ShapeDtypeStruct(tuple(s), F32) for s in acc_outs]
    out_specs = [row_spec((n_rows, w)) for (w, _) in row_outs] + [full_spec(s) for s in acc_outs]
    res = pl.pallas_call(body, name=name, grid=(n_rows // tm,), in_specs=in_specs, out_specs=out_specs,
                         out_shape=out_shape)(*rows, *params)
    return tuple(res)


def _mm(a3, b3, *, trans_b, tm, tn, out_dtype, name):
    nj, m, kb = a3.shape
    n = b3.shape[1] if trans_b else b3.shape[2]
    dims = (((1,), (1,)), ((), ())) if trans_b else (((1,), (0,)), ((), ()))

    def body(a_ref, b_ref, o_ref, acc_ref):
        j = pl.program_id(2)

        @pl.when(j == 0)
        def _():
            acc_ref[...] = jnp.zeros_like(acc_ref)

        acc_ref[...] += lax.dot_general(a_ref[0], b_ref[0], dims, preferred_element_type=F32)

        @pl.when(j == nj - 1)
        def _():
            o_ref[...] = acc_ref[...].astype(o_ref.dtype)

    if trans_b:
        b_spec = pl.BlockSpec((1, tn, kb), lambda i, c, j: (j, c, 0))
    else:
        b_spec = pl.BlockSpec((1, kb, tn), lambda i, c, j: (j, 0, c))
    return pl.pallas_call(
        body, name=name, grid=(m // tm, n // tn, nj),
        in_specs=[pl.BlockSpec((1, tm, kb), lambda i, c, j: (j, i, 0)), b_spec],
        out_specs=pl.BlockSpec((tm, tn), lambda i, c, j: (i, c)),
        out_shape=jax.ShapeDtypeStruct((m, n), out_dtype),
        scratch_shapes=[pltpu.VMEM((tm, tn), F32)],
    )(a3, b3)


def _mm_tn(a3, b3, *, tk, name):
    ja, t, ka = a3.shape
    jb, _, n = b3.shape
    nj = max(ja, jb)

    def body(a_ref, b_ref, o_ref):
        o_ref[0] = lax.dot_general(a_ref[0], b_ref[0], (((0,), (0,)), ((), ())),
                                   preferred_element_type=F32).astype(o_ref.dtype)

    return pl.pallas_call(
        body, name=name, grid=(nj, ka // tk),
        in_specs=[pl.BlockSpec((1, t, tk), (lambda j, c: (j, 0, c)) if ja > 1 else (lambda j, c: (0, 0, c))),
                  pl.BlockSpec((1, t, n), (lambda j, c: (j, 0, 0)) if jb > 1 else (lambda j, c: (0, 0, 0)))],
        out_specs=pl.BlockSpec((1, tk, n), lambda j, c: (j, c, 0)),
        out_shape=jax.ShapeDtypeStruct((nj, ka, n), BF16),
    )(a3, b3)


def _ffn_in(a, w24, *, tm, name):
    t, d = a.shape
    nb, fb = w24.shape[1], w24.shape[3]

    def body(a_ref, w_ref, gu_ref, s_ref):
        x = a_ref[...]
        g = jnp.dot(x, w_ref[0, 0], preferred_element_type=F32)
        u = jnp.dot(x, w_ref[1, 0], preferred_element_type=F32)
        gu_ref[0, 0] = g.astype(BF16)
        gu_ref[1, 0] = u.astype(BF16)
        s_ref[0] = (g * jax.nn.sigmoid(g) * u).astype(BF16)

    return pl.pallas_call(
        body, name=name, grid=(nb, t // tm),
        in_specs=[pl.BlockSpec((tm, d), lambda j, i: (i, 0)), pl.BlockSpec((2, 1, d, fb), lambda j, i: (0, j, 0, 0))],
        out_specs=[pl.BlockSpec((2, 1, tm, fb), lambda j, i: (0, j, i, 0)), pl.BlockSpec((1, tm, fb), lambda j, i: (j, i, 0))],
        out_shape=[jax.ShapeDtypeStruct((2, nb, t, fb), BF16), jax.ShapeDtypeStruct((nb, t, fb), BF16)],
    )(a, w24)


def _ffn_dswiglu(df, wo4, gu, *, tm, name):
    t, d = df.shape
    nb, fb = wo4.shape[0], wo4.shape[1]

    def body(df_ref, wo_ref, gu_ref, dg_ref):
        ds = lax.dot_general(df_ref[...], wo_ref[0], (((1,), (1,)), ((), ())), preferred_element_type=F32)
        g = gu_ref[0, 0].astype(F32)
        u = gu_ref[1, 0].astype(F32)
        sig = jax.nn.sigmoid(g)
        dg_ref[0, 0] = (ds * u * sig * (1.0 + g * (1.0 - sig))).astype(BF16)
        dg_ref[1, 0] = (ds * g * sig).astype(BF16)

    return pl.pallas_call(
        body, name=name, grid=(nb, t // tm),
        in_specs=[pl.BlockSpec((tm, d), lambda j, i: (i, 0)), pl.BlockSpec((1, fb, d), lambda j, i: (j, 0, 0)),
                  pl.BlockSpec((2, 1, tm, fb), lambda j, i: (0, j, i, 0))],
        out_specs=pl.BlockSpec((2, 1, tm, fb), lambda j, i: (0, j, i, 0)),
        out_shape=jax.ShapeDtypeStruct((2, nb, t, fb), BF16),
    )(df, wo4, gu)


def _rms(x, g):
    return x * lax.rsqrt(jnp.mean(x * x, axis=-1, keepdims=True) + RMS_EPS) * g


def _rms_bwd(x, g, dy):
    rstd = lax.rsqrt(jnp.mean(x * x, axis=-1, keepdims=True) + RMS_EPS)
    xh = x * rstd
    dxh = dy * g
    dx = rstd * (dxh - xh * jnp.mean(dxh * xh, axis=-1, keepdims=True))
    return dx, jnp.sum(dy * xh, axis=0, keepdims=True)


def _seg_sum_impl(x, bd):
    parts = [jnp.dot(x[:, q * LANES:(q + 1) * LANES], bd, preferred_element_type=F32, precision=HIGHEST)
             for q in range(x.shape[1] // LANES)]
    return parts[0] if len(parts) == 1 else jnp.concatenate(parts, axis=1)


@jax.custom_vjp
def _seg_sum(x, bd):
    return _seg_sum_impl(x, bd)


def _seg_sum_fwd(x, bd):
    return _seg_sum_impl(x, bd), bd


def _seg_sum_bwd(bd, ct):
    return _seg_sum_impl(ct, bd), jnp.zeros_like(bd)


_seg_sum.defvjp(_seg_sum_fwd, _seg_sum_bwd)


def _group_rms(y, g, bd):
    return y * lax.rsqrt(_seg_sum(y * y, bd) * (1.0 / HEAD) + RMS_EPS) * g


def _expm1(x):
    return jnp.where(jnp.abs(x) < 1e-2, x * (1.0 + x * (0.5 + x * (1.0 / 6.0))), jnp.exp(x) - 1.0)


def _lru_pre(x0, x1, x2, x3, cw0, cw1, cw2, cw3, cb, wa, ba, wx, bx, lam):
    u = x3 * cw0 + x2 * cw1 + x1 * cw2 + x0 * cw3 + cb
    r = jax.nn.sigmoid(jnp.dot(u, wa, preferred_element_type=F32, precision=HIGHEST) + ba)
    i = jax.nn.sigmoid(jnp.dot(u, wx, preferred_element_type=F32, precision=HIGHEST) + bx)
    log_a = -LRU_C * r * jax.nn.softplus(-lam)
    return jnp.exp(log_a), jnp.sqrt(-_expm1(2.0 * log_a)) * (i * u)


def _lru_post(bd, gate, hs, ng):
    return _group_rms(jax.nn.gelu(gate) * hs, ng, bd)


def _sc_fwd(bd, b, c0, x0, c1, x1, c2, x2, w0, w1, w2, ng):
    return _group_rms(b * (w0 * (c2 * x2) + w1 * (c1 * x1) + w2 * (c0 * x0)), ng, bd)


def _rw_pre(bd, zr, zk, zv, zt, sr, sk, sv, st, mur, muk, muv, mut, w0, w2p, a0, a2p, g2p, k_k, k_a):
    r, k, v, tail = zr + (sr - zr) * mur, zk + (sk - zk) * muk, zv + (sv - zv) * muv, zt + (st - zt) * mut
    lane = lax.broadcasted_iota(jnp.int32, tail.shape, 1)
    act = jnp.where(lane < 32, jnp.tanh(tail), jnp.where(lane < 64, tail, jax.nn.sigmoid(tail)))
    dot = functools.partial(jnp.dot, preferred_element_type=F32, precision=HIGHEST)
    w_log = -jax.nn.softplus(-(w0 + dot(act, w2p))) - 0.5
    w = jnp.exp(-jnp.exp(w_log))
    a = jax.nn.sigmoid(a0 + dot(act, a2p))
    g = dot(act, g2p)
    kk = k * k_k
    k2 = k * (1.0 + (a - 1.0) * k_a)
    kkn = kk * lax.rsqrt(jnp.maximum(_seg_sum(kk * kk, bd), 1e-24))
    return r, w, k2, -kkn, kkn * a, v, g


def _rw_post(bd, y, r, k2, v, g, lnw, lnb, r_k):
    mean = _seg_sum(y, bd) * (1.0 / HEAD)
    yc = y - mean
    var = _seg_sum(yc * yc, bd) * (1.0 / HEAD)
    yn = yc * lax.rsqrt(var + LNX_EPS) * lnw + lnb
    return (yn + _seg_sum(r * k2 * r_k, bd) * v) * g


def _vjp_rows(fwd, n_static, n_in, n_ct):
    def fn(*args):
        static, prim, cts = args[:n_static], args[n_static:n_static + n_in], args[n_static + n_in:]
        assert len(cts) == n_ct
        _, vjp = jax.vjp(functools.partial(fwd, *static), *prim)
        return vjp(cts[0] if n_ct == 1 else tuple(cts))
    return fn


SUBLANES = 8


def _store_row(ref, i, cols, row):
    base = pl.multiple_of((i // SUBLANES) * SUBLANES, SUBLANES)
    sub = lax.broadcasted_iota(jnp.int32, (SUBLANES, row.shape[1]), 0)
    ref[pl.ds(base, SUBLANES), cols] = jnp.where(sub == i % SUBLANES, row, ref[pl.ds(base, SUBLANES), cols])


def _lru_scan(a, b, name):
    t, w = a.shape

    def body(a_ref, b_ref, h_ref):
        h_ref[...] = jnp.zeros_like(h_ref)

        def step(i, h):
            h = a_ref[pl.ds(i, 1), :] * h + b_ref[pl.ds(i, 1), :]
            _store_row(h_ref, i, slice(None), h)
            return h
        lax.fori_loop(0, t, step, jnp.zeros((1, w), F32), unroll=8)

    return pl.pallas_call(body, name=name, out_shape=jax.ShapeDtypeStruct((t, w), F32))(a, b)


def _lru_scan_bwd(a, hs, dhs, name):
    t, w = a.shape

    def body(a_ref, h_ref, dh_ref, da_ref, db_ref):
        da_ref[...] = jnp.zeros_like(da_ref)
        db_ref[...] = jnp.zeros_like(db_ref)

        def step(n, carry):
            i = t - 1 - n
            lam = dh_ref[pl.ds(i, 1), :] + carry
            _store_row(db_ref, i, slice(None), lam)
            hprev = h_ref[pl.ds(jnp.maximum(i - 1, 0), 1), :]
            _store_row(da_ref, i, slice(None), jnp.where(i > 0, lam * hprev, 0.0))
            return a_ref[pl.ds(i, 1), :] * lam
        lax.fori_loop(0, t, step, jnp.zeros((1, w), F32), unroll=8)

    return pl.pallas_call(body, name=name, out_shape=[jax.ShapeDtypeStruct((t, w), F32)] * 2)(a, hs, dhs)


N_PAIR = RW_W // LANES


def _bcast_cols(v):
    t = v.shape[0]
    x = v.reshape(t, N_PAIR, 2, HEAD).transpose(0, 1, 3, 2)
    return jnp.broadcast_to(x[..., None], (t, N_PAIR, HEAD, 2, HEAD)).reshape(t, N_PAIR, HEAD, LANES)


def _cols_to_rows(yt):
    nch = yt.shape[2] // LANES
    x = yt.reshape(N_PAIR, HEAD, nch, 2, CHUNK).transpose(2, 4, 0, 3, 1)
    return x.reshape(nch * CHUNK, RW_W)


def _rw_scan(r, w, k, c, b, vb, bd, name):
    t = r.shape[0]
    nch = t // CHUNK

    def body(r_ref, w_ref, k_ref, c_ref, b_ref, vb_ref, bd_ref, spre_ref, yt_ref, s_ref):
        @pl.when(pl.program_id(0) == 0)
        def _():
            s_ref[...] = jnp.zeros_like(s_ref)

        yt_ref[...] = jnp.zeros_like(yt_ref)
        bdv = bd_ref[...]
        lane = lax.broadcasted_iota(jnp.int32, (HEAD, LANES), 1) % CHUNK

        def step(i, st):
            rows = [ref[pl.ds(i, 1), :] for ref in (r_ref, w_ref, k_ref, c_ref, b_ref)]
            rr, ww, kk, cc, bb = [[x[:, p * LANES:(p + 1) * LANES] for p in range(N_PAIR)] for x in rows]
            sa = jnp.dot(jnp.concatenate([st[p] * cc[p] for p in range(N_PAIR)], axis=0), bdv,
                         preferred_element_type=F32, precision=HIGHEST)
            new = []
            for p in range(N_PAIR):
                spre_ref[i, p] = st[p]
                new.append(st[p] * ww[p] + sa[p * HEAD:(p + 1) * HEAD] * bb[p] + vb_ref[i, p] * kk[p])
            yb = jnp.dot(jnp.concatenate([new[p] * rr[p] for p in range(N_PAIR)], axis=0), bdv,
                         preferred_element_type=F32, precision=HIGHEST)
            for p in range(N_PAIR):
                yt_ref[p] = jnp.where(lane == i, yb[p * HEAD:(p + 1) * HEAD], yt_ref[p])
            return tuple(new)

        st = lax.fori_loop(0, CHUNK, step, tuple(s_ref[p] for p in range(N_PAIR)))
        for p in range(N_PAIR):
            s_ref[p] = st[p]

    row = pl.BlockSpec((CHUNK, RW_W), lambda i: (i, 0))
    big = pl.BlockSpec((CHUNK, N_PAIR, HEAD, LANES), lambda i: (i, 0, 0, 0))
    return pl.pallas_call(
        body, name=name, grid=(nch,),
        in_specs=[row] * 5 + [big, pl.BlockSpec((LANES, LANES), lambda i: (0, 0))],
        out_specs=[big, pl.BlockSpec((N_PAIR, HEAD, LANES), lambda i: (0, 0, i))],
        out_shape=[jax.ShapeDtypeStruct((t, N_PAIR, HEAD, LANES), F32), jax.ShapeDtypeStruct((N_PAIR, HEAD, nch * LANES), F32)],
        scratch_shapes=[pltpu.VMEM((N_PAIR, HEAD, LANES), F32)],
    )(r, w, k, c, b, vb, bd)


def _rw_scan_bwd(r, w, k, c, b, vb, dyb, spre, bd, name):
    t = r.shape[0]
    nch = t // CHUNK

    def body(r_ref, w_ref, k_ref, c_ref, b_ref, vb_ref, dyb_ref, spre_ref, bd_ref,
             dr_ref, dw_ref, dk_ref, dc_ref, db_ref, dvt_ref, g_ref, snext_ref):
        @pl.when(pl.program_id(0) == 0)
        def _():
            g_ref[...] = jnp.zeros_like(g_ref)
            snext_ref[...] = jnp.zeros_like(snext_ref)

        for ref in (dr_ref, dw_ref, dk_ref, dc_ref, db_ref, dvt_ref):
            ref[...] = jnp.zeros_like(ref)
        bdv = bd_ref[...]
        lane = lax.broadcasted_iota(jnp.int32, (HEAD, LANES), 1) % CHUNK

        def sum0(x):
            return jnp.sum(x, axis=0, keepdims=True)

        def step(n, gs):
            i = CHUNK - 1 - n
            rows = [ref[pl.ds(i, 1), :] for ref in (r_ref, w_ref, k_ref, c_ref, b_ref)]
            rr, ww, kk, cc, bb = [[x[:, p * LANES:(p + 1) * LANES] for p in range(N_PAIR)] for x in rows]
            sp = [spre_ref[i, p] for p in range(N_PAIR)]
            dy = [dyb_ref[i, p] for p in range(N_PAIR)]
            gs = [gs[p] + dy[p] * rr[p] for p in range(N_PAIR)]
            prods = ([sp[p] * cc[p] for p in range(N_PAIR)] + [gs[p] * bb[p] for p in range(N_PAIR)]
                     + [gs[p] * kk[p] for p in range(N_PAIR)])
            red = jnp.dot(jnp.concatenate(prods, axis=0), bdv, preferred_element_type=F32, precision=HIGHEST)
            new = []
            for p in range(N_PAIR):
                sa = red[p * HEAD:(p + 1) * HEAD]
                dsa = red[(N_PAIR + p) * HEAD:(N_PAIR + p + 1) * HEAD]
                dv = red[(2 * N_PAIR + p) * HEAD:(2 * N_PAIR + p + 1) * HEAD]
                cols = pl.ds(p * LANES, LANES)
                _store_row(dr_ref, i, cols, sum0(snext_ref[p] * dy[p]))
                _store_row(dw_ref, i, cols, sum0(gs[p] * sp[p]))
                _store_row(db_ref, i, cols, sum0(gs[p] * sa))
                _store_row(dk_ref, i, cols, sum0(gs[p] * vb_ref[i, p]))
                _store_row(dc_ref, i, cols, sum0(sp[p] * dsa))
                dvt_ref[p] = jnp.where(lane == i, dv, dvt_ref[p])
                snext_ref[p] = sp[p]
                new.append(gs[p] * ww[p] + dsa * cc[p])
            return tuple(new)

        gs = lax.fori_loop(0, CHUNK, step, tuple(g_ref[p] for p in range(N_PAIR)))
        for p in range(N_PAIR):
            g_ref[p] = gs[p]

    row = pl.BlockSpec((CHUNK, RW_W), lambda i: (nch - 1 - i, 0))
    big = pl.BlockSpec((CHUNK, N_PAIR, HEAD, LANES), lambda i: (nch - 1 - i, 0, 0, 0))
    return pl.pallas_call(
        body, name=name, grid=(nch,),
        in_specs=[row] * 5 + [big, big, big, pl.BlockSpec((LANES, LANES), lambda i: (0, 0))],
        out_specs=[row] * 5 + [pl.BlockSpec((N_PAIR, HEAD, LANES), lambda i: (0, 0, nch - 1 - i))],
        out_shape=[jax.ShapeDtypeStruct((t, RW_W), F32)] * 5 + [jax.ShapeDtypeStruct((N_PAIR, HEAD, nch * LANES), F32)],
        scratch_shapes=[pltpu.VMEM((N_PAIR, HEAD, LANES), F32), pltpu.VMEM((N_PAIR, HEAD, LANES), F32)],
    )(r, w, k, c, b, vb, dyb, spre, bd)


def _exchange(xs, *, gather, name):
    n = len(xs)

    def body(*refs):
        x_refs, o_refs = refs[:n], refs[n:2 * n]
        send_sems, recv_sems, local_sems = refs[2 * n:]
        mx, my, mc = lax.axis_index("x"), lax.axis_index("y"), lax.axis_index("c")
        me = 4 * mx + 2 * my + mc
        copies = []
        for k in range(n):
            src_me = x_refs[k] if gather else x_refs[k].at[me]
            loc = pltpu.make_async_copy(src_me, o_refs[k].at[me], local_sems.at[k])
            loc.start()
            copies.append(loc)
        sends, recvs = [], []
        for d in range(1, N_DEV):
            px, py, pc = mx ^ ((d >> 2) & 1), my ^ ((d >> 1) & 1), mc ^ (d & 1)
            peer = 4 * px + 2 * py + pc
            for k in range(n):
                cp = pltpu.make_async_remote_copy(
                    src_ref=x_refs[k] if gather else x_refs[k].at[peer], dst_ref=o_refs[k].at[me],
                    send_sem=send_sems.at[k, d - 1], recv_sem=recv_sems.at[k, d - 1],
                    device_id=(px, py, pc), device_id_type=pl.DeviceIdType.MESH)
                cp.start()
                sends.append(cp)
                recvs.append(pltpu.make_async_remote_copy(
                    src_ref=x_refs[k] if gather else x_refs[k].at[peer], dst_ref=o_refs[k].at[peer],
                    send_sem=send_sems.at[k, d - 1], recv_sem=recv_sems.at[k, d - 1],
                    device_id=(px, py, pc), device_id_type=pl.DeviceIdType.MESH))
        for cp in sends:
            cp.wait_send()
        for cp in recvs:
            cp.wait_recv()
        for loc in copies:
            loc.wait()

    out_shape = [jax.ShapeDtypeStruct(((N_DEV,) + x.shape) if gather else x.shape, x.dtype) for x in xs]
    any_spec = pl.BlockSpec(memory_space=pl.ANY)
    return pl.pallas_call(
        body, name=name, in_specs=[any_spec] * n, out_specs=[any_spec] * n, out_shape=out_shape,
        scratch_shapes=[pltpu.SemaphoreType.DMA((n, N_DEV - 1)), pltpu.SemaphoreType.DMA((n, N_DEV - 1)),
                        pltpu.SemaphoreType.DMA((n,))],
    )(*xs)


def _adamw_rows(g, w, m, v):
    m = ADAM_B1 * m + (1.0 - ADAM_B1) * g
    v = ADAM_B2 * v + (1.0 - ADAM_B2) * (g * g)
    m_hat = m / (1.0 - ADAM_B1 ** ADAM_STEP)
    v_hat = v / (1.0 - ADAM_B2 ** ADAM_STEP)
    return -ADAM_LR * (m_hat / (jnp.sqrt(v_hat) + ADAM_EPS) + ADAM_WD * w), m, v


def _sum_slots(parts):
    g = parts[0].astype(F32)
    for q in range(1, N_DEV):
        g = g + parts[q].astype(F32)
    return g


def _reduce_adamw(parts, w, m, v, name):
    rows, cols = w.shape

    def fn(parts, w, m, v):
        g = _sum_slots(parts)
        return (g,) + _adamw_rows(g, w, m, v)

    return _rowwise(fn, [parts, w, m, v], [], [(cols, F32)] * 4, [], tm=_pick(rows, (256, 128, 64, 32, 16, 8)), name=name)


def _shift(x, n):
    return jnp.pad(x, ((n, 0), (0, 0)))[:-n]


def _unshift(x, n):
    return jnp.pad(x, ((0, n), (0, 0)))[n:]


def _add_n(xs, *, tm, name):
    def fn(*vals):
        s = vals[0]
        for x in vals[1:]:
            s = s + x
        return s
    return _rowwise(fn, xs, [], [(xs[0].shape[1], F32)], [], tm=tm, name=name)[0]


def _norm_fwd(h, g, *, tm, name):
    return _rowwise(lambda x, gg: _rms(x, gg), [h], [g], [(h.shape[1], BF16)], [], tm=tm, name=name)[0]


def _res_norm_fwd(h, f, g, scale, *, tm, name):
    return _rowwise(lambda hh, ff, gg: hh + scale * _rms(ff, gg), [h, f], [g], [(h.shape[1], F32)], [], tm=tm, name=name)[0]


def _norm_bwd(x, g, dy, scale, res, out_dtype, *, tm, name):
    if res is None:
        def fn(xx, dd, gg):
            dx, dg = _rms_bwd(xx, gg, dd * scale)
            return dx, dg
        rows = [x, dy]
    else:
        def fn(xx, dd, rr, gg):
            dx, dg = _rms_bwd(xx, gg, dd * scale)
            return dx + rr, dg
        rows = [x, dy, res]
    return _rowwise(fn, rows, [g], [(x.shape[1], out_dtype)], [(1, x.shape[1])], tm=tm, name=name)


def _ffn_fwd(h, g_pre, g_post, w24, wo4, tiles, tag):
    tb, ts = tiles
    a = _norm_fwd(h, g_pre, tm=ts, name=f"{tag}_norm")
    gu, s4 = _ffn_in(a, w24, tm=tb, name=f"{tag}_in")
    f = _mm(s4, wo4, trans_b=False, tm=tb, tn=D_MODEL, out_dtype=F32, name=f"{tag}_out")
    h_new = _res_norm_fwd(h, f, g_post, 0.5, tm=ts, name=f"{tag}_res")
    return h_new, (h, a, gu, s4, f)


def _ffn_bwd(dh_new, res, g_pre, g_post, w24, wo4, tiles, tag):
    tb, ts = tiles
    h, a, gu, s4, f = res
    t = h.shape[0]
    df, dg_post = _norm_bwd(f, g_post, dh_new, 0.5, None, BF16, tm=ts, name=f"{tag}_dres")
    dgu = _ffn_dswiglu(df, wo4, gu, tm=tb, name=f"{tag}_dswiglu")
    d_wo = _mm_tn(s4, df[None], tk=FFN_BLK, name=f"{tag}_dwout")
    dgu8 = dgu.reshape(2 * w24.shape[1], t, FFN_BLK)
    w8 = w24.reshape(2 * w24.shape[1], D_MODEL, FFN_BLK)
    da = _mm(dgu8, w8, trans_b=True, tm=tb, tn=D_MODEL, out_dtype=F32, name=f"{tag}_da")
    d_win = _mm_tn(a[None], dgu8, tk=D_MODEL, name=f"{tag}_dwin")
    dh, dg_pre = _norm_bwd(h, g_pre, da, 1.0, dh_new, F32, tm=ts, name=f"{tag}_dnorm")
    return dh, dg_pre, dg_post, d_win, d_wo.reshape(N_DEV, -1, D_MODEL)


def _blockdiag(w4):
    n, b, _ = w4.shape
    eye = jnp.eye(n, dtype=w4.dtype)
    return (eye[:, None, :, None] * w4[:, :, None, :]).reshape(n * b, n * b)


def _blockdiag_grad(d):
    n = d.shape[0] // HEAD
    x = d.reshape(n, HEAD, n, HEAD)
    return jnp.stack([x[i, :, i, :] for i in range(n)])


def _row(v):
    return v.reshape(1, -1)


def _mixer_fwd(h, g_pre, g_post, wi, wo, P, bd, tiles, tag):
    tb, ts = tiles
    a = _norm_fwd(h, g_pre, tm=ts, name=f"{tag}_norm")
    p = _mm(a[None], wi[None], trans_b=False, tm=tb, tn=N_IN, out_dtype=F32, name=f"{tag}_in")
    lx, lg = p[:, 0:256], p[:, 256:512]
    sb, scc, sx = p[:, 512:768], p[:, 768:1024], p[:, 1024:1280]
    z = p[:, 1280:]
    lxs = [lx, _shift(lx, 1), _shift(lx, 2), _shift(lx, 3)]
    cw = [_row(P['lru_conv_w'][kk]) for kk in range(4)]
    lru_par = cw + [_row(P['lru_conv_b']), _blockdiag(P['lru_wa']), _row(P['lru_ba']), _blockdiag(P['lru_wx']),
                    _row(P['lru_bx']), _row(P['lru_lambda'])]
    la, lb = _rowwise(_lru_pre, lxs, lru_par, [(LRU_W, F32)] * 2, [], tm=ts, name=f"{tag}_lru_pre")
    hs = _lru_scan(la, lb, name=f"{tag}_lru_scan")
    y_lru = _rowwise(lambda gg, hh, ng, b_: _lru_post(b_, gg, hh, ng), [lg, hs], [_row(P['lru_norm_g']), bd],
                     [(LRU_W, F32)], [], tm=ts, name=f"{tag}_lru_post")[0]
    sc_rows = [sb, scc, sx, _shift(scc, 1), _shift(sx, 1), _shift(scc, 2), _shift(sx, 2)]
    sc_par = [_row(P['sc_conv_w'][kk]) for kk in range(3)] + [_row(P['sc_norm_g'])]
    y_sc = _rowwise(lambda *v: _sc_fwd(v[-1], *v[:-1]), sc_rows, sc_par + [bd], [(SC_W, F32)], [], tm=ts,
                    name=f"{tag}_sc")[0]
    cuts = (0, RW_W, 2 * RW_W, 3 * RW_W, RW_IN)
    zs = [z[:, cuts[q]:cuts[q + 1]] for q in range(4)]
    z_rows = zs + [_shift(q, 1) for q in zs]
    pad = lambda m, lo: jnp.pad(m, ((lo, LANES - lo - m.shape[0]), (0, 0)))
    rw_par = [_row(P['rwkv_mu'][cuts[q]:cuts[q + 1]]) for q in range(4)]
    rw_par += [_row(P['rwkv_w0']), pad(P['rwkv_w2'], 0), _row(P['rwkv_a0']), pad(P['rwkv_a2'], 32),
               pad(P['rwkv_g2'], 64), _row(P['rwkv_k_k']), _row(P['rwkv_k_a'])]
    r, w, k2, c, b, v, g = _rowwise(lambda *vv: _rw_pre(vv[-1], *vv[:-1]), z_rows, rw_par + [bd], [(RW_W, F32)] * 7, [],
                                    tm=ts, name=f"{tag}_rw_pre")
    vb = _bcast_cols(v)
    spre, yt = _rw_scan(r, w, k2, c, b, vb, bd, name=f"{tag}_rw_scan")
    y = _cols_to_rows(yt)
    post_par = [_row(P['rwkv_lnx_w']), _row(P['rwkv_lnx_b']), _row(P['rwkv_r_k'])]
    y_rw = _rowwise(lambda *vv: _rw_post(vv[-1], *vv[:-1]), [y, r, k2, v, g], post_par + [bd], [(RW_W, F32)], [], tm=ts,
                    name=f"{tag}_rw_post")[0]
    ycat = jnp.concatenate([y_lru, y_sc, y_rw], axis=1).astype(BF16)
    m = _mm(ycat[None], wo[None], trans_b=False, tm=tb, tn=D_MODEL, out_dtype=F32, name=f"{tag}_out")
    h_new = _res_norm_fwd(h, m, g_post, 1.0, tm=ts, name=f"{tag}_res")
    res = dict(h=h, a=a, m=m, ycat=ycat, lxs=lxs, lru_par=lru_par, lg=lg, la=la, hs=hs, sc_rows=sc_rows, sc_par=sc_par,
               z_rows=z_rows, rw_par=rw_par, r=r, w=w, k2=k2, c=c, b=b, v=v, g=g, vb=vb, spre=spre, y=y, post_par=post_par)
    return h_new, res


def _mixer_bwd(dh_new, R, g_pre, g_post, wi, wo, P, bd, tiles, tag):
    tb, ts = tiles
    dm, dg_post = _norm_bwd(R['m'], g_post, dh_new, 1.0, None, BF16, tm=ts, name=f"{tag}_dres")
    dycat = _mm(dm[None], wo[None], trans_b=True, tm=tb, tn=D_MODEL, out_dtype=F32, name=f"{tag}_dycat")
    d_wo = _mm_tn(R['ycat'][None], dm[None], tk=D_MODEL // 2, name=f"{tag}_dwout")[0]
    dy_lru, dy_sc, dy_rw = dycat[:, 0:256], dycat[:, 256:512], dycat[:, 512:]
    G = {}
    d_lg, d_hs, G['lru_norm_g'] = _rowwise(
        lambda gg, hh, ct, ng, b_: _vjp_rows(_lru_post, 1, 3, 1)(b_, gg, hh, ng, ct),
        [R['lg'], R['hs'], dy_lru], [_row(P['lru_norm_g']), bd], [(LRU_W, F32)] * 2, [(1, LRU_W)], tm=ts,
        name=f"{tag}_lru_dpost")
    d_la, d_lb = _lru_scan_bwd(R['la'], R['hs'], d_hs, name=f"{tag}_lru_dscan")

    def lru_pre_bwd(x0, x1, x2, x3, ca, cb_, *par):
        return _vjp_rows(_lru_pre, 0, 14, 2)(x0, x1, x2, x3, *par, ca, cb_)

    par_shapes = [tuple(q.shape) for q in R['lru_par']]
    outs = _rowwise(lru_pre_bwd, R['lxs'] + [d_la, d_lb], R['lru_par'], [(LRU_W, F32)] * 4, par_shapes, tm=ts,
                    name=f"{tag}_lru_dpre")
    dxs, dpar = outs[:4], outs[4:]
    d_lx = _add_n([dxs[0], _unshift(dxs[1], 1), _unshift(dxs[2], 2), _unshift(dxs[3], 3)], tm=ts, name=f"{tag}_lru_dx")
    G['lru_conv_w'] = jnp.concatenate(dpar[0:4], axis=0)
    G['lru_conv_b'] = dpar[4][0]
    G['lru_wa'] = _blockdiag_grad(dpar[5])
    G['lru_ba'] = dpar[6][0]
    G['lru_wx'] = _blockdiag_grad(dpar[7])
    G['lru_bx'] = dpar[8][0]
    G['lru_lambda'] = dpar[9][0]
    G['lru_norm_g'] = G['lru_norm_g'][0]

    def sc_bwd(*vv):
        rows7, ct, par4, b_ = vv[:7], vv[7], vv[8:12], vv[12]
        return _vjp_rows(_sc_fwd, 1, 11, 1)(b_, *rows7, *par4, ct)

    outs = _rowwise(sc_bwd, R['sc_rows'] + [dy_sc], R['sc_par'] + [bd], [(SC_W, F32)] * 7, [(1, SC_W)] * 4, tm=ts,
                    name=f"{tag}_sc_bwd")
    d_sb = outs[0]
    d_sc = _add_n([outs[1], _unshift(outs[3], 1), _unshift(outs[5], 2)], tm=ts, name=f"{tag}_sc_dc")
    d_sx = _add_n([outs[2], _unshift(outs[4], 1), _unshift(outs[6], 2)], tm=ts, name=f"{tag}_sc_dx")
    G['sc_conv_w'] = jnp.concatenate(outs[7:10], axis=0)
    G['sc_norm_g'] = outs[10][0]

    def rw_post_bwd(*vv):
        rows5, ct, par3, b_ = vv[:5], vv[5], vv[6:9], vv[9]
        return _vjp_rows(_rw_post, 1, 8, 1)(b_, *rows5, *par3, ct)

    outs = _rowwise(rw_post_bwd, [R['y'], R['r'], R['k2'], R['v'], R['g'], dy_rw], R['post_par'] + [bd],
                    [(RW_W, F32)] * 5, [(1, RW_W)] * 3, tm=ts, name=f"{tag}_rw_dpost")
    d_y, dr_p, dk_p, dv_p, d_g = outs[:5]
    G['rwkv_lnx_w'], G['rwkv_lnx_b'], G['rwkv_r_k'] = outs[5][0], outs[6][0], outs[7][0]
    dyb = _bcast_cols(d_y)
    dr_s, d_w, dk_s, d_c, d_b, dvt = _rw_scan_bwd(R['r'], R['w'], R['k2'], R['c'], R['b'], R['vb'], dyb, R['spre'], bd,
                                                 name=f"{tag}_rw_dscan")
    dv_s = _cols_to_rows(dvt)

    def rw_pre_bwd(*vv):
        zrows = vv[0:8]
        dr1, dr2, dw_, dk1, dk2_, dc_, db_, dv1, dv2, dg_ = vv[8:18]
        par, b_ = vv[18:29], vv[29]
        return _vjp_rows(_rw_pre, 1, 19, 7)(b_, *zrows, *par, dr1 + dr2, dw_, dk1 + dk2_, dc_, db_, dv1 + dv2, dg_)

    par_shapes = [tuple(q.shape) for q in R['rw_par']]
    widths = [(q.shape[1], F32) for q in R['z_rows']]
    outs = _rowwise(rw_pre_bwd, R['z_rows'] + [dr_p, dr_s, d_w, dk_p, dk_s, d_c, d_b, dv_p, dv_s, d_g],
                    R['rw_par'] + [bd], widths, par_shapes, tm=ts, name=f"{tag}_rw_dpre")
    d_z = _add_n([jnp.concatenate(outs[0:4], axis=1), _unshift(jnp.concatenate(outs[4:8], axis=1), 1)], tm=ts,
                 name=f"{tag}_rw_dz")
    dpar = outs[8:]
    G['rwkv_mu'] = jnp.concatenate([q[0] for q in dpar[0:4]])
    G['rwkv_w0'], G['rwkv_a0'] = dpar[4][0], dpar[6][0]
    G['rwkv_w2'], G['rwkv_a2'], G['rwkv_g2'] = dpar[5][0:32], dpar[7][32:64], dpar[8][64:128]
    G['rwkv_k_k'], G['rwkv_k_a'] = dpar[9][0], dpar[10][0]

    dp = jnp.concatenate([d_lx, d_lg, d_sb, d_sc, d_sx, d_z], axis=1).astype(BF16)
    da = _mm(dp[None], wi[None], trans_b=True, tm=tb, tn=D_MODEL, out_dtype=F32, name=f"{tag}_da")
    d_wi = _mm_tn(R['a'][None], dp[None], tk=D_MODEL // 2, name=f"{tag}_dwin")[0]
    dh, dg_pre = _norm_bwd(R['h'], g_pre, da, 1.0, dh_new, F32, tm=ts, name=f"{tag}_dnorm")
    return dh, dg_pre, dg_post, d_wi, d_wo, G


def _loss_rows(h, tgt, n_seq, *, tm, name):
    d = h.shape[1]

    def body(h_ref, t_ref, dh_ref, l_ref):
        i = pl.program_id(0)
        row = lax.broadcasted_iota(jnp.int32, (tm, 1), 0) + i * tm
        live = (row >= N_META) & (row < N_META + n_seq)
        e = jnp.where(live, h_ref[...] - t_ref[...], 0.0)
        dh_ref[...] = e * (1.0 / d)
        part = 0.5 * jnp.sum(jnp.sum(e * e, axis=1, keepdims=True) * (1.0 / d), axis=0, keepdims=True)

        @pl.when(i == 0)
        def _():
            l_ref[...] = part

        @pl.when(i > 0)
        def _():
            l_ref[...] += part

    blk = pl.BlockSpec((tm, d), lambda i: (i, 0))
    return pl.pallas_call(body, name=name, grid=(h.shape[0] // tm,), in_specs=[blk, blk],
                          out_specs=[blk, pl.BlockSpec((1, 1), lambda i: (0, 0))],
                          out_shape=[jax.ShapeDtypeStruct(h.shape, F32), jax.ShapeDtypeStruct((1, 1), F32)])(h, tgt)


def _pack(arrs, mult):
    flat = jnp.concatenate([a.reshape(-1).astype(F32) for a in arrs])
    n = flat.shape[0]
    tot = -(-n // mult) * mult
    return jnp.pad(flat, (0, tot - n)).reshape(-1, LANES)


def _unpack(buf, shapes):
    flat = buf.reshape(-1)
    out, off = [], 0
    for s in shapes:
        n = 1
        for q in s:
            n *= q
        out.append(flat[off:off + n].reshape(s))
        off += n
    return out


def _step(W, M, V, x, loss_target):
    n_seq = x.shape[1]
    t_real = N_META + n_seq
    t = (t_real // CHUNK + 1) * CHUNK
    tiles = (_pick(t, (704, 512, 256, 128, 64)), _pick(t, (192, 128, 64)))
    me = 4 * lax.axis_index("x") + 2 * lax.axis_index("y") + lax.axis_index("c")
    n_layer = W['norm_g'].shape[0]

    small_sh = list(SMALL_SHARDED)
    packed = _pack([W[n] for n in small_sh], 8 * LANES)
    gathered = _exchange([W[n].astype(BF16) for n in BIG] + [packed], gather=True, name="gather_weights")
    big8 = dict(zip(BIG, gathered[:-1]))
    pieces = [_unpack(gathered[-1][q], [W[n].shape for n in small_sh]) for q in range(N_DEV)]
    full = {n: W[n] for n in SMALL if n not in SMALL_SHARDED}
    for idx, n in enumerate(small_sh):
        full[n] = jnp.concatenate([pieces[q][idx] for q in range(N_DEV)], axis=SMALL_SHARDED[n])

    def layer_weights(l):
        w24_1 = big8['ffn1_w_in'][:, l].reshape(2, N_DEV // 2, D_MODEL, FFN_BLK)
        wo4_1 = big8['ffn1_w_out'][:, l].reshape(N_DEV // 2, FFN_BLK, D_MODEL)
        w24_2 = big8['ffn2_w_in'][:, l].reshape(2, N_DEV // 2, D_MODEL, FFN_BLK)
        wo4_2 = big8['ffn2_w_out'][:, l].reshape(N_DEV // 2, FFN_BLK, D_MODEL)
        wi = big8['mix_w_in'][:, l].transpose(1, 0, 2).reshape(D_MODEL, N_IN)
        wo = big8['mix_w_out'][:, l].reshape(D_MODEL, D_MODEL)
        return w24_1, wo4_1, w24_2, wo4_2, wi, wo

    bd = jnp.kron(jnp.eye(LANES // HEAD, dtype=F32), jnp.ones((HEAD, HEAD), F32))
    small_layer = [n for n in SMALL if n not in ('meta_tokens', 'norm_g')]

    h = jnp.concatenate([full['meta_tokens'], x[0], jnp.zeros((t - t_real, D_MODEL), F32)], axis=0)
    saved = []
    for l in range(n_layer):
        lw = layer_weights(l)
        ng = [_row(full['norm_g'][l, q]) for q in range(6)]
        P = {n: full[n][l] for n in small_layer}
        h, r1 = _ffn_fwd(h, ng[0], ng[1], lw[0], lw[1], tiles, f"l{l}_ffn1")
        h, r2 = _mixer_fwd(h, ng[2], ng[3], lw[4], lw[5], P, bd, tiles, f"l{l}_mix")
        h, r3 = _ffn_fwd(h, ng[4], ng[5], lw[2], lw[3], tiles, f"l{l}_ffn2")
        saved.append((lw, ng, P, r1, r2, r3))

    tgt = jnp.pad(loss_target[0], ((N_META, t - t_real), (0, 0)))
    dh, loss_part = _loss_rows(h, tgt, n_seq, tm=tiles[1], name="loss")
    loss = lax.psum(loss_part[0, 0], MESH_AXES)

    small_grads = [None] * n_layer
    norm_grads = [None] * n_layer
    recv = [None] * n_layer
    for l in reversed(range(n_layer)):
        lw, ng, P, r1, r2, r3 = saved[l]
        dh, g4, g5, d_win2, d_wo2 = _ffn_bwd(dh, r3, ng[4], ng[5], lw[2], lw[3], tiles, f"l{l}_ffn2")
        dh, g2, g3, d_wi, d_wo, G = _mixer_bwd(dh, r2, ng[2], ng[3], lw[4], lw[5], P, bd, tiles, f"l{l}_mix")
        dh, g0, g1, d_win1, d_wo1 = _ffn_bwd(dh, r1, ng[0], ng[1], lw[0], lw[1], tiles, f"l{l}_ffn1")
        small_grads[l] = G
        norm_grads[l] = jnp.concatenate([g0, g1, g2, g3, g4, g5], axis=0)
        d_wi8 = d_wi.reshape(D_MODEL, N_DEV, N_IN // N_DEV).transpose(1, 0, 2)
        d_wo8 = d_wo.reshape(N_DEV, D_MODEL // N_DEV, D_MODEL)
        recv[l] = _exchange([d_win1, d_wo1, d_win2, d_wo2, d_wi8, d_wo8], gather=False, name=f"l{l}_grad_exchange")

    gs = {n: jnp.stack([small_grads[l][n] for l in range(n_layer)]) for n in small_layer}
    gs['norm_g'] = jnp.stack(norm_grads)
    gs['meta_tokens'] = dh[:N_META]
    gpack = _pack([gs[n] for n in SMALL], 8 * LANES)
    gall = _exchange([gpack], gather=True, name="gather_small_grads")[0]
    gsum = _rowwise(lambda parts: _sum_slots(parts), [gall], [], [(LANES, F32)], [], tm=8, name="sum_small_grads")[0]
    gfull = dict(zip(SMALL, _unpack(gsum, [gs[n].shape for n in SMALL])))

    def my_shard(n, a):
        if n not in SMALL_SHARDED:
            return a
        ax = SMALL_SHARDED[n]
        size = a.shape[ax] // N_DEV
        return lax.dynamic_slice_in_dim(a, me * size, size, axis=ax)

    g_loc = [my_shard(n, gfull[n]) for n in SMALL]
    shapes = [W[n].shape for n in SMALL]
    bufs = [_pack(g_loc, 8 * LANES)] + [_pack([D[n] for n in SMALL], 8 * LANES) for D in (W, M, V)]
    d_s, m_s, v_s = _rowwise(_adamw_rows, bufs, [], [(LANES, F32)] * 3, [], tm=8, name="adamw_small")
    out = {'grad': dict(zip(SMALL, g_loc)), 'delta': dict(zip(SMALL, _unpack(d_s, shapes))),
           'm': dict(zip(SMALL, _unpack(m_s, shapes))), 'v': dict(zip(SMALL, _unpack(v_s, shapes)))}

    order = ['ffn1_w_in', 'ffn1_w_out', 'ffn2_w_in', 'ffn2_w_out', 'mix_w_in', 'mix_w_out']
    for idx, n in enumerate(order):
        per_layer = []
        for l in range(n_layer):
            parts = recv[l][idx]
            rows, cols = W[n].shape[1], W[n].shape[2]
            per_layer.append(_reduce_adamw(parts.reshape(N_DEV, rows, cols), W[n][l], M[n][l], V[n][l],
                                           name=f"l{l}_adamw_{n}"))
        for q, key in enumerate(('grad', 'delta', 'm', 'v')):
            out[key][n] = jnp.stack([per_layer[l][q] for l in range(n_layer)])

    return (loss, dh[N_META:t_real][None],
            *[out['grad'][n] for n in WEIGHTS], *[out['delta'][n] for n in WEIGHTS],
            *[out['m'][n] for n in WEIGHTS], *[out['v'][n] for n in WEIGHTS])


def kernel(x, meta_tokens, norm_g, ffn1_w_in, ffn1_w_out, ffn2_w_in, ffn2_w_out, mix_w_in, mix_w_out, lru_conv_w, lru_conv_b, lru_wa, lru_ba, lru_wx, lru_bx, lru_lambda, lru_norm_g, sc_conv_w, sc_norm_g, rwkv_mu, rwkv_w0, rwkv_w2, rwkv_a0, rwkv_a2, rwkv_g2, rwkv_k_k, rwkv_k_a, rwkv_r_k, rwkv_lnx_w, rwkv_lnx_b, loss_target, m_meta_tokens, m_norm_g, m_ffn1_w_in, m_ffn1_w_out, m_ffn2_w_in, m_ffn2_w_out, m_mix_w_in, m_mix_w_out, m_lru_conv_w, m_lru_conv_b, m_lru_wa, m_lru_ba, m_lru_wx, m_lru_bx, m_lru_lambda, m_lru_norm_g, m_sc_conv_w, m_sc_norm_g, m_rwkv_mu, m_rwkv_w0, m_rwkv_w2, m_rwkv_a0, m_rwkv_a2, m_rwkv_g2, m_rwkv_k_k, m_rwkv_k_a, m_rwkv_r_k, m_rwkv_lnx_w, m_rwkv_lnx_b, v_meta_tokens, v_norm_g, v_ffn1_w_in, v_ffn1_w_out, v_ffn2_w_in, v_ffn2_w_out, v_mix_w_in, v_mix_w_out, v_lru_conv_w, v_lru_conv_b, v_lru_wa, v_lru_ba, v_lru_wx, v_lru_bx, v_lru_lambda, v_lru_norm_g, v_sc_conv_w, v_sc_norm_g, v_rwkv_mu, v_rwkv_w0, v_rwkv_w2, v_rwkv_a0, v_rwkv_a2, v_rwkv_g2, v_rwkv_k_k, v_rwkv_k_a, v_rwkv_r_k, v_rwkv_lnx_w, v_rwkv_lnx_b):
    w_vals = (meta_tokens, norm_g, ffn1_w_in, ffn1_w_out, ffn2_w_in, ffn2_w_out, mix_w_in, mix_w_out, lru_conv_w, lru_conv_b, lru_wa, lru_ba, lru_wx, lru_bx, lru_lambda, lru_norm_g, sc_conv_w, sc_norm_g, rwkv_mu, rwkv_w0, rwkv_w2, rwkv_a0, rwkv_a2, rwkv_g2, rwkv_k_k, rwkv_k_a, rwkv_r_k, rwkv_lnx_w, rwkv_lnx_b)
    m_vals = (m_meta_tokens, m_norm_g, m_ffn1_w_in, m_ffn1_w_out, m_ffn2_w_in, m_ffn2_w_out, m_mix_w_in, m_mix_w_out, m_lru_conv_w, m_lru_conv_b, m_lru_wa, m_lru_ba, m_lru_wx, m_lru_bx, m_lru_lambda, m_lru_norm_g, m_sc_conv_w, m_sc_norm_g, m_rwkv_mu, m_rwkv_w0, m_rwkv_w2, m_rwkv_a0, m_rwkv_a2, m_rwkv_g2, m_rwkv_k_k, m_rwkv_k_a, m_rwkv_r_k, m_rwkv_lnx_w, m_rwkv_lnx_b)
    v_vals = (v_meta_tokens, v_norm_g, v_ffn1_w_in, v_ffn1_w_out, v_ffn2_w_in, v_ffn2_w_out, v_mix_w_in, v_mix_w_out, v_lru_conv_w, v_lru_conv_b, v_lru_wa, v_lru_ba, v_lru_wx, v_lru_bx, v_lru_lambda, v_lru_norm_g, v_sc_conv_w, v_sc_norm_g, v_rwkv_mu, v_rwkv_w0, v_rwkv_w2, v_rwkv_a0, v_rwkv_a2, v_rwkv_g2, v_rwkv_k_k, v_rwkv_k_a, v_rwkv_r_k, v_rwkv_lnx_w, v_rwkv_lnx_b)
    return _step(dict(zip(WEIGHTS, w_vals)), dict(zip(WEIGHTS, m_vals)), dict(zip(WEIGHTS, v_vals)), x, loss_target)
```

```python
import functools

import jax
import jax.numpy as jnp
from jax import lax
from jax.experimental import pallas as pl
from jax.experimental.pallas import tpu as pltpu

F32 = jnp.float32
BF16 = jnp.bfloat16
HIGHEST = lax.Precision.HIGHEST

N_DEV = 8
MESH_AXES = ("x", "y", "c")
N_META = 16
D_MODEL = 1024
LRU_W = 256
SC_W = 256
RW_W = 512
HEAD = 64
LANES = 128
CHUNK = 64
RW_IN = 1664
N_IN = 2944
FFN_BLK = 704
RMS_EPS = 1e-6
LNX_EPS = 64e-5
LRU_C = 8.0
ADAM_LR, ADAM_B1, ADAM_B2, ADAM_EPS, ADAM_WD, ADAM_STEP = 0.001, 0.9, 0.999, 1e-08, 0.01, 10

WEIGHTS = ['meta_tokens', 'norm_g', 'ffn1_w_in', 'ffn1_w_out', 'ffn2_w_in', 'ffn2_w_out', 'mix_w_in', 'mix_w_out',
           'lru_conv_w', 'lru_conv_b', 'lru_wa', 'lru_ba', 'lru_wx', 'lru_bx', 'lru_lambda', 'lru_norm_g',
           'sc_conv_w', 'sc_norm_g', 'rwkv_mu', 'rwkv_w0', 'rwkv_w2', 'rwkv_a0', 'rwkv_a2', 'rwkv_g2', 'rwkv_k_k',
           'rwkv_k_a', 'rwkv_r_k', 'rwkv_lnx_w', 'rwkv_lnx_b']
BIG = ['ffn1_w_in', 'ffn1_w_out', 'ffn2_w_in', 'ffn2_w_out', 'mix_w_in', 'mix_w_out']
SMALL_SHARDED = {'meta_tokens': 1, 'norm_g': 2, 'lru_conv_w': 2, 'sc_conv_w': 2, 'rwkv_w2': 2, 'rwkv_a2': 2, 'rwkv_g2': 2}
SMALL = [n for n in WEIGHTS if n not in BIG]


def _pick(n, cands):
    for c in cands:
        if n % c == 0:
            return c
    raise ValueError(f"no tile for {n}")


def _rowwise(fn, rows, params, row_outs, acc_outs, *, tm, name):
    nr, npar, nro, nao = len(rows), len(params), len(row_outs), len(acc_outs)
    n_rows = rows[0].shape[-2]
    assert n_rows % tm == 0, (name, n_rows, tm)

    def body(*refs):
        vals = [r[...] for r in refs[:nr + npar]]
        outs = fn(*vals)
        if not isinstance(outs, (tuple, list)):
            outs = (outs,)
        assert len(outs) == nro + nao, (name, len(outs))
        for o_ref, o in zip(refs[nr + npar:nr + npar + nro], outs[:nro]):
            o_ref[...] = o.astype(o_ref.dtype)
        step = pl.program_id(0)
        for a_ref, a in zip(refs[nr + npar + nro:], outs[nro:]):
            @pl.when(step == 0)
            def _(a_ref=a_ref, a=a):
                a_ref[...] = a.astype(F32)

            @pl.when(step > 0)
            def _(a_ref=a_ref, a=a):
                a_ref[...] += a.astype(F32)

    def row_spec(shape):
        if len(shape) == 2:
            return pl.BlockSpec((tm, shape[1]), lambda i: (i, 0))
        return pl.BlockSpec((shape[0], tm, shape[2]), lambda i: (0, i, 0))

    def full_spec(shape):
        nd = len(shape)
        return pl.BlockSpec(tuple(shape), lambda i, nd=nd: (0,) * nd)

    in_specs = [row_spec(r.shape) for r in rows] + [full_spec(p.shape) for p in params]
    out_shape = [jax.ShapeDtypeStruct((n_rows, w), dt) for (w, dt) in row_outs]
    out_shape += [jax.ShapeDtypeStruct(tuple(s), F32) for s in acc_outs]
    out_specs = [row_spec((n_rows, w)) for (w, _) in row_outs] + [full_spec(s) for s in acc_outs]
    res = pl.pallas_call(body, name=name, grid=(n_rows // tm,), in_specs=in_specs, out_specs=out_specs,
                         out_shape=out_shape)(*rows, *params)
    return tuple(res)


def _mm(a3, b3, *, trans_b, tm, tn, out_dtype, name):
    nj, m, kb = a3.shape
    n = b3.shape[1] if trans_b else b3.shape[2]
    dims = (((1,), (1,)), ((), ())) if trans_b else (((1,), (0,)), ((), ()))

    def body(a_ref, b_ref, o_ref, acc_ref):
        j = pl.program_id(2)

        @pl.when(j == 0)
        def _():
            acc_ref[...] = jnp.zeros_like(acc_ref)

        acc_ref[...] += lax.dot_general(a_ref[0], b_ref[0], dims, preferred_element_type=F32)

        @pl.when(j == nj - 1)
        def _():
            o_ref[...] = acc_ref[...].astype(o_ref.dtype)

    if trans_b:
        b_spec = pl.BlockSpec((1, tn, kb), lambda i, c, j: (j, c, 0))
    else:
        b_spec = pl.BlockSpec((1, kb, tn), lambda i, c, j: (j, 0, c))
    return pl.pallas_call(
        body, name=name, grid=(m // tm, n // tn, nj),
        in_specs=[pl.BlockSpec((1, tm, kb), lambda i, c, j: (j, i, 0)), b_spec],
        out_specs=pl.BlockSpec((tm, tn), lambda i, c, j: (i, c)),
        out_shape=jax.ShapeDtypeStruct((m, n), out_dtype),
        scratch_shapes=[pltpu.VMEM((tm, tn), F32)],
    )(a3, b3)


def _mm_tn(a3, b3, *, tk, name):
    ja, t, ka = a3.shape
    jb, _, n = b3.shape
    nj = max(ja, jb)

    def body(a_ref, b_ref, o_ref):
        o_ref[0] = lax.dot_general(a_ref[0], b_ref[0], (((0,), (0,)), ((), ())),
                                   preferred_element_type=F32).astype(o_ref.dtype)

    return pl.pallas_call(
        body, name=name, grid=(nj, ka // tk),
        in_specs=[pl.BlockSpec((1, t, tk), (lambda j, c: (j, 0, c)) if ja > 1 else (lambda j, c: (0, 0, c))),
                  pl.BlockSpec((1, t, n), (lambda j, c: (j, 0, 0)) if jb > 1 else (lambda j, c: (0, 0, 0)))],
        out_specs=pl.BlockSpec((1, tk, n), lambda j, c: (j, c, 0)),
        out_shape=jax.ShapeDtypeStruct((nj, ka, n), BF16),
    )(a3, b3)


def _ffn_in(a, w24, *, tm, name):
    t, d = a.shape
    nb, fb = w24.shape[1], w24.shape[3]

    def body(a_ref, w_ref, gu_ref, s_ref):
        x = a_ref[...]
        g = jnp.dot(x, w_ref[0, 0], preferred_element_type=F32)
        u = jnp.dot(x, w_ref[1, 0], preferred_element_type=F32)
        gu_ref[0, 0] = g.astype(BF16)
        gu_ref[1, 0] = u.astype(BF16)
        s_ref[0] = (g * jax.nn.sigmoid(g) * u).astype(BF16)

    return pl.pallas_call(
        body, name=name, grid=(nb, t // tm),
        in_specs=[pl.BlockSpec((tm, d), lambda j, i: (i, 0)), pl.BlockSpec((2, 1, d, fb), lambda j, i: (0, j, 0, 0))],
        out_specs=[pl.BlockSpec((2, 1, tm, fb), lambda j, i: (0, j, i, 0)), pl.BlockSpec((1, tm, fb), lambda j, i: (j, i, 0))],
        out_shape=[jax.ShapeDtypeStruct((2, nb, t, fb), BF16), jax.ShapeDtypeStruct((nb, t, fb), BF16)],
    )(a, w24)


def _ffn_dswiglu(df, wo4, gu, *, tm, name):
    t, d = df.shape
    nb, fb = wo4.shape[0], wo4.shape[1]

    def body(df_ref, wo_ref, gu_ref, dg_ref):
        ds = lax.dot_general(df_ref[...], wo_ref[0], (((1,), (1,)), ((), ())), preferred_element_type=F32)
        g = gu_ref[0, 0].astype(F32)
        u = gu_ref[1, 0].astype(F32)
        sig = jax.nn.sigmoid(g)
        dg_ref[0, 0] = (ds * u * sig * (1.0 + g * (1.0 - sig))).astype(BF16)
        dg_ref[1, 0] = (ds * g * sig).astype(BF16)

    return pl.pallas_call(
        body, name=name, grid=(nb, t // tm),
        in_specs=[pl.BlockSpec((tm, d), lambda j, i: (i, 0)), pl.BlockSpec((1, fb, d), lambda j, i: (j, 0, 0)),
                  pl.BlockSpec((2, 1, tm, fb), lambda j, i: (0, j, i, 0))],
        out_specs=pl.BlockSpec((2, 1, tm, fb), lambda j, i: (0, j, i, 0)),
        out_shape=jax.ShapeDtypeStruct((2, nb, t, fb), BF16),
    )(df, wo4, gu)


def _rms(x, g):
    return x * lax.rsqrt(jnp.mean(x * x, axis=-1, keepdims=True) + RMS_EPS) * g


def _rms_bwd(x, g, dy):
    rstd = lax.rsqrt(jnp.mean(x * x, axis=-1, keepdims=True) + RMS_EPS)
    xh = x * rstd
    dxh = dy * g
    dx = rstd * (dxh - xh * jnp.mean(dxh * xh, axis=-1, keepdims=True))
    return dx, jnp.sum(dy * xh, axis=0, keepdims=True)


def _seg_sum_impl(x, bd):
    parts = [jnp.dot(x[:, q * LANES:(q + 1) * LANES], bd, preferred_element_type=F32, precision=HIGHEST)
             for q in range(x.shape[1] // LANES)]
    return parts[0] if len(parts) == 1 else jnp.concatenate(parts, axis=1)


@jax.custom_vjp
def _seg_sum(x, bd):
    return _seg_sum_impl(x, bd)


def _seg_sum_fwd(x, bd):
    return _seg_sum_impl(x, bd), bd


def _seg_sum_bwd(bd, ct):
    return _seg_sum_impl(ct, bd), jnp.zeros_like(bd)


_seg_sum.defvjp(_seg_sum_fwd, _seg_sum_bwd)


def _group_rms(y, g, bd):
    return y * lax.rsqrt(_seg_sum(y * y, bd) * (1.0 / HEAD) + RMS_EPS) * g


def _expm1(x):
    return jnp.where(jnp.abs(x) < 1e-2, x * (1.0 + x * (0.5 + x * (1.0 / 6.0))), jnp.exp(x) - 1.0)


def _lru_pre(x0, x1, x2, x3, cw0, cw1, cw2, cw3, cb, wa, ba, wx, bx, lam):
    u = x3 * cw0 + x2 * cw1 + x1 * cw2 + x0 * cw3 + cb
    r = jax.nn.sigmoid(jnp.dot(u, wa, preferred_element_type=F32, precision=HIGHEST) + ba)
    i = jax.nn.sigmoid(jnp.dot(u, wx, preferred_element_type=F32, precision=HIGHEST) + bx)
    log_a = -LRU_C * r * jax.nn.softplus(-lam)
    return jnp.exp(log_a), jnp.sqrt(-_expm1(2.0 * log_a)) * (i * u)


def _lru_post(bd, gate, hs, ng):
    return _group_rms(jax.nn.gelu(gate) * hs, ng, bd)


def _sc_fwd(bd, b, c0, x0, c1, x1, c2, x2, w0, w1, w2, ng):
    return _group_rms(b * (w0 * (c2 * x2) + w1 * (c1 * x1) + w2 * (c0 * x0)), ng, bd)


def _rw_pre(bd, zr, zk, zv, zt, sr, sk, sv, st, mur, muk, muv, mut, w0, w2p, a0, a2p, g2p, k_k, k_a):
    r, k, v, tail = zr + (sr - zr) * mur, zk + (sk - zk) * muk, zv + (sv - zv) * muv, zt + (st - zt) * mut
    lane = lax.broadcasted_iota(jnp.int32, tail.shape, 1)
    act = jnp.where(lane < 32, jnp.tanh(tail), jnp.where(lane < 64, tail, jax.nn.sigmoid(tail)))
    dot = functools.partial(jnp.dot, preferred_element_type=F32, precision=HIGHEST)
    w_log = -jax.nn.softplus(-(w0 + dot(act, w2p))) - 0.5
    w = jnp.exp(-jnp.exp(w_log))
    a = jax.nn.sigmoid(a0 + dot(act, a2p))
    g = dot(act, g2p)
    kk = k * k_k
    k2 = k * (1.0 + (a - 1.0) * k_a)
    kkn = kk * lax.rsqrt(jnp.maximum(_seg_sum(kk * kk, bd), 1e-24))
    return r, w, k2, -kkn, kkn * a, v, g


def _rw_post(bd, y, r, k2, v, g, lnw, lnb, r_k):
    mean = _seg_sum(y, bd) * (1.0 / HEAD)
    yc = y - mean
    var = _seg_sum(yc * yc, bd) * (1.0 / HEAD)
    yn = yc * lax.rsqrt(var + LNX_EPS) * lnw + lnb
    return (yn + _seg_sum(r * k2 * r_k, bd) * v) * g


def _vjp_rows(fwd, n_static, n_in, n_ct):
    def fn(*args):
        static, prim, cts = args[:n_static], args[n_static:n_static + n_in], args[n_static + n_in:]
        assert len(cts) == n_ct
        _, vjp = jax.vjp(functools.partial(fwd, *static), *prim)
        return vjp(cts[0] if n_ct == 1 else tuple(cts))
    return fn


SUBLANES = 8


def _store_row(ref, i, cols, row):
    base = pl.multiple_of((i // SUBLANES) * SUBLANES, SUBLANES)
    sub = lax.broadcasted_iota(jnp.int32, (SUBLANES, row.shape[1]), 0)
    ref[pl.ds(base, SUBLANES), cols] = jnp.where(sub == i % SUBLANES, row, ref[pl.ds(base, SUBLANES), cols])


def _lru_scan(a, b, name):
    t, w = a.shape

    def body(a_ref, b_ref, h_ref):
        h_ref[...] = jnp.zeros_like(h_ref)

        def step(i, h):
            h = a_ref[pl.ds(i, 1), :] * h + b_ref[pl.ds(i, 1), :]
            _store_row(h_ref, i, slice(None), h)
            return h
        lax.fori_loop(0, t, step, jnp.zeros((1, w), F32), unroll=8)

    return pl.pallas_call(body, name=name, out_shape=jax.ShapeDtypeStruct((t, w), F32))(a, b)


def _lru_scan_bwd(a, hs, dhs, name):
    t, w = a.shape

    def body(a_ref, h_ref, dh_ref, da_ref, db_ref):
        da_ref[...] = jnp.zeros_like(da_ref)
        db_ref[...] = jnp.zeros_like(db_ref)

        def step(n, carry):
            i = t - 1 - n
            lam = dh_ref[pl.ds(i, 1), :] + carry
            _store_row(db_ref, i, slice(None), lam)
            hprev = h_ref[pl.ds(jnp.maximum(i - 1, 0), 1), :]
            _store_row(da_ref, i, slice(None), jnp.where(i > 0, lam * hprev, 0.0))
            return a_ref[pl.ds(i, 1), :] * lam
        lax.fori_loop(0, t, step, jnp.zeros((1, w), F32), unroll=8)

    return pl.pallas_call(body, name=name, out_shape=[jax.ShapeDtypeStruct((t, w), F32)] * 2)(a, hs, dhs)


N_PAIR = RW_W // LANES


def _bcast_cols(v):
    t = v.shape[0]
    x = v.reshape(t, N_PAIR, 2, HEAD)
    shape = (t, N_PAIR, HEAD, LANES)
    lane = lax.broadcasted_iota(jnp.int32, shape, 3)
    return jnp.where(lane < HEAD, jnp.broadcast_to(x[:, :, 0, :, None], shape), jnp.broadcast_to(x[:, :, 1, :, None], shape))


def _cols_to_rows(yt):
    nch = yt.shape[2] // LANES
    x = yt.reshape(N_PAIR, HEAD, nch, 2, CHUNK).transpose(2, 4, 0, 3, 1)
    return x.reshape(nch * CHUNK, RW_W)


def _stacked_bf16(bd):
    return jnp.concatenate([bd, bd], axis=0).astype(BF16)


def _group_sums(prods, bd2):
    x = jnp.concatenate(prods, axis=0)
    hi = x.astype(BF16)
    lo = (x - hi.astype(F32)).astype(BF16)
    return jnp.dot(jnp.concatenate([hi, lo], axis=1), bd2, preferred_element_type=F32)


def _rw_scan(r_prev, w, k, c, b, vb, bd2, name):
    t = w.shape[0]
    nch = t // CHUNK

    def body(r_ref, w_ref, k_ref, c_ref, b_ref, vb_ref, bd_ref, spre_ref, yt_ref, s_ref):
        @pl.when(pl.program_id(0) == 0)
        def _():
            s_ref[...] = jnp.zeros_like(s_ref)

        yt_ref[...] = jnp.zeros_like(yt_ref)
        bdv = bd_ref[...]
        lane = lax.broadcasted_iota(jnp.int32, (HEAD, LANES), 1) % CHUNK

        def step(i, st):
            rows = [ref[pl.ds(i, 1), :] for ref in (r_ref, w_ref, k_ref, c_ref, b_ref)]
            rr, ww, kk, cc, bb = [[x[:, p * LANES:(p + 1) * LANES] for p in range(N_PAIR)] for x in rows]
            red = _group_sums([st[p] * cc[p] for p in range(N_PAIR)] + [st[p] * rr[p] for p in range(N_PAIR)], bdv)
            new = []
            for p in range(N_PAIR):
                spre_ref[i, p] = st[p]
                new.append(st[p] * ww[p] + red[p * HEAD:(p + 1) * HEAD] * bb[p] + vb_ref[i, p] * kk[p])
                yt_ref[p] = jnp.where(lane == i, red[(N_PAIR + p) * HEAD:(N_PAIR + p + 1) * HEAD], yt_ref[p])
            return tuple(new)

        st = lax.fori_loop(0, CHUNK, step, tuple(s_ref[p] for p in range(N_PAIR)))
        for p in range(N_PAIR):
            s_ref[p] = st[p]

    row = pl.BlockSpec((CHUNK, RW_W), lambda i: (i, 0))
    big = pl.BlockSpec((CHUNK, N_PAIR, HEAD, LANES), lambda i: (i, 0, 0, 0))
    return pl.pallas_call(
        body, name=name, grid=(nch,),
        in_specs=[row] * 5 + [big, pl.BlockSpec((2 * LANES, LANES), lambda i: (0, 0))],
        out_specs=[big, pl.BlockSpec((N_PAIR, HEAD, LANES), lambda i: (0, 0, i))],
        out_shape=[jax.ShapeDtypeStruct((t, N_PAIR, HEAD, LANES), F32), jax.ShapeDtypeStruct((N_PAIR, HEAD, nch * LANES), F32)],
        scratch_shapes=[pltpu.VMEM((N_PAIR, HEAD, LANES), F32)],
    )(r_prev, w, k, c, b, vb, bd2)


def _rw_scan_bwd(r, w, k, c, b, vb, dyb, spre, bd, name):
    t = r.shape[0]
    nch = t // CHUNK

    def body(r_ref, w_ref, k_ref, c_ref, b_ref, vb_ref, dyb_ref, spre_ref, bd_ref,
             dr_ref, dw_ref, dk_ref, dc_ref, db_ref, dvt_ref, g_ref, snext_ref):
        @pl.when(pl.program_id(0) == 0)
        def _():
            g_ref[...] = jnp.zeros_like(g_ref)
            snext_ref[...] = jnp.zeros_like(snext_ref)

        for ref in (dr_ref, dw_ref, dk_ref, dc_ref, db_ref, dvt_ref):
            ref[...] = jnp.zeros_like(ref)
        bdv = bd_ref[...]
        lane = lax.broadcasted_iota(jnp.int32, (HEAD, LANES), 1) % CHUNK

        def sum0(x):
            return jnp.sum(x, axis=0, keepdims=True)

        def step(n, gs):
            i = CHUNK - 1 - n
            rows = [ref[pl.ds(i, 1), :] for ref in (r_ref, w_ref, k_ref, c_ref, b_ref)]
            rr, ww, kk, cc, bb = [[x[:, p * LANES:(p + 1) * LANES] for p in range(N_PAIR)] for x in rows]
            sp = [spre_ref[i, p] for p in range(N_PAIR)]
            dy = [dyb_ref[i, p] for p in range(N_PAIR)]
            gs = [gs[p] + dy[p] * rr[p] for p in range(N_PAIR)]
            prods = ([sp[p] * cc[p] for p in range(N_PAIR)] + [gs[p] * bb[p] for p in range(N_PAIR)]
                     + [gs[p] * kk[p] for p in range(N_PAIR)])
            red = _group_sums(prods, bdv)
            new = []
            for p in range(N_PAIR):
                sa = red[p * HEAD:(p + 1) * HEAD]
                dsa = red[(N_PAIR + p) * HEAD:(N_PAIR + p + 1) * HEAD]
                dv = red[(2 * N_PAIR + p) * HEAD:(2 * N_PAIR + p + 1) * HEAD]
                cols = pl.ds(p * LANES, LANES)
                _store_row(dr_ref, i, cols, sum0(snext_ref[p] * dy[p]))
                _store_row(dw_ref, i, cols, sum0(gs[p] * sp[p]))
                _store_row(db_ref, i, cols, sum0(gs[p] * sa))
                _store_row(dk_ref, i, cols, sum0(gs[p] * vb_ref[i, p]))
                _store_row(dc_ref, i, cols, sum0(sp[p] * dsa))
                dvt_ref[p] = jnp.where(lane == i, dv, dvt_ref[p])
                snext_ref[p] = sp[p]
                new.append(gs[p] * ww[p] + dsa * cc[p])
            return tuple(new)

        gs = lax.fori_loop(0, CHUNK, step, tuple(g_ref[p] for p in range(N_PAIR)))
        for p in range(N_PAIR):
            g_ref[p] = gs[p]

    row = pl.BlockSpec((CHUNK, RW_W), lambda i: (nch - 1 - i, 0))
    big = pl.BlockSpec((CHUNK, N_PAIR, HEAD, LANES), lambda i: (nch - 1 - i, 0, 0, 0))
    return pl.pallas_call(
        body, name=name, grid=(nch,),
        in_specs=[row] * 5 + [big, big, big, pl.BlockSpec((2 * LANES, LANES), lambda i: (0, 0))],
        out_specs=[row] * 5 + [pl.BlockSpec((N_PAIR, HEAD, LANES), lambda i: (0, 0, nch - 1 - i))],
        out_shape=[jax.ShapeDtypeStruct((t, RW_W), F32)] * 5 + [jax.ShapeDtypeStruct((N_PAIR, HEAD, nch * LANES), F32)],
        scratch_shapes=[pltpu.VMEM((N_PAIR, HEAD, LANES), F32), pltpu.VMEM((N_PAIR, HEAD, LANES), F32)],
    )(r, w, k, c, b, vb, dyb, spre, bd)


def _exchange(xs, *, gather, name):
    n = len(xs)

    def body(*refs):
        x_refs, o_refs = refs[:n], refs[n:2 * n]
        send_sems, recv_sems, local_sems = refs[2 * n:]
        mx, my, mc = lax.axis_index("x"), lax.axis_index("y"), lax.axis_index("c")
        me = 4 * mx + 2 * my + mc
        copies = []
        for k in range(n):
            src_me = x_refs[k] if gather else x_refs[k].at[me]
            loc = pltpu.make_async_copy(src_me, o_refs[k].at[me], local_sems.at[k])
            loc.start()
            copies.append(loc)
        sends, recvs = [], []
        for d in range(1, N_DEV):
            px, py, pc = mx ^ ((d >> 2) & 1), my ^ ((d >> 1) & 1), mc ^ (d & 1)
            peer = 4 * px + 2 * py + pc
            for k in range(n):
                cp = pltpu.make_async_remote_copy(
                    src_ref=x_refs[k] if gather else x_refs[k].at[peer], dst_ref=o_refs[k].at[me],
                    send_sem=send_sems.at[k, d - 1], recv_sem=recv_sems.at[k, d - 1],
                    device_id=(px, py, pc), device_id_type=pl.DeviceIdType.MESH)
                cp.start()
                sends.append(cp)
                recvs.append(pltpu.make_async_remote_copy(
                    src_ref=x_refs[k] if gather else x_refs[k].at[peer], dst_ref=o_refs[k].at[peer],
                    send_sem=send_sems.at[k, d - 1], recv_sem=recv_sems.at[k, d - 1],
                    device_id=(px, py, pc), device_id_type=pl.DeviceIdType.MESH))
        for cp in sends:
            cp.wait_send()
        for cp in recvs:
            cp.wait_recv()
        for loc in copies:
            loc.wait()

    out_shape = [jax.ShapeDtypeStruct(((N_DEV,) + x.shape) if gather else x.shape, x.dtype) for x in xs]
    any_spec = pl.BlockSpec(memory_space=pl.ANY)
    return pl.pallas_call(
        body, name=name, in_specs=[any_spec] * n, out_specs=[any_spec] * n, out_shape=out_shape,
        scratch_shapes=[pltpu.SemaphoreType.DMA((n, N_DEV - 1)), pltpu.SemaphoreType.DMA((n, N_DEV - 1)),
                        pltpu.SemaphoreType.DMA((n,))],
    )(*xs)


def _adamw_rows(g, w, m, v):
    m = ADAM_B1 * m + (1.0 - ADAM_B1) * g
    v = ADAM_B2 * v + (1.0 - ADAM_B2) * (g * g)
    m_hat = m / (1.0 - ADAM_B1 ** ADAM_STEP)
    v_hat = v / (1.0 - ADAM_B2 ** ADAM_STEP)
    return -ADAM_LR * (m_hat / (jnp.sqrt(v_hat) + ADAM_EPS) + ADAM_WD * w), m, v


def _sum_slots(parts):
    g = parts[0].astype(F32)
    for q in range(1, N_DEV):
        g = g + parts[q].astype(F32)
    return g


def _reduce_adamw(parts, w, m, v, name):
    rows, cols = w.shape

    def fn(parts, w, m, v):
        g = _sum_slots(parts)
        return (g,) + _adamw_rows(g, w, m, v)

    return _rowwise(fn, [parts, w, m, v], [], [(cols, F32)] * 4, [], tm=_pick(rows, (256, 128, 64, 32, 16, 8)), name=name)


def _shift(x, n):
    return jnp.pad(x, ((n, 0), (0, 0)))[:-n]


def _unshift(x, n):
    return jnp.pad(x, ((0, n), (0, 0)))[n:]


def _add_n(xs, *, tm, name):
    def fn(*vals):
        s = vals[0]
        for x in vals[1:]:
            s = s + x
        return s
    return _rowwise(fn, xs, [], [(xs[0].shape[1], F32)], [], tm=tm, name=name)[0]


def _norm_fwd(h, g, *, tm, name):
    return _rowwise(lambda x, gg: _rms(x, gg), [h], [g], [(h.shape[1], BF16)], [], tm=tm, name=name)[0]


def _res_norm_fwd(h, f, g, scale, *, tm, name):
    return _rowwise(lambda hh, ff, gg: hh + scale * _rms(ff, gg), [h, f], [g], [(h.shape[1], F32)], [], tm=tm, name=name)[0]


def _norm_bwd(x, g, dy, scale, res, out_dtype, *, tm, name):
    if res is None:
        def fn(xx, dd, gg):
            dx, dg = _rms_bwd(xx, gg, dd * scale)
            return dx, dg
        rows = [x, dy]
    else:
        def fn(xx, dd, rr, gg):
            dx, dg = _rms_bwd(xx, gg, dd * scale)
            return dx + rr, dg
        rows = [x, dy, res]
    return _rowwise(fn, rows, [g], [(x.shape[1], out_dtype)], [(1, x.shape[1])], tm=tm, name=name)


def _ffn_fwd(h, g_pre, g_post, w24, wo4, tiles, tag):
    tb, ts = tiles
    a = _norm_fwd(h, g_pre, tm=ts, name=f"{tag}_norm")
    gu, s4 = _ffn_in(a, w24, tm=tb, name=f"{tag}_in")
    f = _mm(s4, wo4, trans_b=False, tm=tb, tn=D_MODEL, out_dtype=F32, name=f"{tag}_out")
    h_new = _res_norm_fwd(h, f, g_post, 0.5, tm=ts, name=f"{tag}_res")
    return h_new, (h, a, gu, s4, f)


def _ffn_bwd(dh_new, res, g_pre, g_post, w24, wo4, tiles, tag):
    tb, ts = tiles
    h, a, gu, s4, f = res
    t = h.shape[0]
    df, dg_post = _norm_bwd(f, g_post, dh_new, 0.5, None, BF16, tm=ts, name=f"{tag}_dres")
    dgu = _ffn_dswiglu(df, wo4, gu, tm=tb, name=f"{tag}_dswiglu")
    d_wo = _mm_tn(s4, df[None], tk=FFN_BLK, name=f"{tag}_dwout")
    dgu8 = dgu.reshape(2 * w24.shape[1], t, FFN_BLK)
    w8 = w24.reshape(2 * w24.shape[1], D_MODEL, FFN_BLK)
    da = _mm(dgu8, w8, trans_b=True, tm=tb, tn=D_MODEL, out_dtype=F32, name=f"{tag}_da")
    d_win = _mm_tn(a[None], dgu8, tk=D_MODEL, name=f"{tag}_dwin")
    dh, dg_pre = _norm_bwd(h, g_pre, da, 1.0, dh_new, F32, tm=ts, name=f"{tag}_dnorm")
    return dh, dg_pre, dg_post, d_win, d_wo.reshape(N_DEV, -1, D_MODEL)


def _blockdiag(w4):
    n, b, _ = w4.shape
    eye = jnp.eye(n, dtype=w4.dtype)
    return (eye[:, None, :, None] * w4[:, :, None, :]).reshape(n * b, n * b)


def _blockdiag_grad(d):
    n = d.shape[0] // HEAD
    x = d.reshape(n, HEAD, n, HEAD)
    return jnp.stack([x[i, :, i, :] for i in range(n)])


def _row(v):
    return v.reshape(1, -1)


def _mixer_fwd(h, g_pre, g_post, wi, wo, P, bd, tiles, tag):
    tb, ts = tiles
    a = _norm_fwd(h, g_pre, tm=ts, name=f"{tag}_norm")
    p = _mm(a[None], wi[None], trans_b=False, tm=tb, tn=N_IN, out_dtype=F32, name=f"{tag}_in")
    lx, lg = p[:, 0:256], p[:, 256:512]
    sb, scc, sx = p[:, 512:768], p[:, 768:1024], p[:, 1024:1280]
    z = p[:, 1280:]
    lxs = [lx, _shift(lx, 1), _shift(lx, 2), _shift(lx, 3)]
    cw = [_row(P['lru_conv_w'][kk]) for kk in range(4)]
    lru_par = cw + [_row(P['lru_conv_b']), _blockdiag(P['lru_wa']), _row(P['lru_ba']), _blockdiag(P['lru_wx']),
                    _row(P['lru_bx']), _row(P['lru_lambda'])]
    la, lb = _rowwise(_lru_pre, lxs, lru_par, [(LRU_W, F32)] * 2, [], tm=ts, name=f"{tag}_lru_pre")
    hs = _lru_scan(la, lb, name=f"{tag}_lru_scan")
    y_lru = _rowwise(lambda gg, hh, ng, b_: _lru_post(b_, gg, hh, ng), [lg, hs], [_row(P['lru_norm_g']), bd],
                     [(LRU_W, F32)], [], tm=ts, name=f"{tag}_lru_post")[0]
    sc_rows = [sb, scc, sx, _shift(scc, 1), _shift(sx, 1), _shift(scc, 2), _shift(sx, 2)]
    sc_par = [_row(P['sc_conv_w'][kk]) for kk in range(3)] + [_row(P['sc_norm_g'])]
    y_sc = _rowwise(lambda *v: _sc_fwd(v[-1], *v[:-1]), sc_rows, sc_par + [bd], [(SC_W, F32)], [], tm=ts,
                    name=f"{tag}_sc")[0]
    cuts = (0, RW_W, 2 * RW_W, 3 * RW_W, RW_IN)
    zs = [z[:, cuts[q]:cuts[q + 1]] for q in range(4)]
    z_rows = zs + [_shift(q, 1) for q in zs]
    pad = lambda m, lo: jnp.pad(m, ((lo, LANES - lo - m.shape[0]), (0, 0)))
    rw_par = [_row(P['rwkv_mu'][cuts[q]:cuts[q + 1]]) for q in range(4)]
    rw_par += [_row(P['rwkv_w0']), pad(P['rwkv_w2'], 0), _row(P['rwkv_a0']), pad(P['rwkv_a2'], 32),
               pad(P['rwkv_g2'], 64), _row(P['rwkv_k_k']), _row(P['rwkv_k_a'])]
    r, w, k2, c, b, v, g = _rowwise(lambda *vv: _rw_pre(vv[-1], *vv[:-1]), z_rows, rw_par + [bd], [(RW_W, F32)] * 7, [],
                                    tm=ts, name=f"{tag}_rw_pre")
    vb = _bcast_cols(v)
    spre, yt = _rw_scan(_shift(r, 1), w, k2, c, b, vb, _stacked_bf16(bd), name=f"{tag}_rw_scan")
    y = _unshift(_cols_to_rows(yt), 1)
    post_par =[_row(P['rwkv_lnx_w']), _row(P['rwkv_lnx_b']), _row(P['rwkv_r_k'])]
    y_rw = _rowwise(lambda *vv: _rw_post(vv[-1], *vv[:-1]), [y, r, k2, v, g], post_par + [bd], [(RW_W, F32)], [], tm=ts,
                    name=f"{tag}_rw_post")[0]
    ycat = jnp.concatenate([y_lru, y_sc, y_rw], axis=1).astype(BF16)
    m = _mm(ycat[None], wo[None], trans_b=False, tm=tb, tn=D_MODEL, out_dtype=F32, name=f"{tag}_out")
    h_new = _res_norm_fwd(h, m, g_post, 1.0, tm=ts, name=f"{tag}_res")
    res = dict(h=h, a=a, m=m, ycat=ycat, lxs=lxs, lru_par=lru_par, lg=lg, la=la, hs=hs, sc_rows=sc_rows, sc_par=sc_par,
               z_rows=z_rows, rw_par=rw_par, r=r, w=w, k2=k2, c=c, b=b, v=v, g=g, vb=vb, spre=spre, y=y, post_par=post_par)
    return h_new, res


def _mixer_bwd(dh_new, R, g_pre, g_post, wi, wo, P, bd, tiles, tag):
    tb, ts = tiles
    dm, dg_post = _norm_bwd(R['m'], g_post, dh_new, 1.0, None, BF16, tm=ts, name=f"{tag}_dres")
    dycat = _mm(dm[None], wo[None], trans_b=True, tm=tb, tn=D_MODEL, out_dtype=F32, name=f"{tag}_dycat")
    d_wo = _mm_tn(R['ycat'][None], dm[None], tk=D_MODEL // 2, name=f"{tag}_dwout")[0]
    dy_lru, dy_sc, dy_rw = dycat[:, 0:256], dycat[:, 256:512], dycat[:, 512:]
    G = {}
    d_lg, d_hs, G['lru_norm_g'] = _rowwise(
        lambda gg, hh, ct, ng, b_: _vjp_rows(_lru_post, 1, 3, 1)(b_, gg, hh, ng, ct),
        [R['lg'], R['hs'], dy_lru], [_row(P['lru_norm_g']), bd], [(LRU_W, F32)] * 2, [(1, LRU_W)], tm=ts,
        name=f"{tag}_lru_dpost")
    d_la, d_lb = _lru_scan_bwd(R['la'], R['hs'], d_hs, name=f"{tag}_lru_dscan")

    def lru_pre_bwd(x0, x1, x2, x3, ca, cb_, *par):
        return _vjp_rows(_lru_pre, 0, 14, 2)(x0, x1, x2, x3, *par, ca, cb_)

    par_shapes = [tuple(q.shape) for q in R['lru_par']]
    outs = _rowwise(lru_pre_bwd, R['lxs'] + [d_la, d_lb], R['lru_par'], [(LRU_W, F32)] * 4, par_shapes, tm=ts,
                    name=f"{tag}_lru_dpre")
    dxs, dpar = outs[:4], outs[4:]
    d_lx = _add_n([dxs[0], _unshift(dxs[1], 1), _unshift(dxs[2], 2), _unshift(dxs[3], 3)], tm=ts, name=f"{tag}_lru_dx")
    G['lru_conv_w'] = jnp.concatenate(dpar[0:4], axis=0)
    G['lru_conv_b'] = dpar[4][0]
    G['lru_wa'] = _blockdiag_grad(dpar[5])
    G['lru_ba'] = dpar[6][0]
    G['lru_wx'] = _blockdiag_grad(dpar[7])
    G['lru_bx'] = dpar[8][0]
    G['lru_lambda'] = dpar[9][0]
    G['lru_norm_g'] = G['lru_norm_g'][0]

    def sc_bwd(*vv):
        rows7, ct, par4, b_ = vv[:7], vv[7], vv[8:12], vv[12]
        return _vjp_rows(_sc_fwd, 1, 11, 1)(b_, *rows7, *par4, ct)

    outs = _rowwise(sc_bwd, R['sc_rows'] + [dy_sc], R['sc_par'] + [bd], [(SC_W, F32)] * 7, [(1, SC_W)] * 4, tm=ts,
                    name=f"{tag}_sc_bwd")
    d_sb = outs[0]
    d_sc = _add_n([outs[1], _unshift(outs[3], 1), _unshift(outs[5], 2)], tm=ts, name=f"{tag}_sc_dc")
    d_sx = _add_n([outs[2], _unshift(outs[4], 1), _unshift(outs[6], 2)], tm=ts, name=f"{tag}_sc_dx")
    G['sc_conv_w'] = jnp.concatenate(outs[7:10], axis=0)
    G['sc_norm_g'] = outs[10][0]

    def rw_post_bwd(*vv):
        rows5, ct, par3, b_ = vv[:5], vv[5], vv[6:9], vv[9]
        return _vjp_rows(_rw_post, 1, 8, 1)(b_, *rows5, *par3, ct)

    outs = _rowwise(rw_post_bwd, [R['y'], R['r'], R['k2'], R['v'], R['g'], dy_rw], R['post_par'] + [bd],
                    [(RW_W, F32)] * 5, [(1, RW_W)] * 3, tm=ts, name=f"{tag}_rw_dpost")
    d_y, dr_p, dk_p, dv_p, d_g = outs[:5]
    G['rwkv_lnx_w'], G['rwkv_lnx_b'], G['rwkv_r_k'] = outs[5][0], outs[6][0], outs[7][0]
    dyb = _bcast_cols(d_y)
    dr_s, d_w, dk_s, d_c, d_b, dvt = _rw_scan_bwd(R['r'], R['w'], R['k2'], R['c'], R['b'], R['vb'], dyb, R['spre'], _stacked_bf16(bd),
                                                 name=f"{tag}_rw_dscan")
    dv_s = _cols_to_rows(dvt)

    def rw_pre_bwd(*vv):
        zrows = vv[0:8]
        dr1, dr2, dw_, dk1, dk2_, dc_, db_, dv1, dv2, dg_ = vv[8:18]
        par, b_ = vv[18:29], vv[29]
        return _vjp_rows(_rw_pre, 1, 19, 7)(b_, *zrows, *par, dr1 + dr2, dw_, dk1 + dk2_, dc_, db_, dv1 + dv2, dg_)

    par_shapes = [tuple(q.shape) for q in R['rw_par']]
    widths = [(q.shape[1], F32) for q in R['z_rows']]
    outs = _rowwise(rw_pre_bwd, R['z_rows'] + [dr_p, dr_s, d_w, dk_p, dk_s, d_c, d_b, dv_p, dv_s, d_g],
                    R['rw_par'] + [bd], widths, par_shapes, tm=ts, name=f"{tag}_rw_dpre")
    d_z = _add_n([jnp.concatenate(outs[0:4], axis=1), _unshift(jnp.concatenate(outs[4:8], axis=1), 1)], tm=ts,
                 name=f"{tag}_rw_dz")
    dpar = outs[8:]
    G['rwkv_mu'] = jnp.concatenate([q[0] for q in dpar[0:4]])
    G['rwkv_w0'], G['rwkv_a0'] = dpar[4][0], dpar[6][0]
    G['rwkv_w2'], G['rwkv_a2'], G['rwkv_g2'] = dpar[5][0:32], dpar[7][32:64], dpar[8][64:128]
    G['rwkv_k_k'], G['rwkv_k_a'] = dpar[9][0], dpar[10][0]

    dp = jnp.concatenate([d_lx, d_lg, d_sb, d_sc, d_sx, d_z], axis=1).astype(BF16)
    da = _mm(dp[None], wi[None], trans_b=True, tm=tb, tn=D_MODEL, out_dtype=F32, name=f"{tag}_da")
    d_wi = _mm_tn(R['a'][None], dp[None], tk=D_MODEL // 2, name=f"{tag}_dwin")[0]
    dh, dg_pre = _norm_bwd(R['h'], g_pre, da, 1.0, dh_new, F32, tm=ts, name=f"{tag}_dnorm")
    return dh, dg_pre, dg_post, d_wi, d_wo, G


def _loss_rows(h, tgt, n_seq, *, tm, name):
    d = h.shape[1]

    def body(h_ref, t_ref, dh_ref, l_ref):
        i = pl.program_id(0)
        row = lax.broadcasted_iota(jnp.int32, (tm, 1), 0) + i * tm
        live = (row >= N_META) & (row < N_META + n_seq)
        e = jnp.where(live, h_ref[...] - t_ref[...], 0.0)
        dh_ref[...] = e * (1.0 / d)
        part = 0.5 * jnp.sum(jnp.sum(e * e, axis=1, keepdims=True) * (1.0 / d), axis=0, keepdims=True)

        @pl.when(i == 0)
        def _():
            l_ref[...] = part

        @pl.when(i > 0)
        def _():
            l_ref[...] += part

    blk = pl.BlockSpec((tm, d), lambda i: (i, 0))
    return pl.pallas_call(body, name=name, grid=(h.shape[0] // tm,), in_specs=[blk, blk],
                          out_specs=[blk, pl.BlockSpec((1, 1), lambda i: (0, 0))],
                          out_shape=[jax.ShapeDtypeStruct(h.shape, F32), jax.ShapeDtypeStruct((1, 1), F32)])(h, tgt)


def _pack(arrs, mult):
    flat = jnp.concatenate([a.reshape(-1).astype(F32) for a in arrs])
    n = flat.shape[0]
    tot = -(-n // mult) * mult
    return jnp.pad(flat, (0, tot - n)).reshape(-1, LANES)


def _unpack(buf, shapes):
    flat = buf.reshape(-1)
    out, off = [], 0
    for s in shapes:
        n = 1
        for q in s:
            n *= q
        out.append(flat[off:off + n].reshape(s))
        off += n
    return out


def _step(W, M, V, x, loss_target):
    n_seq = x.shape[1]
    t_real = N_META + n_seq
    t = (t_real // CHUNK + 1) * CHUNK
    tiles = (_pick(t, (704, 512, 256, 128, 64)), _pick(t, (192, 128, 64)))
    me = 4 * lax.axis_index("x") + 2 * lax.axis_index("y") + lax.axis_index("c")
    n_layer = W['norm_g'].shape[0]

    small_sh = list(SMALL_SHARDED)
    packed = _pack([W[n] for n in small_sh], 8 * LANES)
    gathered = _exchange([W[n].astype(BF16) for n in BIG] + [packed], gather=True, name="gather_weights")
    big8 = dict(zip(BIG, gathered[:-1]))
    pieces = [_unpack(gathered[-1][q], [W[n].shape for n in small_sh]) for q in range(N_DEV)]
    full = {n: W[n] for n in SMALL if n not in SMALL_SHARDED}
    for idx, n in enumerate(small_sh):
        full[n] = jnp.concatenate([pieces[q][idx] for q in range(N_DEV)], axis=SMALL_SHARDED[n])

    def layer_weights(l):
        w24_1 = big8['ffn1_w_in'][:, l].reshape(2, N_DEV // 2, D_MODEL, FFN_BLK)
        wo4_1 = big8['ffn1_w_out'][:, l].reshape(N_DEV // 2, FFN_BLK, D_MODEL)
        w24_2 = big8['ffn2_w_in'][:, l].reshape(2, N_DEV // 2, D_MODEL, FFN_BLK)
        wo4_2 = big8['ffn2_w_out'][:, l].reshape(N_DEV // 2, FFN_BLK, D_MODEL)
        wi = big8['mix_w_in'][:, l].transpose(1, 0, 2).reshape(D_MODEL, N_IN)
        wo = big8['mix_w_out'][:, l].reshape(D_MODEL, D_MODEL)
        return w24_1, wo4_1, w24_2, wo4_2, wi, wo

    bd = jnp.kron(jnp.eye(LANES // HEAD, dtype=F32), jnp.ones((HEAD, HEAD), F32))
    small_layer = [n for n in SMALL if n not in ('meta_tokens', 'norm_g')]

    h = jnp.concatenate([full['meta_tokens'], x[0], jnp.zeros((t - t_real, D_MODEL), F32)], axis=0)
    saved = []
    for l in range(n_layer):
        lw = layer_weights(l)
        ng = [_row(full['norm_g'][l, q]) for q in range(6)]
        P = {n: full[n][l] for n in small_layer}
        h, r1 = _ffn_fwd(h, ng[0], ng[1], lw[0], lw[1], tiles, f"l{l}_ffn1")
        h, r2 = _mixer_fwd(h, ng[2], ng[3], lw[4], lw[5], P, bd, tiles, f"l{l}_mix")
        h, r3 = _ffn_fwd(h, ng[4], ng[5], lw[2], lw[3], tiles, f"l{l}_ffn2")
        saved.append((lw, ng, P, r1, r2, r3))

    tgt = jnp.pad(loss_target[0], ((N_META, t - t_real), (0, 0)))
    dh, loss_part = _loss_rows(h, tgt, n_seq, tm=tiles[1], name="loss")
    loss = lax.psum(loss_part[0, 0], MESH_AXES)

    small_grads = [None] * n_layer
    norm_grads = [None] * n_layer
    recv = [None] * n_layer
    for l in reversed(range(n_layer)):
        lw, ng, P, r1, r2, r3 = saved[l]
        dh, g4, g5, d_win2, d_wo2 = _ffn_bwd(dh, r3, ng[4], ng[5], lw[2], lw[3], tiles, f"l{l}_ffn2")
        dh, g2, g3, d_wi, d_wo, G = _mixer_bwd(dh, r2, ng[2], ng[3], lw[4], lw[5], P, bd, tiles, f"l{l}_mix")
        dh, g0, g1, d_win1, d_wo1 = _ffn_bwd(dh, r1, ng[0], ng[1], lw[0], lw[1], tiles, f"l{l}_ffn1")
        small_grads[l] = G
        norm_grads[l] = jnp.concatenate([g0, g1, g2, g3, g4, g5], axis=0)
        d_wi8 = d_wi.reshape(D_MODEL, N_DEV, N_IN // N_DEV).transpose(1, 0, 2)
        d_wo8 = d_wo.reshape(N_DEV, D_MODEL // N_DEV, D_MODEL)
        recv[l] = _exchange([d_win1, d_wo1, d_win2, d_wo2, d_wi8, d_wo8], gather=False, name=f"l{l}_grad_exchange")

    gs = {n: jnp.stack([small_grads[l][n] for l in range(n_layer)]) for n in small_layer}
    gs['norm_g'] = jnp.stack(norm_grads)
    gs['meta_tokens'] = dh[:N_META]
    gpack = _pack([gs[n] for n in SMALL], 8 * LANES)
    gall = _exchange([gpack], gather=True, name="gather_small_grads")[0]
    gsum = _rowwise(lambda parts: _sum_slots(parts), [gall], [], [(LANES, F32)], [], tm=gall.shape[1],
                    name="sum_small_grads")[0]
    gfull = dict(zip(SMALL, _unpack(gsum, [gs[n].shape for n in SMALL])))

    def my_shard(n, a):
        if n not in SMALL_SHARDED:
            return a
        ax = SMALL_SHARDED[n]
        size = a.shape[ax] // N_DEV
        return lax.dynamic_slice_in_dim(a, me * size, size, axis=ax)

    g_loc = [my_shard(n, gfull[n]) for n in SMALL]
    shapes = [W[n].shape for n in SMALL]
    bufs = [_pack(g_loc, 8 * LANES)] + [_pack([D[n] for n in SMALL], 8 * LANES) for D in (W, M, V)]
    d_s, m_s, v_s = _rowwise(_adamw_rows, bufs, [], [(LANES, F32)] * 3, [], tm=bufs[0].shape[0], name="adamw_small")
    out = {'grad': dict(zip(SMALL, g_loc)), 'delta': dict(zip(SMALL, _unpack(d_s, shapes))),
           'm': dict(zip(SMALL, _unpack(m_s, shapes))), 'v': dict(zip(SMALL, _unpack(v_s, shapes)))}

    order = ['ffn1_w_in', 'ffn1_w_out', 'ffn2_w_in', 'ffn2_w_out', 'mix_w_in', 'mix_w_out']
    for idx, n in enumerate(order):
        per_layer = []
        for l in range(n_layer):
            parts = recv[l][idx]
            rows, cols = W[n].shape[1], W[n].shape[2]
            per_layer.append(_reduce_adamw(parts.reshape(N_DEV, rows, cols), W[n][l], M[n][l], V[n][l],
                                           name=f"l{l}_adamw_{n}"))
        for q, key in enumerate(('grad', 'delta', 'm', 'v')):
            out[key][n] = jnp.stack([per_layer[l][q] for l in range(n_layer)])

    return (loss, dh[N_META:t_real][None],
            *[out['grad'][n] for n in WEIGHTS], *[out['delta'][n] for n in WEIGHTS],
            *[out['m'][n] for n in WEIGHTS], *[out['v'][n] for n in WEIGHTS])


def kernel(x, meta_tokens, norm_g, ffn1_w_in, ffn1_w_out, ffn2_w_in, ffn2_w_out, mix_w_in, mix_w_out, lru_conv_w, lru_conv_b, lru_wa, lru_ba, lru_wx, lru_bx, lru_lambda, lru_norm_g, sc_conv_w, sc_norm_g, rwkv_mu, rwkv_w0, rwkv_w2, rwkv_a0, rwkv_a2, rwkv_g2, rwkv_k_k, rwkv_k_a, rwkv_r_k, rwkv_lnx_w, rwkv_lnx_b, loss_target, m_meta_tokens, m_norm_g, m_ffn1_w_in, m_ffn1_w_out, m_ffn2_w_in, m_ffn2_w_out, m_mix_w_in, m_mix_w_out, m_lru_conv_w, m_lru_conv_b, m_lru_wa, m_lru_ba, m_lru_wx, m_lru_bx, m_lru_lambda, m_lru_norm_g, m_sc_conv_w, m_sc_norm_g, m_rwkv_mu, m_rwkv_w0, m_rwkv_w2, m_rwkv_a0, m_rwkv_a2, m_rwkv_g2, m_rwkv_k_k, m_rwkv_k_a, m_rwkv_r_k, m_rwkv_lnx_w, m_rwkv_lnx_b, v_meta_tokens, v_norm_g, v_ffn1_w_in, v_ffn1_w_out, v_ffn2_w_in, v_ffn2_w_out, v_mix_w_in, v_mix_w_out, v_lru_conv_w, v_lru_conv_b, v_lru_wa, v_lru_ba, v_lru_wx, v_lru_bx, v_lru_lambda, v_lru_norm_g, v_sc_conv_w, v_sc_norm_g, v_rwkv_mu, v_rwkv_w0, v_rwkv_w2, v_rwkv_a0, v_rwkv_a2, v_rwkv_g2, v_rwkv_k_k, v_rwkv_k_a, v_rwkv_r_k, v_rwkv_lnx_w, v_rwkv_lnx_b):
    w_vals = (meta_tokens, norm_g, ffn1_w_in, ffn1_w_out, ffn2_w_in, ffn2_w_out, mix_w_in, mix_w_out, lru_conv_w, lru_conv_b, lru_wa, lru_ba, lru_wx, lru_bx, lru_lambda, lru_norm_g, sc_conv_w, sc_norm_g, rwkv_mu, rwkv_w0, rwkv_w2, rwkv_a0, rwkv_a2, rwkv_g2, rwkv_k_k, rwkv_k_a, rwkv_r_k, rwkv_lnx_w, rwkv_lnx_b)
    m_vals = (m_meta_tokens, m_norm_g, m_ffn1_w_in, m_ffn1_w_out, m_ffn2_w_in, m_ffn2_w_out, m_mix_w_in, m_mix_w_out, m_lru_conv_w, m_lru_conv_b, m_lru_wa, m_lru_ba, m_lru_wx, m_lru_bx, m_lru_lambda, m_lru_norm_g, m_sc_conv_w, m_sc_norm_g, m_rwkv_mu, m_rwkv_w0, m_rwkv_w2, m_rwkv_a0, m_rwkv_a2, m_rwkv_g2, m_rwkv_k_k, m_rwkv_k_a, m_rwkv_r_k, m_rwkv_lnx_w, m_rwkv_lnx_b)
    v_vals = (v_meta_tokens, v_norm_g, v_ffn1_w_in, v_ffn1_w_out, v_ffn2_w_in, v_ffn2_w_out, v_mix_w_in, v_mix_w_out, v_lru_conv_w, v_lru_conv_b, v_lru_wa, v_lru_ba, v_lru_wx, v_lru_bx, v_lru_lambda, v_lru_norm_g, v_sc_conv_w, v_sc_norm_g, v_rwkv_mu, v_rwkv_w0, v_rwkv_w2, v_rwkv_a0, v_rwkv_a2, v_rwkv_g2, v_rwkv_k_k, v_rwkv_k_a, v_rwkv_r_k, v_rwkv_lnx_w, v_rwkv_lnx_b)
    return _step(dict(zip(WEIGHTS, w_vals)), dict(zip(WEIGHTS, m_vals)), dict(zip(WEIGHTS, v_vals)), x, loss_target)
```

```python
import functools

import jax
import jax.numpy as jnp
from jax import lax
from jax.experimental import pallas as pl
from jax.experimental.pallas import tpu as pltpu

F32 = jnp.float32
BF16 = jnp.bfloat16
HIGHEST = lax.Precision.HIGHEST

N_DEV = 8
MESH_AXES = ("x", "y", "c")
N_META = 16
D_MODEL = 1024
LRU_W = 256
SC_W = 256
RW_W = 512
HEAD = 64
LANES = 128
CHUNK = 64
RW_IN = 1664
N_IN = 2944
FFN_BLK = 704
RMS_EPS = 1e-6
LNX_EPS = 64e-5
LRU_C = 8.0
ADAM_LR, ADAM_B1, ADAM_B2, ADAM_EPS, ADAM_WD, ADAM_STEP = 0.001, 0.9, 0.999, 1e-08, 0.01, 10

WEIGHTS = ['meta_tokens', 'norm_g', 'ffn1_w_in', 'ffn1_w_out', 'ffn2_w_in', 'ffn2_w_out', 'mix_w_in', 'mix_w_out',
           'lru_conv_w', 'lru_conv_b', 'lru_wa', 'lru_ba', 'lru_wx', 'lru_bx', 'lru_lambda', 'lru_norm_g',
           'sc_conv_w', 'sc_norm_g', 'rwkv_mu', 'rwkv_w0', 'rwkv_w2', 'rwkv_a0', 'rwkv_a2', 'rwkv_g2', 'rwkv_k_k',
           'rwkv_k_a', 'rwkv_r_k', 'rwkv_lnx_w', 'rwkv_lnx_b']
BIG = ['ffn1_w_in', 'ffn1_w_out', 'ffn2_w_in', 'ffn2_w_out', 'mix_w_in', 'mix_w_out']
SMALL_SHARDED = {'meta_tokens': 1, 'norm_g': 2, 'lru_conv_w': 2, 'sc_conv_w': 2, 'rwkv_w2': 2, 'rwkv_a2': 2, 'rwkv_g2': 2}
SMALL = [n for n in WEIGHTS if n not in BIG]


def _pick(n, cands):
    for c in cands:
        if n % c == 0:
            return c
    raise ValueError(f"no tile for {n}")


def _rowwise(fn, rows, params, row_outs, acc_outs, *, tm, name):
    nr, npar, nro, nao = len(rows), len(params), len(row_outs), len(acc_outs)
    n_rows = rows[0].shape[-2]
    assert n_rows % tm == 0, (name, n_rows, tm)

    def body(*refs):
        vals = [r[...] for r in refs[:nr + npar]]
        outs = fn(*vals)
        if not isinstance(outs, (tuple, list)):
            outs = (outs,)
        assert len(outs) == nro + nao, (name, len(outs))
        for o_ref, o in zip(refs[nr + npar:nr + npar + nro], outs[:nro]):
            o_ref[...] = o.astype(o_ref.dtype)
        step = pl.program_id(0)
        for a_ref, a in zip(refs[nr + npar + nro:], outs[nro:]):
            @pl.when(step == 0)
            def _(a_ref=a_ref, a=a):
                a_ref[...] = a.astype(F32)

            @pl.when(step > 0)
            def _(a_ref=a_ref, a=a):
                a_ref[...] += a.astype(F32)

    def row_spec(shape):
        if len(shape) == 2:
            return pl.BlockSpec((tm, shape[1]), lambda i: (i, 0))
        return pl.BlockSpec((shape[0], tm, shape[2]), lambda i: (0, i, 0))

    def full_spec(shape):
        nd = len(shape)
        return pl.BlockSpec(tuple(shape), lambda i, nd=nd: (0,) * nd)

    in_specs = [row_spec(r.shape) for r in rows] + [full_spec(p.shape) for p in params]
    out_shape = [jax.ShapeDtypeStruct((n_rows, w), dt) for (w, dt) in row_outs]
    out_shape += [jax.ShapeDtypeStruct(tuple(s), F32) for s in acc_outs]
    out_specs = [row_spec((n_rows, w)) for (w, _) in row_outs] + [full_spec(s) for s in acc_outs]
    res = pl.pallas_call(body, name=name, grid=(n_rows // tm,), in_specs=in_specs, out_specs=out_specs,
                         out_shape=out_shape)(*rows, *params)
    return tuple(res)


def _mm(a3, b3, *, trans_b, tm, tn, out_dtype, name):
    nj, m, kb = a3.shape
    n = b3.shape[1] if trans_b else b3.shape[2]
    dims = (((1,), (1,)), ((), ())) if trans_b else (((1,), (0,)), ((), ()))

    def body(a_ref, b_ref, o_ref, acc_ref):
        j = pl.program_id(2)

        @pl.when(j == 0)
        def _():
            acc_ref[...] = jnp.zeros_like(acc_ref)

        acc_ref[...] += lax.dot_general(a_ref[0], b_ref[0], dims, preferred_element_type=F32)

        @pl.when(j == nj - 1)
        def _():
            o_ref[...] = acc_ref[...].astype(o_ref.dtype)

    if trans_b:
        b_spec = pl.BlockSpec((1, tn, kb), lambda i, c, j: (j, c, 0))
    else:
        b_spec = pl.BlockSpec((1, kb, tn), lambda i, c, j: (j, 0, c))
    return pl.pallas_call(
        body, name=name, grid=(m // tm, n // tn, nj),
        in_specs=[pl.BlockSpec((1, tm, kb), lambda i, c, j: (j, i, 0)), b_spec],
        out_specs=pl.BlockSpec((tm, tn), lambda i, c, j: (i, c)),
        out_shape=jax.ShapeDtypeStruct((m, n), out_dtype),
        scratch_shapes=[pltpu.VMEM((tm, tn), F32)],
    )(a3, b3)


def _mm_tn(a3, b3, *, tk, name):
    ja, t, ka = a3.shape
    jb, _, n = b3.shape
    nj = max(ja, jb)

    def body(a_ref, b_ref, o_ref):
        o_ref[0] = lax.dot_general(a_ref[0], b_ref[0], (((0,), (0,)), ((), ())),
                                   preferred_element_type=F32).astype(o_ref.dtype)

    return pl.pallas_call(
        body, name=name, grid=(nj, ka // tk),
        in_specs=[pl.BlockSpec((1, t, tk), (lambda j, c: (j, 0, c)) if ja > 1 else (lambda j, c: (0, 0, c))),
                  pl.BlockSpec((1, t, n), (lambda j, c: (j, 0, 0)) if jb > 1 else (lambda j, c: (0, 0, 0)))],
        out_specs=pl.BlockSpec((1, tk, n), lambda j, c: (j, c, 0)),
        out_shape=jax.ShapeDtypeStruct((nj, ka, n), BF16),
    )(a3, b3)


def _ffn_in(a, w24, *, tm, name):
    t, d = a.shape
    nb, fb = w24.shape[1], w24.shape[3]

    def body(a_ref, w_ref, gu_ref, s_ref):
        x = a_ref[...]
        g = jnp.dot(x, w_ref[0, 0], preferred_element_type=F32)
        u = jnp.dot(x, w_ref[1, 0], preferred_element_type=F32)
        gu_ref[0, 0] = g.astype(BF16)
        gu_ref[1, 0] = u.astype(BF16)
        s_ref[0] = (g * jax.nn.sigmoid(g) * u).astype(BF16)

    return pl.pallas_call(
        body, name=name, grid=(nb, t // tm),
        in_specs=[pl.BlockSpec((tm, d), lambda j, i: (i, 0)), pl.BlockSpec((2, 1, d, fb), lambda j, i: (0, j, 0, 0))],
        out_specs=[pl.BlockSpec((2, 1, tm, fb), lambda j, i: (0, j, i, 0)), pl.BlockSpec((1, tm, fb), lambda j, i: (j, i, 0))],
        out_shape=[jax.ShapeDtypeStruct((2, nb, t, fb), BF16), jax.ShapeDtypeStruct((nb, t, fb), BF16)],
    )(a, w24)


def _ffn_dswiglu(df, wo4, gu, *, tm, name):
    t, d = df.shape
    nb, fb = wo4.shape[0], wo4.shape[1]

    def body(df_ref, wo_ref, gu_ref, dg_ref):
        ds = lax.dot_general(df_ref[...], wo_ref[0], (((1,), (1,)), ((), ())), preferred_element_type=F32)
        g = gu_ref[0, 0].astype(F32)
        u = gu_ref[1, 0].astype(F32)
        sig = jax.nn.sigmoid(g)
        dg_ref[0, 0] = (ds * u * sig * (1.0 + g * (1.0 - sig))).astype(BF16)
        dg_ref[1, 0] = (ds * g * sig).astype(BF16)

    return pl.pallas_call(
        body, name=name, grid=(nb, t // tm),
        in_specs=[pl.BlockSpec((tm, d), lambda j, i: (i, 0)), pl.BlockSpec((1, fb, d), lambda j, i: (j, 0, 0)),
                  pl.BlockSpec((2, 1, tm, fb), lambda j, i: (0, j, i, 0))],
        out_specs=pl.BlockSpec((2, 1, tm, fb), lambda j, i: (0, j, i, 0)),
        out_shape=jax.ShapeDtypeStruct((2, nb, t, fb), BF16),
    )(df, wo4, gu)


def _rms(x, g):
    return x * lax.rsqrt(jnp.mean(x * x, axis=-1, keepdims=True) + RMS_EPS) * g


def _rms_bwd(x, g, dy):
    rstd = lax.rsqrt(jnp.mean(x * x, axis=-1, keepdims=True) + RMS_EPS)
    xh = x * rstd
    dxh = dy * g
    dx = rstd * (dxh - xh * jnp.mean(dxh * xh, axis=-1, keepdims=True))
    return dx, jnp.sum(dy * xh, axis=0, keepdims=True)


def _seg_sum_impl(x, bd):
    parts = [jnp.dot(x[:, q * LANES:(q + 1) * LANES], bd, preferred_element_type=F32, precision=HIGHEST)
             for q in range(x.shape[1] // LANES)]
    return parts[0] if len(parts) == 1 else jnp.concatenate(parts, axis=1)


@jax.custom_vjp
def _seg_sum(x, bd):
    return _seg_sum_impl(x, bd)


def _seg_sum_fwd(x, bd):
    return _seg_sum_impl(x, bd), bd


def _seg_sum_bwd(bd, ct):
    return _seg_sum_impl(ct, bd), jnp.zeros_like(bd)


_seg_sum.defvjp(_seg_sum_fwd, _seg_sum_bwd)


def _group_rms(y, g, bd):
    return y * lax.rsqrt(_seg_sum(y * y, bd) * (1.0 / HEAD) + RMS_EPS) * g


def _expm1(x):
    return jnp.where(jnp.abs(x) < 1e-2, x * (1.0 + x * (0.5 + x * (1.0 / 6.0))), jnp.exp(x) - 1.0)


def _lru_pre(x0, x1, x2, x3, cw0, cw1, cw2, cw3, cb, wa, ba, wx, bx, lam):
    u = x3 * cw0 + x2 * cw1 + x1 * cw2 + x0 * cw3 + cb
    r = jax.nn.sigmoid(jnp.dot(u, wa, preferred_element_type=F32, precision=HIGHEST) + ba)
    i = jax.nn.sigmoid(jnp.dot(u, wx, preferred_element_type=F32, precision=HIGHEST) + bx)
    log_a = -LRU_C * r * jax.nn.softplus(-lam)
    return jnp.exp(log_a), jnp.sqrt(-_expm1(2.0 * log_a)) * (i * u)


def _lru_post(bd, gate, hs, ng):
    return _group_rms(jax.nn.gelu(gate) * hs, ng, bd)


def _sc_fwd(bd, b, c0, x0, c1, x1, c2, x2, w0, w1, w2, ng):
    return _group_rms(b * (w0 * (c2 * x2) + w1 * (c1 * x1) + w2 * (c0 * x0)), ng, bd)


def _rw_pre(bd, zr, zk, zv, zt, sr, sk, sv, st, mur, muk, muv, mut, w0, w2p, a0, a2p, g2p, k_k, k_a):
    r, k, v, tail = zr + (sr - zr) * mur, zk + (sk - zk) * muk, zv + (sv - zv) * muv, zt + (st - zt) * mut
    lane = lax.broadcasted_iota(jnp.int32, tail.shape, 1)
    act = jnp.where(lane < 32, jnp.tanh(tail), jnp.where(lane < 64, tail, jax.nn.sigmoid(tail)))
    dot = functools.partial(jnp.dot, preferred_element_type=F32, precision=HIGHEST)
    w_log = -jax.nn.softplus(-(w0 + dot(act, w2p))) - 0.5
    w = jnp.exp(-jnp.exp(w_log))
    a = jax.nn.sigmoid(a0 + dot(act, a2p))
    g = dot(act, g2p)
    kk = k * k_k
    k2 = k * (1.0 + (a - 1.0) * k_a)
    kkn = kk * lax.rsqrt(jnp.maximum(_seg_sum(kk * kk, bd), 1e-24))
    return r, w, k2, -kkn, kkn * a, v, g


def _rw_post(bd, y, r, k2, v, g, lnw, lnb, r_k):
    mean = _seg_sum(y, bd) * (1.0 / HEAD)
    yc = y - mean
    var = _seg_sum(yc * yc, bd) * (1.0 / HEAD)
    yn = yc * lax.rsqrt(var + LNX_EPS) * lnw + lnb
    return (yn + _seg_sum(r * k2 * r_k, bd) * v) * g


def _vjp_rows(fwd, n_static, n_in, n_ct):
    def fn(*args):
        static, prim, cts = args[:n_static], args[n_static:n_static + n_in], args[n_static + n_in:]
        assert len(cts) == n_ct
        _, vjp = jax.vjp(functools.partial(fwd, *static), *prim)
        return vjp(cts[0] if n_ct == 1 else tuple(cts))
    return fn


def _exchange_copies(x_refs, o_refs, sems, gather):
    send_sems, recv_sems, local_sems = sems
    mx, my, mc = lax.axis_index("x"), lax.axis_index("y"), lax.axis_index("c")
    me = 4 * mx + 2 * my + mc
    local, sends, recvs = [], [], []
    for k in range(len(x_refs)):
        local.append(pltpu.make_async_copy(x_refs[k] if gather else x_refs[k].at[me], o_refs[k].at[me], local_sems.at[k]))
    for d in range(1, N_DEV):
        px, py, pc = mx ^ ((d >> 2) & 1), my ^ ((d >> 1) & 1), mc ^ (d & 1)
        peer = 4 * px + 2 * py + pc
        for k in range(len(x_refs)):
            src = x_refs[k] if gather else x_refs[k].at[peer]
            common = dict(src_ref=src, send_sem=send_sems.at[k, d - 1], recv_sem=recv_sems.at[k, d - 1],
                          device_id=(px, py, pc), device_id_type=pl.DeviceIdType.MESH)
            sends.append(pltpu.make_async_remote_copy(dst_ref=o_refs[k].at[me], **common))
            recvs.append(pltpu.make_async_remote_copy(dst_ref=o_refs[k].at[peer], **common))
    return local, sends, recvs


def _exchange_start(x_refs, o_refs, sems, gather):
    local, sends, _ = _exchange_copies(x_refs, o_refs, sems, gather)
    for cp in local + sends:
        cp.start()


def _exchange_wait(x_refs, o_refs, sems, gather):
    local, sends, recvs = _exchange_copies(x_refs, o_refs, sems, gather)
    for cp in sends:
        cp.wait_send()
    for cp in recvs:
        cp.wait_recv()
    for cp in local:
        cp.wait()


def _exchange_out_shape(xs, gather):
    return [jax.ShapeDtypeStruct(((N_DEV,) + x.shape) if gather else x.shape, x.dtype) for x in xs]


def _exchange_sems(n):
    return [pltpu.SemaphoreType.DMA((n, N_DEV - 1)), pltpu.SemaphoreType.DMA((n, N_DEV - 1)), pltpu.SemaphoreType.DMA((n,))]


SUBLANES = 8


def _store_row(ref, i, cols, row):
    base = pl.multiple_of((i // SUBLANES) * SUBLANES, SUBLANES)
    sub = lax.broadcasted_iota(jnp.int32, (SUBLANES, row.shape[1]), 0)
    ref[pl.ds(base, SUBLANES), cols] = jnp.where(sub == i % SUBLANES, row, ref[pl.ds(base, SUBLANES), cols])


def _lru_scan(a, b, name):
    t, w = a.shape

    def body(a_ref, b_ref, h_ref):
        h_ref[...] = jnp.zeros_like(h_ref)

        def step(i, h):
            h = a_ref[pl.ds(i, 1), :] * h + b_ref[pl.ds(i, 1), :]
            _store_row(h_ref, i, slice(None), h)
            return h
        lax.fori_loop(0, t, step, jnp.zeros((1, w), F32), unroll=8)

    return pl.pallas_call(body, name=name, out_shape=jax.ShapeDtypeStruct((t, w), F32))(a, b)


def _lru_scan_bwd(a, hs, dhs, name):
    t, w = a.shape

    def body(a_ref, h_ref, dh_ref, da_ref, db_ref):
        da_ref[...] = jnp.zeros_like(da_ref)
        db_ref[...] = jnp.zeros_like(db_ref)

        def step(n, carry):
            i = t - 1 - n
            lam = dh_ref[pl.ds(i, 1), :] + carry
            _store_row(db_ref, i, slice(None), lam)
            hprev = h_ref[pl.ds(jnp.maximum(i - 1, 0), 1), :]
            _store_row(da_ref, i, slice(None), jnp.where(i > 0, lam * hprev, 0.0))
            return a_ref[pl.ds(i, 1), :] * lam
        lax.fori_loop(0, t, step, jnp.zeros((1, w), F32), unroll=8)

    return pl.pallas_call(body, name=name, out_shape=[jax.ShapeDtypeStruct((t, w), F32)] * 2)(a, hs, dhs)


N_PAIR = RW_W // LANES
UNROLL = 2


def _bcast_cols(v):
    t = v.shape[0]
    x = v.reshape(t, N_PAIR, 2, HEAD)
    shape = (t, N_PAIR, HEAD, LANES)
    lane = lax.broadcasted_iota(jnp.int32, shape, 3)
    return jnp.where(lane < HEAD, jnp.broadcast_to(x[:, :, 0, :, None], shape), jnp.broadcast_to(x[:, :, 1, :, None], shape))


def _cols_to_rows(yt):
    nch = yt.shape[2] // LANES
    x = yt.reshape(N_PAIR, HEAD, nch, 2, CHUNK).transpose(2, 4, 0, 3, 1)
    return x.reshape(nch * CHUNK, RW_W)


def _stacked_bf16(bd):
    return jnp.concatenate([bd, bd], axis=0).astype(BF16)


def _group_sums(prods, bd2):
    x = jnp.concatenate(prods, axis=0)
    hi = x.astype(BF16)
    lo = (x - hi.astype(F32)).astype(BF16)
    return jnp.dot(jnp.concatenate([hi, lo], axis=1), bd2, preferred_element_type=F32)


def _rw_scan(r, r_prev, w, k, c, c_next, b, vb, bd2, name, ride=None):
    t = w.shape[0]
    nch = t // CHUNK
    ride_xs, ride_gather = ride if ride is not None else ([], False)
    n_ride = len(ride_xs)

    def body(*refs):
        r_ref, rp_ref, w_ref, k_ref, c_ref, cn_ref, b_ref, vb_ref, bd_ref = refs[:9]
        x_refs = refs[9:9 + n_ride]
        spre_ref, yt_ref = refs[9 + n_ride:11 + n_ride]
        o_refs = refs[11 + n_ride:11 + 2 * n_ride]
        s_ref, wc_ref, wr_ref, bc_ref, kc_ref, br_ref, kr_ref = refs[11 + 2 * n_ride:18 + 2 * n_ride]
        sems = refs[18 + 2 * n_ride:]

        @pl.when(pl.program_id(0) == 0)
        def _():
            s_ref[...] = jnp.zeros_like(s_ref)
            if n_ride:
                _exchange_start(x_refs, o_refs, sems, ride_gather)

        yt_ref[...] = jnp.zeros_like(yt_ref)
        bdv = bd_ref[...]
        lane = lax.broadcasted_iota(jnp.int32, (HEAD, LANES), 1) % CHUNK

        wv, cn, rv, bv, kv = w_ref[...], cn_ref[...], r_ref[...], b_ref[...], k_ref[...]
        wc_ref[...] = wv * cn
        wr_ref[...] = wv * rv
        for ref, x in ((bc_ref, bv * cn), (kc_ref, kv * cn), (br_ref, bv * rv), (kr_ref, kv * rv)):
            sums = _group_sums([x[:, q * LANES:(q + 1) * LANES] for q in range(N_PAIR)], bdv)
            for q in range(N_PAIR):
                ref[:, q * LANES:(q + 1) * LANES] = sums[q * CHUNK:(q + 1) * CHUNK]

        def cut(ref, i):
            x = ref[pl.ds(i, 1), :]
            return [x[:, p * LANES:(p + 1) * LANES] for p in range(N_PAIR)]

        def two_steps(j, st):
            i0 = 2 * j
            i1 = i0 + 1
            c0, wc0, rp0, wr0 = cut(c_ref, i0), cut(wc_ref, i0), cut(rp_ref, i0), cut(wr_ref, i0)
            w0, b0, k0, w1, b1, k1 = cut(w_ref, i0), cut(b_ref, i0), cut(k_ref, i0), cut(w_ref, i1), cut(b_ref, i1), cut(k_ref, i1)
            bc0, kc0, br0, kr0 = cut(bc_ref, i0), cut(kc_ref, i0), cut(br_ref, i0), cut(kr_ref, i0)
            pairs = range(N_PAIR)
            red = _group_sums([st[p] * c0[p] for p in pairs] + [st[p] * wc0[p] for p in pairs], bdv)
            out = _group_sums([st[p] * rp0[p] for p in pairs] + [st[p] * wr0[p] for p in pairs], bdv)
            new = []
            for p in pairs:
                v0, v1 = vb_ref[i0, p], vb_ref[i1, p]
                sa0 = red[p * HEAD:(p + 1) * HEAD]
                sa1 = red[(N_PAIR + p) * HEAD:(N_PAIR + p + 1) * HEAD] + sa0 * bc0[p] + v0 * kc0[p]
                y_before = out[p * HEAD:(p + 1) * HEAD]
                y0 = out[(N_PAIR + p) * HEAD:(N_PAIR + p + 1) * HEAD] + sa0 * br0[p] + v0 * kr0[p]
                spre_ref[i0, p] = st[p]
                s1 = st[p] * w0[p] + sa0 * b0[p] + v0 * k0[p]
                spre_ref[i1, p] = s1
                new.append(s1 * w1[p] + sa1 * b1[p] + v1 * k1[p])
                yt_ref[p] = jnp.where(lane == i0, y_before, jnp.where(lane == i1, y0, yt_ref[p]))
            return tuple(new)

        st = lax.fori_loop(0, CHUNK // 2, two_steps, tuple(s_ref[p] for p in range(N_PAIR)))
        for p in range(N_PAIR):
            s_ref[p] = st[p]

        if n_ride:
            @pl.when(pl.program_id(0) == nch - 1)
            def _():
                _exchange_wait(x_refs, o_refs, sems, ride_gather)

    row = pl.BlockSpec((CHUNK, RW_W), lambda i: (i, 0))
    big = pl.BlockSpec((CHUNK, N_PAIR, HEAD, LANES), lambda i: (i, 0, 0, 0))
    any_spec = pl.BlockSpec(memory_space=pl.ANY)
    return pl.pallas_call(
        body, name=name, grid=(nch,),
        in_specs=[row] * 7 + [big, pl.BlockSpec((2 * LANES, LANES), lambda i: (0, 0))] + [any_spec] * n_ride,
        out_specs=[big, pl.BlockSpec((N_PAIR, HEAD, LANES), lambda i: (0, 0, i))] + [any_spec] * n_ride,
        out_shape=[jax.ShapeDtypeStruct((t, N_PAIR, HEAD, LANES), F32), jax.ShapeDtypeStruct((N_PAIR, HEAD, nch * LANES), F32)]
        + _exchange_out_shape(ride_xs, ride_gather),
        scratch_shapes=[pltpu.VMEM((N_PAIR, HEAD, LANES), F32)] + [pltpu.VMEM((CHUNK, RW_W), F32)] * 6
        + (_exchange_sems(n_ride) if n_ride else []),
    )(r, r_prev, w, k, c, c_next, b, vb, bd2, *ride_xs)


def _rw_scan_bwd(r, w, k, c, b, vb, dyb, spre, bd, name, ride=None):
    t = r.shape[0]
    nch = t // CHUNK
    ride_xs, ride_gather = ride if ride is not None else ([], False)
    n_ride = len(ride_xs)

    def body(*refs):
        r_ref, w_ref, k_ref, c_ref, b_ref, vb_ref, dyb_ref, spre_ref, bd_ref = refs[:9]
        x_refs = refs[9:9 + n_ride]
        dr_ref, dw_ref, dk_ref, dc_ref, db_ref, dvt_ref = refs[9 + n_ride:15 + n_ride]
        o_refs = refs[15 + n_ride:15 + 2 * n_ride]
        g_ref, snext_ref = refs[15 + 2 * n_ride:17 + 2 * n_ride]
        sems = refs[17 + 2 * n_ride:]

        @pl.when(pl.program_id(0) == 0)
        def _():
            g_ref[...] = jnp.zeros_like(g_ref)
            snext_ref[...] = jnp.zeros_like(snext_ref)
            if n_ride:
                _exchange_start(x_refs, o_refs, sems, ride_gather)

        for ref in (dr_ref, dw_ref, dk_ref, dc_ref, db_ref, dvt_ref):
            ref[...] = jnp.zeros_like(ref)
        bdv = bd_ref[...]
        lane = lax.broadcasted_iota(jnp.int32, (HEAD, LANES), 1) % CHUNK

        def sum0(x):
            return jnp.sum(x, axis=0, keepdims=True)

        def step(n, gs):
            i = CHUNK - 1 - n
            rows = [ref[pl.ds(i, 1), :] for ref in (r_ref, w_ref, k_ref, c_ref, b_ref)]
            rr, ww, kk, cc, bb = [[x[:, p * LANES:(p + 1) * LANES] for p in range(N_PAIR)] for x in rows]
            sp = [spre_ref[i, p] for p in range(N_PAIR)]
            dy = [dyb_ref[i, p] for p in range(N_PAIR)]
            gs = [gs[p] + dy[p] * rr[p] for p in range(N_PAIR)]
            dsa_all = _group_sums([gs[p] * bb[p] for p in range(N_PAIR)], bdv)
            red = _group_sums([sp[p] * cc[p] for p in range(N_PAIR)] + [gs[p] * kk[p] for p in range(N_PAIR)], bdv)
            new = []
            for p in range(N_PAIR):
                dsa = dsa_all[p * HEAD:(p + 1) * HEAD]
                sa = red[p * HEAD:(p + 1) * HEAD]
                dv = red[(N_PAIR + p) * HEAD:(N_PAIR + p + 1) * HEAD]
                cols = pl.ds(p * LANES, LANES)
                _store_row(dr_ref, i, cols, sum0(snext_ref[p] * dy[p]))
                _store_row(dw_ref, i, cols, sum0(gs[p] * sp[p]))
                _store_row(db_ref, i, cols, sum0(gs[p] * sa))
                _store_row(dk_ref, i, cols, sum0(gs[p] * vb_ref[i, p]))
                _store_row(dc_ref, i, cols, sum0(sp[p] * dsa))
                dvt_ref[p] = jnp.where(lane == i, dv, dvt_ref[p])
                snext_ref[p] = sp[p]
                new.append(gs[p] * ww[p] + dsa * cc[p])
            return tuple(new)

        def steps(j, gs):
            for u in range(UNROLL):
                gs = step(j * UNROLL + u, gs)
            return gs

        gs = lax.fori_loop(0, CHUNK // UNROLL, steps, tuple(g_ref[p] for p in range(N_PAIR)))
        for p in range(N_PAIR):
            g_ref[p] = gs[p]

        if n_ride:
            @pl.when(pl.program_id(0) == nch - 1)
            def _():
                _exchange_wait(x_refs, o_refs, sems, ride_gather)

    row = pl.BlockSpec((CHUNK, RW_W), lambda i: (nch - 1 - i, 0))
    big = pl.BlockSpec((CHUNK, N_PAIR, HEAD, LANES), lambda i: (nch - 1 - i, 0, 0, 0))
    any_spec = pl.BlockSpec(memory_space=pl.ANY)
    return pl.pallas_call(
        body, name=name, grid=(nch,),
        in_specs=[row] * 5 + [big, big, big, pl.BlockSpec((2 * LANES, LANES), lambda i: (0, 0))] + [any_spec] * n_ride,
        out_specs=[row] * 5 + [pl.BlockSpec((N_PAIR, HEAD, LANES), lambda i: (0, 0, nch - 1 - i))] + [any_spec] * n_ride,
        out_shape=[jax.ShapeDtypeStruct((t, RW_W), F32)] * 5 + [jax.ShapeDtypeStruct((N_PAIR, HEAD, nch * LANES), F32)]
        + _exchange_out_shape(ride_xs, ride_gather),
        scratch_shapes=[pltpu.VMEM((N_PAIR, HEAD, LANES), F32), pltpu.VMEM((N_PAIR, HEAD, LANES), F32)]
        + (_exchange_sems(n_ride) if n_ride else []),
    )(r, w, k, c, b, vb, dyb, spre, bd, *ride_xs)


def _exchange(xs, *, gather, name):
    n = len(xs)

    def body(*refs):
        _exchange_start(refs[:n], refs[n:2 * n], refs[2 * n:], gather)
        _exchange_wait(refs[:n], refs[n:2 * n], refs[2 * n:], gather)

    any_spec = pl.BlockSpec(memory_space=pl.ANY)
    return pl.pallas_call(
        body, name=name, in_specs=[any_spec] * n, out_specs=[any_spec] * n, out_shape=_exchange_out_shape(xs, gather),
        scratch_shapes=_exchange_sems(n),
    )(*xs)


def _adamw_rows(g, w, m, v):
    m = ADAM_B1 * m + (1.0 - ADAM_B1) * g
    v = ADAM_B2 * v + (1.0 - ADAM_B2) * (g * g)
    m_hat = m / (1.0 - ADAM_B1 ** ADAM_STEP)
    v_hat = v / (1.0 - ADAM_B2 ** ADAM_STEP)
    return -ADAM_LR * (m_hat / (jnp.sqrt(v_hat) + ADAM_EPS) + ADAM_WD * w), m, v


def _sum_slots(parts):
    g = parts[0].astype(F32)
    for q in range(1, N_DEV):
        g = g + parts[q].astype(F32)
    return g


def _reduce_adamw(parts, w, m, v, name):
    rows, cols = w.shape

    def fn(parts, w, m, v):
        g = _sum_slots(parts)
        return (g,) + _adamw_rows(g, w, m, v)

    return _rowwise(fn, [parts, w, m, v], [], [(cols, F32)] * 4, [], tm=_pick(rows, (256, 128, 64, 32, 16, 8)), name=name)


def _shift(x, n):
    return jnp.pad(x, ((n, 0), (0, 0)))[:-n]


def _unshift(x, n):
    return jnp.pad(x, ((0, n), (0, 0)))[n:]


def _add_n(xs, *, tm, name):
    def fn(*vals):
        s = vals[0]
        for x in vals[1:]:
            s = s + x
        return s
    return _rowwise(fn, xs, [], [(xs[0].shape[1], F32)], [], tm=tm, name=name)[0]


def _norm_fwd(h, g, *, tm, name):
    return _rowwise(lambda x, gg: _rms(x, gg), [h], [g], [(h.shape[1], BF16)], [], tm=tm, name=name)[0]


def _res_norm_fwd(h, f, g, scale, *, tm, name):
    return _rowwise(lambda hh, ff, gg: hh + scale * _rms(ff, gg), [h, f], [g], [(h.shape[1], F32)], [], tm=tm, name=name)[0]


def _norm_bwd(x, g, dy, scale, res, out_dtype, *, tm, name):
    if res is None:
        def fn(xx, dd, gg):
            dx, dg = _rms_bwd(xx, gg, dd * scale)
            return dx, dg
        rows = [x, dy]
    else:
        def fn(xx, dd, rr, gg):
            dx, dg = _rms_bwd(xx, gg, dd * scale)
            return dx + rr, dg
        rows = [x, dy, res]
    return _rowwise(fn, rows, [g], [(x.shape[1], out_dtype)], [(1, x.shape[1])], tm=tm, name=name)


def _ffn_fwd(h, g_pre, g_post, w24, wo4, tiles, tag):
    tb, ts = tiles
    a = _norm_fwd(h, g_pre, tm=ts, name=f"{tag}_norm")
    gu, s4 = _ffn_in(a, w24, tm=tb, name=f"{tag}_in")
    f = _mm(s4, wo4, trans_b=False, tm=tb, tn=D_MODEL, out_dtype=F32, name=f"{tag}_out")
    h_new = _res_norm_fwd(h, f, g_post, 0.5, tm=ts, name=f"{tag}_res")
    return h_new, (h, a, gu, s4, f)


def _ffn_bwd(dh_new, res, g_pre, g_post, w24, wo4, tiles, tag):
    tb, ts = tiles
    h, a, gu, s4, f = res
    t = h.shape[0]
    df, dg_post = _norm_bwd(f, g_post, dh_new, 0.5, None, BF16, tm=ts, name=f"{tag}_dres")
    dgu = _ffn_dswiglu(df, wo4, gu, tm=tb, name=f"{tag}_dswiglu")
    d_wo = _mm_tn(s4, df[None], tk=FFN_BLK, name=f"{tag}_dwout")
    dgu8 = dgu.reshape(2 * w24.shape[1], t, FFN_BLK)
    w8 = w24.reshape(2 * w24.shape[1], D_MODEL, FFN_BLK)
    da = _mm(dgu8, w8, trans_b=True, tm=tb, tn=D_MODEL, out_dtype=F32, name=f"{tag}_da")
    d_win = _mm_tn(a[None], dgu8, tk=D_MODEL, name=f"{tag}_dwin")
    dh, dg_pre = _norm_bwd(h, g_pre, da, 1.0, dh_new, F32, tm=ts, name=f"{tag}_dnorm")
    return dh, dg_pre, dg_post, d_win, d_wo.reshape(N_DEV, -1, D_MODEL)


def _blockdiag(w4):
    n, b, _ = w4.shape
    eye = jnp.eye(n, dtype=w4.dtype)
    return (eye[:, None, :, None] * w4[:, :, None, :]).reshape(n * b, n * b)


def _blockdiag_grad(d):
    n = d.shape[0] // HEAD
    x = d.reshape(n, HEAD, n, HEAD)
    return jnp.stack([x[i, :, i, :] for i in range(n)])


def _row(v):
    return v.reshape(1, -1)


def _mixer_fwd(h, g_pre, g_post, wi, wo, P, bd, tiles, tag, ride=None):
    tb, ts = tiles
    a = _norm_fwd(h, g_pre, tm=ts, name=f"{tag}_norm")
    p = _mm(a[None], wi[None], trans_b=False, tm=tb, tn=N_IN, out_dtype=F32, name=f"{tag}_in")
    lx, lg = p[:, 0:256], p[:, 256:512]
    sb, scc, sx = p[:, 512:768], p[:, 768:1024], p[:, 1024:1280]
    z = p[:, 1280:]
    lxs = [lx, _shift(lx, 1), _shift(lx, 2), _shift(lx, 3)]
    cw = [_row(P['lru_conv_w'][kk]) for kk in range(4)]
    lru_par = cw + [_row(P['lru_conv_b']), _blockdiag(P['lru_wa']), _row(P['lru_ba']), _blockdiag(P['lru_wx']),
                    _row(P['lru_bx']), _row(P['lru_lambda'])]
    la, lb = _rowwise(_lru_pre, lxs, lru_par, [(LRU_W, F32)] * 2, [], tm=ts, name=f"{tag}_lru_pre")
    hs = _lru_scan(la, lb, name=f"{tag}_lru_scan")
    y_lru = _rowwise(lambda gg, hh, ng, b_: _lru_post(b_, gg, hh, ng), [lg, hs], [_row(P['lru_norm_g']), bd],
                     [(LRU_W, F32)], [], tm=ts, name=f"{tag}_lru_post")[0]
    sc_rows = [sb, scc, sx, _shift(scc, 1), _shift(sx, 1), _shift(scc, 2), _shift(sx, 2)]
    sc_par = [_row(P['sc_conv_w'][kk]) for kk in range(3)] + [_row(P['sc_norm_g'])]
    y_sc = _rowwise(lambda *v: _sc_fwd(v[-1], *v[:-1]), sc_rows, sc_par + [bd], [(SC_W, F32)], [], tm=ts,
                    name=f"{tag}_sc")[0]
    cuts = (0, RW_W, 2 * RW_W, 3 * RW_W, RW_IN)
    zs = [z[:, cuts[q]:cuts[q + 1]] for q in range(4)]
    z_rows = zs + [_shift(q, 1) for q in zs]
    pad = lambda m, lo: jnp.pad(m, ((lo, LANES - lo - m.shape[0]), (0, 0)))
    rw_par = [_row(P['rwkv_mu'][cuts[q]:cuts[q + 1]]) for q in range(4)]
    rw_par += [_row(P['rwkv_w0']), pad(P['rwkv_w2'], 0), _row(P['rwkv_a0']), pad(P['rwkv_a2'], 32),
               pad(P['rwkv_g2'], 64), _row(P['rwkv_k_k']), _row(P['rwkv_k_a'])]
    r, w, k2, c, b, v, g = _rowwise(lambda *vv: _rw_pre(vv[-1], *vv[:-1]), z_rows, rw_par + [bd], [(RW_W, F32)] * 7, [],
                                    tm=ts, name=f"{tag}_rw_pre")
    vb = _bcast_cols(v)
    spre, yt, *rode = _rw_scan(r, _shift(r, 1), w, k2, c, _unshift(c, 1), b, vb, _stacked_bf16(bd), name=f"{tag}_rw_scan",
                               ride=ride)
    y = _unshift(_cols_to_rows(yt), 1)
    post_par =[_row(P['rwkv_lnx_w']), _row(P['rwkv_lnx_b']), _row(P['rwkv_r_k'])]
    y_rw = _rowwise(lambda *vv: _rw_post(vv[-1], *vv[:-1]), [y, r, k2, v, g], post_par + [bd], [(RW_W, F32)], [], tm=ts,
                    name=f"{tag}_rw_post")[0]
    ycat = jnp.concatenate([y_lru, y_sc, y_rw], axis=1).astype(BF16)
    m = _mm(ycat[None], wo[None], trans_b=False, tm=tb, tn=D_MODEL, out_dtype=F32, name=f"{tag}_out")
    h_new = _res_norm_fwd(h, m, g_post, 1.0, tm=ts, name=f"{tag}_res")
    res = dict(h=h, a=a, m=m, ycat=ycat, lxs=lxs, lru_par=lru_par, lg=lg, la=la, hs=hs, sc_rows=sc_rows, sc_par=sc_par,
               z_rows=z_rows, rw_par=rw_par, r=r, w=w, k2=k2, c=c, b=b, v=v, g=g, vb=vb, spre=spre, y=y, post_par=post_par)
    return h_new, res, rode


def _mixer_bwd(dh_new, R, g_pre, g_post, wi, wo, P, bd, tiles, tag, ride=None):
    tb, ts = tiles
    dm, dg_post = _norm_bwd(R['m'], g_post, dh_new, 1.0, None, BF16, tm=ts, name=f"{tag}_dres")
    dycat = _mm(dm[None], wo[None], trans_b=True, tm=tb, tn=D_MODEL, out_dtype=F32, name=f"{tag}_dycat")
    d_wo = _mm_tn(R['ycat'][None], dm[None], tk=D_MODEL // 2, name=f"{tag}_dwout")[0]
    dy_lru, dy_sc, dy_rw = dycat[:, 0:256], dycat[:, 256:512], dycat[:, 512:]
    G = {}
    d_lg, d_hs, G['lru_norm_g'] = _rowwise(
        lambda gg, hh, ct, ng, b_: _vjp_rows(_lru_post, 1, 3, 1)(b_, gg, hh, ng, ct),
        [R['lg'], R['hs'], dy_lru], [_row(P['lru_norm_g']), bd], [(LRU_W, F32)] * 2, [(1, LRU_W)], tm=ts,
        name=f"{tag}_lru_dpost")
    d_la, d_lb = _lru_scan_bwd(R['la'], R['hs'], d_hs, name=f"{tag}_lru_dscan")

    def lru_pre_bwd(x0, x1, x2, x3, ca, cb_, *par):
        return _vjp_rows(_lru_pre, 0, 14, 2)(x0, x1, x2, x3, *par, ca, cb_)

    par_shapes = [tuple(q.shape) for q in R['lru_par']]
    outs = _rowwise(lru_pre_bwd, R['lxs'] + [d_la, d_lb], R['lru_par'], [(LRU_W, F32)] * 4, par_shapes, tm=ts,
                    name=f"{tag}_lru_dpre")
    dxs, dpar = outs[:4], outs[4:]
    d_lx = _add_n([dxs[0], _unshift(dxs[1], 1), _unshift(dxs[2], 2), _unshift(dxs[3], 3)], tm=ts, name=f"{tag}_lru_dx")
    G['lru_conv_w'] = jnp.concatenate(dpar[0:4], axis=0)
    G['lru_conv_b'] = dpar[4][0]
    G['lru_wa'] = _blockdiag_grad(dpar[5])
    G['lru_ba'] = dpar[6][0]
    G['lru_wx'] = _blockdiag_grad(dpar[7])
    G['lru_bx'] = dpar[8][0]
    G['lru_lambda'] = dpar[9][0]
    G['lru_norm_g'] = G['lru_norm_g'][0]

    def sc_bwd(*vv):
        rows7, ct, par4, b_ = vv[:7], vv[7], vv[8:12], vv[12]
        return _vjp_rows(_sc_fwd, 1, 11, 1)(b_, *rows7, *par4, ct)

    outs = _rowwise(sc_bwd, R['sc_rows'] + [dy_sc], R['sc_par'] + [bd], [(SC_W, F32)] * 7, [(1, SC_W)] * 4, tm=ts,
                    name=f"{tag}_sc_bwd")
    d_sb = outs[0]
    d_sc = _add_n([outs[1], _unshift(outs[3], 1), _unshift(outs[5], 2)], tm=ts, name=f"{tag}_sc_dc")
    d_sx = _add_n([outs[2], _unshift(outs[4], 1), _unshift(outs[6], 2)], tm=ts, name=f"{tag}_sc_dx")
    G['sc_conv_w'] = jnp.concatenate(outs[7:10], axis=0)
    G['sc_norm_g'] = outs[10][0]

    def rw_post_bwd(*vv):
        rows5, ct, par3, b_ = vv[:5], vv[5], vv[6:9], vv[9]
        return _vjp_rows(_rw_post, 1, 8, 1)(b_, *rows5, *par3, ct)

    outs = _rowwise(rw_post_bwd, [R['y'], R['r'], R['k2'], R['v'], R['g'], dy_rw], R['post_par'] + [bd],
                    [(RW_W, F32)] * 5, [(1, RW_W)] * 3, tm=ts, name=f"{tag}_rw_dpost")
    d_y, dr_p, dk_p, dv_p, d_g = outs[:5]
    G['rwkv_lnx_w'], G['rwkv_lnx_b'], G['rwkv_r_k'] = outs[5][0], outs[6][0], outs[7][0]
    dyb = _bcast_cols(d_y)
    dr_s, d_w, dk_s, d_c, d_b, dvt, *rode = _rw_scan_bwd(R['r'], R['w'], R['k2'], R['c'], R['b'], R['vb'], dyb, R['spre'],
                                                        _stacked_bf16(bd), name=f"{tag}_rw_dscan", ride=ride)
    dv_s = _cols_to_rows(dvt)

    def rw_pre_bwd(*vv):
        zrows = vv[0:8]
        dr1, dr2, dw_, dk1, dk2_, dc_, db_, dv1, dv2, dg_ = vv[8:18]
        par, b_ = vv[18:29], vv[29]
        return _vjp_rows(_rw_pre, 1, 19, 7)(b_, *zrows, *par, dr1 + dr2, dw_, dk1 + dk2_, dc_, db_, dv1 + dv2, dg_)

    par_shapes = [tuple(q.shape) for q in R['rw_par']]
    widths = [(q.shape[1], F32) for q in R['z_rows']]
    outs = _rowwise(rw_pre_bwd, R['z_rows'] + [dr_p, dr_s, d_w, dk_p, dk_s, d_c, d_b, dv_p, dv_s, d_g],
                    R['rw_par'] + [bd], widths, par_shapes, tm=ts, name=f"{tag}_rw_dpre")
    d_z = _add_n([jnp.concatenate(outs[0:4], axis=1), _unshift(jnp.concatenate(outs[4:8], axis=1), 1)], tm=ts,
                 name=f"{tag}_rw_dz")
    dpar = outs[8:]
    G['rwkv_mu'] = jnp.concatenate([q[0] for q in dpar[0:4]])
    G['rwkv_w0'], G['rwkv_a0'] = dpar[4][0], dpar[6][0]
    G['rwkv_w2'], G['rwkv_a2'], G['rwkv_g2'] = dpar[5][0:32], dpar[7][32:64], dpar[8][64:128]
    G['rwkv_k_k'], G['rwkv_k_a'] = dpar[9][0], dpar[10][0]

    dp = jnp.concatenate([d_lx, d_lg, d_sb, d_sc, d_sx, d_z], axis=1).astype(BF16)
    da = _mm(dp[None], wi[None], trans_b=True, tm=tb, tn=D_MODEL, out_dtype=F32, name=f"{tag}_da")
    d_wi = _mm_tn(R['a'][None], dp[None], tk=D_MODEL // 2, name=f"{tag}_dwin")[0]
    dh, dg_pre = _norm_bwd(R['h'], g_pre, da, 1.0, dh_new, F32, tm=ts, name=f"{tag}_dnorm")
    return dh, dg_pre, dg_post, d_wi, d_wo, G, rode


def _loss_rows(h, tgt, n_seq, *, tm, name):
    d = h.shape[1]

    def body(h_ref, t_ref, dh_ref, l_ref):
        i = pl.program_id(0)
        row = lax.broadcasted_iota(jnp.int32, (tm, 1), 0) + i * tm
        live = (row >= N_META) & (row < N_META + n_seq)
        e = jnp.where(live, h_ref[...] - t_ref[...], 0.0)
        dh_ref[...] = e * (1.0 / d)
        part = 0.5 * jnp.sum(jnp.sum(e * e, axis=1, keepdims=True) * (1.0 / d), axis=0, keepdims=True)

        @pl.when(i == 0)
        def _():
            l_ref[...] = part

        @pl.when(i > 0)
        def _():
            l_ref[...] += part

    blk = pl.BlockSpec((tm, d), lambda i: (i, 0))
    return pl.pallas_call(body, name=name, grid=(h.shape[0] // tm,), in_specs=[blk, blk],
                          out_specs=[blk, pl.BlockSpec((1, 1), lambda i: (0, 0))],
                          out_shape=[jax.ShapeDtypeStruct(h.shape, F32), jax.ShapeDtypeStruct((1, 1), F32)])(h, tgt)


def _pack(arrs, mult):
    flat = jnp.concatenate([a.reshape(-1).astype(F32) for a in arrs])
    n = flat.shape[0]
    tot = -(-n // mult) * mult
    return jnp.pad(flat, (0, tot - n)).reshape(-1, LANES)


def _unpack(buf, shapes):
    flat = buf.reshape(-1)
    out, off = [], 0
    for s in shapes:
        n = 1
        for q in s:
            n *= q
        out.append(flat[off:off + n].reshape(s))
        off += n
    return out


def _step(W, M, V, x, loss_target):
    n_seq = x.shape[1]
    t_real = N_META + n_seq
    t = (t_real // CHUNK + 1) * CHUNK
    tiles = (_pick(t, (704, 512, 256, 128, 64)), _pick(t, (192, 128, 64)))
    me = 4 * lax.axis_index("x") + 2 * lax.axis_index("y") + lax.axis_index("c")
    n_layer = W['norm_g'].shape[0]

    small_sh = list(SMALL_SHARDED)
    packed = _pack([W[n] for n in small_sh], 8 * LANES)
    gathered = _exchange([W[n][0].astype(BF16) for n in BIG] + [packed], gather=True, name="gather_weights")
    big8 = [dict(zip(BIG, gathered[:-1]))] + [None] * (n_layer - 1)
    pieces = [_unpack(gathered[-1][q], [W[n].shape for n in small_sh]) for q in range(N_DEV)]
    full = {n: W[n] for n in SMALL if n not in SMALL_SHARDED}
    for idx, n in enumerate(small_sh):
        full[n] = jnp.concatenate([pieces[q][idx] for q in range(N_DEV)], axis=SMALL_SHARDED[n])

    def layer_weights(l):
        w24_1 = big8[l]['ffn1_w_in'].reshape(2, N_DEV // 2, D_MODEL, FFN_BLK)
        wo4_1 = big8[l]['ffn1_w_out'].reshape(N_DEV // 2, FFN_BLK, D_MODEL)
        w24_2 = big8[l]['ffn2_w_in'].reshape(2, N_DEV // 2, D_MODEL, FFN_BLK)
        wo4_2 = big8[l]['ffn2_w_out'].reshape(N_DEV // 2, FFN_BLK, D_MODEL)
        wi = big8[l]['mix_w_in'].transpose(1, 0, 2).reshape(D_MODEL, N_IN)
        wo = big8[l]['mix_w_out'].reshape(D_MODEL, D_MODEL)
        return w24_1, wo4_1, w24_2, wo4_2, wi, wo

    bd = jnp.kron(jnp.eye(LANES // HEAD, dtype=F32), jnp.ones((HEAD, HEAD), F32))
    small_layer = [n for n in SMALL if n not in ('meta_tokens', 'norm_g')]

    h = jnp.concatenate([full['meta_tokens'], x[0], jnp.zeros((t - t_real, D_MODEL), F32)], axis=0)
    saved = []
    for l in range(n_layer):
        lw = layer_weights(l)
        ng = [_row(full['norm_g'][l, q]) for q in range(6)]
        P = {n: full[n][l] for n in small_layer}
        h, r1 = _ffn_fwd(h, ng[0], ng[1], lw[0], lw[1], tiles, f"l{l}_ffn1")
        ride = ([W[n][l + 1].astype(BF16) for n in BIG], True) if l + 1 < n_layer else None
        h, r2, rode = _mixer_fwd(h, ng[2], ng[3], lw[4], lw[5], P, bd, tiles, f"l{l}_mix", ride=ride)
        if ride is not None:
            big8[l + 1] = dict(zip(BIG, rode))
        h, r3 = _ffn_fwd(h, ng[4], ng[5], lw[2], lw[3], tiles, f"l{l}_ffn2")
        saved.append((lw, ng, P, r1, r2, r3))

    tgt = jnp.pad(loss_target[0], ((N_META, t - t_real), (0, 0)))
    dh, loss_part = _loss_rows(h, tgt, n_seq, tm=tiles[1], name="loss")
    loss = lax.psum(loss_part[0, 0], MESH_AXES)

    small_grads = [None] * n_layer
    norm_grads = [None] * n_layer
    recv = [None] * n_layer
    outgoing = None
    for l in reversed(range(n_layer)):
        lw, ng, P, r1, r2, r3 = saved[l]
        dh, g4, g5, d_win2, d_wo2 = _ffn_bwd(dh, r3, ng[4], ng[5], lw[2], lw[3], tiles, f"l{l}_ffn2")
        ride = (outgoing, False) if outgoing is not None else None
        dh, g2, g3, d_wi, d_wo, G, rode = _mixer_bwd(dh, r2, ng[2], ng[3], lw[4], lw[5], P, bd, tiles, f"l{l}_mix", ride=ride)
        if ride is not None:
            recv[l + 1] = rode
        dh, g0, g1, d_win1, d_wo1 = _ffn_bwd(dh, r1, ng[0], ng[1], lw[0], lw[1], tiles, f"l{l}_ffn1")
        small_grads[l] = G
        norm_grads[l] = jnp.concatenate([g0, g1, g2, g3, g4, g5], axis=0)
        d_wi8 = d_wi.reshape(D_MODEL, N_DEV, N_IN // N_DEV).transpose(1, 0, 2)
        d_wo8 = d_wo.reshape(N_DEV, D_MODEL // N_DEV, D_MODEL)
        outgoing = [d_win1, d_wo1, d_win2, d_wo2, d_wi8, d_wo8]
    recv[0] = _exchange(outgoing, gather=False, name="l0_grad_exchange")

    gs = {n: jnp.stack([small_grads[l][n] for l in range(n_layer)]) for n in small_layer}
    gs['norm_g'] = jnp.stack(norm_grads)
    gs['meta_tokens'] = dh[:N_META]
    gpack = _pack([gs[n] for n in SMALL], 8 * LANES)
    gall = _exchange([gpack], gather=True, name="gather_small_grads")[0]
    gsum = _rowwise(lambda parts: _sum_slots(parts), [gall], [], [(LANES, F32)], [], tm=gall.shape[1],
                    name="sum_small_grads")[0]
    gfull = dict(zip(SMALL, _unpack(gsum, [gs[n].shape for n in SMALL])))

    def my_shard(n, a):
        if n not in SMALL_SHARDED:
            return a
        ax = SMALL_SHARDED[n]
        size = a.shape[ax] // N_DEV
        return lax.dynamic_slice_in_dim(a, me * size, size, axis=ax)

    g_loc = [my_shard(n, gfull[n]) for n in SMALL]
    shapes = [W[n].shape for n in SMALL]
    bufs = [_pack(g_loc, 8 * LANES)] + [_pack([D[n] for n in SMALL], 8 * LANES) for D in (W, M, V)]
    d_s, m_s, v_s = _rowwise(_adamw_rows, bufs, [], [(LANES, F32)] * 3, [], tm=bufs[0].shape[0], name="adamw_small")
    out = {'grad': dict(zip(SMALL, g_loc)), 'delta': dict(zip(SMALL, _unpack(d_s, shapes))),
           'm': dict(zip(SMALL, _unpack(m_s, shapes))), 'v': dict(zip(SMALL, _unpack(v_s, shapes)))}

    order = ['ffn1_w_in', 'ffn1_w_out', 'ffn2_w_in', 'ffn2_w_out', 'mix_w_in', 'mix_w_out']
    for idx, n in enumerate(order):
        per_layer = []
        for l in range(n_layer):
            parts = recv[l][idx]
            rows, cols = W[n].shape[1], W[n].shape[2]
            per_layer.append(_reduce_adamw(parts.reshape(N_DEV, rows, cols), W[n][l], M[n][l], V[n][l],
                                           name=f"l{l}_adamw_{n}"))
        for q, key in enumerate(('grad', 'delta', 'm', 'v')):
            out[key][n] = jnp.stack([per_layer[l][q] for l in range(n_layer)])

    return (loss, dh[N_META:t_real][None],
            *[out['grad'][n] for n in WEIGHTS], *[out['delta'][n] for n in WEIGHTS],
            *[out['m'][n] for n in WEIGHTS], *[out['v'][n] for n in WEIGHTS])


def kernel(x, meta_tokens, norm_g, ffn1_w_in, ffn1_w_out, ffn2_w_in, ffn2_w_out, mix_w_in, mix_w_out, lru_conv_w, lru_conv_b, lru_wa, lru_ba, lru_wx, lru_bx, lru_lambda, lru_norm_g, sc_conv_w, sc_norm_g, rwkv_mu, rwkv_w0, rwkv_w2, rwkv_a0, rwkv_a2, rwkv_g2, rwkv_k_k, rwkv_k_a, rwkv_r_k, rwkv_lnx_w, rwkv_lnx_b, loss_target, m_meta_tokens, m_norm_g, m_ffn1_w_in, m_ffn1_w_out, m_ffn2_w_in, m_ffn2_w_out, m_mix_w_in, m_mix_w_out, m_lru_conv_w, m_lru_conv_b, m_lru_wa, m_lru_ba, m_lru_wx, m_lru_bx, m_lru_lambda, m_lru_norm_g, m_sc_conv_w, m_sc_norm_g, m_rwkv_mu, m_rwkv_w0, m_rwkv_w2, m_rwkv_a0, m_rwkv_a2, m_rwkv_g2, m_rwkv_k_k, m_rwkv_k_a, m_rwkv_r_k, m_rwkv_lnx_w, m_rwkv_lnx_b, v_meta_tokens, v_norm_g, v_ffn1_w_in, v_ffn1_w_out, v_ffn2_w_in, v_ffn2_w_out, v_mix_w_in, v_mix_w_out, v_lru_conv_w, v_lru_conv_b, v_lru_wa, v_lru_ba, v_lru_wx, v_lru_bx, v_lru_lambda, v_lru_norm_g, v_sc_conv_w, v_sc_norm_g, v_rwkv_mu, v_rwkv_w0, v_rwkv_w2, v_rwkv_a0, v_rwkv_a2, v_rwkv_g2, v_rwkv_k_k, v_rwkv_k_a, v_rwkv_r_k, v_rwkv_lnx_w, v_rwkv_lnx_b):
    w_vals = (meta_tokens, norm_g, ffn1_w_in, ffn1_w_out, ffn2_w_in, ffn2_w_out, mix_w_in, mix_w_out, lru_conv_w, lru_conv_b, lru_wa, lru_ba, lru_wx, lru_bx, lru_lambda, lru_norm_g, sc_conv_w, sc_norm_g, rwkv_mu, rwkv_w0, rwkv_w2, rwkv_a0, rwkv_a2, rwkv_g2, rwkv_k_k, rwkv_k_a, rwkv_r_k, rwkv_lnx_w, rwkv_lnx_b)
    m_vals = (m_meta_tokens, m_norm_g, m_ffn1_w_in, m_ffn1_w_out, m_ffn2_w_in, m_ffn2_w_out, m_mix_w_in, m_mix_w_out, m_lru_conv_w, m_lru_conv_b, m_lru_wa, m_lru_ba, m_lru_wx, m_lru_bx, m_lru_lambda, m_lru_norm_g, m_sc_conv_w, m_sc_norm_g, m_rwkv_mu, m_rwkv_w0, m_rwkv_w2, m_rwkv_a0, m_rwkv_a2, m_rwkv_g2, m_rwkv_k_k, m_rwkv_k_a, m_rwkv_r_k, m_rwkv_lnx_w, m_rwkv_lnx_b)
    v_vals = (v_meta_tokens, v_norm_g, v_ffn1_w_in, v_ffn1_w_out, v_ffn2_w_in, v_ffn2_w_out, v_mix_w_in, v_mix_w_out, v_lru_conv_w, v_lru_conv_b, v_lru_wa, v_lru_ba, v_lru_wx, v_lru_bx, v_lru_lambda, v_lru_norm_g, v_sc_conv_w, v_sc_norm_g, v_rwkv_mu, v_rwkv_w0, v_rwkv_w2, v_rwkv_a0, v_rwkv_a2, v_rwkv_g2, v_rwkv_k_k, v_rwkv_k_a, v_rwkv_r_k, v_rwkv_lnx_w, v_rwkv_lnx_b)
    return _step(dict(zip(WEIGHTS, w_vals)), dict(zip(WEIGHTS, m_vals)), dict(zip(WEIGHTS, v_vals)), x, loss_target)
```

```python
import functools

import jax
import jax.numpy as jnp
from jax import lax
from jax.experimental import pallas as pl
from jax.experimental.pallas import tpu as pltpu

F32 = jnp.float32
BF16 = jnp.bfloat16
HIGHEST = lax.Precision.HIGHEST

N_DEV = 8
MESH_AXES = ("x", "y", "c")
N_META = 16
D_MODEL = 1024
LRU_W = 256
SC_W = 256
RW_W = 512
HEAD = 64
LANES = 128
CHUNK = 64
RW_IN = 1664
N_IN = 2944
FFN_BLK = 704
RMS_EPS = 1e-6
LNX_EPS = 64e-5
LRU_C = 8.0
ADAM_LR, ADAM_B1, ADAM_B2, ADAM_EPS, ADAM_WD, ADAM_STEP = 0.001, 0.9, 0.999, 1e-08, 0.01, 10

WEIGHTS = ['meta_tokens', 'norm_g', 'ffn1_w_in', 'ffn1_w_out', 'ffn2_w_in', 'ffn2_w_out', 'mix_w_in', 'mix_w_out',
           'lru_conv_w', 'lru_conv_b', 'lru_wa', 'lru_ba', 'lru_wx', 'lru_bx', 'lru_lambda', 'lru_norm_g',
           'sc_conv_w', 'sc_norm_g', 'rwkv_mu', 'rwkv_w0', 'rwkv_w2', 'rwkv_a0', 'rwkv_a2', 'rwkv_g2', 'rwkv_k_k',
           'rwkv_k_a', 'rwkv_r_k', 'rwkv_lnx_w', 'rwkv_lnx_b']
BIG = ['ffn1_w_in', 'ffn1_w_out', 'ffn2_w_in', 'ffn2_w_out', 'mix_w_in', 'mix_w_out']
SMALL_SHARDED = {'meta_tokens': 1, 'norm_g': 2, 'lru_conv_w': 2, 'sc_conv_w': 2, 'rwkv_w2': 2, 'rwkv_a2': 2, 'rwkv_g2': 2}
SMALL = [n for n in WEIGHTS if n not in BIG]


def _pick(n, cands):
    for c in cands:
        if n % c == 0:
            return c
    raise ValueError(f"no tile for {n}")


def _rowwise(fn, rows, params, row_outs, acc_outs, *, tm, name):
    nr, npar, nro, nao = len(rows), len(params), len(row_outs), len(acc_outs)
    n_rows = rows[0].shape[-2]
    assert n_rows % tm == 0, (name, n_rows, tm)

    def body(*refs):
        vals = [r[...] for r in refs[:nr + npar]]
        outs = fn(*vals)
        if not isinstance(outs, (tuple, list)):
            outs = (outs,)
        assert len(outs) == nro + nao, (name, len(outs))
        for o_ref, o in zip(refs[nr + npar:nr + npar + nro], outs[:nro]):
            o_ref[...] = o.astype(o_ref.dtype)
        step = pl.program_id(0)
        for a_ref, a in zip(refs[nr + npar + nro:], outs[nro:]):
            @pl.when(step == 0)
            def _(a_ref=a_ref, a=a):
                a_ref[...] = a.astype(F32)

            @pl.when(step > 0)
            def _(a_ref=a_ref, a=a):
                a_ref[...] += a.astype(F32)

    def row_spec(shape):
        if len(shape) == 2:
            return pl.BlockSpec((tm, shape[1]), lambda i: (i, 0))
        return pl.BlockSpec((shape[0], tm, shape[2]), lambda i: (0, i, 0))

    def full_spec(shape):
        nd = len(shape)
        return pl.BlockSpec(tuple(shape), lambda i, nd=nd: (0,) * nd)

    in_specs = [row_spec(r.shape) for r in rows] + [full_spec(p.shape) for p in params]
    out_shape = [jax.ShapeDtypeStruct((n_rows, w), dt) for (w, dt) in row_outs]
    out_shape += [jax.ShapeDtypeStruct(tuple(s), F32) for s in acc_outs]
    out_specs = [row_spec((n_rows, w)) for (w, _) in row_outs] + [full_spec(s) for s in acc_outs]
    res = pl.pallas_call(body, name=name, grid=(n_rows // tm,), in_specs=in_specs, out_specs=out_specs,
                         out_shape=out_shape)(*rows, *params)
    return tuple(res)


def _mm(a3, b3, *, trans_b, tm, tn, out_dtype, name):
    nj, m, kb = a3.shape
    n = b3.shape[1] if trans_b else b3.shape[2]
    dims = (((1,), (1,)), ((), ())) if trans_b else (((1,), (0,)), ((), ()))

    def body(a_ref, b_ref, o_ref, acc_ref):
        j = pl.program_id(2)

        @pl.when(j == 0)
        def _():
            acc_ref[...] = jnp.zeros_like(acc_ref)

        acc_ref[...] += lax.dot_general(a_ref[0], b_ref[0], dims, preferred_element_type=F32)

        @pl.when(j == nj - 1)
        def _():
            o_ref[...] = acc_ref[...].astype(o_ref.dtype)

    if trans_b:
        b_spec = pl.BlockSpec((1, tn, kb), lambda i, c, j: (j, c, 0))
    else:
        b_spec = pl.BlockSpec((1, kb, tn), lambda i, c, j: (j, 0, c))
    return pl.pallas_call(
        body, name=name, grid=(m // tm, n // tn, nj),
        in_specs=[pl.BlockSpec((1, tm, kb), lambda i, c, j: (j, i, 0)), b_spec],
        out_specs=pl.BlockSpec((tm, tn), lambda i, c, j: (i, c)),
        out_shape=jax.ShapeDtypeStruct((m, n), out_dtype),
        scratch_shapes=[pltpu.VMEM((tm, tn), F32)],
    )(a3, b3)


def _mm_tn(a3, b3, *, tk, name):
    ja, t, ka = a3.shape
    jb, _, n = b3.shape
    nj = max(ja, jb)

    def body(a_ref, b_ref, o_ref):
        o_ref[0] = lax.dot_general(a_ref[0], b_ref[0], (((0,), (0,)), ((), ())),
                                   preferred_element_type=F32).astype(o_ref.dtype)

    return pl.pallas_call(
        body, name=name, grid=(nj, ka // tk),
        in_specs=[pl.BlockSpec((1, t, tk), (lambda j, c: (j, 0, c)) if ja > 1 else (lambda j, c: (0, 0, c))),
                  pl.BlockSpec((1, t, n), (lambda j, c: (j, 0, 0)) if jb > 1 else (lambda j, c: (0, 0, 0)))],
        out_specs=pl.BlockSpec((1, tk, n), lambda j, c: (j, c, 0)),
        out_shape=jax.ShapeDtypeStruct((nj, ka, n), BF16),
    )(a3, b3)


def _ffn_in(a, w24, *, tm, name):
    t, d = a.shape
    nb, fb = w24.shape[1], w24.shape[3]

    def body(a_ref, w_ref, gu_ref, s_ref):
        x = a_ref[...]
        g = jnp.dot(x, w_ref[0, 0], preferred_element_type=F32)
        u = jnp.dot(x, w_ref[1, 0], preferred_element_type=F32)
        gu_ref[0, 0] = g.astype(BF16)
        gu_ref[1, 0] = u.astype(BF16)
        s_ref[0] = (g * jax.nn.sigmoid(g) * u).astype(BF16)

    return pl.pallas_call(
        body, name=name, grid=(nb, t // tm),
        in_specs=[pl.BlockSpec((tm, d), lambda j, i: (i, 0)), pl.BlockSpec((2, 1, d, fb), lambda j, i: (0, j, 0, 0))],
        out_specs=[pl.BlockSpec((2, 1, tm, fb), lambda j, i: (0, j, i, 0)), pl.BlockSpec((1, tm, fb), lambda j, i: (j, i, 0))],
        out_shape=[jax.ShapeDtypeStruct((2, nb, t, fb), BF16), jax.ShapeDtypeStruct((nb, t, fb), BF16)],
    )(a, w24)


def _ffn_dswiglu(df, wo4, gu, *, tm, name):
    t, d = df.shape
    nb, fb = wo4.shape[0], wo4.shape[1]

    def body(df_ref, wo_ref, gu_ref, dg_ref):
        ds = lax.dot_general(df_ref[...], wo_ref[0], (((1,), (1,)), ((), ())), preferred_element_type=F32)
        g = gu_ref[0, 0].astype(F32)
        u = gu_ref[1, 0].astype(F32)
        sig = jax.nn.sigmoid(g)
        dg_ref[0, 0] = (ds * u * sig * (1.0 + g * (1.0 - sig))).astype(BF16)
        dg_ref[1, 0] = (ds * g * sig).astype(BF16)

    return pl.pallas_call(
        body, name=name, grid=(nb, t // tm),
        in_specs=[pl.BlockSpec((tm, d), lambda j, i: (i, 0)), pl.BlockSpec((1, fb, d), lambda j, i: (j, 0, 0)),
                  pl.BlockSpec((2, 1, tm, fb), lambda j, i: (0, j, i, 0))],
        out_specs=pl.BlockSpec((2, 1, tm, fb), lambda j, i: (0, j, i, 0)),
        out_shape=jax.ShapeDtypeStruct((2, nb, t, fb), BF16),
    )(df, wo4, gu)


def _rms(x, g):
    return x * lax.rsqrt(jnp.mean(x * x, axis=-1, keepdims=True) + RMS_EPS) * g


def _rms_bwd(x, g, dy):
    rstd = lax.rsqrt(jnp.mean(x * x, axis=-1, keepdims=True) + RMS_EPS)
    xh = x * rstd
    dxh = dy * g
    dx = rstd * (dxh - xh * jnp.mean(dxh * xh, axis=-1, keepdims=True))
    return dx, jnp.sum(dy * xh, axis=0, keepdims=True)


def _seg_sum_impl(x, bd):
    parts = [jnp.dot(x[:, q * LANES:(q + 1) * LANES], bd, preferred_element_type=F32, precision=HIGHEST)
             for q in range(x.shape[1] // LANES)]
    return parts[0] if len(parts) == 1 else jnp.concatenate(parts, axis=1)


@jax.custom_vjp
def _seg_sum(x, bd):
    return _seg_sum_impl(x, bd)


def _seg_sum_fwd(x, bd):
    return _seg_sum_impl(x, bd), bd


def _seg_sum_bwd(bd, ct):
    return _seg_sum_impl(ct, bd), jnp.zeros_like(bd)


_seg_sum.defvjp(_seg_sum_fwd, _seg_sum_bwd)


def _group_rms(y, g, bd):
    return y * lax.rsqrt(_seg_sum(y * y, bd) * (1.0 / HEAD) + RMS_EPS) * g


def _expm1(x):
    return jnp.where(jnp.abs(x) < 1e-2, x * (1.0 + x * (0.5 + x * (1.0 / 6.0))), jnp.exp(x) - 1.0)


def _lru_pre(x0, x1, x2, x3, cw0, cw1, cw2, cw3, cb, wa, ba, wx, bx, lam):
    u = x3 * cw0 + x2 * cw1 + x1 * cw2 + x0 * cw3 + cb
    r = jax.nn.sigmoid(jnp.dot(u, wa, preferred_element_type=F32, precision=HIGHEST) + ba)
    i = jax.nn.sigmoid(jnp.dot(u, wx, preferred_element_type=F32, precision=HIGHEST) + bx)
    log_a = -LRU_C * r * jax.nn.softplus(-lam)
    return jnp.exp(log_a), jnp.sqrt(-_expm1(2.0 * log_a)) * (i * u)


def _lru_post(bd, gate, hs, ng):
    return _group_rms(jax.nn.gelu(gate) * hs, ng, bd)


def _sc_fwd(bd, b, c0, x0, c1, x1, c2, x2, w0, w1, w2, ng):
    return _group_rms(b * (w0 * (c2 * x2) + w1 * (c1 * x1) + w2 * (c0 * x0)), ng, bd)


def _rw_pre(bd, zr, zk, zv, zt, sr, sk, sv, st, mur, muk, muv, mut, w0, w2p, a0, a2p, g2p, k_k, k_a):
    r, k, v, tail = zr + (sr - zr) * mur, zk + (sk - zk) * muk, zv + (sv - zv) * muv, zt + (st - zt) * mut
    lane = lax.broadcasted_iota(jnp.int32, tail.shape, 1)
    act = jnp.where(lane < 32, jnp.tanh(tail), jnp.where(lane < 64, tail, jax.nn.sigmoid(tail)))
    dot = functools.partial(jnp.dot, preferred_element_type=F32, precision=HIGHEST)
    w_log = -jax.nn.softplus(-(w0 + dot(act, w2p))) - 0.5
    w = jnp.exp(-jnp.exp(w_log))
    a = jax.nn.sigmoid(a0 + dot(act, a2p))
    g = dot(act, g2p)
    kk = k * k_k
    k2 = k * (1.0 + (a - 1.0) * k_a)
    kkn = kk * lax.rsqrt(jnp.maximum(_seg_sum(kk * kk, bd), 1e-24))
    return r, w, k2, -kkn, kkn * a, v, g


def _rw_post(bd, y, r, k2, v, g, lnw, lnb, r_k):
    mean = _seg_sum(y, bd) * (1.0 / HEAD)
    yc = y - mean
    var = _seg_sum(yc * yc, bd) * (1.0 / HEAD)
    yn = yc * lax.rsqrt(var + LNX_EPS) * lnw + lnb
    return (yn + _seg_sum(r * k2 * r_k, bd) * v) * g


def _vjp_rows(fwd, n_static, n_in, n_ct):
    def fn(*args):
        static, prim, cts = args[:n_static], args[n_static:n_static + n_in], args[n_static + n_in:]
        assert len(cts) == n_ct
        _, vjp = jax.vjp(functools.partial(fwd, *static), *prim)
        return vjp(cts[0] if n_ct == 1 else tuple(cts))
    return fn


def _exchange_copies(x_refs, o_refs, sems, gather):
    send_sems, recv_sems, local_sems = sems
    mx, my, mc = lax.axis_index("x"), lax.axis_index("y"), lax.axis_index("c")
    me = 4 * mx + 2 * my + mc
    local, sends, recvs = [], [], []
    for k in range(len(x_refs)):
        local.append(pltpu.make_async_copy(x_refs[k] if gather else x_refs[k].at[me], o_refs[k].at[me], local_sems.at[k]))
    for d in range(1, N_DEV):
        px, py, pc = mx ^ ((d >> 2) & 1), my ^ ((d >> 1) & 1), mc ^ (d & 1)
        peer = 4 * px + 2 * py + pc
        for k in range(len(x_refs)):
            src = x_refs[k] if gather else x_refs[k].at[peer]
            common = dict(src_ref=src, send_sem=send_sems.at[k, d - 1], recv_sem=recv_sems.at[k, d - 1],
                          device_id=(px, py, pc), device_id_type=pl.DeviceIdType.MESH)
            sends.append(pltpu.make_async_remote_copy(dst_ref=o_refs[k].at[me], **common))
            recvs.append(pltpu.make_async_remote_copy(dst_ref=o_refs[k].at[peer], **common))
    return local, sends, recvs


def _gather2_copies(x_refs, o_refs, sems):
    send_sems, recv_sems, local_sems = sems
    mx, my, mc = lax.axis_index("x"), lax.axis_index("y"), lax.axis_index("c")
    sibling = (mx, my, 1 - mc)
    chips = [(1 - mx, my), (mx, 1 - my), (1 - mx, 1 - my)]

    def slot(px, py, pc):
        return 4 * px + 2 * py + pc

    out = dict(local=[], first=[], first_recv=[], ici_recv=[], passed=[], passed_recv=[])
    for k in range(len(x_refs)):
        def copy(sem, src, dst_slot, to, k=k):
            return pltpu.make_async_remote_copy(src_ref=src, dst_ref=o_refs[k].at[dst_slot], send_sem=send_sems.at[k, sem],
                                                recv_sem=recv_sems.at[k, sem], device_id=to, device_id_type=pl.DeviceIdType.MESH)
        me = slot(mx, my, mc)
        out['local'].append(pltpu.make_async_copy(x_refs[k], o_refs[k].at[me], local_sems.at[k]))
        out['first'].append(copy(0, x_refs[k], me, sibling))
        out['first_recv'].append(copy(0, x_refs[k], slot(mx, my, 1 - mc), sibling))
        for j, (px, py) in enumerate(chips):
            out['first'].append(copy(1 + j, x_refs[k], me, (px, py, mc)))
            out['ici_recv'].append(copy(1 + j, x_refs[k], slot(px, py, mc), (px, py, mc)))
            out['passed'].append(copy(4 + j, o_refs[k].at[slot(px, py, mc)], slot(px, py, mc), sibling))
            out['passed_recv'].append(copy(4 + j, x_refs[k], slot(px, py, 1 - mc), sibling))
    return out


def _exchange_start(x_refs, o_refs, sems, gather):
    if gather:
        cps = _gather2_copies(x_refs, o_refs, sems)
        for cp in cps['local'] + cps['first']:
            cp.start()
        return
    local, sends, _ = _exchange_copies(x_refs, o_refs, sems, gather)
    for cp in local + sends:
        cp.start()


def _exchange_wait(x_refs, o_refs, sems, gather):
    if gather:
        cps = _gather2_copies(x_refs, o_refs, sems)
        for arrived, onward in zip(cps['ici_recv'], cps['passed']):
            arrived.wait_recv()
            onward.start()
        for cp in cps['first'] + cps['passed']:
            cp.wait_send()
        for cp in cps['first_recv'] + cps['passed_recv']:
            cp.wait_recv()
        for cp in cps['local']:
            cp.wait()
        return
    local, sends, recvs = _exchange_copies(x_refs, o_refs, sems, gather)
    for cp in sends:
        cp.wait_send()
    for cp in recvs:
        cp.wait_recv()
    for cp in local:
        cp.wait()


def _exchange_out_shape(xs, gather):
    return [jax.ShapeDtypeStruct(((N_DEV,) + x.shape) if gather else x.shape, x.dtype) for x in xs]


def _exchange_sems(n):
    return [pltpu.SemaphoreType.DMA((n, N_DEV - 1)), pltpu.SemaphoreType.DMA((n, N_DEV - 1)), pltpu.SemaphoreType.DMA((n,))]


SUBLANES = 8


def _store_row(ref, i, cols, row):
    base = pl.multiple_of((i // SUBLANES) * SUBLANES, SUBLANES)
    sub = lax.broadcasted_iota(jnp.int32, (SUBLANES, row.shape[1]), 0)
    ref[pl.ds(base, SUBLANES), cols] = jnp.where(sub == i % SUBLANES, row, ref[pl.ds(base, SUBLANES), cols])


def _lru_scan(a, b, name):
    t, w = a.shape

    def body(a_ref, b_ref, h_ref):
        h_ref[...] = jnp.zeros_like(h_ref)

        def step(i, h):
            h = a_ref[pl.ds(i, 1), :] * h + b_ref[pl.ds(i, 1), :]
            _store_row(h_ref, i, slice(None), h)
            return h
        lax.fori_loop(0, t, step, jnp.zeros((1, w), F32), unroll=8)

    return pl.pallas_call(body, name=name, out_shape=jax.ShapeDtypeStruct((t, w), F32))(a, b)


def _lru_scan_bwd(a, hs, dhs, name):
    t, w = a.shape

    def body(a_ref, h_ref, dh_ref, da_ref, db_ref):
        da_ref[...] = jnp.zeros_like(da_ref)
        db_ref[...] = jnp.zeros_like(db_ref)

        def step(n, carry):
            i = t - 1 - n
            lam = dh_ref[pl.ds(i, 1), :] + carry
            _store_row(db_ref, i, slice(None), lam)
            hprev = h_ref[pl.ds(jnp.maximum(i - 1, 0), 1), :]
            _store_row(da_ref, i, slice(None), jnp.where(i > 0, lam * hprev, 0.0))
            return a_ref[pl.ds(i, 1), :] * lam
        lax.fori_loop(0, t, step, jnp.zeros((1, w), F32), unroll=8)

    return pl.pallas_call(body, name=name, out_shape=[jax.ShapeDtypeStruct((t, w), F32)] * 2)(a, hs, dhs)


N_PAIR = RW_W // LANES
UNROLL = 2


def _bcast_cols(v):
    t = v.shape[0]
    x = v.astype(BF16).reshape(t, N_PAIR, 2, HEAD)
    shape = (t, N_PAIR, HEAD, LANES)
    lane = lax.broadcasted_iota(jnp.int32, shape, 3)
    return jnp.where(lane < HEAD, jnp.broadcast_to(x[:, :, 0, :, None], shape), jnp.broadcast_to(x[:, :, 1, :, None], shape))


def _cols_to_rows(yt):
    nch = yt.shape[2] // LANES
    x = yt.reshape(N_PAIR, HEAD, nch, 2, CHUNK).transpose(2, 4, 0, 3, 1)
    return x.reshape(nch * CHUNK, RW_W)


def _stacked_bf16(bd):
    return jnp.concatenate([bd, bd], axis=0).astype(BF16)


def _group_sums(prods, bd2):
    x = jnp.concatenate(prods, axis=0)
    hi = x.astype(BF16)
    lo = (x - hi.astype(F32)).astype(BF16)
    return jnp.dot(jnp.concatenate([hi, lo], axis=1), bd2, preferred_element_type=F32)


def _rw_scan(r, r_prev, w, k, c, c_next, b, vb, bd2, name, ride=None):
    t = w.shape[0]
    nch = t // CHUNK
    ride_xs, ride_gather = ride if ride is not None else ([], False)
    n_ride = len(ride_xs)

    def body(*refs):
        r_ref, rp_ref, w_ref, k_ref, c_ref, cn_ref, b_ref, vb_ref, bd_ref = refs[:9]
        x_refs = refs[9:9 + n_ride]
        spre_ref, yt_ref = refs[9 + n_ride:11 + n_ride]
        o_refs = refs[11 + n_ride:11 + 2 * n_ride]
        s_ref, wc_ref, wr_ref, bc_ref, kc_ref, br_ref, kr_ref = refs[11 + 2 * n_ride:18 + 2 * n_ride]
        sems = refs[18 + 2 * n_ride:]

        @pl.when(pl.program_id(0) == 0)
        def _():
            s_ref[...] = jnp.zeros_like(s_ref)
            if n_ride:
                _exchange_start(x_refs, o_refs, sems, ride_gather)

        yt_ref[...] = jnp.zeros_like(yt_ref)
        bdv = bd_ref[...]
        lane = lax.broadcasted_iota(jnp.int32, (HEAD, LANES), 1) % CHUNK

        wv, cn, rv, bv, kv = w_ref[...], cn_ref[...], r_ref[...], b_ref[...], k_ref[...]
        wc_ref[...] = wv * cn
        wr_ref[...] = wv * rv
        for ref, x in ((bc_ref, bv * cn), (kc_ref, kv * cn), (br_ref, bv * rv), (kr_ref, kv * rv)):
            sums = _group_sums([x[:, q * LANES:(q + 1) * LANES] for q in range(N_PAIR)], bdv)
            for q in range(N_PAIR):
                ref[:, q * LANES:(q + 1) * LANES] = sums[q * CHUNK:(q + 1) * CHUNK]

        def cut(ref, i):
            x = ref[pl.ds(i, 1), :]
            return [x[:, p * LANES:(p + 1) * LANES] for p in range(N_PAIR)]

        def two_steps(j, st):
            i0 = 2 * j
            i1 = i0 + 1
            c0, wc0, rp0, wr0 = cut(c_ref, i0), cut(wc_ref, i0), cut(rp_ref, i0), cut(wr_ref, i0)
            w0, b0, k0, w1, b1, k1 = cut(w_ref, i0), cut(b_ref, i0), cut(k_ref, i0), cut(w_ref, i1), cut(b_ref, i1), cut(k_ref, i1)
            bc0, kc0, br0, kr0 = cut(bc_ref, i0), cut(kc_ref, i0), cut(br_ref, i0), cut(kr_ref, i0)
            pairs = range(N_PAIR)
            red = _group_sums([st[p] * c0[p] for p in pairs] + [st[p] * wc0[p] for p in pairs], bdv)
            out = _group_sums([st[p] * rp0[p] for p in pairs] + [st[p] * wr0[p] for p in pairs], bdv)
            new = []
            for p in pairs:
                v0, v1 = vb_ref[i0, p].astype(F32), vb_ref[i1, p].astype(F32)
                sa0 = red[p * HEAD:(p + 1) * HEAD]
                sa1 = red[(N_PAIR + p) * HEAD:(N_PAIR + p + 1) * HEAD] + sa0 * bc0[p] + v0 * kc0[p]
                y_before = out[p * HEAD:(p + 1) * HEAD]
                y0 = out[(N_PAIR + p) * HEAD:(N_PAIR + p + 1) * HEAD] + sa0 * br0[p] + v0 * kr0[p]
                spre_ref[i0, p] = st[p]
                s1 = st[p] * w0[p] + sa0 * b0[p] + v0 * k0[p]
                spre_ref[i1, p] = s1
                new.append(s1 * w1[p] + sa1 * b1[p] + v1 * k1[p])
                yt_ref[p] = jnp.where(lane == i0, y_before, jnp.where(lane == i1, y0, yt_ref[p]))
            return tuple(new)

        st = lax.fori_loop(0, CHUNK // 2, two_steps, tuple(s_ref[p] for p in range(N_PAIR)))
        for p in range(N_PAIR):
            s_ref[p] = st[p]

        if n_ride:
            @pl.when(pl.program_id(0) == nch - 1)
            def _():
                _exchange_wait(x_refs, o_refs, sems, ride_gather)

    row = pl.BlockSpec((CHUNK, RW_W), lambda i: (i, 0))
    big = pl.BlockSpec((CHUNK, N_PAIR, HEAD, LANES), lambda i: (i, 0, 0, 0))
    any_spec = pl.BlockSpec(memory_space=pl.ANY)
    return pl.pallas_call(
        body, name=name, grid=(nch,),
        in_specs=[row] * 7 + [big, pl.BlockSpec((2 * LANES, LANES), lambda i: (0, 0))] + [any_spec] * n_ride,
        out_specs=[big, pl.BlockSpec((N_PAIR, HEAD, LANES), lambda i: (0, 0, i))] + [any_spec] * n_ride,
        out_shape=[jax.ShapeDtypeStruct((t, N_PAIR, HEAD, LANES), F32), jax.ShapeDtypeStruct((N_PAIR, HEAD, nch * LANES), F32)]
        + _exchange_out_shape(ride_xs, ride_gather),
        scratch_shapes=[pltpu.VMEM((N_PAIR, HEAD, LANES), F32)] + [pltpu.VMEM((CHUNK, RW_W), F32)] * 6
        + (_exchange_sems(n_ride) if n_ride else []),
    )(r, r_prev, w, k, c, c_next, b, vb, bd2, *ride_xs)


def _rw_scan_bwd(r, w, k, c, b, vb, dyb, spre, bd, name, ride=None):
    t = r.shape[0]
    nch = t // CHUNK
    ride_xs, ride_gather = ride if ride is not None else ([], False)
    n_ride = len(ride_xs)

    def body(*refs):
        r_ref, w_ref, k_ref, c_ref, b_ref, vb_ref, dyb_ref, spre_ref, bd_ref = refs[:9]
        x_refs = refs[9:9 + n_ride]
        dr_ref, dw_ref, dk_ref, dc_ref, db_ref, dvt_ref = refs[9 + n_ride:15 + n_ride]
        o_refs = refs[15 + n_ride:15 + 2 * n_ride]
        g_ref, snext_ref = refs[15 + 2 * n_ride:17 + 2 * n_ride]
        sems = refs[17 + 2 * n_ride:]

        @pl.when(pl.program_id(0) == 0)
        def _():
            g_ref[...] = jnp.zeros_like(g_ref)
            snext_ref[...] = jnp.zeros_like(snext_ref)
            if n_ride:
                _exchange_start(x_refs, o_refs, sems, ride_gather)

        for ref in (dr_ref, dw_ref, dk_ref, dc_ref, db_ref, dvt_ref):
            ref[...] = jnp.zeros_like(ref)
        bdv = bd_ref[...]
        lane = lax.broadcasted_iota(jnp.int32, (HEAD, LANES), 1) % CHUNK

        def sum0(x):
            return jnp.sum(x, axis=0, keepdims=True)

        def step(n, gs):
            i = CHUNK - 1 - n
            rows = [ref[pl.ds(i, 1), :] for ref in (r_ref, w_ref, k_ref, c_ref, b_ref)]
            rr, ww, kk, cc, bb = [[x[:, p * LANES:(p + 1) * LANES] for p in range(N_PAIR)] for x in rows]
            sp = [spre_ref[i, p] for p in range(N_PAIR)]
            dy = [dyb_ref[i, p].astype(F32) for p in range(N_PAIR)]
            gs = [gs[p] + dy[p] * rr[p] for p in range(N_PAIR)]
            dsa_all = _group_sums([gs[p] * bb[p] for p in range(N_PAIR)], bdv)
            red = _group_sums([sp[p] * cc[p] for p in range(N_PAIR)] + [gs[p] * kk[p] for p in range(N_PAIR)], bdv)
            new = []
            for p in range(N_PAIR):
                dsa = dsa_all[p * HEAD:(p + 1) * HEAD]
                sa = red[p * HEAD:(p + 1) * HEAD]
                dv = red[(N_PAIR + p) * HEAD:(N_PAIR + p + 1) * HEAD]
                cols = pl.ds(p * LANES, LANES)
                _store_row(dr_ref, i, cols, sum0(snext_ref[p] * dy[p]))
                _store_row(dw_ref, i, cols, sum0(gs[p] * sp[p]))
                _store_row(db_ref, i, cols, sum0(gs[p] * sa))
                _store_row(dk_ref, i, cols, sum0(gs[p] * vb_ref[i, p].astype(F32)))
                _store_row(dc_ref, i, cols, sum0(sp[p] * dsa))
                dvt_ref[p] = jnp.where(lane == i, dv, dvt_ref[p])
                snext_ref[p] = sp[p]
                new.append(gs[p] * ww[p] + dsa * cc[p])
            return tuple(new)

        def steps(j, gs):
            for u in range(UNROLL):
                gs = step(j * UNROLL + u, gs)
            return gs

        gs = lax.fori_loop(0, CHUNK // UNROLL, steps, tuple(g_ref[p] for p in range(N_PAIR)))
        for p in range(N_PAIR):
            g_ref[p] = gs[p]

        if n_ride:
            @pl.when(pl.program_id(0) == nch - 1)
            def _():
                _exchange_wait(x_refs, o_refs, sems, ride_gather)

    row = pl.BlockSpec((CHUNK, RW_W), lambda i: (nch - 1 - i, 0))
    big = pl.BlockSpec((CHUNK, N_PAIR, HEAD, LANES), lambda i: (nch - 1 - i, 0, 0, 0))
    any_spec = pl.BlockSpec(memory_space=pl.ANY)
    return pl.pallas_call(
        body, name=name, grid=(nch,),
        in_specs=[row] * 5 + [big, big, big, pl.BlockSpec((2 * LANES, LANES), lambda i: (0, 0))] + [any_spec] * n_ride,
        out_specs=[row] * 5 + [pl.BlockSpec((N_PAIR, HEAD, LANES), lambda i: (0, 0, nch - 1 - i))] + [any_spec] * n_ride,
        out_shape=[jax.ShapeDtypeStruct((t, RW_W), F32)] * 5 + [jax.ShapeDtypeStruct((N_PAIR, HEAD, nch * LANES), F32)]
        + _exchange_out_shape(ride_xs, ride_gather),
        scratch_shapes=[pltpu.VMEM((N_PAIR, HEAD, LANES), F32), pltpu.VMEM((N_PAIR, HEAD, LANES), F32)]
        + (_exchange_sems(n_ride) if n_ride else []),
    )(r, w, k, c, b, vb, dyb, spre, bd, *ride_xs)


def _exchange(xs, *, gather, name):
    n = len(xs)

    def body(*refs):
        _exchange_start(refs[:n], refs[n:2 * n], refs[2 * n:], gather)
        _exchange_wait(refs[:n], refs[n:2 * n], refs[2 * n:], gather)

    any_spec = pl.BlockSpec(memory_space=pl.ANY)
    return pl.pallas_call(
        body, name=name, in_specs=[any_spec] * n, out_specs=[any_spec] * n, out_shape=_exchange_out_shape(xs, gather),
        scratch_shapes=_exchange_sems(n),
    )(*xs)


def _adamw_rows(g, w, m, v):
    m = ADAM_B1 * m + (1.0 - ADAM_B1) * g
    v = ADAM_B2 * v + (1.0 - ADAM_B2) * (g * g)
    m_hat = m / (1.0 - ADAM_B1 ** ADAM_STEP)
    v_hat = v / (1.0 - ADAM_B2 ** ADAM_STEP)
    return -ADAM_LR * (m_hat / (jnp.sqrt(v_hat) + ADAM_EPS) + ADAM_WD * w), m, v


def _sum_slots(parts):
    g = parts[0].astype(F32)
    for q in range(1, N_DEV):
        g = g + parts[q].astype(F32)
    return g


def _reduce_adamw(parts, w, m, v, name):
    rows, cols = w.shape

    def fn(parts, w, m, v):
        g = _sum_slots(parts)
        return (g,) + _adamw_rows(g, w, m, v)

    return _rowwise(fn, [parts, w, m, v], [], [(cols, F32)] * 4, [], tm=_pick(rows, (256, 128, 64, 32, 16, 8)), name=name)


def _shift(x, n):
    return jnp.pad(x, ((n, 0), (0, 0)))[:-n]


def _unshift(x, n):
    return jnp.pad(x, ((0, n), (0, 0)))[n:]


def _add_n(xs, *, tm, name):
    def fn(*vals):
        s = vals[0]
        for x in vals[1:]:
            s = s + x
        return s
    return _rowwise(fn, xs, [], [(xs[0].shape[1], F32)], [], tm=tm, name=name)[0]


def _norm_fwd(h, g, *, tm, name):
    return _rowwise(lambda x, gg: _rms(x, gg), [h], [g], [(h.shape[1], BF16)], [], tm=tm, name=name)[0]


def _res_norm_fwd(h, f, g, scale, *, tm, name):
    return _rowwise(lambda hh, ff, gg: hh + scale * _rms(ff, gg), [h, f], [g], [(h.shape[1], F32)], [], tm=tm, name=name)[0]


def _norm_bwd(x, g, dy, scale, res, out_dtype, *, tm, name):
    if res is None:
        def fn(xx, dd, gg):
            dx, dg = _rms_bwd(xx, gg, dd * scale)
            return dx, dg
        rows = [x, dy]
    else:
        def fn(xx, dd, rr, gg):
            dx, dg = _rms_bwd(xx, gg, dd * scale)
            return dx + rr, dg
        rows = [x, dy, res]
    return _rowwise(fn, rows, [g], [(x.shape[1], out_dtype)], [(1, x.shape[1])], tm=tm, name=name)


def _ffn_fwd(h, g_pre, g_post, w24, wo4, tiles, tag):
    tb, ts = tiles
    a = _norm_fwd(h, g_pre, tm=ts, name=f"{tag}_norm")
    gu, s4 = _ffn_in(a, w24, tm=tb, name=f"{tag}_in")
    f = _mm(s4, wo4, trans_b=False, tm=tb, tn=D_MODEL, out_dtype=F32, name=f"{tag}_out")
    h_new = _res_norm_fwd(h, f, g_post, 0.5, tm=ts, name=f"{tag}_res")
    return h_new, (h, a, gu, s4, f)


def _ffn_bwd(dh_new, res, g_pre, g_post, w24, wo4, tiles, tag):
    tb, ts = tiles
    h, a, gu, s4, f = res
    t = h.shape[0]
    df, dg_post = _norm_bwd(f, g_post, dh_new, 0.5, None, BF16, tm=ts, name=f"{tag}_dres")
    dgu = _ffn_dswiglu(df, wo4, gu, tm=tb, name=f"{tag}_dswiglu")
    d_wo = _mm_tn(s4, df[None], tk=FFN_BLK, name=f"{tag}_dwout")
    dgu8 = dgu.reshape(2 * w24.shape[1], t, FFN_BLK)
    w8 = w24.reshape(2 * w24.shape[1], D_MODEL, FFN_BLK)
    da = _mm(dgu8, w8, trans_b=True, tm=tb, tn=D_MODEL, out_dtype=F32, name=f"{tag}_da")
    d_win = _mm_tn(a[None], dgu8, tk=D_MODEL, name=f"{tag}_dwin")
    dh, dg_pre = _norm_bwd(h, g_pre, da, 1.0, dh_new, F32, tm=ts, name=f"{tag}_dnorm")
    return dh, dg_pre, dg_post, d_win, d_wo.reshape(N_DEV, -1, D_MODEL)


def _blockdiag(w4):
    n, b, _ = w4.shape
    eye = jnp.eye(n, dtype=w4.dtype)
    return (eye[:, None, :, None] * w4[:, :, None, :]).reshape(n * b, n * b)


def _blockdiag_grad(d):
    n = d.shape[0] // HEAD
    x = d.reshape(n, HEAD, n, HEAD)
    return jnp.stack([x[i, :, i, :] for i in range(n)])


def _row(v):
    return v.reshape(1, -1)


def _mixer_fwd(h, g_pre, g_post, wi, wo, P, bd, tiles, tag, ride=None):
    tb, ts = tiles
    a = _norm_fwd(h, g_pre, tm=ts, name=f"{tag}_norm")
    p = _mm(a[None], wi[None], trans_b=False, tm=tb, tn=N_IN, out_dtype=F32, name=f"{tag}_in")
    lx, lg = p[:, 0:256], p[:, 256:512]
    sb, scc, sx = p[:, 512:768], p[:, 768:1024], p[:, 1024:1280]
    z = p[:, 1280:]
    lxs = [lx, _shift(lx, 1), _shift(lx, 2), _shift(lx, 3)]
    cw = [_row(P['lru_conv_w'][kk]) for kk in range(4)]
    lru_par = cw + [_row(P['lru_conv_b']), _blockdiag(P['lru_wa']), _row(P['lru_ba']), _blockdiag(P['lru_wx']),
                    _row(P['lru_bx']), _row(P['lru_lambda'])]
    la, lb = _rowwise(_lru_pre, lxs, lru_par, [(LRU_W, F32)] * 2, [], tm=ts, name=f"{tag}_lru_pre")
    hs = _lru_scan(la, lb, name=f"{tag}_lru_scan")
    y_lru = _rowwise(lambda gg, hh, ng, b_: _lru_post(b_, gg, hh, ng), [lg, hs], [_row(P['lru_norm_g']), bd],
                     [(LRU_W, F32)], [], tm=ts, name=f"{tag}_lru_post")[0]
    sc_rows = [sb, scc, sx, _shift(scc, 1), _shift(sx, 1), _shift(scc, 2), _shift(sx, 2)]
    sc_par = [_row(P['sc_conv_w'][kk]) for kk in range(3)] + [_row(P['sc_norm_g'])]
    y_sc = _rowwise(lambda *v: _sc_fwd(v[-1], *v[:-1]), sc_rows, sc_par + [bd], [(SC_W, F32)], [], tm=ts,
                    name=f"{tag}_sc")[0]
    cuts = (0, RW_W, 2 * RW_W, 3 * RW_W, RW_IN)
    zs = [z[:, cuts[q]:cuts[q + 1]] for q in range(4)]
    z_rows = zs + [_shift(q, 1) for q in zs]
    pad = lambda m, lo: jnp.pad(m, ((lo, LANES - lo - m.shape[0]), (0, 0)))
    rw_par = [_row(P['rwkv_mu'][cuts[q]:cuts[q + 1]]) for q in range(4)]
    rw_par += [_row(P['rwkv_w0']), pad(P['rwkv_w2'], 0), _row(P['rwkv_a0']), pad(P['rwkv_a2'], 32),
               pad(P['rwkv_g2'], 64), _row(P['rwkv_k_k']), _row(P['rwkv_k_a'])]
    r, w, k2, c, b, v, g = _rowwise(lambda *vv: _rw_pre(vv[-1], *vv[:-1]), z_rows, rw_par + [bd], [(RW_W, F32)] * 7, [],
                                    tm=ts, name=f"{tag}_rw_pre")
    vb = _bcast_cols(v)
    spre, yt, *rode = _rw_scan(r, _shift(r, 1), w, k2, c, _unshift(c, 1), b, vb, _stacked_bf16(bd), name=f"{tag}_rw_scan",
                               ride=ride)
    y = _unshift(_cols_to_rows(yt), 1)
    post_par =[_row(P['rwkv_lnx_w']), _row(P['rwkv_lnx_b']), _row(P['rwkv_r_k'])]
    y_rw = _rowwise(lambda *vv: _rw_post(vv[-1], *vv[:-1]), [y, r, k2, v, g], post_par + [bd], [(RW_W, F32)], [], tm=ts,
                    name=f"{tag}_rw_post")[0]
    ycat = jnp.concatenate([y_lru, y_sc, y_rw], axis=1).astype(BF16)
    m = _mm(ycat[None], wo[None], trans_b=False, tm=tb, tn=D_MODEL, out_dtype=F32, name=f"{tag}_out")
    h_new = _res_norm_fwd(h, m, g_post, 1.0, tm=ts, name=f"{tag}_res")
    res = dict(h=h, a=a, m=m, ycat=ycat, lxs=lxs, lru_par=lru_par, lg=lg, la=la, hs=hs, sc_rows=sc_rows, sc_par=sc_par,
               z_rows=z_rows, rw_par=rw_par, r=r, w=w, k2=k2, c=c, b=b, v=v, g=g, vb=vb, spre=spre, y=y, post_par=post_par)
    return h_new, res, rode


def _mixer_bwd(dh_new, R, g_pre, g_post, wi, wo, P, bd, tiles, tag, ride=None):
    tb, ts = tiles
    dm, dg_post = _norm_bwd(R['m'], g_post, dh_new, 1.0, None, BF16, tm=ts, name=f"{tag}_dres")
    dycat = _mm(dm[None], wo[None], trans_b=True, tm=tb, tn=D_MODEL, out_dtype=F32, name=f"{tag}_dycat")
    d_wo = _mm_tn(R['ycat'][None], dm[None], tk=D_MODEL // 2, name=f"{tag}_dwout")[0]
    dy_lru, dy_sc, dy_rw = dycat[:, 0:256], dycat[:, 256:512], dycat[:, 512:]
    G = {}
    d_lg, d_hs, G['lru_norm_g'] = _rowwise(
        lambda gg, hh, ct, ng, b_: _vjp_rows(_lru_post, 1, 3, 1)(b_, gg, hh, ng, ct),
        [R['lg'], R['hs'], dy_lru], [_row(P['lru_norm_g']), bd], [(LRU_W, F32)] * 2, [(1, LRU_W)], tm=ts,
        name=f"{tag}_lru_dpost")
    d_la, d_lb = _lru_scan_bwd(R['la'], R['hs'], d_hs, name=f"{tag}_lru_dscan")

    def lru_pre_bwd(x0, x1, x2, x3, ca, cb_, *par):
        return _vjp_rows(_lru_pre, 0, 14, 2)(x0, x1, x2, x3, *par, ca, cb_)

    par_shapes = [tuple(q.shape) for q in R['lru_par']]
    outs = _rowwise(lru_pre_bwd, R['lxs'] + [d_la, d_lb], R['lru_par'], [(LRU_W, F32)] * 4, par_shapes, tm=ts,
                    name=f"{tag}_lru_dpre")
    dxs, dpar = outs[:4], outs[4:]
    d_lx = _add_n([dxs[0], _unshift(dxs[1], 1), _unshift(dxs[2], 2), _unshift(dxs[3], 3)], tm=ts, name=f"{tag}_lru_dx")
    G['lru_conv_w'] = jnp.concatenate(dpar[0:4], axis=0)
    G['lru_conv_b'] = dpar[4][0]
    G['lru_wa'] = _blockdiag_grad(dpar[5])
    G['lru_ba'] = dpar[6][0]
    G['lru_wx'] = _blockdiag_grad(dpar[7])
    G['lru_bx'] = dpar[8][0]
    G['lru_lambda'] = dpar[9][0]
    G['lru_norm_g'] = G['lru_norm_g'][0]

    def sc_bwd(*vv):
        rows7, ct, par4, b_ = vv[:7], vv[7], vv[8:12], vv[12]
        return _vjp_rows(_sc_fwd, 1, 11, 1)(b_, *rows7, *par4, ct)

    outs = _rowwise(sc_bwd, R['sc_rows'] + [dy_sc], R['sc_par'] + [bd], [(SC_W, F32)] * 7, [(1, SC_W)] * 4, tm=ts,
                    name=f"{tag}_sc_bwd")
    d_sb = outs[0]
    d_sc = _add_n([outs[1], _unshift(outs[3], 1), _unshift(outs[5], 2)], tm=ts, name=f"{tag}_sc_dc")
    d_sx = _add_n([outs[2], _unshift(outs[4], 1), _unshift(outs[6], 2)], tm=ts, name=f"{tag}_sc_dx")
    G['sc_conv_w'] = jnp.concatenate(outs[7:10], axis=0)
    G['sc_norm_g'] = outs[10][0]

    def rw_post_bwd(*vv):
        rows5, ct, par3, b_ = vv[:5], vv[5], vv[6:9], vv[9]
        return _vjp_rows(_rw_post, 1, 8, 1)(b_, *rows5, *par3, ct)

    outs = _rowwise(rw_post_bwd, [R['y'], R['r'], R['k2'], R['v'], R['g'], dy_rw], R['post_par'] + [bd],
                    [(RW_W, F32)] * 5, [(1, RW_W)] * 3, tm=ts, name=f"{tag}_rw_dpost")
    d_y, dr_p, dk_p, dv_p, d_g = outs[:5]
    G['rwkv_lnx_w'], G['rwkv_lnx_b'], G['rwkv_r_k'] = outs[5][0], outs[6][0], outs[7][0]
    dyb = _bcast_cols(d_y)
    dr_s, d_w, dk_s, d_c, d_b, dvt, *rode = _rw_scan_bwd(R['r'], R['w'], R['k2'], R['c'], R['b'], R['vb'], dyb, R['spre'],
                                                        _stacked_bf16(bd), name=f"{tag}_rw_dscan", ride=ride)
    dv_s = _cols_to_rows(dvt)

    def rw_pre_bwd(*vv):
        zrows = vv[0:8]
        dr1, dr2, dw_, dk1, dk2_, dc_, db_, dv1, dv2, dg_ = vv[8:18]
        par, b_ = vv[18:29], vv[29]
        return _vjp_rows(_rw_pre, 1, 19, 7)(b_, *zrows, *par, dr1 + dr2, dw_, dk1 + dk2_, dc_, db_, dv1 + dv2, dg_)

    par_shapes = [tuple(q.shape) for q in R['rw_par']]
    widths = [(q.shape[1], F32) for q in R['z_rows']]
    outs = _rowwise(rw_pre_bwd, R['z_rows'] + [dr_p, dr_s, d_w, dk_p, dk_s, d_c, d_b, dv_p, dv_s, d_g],
                    R['rw_par'] + [bd], widths, par_shapes, tm=ts, name=f"{tag}_rw_dpre")
    d_z = _add_n([jnp.concatenate(outs[0:4], axis=1), _unshift(jnp.concatenate(outs[4:8], axis=1), 1)], tm=ts,
                 name=f"{tag}_rw_dz")
    dpar = outs[8:]
    G['rwkv_mu'] = jnp.concatenate([q[0] for q in dpar[0:4]])
    G['rwkv_w0'], G['rwkv_a0'] = dpar[4][0], dpar[6][0]
    G['rwkv_w2'], G['rwkv_a2'], G['rwkv_g2'] = dpar[5][0:32], dpar[7][32:64], dpar[8][64:128]
    G['rwkv_k_k'], G['rwkv_k_a'] = dpar[9][0], dpar[10][0]

    dp = jnp.concatenate([d_lx, d_lg, d_sb, d_sc, d_sx, d_z], axis=1).astype(BF16)
    da = _mm(dp[None], wi[None], trans_b=True, tm=tb, tn=D_MODEL, out_dtype=F32, name=f"{tag}_da")
    d_wi = _mm_tn(R['a'][None], dp[None], tk=D_MODEL // 2, name=f"{tag}_dwin")[0]
    dh, dg_pre = _norm_bwd(R['h'], g_pre, da, 1.0, dh_new, F32, tm=ts, name=f"{tag}_dnorm")
    return dh, dg_pre, dg_post, d_wi, d_wo, G, rode


def _loss_rows(h, tgt, n_seq, *, tm, name):
    d = h.shape[1]

    def body(h_ref, t_ref, dh_ref, l_ref):
        i = pl.program_id(0)
        row = lax.broadcasted_iota(jnp.int32, (tm, 1), 0) + i * tm
        live = (row >= N_META) & (row < N_META + n_seq)
        e = jnp.where(live, h_ref[...] - t_ref[...], 0.0)
        dh_ref[...] = e * (1.0 / d)
        part = 0.5 * jnp.sum(jnp.sum(e * e, axis=1, keepdims=True) * (1.0 / d), axis=0, keepdims=True)

        @pl.when(i == 0)
        def _():
            l_ref[...] = part

        @pl.when(i > 0)
        def _():
            l_ref[...] += part

    blk = pl.BlockSpec((tm, d), lambda i: (i, 0))
    return pl.pallas_call(body, name=name, grid=(h.shape[0] // tm,), in_specs=[blk, blk],
                          out_specs=[blk, pl.BlockSpec((1, 1), lambda i: (0, 0))],
                          out_shape=[jax.ShapeDtypeStruct(h.shape, F32), jax.ShapeDtypeStruct((1, 1), F32)])(h, tgt)


def _pack(arrs, mult):
    flat = jnp.concatenate([a.reshape(-1).astype(F32) for a in arrs])
    n = flat.shape[0]
    tot = -(-n // mult) * mult
    return jnp.pad(flat, (0, tot - n)).reshape(-1, LANES)


def _unpack(buf, shapes):
    flat = buf.reshape(-1)
    out, off = [], 0
    for s in shapes:
        n = 1
        for q in s:
            n *= q
        out.append(flat[off:off + n].reshape(s))
        off += n
    return out


def _step(W, M, V, x, loss_target):
    n_seq = x.shape[1]
    t_real = N_META + n_seq
    t = (t_real // CHUNK + 1) * CHUNK
    tiles = (_pick(t, (704, 512, 256, 128, 64)), _pick(t, (192, 128, 64)))
    me = 4 * lax.axis_index("x") + 2 * lax.axis_index("y") + lax.axis_index("c")
    n_layer = W['norm_g'].shape[0]

    small_sh = list(SMALL_SHARDED)
    packed = _pack([W[n] for n in small_sh], 8 * LANES)
    gathered = _exchange([W[n][0].astype(BF16) for n in BIG] + [packed], gather=True, name="gather_weights")
    big8 = [dict(zip(BIG, gathered[:-1]))] + [None] * (n_layer - 1)
    pieces = [_unpack(gathered[-1][q], [W[n].shape for n in small_sh]) for q in range(N_DEV)]
    full = {n: W[n] for n in SMALL if n not in SMALL_SHARDED}
    for idx, n in enumerate(small_sh):
        full[n] = jnp.concatenate([pieces[q][idx] for q in range(N_DEV)], axis=SMALL_SHARDED[n])

    def layer_weights(l):
        w24_1 = big8[l]['ffn1_w_in'].reshape(2, N_DEV // 2, D_MODEL, FFN_BLK)
        wo4_1 = big8[l]['ffn1_w_out'].reshape(N_DEV // 2, FFN_BLK, D_MODEL)
        w24_2 = big8[l]['ffn2_w_in'].reshape(2, N_DEV // 2, D_MODEL, FFN_BLK)
        wo4_2 = big8[l]['ffn2_w_out'].reshape(N_DEV // 2, FFN_BLK, D_MODEL)
        wi = big8[l]['mix_w_in'].transpose(1, 0, 2).reshape(D_MODEL, N_IN)
        wo = big8[l]['mix_w_out'].reshape(D_MODEL, D_MODEL)
        return w24_1, wo4_1, w24_2, wo4_2, wi, wo

    bd = jnp.kron(jnp.eye(LANES // HEAD, dtype=F32), jnp.ones((HEAD, HEAD), F32))
    small_layer = [n for n in SMALL if n not in ('meta_tokens', 'norm_g')]

    h = jnp.concatenate([full['meta_tokens'], x[0], jnp.zeros((t - t_real, D_MODEL), F32)], axis=0)
    saved = []
    for l in range(n_layer):
        lw = layer_weights(l)
        ng = [_row(full['norm_g'][l, q]) for q in range(6)]
        P = {n: full[n][l] for n in small_layer}
        h, r1 = _ffn_fwd(h, ng[0], ng[1], lw[0], lw[1], tiles, f"l{l}_ffn1")
        ride = ([W[n][l + 1].astype(BF16) for n in BIG], True) if l + 1 < n_layer else None
        h, r2, rode = _mixer_fwd(h, ng[2], ng[3], lw[4], lw[5], P, bd, tiles, f"l{l}_mix", ride=ride)
        if ride is not None:
            big8[l + 1] = dict(zip(BIG, rode))
        h, r3 = _ffn_fwd(h, ng[4], ng[5], lw[2], lw[3], tiles, f"l{l}_ffn2")
        saved.append((lw, ng, P, r1, r2, r3))

    tgt = jnp.pad(loss_target[0], ((N_META, t - t_real), (0, 0)))
    dh, loss_part = _loss_rows(h, tgt, n_seq, tm=tiles[1], name="loss")
    loss = lax.psum(loss_part[0, 0], MESH_AXES)

    small_grads = [None] * n_layer
    norm_grads = [None] * n_layer
    recv = [None] * n_layer
    outgoing = None
    for l in reversed(range(n_layer)):
        lw, ng, P, r1, r2, r3 = saved[l]
        dh, g4, g5, d_win2, d_wo2 = _ffn_bwd(dh, r3, ng[4], ng[5], lw[2], lw[3], tiles, f"l{l}_ffn2")
        ride = (outgoing, False) if outgoing is not None else None
        dh, g2, g3, d_wi, d_wo, G, rode = _mixer_bwd(dh, r2, ng[2], ng[3], lw[4], lw[5], P, bd, tiles, f"l{l}_mix", ride=ride)
        if ride is not None:
            recv[l + 1] = rode
        dh, g0, g1, d_win1, d_wo1 = _ffn_bwd(dh, r1, ng[0], ng[1], lw[0], lw[1], tiles, f"l{l}_ffn1")
        small_grads[l] = G
        norm_grads[l] = jnp.concatenate([g0, g1, g2, g3, g4, g5], axis=0)
        d_wi8 = d_wi.reshape(D_MODEL, N_DEV, N_IN // N_DEV).transpose(1, 0, 2)
        d_wo8 = d_wo.reshape(N_DEV, D_MODEL // N_DEV, D_MODEL)
        outgoing = [d_win1, d_wo1, d_win2, d_wo2, d_wi8, d_wo8]
    recv[0] = _exchange(outgoing, gather=False, name="l0_grad_exchange")

    gs = {n: jnp.stack([small_grads[l][n] for l in range(n_layer)]) for n in small_layer}
    gs['norm_g'] = jnp.stack(norm_grads)
    gs['meta_tokens'] = dh[:N_META]
    gpack = _pack([gs[n] for n in SMALL], 8 * LANES)
    gall = _exchange([gpack], gather=True, name="gather_small_grads")[0]
    gsum = _rowwise(lambda parts: _sum_slots(parts), [gall], [], [(LANES, F32)], [], tm=gall.shape[1],
                    name="sum_small_grads")[0]
    gfull = dict(zip(SMALL, _unpack(gsum, [gs[n].shape for n in SMALL])))

    def my_shard(n, a):
        if n not in SMALL_SHARDED:
            return a
        ax = SMALL_SHARDED[n]
        size = a.shape[ax] // N_DEV
        return lax.dynamic_slice_in_dim(a, me * size, size, axis=ax)

    g_loc = [my_shard(n, gfull[n]) for n in SMALL]
    shapes = [W[n].shape for n in SMALL]
    bufs = [_pack(g_loc, 8 * LANES)] + [_pack([D[n] for n in SMALL], 8 * LANES) for D in (W, M, V)]
    d_s, m_s, v_s = _rowwise(_adamw_rows, bufs, [], [(LANES, F32)] * 3, [], tm=bufs[0].shape[0], name="adamw_small")
    out = {'grad': dict(zip(SMALL, g_loc)), 'delta': dict(zip(SMALL, _unpack(d_s, shapes))),
           'm': dict(zip(SMALL, _unpack(m_s, shapes))), 'v': dict(zip(SMALL, _unpack(v_s, shapes)))}

    order = ['ffn1_w_in', 'ffn1_w_out', 'ffn2_w_in', 'ffn2_w_out', 'mix_w_in', 'mix_w_out']
    for idx, n in enumerate(order):
        per_layer = []
        for l in range(n_layer):
            parts = recv[l][idx]
            rows, cols = W[n].shape[1], W[n].shape[2]
            per_layer.append(_reduce_adamw(parts.reshape(N_DEV, rows, cols), W[n][l], M[n][l], V[n][l],
                                           name=f"l{l}_adamw_{n}"))
        for q, key in enumerate(('grad', 'delta', 'm', 'v')):
            out[key][n] = jnp.stack([per_layer[l][q] for l in range(n_layer)])

    return (loss, dh[N_META:t_real][None],
            *[out['grad'][n] for n in WEIGHTS], *[out['delta'][n] for n in WEIGHTS],
            *[out['m'][n] for n in WEIGHTS], *[out['v'][n] for n in WEIGHTS])


def kernel(x, meta_tokens, norm_g, ffn1_w_in, ffn1_w_out, ffn2_w_in, ffn2_w_out, mix_w_in, mix_w_out, lru_conv_w, lru_conv_b, lru_wa, lru_ba, lru_wx, lru_bx, lru_lambda, lru_norm_g, sc_conv_w, sc_norm_g, rwkv_mu, rwkv_w0, rwkv_w2, rwkv_a0, rwkv_a2, rwkv_g2, rwkv_k_k, rwkv_k_a, rwkv_r_k, rwkv_lnx_w, rwkv_lnx_b, loss_target, m_meta_tokens, m_norm_g, m_ffn1_w_in, m_ffn1_w_out, m_ffn2_w_in, m_ffn2_w_out, m_mix_w_in, m_mix_w_out, m_lru_conv_w, m_lru_conv_b, m_lru_wa, m_lru_ba, m_lru_wx, m_lru_bx, m_lru_lambda, m_lru_norm_g, m_sc_conv_w, m_sc_norm_g, m_rwkv_mu, m_rwkv_w0, m_rwkv_w2, m_rwkv_a0, m_rwkv_a2, m_rwkv_g2, m_rwkv_k_k, m_rwkv_k_a, m_rwkv_r_k, m_rwkv_lnx_w, m_rwkv_lnx_b, v_meta_tokens, v_norm_g, v_ffn1_w_in, v_ffn1_w_out, v_ffn2_w_in, v_ffn2_w_out, v_mix_w_in, v_mix_w_out, v_lru_conv_w, v_lru_conv_b, v_lru_wa, v_lru_ba, v_lru_wx, v_lru_bx, v_lru_lambda, v_lru_norm_g, v_sc_conv_w, v_sc_norm_g, v_rwkv_mu, v_rwkv_w0, v_rwkv_w2, v_rwkv_a0, v_rwkv_a2, v_rwkv_g2, v_rwkv_k_k, v_rwkv_k_a, v_rwkv_r_k, v_rwkv_lnx_w, v_rwkv_lnx_b):
    w_vals = (meta_tokens, norm_g, ffn1_w_in, ffn1_w_out, ffn2_w_in, ffn2_w_out, mix_w_in, mix_w_out, lru_conv_w, lru_conv_b, lru_wa, lru_ba, lru_wx, lru_bx, lru_lambda, lru_norm_g, sc_conv_w, sc_norm_g, rwkv_mu, rwkv_w0, rwkv_w2, rwkv_a0, rwkv_a2, rwkv_g2, rwkv_k_k, rwkv_k_a, rwkv_r_k, rwkv_lnx_w, rwkv_lnx_b)
    m_vals = (m_meta_tokens, m_norm_g, m_ffn1_w_in, m_ffn1_w_out, m_ffn2_w_in, m_ffn2_w_out, m_mix_w_in, m_mix_w_out, m_lru_conv_w, m_lru_conv_b, m_lru_wa, m_lru_ba, m_lru_wx, m_lru_bx, m_lru_lambda, m_lru_norm_g, m_sc_conv_w, m_sc_norm_g, m_rwkv_mu, m_rwkv_w0, m_rwkv_w2, m_rwkv_a0, m_rwkv_a2, m_rwkv_g2, m_rwkv_k_k, m_rwkv_k_a, m_rwkv_r_k, m_rwkv_lnx_w, m_rwkv_lnx_b)
    v_vals = (v_meta_tokens, v_norm_g, v_ffn1_w_in, v_ffn1_w_out, v_ffn2_w_in, v_ffn2_w_out, v_mix_w_in, v_mix_w_out, v_lru_conv_w, v_lru_conv_b, v_lru_wa, v_lru_ba, v_lru_wx, v_lru_bx, v_lru_lambda, v_lru_norm_g, v_sc_conv_w, v_sc_norm_g, v_rwkv_mu, v_rwkv_w0, v_rwkv_w2, v_rwkv_a0, v_rwkv_a2, v_rwkv_g2, v_rwkv_k_k, v_rwkv_k_a, v_rwkv_r_k, v_rwkv_lnx_w, v_rwkv_lnx_b)
    return _step(dict(zip(WEIGHTS, w_vals)), dict(zip(WEIGHTS, m_vals)), dict(zip(WEIGHTS, v_vals)), x, loss_target)
```

```python
import functools

import jax
import jax.numpy as jnp
from jax import lax
from jax.experimental import pallas as pl
from jax.experimental.pallas import tpu as pltpu

F32 = jnp.float32
BF16 = jnp.bfloat16
HIGHEST = lax.Precision.HIGHEST

N_DEV = 8
MESH_AXES = ("x", "y", "c")
N_META = 16
D_MODEL = 1024
LRU_W = 256
SC_W = 256
RW_W = 512
HEAD = 64
LANES = 128
CHUNK = 64
RW_IN = 1664
N_IN = 2944
FFN_BLK = 704
RMS_EPS = 1e-6
LNX_EPS = 64e-5
LRU_C = 8.0
ADAM_LR, ADAM_B1, ADAM_B2, ADAM_EPS, ADAM_WD, ADAM_STEP = 0.001, 0.9, 0.999, 1e-08, 0.01, 10

WEIGHTS = ['meta_tokens', 'norm_g', 'ffn1_w_in', 'ffn1_w_out', 'ffn2_w_in', 'ffn2_w_out', 'mix_w_in', 'mix_w_out',
           'lru_conv_w', 'lru_conv_b', 'lru_wa', 'lru_ba', 'lru_wx', 'lru_bx', 'lru_lambda', 'lru_norm_g',
           'sc_conv_w', 'sc_norm_g', 'rwkv_mu', 'rwkv_w0', 'rwkv_w2', 'rwkv_a0', 'rwkv_a2', 'rwkv_g2', 'rwkv_k_k',
           'rwkv_k_a', 'rwkv_r_k', 'rwkv_lnx_w', 'rwkv_lnx_b']
BIG = ['ffn1_w_in', 'ffn1_w_out', 'ffn2_w_in', 'ffn2_w_out', 'mix_w_in', 'mix_w_out']
SMALL_SHARDED = {'meta_tokens': 1, 'norm_g': 2, 'lru_conv_w': 2, 'sc_conv_w': 2, 'rwkv_w2': 2, 'rwkv_a2': 2, 'rwkv_g2': 2}
SMALL = [n for n in WEIGHTS if n not in BIG]


def _pick(n, cands):
    for c in cands:
        if n % c == 0:
            return c
    raise ValueError(f"no tile for {n}")


def _rowwise(fn, rows, params, row_outs, acc_outs, *, tm, name):
    nr, npar, nro, nao = len(rows), len(params), len(row_outs), len(acc_outs)
    n_rows = rows[0].shape[-2]
    assert n_rows % tm == 0, (name, n_rows, tm)

    def body(*refs):
        vals = [r[...] for r in refs[:nr + npar]]
        outs = fn(*vals)
        if not isinstance(outs, (tuple, list)):
            outs = (outs,)
        assert len(outs) == nro + nao, (name, len(outs))
        for o_ref, o in zip(refs[nr + npar:nr + npar + nro], outs[:nro]):
            o_ref[...] = o.astype(o_ref.dtype)
        step = pl.program_id(0)
        for a_ref, a in zip(refs[nr + npar + nro:], outs[nro:]):
            @pl.when(step == 0)
            def _(a_ref=a_ref, a=a):
                a_ref[...] = a.astype(F32)

            @pl.when(step > 0)
            def _(a_ref=a_ref, a=a):
                a_ref[...] += a.astype(F32)

    def row_spec(shape):
        if len(shape) == 2:
            return pl.BlockSpec((tm, shape[1]), lambda i: (i, 0))
        return pl.BlockSpec((shape[0], tm, shape[2]), lambda i: (0, i, 0))

    def full_spec(shape):
        nd = len(shape)
        return pl.BlockSpec(tuple(shape), lambda i, nd=nd: (0,) * nd)

    in_specs = [row_spec(r.shape) for r in rows] + [full_spec(p.shape) for p in params]
    out_shape = [jax.ShapeDtypeStruct((n_rows, w), dt) for (w, dt) in row_outs]
    out_shape += [jax.ShapeDtypeStruct(tuple(s), F32) for s in acc_outs]
    out_specs = [row_spec((n_rows, w)) for (w, _) in row_outs] + [full_spec(s) for s in acc_outs]
    res = pl.pallas_call(body, name=name, grid=(n_rows // tm,), in_specs=in_specs, out_specs=out_specs,
                         out_shape=out_shape)(*rows, *params)
    return tuple(res)


def _mm(a3, b3, *, trans_b, tm, tn, out_dtype, name):
    nj, m, kb = a3.shape
    n = b3.shape[1] if trans_b else b3.shape[2]
    dims = (((1,), (1,)), ((), ())) if trans_b else (((1,), (0,)), ((), ()))

    def body(a_ref, b_ref, o_ref, acc_ref):
        j = pl.program_id(2)

        @pl.when(j == 0)
        def _():
            acc_ref[...] = jnp.zeros_like(acc_ref)

        acc_ref[...] += lax.dot_general(a_ref[0], b_ref[0], dims, preferred_element_type=F32)

        @pl.when(j == nj - 1)
        def _():
            o_ref[...] = acc_ref[...].astype(o_ref.dtype)

    if trans_b:
        b_spec = pl.BlockSpec((1, tn, kb), lambda i, c, j: (j, c, 0))
    else:
        b_spec = pl.BlockSpec((1, kb, tn), lambda i, c, j: (j, 0, c))
    return pl.pallas_call(
        body, name=name, grid=(m // tm, n // tn, nj),
        in_specs=[pl.BlockSpec((1, tm, kb), lambda i, c, j: (j, i, 0)), b_spec],
        out_specs=pl.BlockSpec((tm, tn), lambda i, c, j: (i, c)),
        out_shape=jax.ShapeDtypeStruct((m, n), out_dtype),
        scratch_shapes=[pltpu.VMEM((tm, tn), F32)],
    )(a3, b3)


def _mm_tn(a3, b3, *, tk, name):
    ja, t, ka = a3.shape
    jb, _, n = b3.shape
    nj = max(ja, jb)

    def body(a_ref, b_ref, o_ref):
        o_ref[0] = lax.dot_general(a_ref[0], b_ref[0], (((0,), (0,)), ((), ())),
                                   preferred_element_type=F32).astype(o_ref.dtype)

    return pl.pallas_call(
        body, name=name, grid=(nj, ka // tk),
        in_specs=[pl.BlockSpec((1, t, tk), (lambda j, c: (j, 0, c)) if ja > 1 else (lambda j, c: (0, 0, c))),
                  pl.BlockSpec((1, t, n), (lambda j, c: (j, 0, 0)) if jb > 1 else (lambda j, c: (0, 0, 0)))],
        out_specs=pl.BlockSpec((1, tk, n), lambda j, c: (j, c, 0)),
        out_shape=jax.ShapeDtypeStruct((nj, ka, n), BF16),
    )(a3, b3)


def _ffn_in(a, w24, *, tm, name):
    t, d = a.shape
    nb, fb = w24.shape[1], w24.shape[3]

    def body(a_ref, w_ref, gu_ref, s_ref):
        x = a_ref[...]
        g = jnp.dot(x, w_ref[0, 0], preferred_element_type=F32)
        u = jnp.dot(x, w_ref[1, 0], preferred_element_type=F32)
        gu_ref[0, 0] = g.astype(BF16)
        gu_ref[1, 0] = u.astype(BF16)
        s_ref[0] = (g * jax.nn.sigmoid(g) * u).astype(BF16)

    return pl.pallas_call(
        body, name=name, grid=(nb, t // tm),
        in_specs=[pl.BlockSpec((tm, d), lambda j, i: (i, 0)), pl.BlockSpec((2, 1, d, fb), lambda j, i: (0, j, 0, 0))],
        out_specs=[pl.BlockSpec((2, 1, tm, fb), lambda j, i: (0, j, i, 0)), pl.BlockSpec((1, tm, fb), lambda j, i: (j, i, 0))],
        out_shape=[jax.ShapeDtypeStruct((2, nb, t, fb), BF16), jax.ShapeDtypeStruct((nb, t, fb), BF16)],
    )(a, w24)


def _ffn_dswiglu(df, wo4, gu, *, tm, name):
    t, d = df.shape
    nb, fb = wo4.shape[0], wo4.shape[1]

    def body(df_ref, wo_ref, gu_ref, dg_ref):
        ds = lax.dot_general(df_ref[...], wo_ref[0], (((1,), (1,)), ((), ())), preferred_element_type=F32)
        g = gu_ref[0, 0].astype(F32)
        u = gu_ref[1, 0].astype(F32)
        sig = jax.nn.sigmoid(g)
        dg_ref[0, 0] = (ds * u * sig * (1.0 + g * (1.0 - sig))).astype(BF16)
        dg_ref[1, 0] = (ds * g * sig).astype(BF16)

    return pl.pallas_call(
        body, name=name, grid=(nb, t // tm),
        in_specs=[pl.BlockSpec((tm, d), lambda j, i: (i, 0)), pl.BlockSpec((1, fb, d), lambda j, i: (j, 0, 0)),
                  pl.BlockSpec((2, 1, tm, fb), lambda j, i: (0, j, i, 0))],
        out_specs=pl.BlockSpec((2, 1, tm, fb), lambda j, i: (0, j, i, 0)),
        out_shape=jax.ShapeDtypeStruct((2, nb, t, fb), BF16),
    )(df, wo4, gu)


def _rms(x, g):
    return x * lax.rsqrt(jnp.mean(x * x, axis=-1, keepdims=True) + RMS_EPS) * g


def _rms_bwd(x, g, dy):
    rstd = lax.rsqrt(jnp.mean(x * x, axis=-1, keepdims=True) + RMS_EPS)
    xh = x * rstd
    dxh = dy * g
    dx = rstd * (dxh - xh * jnp.mean(dxh * xh, axis=-1, keepdims=True))
    return dx, jnp.sum(dy * xh, axis=0, keepdims=True)


def _seg_sum_impl(x, bd):
    parts = [jnp.dot(x[:, q * LANES:(q + 1) * LANES], bd, preferred_element_type=F32, precision=HIGHEST)
             for q in range(x.shape[1] // LANES)]
    return parts[0] if len(parts) == 1 else jnp.concatenate(parts, axis=1)


@jax.custom_vjp
def _seg_sum(x, bd):
    return _seg_sum_impl(x, bd)


def _seg_sum_fwd(x, bd):
    return _seg_sum_impl(x, bd), bd


def _seg_sum_bwd(bd, ct):
    return _seg_sum_impl(ct, bd), jnp.zeros_like(bd)


_seg_sum.defvjp(_seg_sum_fwd, _seg_sum_bwd)


def _group_rms(y, g, bd):
    return y * lax.rsqrt(_seg_sum(y * y, bd) * (1.0 / HEAD) + RMS_EPS) * g


def _expm1(x):
    return jnp.where(jnp.abs(x) < 1e-2, x * (1.0 + x * (0.5 + x * (1.0 / 6.0))), jnp.exp(x) - 1.0)


def _lru_pre(x0, x1, x2, x3, cw0, cw1, cw2, cw3, cb, wa, ba, wx, bx, lam):
    u = x3 * cw0 + x2 * cw1 + x1 * cw2 + x0 * cw3 + cb
    r = jax.nn.sigmoid(jnp.dot(u, wa, preferred_element_type=F32, precision=HIGHEST) + ba)
    i = jax.nn.sigmoid(jnp.dot(u, wx, preferred_element_type=F32, precision=HIGHEST) + bx)
    log_a = -LRU_C * r * jax.nn.softplus(-lam)
    return jnp.exp(log_a), jnp.sqrt(-_expm1(2.0 * log_a)) * (i * u)


def _lru_post(bd, gate, hs, ng):
    return _group_rms(jax.nn.gelu(gate) * hs, ng, bd)


def _sc_fwd(bd, b, c0, x0, c1, x1, c2, x2, w0, w1, w2, ng):
    return _group_rms(b * (w0 * (c2 * x2) + w1 * (c1 * x1) + w2 * (c0 * x0)), ng, bd)


def _rw_pre(bd, zr, zk, zv, zt, sr, sk, sv, st, mur, muk, muv, mut, w0, w2p, a0, a2p, g2p, k_k, k_a):
    r, k, v, tail = zr + (sr - zr) * mur, zk + (sk - zk) * muk, zv + (sv - zv) * muv, zt + (st - zt) * mut
    lane = lax.broadcasted_iota(jnp.int32, tail.shape, 1)
    act = jnp.where(lane < 32, jnp.tanh(tail), jnp.where(lane < 64, tail, jax.nn.sigmoid(tail)))
    dot = functools.partial(jnp.dot, preferred_element_type=F32, precision=HIGHEST)
    w_log = -jax.nn.softplus(-(w0 + dot(act, w2p))) - 0.5
    w = jnp.exp(-jnp.exp(w_log))
    a = jax.nn.sigmoid(a0 + dot(act, a2p))
    g = dot(act, g2p)
    kk = k * k_k
    k2 = k * (1.0 + (a - 1.0) * k_a)
    kkn = kk * lax.rsqrt(jnp.maximum(_seg_sum(kk * kk, bd), 1e-24))
    return r, w, k2, -kkn, kkn * a, v, g


def _rw_post(bd, y, r, k2, v, g, lnw, lnb, r_k):
    mean = _seg_sum(y, bd) * (1.0 / HEAD)
    yc = y - mean
    var = _seg_sum(yc * yc, bd) * (1.0 / HEAD)
    yn = yc * lax.rsqrt(var + LNX_EPS) * lnw + lnb
    return (yn + _seg_sum(r * k2 * r_k, bd) * v) * g


def _vjp_rows(fwd, n_static, n_in, n_ct):
    def fn(*args):
        static, prim, cts = args[:n_static], args[n_static:n_static + n_in], args[n_static + n_in:]
        assert len(cts) == n_ct
        _, vjp = jax.vjp(functools.partial(fwd, *static), *prim)
        return vjp(cts[0] if n_ct == 1 else tuple(cts))
    return fn


def _exchange_copies(x_refs, o_refs, sems, gather):
    send_sems, recv_sems, local_sems = sems
    mx, my, mc = lax.axis_index("x"), lax.axis_index("y"), lax.axis_index("c")
    me = 4 * mx + 2 * my + mc
    local, sends, recvs = [], [], []
    for k in range(len(x_refs)):
        local.append(pltpu.make_async_copy(x_refs[k] if gather else x_refs[k].at[me], o_refs[k].at[me], local_sems.at[k]))
    for d in range(1, N_DEV):
        px, py, pc = mx ^ ((d >> 2) & 1), my ^ ((d >> 1) & 1), mc ^ (d & 1)
        peer = 4 * px + 2 * py + pc
        for k in range(len(x_refs)):
            src = x_refs[k] if gather else x_refs[k].at[peer]
            common = dict(src_ref=src, send_sem=send_sems.at[k, d - 1], recv_sem=recv_sems.at[k, d - 1],
                          device_id=(px, py, pc), device_id_type=pl.DeviceIdType.MESH)
            sends.append(pltpu.make_async_remote_copy(dst_ref=o_refs[k].at[me], **common))
            recvs.append(pltpu.make_async_remote_copy(dst_ref=o_refs[k].at[peer], **common))
    return local, sends, recvs


def _gather2_copies(x_refs, o_refs, sems):
    send_sems, recv_sems, local_sems = sems
    mx, my, mc = lax.axis_index("x"), lax.axis_index("y"), lax.axis_index("c")
    sibling = (mx, my, 1 - mc)
    chips = [(1 - mx, my), (mx, 1 - my), (1 - mx, 1 - my)]

    def slot(px, py, pc):
        return 4 * px + 2 * py + pc

    out = dict(local=[], first=[], first_recv=[], ici_recv=[], passed=[], passed_recv=[])
    for k in range(len(x_refs)):
        def copy(sem, src, dst_slot, to, k=k):
            return pltpu.make_async_remote_copy(src_ref=src, dst_ref=o_refs[k].at[dst_slot], send_sem=send_sems.at[k, sem],
                                                recv_sem=recv_sems.at[k, sem], device_id=to, device_id_type=pl.DeviceIdType.MESH)
        me = slot(mx, my, mc)
        out['local'].append(pltpu.make_async_copy(x_refs[k], o_refs[k].at[me], local_sems.at[k]))
        out['first'].append(copy(0, x_refs[k], me, sibling))
        out['first_recv'].append(copy(0, x_refs[k], slot(mx, my, 1 - mc), sibling))
        for j, (px, py) in enumerate(chips):
            out['first'].append(copy(1 + j, x_refs[k], me, (px, py, mc)))
            out['ici_recv'].append(copy(1 + j, x_refs[k], slot(px, py, mc), (px, py, mc)))
            out['passed'].append(copy(4 + j, o_refs[k].at[slot(px, py, mc)], slot(px, py, mc), sibling))
            out['passed_recv'].append(copy(4 + j, x_refs[k], slot(px, py, 1 - mc), sibling))
    return out


def _exchange_start(x_refs, o_refs, sems, gather):
    if gather:
        cps = _gather2_copies(x_refs, o_refs, sems)
        for cp in cps['local'] + cps['first']:
            cp.start()
        return
    local, sends, _ = _exchange_copies(x_refs, o_refs, sems, gather)
    for cp in local + sends:
        cp.start()


def _exchange_wait(x_refs, o_refs, sems, gather):
    if gather:
        cps = _gather2_copies(x_refs, o_refs, sems)
        for arrived, onward in zip(cps['ici_recv'], cps['passed']):
            arrived.wait_recv()
            onward.start()
        for cp in cps['first'] + cps['passed']:
            cp.wait_send()
        for cp in cps['first_recv'] + cps['passed_recv']:
            cp.wait_recv()
        for cp in cps['local']:
            cp.wait()
        return
    local, sends, recvs = _exchange_copies(x_refs, o_refs, sems, gather)
    for cp in sends:
        cp.wait_send()
    for cp in recvs:
        cp.wait_recv()
    for cp in local:
        cp.wait()


def _exchange_out_shape(xs, gather):
    return [jax.ShapeDtypeStruct(((N_DEV,) + x.shape) if gather else x.shape, x.dtype) for x in xs]


def _exchange_sems(n):
    return [pltpu.SemaphoreType.DMA((n, N_DEV - 1)), pltpu.SemaphoreType.DMA((n, N_DEV - 1)), pltpu.SemaphoreType.DMA((n,))]


SUBLANES = 8


def _store_row(ref, i, cols, row):
    base = pl.multiple_of((i // SUBLANES) * SUBLANES, SUBLANES)
    sub = lax.broadcasted_iota(jnp.int32, (SUBLANES, row.shape[1]), 0)
    ref[pl.ds(base, SUBLANES), cols] = jnp.where(sub == i % SUBLANES, row, ref[pl.ds(base, SUBLANES), cols])


def _tile_scan(a, b, reverse):
    sub = lax.broadcasted_iota(jnp.int32, a.shape, 0)
    for sh in (1, 2, 4):
        if reverse:
            live = sub < SUBLANES - sh
            a_s, b_s = pltpu.roll(a, SUBLANES - sh, 0), pltpu.roll(b, SUBLANES - sh, 0)
        else:
            live = sub >= sh
            a_s, b_s = pltpu.roll(a, sh, 0), pltpu.roll(b, sh, 0)
        b = jnp.where(live, a * b_s, 0.0) + b
        a = jnp.where(live, a * a_s, a)
    return a, b


def _lru_scan(a, b, name):
    t, w = a.shape

    def body(a_ref, b_ref, h_ref):
        def tile(j, h):
            rows = pl.ds(pl.multiple_of(j * SUBLANES, SUBLANES), SUBLANES)
            ca, cb = _tile_scan(a_ref[rows, :], b_ref[rows, :], False)
            out = ca * h + cb
            h_ref[rows, :] = out
            return out[SUBLANES - 1:SUBLANES]
        lax.fori_loop(0, t // SUBLANES, tile, jnp.zeros((1, w), F32))

    return pl.pallas_call(body, name=name, out_shape=jax.ShapeDtypeStruct((t, w), F32))(a, b)


def _lru_scan_bwd(a_next, h_prev, dhs, name):
    t, w = dhs.shape

    def body(a_ref, h_ref, dh_ref, da_ref, db_ref):
        def tile(n, lam):
            rows = pl.ds(pl.multiple_of((t // SUBLANES - 1 - n) * SUBLANES, SUBLANES), SUBLANES)
            ca, cb = _tile_scan(a_ref[rows, :], dh_ref[rows, :], True)
            out = ca * lam + cb
            db_ref[rows, :] = out
            da_ref[rows, :] = out * h_ref[rows, :]
            return out[0:1]
        lax.fori_loop(0, t // SUBLANES, tile, jnp.zeros((1, w), F32))

    return pl.pallas_call(body, name=name, out_shape=[jax.ShapeDtypeStruct((t, w), F32)] * 2)(a_next, h_prev, dhs)


N_PAIR = RW_W // LANES
UNROLL = 2


def _bcast_cols(v):
    t = v.shape[0]
    x = v.reshape(t, N_PAIR, 2, HEAD)
    shape = (t, N_PAIR, HEAD, LANES)
    lane = lax.broadcasted_iota(jnp.int32, shape, 3)
    return jnp.where(lane < HEAD, jnp.broadcast_to(x[:, :, 0, :, None], shape), jnp.broadcast_to(x[:, :, 1, :, None], shape))


def _cols_to_rows(yt):
    nch = yt.shape[2] // LANES
    x = yt.reshape(N_PAIR, HEAD, nch, 2, CHUNK).transpose(2, 4, 0, 3, 1)
    return x.reshape(nch * CHUNK, RW_W)


def _stacked_bf16(bd):
    return jnp.concatenate([bd, bd], axis=0).astype(BF16)


def _group_sums(prods, bd2):
    x = jnp.concatenate(prods, axis=0)
    hi = x.astype(BF16)
    lo = (x - hi.astype(F32)).astype(BF16)
    return jnp.dot(jnp.concatenate([hi, lo], axis=1), bd2, preferred_element_type=F32)


def _rw_scan(r, r_prev, w, k, c, c_next, b, vb, bd2, name, ride=None):
    t = w.shape[0]
    nch = t // CHUNK
    ride_xs, ride_gather = ride if ride is not None else ([], False)
    n_ride = len(ride_xs)

    def body(*refs):
        r_ref, rp_ref, w_ref, k_ref, c_ref, cn_ref, b_ref, vb_ref, bd_ref = refs[:9]
        x_refs = refs[9:9 + n_ride]
        spre_ref, yt_ref = refs[9 + n_ride:11 + n_ride]
        o_refs = refs[11 + n_ride:11 + 2 * n_ride]
        s_ref, wc_ref, wr_ref, bc_ref, kc_ref, br_ref, kr_ref = refs[11 + 2 * n_ride:18 + 2 * n_ride]
        sems = refs[18 + 2 * n_ride:]

        @pl.when(pl.program_id(0) == 0)
        def _():
            s_ref[...] = jnp.zeros_like(s_ref)
            if n_ride:
                _exchange_start(x_refs, o_refs, sems, ride_gather)

        yt_ref[...] = jnp.zeros_like(yt_ref)
        bdv = bd_ref[...]
        lane = lax.broadcasted_iota(jnp.int32, (HEAD, LANES), 1) % CHUNK

        wv, cn, rv, bv, kv = w_ref[...], cn_ref[...], r_ref[...], b_ref[...], k_ref[...]
        wc_ref[...] = wv * cn
        wr_ref[...] = wv * rv
        for ref, x in ((bc_ref, bv * cn), (kc_ref, kv * cn), (br_ref, bv * rv), (kr_ref, kv * rv)):
            sums = _group_sums([x[:, q * LANES:(q + 1) * LANES] for q in range(N_PAIR)], bdv)
            for q in range(N_PAIR):
                ref[:, q * LANES:(q + 1) * LANES] = sums[q * CHUNK:(q + 1) * CHUNK]

        def cut(ref, i):
            x = ref[pl.ds(i, 1), :]
            return [x[:, p * LANES:(p + 1) * LANES] for p in range(N_PAIR)]

        def two_steps(j, st):
            i0 = 2 * j
            i1 = i0 + 1
            c0, wc0, rp0, wr0 = cut(c_ref, i0), cut(wc_ref, i0), cut(rp_ref, i0), cut(wr_ref, i0)
            w0, b0, k0, w1, b1, k1 = cut(w_ref, i0), cut(b_ref, i0), cut(k_ref, i0), cut(w_ref, i1), cut(b_ref, i1), cut(k_ref, i1)
            bc0, kc0, br0, kr0 = cut(bc_ref, i0), cut(kc_ref, i0), cut(br_ref, i0), cut(kr_ref, i0)
            pairs = range(N_PAIR)
            red = _group_sums([st[p] * c0[p] for p in pairs] + [st[p] * wc0[p] for p in pairs], bdv)
            out = _group_sums([st[p] * rp0[p] for p in pairs] + [st[p] * wr0[p] for p in pairs], bdv)
            new = []
            for p in pairs:
                v0, v1 = vb_ref[i0, p], vb_ref[i1, p]
                sa0 = red[p * HEAD:(p + 1) * HEAD]
                sa1 = red[(N_PAIR + p) * HEAD:(N_PAIR + p + 1) * HEAD] + sa0 * bc0[p] + v0 * kc0[p]
                y_before = out[p * HEAD:(p + 1) * HEAD]
                y0 = out[(N_PAIR + p) * HEAD:(N_PAIR + p + 1) * HEAD] + sa0 * br0[p] + v0 * kr0[p]
                spre_ref[i0, p] = st[p]
                s1 = st[p] * w0[p] + sa0 * b0[p] + v0 * k0[p]
                spre_ref[i1, p] = s1
                new.append(s1 * w1[p] + sa1 * b1[p] + v1 * k1[p])
                yt_ref[p] = jnp.where(lane == i0, y_before, jnp.where(lane == i1, y0, yt_ref[p]))
            return tuple(new)

        st = lax.fori_loop(0, CHUNK // 2, two_steps, tuple(s_ref[p] for p in range(N_PAIR)))
        for p in range(N_PAIR):
            s_ref[p] = st[p]

        if n_ride:
            @pl.when(pl.program_id(0) == nch - 1)
            def _():
                _exchange_wait(x_refs, o_refs, sems, ride_gather)

    row = pl.BlockSpec((CHUNK, RW_W), lambda i: (i, 0))
    big = pl.BlockSpec((CHUNK, N_PAIR, HEAD, LANES), lambda i: (i, 0, 0, 0))
    any_spec = pl.BlockSpec(memory_space=pl.ANY)
    return pl.pallas_call(
        body, name=name, grid=(nch,),
        in_specs=[row] * 7 + [big, pl.BlockSpec((2 * LANES, LANES), lambda i: (0, 0))] + [any_spec] * n_ride,
        out_specs=[big, pl.BlockSpec((N_PAIR, HEAD, LANES), lambda i: (0, 0, i))] + [any_spec] * n_ride,
        out_shape=[jax.ShapeDtypeStruct((t, N_PAIR, HEAD, LANES), F32), jax.ShapeDtypeStruct((N_PAIR, HEAD, nch * LANES), F32)]
        + _exchange_out_shape(ride_xs, ride_gather),
        scratch_shapes=[pltpu.VMEM((N_PAIR, HEAD, LANES), F32)] + [pltpu.VMEM((CHUNK, RW_W), F32)] * 6
        + (_exchange_sems(n_ride) if n_ride else []),
    )(r, r_prev, w, k, c, c_next, b, vb, bd2, *ride_xs)


def _rw_scan_bwd(r, w, k, c, b, vb, dyb, spre, bd, name, ride=None):
    t = r.shape[0]
    nch = t // CHUNK
    ride_xs, ride_gather = ride if ride is not None else ([], False)
    n_ride = len(ride_xs)
    n_pre = 8

    def body(*refs):
        r_ref, w_ref, k_ref, c_ref, b_ref, vb_ref, dyb_ref, spre_ref, bd_ref = refs[:9]
        x_refs = refs[9:9 + n_ride]
        dr_ref, dw_ref, dk_ref, dc_ref, db_ref, dvt_ref = refs[9 + n_ride:15 + n_ride]
        o_refs = refs[15 + n_ride:15 + 2 * n_ride]
        g_ref, snext_ref = refs[15 + 2 * n_ride:17 + 2 * n_ride]
        pre = refs[17 + 2 * n_ride:17 + 2 * n_ride + n_pre]
        sems = refs[17 + 2 * n_ride + n_pre:]
        wb_ref, wk_ref, rb_ref, rk_ref, rwb_ref, cb_ref, rwk_ref, ck_ref = pre[:8]

        @pl.when(pl.program_id(0) == 0)
        def _():
            g_ref[...] = jnp.zeros_like(g_ref)
            snext_ref[...] = jnp.zeros_like(snext_ref)
            if n_ride:
                _exchange_start(x_refs, o_refs, sems, ride_gather)

        for ref in (dr_ref, dw_ref, dk_ref, dc_ref, db_ref, dvt_ref):
            ref[...] = jnp.zeros_like(ref)
        bdv = bd_ref[...]
        bd1 = bdv[0:LANES]
        lane = lax.broadcasted_iota(jnp.int32, (HEAD, LANES), 1) % CHUNK

        rv, wv, kv, cv, bv = r_ref[...], w_ref[...], k_ref[...], c_ref[...], b_ref[...]
        b_b, k_b = pltpu.roll(bv, 1, 0), pltpu.roll(kv, 1, 0)
        wb_ref[...] = wv * b_b
        wk_ref[...] = wv * k_b
        rw = rv * wv
        for ref, x in ((rb_ref, rv * bv), (rk_ref, rv * kv), (rwb_ref, rw * b_b), (cb_ref, cv * b_b), (rwk_ref, rw * k_b),
                       (ck_ref, cv * k_b)):
            sums = _group_sums([x[:, q * LANES:(q + 1) * LANES] for q in range(N_PAIR)], bdv)
            for q in range(N_PAIR):
                ref[:, q * LANES:(q + 1) * LANES] = sums[q * CHUNK:(q + 1) * CHUNK]

        def sum0(x):
            return jnp.sum(x, axis=0, keepdims=True)

        def cut(ref, i):
            x = ref[pl.ds(i, 1), :]
            return [x[:, p * LANES:(p + 1) * LANES] for p in range(N_PAIR)]

        def sums_bf16(prods):
            return jnp.dot(jnp.concatenate(prods, axis=0).astype(BF16), bd1, preferred_element_type=F32)

        def two_steps(n, gs):
            ia = CHUNK - 1 - 2 * n
            ib = ia - 1
            r_a, w_a, k_a, c_a, b_a = [cut(ref, ia) for ref in (r_ref, w_ref, k_ref, c_ref, b_ref)]
            r_b, w_b, c_b = [cut(ref, ib) for ref in (r_ref, w_ref, c_ref)]
            wb, wk, rb_a, rk_a, rwb, cb, rwk, ck = [cut(ref, ia) for ref in pre[:8]]
            rb_b, rk_b = cut(rb_ref, ib), cut(rk_ref, ib)
            pairs = range(N_PAIR)
            sp_a = [spre_ref[ia, p] for p in pairs]
            sp_b = [spre_ref[ib, p] for p in pairs]
            dy_a = [dyb_ref[ia, p] for p in pairs]
            dy_b = [dyb_ref[ib, p] for p in pairs]
            chain = _group_sums([gs[p] * b_a[p] for p in pairs] + [gs[p] * wb[p] for p in pairs], bdv)
            off = sums_bf16([gs[p] * k_a[p] for p in pairs] + [gs[p] * wk[p] for p in pairs]
                            + [sp_a[p] * c_a[p] for p in pairs] + [sp_b[p] * c_b[p] for p in pairs])
            new = []
            for p in pairs:
                def part(x, q, p=p):
                    return x[(q * N_PAIR + p) * HEAD:(q * N_PAIR + p + 1) * HEAD]
                dsa_a = part(chain, 0) + dy_a[p] * rb_a[p]
                dv_a = part(off, 0) + dy_a[p] * rk_a[p]
                dsa_b = part(chain, 1) + dy_a[p] * rwb[p] + dsa_a * cb[p] + dy_b[p] * rb_b[p]
                dv_b = part(off, 1) + dy_a[p] * rwk[p] + dsa_a * ck[p] + dy_b[p] * rk_b[p]
                sa_a, sa_b = part(off, 2), part(off, 3)
                g_a = gs[p] + dy_a[p] * r_a[p]
                g_mid = g_a * w_a[p] + dsa_a * c_a[p]
                g_b = g_mid + dy_b[p] * r_b[p]
                new.append(g_b * w_b[p] + dsa_b * c_b[p])
                cols = pl.ds(p * LANES, LANES)
                for i, dy, s_post, s_pre, g, sa, dsa in ((ia, dy_a[p], snext_ref[p], sp_a[p], g_a, sa_a, dsa_a),
                                                         (ib, dy_b[p], sp_a[p], sp_b[p], g_b, sa_b, dsa_b)):
                    _store_row(dr_ref, i, cols, sum0(s_post * dy))
                    _store_row(dw_ref, i, cols, sum0(g * s_pre))
                    _store_row(db_ref, i, cols, sum0(g * sa))
                    _store_row(dk_ref, i, cols, sum0(g * vb_ref[i, p]))
                    _store_row(dc_ref, i, cols, sum0(s_pre * dsa))
                dvt_ref[p] = jnp.where(lane == ia, dv_a, jnp.where(lane == ib, dv_b, dvt_ref[p]))
                snext_ref[p] = sp_b[p]
            return tuple(new)

        gs = lax.fori_loop(0, CHUNK // 2, two_steps, tuple(g_ref[p] for p in range(N_PAIR)))
        for p in range(N_PAIR):
            g_ref[p] = gs[p]

        if n_ride:
            @pl.when(pl.program_id(0) == nch - 1)
            def _():
                _exchange_wait(x_refs, o_refs, sems, ride_gather)

    row = pl.BlockSpec((CHUNK, RW_W), lambda i: (nch - 1 - i, 0))
    big = pl.BlockSpec((CHUNK, N_PAIR, HEAD, LANES), lambda i: (nch - 1 - i, 0, 0, 0))
    any_spec = pl.BlockSpec(memory_space=pl.ANY)
    return pl.pallas_call(
        body, name=name, grid=(nch,),
        in_specs=[row] * 5 + [big, big, big, pl.BlockSpec((2 * LANES, LANES), lambda i: (0, 0))] + [any_spec] * n_ride,
        out_specs=[row] * 5 + [pl.BlockSpec((N_PAIR, HEAD, LANES), lambda i: (0, 0, nch - 1 - i))] + [any_spec] * n_ride,
        out_shape=[jax.ShapeDtypeStruct((t, RW_W), F32)] * 5 + [jax.ShapeDtypeStruct((N_PAIR, HEAD, nch * LANES), F32)]
        + _exchange_out_shape(ride_xs, ride_gather),
        scratch_shapes=[pltpu.VMEM((N_PAIR, HEAD, LANES), F32), pltpu.VMEM((N_PAIR, HEAD, LANES), F32)]
        + [pltpu.VMEM((CHUNK, RW_W), F32)] * n_pre + (_exchange_sems(n_ride) if n_ride else []),
    )(r, w, k, c, b, vb, dyb, spre, bd, *ride_xs)


def _exchange(xs, *, gather, name):
    n = len(xs)

    def body(*refs):
        _exchange_start(refs[:n], refs[n:2 * n], refs[2 * n:], gather)
        _exchange_wait(refs[:n], refs[n:2 * n], refs[2 * n:], gather)

    any_spec = pl.BlockSpec(memory_space=pl.ANY)
    return pl.pallas_call(
        body, name=name, in_specs=[any_spec] * n, out_specs=[any_spec] * n, out_shape=_exchange_out_shape(xs, gather),
        scratch_shapes=_exchange_sems(n),
    )(*xs)


def _adamw_rows(g, w, m, v):
    m = ADAM_B1 * m + (1.0 - ADAM_B1) * g
    v = ADAM_B2 * v + (1.0 - ADAM_B2) * (g * g)
    m_hat = m / (1.0 - ADAM_B1 ** ADAM_STEP)
    v_hat = v / (1.0 - ADAM_B2 ** ADAM_STEP)
    return -ADAM_LR * (m_hat / (jnp.sqrt(v_hat) + ADAM_EPS) + ADAM_WD * w), m, v


def _sum_slots(parts):
    g = parts[0].astype(F32)
    for q in range(1, N_DEV):
        g = g + parts[q].astype(F32)
    return g


def _reduce_adamw(parts, w, m, v, name):
    rows, cols = w.shape

    def fn(parts, w, m, v):
        g = _sum_slots(parts)
        return (g,) + _adamw_rows(g, w, m, v)

    return _rowwise(fn, [parts, w, m, v], [], [(cols, F32)] * 4, [], tm=_pick(rows, (256, 128, 64, 32, 16, 8)), name=name)


def _shift(x, n):
    return jnp.pad(x, ((n, 0), (0, 0)))[:-n]


def _unshift(x, n):
    return jnp.pad(x, ((0, n), (0, 0)))[n:]


def _add_n(xs, *, tm, name):
    def fn(*vals):
        s = vals[0]
        for x in vals[1:]:
            s = s + x
        return s
    return _rowwise(fn, xs, [], [(xs[0].shape[1], F32)], [], tm=tm, name=name)[0]


def _norm_fwd(h, g, *, tm, name):
    return _rowwise(lambda x, gg: _rms(x, gg), [h], [g], [(h.shape[1], BF16)], [], tm=tm, name=name)[0]


def _res_norm_fwd(h, f, g, scale, *, tm, name):
    return _rowwise(lambda hh, ff, gg: hh + scale * _rms(ff, gg), [h, f], [g], [(h.shape[1], F32)], [], tm=tm, name=name)[0]


def _norm_bwd(x, g, dy, scale, res, out_dtype, *, tm, name):
    if res is None:
        def fn(xx, dd, gg):
            dx, dg = _rms_bwd(xx, gg, dd * scale)
            return dx, dg
        rows = [x, dy]
    else:
        def fn(xx, dd, rr, gg):
            dx, dg = _rms_bwd(xx, gg, dd * scale)
            return dx + rr, dg
        rows = [x, dy, res]
    return _rowwise(fn, rows, [g], [(x.shape[1], out_dtype)], [(1, x.shape[1])], tm=tm, name=name)


def _ffn_fwd(h, g_pre, g_post, w24, wo4, tiles, tag):
    tb, ts = tiles
    a = _norm_fwd(h, g_pre, tm=ts, name=f"{tag}_norm")
    gu, s4 = _ffn_in(a, w24, tm=tb, name=f"{tag}_in")
    f = _mm(s4, wo4, trans_b=False, tm=tb, tn=D_MODEL, out_dtype=F32, name=f"{tag}_out")
    h_new = _res_norm_fwd(h, f, g_post, 0.5, tm=ts, name=f"{tag}_res")
    return h_new, (h, a, gu, s4, f)


def _ffn_bwd(dh_new, res, g_pre, g_post, w24, wo4, tiles, tag):
    tb, ts = tiles
    h, a, gu, s4, f = res
    t = h.shape[0]
    df, dg_post = _norm_bwd(f, g_post, dh_new, 0.5, None, BF16, tm=ts, name=f"{tag}_dres")
    dgu = _ffn_dswiglu(df, wo4, gu, tm=tb, name=f"{tag}_dswiglu")
    d_wo = _mm_tn(s4, df[None], tk=FFN_BLK, name=f"{tag}_dwout")
    dgu8 = dgu.reshape(2 * w24.shape[1], t, FFN_BLK)
    w8 = w24.reshape(2 * w24.shape[1], D_MODEL, FFN_BLK)
    da = _mm(dgu8, w8, trans_b=True, tm=tb, tn=D_MODEL, out_dtype=F32, name=f"{tag}_da")
    d_win = _mm_tn(a[None], dgu8, tk=D_MODEL, name=f"{tag}_dwin")
    dh, dg_pre = _norm_bwd(h, g_pre, da, 1.0, dh_new, F32, tm=ts, name=f"{tag}_dnorm")
    return dh, dg_pre, dg_post, d_win, d_wo.reshape(N_DEV, -1, D_MODEL)


def _blockdiag(w4):
    n, b, _ = w4.shape
    eye = jnp.eye(n, dtype=w4.dtype)
    return (eye[:, None, :, None] * w4[:, :, None, :]).reshape(n * b, n * b)


def _blockdiag_grad(d):
    n = d.shape[0] // HEAD
    x = d.reshape(n, HEAD, n, HEAD)
    return jnp.stack([x[i, :, i, :] for i in range(n)])


def _row(v):
    return v.reshape(1, -1)


def _mixer_fwd(h, g_pre, g_post, wi, wo, P, bd, tiles, tag, ride=None):
    tb, ts = tiles
    a = _norm_fwd(h, g_pre, tm=ts, name=f"{tag}_norm")
    p = _mm(a[None], wi[None], trans_b=False, tm=tb, tn=N_IN, out_dtype=F32, name=f"{tag}_in")
    lx, lg = p[:, 0:256], p[:, 256:512]
    sb, scc, sx = p[:, 512:768], p[:, 768:1024], p[:, 1024:1280]
    z = p[:, 1280:]
    lxs = [lx, _shift(lx, 1), _shift(lx, 2), _shift(lx, 3)]
    cw = [_row(P['lru_conv_w'][kk]) for kk in range(4)]
    lru_par = cw + [_row(P['lru_conv_b']), _blockdiag(P['lru_wa']), _row(P['lru_ba']), _blockdiag(P['lru_wx']),
                    _row(P['lru_bx']), _row(P['lru_lambda'])]
    la, lb = _rowwise(_lru_pre, lxs, lru_par, [(LRU_W, F32)] * 2, [], tm=ts, name=f"{tag}_lru_pre")
    hs = _lru_scan(la, lb, name=f"{tag}_lru_scan")
    y_lru = _rowwise(lambda gg, hh, ng, b_: _lru_post(b_, gg, hh, ng), [lg, hs], [_row(P['lru_norm_g']), bd],
                     [(LRU_W, F32)], [], tm=ts, name=f"{tag}_lru_post")[0]
    sc_rows = [sb, scc, sx, _shift(scc, 1), _shift(sx, 1), _shift(scc, 2), _shift(sx, 2)]
    sc_par = [_row(P['sc_conv_w'][kk]) for kk in range(3)] + [_row(P['sc_norm_g'])]
    y_sc = _rowwise(lambda *v: _sc_fwd(v[-1], *v[:-1]), sc_rows, sc_par + [bd], [(SC_W, F32)], [], tm=ts,
                    name=f"{tag}_sc")[0]
    cuts = (0, RW_W, 2 * RW_W, 3 * RW_W, RW_IN)
    zs = [z[:, cuts[q]:cuts[q + 1]] for q in range(4)]
    z_rows = zs + [_shift(q, 1) for q in zs]
    pad = lambda m, lo: jnp.pad(m, ((lo, LANES - lo - m.shape[0]), (0, 0)))
    rw_par = [_row(P['rwkv_mu'][cuts[q]:cuts[q + 1]]) for q in range(4)]
    rw_par += [_row(P['rwkv_w0']), pad(P['rwkv_w2'], 0), _row(P['rwkv_a0']), pad(P['rwkv_a2'], 32),
               pad(P['rwkv_g2'], 64), _row(P['rwkv_k_k']), _row(P['rwkv_k_a'])]
    r, w, k2, c, b, v, g = _rowwise(lambda *vv: _rw_pre(vv[-1], *vv[:-1]), z_rows, rw_par + [bd], [(RW_W, F32)] * 7, [],
                                    tm=ts, name=f"{tag}_rw_pre")
    vb = _bcast_cols(v)
    spre, yt, *rode = _rw_scan(r, _shift(r, 1), w, k2, c, _unshift(c, 1), b, vb, _stacked_bf16(bd), name=f"{tag}_rw_scan",
                               ride=ride)
    y = _unshift(_cols_to_rows(yt), 1)
    post_par =[_row(P['rwkv_lnx_w']), _row(P['rwkv_lnx_b']), _row(P['rwkv_r_k'])]
    y_rw = _rowwise(lambda *vv: _rw_post(vv[-1], *vv[:-1]), [y, r, k2, v, g], post_par + [bd], [(RW_W, F32)], [], tm=ts,
                    name=f"{tag}_rw_post")[0]
    ycat = jnp.concatenate([y_lru, y_sc, y_rw], axis=1).astype(BF16)
    m = _mm(ycat[None], wo[None], trans_b=False, tm=tb, tn=D_MODEL, out_dtype=F32, name=f"{tag}_out")
    h_new = _res_norm_fwd(h, m, g_post, 1.0, tm=ts, name=f"{tag}_res")
    res = dict(h=h, a=a, m=m, ycat=ycat, lxs=lxs, lru_par=lru_par, lg=lg, la=la, hs=hs, sc_rows=sc_rows, sc_par=sc_par,
               z_rows=z_rows, rw_par=rw_par, r=r, w=w, k2=k2, c=c, b=b, v=v, g=g, vb=vb, spre=spre, y=y, post_par=post_par)
    return h_new, res, rode


def _mixer_bwd(dh_new, R, g_pre, g_post, wi, wo, P, bd, tiles, tag, ride=None):
    tb, ts = tiles
    dm, dg_post = _norm_bwd(R['m'], g_post, dh_new, 1.0, None, BF16, tm=ts, name=f"{tag}_dres")
    dycat = _mm(dm[None], wo[None], trans_b=True, tm=tb, tn=D_MODEL, out_dtype=F32, name=f"{tag}_dycat")
    d_wo = _mm_tn(R['ycat'][None], dm[None], tk=D_MODEL // 2, name=f"{tag}_dwout")[0]
    dy_lru, dy_sc, dy_rw = dycat[:, 0:256], dycat[:, 256:512], dycat[:, 512:]
    G = {}
    d_lg, d_hs, G['lru_norm_g'] = _rowwise(
        lambda gg, hh, ct, ng, b_: _vjp_rows(_lru_post, 1, 3, 1)(b_, gg, hh, ng, ct),
        [R['lg'], R['hs'], dy_lru], [_row(P['lru_norm_g']), bd], [(LRU_W, F32)] * 2, [(1, LRU_W)], tm=ts,
        name=f"{tag}_lru_dpost")
    d_la, d_lb = _lru_scan_bwd(_unshift(R['la'], 1), _shift(R['hs'], 1), d_hs, name=f"{tag}_lru_dscan")

    def lru_pre_bwd(x0, x1, x2, x3, ca, cb_, *par):
        return _vjp_rows(_lru_pre, 0, 14, 2)(x0, x1, x2, x3, *par, ca, cb_)

    par_shapes = [tuple(q.shape) for q in R['lru_par']]
    outs = _rowwise(lru_pre_bwd, R['lxs'] + [d_la, d_lb], R['lru_par'], [(LRU_W, F32)] * 4, par_shapes, tm=ts,
                    name=f"{tag}_lru_dpre")
    dxs, dpar = outs[:4], outs[4:]
    d_lx = _add_n([dxs[0], _unshift(dxs[1], 1), _unshift(dxs[2], 2), _unshift(dxs[3], 3)], tm=ts, name=f"{tag}_lru_dx")
    G['lru_conv_w'] = jnp.concatenate(dpar[0:4], axis=0)
    G['lru_conv_b'] = dpar[4][0]
    G['lru_wa'] = _blockdiag_grad(dpar[5])
    G['lru_ba'] = dpar[6][0]
    G['lru_wx'] = _blockdiag_grad(dpar[7])
    G['lru_bx'] = dpar[8][0]
    G['lru_lambda'] = dpar[9][0]
    G['lru_norm_g'] = G['lru_norm_g'][0]

    def sc_bwd(*vv):
        rows7, ct, par4, b_ = vv[:7], vv[7], vv[8:12], vv[12]
        return _vjp_rows(_sc_fwd, 1, 11, 1)(b_, *rows7, *par4, ct)

    outs = _rowwise(sc_bwd, R['sc_rows'] + [dy_sc], R['sc_par'] + [bd], [(SC_W, F32)] * 7, [(1, SC_W)] * 4, tm=ts,
                    name=f"{tag}_sc_bwd")
    d_sb = outs[0]
    d_sc = _add_n([outs[1], _unshift(outs[3], 1), _unshift(outs[5], 2)], tm=ts, name=f"{tag}_sc_dc")
    d_sx = _add_n([outs[2], _unshift(outs[4], 1), _unshift(outs[6], 2)], tm=ts, name=f"{tag}_sc_dx")
    G['sc_conv_w'] = jnp.concatenate(outs[7:10], axis=0)
    G['sc_norm_g'] = outs[10][0]

    def rw_post_bwd(*vv):
        rows5, ct, par3, b_ = vv[:5], vv[5], vv[6:9], vv[9]
        return _vjp_rows(_rw_post, 1, 8, 1)(b_, *rows5, *par3, ct)

    outs = _rowwise(rw_post_bwd, [R['y'], R['r'], R['k2'], R['v'], R['g'], dy_rw], R['post_par'] + [bd],
                    [(RW_W, F32)] * 5, [(1, RW_W)] * 3, tm=ts, name=f"{tag}_rw_dpost")
    d_y, dr_p, dk_p, dv_p, d_g = outs[:5]
    G['rwkv_lnx_w'], G['rwkv_lnx_b'], G['rwkv_r_k'] = outs[5][0], outs[6][0], outs[7][0]
    dyb = _bcast_cols(d_y)
    dr_s, d_w, dk_s, d_c, d_b, dvt, *rode = _rw_scan_bwd(R['r'], R['w'], R['k2'], R['c'], R['b'], R['vb'], dyb, R['spre'],
                                                        _stacked_bf16(bd), name=f"{tag}_rw_dscan", ride=ride)
    dv_s = _cols_to_rows(dvt)

    def rw_pre_bwd(*vv):
        zrows = vv[0:8]
        dr1, dr2, dw_, dk1, dk2_, dc_, db_, dv1, dv2, dg_ = vv[8:18]
        par, b_ = vv[18:29], vv[29]
        return _vjp_rows(_rw_pre, 1, 19, 7)(b_, *zrows, *par, dr1 + dr2, dw_, dk1 + dk2_, dc_, db_, dv1 + dv2, dg_)

    par_shapes = [tuple(q.shape) for q in R['rw_par']]
    widths = [(q.shape[1], F32) for q in R['z_rows']]
    outs = _rowwise(rw_pre_bwd, R['z_rows'] + [dr_p, dr_s, d_w, dk_p, dk_s, d_c, d_b, dv_p, dv_s, d_g],
                    R['rw_par'] + [bd], widths, par_shapes, tm=ts, name=f"{tag}_rw_dpre")
    d_z = _add_n([jnp.concatenate(outs[0:4], axis=1), _unshift(jnp.concatenate(outs[4:8], axis=1), 1)], tm=ts,
                 name=f"{tag}_rw_dz")
    dpar = outs[8:]
    G['rwkv_mu'] = jnp.concatenate([q[0] for q in dpar[0:4]])
    G['rwkv_w0'], G['rwkv_a0'] = dpar[4][0], dpar[6][0]
    G['rwkv_w2'], G['rwkv_a2'], G['rwkv_g2'] = dpar[5][0:32], dpar[7][32:64], dpar[8][64:128]
    G['rwkv_k_k'], G['rwkv_k_a'] = dpar[9][0], dpar[10][0]

    dp = jnp.concatenate([d_lx, d_lg, d_sb, d_sc, d_sx, d_z], axis=1).astype(BF16)
    da = _mm(dp[None], wi[None], trans_b=True, tm=tb, tn=D_MODEL, out_dtype=F32, name=f"{tag}_da")
    d_wi = _mm_tn(R['a'][None], dp[None], tk=D_MODEL // 2, name=f"{tag}_dwin")[0]
    dh, dg_pre = _norm_bwd(R['h'], g_pre, da, 1.0, dh_new, F32, tm=ts, name=f"{tag}_dnorm")
    return dh, dg_pre, dg_post, d_wi, d_wo, G, rode


def _loss_rows(h, tgt, n_seq, *, tm, name):
    d = h.shape[1]

    def body(h_ref, t_ref, dh_ref, l_ref):
        i = pl.program_id(0)
        row = lax.broadcasted_iota(jnp.int32, (tm, 1), 0) + i * tm
        live = (row >= N_META) & (row < N_META + n_seq)
        e = jnp.where(live, h_ref[...] - t_ref[...], 0.0)
        dh_ref[...] = e * (1.0 / d)
        part = 0.5 * jnp.sum(jnp.sum(e * e, axis=1, keepdims=True) * (1.0 / d), axis=0, keepdims=True)

        @pl.when(i == 0)
        def _():
            l_ref[...] = part

        @pl.when(i > 0)
        def _():
            l_ref[...] += part

    blk = pl.BlockSpec((tm, d), lambda i: (i, 0))
    return pl.pallas_call(body, name=name, grid=(h.shape[0] // tm,), in_specs=[blk, blk],
                          out_specs=[blk, pl.BlockSpec((1, 1), lambda i: (0, 0))],
                          out_shape=[jax.ShapeDtypeStruct(h.shape, F32), jax.ShapeDtypeStruct((1, 1), F32)])(h, tgt)


def _pack(arrs, mult):
    flat = jnp.concatenate([a.reshape(-1).astype(F32) for a in arrs])
    n = flat.shape[0]
    tot = -(-n // mult) * mult
    return jnp.pad(flat, (0, tot - n)).reshape(-1, LANES)


def _unpack(buf, shapes):
    flat = buf.reshape(-1)
    out, off = [], 0
    for s in shapes:
        n = 1
        for q in s:
            n *= q
        out.append(flat[off:off + n].reshape(s))
        off += n
    return out


def _step(W, M, V, x, loss_target):
    n_seq = x.shape[1]
    t_real = N_META + n_seq
    t = (t_real // CHUNK + 1) * CHUNK
    tiles = (_pick(t, (704, 512, 256, 128, 64)), _pick(t, (192, 128, 64)))
    me = 4 * lax.axis_index("x") + 2 * lax.axis_index("y") + lax.axis_index("c")
    n_layer = W['norm_g'].shape[0]

    small_sh = list(SMALL_SHARDED)
    packed = _pack([W[n] for n in small_sh], 8 * LANES)
    gathered = _exchange([W[n][0].astype(BF16) for n in BIG] + [packed], gather=True, name="gather_weights")
    big8 = [dict(zip(BIG, gathered[:-1]))] + [None] * (n_layer - 1)
    pieces = [_unpack(gathered[-1][q], [W[n].shape for n in small_sh]) for q in range(N_DEV)]
    full = {n: W[n] for n in SMALL if n not in SMALL_SHARDED}
    for idx, n in enumerate(small_sh):
        full[n] = jnp.concatenate([pieces[q][idx] for q in range(N_DEV)], axis=SMALL_SHARDED[n])

    def layer_weights(l):
        w24_1 = big8[l]['ffn1_w_in'].reshape(2, N_DEV // 2, D_MODEL, FFN_BLK)
        wo4_1 = big8[l]['ffn1_w_out'].reshape(N_DEV // 2, FFN_BLK, D_MODEL)
        w24_2 = big8[l]['ffn2_w_in'].reshape(2, N_DEV // 2, D_MODEL, FFN_BLK)
        wo4_2 = big8[l]['ffn2_w_out'].reshape(N_DEV // 2, FFN_BLK, D_MODEL)
        wi = big8[l]['mix_w_in'].transpose(1, 0, 2).reshape(D_MODEL, N_IN)
        wo = big8[l]['mix_w_out'].reshape(D_MODEL, D_MODEL)
        return w24_1, wo4_1, w24_2, wo4_2, wi, wo

    bd = jnp.kron(jnp.eye(LANES // HEAD, dtype=F32), jnp.ones((HEAD, HEAD), F32))
    small_layer = [n for n in SMALL if n not in ('meta_tokens', 'norm_g')]

    h = jnp.concatenate([full['meta_tokens'], x[0], jnp.zeros((t - t_real, D_MODEL), F32)], axis=0)
    saved = []
    for l in range(n_layer):
        lw = layer_weights(l)
        ng = [_row(full['norm_g'][l, q]) for q in range(6)]
        P = {n: full[n][l] for n in small_layer}
        h, r1 = _ffn_fwd(h, ng[0], ng[1], lw[0], lw[1], tiles, f"l{l}_ffn1")
        ride = ([W[n][l + 1].astype(BF16) for n in BIG], True) if l + 1 < n_layer else None
        h, r2, rode = _mixer_fwd(h, ng[2], ng[3], lw[4], lw[5], P, bd, tiles, f"l{l}_mix", ride=ride)
        if ride is not None:
            big8[l + 1] = dict(zip(BIG, rode))
        h, r3 = _ffn_fwd(h, ng[4], ng[5], lw[2], lw[3], tiles, f"l{l}_ffn2")
        saved.append((lw, ng, P, r1, r2, r3))

    tgt = jnp.pad(loss_target[0], ((N_META, t - t_real), (0, 0)))
    dh, loss_part = _loss_rows(h, tgt, n_seq, tm=tiles[1], name="loss")
    loss = lax.psum(loss_part[0, 0], MESH_AXES)

    small_grads = [None] * n_layer
    norm_grads = [None] * n_layer
    recv = [None] * n_layer
    outgoing = None
    for l in reversed(range(n_layer)):
        lw, ng, P, r1, r2, r3 = saved[l]
        dh, g4, g5, d_win2, d_wo2 = _ffn_bwd(dh, r3, ng[4], ng[5], lw[2], lw[3], tiles, f"l{l}_ffn2")
        ride = (outgoing, False) if outgoing is not None else None
        dh, g2, g3, d_wi, d_wo, G, rode = _mixer_bwd(dh, r2, ng[2], ng[3], lw[4], lw[5], P, bd, tiles, f"l{l}_mix", ride=ride)
        if ride is not None:
            recv[l + 1] = rode
        dh, g0, g1, d_win1, d_wo1 = _ffn_bwd(dh, r1, ng[0], ng[1], lw[0], lw[1], tiles, f"l{l}_ffn1")
        small_grads[l] = G
        norm_grads[l] = jnp.concatenate([g0, g1, g2, g3, g4, g5], axis=0)
        d_wi8 = d_wi.reshape(D_MODEL, N_DEV, N_IN // N_DEV).transpose(1, 0, 2)
        d_wo8 = d_wo.reshape(N_DEV, D_MODEL // N_DEV, D_MODEL)
        outgoing = [d_win1, d_wo1, d_win2, d_wo2, d_wi8, d_wo8]
    recv[0] = _exchange(outgoing, gather=False, name="l0_grad_exchange")

    gs = {n: jnp.stack([small_grads[l][n] for l in range(n_layer)]) for n in small_layer}
    gs['norm_g'] = jnp.stack(norm_grads)
    gs['meta_tokens'] = dh[:N_META]
    gpack = _pack([gs[n] for n in SMALL], 8 * LANES)
    gall = _exchange([gpack], gather=True, name="gather_small_grads")[0]
    gsum = _rowwise(lambda parts: _sum_slots(parts), [gall], [], [(LANES, F32)], [], tm=gall.shape[1],
                    name="sum_small_grads")[0]
    gfull = dict(zip(SMALL, _unpack(gsum, [gs[n].shape for n in SMALL])))

    def my_shard(n, a):
        if n not in SMALL_SHARDED:
            return a
        ax = SMALL_SHARDED[n]
        size = a.shape[ax] // N_DEV
        return lax.dynamic_slice_in_dim(a, me * size, size, axis=ax)

    g_loc = [my_shard(n, gfull[n]) for n in SMALL]
    shapes = [W[n].shape for n in SMALL]
    bufs = [_pack(g_loc, 8 * LANES)] + [_pack([D[n] for n in SMALL], 8 * LANES) for D in (W, M, V)]
    d_s, m_s, v_s = _rowwise(_adamw_rows, bufs, [], [(LANES, F32)] * 3, [], tm=bufs[0].shape[0], name="adamw_small")
    out = {'grad': dict(zip(SMALL, g_loc)), 'delta': dict(zip(SMALL, _unpack(d_s, shapes))),
           'm': dict(zip(SMALL, _unpack(m_s, shapes))), 'v': dict(zip(SMALL, _unpack(v_s, shapes)))}

    order = ['ffn1_w_in', 'ffn1_w_out', 'ffn2_w_in', 'ffn2_w_out', 'mix_w_in', 'mix_w_out']
    for idx, n in enumerate(order):
        per_layer = []
        for l in range(n_layer):
            parts = recv[l][idx]
            rows, cols = W[n].shape[1], W[n].shape[2]
            per_layer.append(_reduce_adamw(parts.reshape(N_DEV, rows, cols), W[n][l], M[n][l], V[n][l],
                                           name=f"l{l}_adamw_{n}"))
        for q, key in enumerate(('grad', 'delta', 'm', 'v')):
            out[key][n] = jnp.stack([per_layer[l][q] for l in range(n_layer)])

    return (loss, dh[N_META:t_real][None],
            *[out['grad'][n] for n in WEIGHTS], *[out['delta'][n] for n in WEIGHTS],
            *[out['m'][n] for n in WEIGHTS], *[out['v'][n] for n in WEIGHTS])


def kernel(x, meta_tokens, norm_g, ffn1_w_in, ffn1_w_out, ffn2_w_in, ffn2_w_out, mix_w_in, mix_w_out, lru_conv_w, lru_conv_b, lru_wa, lru_ba, lru_wx, lru_bx, lru_lambda, lru_norm_g, sc_conv_w, sc_norm_g, rwkv_mu, rwkv_w0, rwkv_w2, rwkv_a0, rwkv_a2, rwkv_g2, rwkv_k_k, rwkv_k_a, rwkv_r_k, rwkv_lnx_w, rwkv_lnx_b, loss_target, m_meta_tokens, m_norm_g, m_ffn1_w_in, m_ffn1_w_out, m_ffn2_w_in, m_ffn2_w_out, m_mix_w_in, m_mix_w_out, m_lru_conv_w, m_lru_conv_b, m_lru_wa, m_lru_ba, m_lru_wx, m_lru_bx, m_lru_lambda, m_lru_norm_g, m_sc_conv_w, m_sc_norm_g, m_rwkv_mu, m_rwkv_w0, m_rwkv_w2, m_rwkv_a0, m_rwkv_a2, m_rwkv_g2, m_rwkv_k_k, m_rwkv_k_a, m_rwkv_r_k, m_rwkv_lnx_w, m_rwkv_lnx_b, v_meta_tokens, v_norm_g, v_ffn1_w_in, v_ffn1_w_out, v_ffn2_w_in, v_ffn2_w_out, v_mix_w_in, v_mix_w_out, v_lru_conv_w, v_lru_conv_b, v_lru_wa, v_lru_ba, v_lru_wx, v_lru_bx, v_lru_lambda, v_lru_norm_g, v_sc_conv_w, v_sc_norm_g, v_rwkv_mu, v_rwkv_w0, v_rwkv_w2, v_rwkv_a0, v_rwkv_a2, v_rwkv_g2, v_rwkv_k_k, v_rwkv_k_a, v_rwkv_r_k, v_rwkv_lnx_w, v_rwkv_lnx_b):
    w_vals = (meta_tokens, norm_g, ffn1_w_in, ffn1_w_out, ffn2_w_in, ffn2_w_out, mix_w_in, mix_w_out, lru_conv_w, lru_conv_b, lru_wa, lru_ba, lru_wx, lru_bx, lru_lambda, lru_norm_g, sc_conv_w, sc_norm_g, rwkv_mu, rwkv_w0, rwkv_w2, rwkv_a0, rwkv_a2, rwkv_g2, rwkv_k_k, rwkv_k_a, rwkv_r_k, rwkv_lnx_w, rwkv_lnx_b)
    m_vals = (m_meta_tokens, m_norm_g, m_ffn1_w_in, m_ffn1_w_out, m_ffn2_w_in, m_ffn2_w_out, m_mix_w_in, m_mix_w_out, m_lru_conv_w, m_lru_conv_b, m_lru_wa, m_lru_ba, m_lru_wx, m_lru_bx, m_lru_lambda, m_lru_norm_g, m_sc_conv_w, m_sc_norm_g, m_rwkv_mu, m_rwkv_w0, m_rwkv_w2, m_rwkv_a0, m_rwkv_a2, m_rwkv_g2, m_rwkv_k_k, m_rwkv_k_a, m_rwkv_r_k, m_rwkv_lnx_w, m_rwkv_lnx_b)
    v_vals = (v_meta_tokens, v_norm_g, v_ffn1_w_in, v_ffn1_w_out, v_ffn2_w_in, v_ffn2_w_out, v_mix_w_in, v_mix_w_out, v_lru_conv_w, v_lru_conv_b, v_lru_wa, v_lru_ba, v_lru_wx, v_lru_bx, v_lru_lambda, v_lru_norm_g, v_sc_conv_w, v_sc_norm_g, v_rwkv_mu, v_rwkv_w0, v_rwkv_w2, v_rwkv_a0, v_rwkv_a2, v_rwkv_g2, v_rwkv_k_k, v_rwkv_k_a, v_rwkv_r_k, v_rwkv_lnx_w, v_rwkv_lnx_b)
    return _step(dict(zip(WEIGHTS, w_vals)), dict(zip(WEIGHTS, m_vals)), dict(zip(WEIGHTS, v_vals)), x, loss_target)
```

```python
import functools

import jax
import jax.numpy as jnp
from jax import lax
from jax.experimental import pallas as pl
from jax.experimental.pallas import tpu as pltpu

F32 = jnp.float32
BF16 = jnp.bfloat16
HIGHEST = lax.Precision.HIGHEST

N_DEV = 8
MESH_AXES = ("x", "y", "c")
N_META = 16
D_MODEL = 1024
LRU_W = 256
SC_W = 256
RW_W = 512
HEAD = 64
LANES = 128
CHUNK = 64
RW_IN = 1664
N_IN = 2944
FFN_BLK = 704
RMS_EPS = 1e-6
LNX_EPS = 64e-5
LRU_C = 8.0
ADAM_LR, ADAM_B1, ADAM_B2, ADAM_EPS, ADAM_WD, ADAM_STEP = 0.001, 0.9, 0.999, 1e-08, 0.01, 10

WEIGHTS = ['meta_tokens', 'norm_g', 'ffn1_w_in', 'ffn1_w_out', 'ffn2_w_in', 'ffn2_w_out', 'mix_w_in', 'mix_w_out',
           'lru_conv_w', 'lru_conv_b', 'lru_wa', 'lru_ba', 'lru_wx', 'lru_bx', 'lru_lambda', 'lru_norm_g',
           'sc_conv_w', 'sc_norm_g', 'rwkv_mu', 'rwkv_w0', 'rwkv_w2', 'rwkv_a0', 'rwkv_a2', 'rwkv_g2', 'rwkv_k_k',
           'rwkv_k_a', 'rwkv_r_k', 'rwkv_lnx_w', 'rwkv_lnx_b']
BIG = ['ffn1_w_in', 'ffn1_w_out', 'ffn2_w_in', 'ffn2_w_out', 'mix_w_in', 'mix_w_out']
SMALL_SHARDED = {'meta_tokens': 1, 'norm_g': 2, 'lru_conv_w': 2, 'sc_conv_w': 2, 'rwkv_w2': 2, 'rwkv_a2': 2, 'rwkv_g2': 2}
SMALL = [n for n in WEIGHTS if n not in BIG]


def _pick(n, cands):
    for c in cands:
        if n % c == 0:
            return c
    raise ValueError(f"no tile for {n}")


def _rowwise(fn, rows, params, row_outs, acc_outs, *, tm, name):
    nr, npar, nro, nao = len(rows), len(params), len(row_outs), len(acc_outs)
    n_rows = rows[0].shape[-2]
    assert n_rows % tm == 0, (name, n_rows, tm)

    def body(*refs):
        vals = [r[...] for r in refs[:nr + npar]]
        outs = fn(*vals)
        if not isinstance(outs, (tuple, list)):
            outs = (outs,)
        assert len(outs) == nro + nao, (name, len(outs))
        for o_ref, o in zip(refs[nr + npar:nr + npar + nro], outs[:nro]):
            o_ref[...] = o.astype(o_ref.dtype)
        step = pl.program_id(0)
        for a_ref, a in zip(refs[nr + npar + nro:], outs[nro:]):
            @pl.when(step == 0)
            def _(a_ref=a_ref, a=a):
                a_ref[...] = a.astype(F32)

            @pl.when(step > 0)
            def _(a_ref=a_ref, a=a):
                a_ref[...] += a.astype(F32)

    def row_spec(shape):
        if len(shape) == 2:
            return pl.BlockSpec((tm, shape[1]), lambda i: (i, 0))
        return pl.BlockSpec((shape[0], tm, shape[2]), lambda i: (0, i, 0))

    def full_spec(shape):
        nd = len(shape)
        return pl.BlockSpec(tuple(shape), lambda i, nd=nd: (0,) * nd)

    in_specs = [row_spec(r.shape) for r in rows] + [full_spec(p.shape) for p in params]
    out_shape = [jax.ShapeDtypeStruct((n_rows, w), dt) for (w, dt) in row_outs]
    out_shape += [jax.ShapeDtypeStruct(tuple(s), F32) for s in acc_outs]
    out_specs = [row_spec((n_rows, w)) for (w, _) in row_outs] + [full_spec(s) for s in acc_outs]
    res = pl.pallas_call(body, name=name, grid=(n_rows // tm,), in_specs=in_specs, out_specs=out_specs,
                         out_shape=out_shape)(*rows, *params)
    return tuple(res)


def _mm(a3, b3, *, trans_b, tm, tn, out_dtype, name):
    nj, m, kb = a3.shape
    n = b3.shape[1] if trans_b else b3.shape[2]
    dims = (((1,), (1,)), ((), ())) if trans_b else (((1,), (0,)), ((), ()))

    def body(a_ref, b_ref, o_ref, acc_ref):
        j = pl.program_id(2)

        @pl.when(j == 0)
        def _():
            acc_ref[...] = jnp.zeros_like(acc_ref)

        acc_ref[...] += lax.dot_general(a_ref[0], b_ref[0], dims, preferred_element_type=F32)

        @pl.when(j == nj - 1)
        def _():
            o_ref[...] = acc_ref[...].astype(o_ref.dtype)

    if trans_b:
        b_spec = pl.BlockSpec((1, tn, kb), lambda i, c, j: (j, c, 0))
    else:
        b_spec = pl.BlockSpec((1, kb, tn), lambda i, c, j: (j, 0, c))
    return pl.pallas_call(
        body, name=name, grid=(m // tm, n // tn, nj),
        in_specs=[pl.BlockSpec((1, tm, kb), lambda i, c, j: (j, i, 0)), b_spec],
        out_specs=pl.BlockSpec((tm, tn), lambda i, c, j: (i, c)),
        out_shape=jax.ShapeDtypeStruct((m, n), out_dtype),
        scratch_shapes=[pltpu.VMEM((tm, tn), F32)],
    )(a3, b3)


def _mm_tn(a3, b3, *, tk, name):
    ja, t, ka = a3.shape
    jb, _, n = b3.shape
    nj = max(ja, jb)

    def body(a_ref, b_ref, o_ref):
        o_ref[0] = lax.dot_general(a_ref[0], b_ref[0], (((0,), (0,)), ((), ())),
                                   preferred_element_type=F32).astype(o_ref.dtype)

    return pl.pallas_call(
        body, name=name, grid=(nj, ka // tk),
        in_specs=[pl.BlockSpec((1, t, tk), (lambda j, c: (j, 0, c)) if ja > 1 else (lambda j, c: (0, 0, c))),
                  pl.BlockSpec((1, t, n), (lambda j, c: (j, 0, 0)) if jb > 1 else (lambda j, c: (0, 0, 0)))],
        out_specs=pl.BlockSpec((1, tk, n), lambda j, c: (j, c, 0)),
        out_shape=jax.ShapeDtypeStruct((nj, ka, n), BF16),
    )(a3, b3)


def _ffn_in(a, w24, *, tm, name):
    t, d = a.shape
    nb, fb = w24.shape[1], w24.shape[3]

    def body(a_ref, w_ref, gu_ref, s_ref):
        x = a_ref[...]
        g = jnp.dot(x, w_ref[0, 0], preferred_element_type=F32)
        u = jnp.dot(x, w_ref[1, 0], preferred_element_type=F32)
        gu_ref[0, 0] = g.astype(BF16)
        gu_ref[1, 0] = u.astype(BF16)
        s_ref[0] = (g * jax.nn.sigmoid(g) * u).astype(BF16)

    return pl.pallas_call(
        body, name=name, grid=(nb, t // tm),
        in_specs=[pl.BlockSpec((tm, d), lambda j, i: (i, 0)), pl.BlockSpec((2, 1, d, fb), lambda j, i: (0, j, 0, 0))],
        out_specs=[pl.BlockSpec((2, 1, tm, fb), lambda j, i: (0, j, i, 0)), pl.BlockSpec((1, tm, fb), lambda j, i: (j, i, 0))],
        out_shape=[jax.ShapeDtypeStruct((2, nb, t, fb), BF16), jax.ShapeDtypeStruct((nb, t, fb), BF16)],
    )(a, w24)


def _ffn_dswiglu(df, wo4, gu, *, tm, name):
    t, d = df.shape
    nb, fb = wo4.shape[0], wo4.shape[1]

    def body(df_ref, wo_ref, gu_ref, dg_ref):
        ds = lax.dot_general(df_ref[...], wo_ref[0], (((1,), (1,)), ((), ())), preferred_element_type=F32)
        g = gu_ref[0, 0].astype(F32)
        u = gu_ref[1, 0].astype(F32)
        sig = jax.nn.sigmoid(g)
        dg_ref[0, 0] = (ds * u * sig * (1.0 + g * (1.0 - sig))).astype(BF16)
        dg_ref[1, 0] = (ds * g * sig).astype(BF16)

    return pl.pallas_call(
        body, name=name, grid=(nb, t // tm),
        in_specs=[pl.BlockSpec((tm, d), lambda j, i: (i, 0)), pl.BlockSpec((1, fb, d), lambda j, i: (j, 0, 0)),
                  pl.BlockSpec((2, 1, tm, fb), lambda j, i: (0, j, i, 0))],
        out_specs=pl.BlockSpec((2, 1, tm, fb), lambda j, i: (0, j, i, 0)),
        out_shape=jax.ShapeDtypeStruct((2, nb, t, fb), BF16),
    )(df, wo4, gu)


def _rms(x, g):
    return x * lax.rsqrt(jnp.mean(x * x, axis=-1, keepdims=True) + RMS_EPS) * g


def _rms_bwd(x, g, dy):
    rstd = lax.rsqrt(jnp.mean(x * x, axis=-1, keepdims=True) + RMS_EPS)
    xh = x * rstd
    dxh = dy * g
    dx = rstd * (dxh - xh * jnp.mean(dxh * xh, axis=-1, keepdims=True))
    return dx, jnp.sum(dy * xh, axis=0, keepdims=True)


def _seg_sum_impl(x, bd):
    parts = [jnp.dot(x[:, q * LANES:(q + 1) * LANES], bd, preferred_element_type=F32, precision=HIGHEST)
             for q in range(x.shape[1] // LANES)]
    return parts[0] if len(parts) == 1 else jnp.concatenate(parts, axis=1)


@jax.custom_vjp
def _seg_sum(x, bd):
    return _seg_sum_impl(x, bd)


def _seg_sum_fwd(x, bd):
    return _seg_sum_impl(x, bd), bd


def _seg_sum_bwd(bd, ct):
    return _seg_sum_impl(ct, bd), jnp.zeros_like(bd)


_seg_sum.defvjp(_seg_sum_fwd, _seg_sum_bwd)


def _group_rms(y, g, bd):
    return y * lax.rsqrt(_seg_sum(y * y, bd) * (1.0 / HEAD) + RMS_EPS) * g


def _expm1(x):
    return jnp.where(jnp.abs(x) < 1e-2, x * (1.0 + x * (0.5 + x * (1.0 / 6.0))), jnp.exp(x) - 1.0)


def _lru_pre(x0, x1, x2, x3, cw0, cw1, cw2, cw3, cb, wa, ba, wx, bx, lam):
    u = x3 * cw0 + x2 * cw1 + x1 * cw2 + x0 * cw3 + cb
    r = jax.nn.sigmoid(jnp.dot(u, wa, preferred_element_type=F32, precision=HIGHEST) + ba)
    i = jax.nn.sigmoid(jnp.dot(u, wx, preferred_element_type=F32, precision=HIGHEST) + bx)
    log_a = -LRU_C * r * jax.nn.softplus(-lam)
    return jnp.exp(log_a), jnp.sqrt(-_expm1(2.0 * log_a)) * (i * u)


def _lru_post(bd, gate, hs, ng):
    return _group_rms(jax.nn.gelu(gate) * hs, ng, bd)


def _sc_fwd(bd, b, c0, x0, c1, x1, c2, x2, w0, w1, w2, ng):
    return _group_rms(b * (w0 * (c2 * x2) + w1 * (c1 * x1) + w2 * (c0 * x0)), ng, bd)


def _rw_pre(bd, zr, zk, zv, zt, sr, sk, sv, st, mur, muk, muv, mut, w0, w2p, a0, a2p, g2p, k_k, k_a):
    r, k, v, tail = zr + (sr - zr) * mur, zk + (sk - zk) * muk, zv + (sv - zv) * muv, zt + (st - zt) * mut
    lane = lax.broadcasted_iota(jnp.int32, tail.shape, 1)
    act = jnp.where(lane < 32, jnp.tanh(tail), jnp.where(lane < 64, tail, jax.nn.sigmoid(tail)))
    dot = functools.partial(jnp.dot, preferred_element_type=F32, precision=HIGHEST)
    w_log = -jax.nn.softplus(-(w0 + dot(act, w2p))) - 0.5
    w = jnp.exp(-jnp.exp(w_log))
    a = jax.nn.sigmoid(a0 + dot(act, a2p))
    g = dot(act, g2p)
    kk = k * k_k
    k2 = k * (1.0 + (a - 1.0) * k_a)
    kkn = kk * lax.rsqrt(jnp.maximum(_seg_sum(kk * kk, bd), 1e-24))
    return r, w, k2, -kkn, kkn * a, v, g


def _rw_post(bd, y, r, k2, v, g, lnw, lnb, r_k):
    mean = _seg_sum(y, bd) * (1.0 / HEAD)
    yc = y - mean
    var = _seg_sum(yc * yc, bd) * (1.0 / HEAD)
    yn = yc * lax.rsqrt(var + LNX_EPS) * lnw + lnb
    return (yn + _seg_sum(r * k2 * r_k, bd) * v) * g


def _vjp_rows(fwd, n_static, n_in, n_ct):
    def fn(*args):
        static, prim, cts = args[:n_static], args[n_static:n_static + n_in], args[n_static + n_in:]
        assert len(cts) == n_ct
        _, vjp = jax.vjp(functools.partial(fwd, *static), *prim)
        return vjp(cts[0] if n_ct == 1 else tuple(cts))
    return fn


def _exchange_copies(x_refs, o_refs, sems, gather):
    send_sems, recv_sems, local_sems = sems
    mx, my, mc = lax.axis_index("x"), lax.axis_index("y"), lax.axis_index("c")
    me = 4 * mx + 2 * my + mc
    local, sends, recvs = [], [], []
    for k in range(len(x_refs)):
        local.append(pltpu.make_async_copy(x_refs[k] if gather else x_refs[k].at[me], o_refs[k].at[me], local_sems.at[k]))
    for d in range(1, N_DEV):
        px, py, pc = mx ^ ((d >> 2) & 1), my ^ ((d >> 1) & 1), mc ^ (d & 1)
        peer = 4 * px + 2 * py + pc
        for k in range(len(x_refs)):
            src = x_refs[k] if gather else x_refs[k].at[peer]
            common = dict(src_ref=src, send_sem=send_sems.at[k, d - 1], recv_sem=recv_sems.at[k, d - 1],
                          device_id=(px, py, pc), device_id_type=pl.DeviceIdType.MESH)
            sends.append(pltpu.make_async_remote_copy(dst_ref=o_refs[k].at[me], **common))
            recvs.append(pltpu.make_async_remote_copy(dst_ref=o_refs[k].at[peer], **common))
    return local, sends, recvs


def _gather2_copies(x_refs, o_refs, sems):
    send_sems, recv_sems, local_sems = sems
    mx, my, mc = lax.axis_index("x"), lax.axis_index("y"), lax.axis_index("c")
    sibling = (mx, my, 1 - mc)
    chips = [(1 - mx, my), (mx, 1 - my), (1 - mx, 1 - my)]

    def slot(px, py, pc):
        return 4 * px + 2 * py + pc

    out = dict(local=[], first=[], first_recv=[], ici_recv=[], passed=[], passed_recv=[])
    for k in range(len(x_refs)):
        def copy(sem, src, dst_slot, to, k=k):
            return pltpu.make_async_remote_copy(src_ref=src, dst_ref=o_refs[k].at[dst_slot], send_sem=send_sems.at[k, sem],
                                                recv_sem=recv_sems.at[k, sem], device_id=to, device_id_type=pl.DeviceIdType.MESH)
        me = slot(mx, my, mc)
        out['local'].append(pltpu.make_async_copy(x_refs[k], o_refs[k].at[me], local_sems.at[k]))
        out['first'].append(copy(0, x_refs[k], me, sibling))
        out['first_recv'].append(copy(0, x_refs[k], slot(mx, my, 1 - mc), sibling))
        for j, (px, py) in enumerate(chips):
            out['first'].append(copy(1 + j, x_refs[k], me, (px, py, mc)))
            out['ici_recv'].append(copy(1 + j, x_refs[k], slot(px, py, mc), (px, py, mc)))
            out['passed'].append(copy(4 + j, o_refs[k].at[slot(px, py, mc)], slot(px, py, mc), sibling))
            out['passed_recv'].append(copy(4 + j, x_refs[k], slot(px, py, 1 - mc), sibling))
    return out


def _exchange_start(x_refs, o_refs, sems, gather):
    if gather:
        cps = _gather2_copies(x_refs, o_refs, sems)
        for cp in cps['local'] + cps['first']:
            cp.start()
        return
    local, sends, _ = _exchange_copies(x_refs, o_refs, sems, gather)
    for cp in local + sends:
        cp.start()


def _exchange_wait(x_refs, o_refs, sems, gather):
    if gather:
        cps = _gather2_copies(x_refs, o_refs, sems)
        for arrived, onward in zip(cps['ici_recv'], cps['passed']):
            arrived.wait_recv()
            onward.start()
        for cp in cps['first'] + cps['passed']:
            cp.wait_send()
        for cp in cps['first_recv'] + cps['passed_recv']:
            cp.wait_recv()
        for cp in cps['local']:
            cp.wait()
        return
    local, sends, recvs = _exchange_copies(x_refs, o_refs, sems, gather)
    for cp in sends:
        cp.wait_send()
    for cp in recvs:
        cp.wait_recv()
    for cp in local:
        cp.wait()


def _exchange_out_shape(xs, gather):
    return [jax.ShapeDtypeStruct(((N_DEV,) + x.shape) if gather else x.shape, x.dtype) for x in xs]


def _exchange_sems(n):
    return [pltpu.SemaphoreType.DMA((n, N_DEV - 1)), pltpu.SemaphoreType.DMA((n, N_DEV - 1)), pltpu.SemaphoreType.DMA((n,))]


SUBLANES = 8


def _store_row(ref, i, cols, row):
    base = pl.multiple_of((i // SUBLANES) * SUBLANES, SUBLANES)
    sub = lax.broadcasted_iota(jnp.int32, (SUBLANES, row.shape[1]), 0)
    ref[pl.ds(base, SUBLANES), cols] = jnp.where(sub == i % SUBLANES, row, ref[pl.ds(base, SUBLANES), cols])


def _tile_scan(a, b, reverse):
    sub = lax.broadcasted_iota(jnp.int32, a.shape, 0)
    for sh in (1, 2, 4):
        if reverse:
            live = sub < SUBLANES - sh
            a_s, b_s = pltpu.roll(a, SUBLANES - sh, 0), pltpu.roll(b, SUBLANES - sh, 0)
        else:
            live = sub >= sh
            a_s, b_s = pltpu.roll(a, sh, 0), pltpu.roll(b, sh, 0)
        b = jnp.where(live, a * b_s, 0.0) + b
        a = jnp.where(live, a * a_s, a)
    return a, b


def _lru_scan(a, b, name):
    t, w = a.shape

    def body(a_ref, b_ref, h_ref):
        def tile(j, h):
            rows = pl.ds(pl.multiple_of(j * SUBLANES, SUBLANES), SUBLANES)
            ca, cb = _tile_scan(a_ref[rows, :], b_ref[rows, :], False)
            out = ca * h + cb
            h_ref[rows, :] = out
            return out[SUBLANES - 1:SUBLANES]
        lax.fori_loop(0, t // SUBLANES, tile, jnp.zeros((1, w), F32))

    return pl.pallas_call(body, name=name, out_shape=jax.ShapeDtypeStruct((t, w), F32))(a, b)


def _lru_scan_bwd(a_next, h_prev, dhs, name):
    t, w = dhs.shape

    def body(a_ref, h_ref, dh_ref, da_ref, db_ref):
        def tile(n, lam):
            rows = pl.ds(pl.multiple_of((t // SUBLANES - 1 - n) * SUBLANES, SUBLANES), SUBLANES)
            ca, cb = _tile_scan(a_ref[rows, :], dh_ref[rows, :], True)
            out = ca * lam + cb
            db_ref[rows, :] = out
            da_ref[rows, :] = out * h_ref[rows, :]
            return out[0:1]
        lax.fori_loop(0, t // SUBLANES, tile, jnp.zeros((1, w), F32))

    return pl.pallas_call(body, name=name, out_shape=[jax.ShapeDtypeStruct((t, w), F32)] * 2)(a_next, h_prev, dhs)


N_PAIR = RW_W // LANES
UNROLL = 2


def _bcast_cols(v, bd2, name):
    t = v.shape[0]

    def body(v_ref, bd_ref, o_ref):
        bdv = bd_ref[...]
        sub = lax.broadcasted_iota(jnp.int32, (HEAD, LANES), 0)
        own = lax.broadcasted_iota(jnp.int32, (HEAD, LANES), 1) % HEAD == sub
        for i in range(CHUNK):
            row = v_ref[i:i + 1, :]
            sums = _group_sums([jnp.where(own, row[:, p * LANES:(p + 1) * LANES], 0.0) for p in range(N_PAIR)], bdv)
            for p in range(N_PAIR):
                o_ref[i, p] = sums[p * HEAD:(p + 1) * HEAD]

    return pl.pallas_call(
        body, name=name, grid=(t // CHUNK,),
        in_specs=[pl.BlockSpec((CHUNK, RW_W), lambda i: (i, 0)), pl.BlockSpec((2 * LANES, LANES), lambda i: (0, 0))],
        out_specs=pl.BlockSpec((CHUNK, N_PAIR, HEAD, LANES), lambda i: (i, 0, 0, 0)),
        out_shape=jax.ShapeDtypeStruct((t, N_PAIR, HEAD, LANES), F32),
    )(v, bd2)


def _cols_to_rows(cols_ref, rows_ref):
    lane = lax.broadcasted_iota(jnp.int32, (CHUNK, LANES), 1)
    for p in range(N_PAIR):
        tile = cols_ref[p]
        sq = jnp.concatenate([tile, jnp.zeros_like(tile)], axis=0).T
        rows_ref[:, p * LANES:(p + 1) * LANES] = jnp.where(lane < HEAD, sq[0:CHUNK], pltpu.roll(sq[CHUNK:2 * CHUNK], HEAD, 1))


def _stacked_bf16(bd):
    return jnp.concatenate([bd, bd], axis=0).astype(BF16)


def _group_sums(prods, bd2):
    x = jnp.concatenate(prods, axis=0)
    hi = x.astype(BF16)
    lo = (x - hi.astype(F32)).astype(BF16)
    return jnp.dot(jnp.concatenate([hi, lo], axis=1), bd2, preferred_element_type=F32)


def _rw_scan(r, r_prev, w, k, c, c_next, b, vb, bd2, name, ride=None):
    t = w.shape[0]
    nch = t // CHUNK
    ride_xs, ride_gather = ride if ride is not None else ([], False)
    n_ride = len(ride_xs)

    def body(*refs):
        r_ref, rp_ref, w_ref, k_ref, c_ref, cn_ref, b_ref, vb_ref, bd_ref = refs[:9]
        x_refs = refs[9:9 + n_ride]
        spre_ref, y_ref = refs[9 + n_ride:11 + n_ride]
        o_refs = refs[11 + n_ride:11 + 2 * n_ride]
        s_ref, wc_ref, wr_ref, bc_ref, kc_ref, br_ref, kr_ref, yt_ref = refs[11 + 2 * n_ride:19 + 2 * n_ride]
        sems = refs[19 + 2 * n_ride:]

        @pl.when(pl.program_id(0) == 0)
        def _():
            s_ref[...] = jnp.zeros_like(s_ref)
            if n_ride:
                _exchange_start(x_refs, o_refs, sems, ride_gather)

        yt_ref[...] = jnp.zeros_like(yt_ref)
        bdv = bd_ref[...]
        lane = lax.broadcasted_iota(jnp.int32, (HEAD, LANES), 1) % CHUNK

        wv, cn, rv, bv, kv = w_ref[...], cn_ref[...], r_ref[...], b_ref[...], k_ref[...]
        wc_ref[...] = wv * cn
        wr_ref[...] = wv * rv
        for ref, x in ((bc_ref, bv * cn), (kc_ref, kv * cn), (br_ref, bv * rv), (kr_ref, kv * rv)):
            sums = _group_sums([x[:, q * LANES:(q + 1) * LANES] for q in range(N_PAIR)], bdv)
            for q in range(N_PAIR):
                ref[:, q * LANES:(q + 1) * LANES] = sums[q * CHUNK:(q + 1) * CHUNK]

        def cut(ref, i):
            x = ref[pl.ds(i, 1), :]
            return [x[:, p * LANES:(p + 1) * LANES] for p in range(N_PAIR)]

        def two_steps(j, st):
            i0 = 2 * j
            i1 = i0 + 1
            c0, wc0, rp0, wr0 = cut(c_ref, i0), cut(wc_ref, i0), cut(rp_ref, i0), cut(wr_ref, i0)
            w0, b0, k0, w1, b1, k1 = cut(w_ref, i0), cut(b_ref, i0), cut(k_ref, i0), cut(w_ref, i1), cut(b_ref, i1), cut(k_ref, i1)
            bc0, kc0, br0, kr0 = cut(bc_ref, i0), cut(kc_ref, i0), cut(br_ref, i0), cut(kr_ref, i0)
            pairs = range(N_PAIR)
            red = _group_sums([st[p] * c0[p] for p in pairs] + [st[p] * wc0[p] for p in pairs], bdv)
            out = _group_sums([st[p] * rp0[p] for p in pairs] + [st[p] * wr0[p] for p in pairs], bdv)
            new = []
            for p in pairs:
                v0, v1 = vb_ref[i0, p], vb_ref[i1, p]
                sa0 = red[p * HEAD:(p + 1) * HEAD]
                sa1 = red[(N_PAIR + p) * HEAD:(N_PAIR + p + 1) * HEAD] + sa0 * bc0[p] + v0 * kc0[p]
                y_before = out[p * HEAD:(p + 1) * HEAD]
                y0 = out[(N_PAIR + p) * HEAD:(N_PAIR + p + 1) * HEAD] + sa0 * br0[p] + v0 * kr0[p]
                spre_ref[i0, p] = st[p]
                s1 = st[p] * w0[p] + sa0 * b0[p] + v0 * k0[p]
                spre_ref[i1, p] = s1
                new.append(s1 * w1[p] + sa1 * b1[p] + v1 * k1[p])
                yt_ref[p] = jnp.where(lane == i0, y_before, jnp.where(lane == i1, y0, yt_ref[p]))
            return tuple(new)

        st = lax.fori_loop(0, CHUNK // 2, two_steps, tuple(s_ref[p] for p in range(N_PAIR)))
        for p in range(N_PAIR):
            s_ref[p] = st[p]
        _cols_to_rows(yt_ref, y_ref)

        if n_ride:
            @pl.when(pl.program_id(0) == nch - 1)
            def _():
                _exchange_wait(x_refs, o_refs, sems, ride_gather)

    row = pl.BlockSpec((CHUNK, RW_W), lambda i: (i, 0))
    big = pl.BlockSpec((CHUNK, N_PAIR, HEAD, LANES), lambda i: (i, 0, 0, 0))
    any_spec = pl.BlockSpec(memory_space=pl.ANY)
    return pl.pallas_call(
        body, name=name, grid=(nch,),
        in_specs=[row] * 7 + [big, pl.BlockSpec((2 * LANES, LANES), lambda i: (0, 0))] + [any_spec] * n_ride,
        out_specs=[big, row] + [any_spec] * n_ride,
        out_shape=[jax.ShapeDtypeStruct((t, N_PAIR, HEAD, LANES), F32), jax.ShapeDtypeStruct((t, RW_W), F32)]
        + _exchange_out_shape(ride_xs, ride_gather),
        scratch_shapes=[pltpu.VMEM((N_PAIR, HEAD, LANES), F32)] + [pltpu.VMEM((CHUNK, RW_W), F32)] * 6
        + [pltpu.VMEM((N_PAIR, HEAD, LANES), F32)] + (_exchange_sems(n_ride) if n_ride else []),
    )(r, r_prev, w, k, c, c_next, b, vb, bd2, *ride_xs)


def _rw_scan_bwd(r, w, k, c, b, vb, dyb, spre, bd, name, ride=None):
    t = r.shape[0]
    nch = t // CHUNK
    ride_xs, ride_gather = ride if ride is not None else ([], False)
    n_ride = len(ride_xs)
    n_pre = 8

    def body(*refs):
        r_ref, w_ref, k_ref, c_ref, b_ref, vb_ref, dyb_ref, spre_ref, bd_ref = refs[:9]
        x_refs = refs[9:9 + n_ride]
        dr_ref, dw_ref, dk_ref, dc_ref, db_ref, dv_ref = refs[9 + n_ride:15 + n_ride]
        o_refs = refs[15 + n_ride:15 + 2 * n_ride]
        g_ref, snext_ref, dvt_ref = refs[15 + 2 * n_ride:18 + 2 * n_ride]
        pre = refs[18 + 2 * n_ride:18 + 2 * n_ride + n_pre]
        sems = refs[18 + 2 * n_ride + n_pre:]
        wb_ref, wk_ref, rb_ref, rk_ref, rwb_ref, cb_ref, rwk_ref, ck_ref = pre[:8]

        @pl.when(pl.program_id(0) == 0)
        def _():
            g_ref[...] = jnp.zeros_like(g_ref)
            snext_ref[...] = jnp.zeros_like(snext_ref)
            if n_ride:
                _exchange_start(x_refs, o_refs, sems, ride_gather)

        for ref in (dr_ref, dw_ref, dk_ref, dc_ref, db_ref, dvt_ref):
            ref[...] = jnp.zeros_like(ref)
        bdv = bd_ref[...]
        bd1 = bdv[0:LANES]
        lane = lax.broadcasted_iota(jnp.int32, (HEAD, LANES), 1) % CHUNK

        rv, wv, kv, cv, bv = r_ref[...], w_ref[...], k_ref[...], c_ref[...], b_ref[...]
        b_b, k_b = pltpu.roll(bv, 1, 0), pltpu.roll(kv, 1, 0)
        wb_ref[...] = wv * b_b
        wk_ref[...] = wv * k_b
        rw = rv * wv
        for ref, x in ((rb_ref, rv * bv), (rk_ref, rv * kv), (rwb_ref, rw * b_b), (cb_ref, cv * b_b), (rwk_ref, rw * k_b),
                       (ck_ref, cv * k_b)):
            sums = _group_sums([x[:, q * LANES:(q + 1) * LANES] for q in range(N_PAIR)], bdv)
            for q in range(N_PAIR):
                ref[:, q * LANES:(q + 1) * LANES] = sums[q * CHUNK:(q + 1) * CHUNK]

        def sum0(x):
            return jnp.sum(x, axis=0, keepdims=True)

        def cut(ref, i):
            x = ref[pl.ds(i, 1), :]
            return [x[:, p * LANES:(p + 1) * LANES] for p in range(N_PAIR)]

        def sums_bf16(prods):
            return jnp.dot(jnp.concatenate(prods, axis=0).astype(BF16), bd1, preferred_element_type=F32)

        def two_steps(n, gs):
            ia = CHUNK - 1 - 2 * n
            ib = ia - 1
            r_a, w_a, k_a, c_a, b_a = [cut(ref, ia) for ref in (r_ref, w_ref, k_ref, c_ref, b_ref)]
            r_b, w_b, c_b = [cut(ref, ib) for ref in (r_ref, w_ref, c_ref)]
            wb, wk, rb_a, rk_a, rwb, cb, rwk, ck = [cut(ref, ia) for ref in pre[:8]]
            rb_b, rk_b = cut(rb_ref, ib), cut(rk_ref, ib)
            pairs = range(N_PAIR)
            sp_a = [spre_ref[ia, p] for p in pairs]
            sp_b = [spre_ref[ib, p] for p in pairs]
            dy_a = [dyb_ref[ia, p] for p in pairs]
            dy_b = [dyb_ref[ib, p] for p in pairs]
            chain = _group_sums([gs[p] * b_a[p] for p in pairs] + [gs[p] * wb[p] for p in pairs], bdv)
            off = sums_bf16([gs[p] * k_a[p] for p in pairs] + [gs[p] * wk[p] for p in pairs]
                            + [sp_a[p] * c_a[p] for p in pairs] + [sp_b[p] * c_b[p] for p in pairs])
            new = []
            for p in pairs:
                def part(x, q, p=p):
                    return x[(q * N_PAIR + p) * HEAD:(q * N_PAIR + p + 1) * HEAD]
                dsa_a = part(chain, 0) + dy_a[p] * rb_a[p]
                dv_a = part(off, 0) + dy_a[p] * rk_a[p]
                dsa_b = part(chain, 1) + dy_a[p] * rwb[p] + dsa_a * cb[p] + dy_b[p] * rb_b[p]
                dv_b = part(off, 1) + dy_a[p] * rwk[p] + dsa_a * ck[p] + dy_b[p] * rk_b[p]
                sa_a, sa_b = part(off, 2), part(off, 3)
                g_a = gs[p] + dy_a[p] * r_a[p]
                g_mid = g_a * w_a[p] + dsa_a * c_a[p]
                g_b = g_mid + dy_b[p] * r_b[p]
                new.append(g_b * w_b[p] + dsa_b * c_b[p])
                cols = pl.ds(p * LANES, LANES)
                for i, dy, s_post, s_pre, g, sa, dsa in ((ia, dy_a[p], snext_ref[p], sp_a[p], g_a, sa_a, dsa_a),
                                                         (ib, dy_b[p], sp_a[p], sp_b[p], g_b, sa_b, dsa_b)):
                    _store_row(dr_ref, i, cols, sum0(s_post * dy))
                    _store_row(dw_ref, i, cols, sum0(g * s_pre))
                    _store_row(db_ref, i, cols, sum0(g * sa))
                    _store_row(dk_ref, i, cols, sum0(g * vb_ref[i, p]))
                    _store_row(dc_ref, i, cols, sum0(s_pre * dsa))
                dvt_ref[p] = jnp.where(lane == ia, dv_a, jnp.where(lane == ib, dv_b, dvt_ref[p]))
                snext_ref[p] = sp_b[p]
            return tuple(new)

        gs = lax.fori_loop(0, CHUNK // 2, two_steps, tuple(g_ref[p] for p in range(N_PAIR)))
        for p in range(N_PAIR):
            g_ref[p] = gs[p]
        _cols_to_rows(dvt_ref, dv_ref)

        if n_ride:
            @pl.when(pl.program_id(0) == nch - 1)
            def _():
                _exchange_wait(x_refs, o_refs, sems, ride_gather)

    row = pl.BlockSpec((CHUNK, RW_W), lambda i: (nch - 1 - i, 0))
    big = pl.BlockSpec((CHUNK, N_PAIR, HEAD, LANES), lambda i: (nch - 1 - i, 0, 0, 0))
    any_spec = pl.BlockSpec(memory_space=pl.ANY)
    return pl.pallas_call(
        body, name=name, grid=(nch,),
        in_specs=[row] * 5 + [big, big, big, pl.BlockSpec((2 * LANES, LANES), lambda i: (0, 0))] + [any_spec] * n_ride,
        out_specs=[row] * 6 + [any_spec] * n_ride,
        out_shape=[jax.ShapeDtypeStruct((t, RW_W), F32)] * 6 + _exchange_out_shape(ride_xs, ride_gather),
        scratch_shapes=[pltpu.VMEM((N_PAIR, HEAD, LANES), F32)] * 3
        + [pltpu.VMEM((CHUNK, RW_W), F32)] * n_pre + (_exchange_sems(n_ride) if n_ride else []),
    )(r, w, k, c, b, vb, dyb, spre, bd, *ride_xs)


def _exchange(xs, *, gather, name):
    n = len(xs)

    def body(*refs):
        _exchange_start(refs[:n], refs[n:2 * n], refs[2 * n:], gather)
        _exchange_wait(refs[:n], refs[n:2 * n], refs[2 * n:], gather)

    any_spec = pl.BlockSpec(memory_space=pl.ANY)
    return pl.pallas_call(
        body, name=name, in_specs=[any_spec] * n, out_specs=[any_spec] * n, out_shape=_exchange_out_shape(xs, gather),
        scratch_shapes=_exchange_sems(n),
    )(*xs)


def _adamw_rows(g, w, m, v):
    m = ADAM_B1 * m + (1.0 - ADAM_B1) * g
    v = ADAM_B2 * v + (1.0 - ADAM_B2) * (g * g)
    m_hat = m / (1.0 - ADAM_B1 ** ADAM_STEP)
    v_hat = v / (1.0 - ADAM_B2 ** ADAM_STEP)
    return -ADAM_LR * (m_hat / (jnp.sqrt(v_hat) + ADAM_EPS) + ADAM_WD * w), m, v


def _sum_slots(parts):
    g = parts[0].astype(F32)
    for q in range(1, N_DEV):
        g = g + parts[q].astype(F32)
    return g


def _reduce_adamw(parts, w, m, v, name):
    rows, cols = w.shape

    def fn(parts, w, m, v):
        g = _sum_slots(parts)
        return (g,) + _adamw_rows(g, w, m, v)

    return _rowwise(fn, [parts, w, m, v], [], [(cols, F32)] * 4, [], tm=_pick(rows, (256, 128, 64, 32, 16, 8)), name=name)


def _shift(x, n):
    return jnp.pad(x, ((n, 0), (0, 0)))[:-n]


def _unshift(x, n):
    return jnp.pad(x, ((0, n), (0, 0)))[n:]


def _add_n(xs, *, tm, name):
    def fn(*vals):
        s = vals[0]
        for x in vals[1:]:
            s = s + x
        return s
    return _rowwise(fn, xs, [], [(xs[0].shape[1], F32)], [], tm=tm, name=name)[0]


def _norm_fwd(h, g, *, tm, name):
    return _rowwise(lambda x, gg: _rms(x, gg), [h], [g], [(h.shape[1], BF16)], [], tm=tm, name=name)[0]


def _res_norm_fwd(h, f, g, scale, *, tm, name):
    return _rowwise(lambda hh, ff, gg: hh + scale * _rms(ff, gg), [h, f], [g], [(h.shape[1], F32)], [], tm=tm, name=name)[0]


def _norm_bwd(x, g, dy, scale, res, out_dtype, *, tm, name):
    if res is None:
        def fn(xx, dd, gg):
            dx, dg = _rms_bwd(xx, gg, dd * scale)
            return dx, dg
        rows = [x, dy]
    else:
        def fn(xx, dd, rr, gg):
            dx, dg = _rms_bwd(xx, gg, dd * scale)
            return dx + rr, dg
        rows = [x, dy, res]
    return _rowwise(fn, rows, [g], [(x.shape[1], out_dtype)], [(1, x.shape[1])], tm=tm, name=name)


def _ffn_fwd(h, g_pre, g_post, w24, wo4, tiles, tag):
    tb, ts = tiles
    a = _norm_fwd(h, g_pre, tm=ts, name=f"{tag}_norm")
    gu, s4 = _ffn_in(a, w24, tm=tb, name=f"{tag}_in")
    f = _mm(s4, wo4, trans_b=False, tm=tb, tn=D_MODEL, out_dtype=F32, name=f"{tag}_out")
    h_new = _res_norm_fwd(h, f, g_post, 0.5, tm=ts, name=f"{tag}_res")
    return h_new, (h, a, gu, s4, f)


def _ffn_bwd(dh_new, res, g_pre, g_post, w24, wo4, tiles, tag):
    tb, ts = tiles
    h, a, gu, s4, f = res
    t = h.shape[0]
    df, dg_post = _norm_bwd(f, g_post, dh_new, 0.5, None, BF16, tm=ts, name=f"{tag}_dres")
    dgu = _ffn_dswiglu(df, wo4, gu, tm=tb, name=f"{tag}_dswiglu")
    d_wo = _mm_tn(s4, df[None], tk=FFN_BLK, name=f"{tag}_dwout")
    dgu8 = dgu.reshape(2 * w24.shape[1], t, FFN_BLK)
    w8 = w24.reshape(2 * w24.shape[1], D_MODEL, FFN_BLK)
    da = _mm(dgu8, w8, trans_b=True, tm=tb, tn=D_MODEL, out_dtype=F32, name=f"{tag}_da")
    d_win = _mm_tn(a[None], dgu8, tk=D_MODEL, name=f"{tag}_dwin")
    dh, dg_pre = _norm_bwd(h, g_pre, da, 1.0, dh_new, F32, tm=ts, name=f"{tag}_dnorm")
    return dh, dg_pre, dg_post, d_win, d_wo.reshape(N_DEV, -1, D_MODEL)


def _blockdiag(w4):
    n, b, _ = w4.shape
    eye = jnp.eye(n, dtype=w4.dtype)
    return (eye[:, None, :, None] * w4[:, :, None, :]).reshape(n * b, n * b)


def _blockdiag_grad(d):
    n = d.shape[0] // HEAD
    x = d.reshape(n, HEAD, n, HEAD)
    return jnp.stack([x[i, :, i, :] for i in range(n)])


def _row(v):
    return v.reshape(1, -1)


def _mixer_fwd(h, g_pre, g_post, wi, wo, P, bd, tiles, tag, ride=None):
    tb, ts = tiles
    a = _norm_fwd(h, g_pre, tm=ts, name=f"{tag}_norm")
    p = _mm(a[None], wi[None], trans_b=False, tm=tb, tn=N_IN, out_dtype=F32, name=f"{tag}_in")
    lx, lg = p[:, 0:256], p[:, 256:512]
    sb, scc, sx = p[:, 512:768], p[:, 768:1024], p[:, 1024:1280]
    z = p[:, 1280:]
    lxs = [lx, _shift(lx, 1), _shift(lx, 2), _shift(lx, 3)]
    cw = [_row(P['lru_conv_w'][kk]) for kk in range(4)]
    lru_par = cw + [_row(P['lru_conv_b']), _blockdiag(P['lru_wa']), _row(P['lru_ba']), _blockdiag(P['lru_wx']),
                    _row(P['lru_bx']), _row(P['lru_lambda'])]
    la, lb = _rowwise(_lru_pre, lxs, lru_par, [(LRU_W, F32)] * 2, [], tm=ts, name=f"{tag}_lru_pre")
    hs = _lru_scan(la, lb, name=f"{tag}_lru_scan")
    y_lru = _rowwise(lambda gg, hh, ng, b_: _lru_post(b_, gg, hh, ng), [lg, hs], [_row(P['lru_norm_g']), bd],
                     [(LRU_W, F32)], [], tm=ts, name=f"{tag}_lru_post")[0]
    sc_rows = [sb, scc, sx, _shift(scc, 1), _shift(sx, 1), _shift(scc, 2), _shift(sx, 2)]
    sc_par = [_row(P['sc_conv_w'][kk]) for kk in range(3)] + [_row(P['sc_norm_g'])]
    y_sc = _rowwise(lambda *v: _sc_fwd(v[-1], *v[:-1]), sc_rows, sc_par + [bd], [(SC_W, F32)], [], tm=ts,
                    name=f"{tag}_sc")[0]
    cuts = (0, RW_W, 2 * RW_W, 3 * RW_W, RW_IN)
    zs = [z[:, cuts[q]:cuts[q + 1]] for q in range(4)]
    z_rows = zs + [_shift(q, 1) for q in zs]
    pad = lambda m, lo: jnp.pad(m, ((lo, LANES - lo - m.shape[0]), (0, 0)))
    rw_par = [_row(P['rwkv_mu'][cuts[q]:cuts[q + 1]]) for q in range(4)]
    rw_par += [_row(P['rwkv_w0']), pad(P['rwkv_w2'], 0), _row(P['rwkv_a0']), pad(P['rwkv_a2'], 32),
               pad(P['rwkv_g2'], 64), _row(P['rwkv_k_k']), _row(P['rwkv_k_a'])]
    r, w, k2, c, b, v, g = _rowwise(lambda *vv: _rw_pre(vv[-1], *vv[:-1]), z_rows, rw_par + [bd], [(RW_W, F32)] * 7, [],
                                    tm=ts, name=f"{tag}_rw_pre")
    vb = _bcast_cols(v, _stacked_bf16(bd), name=f"{tag}_rw_vcols")
    spre, yt, *rode = _rw_scan(r, _shift(r, 1), w, k2, c, _unshift(c, 1), b, vb, _stacked_bf16(bd), name=f"{tag}_rw_scan",
                               ride=ride)
    y = _unshift(yt, 1)
    post_par =[_row(P['rwkv_lnx_w']), _row(P['rwkv_lnx_b']), _row(P['rwkv_r_k'])]
    y_rw = _rowwise(lambda *vv: _rw_post(vv[-1], *vv[:-1]), [y, r, k2, v, g], post_par + [bd], [(RW_W, F32)], [], tm=ts,
                    name=f"{tag}_rw_post")[0]
    ycat = jnp.concatenate([y_lru, y_sc, y_rw], axis=1).astype(BF16)
    m = _mm(ycat[None], wo[None], trans_b=False, tm=tb, tn=D_MODEL, out_dtype=F32, name=f"{tag}_out")
    h_new = _res_norm_fwd(h, m, g_post, 1.0, tm=ts, name=f"{tag}_res")
    res = dict(h=h, a=a, m=m, ycat=ycat, lxs=lxs, lru_par=lru_par, lg=lg, la=la, hs=hs, sc_rows=sc_rows, sc_par=sc_par,
               z_rows=z_rows, rw_par=rw_par, r=r, w=w, k2=k2, c=c, b=b, v=v, g=g, vb=vb, spre=spre, y=y, post_par=post_par)
    return h_new, res, rode


def _mixer_bwd(dh_new, R, g_pre, g_post, wi, wo, P, bd, tiles, tag, ride=None):
    tb, ts = tiles
    dm, dg_post = _norm_bwd(R['m'], g_post, dh_new, 1.0, None, BF16, tm=ts, name=f"{tag}_dres")
    dycat = _mm(dm[None], wo[None], trans_b=True, tm=tb, tn=D_MODEL, out_dtype=F32, name=f"{tag}_dycat")
    d_wo = _mm_tn(R['ycat'][None], dm[None], tk=D_MODEL // 2, name=f"{tag}_dwout")[0]
    dy_lru, dy_sc, dy_rw = dycat[:, 0:256], dycat[:, 256:512], dycat[:, 512:]
    G = {}
    d_lg, d_hs, G['lru_norm_g'] = _rowwise(
        lambda gg, hh, ct, ng, b_: _vjp_rows(_lru_post, 1, 3, 1)(b_, gg, hh, ng, ct),
        [R['lg'], R['hs'], dy_lru], [_row(P['lru_norm_g']), bd], [(LRU_W, F32)] * 2, [(1, LRU_W)], tm=ts,
        name=f"{tag}_lru_dpost")
    d_la, d_lb = _lru_scan_bwd(_unshift(R['la'], 1), _shift(R['hs'], 1), d_hs, name=f"{tag}_lru_dscan")

    def lru_pre_bwd(x0, x1, x2, x3, ca, cb_, *par):
        return _vjp_rows(_lru_pre, 0, 14, 2)(x0, x1, x2, x3, *par, ca, cb_)

    par_shapes = [tuple(q.shape) for q in R['lru_par']]
    outs = _rowwise(lru_pre_bwd, R['lxs'] + [d_la, d_lb], R['lru_par'], [(LRU_W, F32)] * 4, par_shapes, tm=ts,
                    name=f"{tag}_lru_dpre")
    dxs, dpar = outs[:4], outs[4:]
    d_lx = _add_n([dxs[0], _unshift(dxs[1], 1), _unshift(dxs[2], 2), _unshift(dxs[3], 3)], tm=ts, name=f"{tag}_lru_dx")
    G['lru_conv_w'] = jnp.concatenate(dpar[0:4], axis=0)
    G['lru_conv_b'] = dpar[4][0]
    G['lru_wa'] = _blockdiag_grad(dpar[5])
    G['lru_ba'] = dpar[6][0]
    G['lru_wx'] = _blockdiag_grad(dpar[7])
    G['lru_bx'] = dpar[8][0]
    G['lru_lambda'] = dpar[9][0]
    G['lru_norm_g'] = G['lru_norm_g'][0]

    def sc_bwd(*vv):
        rows7, ct, par4, b_ = vv[:7], vv[7], vv[8:12], vv[12]
        return _vjp_rows(_sc_fwd, 1, 11, 1)(b_, *rows7, *par4, ct)

    outs = _rowwise(sc_bwd, R['sc_rows'] + [dy_sc], R['sc_par'] + [bd], [(SC_W, F32)] * 7, [(1, SC_W)] * 4, tm=ts,
                    name=f"{tag}_sc_bwd")
    d_sb = outs[0]
    d_sc = _add_n([outs[1], _unshift(outs[3], 1), _unshift(outs[5], 2)], tm=ts, name=f"{tag}_sc_dc")
    d_sx = _add_n([outs[2], _unshift(outs[4], 1), _unshift(outs[6], 2)], tm=ts, name=f"{tag}_sc_dx")
    G['sc_conv_w'] = jnp.concatenate(outs[7:10], axis=0)
    G['sc_norm_g'] = outs[10][0]

    def rw_post_bwd(*vv):
        rows5, ct, par3, b_ = vv[:5], vv[5], vv[6:9], vv[9]
        return _vjp_rows(_rw_post, 1, 8, 1)(b_, *rows5, *par3, ct)

    outs = _rowwise(rw_post_bwd, [R['y'], R['r'], R['k2'], R['v'], R['g'], dy_rw], R['post_par'] + [bd],
                    [(RW_W, F32)] * 5, [(1, RW_W)] * 3, tm=ts, name=f"{tag}_rw_dpost")
    d_y, dr_p, dk_p, dv_p, d_g = outs[:5]
    G['rwkv_lnx_w'], G['rwkv_lnx_b'], G['rwkv_r_k'] = outs[5][0], outs[6][0], outs[7][0]
    dyb = _bcast_cols(d_y, _stacked_bf16(bd), name=f"{tag}_rw_dycols")
    dr_s, d_w, dk_s, d_c, d_b, dvt, *rode = _rw_scan_bwd(R['r'], R['w'], R['k2'], R['c'], R['b'], R['vb'], dyb, R['spre'],
                                                        _stacked_bf16(bd), name=f"{tag}_rw_dscan", ride=ride)
    dv_s = dvt

    def rw_pre_bwd(*vv):
        zrows = vv[0:8]
        dr1, dr2, dw_, dk1, dk2_, dc_, db_, dv1, dv2, dg_ = vv[8:18]
        par, b_ = vv[18:29], vv[29]
        return _vjp_rows(_rw_pre, 1, 19, 7)(b_, *zrows, *par, dr1 + dr2, dw_, dk1 + dk2_, dc_, db_, dv1 + dv2, dg_)

    par_shapes = [tuple(q.shape) for q in R['rw_par']]
    widths = [(q.shape[1], F32) for q in R['z_rows']]
    outs = _rowwise(rw_pre_bwd, R['z_rows'] + [dr_p, dr_s, d_w, dk_p, dk_s, d_c, d_b, dv_p, dv_s, d_g],
                    R['rw_par'] + [bd], widths, par_shapes, tm=ts, name=f"{tag}_rw_dpre")
    d_z = _add_n([jnp.concatenate(outs[0:4], axis=1), _unshift(jnp.concatenate(outs[4:8], axis=1), 1)], tm=ts,
                 name=f"{tag}_rw_dz")
    dpar = outs[8:]
    G['rwkv_mu'] = jnp.concatenate([q[0] for q in dpar[0:4]])
    G['rwkv_w0'], G['rwkv_a0'] = dpar[4][0], dpar[6][0]
    G['rwkv_w2'], G['rwkv_a2'], G['rwkv_g2'] = dpar[5][0:32], dpar[7][32:64], dpar[8][64:128]
    G['rwkv_k_k'], G['rwkv_k_a'] = dpar[9][0], dpar[10][0]

    dp = jnp.concatenate([d_lx, d_lg, d_sb, d_sc, d_sx, d_z], axis=1).astype(BF16)
    da = _mm(dp[None], wi[None], trans_b=True, tm=tb, tn=D_MODEL, out_dtype=F32, name=f"{tag}_da")
    d_wi = _mm_tn(R['a'][None], dp[None], tk=D_MODEL // 2, name=f"{tag}_dwin")[0]
    dh, dg_pre = _norm_bwd(R['h'], g_pre, da, 1.0, dh_new, F32, tm=ts, name=f"{tag}_dnorm")
    return dh, dg_pre, dg_post, d_wi, d_wo, G, rode


def _loss_rows(h, tgt, n_seq, *, tm, name):
    d = h.shape[1]

    def body(h_ref, t_ref, dh_ref, l_ref):
        i = pl.program_id(0)
        row = lax.broadcasted_iota(jnp.int32, (tm, 1), 0) + i * tm
        live = (row >= N_META) & (row < N_META + n_seq)
        e = jnp.where(live, h_ref[...] - t_ref[...], 0.0)
        dh_ref[...] = e * (1.0 / d)
        part = 0.5 * jnp.sum(jnp.sum(e * e, axis=1, keepdims=True) * (1.0 / d), axis=0, keepdims=True)

        @pl.when(i == 0)
        def _():
            l_ref[...] = part

        @pl.when(i > 0)
        def _():
            l_ref[...] += part

    blk = pl.BlockSpec((tm, d), lambda i: (i, 0))
    return pl.pallas_call(body, name=name, grid=(h.shape[0] // tm,), in_specs=[blk, blk],
                          out_specs=[blk, pl.BlockSpec((1, 1), lambda i: (0, 0))],
                          out_shape=[jax.ShapeDtypeStruct(h.shape, F32), jax.ShapeDtypeStruct((1, 1), F32)])(h, tgt)


def _pack(arrs, mult):
    flat = jnp.concatenate([a.reshape(-1).astype(F32) for a in arrs])
    n = flat.shape[0]
    tot = -(-n // mult) * mult
    return jnp.pad(flat, (0, tot - n)).reshape(-1, LANES)


def _unpack(buf, shapes):
    flat = buf.reshape(-1)
    out, off = [], 0
    for s in shapes:
        n = 1
        for q in s:
            n *= q
        out.append(flat[off:off + n].reshape(s))
        off += n
    return out


def _step(W, M, V, x, loss_target):
    n_seq = x.shape[1]
    t_real = N_META + n_seq
    t = (t_real // CHUNK + 1) * CHUNK
    tiles = (_pick(t, (704, 512, 256, 128, 64)), _pick(t, (192, 128, 64)))
    me = 4 * lax.axis_index("x") + 2 * lax.axis_index("y") + lax.axis_index("c")
    n_layer = W['norm_g'].shape[0]

    small_sh = list(SMALL_SHARDED)
    packed = _pack([W[n] for n in small_sh], 8 * LANES)
    gathered = _exchange([W[n][0].astype(BF16) for n in BIG] + [packed], gather=True, name="gather_weights")
    big8 = [dict(zip(BIG, gathered[:-1]))] + [None] * (n_layer - 1)
    pieces = [_unpack(gathered[-1][q], [W[n].shape for n in small_sh]) for q in range(N_DEV)]
    full = {n: W[n] for n in SMALL if n not in SMALL_SHARDED}
    for idx, n in enumerate(small_sh):
        full[n] = jnp.concatenate([pieces[q][idx] for q in range(N_DEV)], axis=SMALL_SHARDED[n])

    def layer_weights(l):
        w24_1 = big8[l]['ffn1_w_in'].reshape(2, N_DEV // 2, D_MODEL, FFN_BLK)
        wo4_1 = big8[l]['ffn1_w_out'].reshape(N_DEV // 2, FFN_BLK, D_MODEL)
        w24_2 = big8[l]['ffn2_w_in'].reshape(2, N_DEV // 2, D_MODEL, FFN_BLK)
        wo4_2 = big8[l]['ffn2_w_out'].reshape(N_DEV // 2, FFN_BLK, D_MODEL)
        wi = big8[l]['mix_w_in'].transpose(1, 0, 2).reshape(D_MODEL, N_IN)
        wo = big8[l]['mix_w_out'].reshape(D_MODEL, D_MODEL)
        return w24_1, wo4_1, w24_2, wo4_2, wi, wo

    bd = jnp.kron(jnp.eye(LANES // HEAD, dtype=F32), jnp.ones((HEAD, HEAD), F32))
    small_layer = [n for n in SMALL if n not in ('meta_tokens', 'norm_g')]

    h = jnp.concatenate([full['meta_tokens'], x[0], jnp.zeros((t - t_real, D_MODEL), F32)], axis=0)
    saved = []
    for l in range(n_layer):
        lw = layer_weights(l)
        ng = [_row(full['norm_g'][l, q]) for q in range(6)]
        P = {n: full[n][l] for n in small_layer}
        h, r1 = _ffn_fwd(h, ng[0], ng[1], lw[0], lw[1], tiles, f"l{l}_ffn1")
        ride = ([W[n][l + 1].astype(BF16) for n in BIG], True) if l + 1 < n_layer else None
        h, r2, rode = _mixer_fwd(h, ng[2], ng[3], lw[4], lw[5], P, bd, tiles, f"l{l}_mix", ride=ride)
        if ride is not None:
            big8[l + 1] = dict(zip(BIG, rode))
        h, r3 = _ffn_fwd(h, ng[4], ng[5], lw[2], lw[3], tiles, f"l{l}_ffn2")
        saved.append((lw, ng, P, r1, r2, r3))

    tgt = jnp.pad(loss_target[0], ((N_META, t - t_real), (0, 0)))
    dh, loss_part = _loss_rows(h, tgt, n_seq, tm=tiles[1], name="loss")
    loss = lax.psum(loss_part[0, 0], MESH_AXES)

    small_grads = [None] * n_layer
    norm_grads = [None] * n_layer
    recv = [None] * n_layer
    outgoing = None
    for l in reversed(range(n_layer)):
        lw, ng, P, r1, r2, r3 = saved[l]
        dh, g4, g5, d_win2, d_wo2 = _ffn_bwd(dh, r3, ng[4], ng[5], lw[2], lw[3], tiles, f"l{l}_ffn2")
        ride = (outgoing, False) if outgoing is not None else None
        dh, g2, g3, d_wi, d_wo, G, rode = _mixer_bwd(dh, r2, ng[2], ng[3], lw[4], lw[5], P, bd, tiles, f"l{l}_mix", ride=ride)
        if ride is not None:
            recv[l + 1] = rode
        dh, g0, g1, d_win1, d_wo1 = _ffn_bwd(dh, r1, ng[0], ng[1], lw[0], lw[1], tiles, f"l{l}_ffn1")
        small_grads[l] = G
        norm_grads[l] = jnp.concatenate([g0, g1, g2, g3, g4, g5], axis=0)
        d_wi8 = d_wi.reshape(D_MODEL, N_DEV, N_IN // N_DEV).transpose(1, 0, 2)
        d_wo8 = d_wo.reshape(N_DEV, D_MODEL // N_DEV, D_MODEL)
        outgoing = [d_win1, d_wo1, d_win2, d_wo2, d_wi8, d_wo8]
    recv[0] = _exchange(outgoing, gather=False, name="l0_grad_exchange")

    gs = {n: jnp.stack([small_grads[l][n] for l in range(n_layer)]) for n in small_layer}
    gs['norm_g'] = jnp.stack(norm_grads)
    gs['meta_tokens'] = dh[:N_META]
    gpack = _pack([gs[n] for n in SMALL], 8 * LANES)
    gall = _exchange([gpack], gather=True, name="gather_small_grads")[0]
    gsum = _rowwise(lambda parts: _sum_slots(parts), [gall], [], [(LANES, F32)], [], tm=gall.shape[1],
                    name="sum_small_grads")[0]
    gfull = dict(zip(SMALL, _unpack(gsum, [gs[n].shape for n in SMALL])))

    def my_shard(n, a):
        if n not in SMALL_SHARDED:
            return a
        ax = SMALL_SHARDED[n]
        size = a.shape[ax] // N_DEV
        return lax.dynamic_slice_in_dim(a, me * size, size, axis=ax)

    g_loc = [my_shard(n, gfull[n]) for n in SMALL]
    shapes = [W[n].shape for n in SMALL]
    bufs = [_pack(g_loc, 8 * LANES)] + [_pack([D[n] for n in SMALL], 8 * LANES) for D in (W, M, V)]
    d_s, m_s, v_s = _rowwise(_adamw_rows, bufs, [], [(LANES, F32)] * 3, [], tm=bufs[0].shape[0], name="adamw_small")
    out = {'grad': dict(zip(SMALL, g_loc)), 'delta': dict(zip(SMALL, _unpack(d_s, shapes))),
           'm': dict(zip(SMALL, _unpack(m_s, shapes))), 'v': dict(zip(SMALL, _unpack(v_s, shapes)))}

    order = ['ffn1_w_in', 'ffn1_w_out', 'ffn2_w_in', 'ffn2_w_out', 'mix_w_in', 'mix_w_out']
    for idx, n in enumerate(order):
        per_layer = []
        for l in range(n_layer):
            parts = recv[l][idx]
            rows, cols = W[n].shape[1], W[n].shape[2]
            per_layer.append(_reduce_adamw(parts.reshape(N_DEV, rows, cols), W[n][l], M[n][l], V[n][l],
                                           name=f"l{l}_adamw_{n}"))
        for q, key in enumerate(('grad', 'delta', 'm', 'v')):
            out[key][n] = jnp.stack([per_layer[l][q] for l in range(n_layer)])

    return (loss, dh[N_META:t_real][None],
            *[out['grad'][n] for n in WEIGHTS], *[out['delta'][n] for n in WEIGHTS],
            *[out['m'][n] for n in WEIGHTS], *[out['v'][n] for n in WEIGHTS])


def kernel(x, meta_tokens, norm_g, ffn1_w_in, ffn1_w_out, ffn2_w_in, ffn2_w_out, mix_w_in, mix_w_out, lru_conv_w, lru_conv_b, lru_wa, lru_ba, lru_wx, lru_bx, lru_lambda, lru_norm_g, sc_conv_w, sc_norm_g, rwkv_mu, rwkv_w0, rwkv_w2, rwkv_a0, rwkv_a2, rwkv_g2, rwkv_k_k, rwkv_k_a, rwkv_r_k, rwkv_lnx_w, rwkv_lnx_b, loss_target, m_meta_tokens, m_norm_g, m_ffn1_w_in, m_ffn1_w_out, m_ffn2_w_in, m_ffn2_w_out, m_mix_w_in, m_mix_w_out, m_lru_conv_w, m_lru_conv_b, m_lru_wa, m_lru_ba, m_lru_wx, m_lru_bx, m_lru_lambda, m_lru_norm_g, m_sc_conv_w, m_sc_norm_g, m_rwkv_mu, m_rwkv_w0, m_rwkv_w2, m_rwkv_a0, m_rwkv_a2, m_rwkv_g2, m_rwkv_k_k, m_rwkv_k_a, m_rwkv_r_k, m_rwkv_lnx_w, m_rwkv_lnx_b, v_meta_tokens, v_norm_g, v_ffn1_w_in, v_ffn1_w_out, v_ffn2_w_in, v_ffn2_w_out, v_mix_w_in, v_mix_w_out, v_lru_conv_w, v_lru_conv_b, v_lru_wa, v_lru_ba, v_lru_wx, v_lru_bx, v_lru_lambda, v_lru_norm_g, v_sc_conv_w, v_sc_norm_g, v_rwkv_mu, v_rwkv_w0, v_rwkv_w2, v_rwkv_a0, v_rwkv_a2, v_rwkv_g2, v_rwkv_k_k, v_rwkv_k_a, v_rwkv_r_k, v_rwkv_lnx_w, v_rwkv_lnx_b):
    w_vals = (meta_tokens, norm_g, ffn1_w_in, ffn1_w_out, ffn2_w_in, ffn2_w_out, mix_w_in, mix_w_out, lru_conv_w, lru_conv_b, lru_wa, lru_ba, lru_wx, lru_bx, lru_lambda, lru_norm_g, sc_conv_w, sc_norm_g, rwkv_mu, rwkv_w0, rwkv_w2, rwkv_a0, rwkv_a2, rwkv_g2, rwkv_k_k, rwkv_k_a, rwkv_r_k, rwkv_lnx_w, rwkv_lnx_b)
    m_vals = (m_meta_tokens, m_norm_g, m_ffn1_w_in, m_ffn1_w_out, m_ffn2_w_in, m_ffn2_w_out, m_mix_w_in, m_mix_w_out, m_lru_conv_w, m_lru_conv_b, m_lru_wa, m_lru_ba, m_lru_wx, m_lru_bx, m_lru_lambda, m_lru_norm_g, m_sc_conv_w, m_sc_norm_g, m_rwkv_mu, m_rwkv_w0, m_rwkv_w2, m_rwkv_a0, m_rwkv_a2, m_rwkv_g2, m_rwkv_k_k, m_rwkv_k_a, m_rwkv_r_k, m_rwkv_lnx_w, m_rwkv_lnx_b)
    v_vals = (v_meta_tokens, v_norm_g, v_ffn1_w_in, v_ffn1_w_out, v_ffn2_w_in, v_ffn2_w_out, v_mix_w_in, v_mix_w_out, v_lru_conv_w, v_lru_conv_b, v_lru_wa, v_lru_ba, v_lru_wx, v_lru_bx, v_lru_lambda, v_lru_norm_g, v_sc_conv_w, v_sc_norm_g, v_rwkv_mu, v_rwkv_w0, v_rwkv_w2, v_rwkv_a0, v_rwkv_a2, v_rwkv_g2, v_rwkv_k_k, v_rwkv_k_a, v_rwkv_r_k, v_rwkv_lnx_w, v_rwkv_lnx_b)
    return _step(dict(zip(WEIGHTS, w_vals)), dict(zip(WEIGHTS, m_vals)), dict(zip(WEIGHTS, v_vals)), x, loss_target)
```

```python
import functools

import jax
import jax.numpy as jnp
from jax import lax
from jax.experimental import pallas as pl
from jax.experimental.pallas import tpu as pltpu

F32 = jnp.float32
BF16 = jnp.bfloat16
HIGHEST = lax.Precision.HIGHEST

N_DEV = 8
MESH_AXES = ("x", "y", "c")
N_META = 16
D_MODEL = 1024
LRU_W = 256
SC_W = 256
RW_W = 512
HEAD = 64
LANES = 128
CHUNK = 64
RW_IN = 1664
N_IN = 2944
FFN_BLK = 704
RMS_EPS = 1e-6
LNX_EPS = 64e-5
LRU_C = 8.0
ADAM_LR, ADAM_B1, ADAM_B2, ADAM_EPS, ADAM_WD, ADAM_STEP = 0.001, 0.9, 0.999, 1e-08, 0.01, 10

WEIGHTS = ['meta_tokens', 'norm_g', 'ffn1_w_in', 'ffn1_w_out', 'ffn2_w_in', 'ffn2_w_out', 'mix_w_in', 'mix_w_out',
           'lru_conv_w', 'lru_conv_b', 'lru_wa', 'lru_ba', 'lru_wx', 'lru_bx', 'lru_lambda', 'lru_norm_g',
           'sc_conv_w', 'sc_norm_g', 'rwkv_mu', 'rwkv_w0', 'rwkv_w2', 'rwkv_a0', 'rwkv_a2', 'rwkv_g2', 'rwkv_k_k',
           'rwkv_k_a', 'rwkv_r_k', 'rwkv_lnx_w', 'rwkv_lnx_b']
BIG = ['ffn1_w_in', 'ffn1_w_out', 'ffn2_w_in', 'ffn2_w_out', 'mix_w_in', 'mix_w_out']
SMALL_SHARDED = {'meta_tokens': 1, 'norm_g': 2, 'lru_conv_w': 2, 'sc_conv_w': 2, 'rwkv_w2': 2, 'rwkv_a2': 2, 'rwkv_g2': 2}
SMALL = [n for n in WEIGHTS if n not in BIG]


def _pick(n, cands):
    for c in cands:
        if n % c == 0:
            return c
    raise ValueError(f"no tile for {n}")


def _rowwise(fn, rows, params, row_outs, acc_outs, *, tm, name):
    nr, npar, nro, nao = len(rows), len(params), len(row_outs), len(acc_outs)
    n_rows = rows[0].shape[-2]
    assert n_rows % tm == 0, (name, n_rows, tm)

    def body(*refs):
        vals = [r[...] for r in refs[:nr + npar]]
        outs = fn(*vals)
        if not isinstance(outs, (tuple, list)):
            outs = (outs,)
        assert len(outs) == nro + nao, (name, len(outs))
        for o_ref, o in zip(refs[nr + npar:nr + npar + nro], outs[:nro]):
            o_ref[...] = o.astype(o_ref.dtype)
        step = pl.program_id(0)
        for a_ref, a in zip(refs[nr + npar + nro:], outs[nro:]):
            @pl.when(step == 0)
            def _(a_ref=a_ref, a=a):
                a_ref[...] = a.astype(F32)

            @pl.when(step > 0)
            def _(a_ref=a_ref, a=a):
                a_ref[...] += a.astype(F32)

    def row_spec(shape):
        if len(shape) == 2:
            return pl.BlockSpec((tm, shape[1]), lambda i: (i, 0))
        return pl.BlockSpec((shape[0], tm, shape[2]), lambda i: (0, i, 0))

    def full_spec(shape):
        nd = len(shape)
        return pl.BlockSpec(tuple(shape), lambda i, nd=nd: (0,) * nd)

    in_specs = [row_spec(r.shape) for r in rows] + [full_spec(p.shape) for p in params]
    out_shape = [jax.ShapeDtypeStruct((n_rows, w), dt) for (w, dt) in row_outs]
    out_shape += [jax.ShapeDtypeStruct(tuple(s), F32) for s in acc_outs]
    out_specs = [row_spec((n_rows, w)) for (w, _) in row_outs] + [full_spec(s) for s in acc_outs]
    res = pl.pallas_call(body, name=name, grid=(n_rows // tm,), in_specs=in_specs, out_specs=out_specs,
                         out_shape=out_shape)(*rows, *params)
    return tuple(res)


def _mm(a3, b3, *, trans_b, tm, tn, out_dtype, name):
    nj, m, kb = a3.shape
    n = b3.shape[1] if trans_b else b3.shape[2]
    dims = (((1,), (1,)), ((), ())) if trans_b else (((1,), (0,)), ((), ()))

    def body(a_ref, b_ref, o_ref, acc_ref):
        j = pl.program_id(2)

        @pl.when(j == 0)
        def _():
            acc_ref[...] = jnp.zeros_like(acc_ref)

        acc_ref[...] += lax.dot_general(a_ref[0], b_ref[0], dims, preferred_element_type=F32)

        @pl.when(j == nj - 1)
        def _():
            o_ref[...] = acc_ref[...].astype(o_ref.dtype)

    if trans_b:
        b_spec = pl.BlockSpec((1, tn, kb), lambda i, c, j: (j, c, 0))
    else:
        b_spec = pl.BlockSpec((1, kb, tn), lambda i, c, j: (j, 0, c))
    return pl.pallas_call(
        body, name=name, grid=(m // tm, n // tn, nj),
        in_specs=[pl.BlockSpec((1, tm, kb), lambda i, c, j: (j, i, 0)), b_spec],
        out_specs=pl.BlockSpec((tm, tn), lambda i, c, j: (i, c)),
        out_shape=jax.ShapeDtypeStruct((m, n), out_dtype),
        scratch_shapes=[pltpu.VMEM((tm, tn), F32)],
    )(a3, b3)


def _mm_tn(a3, b3, *, tk, name):
    ja, t, ka = a3.shape
    jb, _, n = b3.shape
    nj = max(ja, jb)

    def body(a_ref, b_ref, o_ref):
        o_ref[0] = lax.dot_general(a_ref[0], b_ref[0], (((0,), (0,)), ((), ())),
                                   preferred_element_type=F32).astype(o_ref.dtype)

    return pl.pallas_call(
        body, name=name, grid=(nj, ka // tk),
        in_specs=[pl.BlockSpec((1, t, tk), (lambda j, c: (j, 0, c)) if ja > 1 else (lambda j, c: (0, 0, c))),
                  pl.BlockSpec((1, t, n), (lambda j, c: (j, 0, 0)) if jb > 1 else (lambda j, c: (0, 0, 0)))],
        out_specs=pl.BlockSpec((1, tk, n), lambda j, c: (j, c, 0)),
        out_shape=jax.ShapeDtypeStruct((nj, ka, n), BF16),
    )(a3, b3)


def _ffn_in(a, w24, *, tm, name):
    t, d = a.shape
    nb, fb = w24.shape[1], w24.shape[3]

    def body(a_ref, w_ref, gu_ref, s_ref):
        x = a_ref[...]
        g = jnp.dot(x, w_ref[0, 0], preferred_element_type=F32)
        u = jnp.dot(x, w_ref[1, 0], preferred_element_type=F32)
        gu_ref[0, 0] = g.astype(BF16)
        gu_ref[1, 0] = u.astype(BF16)
        s_ref[0] = (g * jax.nn.sigmoid(g) * u).astype(BF16)

    return pl.pallas_call(
        body, name=name, grid=(nb, t // tm),
        in_specs=[pl.BlockSpec((tm, d), lambda j, i: (i, 0)), pl.BlockSpec((2, 1, d, fb), lambda j, i: (0, j, 0, 0))],
        out_specs=[pl.BlockSpec((2, 1, tm, fb), lambda j, i: (0, j, i, 0)), pl.BlockSpec((1, tm, fb), lambda j, i: (j, i, 0))],
        out_shape=[jax.ShapeDtypeStruct((2, nb, t, fb), BF16), jax.ShapeDtypeStruct((nb, t, fb), BF16)],
    )(a, w24)


def _ffn_dswiglu(df, wo4, gu, *, tm, name):
    t, d = df.shape
    nb, fb = wo4.shape[0], wo4.shape[1]

    def body(df_ref, wo_ref, gu_ref, dg_ref):
        ds = lax.dot_general(df_ref[...], wo_ref[0], (((1,), (1,)), ((), ())), preferred_element_type=F32)
        g = gu_ref[0, 0].astype(F32)
        u = gu_ref[1, 0].astype(F32)
        sig = jax.nn.sigmoid(g)
        dg_ref[0, 0] = (ds * u * sig * (1.0 + g * (1.0 - sig))).astype(BF16)
        dg_ref[1, 0] = (ds * g * sig).astype(BF16)

    return pl.pallas_call(
        body, name=name, grid=(nb, t // tm),
        in_specs=[pl.BlockSpec((tm, d), lambda j, i: (i, 0)), pl.BlockSpec((1, fb, d), lambda j, i: (j, 0, 0)),
                  pl.BlockSpec((2, 1, tm, fb), lambda j, i: (0, j, i, 0))],
        out_specs=pl.BlockSpec((2, 1, tm, fb), lambda j, i: (0, j, i, 0)),
        out_shape=jax.ShapeDtypeStruct((2, nb, t, fb), BF16),
    )(df, wo4, gu)


def _rms(x, g):
    return x * lax.rsqrt(jnp.mean(x * x, axis=-1, keepdims=True) + RMS_EPS) * g


def _rms_bwd(x, g, dy):
    rstd = lax.rsqrt(jnp.mean(x * x, axis=-1, keepdims=True) + RMS_EPS)
    xh = x * rstd
    dxh = dy * g
    dx = rstd * (dxh - xh * jnp.mean(dxh * xh, axis=-1, keepdims=True))
    return dx, jnp.sum(dy * xh, axis=0, keepdims=True)


def _seg_sum_impl(x, bd):
    parts = [jnp.dot(x[:, q * LANES:(q + 1) * LANES], bd, preferred_element_type=F32, precision=HIGHEST)
             for q in range(x.shape[1] // LANES)]
    return parts[0] if len(parts) == 1 else jnp.concatenate(parts, axis=1)


@jax.custom_vjp
def _seg_sum(x, bd):
    return _seg_sum_impl(x, bd)


def _seg_sum_fwd(x, bd):
    return _seg_sum_impl(x, bd), bd


def _seg_sum_bwd(bd, ct):
    return _seg_sum_impl(ct, bd), jnp.zeros_like(bd)


_seg_sum.defvjp(_seg_sum_fwd, _seg_sum_bwd)


def _group_rms(y, g, bd):
    return y * lax.rsqrt(_seg_sum(y * y, bd) * (1.0 / HEAD) + RMS_EPS) * g


def _expm1(x):
    return jnp.where(jnp.abs(x) < 1e-2, x * (1.0 + x * (0.5 + x * (1.0 / 6.0))), jnp.exp(x) - 1.0)


def _lru_pre(x0, x1, x2, x3, cw0, cw1, cw2, cw3, cb, wa, ba, wx, bx, lam):
    u = x3 * cw0 + x2 * cw1 + x1 * cw2 + x0 * cw3 + cb
    r = jax.nn.sigmoid(jnp.dot(u, wa, preferred_element_type=F32, precision=HIGHEST) + ba)
    i = jax.nn.sigmoid(jnp.dot(u, wx, preferred_element_type=F32, precision=HIGHEST) + bx)
    log_a = -LRU_C * r * jax.nn.softplus(-lam)
    return jnp.exp(log_a), jnp.sqrt(-_expm1(2.0 * log_a)) * (i * u)


def _lru_post(bd, gate, hs, ng):
    return _group_rms(jax.nn.gelu(gate) * hs, ng, bd)


def _sc_fwd(bd, b, c0, x0, c1, x1, c2, x2, w0, w1, w2, ng):
    return _group_rms(b * (w0 * (c2 * x2) + w1 * (c1 * x1) + w2 * (c0 * x0)), ng, bd)


def _rw_pre(bd, zr, zk, zv, zt, sr, sk, sv, st, mur, muk, muv, mut, w0, w2p, a0, a2p, g2p, k_k, k_a):
    r, k, v, tail = zr + (sr - zr) * mur, zk + (sk - zk) * muk, zv + (sv - zv) * muv, zt + (st - zt) * mut
    lane = lax.broadcasted_iota(jnp.int32, tail.shape, 1)
    act = jnp.where(lane < 32, jnp.tanh(tail), jnp.where(lane < 64, tail, jax.nn.sigmoid(tail)))
    dot = functools.partial(jnp.dot, preferred_element_type=F32, precision=HIGHEST)
    w_log = -jax.nn.softplus(-(w0 + dot(act, w2p))) - 0.5
    w = jnp.exp(-jnp.exp(w_log))
    a = jax.nn.sigmoid(a0 + dot(act, a2p))
    g = dot(act, g2p)
    kk = k * k_k
    k2 = k * (1.0 + (a - 1.0) * k_a)
    kkn = kk * lax.rsqrt(jnp.maximum(_seg_sum(kk * kk, bd), 1e-24))
    return r, w, k2, -kkn, kkn * a, v, g


def _rw_post(bd, y, r, k2, v, g, lnw, lnb, r_k):
    mean = _seg_sum(y, bd) * (1.0 / HEAD)
    yc = y - mean
    var = _seg_sum(yc * yc, bd) * (1.0 / HEAD)
    yn = yc * lax.rsqrt(var + LNX_EPS) * lnw + lnb
    return (yn + _seg_sum(r * k2 * r_k, bd) * v) * g


def _vjp_rows(fwd, n_static, n_in, n_ct):
    def fn(*args):
        static, prim, cts = args[:n_static], args[n_static:n_static + n_in], args[n_static + n_in:]
        assert len(cts) == n_ct
        _, vjp = jax.vjp(functools.partial(fwd, *static), *prim)
        return vjp(cts[0] if n_ct == 1 else tuple(cts))
    return fn


def _exchange_copies(x_refs, o_refs, sems, gather):
    send_sems, recv_sems, local_sems = sems
    mx, my, mc = lax.axis_index("x"), lax.axis_index("y"), lax.axis_index("c")
    me = 4 * mx + 2 * my + mc
    local, sends, recvs = [], [], []
    for k in range(len(x_refs)):
        local.append(pltpu.make_async_copy(x_refs[k] if gather else x_refs[k].at[me], o_refs[k].at[me], local_sems.at[k]))
    for d in range(1, N_DEV):
        px, py, pc = mx ^ ((d >> 2) & 1), my ^ ((d >> 1) & 1), mc ^ (d & 1)
        peer = 4 * px + 2 * py + pc
        for k in range(len(x_refs)):
            src = x_refs[k] if gather else x_refs[k].at[peer]
            common = dict(src_ref=src, send_sem=send_sems.at[k, d - 1], recv_sem=recv_sems.at[k, d - 1],
                          device_id=(px, py, pc), device_id_type=pl.DeviceIdType.MESH)
            sends.append(pltpu.make_async_remote_copy(dst_ref=o_refs[k].at[me], **common))
            recvs.append(pltpu.make_async_remote_copy(dst_ref=o_refs[k].at[peer], **common))
    return local, sends, recvs


def _gather2_copies(x_refs, o_refs, sems):
    send_sems, recv_sems, local_sems = sems
    mx, my, mc = lax.axis_index("x"), lax.axis_index("y"), lax.axis_index("c")
    sibling = (mx, my, 1 - mc)
    chips = [(1 - mx, my), (mx, 1 - my), (1 - mx, 1 - my)]

    def slot(px, py, pc):
        return 4 * px + 2 * py + pc

    out = dict(local=[], first=[], first_recv=[], ici_recv=[], passed=[], passed_recv=[])
    for k in range(len(x_refs)):
        def copy(sem, src, dst_slot, to, k=k):
            return pltpu.make_async_remote_copy(src_ref=src, dst_ref=o_refs[k].at[dst_slot], send_sem=send_sems.at[k, sem],
                                                recv_sem=recv_sems.at[k, sem], device_id=to, device_id_type=pl.DeviceIdType.MESH)
        me = slot(mx, my, mc)
        out['local'].append(pltpu.make_async_copy(x_refs[k], o_refs[k].at[me], local_sems.at[k]))
        out['first'].append(copy(0, x_refs[k], me, sibling))
        out['first_recv'].append(copy(0, x_refs[k], slot(mx, my, 1 - mc), sibling))
        for j, (px, py) in enumerate(chips):
            out['first'].append(copy(1 + j, x_refs[k], me, (px, py, mc)))
            out['ici_recv'].append(copy(1 + j, x_refs[k], slot(px, py, mc), (px, py, mc)))
            out['passed'].append(copy(4 + j, o_refs[k].at[slot(px, py, mc)], slot(px, py, mc), sibling))
            out['passed_recv'].append(copy(4 + j, x_refs[k], slot(px, py, 1 - mc), sibling))
    return out


def _exchange_start(x_refs, o_refs, sems, gather):
    if gather:
        cps = _gather2_copies(x_refs, o_refs, sems)
        for cp in cps['local'] + cps['first']:
            cp.start()
        return
    local, sends, _ = _exchange_copies(x_refs, o_refs, sems, gather)
    for cp in local + sends:
        cp.start()


def _exchange_wait(x_refs, o_refs, sems, gather):
    if gather:
        cps = _gather2_copies(x_refs, o_refs, sems)
        for arrived, onward in zip(cps['ici_recv'], cps['passed']):
            arrived.wait_recv()
            onward.start()
        for cp in cps['first'] + cps['passed']:
            cp.wait_send()
        for cp in cps['first_recv'] + cps['passed_recv']:
            cp.wait_recv()
        for cp in cps['local']:
            cp.wait()
        return
    local, sends, recvs = _exchange_copies(x_refs, o_refs, sems, gather)
    for cp in sends:
        cp.wait_send()
    for cp in recvs:
        cp.wait_recv()
    for cp in local:
        cp.wait()


def _exchange_out_shape(xs, gather):
    return [jax.ShapeDtypeStruct(((N_DEV,) + x.shape) if gather else x.shape, x.dtype) for x in xs]


def _exchange_sems(n):
    return [pltpu.SemaphoreType.DMA((n, N_DEV - 1)), pltpu.SemaphoreType.DMA((n, N_DEV - 1)), pltpu.SemaphoreType.DMA((n,))]


SUBLANES = 8


def _store_row(ref, i, cols, row):
    base = pl.multiple_of((i // SUBLANES) * SUBLANES, SUBLANES)
    sub = lax.broadcasted_iota(jnp.int32, (SUBLANES, row.shape[1]), 0)
    ref[pl.ds(base, SUBLANES), cols] = jnp.where(sub == i % SUBLANES, row, ref[pl.ds(base, SUBLANES), cols])


def _tile_scan(a, b, reverse):
    sub = lax.broadcasted_iota(jnp.int32, a.shape, 0)
    for sh in (1, 2, 4):
        if reverse:
            live = sub < SUBLANES - sh
            a_s, b_s = pltpu.roll(a, SUBLANES - sh, 0), pltpu.roll(b, SUBLANES - sh, 0)
        else:
            live = sub >= sh
            a_s, b_s = pltpu.roll(a, sh, 0), pltpu.roll(b, sh, 0)
        b = jnp.where(live, a * b_s, 0.0) + b
        a = jnp.where(live, a * a_s, a)
    return a, b


def _lru_scan(a, b, name):
    t, w = a.shape

    def body(a_ref, b_ref, h_ref):
        def tile(j, h):
            rows = pl.ds(pl.multiple_of(j * SUBLANES, SUBLANES), SUBLANES)
            ca, cb = _tile_scan(a_ref[rows, :], b_ref[rows, :], False)
            out = ca * h + cb
            h_ref[rows, :] = out
            return out[SUBLANES - 1:SUBLANES]
        lax.fori_loop(0, t // SUBLANES, tile, jnp.zeros((1, w), F32))

    return pl.pallas_call(body, name=name, out_shape=jax.ShapeDtypeStruct((t, w), F32))(a, b)


def _lru_scan_bwd(a_next, h_prev, dhs, name):
    t, w = dhs.shape

    def body(a_ref, h_ref, dh_ref, da_ref, db_ref):
        def tile(n, lam):
            rows = pl.ds(pl.multiple_of((t // SUBLANES - 1 - n) * SUBLANES, SUBLANES), SUBLANES)
            ca, cb = _tile_scan(a_ref[rows, :], dh_ref[rows, :], True)
            out = ca * lam + cb
            db_ref[rows, :] = out
            da_ref[rows, :] = out * h_ref[rows, :]
            return out[0:1]
        lax.fori_loop(0, t // SUBLANES, tile, jnp.zeros((1, w), F32))

    return pl.pallas_call(body, name=name, out_shape=[jax.ShapeDtypeStruct((t, w), F32)] * 2)(a_next, h_prev, dhs)


N_PAIR = RW_W // LANES
UNROLL = 2


def _bcast_cols(v, bd2, name):
    t = v.shape[0]

    def body(v_ref, bd_ref, o_ref):
        bdv = bd_ref[...]
        sub = lax.broadcasted_iota(jnp.int32, (HEAD, LANES), 0)
        own = lax.broadcasted_iota(jnp.int32, (HEAD, LANES), 1) % HEAD == sub
        for i in range(CHUNK):
            row = v_ref[i:i + 1, :]
            sums = _group_sums([jnp.where(own, row[:, p * LANES:(p + 1) * LANES], 0.0) for p in range(N_PAIR)], bdv)
            for p in range(N_PAIR):
                o_ref[i, p] = sums[p * HEAD:(p + 1) * HEAD]

    return pl.pallas_call(
        body, name=name, grid=(t // CHUNK,),
        in_specs=[pl.BlockSpec((CHUNK, RW_W), lambda i: (i, 0)), pl.BlockSpec((2 * LANES, LANES), lambda i: (0, 0))],
        out_specs=pl.BlockSpec((CHUNK, N_PAIR, HEAD, LANES), lambda i: (i, 0, 0, 0)),
        out_shape=jax.ShapeDtypeStruct((t, N_PAIR, HEAD, LANES), F32),
    )(v, bd2)


def _cols_to_rows(cols_ref, rows_ref):
    lane = lax.broadcasted_iota(jnp.int32, (CHUNK, LANES), 1)
    for p in range(N_PAIR):
        tile = cols_ref[p]
        sq = jnp.concatenate([tile, jnp.zeros_like(tile)], axis=0).T
        rows_ref[:, p * LANES:(p + 1) * LANES] = jnp.where(lane < HEAD, sq[0:CHUNK], pltpu.roll(sq[CHUNK:2 * CHUNK], HEAD, 1))


def _stacked_bf16(bd):
    return jnp.concatenate([bd, bd], axis=0).astype(BF16)


def _group_sums(prods, bd2):
    x = jnp.concatenate(prods, axis=0)
    hi = x.astype(BF16)
    lo = (x - hi.astype(F32)).astype(BF16)
    return jnp.dot(jnp.concatenate([hi, lo], axis=1), bd2, preferred_element_type=F32)


def _rw_scan(r, r_prev, w, k, c, c_next, b, vb, bd2, name, ride=None):
    t = w.shape[0]
    nch = t // CHUNK
    ride_xs, ride_gather = ride if ride is not None else ([], False)
    n_ride = len(ride_xs)

    def body(*refs):
        r_ref, rp_ref, w_ref, k_ref, c_ref, cn_ref, b_ref, vb_ref, bd_ref = refs[:9]
        x_refs = refs[9:9 + n_ride]
        spre_ref, y_ref = refs[9 + n_ride:11 + n_ride]
        o_refs = refs[11 + n_ride:11 + 2 * n_ride]
        s_ref, wc_ref, wr_ref, bc_ref, kc_ref, br_ref, kr_ref, yt_ref = refs[11 + 2 * n_ride:19 + 2 * n_ride]
        sems = refs[19 + 2 * n_ride:]

        @pl.when(pl.program_id(0) == 0)
        def _():
            s_ref[...] = jnp.zeros_like(s_ref)
            if n_ride:
                _exchange_start(x_refs, o_refs, sems, ride_gather)

        yt_ref[...] = jnp.zeros_like(yt_ref)
        bdv = bd_ref[...]
        lane = lax.broadcasted_iota(jnp.int32, (HEAD, LANES), 1) % CHUNK

        wv, cn, rv, bv, kv = w_ref[...], cn_ref[...], r_ref[...], b_ref[...], k_ref[...]
        wc_ref[...] = wv * cn
        wr_ref[...] = wv * rv
        for ref, x in ((bc_ref, bv * cn), (kc_ref, kv * cn), (br_ref, bv * rv), (kr_ref, kv * rv)):
            sums = _group_sums([x[:, q * LANES:(q + 1) * LANES] for q in range(N_PAIR)], bdv)
            for q in range(N_PAIR):
                ref[:, q * LANES:(q + 1) * LANES] = sums[q * CHUNK:(q + 1) * CHUNK]

        def cut(ref, i):
            x = ref[pl.ds(i, 1), :]
            return [x[:, p * LANES:(p + 1) * LANES] for p in range(N_PAIR)]

        def two_steps(j, st):
            i0 = 2 * j
            i1 = i0 + 1
            c0, wc0, rp0, wr0 = cut(c_ref, i0), cut(wc_ref, i0), cut(rp_ref, i0), cut(wr_ref, i0)
            w0, b0, k0, w1, b1, k1 = cut(w_ref, i0), cut(b_ref, i0), cut(k_ref, i0), cut(w_ref, i1), cut(b_ref, i1), cut(k_ref, i1)
            bc0, kc0, br0, kr0 = cut(bc_ref, i0), cut(kc_ref, i0), cut(br_ref, i0), cut(kr_ref, i0)
            pairs = range(N_PAIR)
            red = _group_sums([st[p] * c0[p] for p in pairs] + [st[p] * wc0[p] for p in pairs], bdv)
            out = _group_sums([st[p] * rp0[p] for p in pairs] + [st[p] * wr0[p] for p in pairs], bdv)
            new = []
            for p in pairs:
                v0, v1 = vb_ref[i0, p], vb_ref[i1, p]
                sa0 = red[p * HEAD:(p + 1) * HEAD]
                sa1 = red[(N_PAIR + p) * HEAD:(N_PAIR + p + 1) * HEAD] + sa0 * bc0[p] + v0 * kc0[p]
                y_before = out[p * HEAD:(p + 1) * HEAD]
                y0 = out[(N_PAIR + p) * HEAD:(N_PAIR + p + 1) * HEAD] + sa0 * br0[p] + v0 * kr0[p]
                spre_ref[i0, p] = st[p]
                s1 = st[p] * w0[p] + sa0 * b0[p] + v0 * k0[p]
                spre_ref[i1, p] = s1
                new.append(s1 * w1[p] + sa1 * b1[p] + v1 * k1[p])
                yt_ref[p] = jnp.where(lane == i0, y_before, jnp.where(lane == i1, y0, yt_ref[p]))
            return tuple(new)

        st = lax.fori_loop(0, CHUNK // 2, two_steps, tuple(s_ref[p] for p in range(N_PAIR)))
        for p in range(N_PAIR):
            s_ref[p] = st[p]
        _cols_to_rows(yt_ref, y_ref)

        if n_ride:
            @pl.when(pl.program_id(0) == nch - 1)
            def _():
                _exchange_wait(x_refs, o_refs, sems, ride_gather)

    row = pl.BlockSpec((CHUNK, RW_W), lambda i: (i, 0))
    big = pl.BlockSpec((CHUNK, N_PAIR, HEAD, LANES), lambda i: (i, 0, 0, 0))
    any_spec = pl.BlockSpec(memory_space=pl.ANY)
    return pl.pallas_call(
        body, name=name, grid=(nch,),
        in_specs=[row] * 7 + [big, pl.BlockSpec((2 * LANES, LANES), lambda i: (0, 0))] + [any_spec] * n_ride,
        out_specs=[big, row] + [any_spec] * n_ride,
        out_shape=[jax.ShapeDtypeStruct((t, N_PAIR, HEAD, LANES), F32), jax.ShapeDtypeStruct((t, RW_W), F32)]
        + _exchange_out_shape(ride_xs, ride_gather),
        scratch_shapes=[pltpu.VMEM((N_PAIR, HEAD, LANES), F32)] + [pltpu.VMEM((CHUNK, RW_W), F32)] * 6
        + [pltpu.VMEM((N_PAIR, HEAD, LANES), F32)] + (_exchange_sems(n_ride) if n_ride else []),
    )(r, r_prev, w, k, c, c_next, b, vb, bd2, *ride_xs)


def _rw_scan_bwd(r, w, k, c, b, vb, dyb, spre, bd, name, ride=None):
    t = r.shape[0]
    nch = t // CHUNK
    ride_xs, ride_gather = ride if ride is not None else ([], False)
    n_ride = len(ride_xs)
    n_pre = 8

    def body(*refs):
        r_ref, w_ref, k_ref, c_ref, b_ref, vb_ref, dyb_ref, spre_ref, bd_ref = refs[:9]
        x_refs = refs[9:9 + n_ride]
        dr_ref, dw_ref, dk_ref, dc_ref, db_ref, dv_ref = refs[9 + n_ride:15 + n_ride]
        o_refs = refs[15 + n_ride:15 + 2 * n_ride]
        g_ref, snext_ref, dvt_ref = refs[15 + 2 * n_ride:18 + 2 * n_ride]
        pre = refs[18 + 2 * n_ride:18 + 2 * n_ride + n_pre]
        sems = refs[18 + 2 * n_ride + n_pre:]
        wb_ref, wk_ref, rb_ref, rk_ref, rwb_ref, cb_ref, rwk_ref, ck_ref = pre[:8]

        @pl.when(pl.program_id(0) == 0)
        def _():
            g_ref[...] = jnp.zeros_like(g_ref)
            snext_ref[...] = jnp.zeros_like(snext_ref)
            if n_ride:
                _exchange_start(x_refs, o_refs, sems, ride_gather)

        for ref in (dr_ref, dw_ref, dk_ref, dc_ref, db_ref, dvt_ref):
            ref[...] = jnp.zeros_like(ref)
        bdv = bd_ref[...]
        bd1 = bdv[0:LANES]
        lane = lax.broadcasted_iota(jnp.int32, (HEAD, LANES), 1) % CHUNK

        rv, wv, kv, cv, bv = r_ref[...], w_ref[...], k_ref[...], c_ref[...], b_ref[...]
        b_b, k_b = pltpu.roll(bv, 1, 0), pltpu.roll(kv, 1, 0)
        wb_ref[...] = wv * b_b
        wk_ref[...] = wv * k_b
        rw = rv * wv
        for ref, x in ((rb_ref, rv * bv), (rk_ref, rv * kv), (rwb_ref, rw * b_b), (cb_ref, cv * b_b), (rwk_ref, rw * k_b),
                       (ck_ref, cv * k_b)):
            sums = _group_sums([x[:, q * LANES:(q + 1) * LANES] for q in range(N_PAIR)], bdv)
            for q in range(N_PAIR):
                ref[:, q * LANES:(q + 1) * LANES] = sums[q * CHUNK:(q + 1) * CHUNK]

        def sum0(x):
            return jnp.sum(x, axis=0, keepdims=True)

        def cut(ref, i):
            x = ref[pl.ds(i, 1), :]
            return [x[:, p * LANES:(p + 1) * LANES] for p in range(N_PAIR)]

        def sums_bf16(prods):
            return jnp.dot(jnp.concatenate(prods, axis=0).astype(BF16), bd1, preferred_element_type=F32)

        def two_steps(n, gs):
            ia = CHUNK - 1 - 2 * n
            ib = ia - 1
            r_a, w_a, k_a, c_a, b_a = [cut(ref, ia) for ref in (r_ref, w_ref, k_ref, c_ref, b_ref)]
            r_b, w_b, c_b = [cut(ref, ib) for ref in (r_ref, w_ref, c_ref)]
            wb, wk, rb_a, rk_a, rwb, cb, rwk, ck = [cut(ref, ia) for ref in pre[:8]]
            rb_b, rk_b = cut(rb_ref, ib), cut(rk_ref, ib)
            pairs = range(N_PAIR)
            sp_a = [spre_ref[ia, p] for p in pairs]
            sp_b = [spre_ref[ib, p] for p in pairs]
            dy_a = [dyb_ref[ia, p] for p in pairs]
            dy_b = [dyb_ref[ib, p] for p in pairs]
            chain = _group_sums([gs[p] * b_a[p] for p in pairs] + [gs[p] * wb[p] for p in pairs], bdv)
            off = sums_bf16([gs[p] * k_a[p] for p in pairs] + [gs[p] * wk[p] for p in pairs]
                            + [sp_a[p] * c_a[p] for p in pairs] + [sp_b[p] * c_b[p] for p in pairs])
            new = []
            for p in pairs:
                def part(x, q, p=p):
                    return x[(q * N_PAIR + p) * HEAD:(q * N_PAIR + p + 1) * HEAD]
                dsa_a = part(chain, 0) + dy_a[p] * rb_a[p]
                dv_a = part(off, 0) + dy_a[p] * rk_a[p]
                dsa_b = part(chain, 1) + dy_a[p] * rwb[p] + dsa_a * cb[p] + dy_b[p] * rb_b[p]
                dv_b = part(off, 1) + dy_a[p] * rwk[p] + dsa_a * ck[p] + dy_b[p] * rk_b[p]
                sa_a, sa_b = part(off, 2), part(off, 3)
                g_a = gs[p] + dy_a[p] * r_a[p]
                g_mid = g_a * w_a[p] + dsa_a * c_a[p]
                g_b = g_mid + dy_b[p] * r_b[p]
                new.append(g_b * w_b[p] + dsa_b * c_b[p])
                cols = pl.ds(p * LANES, LANES)
                for i, dy, s_post, s_pre, g, sa, dsa in ((ia, dy_a[p], snext_ref[p], sp_a[p], g_a, sa_a, dsa_a),
                                                         (ib, dy_b[p], sp_a[p], sp_b[p], g_b, sa_b, dsa_b)):
                    _store_row(dr_ref, i, cols, sum0(s_post * dy))
                    _store_row(dw_ref, i, cols, sum0(g * s_pre))
                    _store_row(db_ref, i, cols, sum0(g * sa))
                    _store_row(dk_ref, i, cols, sum0(g * vb_ref[i, p]))
                    _store_row(dc_ref, i, cols, sum0(s_pre * dsa))
                dvt_ref[p] = jnp.where(lane == ia, dv_a, jnp.where(lane == ib, dv_b, dvt_ref[p]))
                snext_ref[p] = sp_b[p]
            return tuple(new)

        gs = lax.fori_loop(0, CHUNK // 2, two_steps, tuple(g_ref[p] for p in range(N_PAIR)))
        for p in range(N_PAIR):
            g_ref[p] = gs[p]
        _cols_to_rows(dvt_ref, dv_ref)

        if n_ride:
            @pl.when(pl.program_id(0) == nch - 1)
            def _():
                _exchange_wait(x_refs, o_refs, sems, ride_gather)

    row = pl.BlockSpec((CHUNK, RW_W), lambda i: (nch - 1 - i, 0))
    big = pl.BlockSpec((CHUNK, N_PAIR, HEAD, LANES), lambda i: (nch - 1 - i, 0, 0, 0))
    any_spec = pl.BlockSpec(memory_space=pl.ANY)
    return pl.pallas_call(
        body, name=name, grid=(nch,),
        in_specs=[row] * 5 + [big, big, big, pl.BlockSpec((2 * LANES, LANES), lambda i: (0, 0))] + [any_spec] * n_ride,
        out_specs=[row] * 6 + [any_spec] * n_ride,
        out_shape=[jax.ShapeDtypeStruct((t, RW_W), F32)] * 6 + _exchange_out_shape(ride_xs, ride_gather),
        scratch_shapes=[pltpu.VMEM((N_PAIR, HEAD, LANES), F32)] * 3
        + [pltpu.VMEM((CHUNK, RW_W), F32)] * n_pre + (_exchange_sems(n_ride) if n_ride else []),
    )(r, w, k, c, b, vb, dyb, spre, bd, *ride_xs)


def _exchange(xs, *, gather, name):
    n = len(xs)

    def body(*refs):
        _exchange_start(refs[:n], refs[n:2 * n], refs[2 * n:], gather)
        _exchange_wait(refs[:n], refs[n:2 * n], refs[2 * n:], gather)

    any_spec = pl.BlockSpec(memory_space=pl.ANY)
    return pl.pallas_call(
        body, name=name, in_specs=[any_spec] * n, out_specs=[any_spec] * n, out_shape=_exchange_out_shape(xs, gather),
        scratch_shapes=_exchange_sems(n),
    )(*xs)


def _adamw_rows(g, w, m, v):
    m = ADAM_B1 * m + (1.0 - ADAM_B1) * g
    v = ADAM_B2 * v + (1.0 - ADAM_B2) * (g * g)
    m_hat = m / (1.0 - ADAM_B1 ** ADAM_STEP)
    v_hat = v / (1.0 - ADAM_B2 ** ADAM_STEP)
    return -ADAM_LR * (m_hat / (jnp.sqrt(v_hat) + ADAM_EPS) + ADAM_WD * w), m, v


def _sum_slots(parts):
    g = parts[0].astype(F32)
    for q in range(1, N_DEV):
        g = g + parts[q].astype(F32)
    return g


def _reduce_adamw(parts, w, m, v, name):
    rows, cols = w.shape

    def fn(parts, w, m, v):
        g = _sum_slots(parts)
        return (g,) + _adamw_rows(g, w, m, v)

    return _rowwise(fn, [parts, w, m, v], [], [(cols, F32)] * 4, [], tm=_pick(rows, (256, 128, 64, 32, 16, 8)), name=name)


def _shift(x, n):
    return jnp.pad(x, ((n, 0), (0, 0)))[:-n]


def _unshift(x, n):
    return jnp.pad(x, ((0, n), (0, 0)))[n:]


def _add_n(xs, *, tm, name):
    def fn(*vals):
        s = vals[0]
        for x in vals[1:]:
            s = s + x
        return s
    return _rowwise(fn, xs, [], [(xs[0].shape[1], F32)], [], tm=tm, name=name)[0]


def _norm_fwd(h, g, *, tm, name):
    return _rowwise(lambda x, gg: _rms(x, gg), [h], [g], [(h.shape[1], BF16)], [], tm=tm, name=name)[0]


def _res_norm_fwd(h, f, g, scale, *, tm, name):
    return _rowwise(lambda hh, ff, gg: hh + scale * _rms(ff, gg), [h, f], [g], [(h.shape[1], F32)], [], tm=tm, name=name)[0]


def _norm_bwd(x, g, dy, scale, res, out_dtype, *, tm, name):
    if res is None:
        def fn(xx, dd, gg):
            dx, dg = _rms_bwd(xx, gg, dd * scale)
            return dx, dg
        rows = [x, dy]
    else:
        def fn(xx, dd, rr, gg):
            dx, dg = _rms_bwd(xx, gg, dd * scale)
            return dx + rr, dg
        rows = [x, dy, res]
    return _rowwise(fn, rows, [g], [(x.shape[1], out_dtype)], [(1, x.shape[1])], tm=tm, name=name)


def _ffn_fwd(h, g_pre, g_post, w24, wo4, tiles, tag):
    tb, ts = tiles
    a = _norm_fwd(h, g_pre, tm=ts, name=f"{tag}_norm")
    gu, s4 = _ffn_in(a, w24, tm=tb, name=f"{tag}_in")
    f = _mm(s4, wo4, trans_b=False, tm=tb, tn=D_MODEL, out_dtype=F32, name=f"{tag}_out")
    h_new = _res_norm_fwd(h, f, g_post, 0.5, tm=ts, name=f"{tag}_res")
    return h_new, (h, a, gu, s4, f)


def _ffn_bwd(dh_new, res, g_pre, g_post, w24, wo4, tiles, tag):
    tb, ts = tiles
    h, a, gu, s4, f = res
    t = h.shape[0]
    df, dg_post = _norm_bwd(f, g_post, dh_new, 0.5, None, BF16, tm=ts, name=f"{tag}_dres")
    dgu = _ffn_dswiglu(df, wo4, gu, tm=tb, name=f"{tag}_dswiglu")
    d_wo = _mm_tn(s4, df[None], tk=FFN_BLK, name=f"{tag}_dwout")
    dgu8 = dgu.reshape(2 * w24.shape[1], t, FFN_BLK)
    w8 = w24.reshape(2 * w24.shape[1], D_MODEL, FFN_BLK)
    da = _mm(dgu8, w8, trans_b=True, tm=tb, tn=D_MODEL, out_dtype=F32, name=f"{tag}_da")
    d_win = _mm_tn(a[None], dgu8, tk=D_MODEL, name=f"{tag}_dwin")
    dh, dg_pre = _norm_bwd(h, g_pre, da, 1.0, dh_new, F32, tm=ts, name=f"{tag}_dnorm")
    return dh, dg_pre, dg_post, d_win, d_wo.reshape(N_DEV, -1, D_MODEL)


def _blockdiag(w4):
    n, b, _ = w4.shape
    eye = jnp.eye(n, dtype=w4.dtype)
    return (eye[:, None, :, None] * w4[:, :, None, :]).reshape(n * b, n * b)


def _blockdiag_grad(d):
    n = d.shape[0] // HEAD
    x = d.reshape(n, HEAD, n, HEAD)
    return jnp.stack([x[i, :, i, :] for i in range(n)])


def _row(v):
    return v.reshape(1, -1)


def _mixer_fwd(h, g_pre, g_post, wi, wo, P, bd, tiles, tag, ride=None):
    tb, ts = tiles
    a = _norm_fwd(h, g_pre, tm=ts, name=f"{tag}_norm")
    p = _mm(a[None], wi[None], trans_b=False, tm=tb, tn=N_IN, out_dtype=F32, name=f"{tag}_in")
    lx, lg = p[:, 0:256], p[:, 256:512]
    sb, scc, sx = p[:, 512:768], p[:, 768:1024], p[:, 1024:1280]
    z = p[:, 1280:]
    lxs = [lx, _shift(lx, 1), _shift(lx, 2), _shift(lx, 3)]
    cw = [_row(P['lru_conv_w'][kk]) for kk in range(4)]
    lru_par = cw + [_row(P['lru_conv_b']), _blockdiag(P['lru_wa']), _row(P['lru_ba']), _blockdiag(P['lru_wx']),
                    _row(P['lru_bx']), _row(P['lru_lambda'])]
    la, lb = _rowwise(_lru_pre, lxs, lru_par, [(LRU_W, F32)] * 2, [], tm=ts, name=f"{tag}_lru_pre")
    hs = _lru_scan(la, lb, name=f"{tag}_lru_scan")
    y_lru = _rowwise(lambda gg, hh, ng, b_: _lru_post(b_, gg, hh, ng), [lg, hs], [_row(P['lru_norm_g']), bd],
                     [(LRU_W, F32)], [], tm=ts, name=f"{tag}_lru_post")[0]
    sc_rows = [sb, scc, sx, _shift(scc, 1), _shift(sx, 1), _shift(scc, 2), _shift(sx, 2)]
    sc_par = [_row(P['sc_conv_w'][kk]) for kk in range(3)] + [_row(P['sc_norm_g'])]
    y_sc = _rowwise(lambda *v: _sc_fwd(v[-1], *v[:-1]), sc_rows, sc_par + [bd], [(SC_W, F32)], [], tm=ts,
                    name=f"{tag}_sc")[0]
    cuts = (0, RW_W, 2 * RW_W, 3 * RW_W, RW_IN)
    zs = [z[:, cuts[q]:cuts[q + 1]] for q in range(4)]
    z_rows = zs + [_shift(q, 1) for q in zs]
    pad = lambda m, lo: jnp.pad(m, ((lo, LANES - lo - m.shape[0]), (0, 0)))
    rw_par = [_row(P['rwkv_mu'][cuts[q]:cuts[q + 1]]) for q in range(4)]
    rw_par += [_row(P['rwkv_w0']), pad(P['rwkv_w2'], 0), _row(P['rwkv_a0']), pad(P['rwkv_a2'], 32),
               pad(P['rwkv_g2'], 64), _row(P['rwkv_k_k']), _row(P['rwkv_k_a'])]
    r, w, k2, c, b, v, g = _rowwise(lambda *vv: _rw_pre(vv[-1], *vv[:-1]), z_rows, rw_par + [bd], [(RW_W, F32)] * 7, [],
                                    tm=ts, name=f"{tag}_rw_pre")
    vb = _bcast_cols(v, _stacked_bf16(bd), name=f"{tag}_rw_vcols")
    spre, yt, *rode = _rw_scan(r, _shift(r, 1), w, k2, c, _unshift(c, 1), b, vb, _stacked_bf16(bd), name=f"{tag}_rw_scan",
                               ride=ride)
    y = _unshift(yt, 1)
    post_par =[_row(P['rwkv_lnx_w']), _row(P['rwkv_lnx_b']), _row(P['rwkv_r_k'])]
    y_rw = _rowwise(lambda *vv: _rw_post(vv[-1], *vv[:-1]), [y, r, k2, v, g], post_par + [bd], [(RW_W, F32)], [], tm=ts,
                    name=f"{tag}_rw_post")[0]
    ycat = jnp.concatenate([y_lru, y_sc, y_rw], axis=1).astype(BF16)
    m = _mm(ycat[None], wo[None], trans_b=False, tm=tb, tn=D_MODEL, out_dtype=F32, name=f"{tag}_out")
    h_new = _res_norm_fwd(h, m, g_post, 1.0, tm=ts, name=f"{tag}_res")
    res = dict(h=h, a=a, m=m, ycat=ycat, lxs=lxs, lru_par=lru_par, lg=lg, la=la, hs=hs, sc_rows=sc_rows, sc_par=sc_par,
               z_rows=z_rows, rw_par=rw_par, r=r, w=w, k2=k2, c=c, b=b, v=v, g=g, vb=vb, spre=spre, y=y, post_par=post_par)
    return h_new, res, rode


def _mixer_bwd(dh_new, R, g_pre, g_post, wi, wo, P, bd, tiles, tag, ride=None):
    tb, ts = tiles
    dm, dg_post = _norm_bwd(R['m'], g_post, dh_new, 1.0, None, BF16, tm=ts, name=f"{tag}_dres")
    dycat = _mm(dm[None], wo[None], trans_b=True, tm=tb, tn=D_MODEL, out_dtype=F32, name=f"{tag}_dycat")
    d_wo = _mm_tn(R['ycat'][None], dm[None], tk=D_MODEL // 2, name=f"{tag}_dwout")[0]
    dy_lru, dy_sc, dy_rw = dycat[:, 0:256], dycat[:, 256:512], dycat[:, 512:]
    G = {}
    d_lg, d_hs, G['lru_norm_g'] = _rowwise(
        lambda gg, hh, ct, ng, b_: _vjp_rows(_lru_post, 1, 3, 1)(b_, gg, hh, ng, ct),
        [R['lg'], R['hs'], dy_lru], [_row(P['lru_norm_g']), bd], [(LRU_W, F32)] * 2, [(1, LRU_W)], tm=ts,
        name=f"{tag}_lru_dpost")
    d_la, d_lb = _lru_scan_bwd(_unshift(R['la'], 1), _shift(R['hs'], 1), d_hs, name=f"{tag}_lru_dscan")

    def lru_pre_bwd(x0, x1, x2, x3, ca, cb_, *par):
        return _vjp_rows(_lru_pre, 0, 14, 2)(x0, x1, x2, x3, *par, ca, cb_)

    par_shapes = [tuple(q.shape) for q in R['lru_par']]
    outs = _rowwise(lru_pre_bwd, R['lxs'] + [d_la, d_lb], R['lru_par'], [(LRU_W, F32)] * 4, par_shapes, tm=ts,
                    name=f"{tag}_lru_dpre")
    dxs, dpar = outs[:4], outs[4:]
    d_lx = _add_n([dxs[0], _unshift(dxs[1], 1), _unshift(dxs[2], 2), _unshift(dxs[3], 3)], tm=ts, name=f"{tag}_lru_dx")
    G['lru_conv_w'] = jnp.concatenate(dpar[0:4], axis=0)
    G['lru_conv_b'] = dpar[4][0]
    G['lru_wa'] = _blockdiag_grad(dpar[5])
    G['lru_ba'] = dpar[6][0]
    G['lru_wx'] = _blockdiag_grad(dpar[7])
    G['lru_bx'] = dpar[8][0]
    G['lru_lambda'] = dpar[9][0]
    G['lru_norm_g'] = G['lru_norm_g'][0]

    def sc_bwd(*vv):
        rows7, ct, par4, b_ = vv[:7], vv[7], vv[8:12], vv[12]
        return _vjp_rows(_sc_fwd, 1, 11, 1)(b_, *rows7, *par4, ct)

    outs = _rowwise(sc_bwd, R['sc_rows'] + [dy_sc], R['sc_par'] + [bd], [(SC_W, F32)] * 7, [(1, SC_W)] * 4, tm=ts,
                    name=f"{tag}_sc_bwd")
    d_sb = outs[0]
    d_sc = _add_n([outs[1], _unshift(outs[3], 1), _unshift(outs[5], 2)], tm=ts, name=f"{tag}_sc_dc")
    d_sx = _add_n([outs[2], _unshift(outs[4], 1), _unshift(outs[6], 2)], tm=ts, name=f"{tag}_sc_dx")
    G['sc_conv_w'] = jnp.concatenate(outs[7:10], axis=0)
    G['sc_norm_g'] = outs[10][0]

    def rw_post_bwd(*vv):
        rows5, ct, par3, b_ = vv[:5], vv[5], vv[6:9], vv[9]
        return _vjp_rows(_rw_post, 1, 8, 1)(b_, *rows5, *par3, ct)

    outs = _rowwise(rw_post_bwd, [R['y'], R['r'], R['k2'], R['v'], R['g'], dy_rw], R['post_par'] + [bd],
                    [(RW_W, F32)] * 5, [(1, RW_W)] * 3, tm=ts, name=f"{tag}_rw_dpost")
    d_y, dr_p, dk_p, dv_p, d_g = outs[:5]
    G['rwkv_lnx_w'], G['rwkv_lnx_b'], G['rwkv_r_k'] = outs[5][0], outs[6][0], outs[7][0]
    dyb = _bcast_cols(d_y, _stacked_bf16(bd), name=f"{tag}_rw_dycols")
    dr_s, d_w, dk_s, d_c, d_b, dvt, *rode = _rw_scan_bwd(R['r'], R['w'], R['k2'], R['c'], R['b'], R['vb'], dyb, R['spre'],
                                                        _stacked_bf16(bd), name=f"{tag}_rw_dscan", ride=ride)
    dv_s = dvt

    def rw_pre_bwd(*vv):
        zrows = vv[0:8]
        dr1, dr2, dw_, dk1, dk2_, dc_, db_, dv1, dv2, dg_ = vv[8:18]
        par, b_ = vv[18:29], vv[29]
        return _vjp_rows(_rw_pre, 1, 19, 7)(b_, *zrows, *par, dr1 + dr2, dw_, dk1 + dk2_, dc_, db_, dv1 + dv2, dg_)

    par_shapes = [tuple(q.shape) for q in R['rw_par']]
    widths = [(q.shape[1], F32) for q in R['z_rows']]
    outs = _rowwise(rw_pre_bwd, R['z_rows'] + [dr_p, dr_s, d_w, dk_p, dk_s, d_c, d_b, dv_p, dv_s, d_g],
                    R['rw_par'] + [bd], widths, par_shapes, tm=ts, name=f"{tag}_rw_dpre")
    d_z = _add_n([jnp.concatenate(outs[0:4], axis=1), _unshift(jnp.concatenate(outs[4:8], axis=1), 1)], tm=ts,
                 name=f"{tag}_rw_dz")
    dpar = outs[8:]
    G['rwkv_mu'] = jnp.concatenate([q[0] for q in dpar[0:4]])
    G['rwkv_w0'], G['rwkv_a0'] = dpar[4][0], dpar[6][0]
    G['rwkv_w2'], G['rwkv_a2'], G['rwkv_g2'] = dpar[5][0:32], dpar[7][32:64], dpar[8][64:128]
    G['rwkv_k_k'], G['rwkv_k_a'] = dpar[9][0], dpar[10][0]

    dp = jnp.concatenate([d_lx, d_lg, d_sb, d_sc, d_sx, d_z], axis=1).astype(BF16)
    da = _mm(dp[None], wi[None], trans_b=True, tm=tb, tn=D_MODEL, out_dtype=F32, name=f"{tag}_da")
    d_wi = _mm_tn(R['a'][None], dp[None], tk=D_MODEL // 2, name=f"{tag}_dwin")[0]
    dh, dg_pre = _norm_bwd(R['h'], g_pre, da, 1.0, dh_new, F32, tm=ts, name=f"{tag}_dnorm")
    return dh, dg_pre, dg_post, d_wi, d_wo, G, rode


def _loss_rows(h, tgt, n_seq, *, tm, name):
    d = h.shape[1]

    def body(h_ref, t_ref, dh_ref, l_ref):
        i = pl.program_id(0)
        row = lax.broadcasted_iota(jnp.int32, (tm, 1), 0) + i * tm
        live = (row >= N_META) & (row < N_META + n_seq)
        e = jnp.where(live, h_ref[...] - t_ref[...], 0.0)
        dh_ref[...] = e * (1.0 / d)
        part = 0.5 * jnp.sum(jnp.sum(e * e, axis=1, keepdims=True) * (1.0 / d), axis=0, keepdims=True)

        @pl.when(i == 0)
        def _():
            l_ref[...] = part

        @pl.when(i > 0)
        def _():
            l_ref[...] += part

    blk = pl.BlockSpec((tm, d), lambda i: (i, 0))
    return pl.pallas_call(body, name=name, grid=(h.shape[0] // tm,), in_specs=[blk, blk],
                          out_specs=[blk, pl.BlockSpec((1, 1), lambda i: (0, 0))],
                          out_shape=[jax.ShapeDtypeStruct(h.shape, F32), jax.ShapeDtypeStruct((1, 1), F32)])(h, tgt)


def _pack(arrs, mult):
    flat = jnp.concatenate([a.reshape(-1).astype(F32) for a in arrs])
    n = flat.shape[0]
    tot = -(-n // mult) * mult
    return jnp.pad(flat, (0, tot - n)).reshape(-1, LANES)


def _unpack(buf, shapes):
    flat = buf.reshape(-1)
    out, off = [], 0
    for s in shapes:
        n = 1
        for q in s:
            n *= q
        out.append(flat[off:off + n].reshape(s))
        off += n
    return out


def _step(W, M, V, x, loss_target):
    n_seq = x.shape[1]
    t_real = N_META + n_seq
    t = (t_real // CHUNK + 1) * CHUNK
    tiles = (_pick(t, (704, 512, 256, 128, 64)), _pick(t, (192, 128, 64)))
    me = 4 * lax.axis_index("x") + 2 * lax.axis_index("y") + lax.axis_index("c")
    n_layer = W['norm_g'].shape[0]

    small_sh = list(SMALL_SHARDED)
    packed = _pack([W[n] for n in small_sh], 8 * LANES)
    early = ['ffn1_w_in', 'ffn1_w_out', 'mix_w_in', 'mix_w_out']
    late = ['ffn2_w_in', 'ffn2_w_out']
    gathered = _exchange([W[n][0].astype(BF16) for n in early] + [packed], gather=True, name="gather_weights")
    big8 = [dict(zip(early, gathered[:-1]))] + [{} for _ in range(n_layer - 1)]
    pieces = [_unpack(gathered[-1][q], [W[n].shape for n in small_sh]) for q in range(N_DEV)]
    full = {n: W[n] for n in SMALL if n not in SMALL_SHARDED}
    for idx, n in enumerate(small_sh):
        full[n] = jnp.concatenate([pieces[q][idx] for q in range(N_DEV)], axis=SMALL_SHARDED[n])

    def ffn_weights(l, which):
        w24 = big8[l][f'{which}_w_in'].reshape(2, N_DEV // 2, D_MODEL, FFN_BLK)
        wo4 = big8[l][f'{which}_w_out'].reshape(N_DEV // 2, FFN_BLK, D_MODEL)
        return w24, wo4

    def mixer_weights(l):
        return big8[l]['mix_w_in'].transpose(1, 0, 2).reshape(D_MODEL, N_IN), big8[l]['mix_w_out'].reshape(D_MODEL, D_MODEL)

    bd = jnp.kron(jnp.eye(LANES // HEAD, dtype=F32), jnp.ones((HEAD, HEAD), F32))
    small_layer = [n for n in SMALL if n not in ('meta_tokens', 'norm_g')]

    h = jnp.concatenate([full['meta_tokens'], x[0], jnp.zeros((t - t_real, D_MODEL), F32)], axis=0)
    saved = []
    for l in range(n_layer):
        ng = [_row(full['norm_g'][l, q]) for q in range(6)]
        P = {n: full[n][l] for n in small_layer}
        w1 = ffn_weights(l, 'ffn1')
        h, r1 = _ffn_fwd(h, ng[0], ng[1], w1[0], w1[1], tiles, f"l{l}_ffn1")
        riders = [(l, n) for n in late] + ([(l + 1, n) for n in BIG] if l + 1 < n_layer else [])
        wm = mixer_weights(l)
        h, r2, rode = _mixer_fwd(h, ng[2], ng[3], wm[0], wm[1], P, bd, tiles, f"l{l}_mix",
                                 ride=([W[n][q].astype(BF16) for q, n in riders], True))
        for (q, n), arrived in zip(riders, rode):
            big8[q][n] = arrived
        w2 = ffn_weights(l, 'ffn2')
        h, r3 = _ffn_fwd(h, ng[4], ng[5], w2[0], w2[1], tiles, f"l{l}_ffn2")
        saved.append(((w1[0], w1[1], w2[0], w2[1], wm[0], wm[1]), ng, P, r1, r2, r3))

    tgt = jnp.pad(loss_target[0], ((N_META, t - t_real), (0, 0)))
    dh, loss_part = _loss_rows(h, tgt, n_seq, tm=tiles[1], name="loss")
    loss = lax.psum(loss_part[0, 0], MESH_AXES)

    small_grads = [None] * n_layer
    norm_grads = [None] * n_layer
    recv = [{} for _ in range(n_layer)]
    outgoing = []
    for l in reversed(range(n_layer)):
        lw, ng, P, r1, r2, r3 = saved[l]
        dh, g4, g5, d_win2, d_wo2 = _ffn_bwd(dh, r3, ng[4], ng[5], lw[2], lw[3], tiles, f"l{l}_ffn2")
        outgoing += [((l, 'ffn2_w_in'), d_win2), ((l, 'ffn2_w_out'), d_wo2)]
        dh, g2, g3, d_wi, d_wo, G, rode = _mixer_bwd(dh, r2, ng[2], ng[3], lw[4], lw[5], P, bd, tiles, f"l{l}_mix",
                                                     ride=([a for _, a in outgoing], False))
        for ((q, n), _), arrived in zip(outgoing, rode):
            recv[q][n] = arrived
        dh, g0, g1, d_win1, d_wo1 = _ffn_bwd(dh, r1, ng[0], ng[1], lw[0], lw[1], tiles, f"l{l}_ffn1")
        small_grads[l] = G
        norm_grads[l] = jnp.concatenate([g0, g1, g2, g3, g4, g5], axis=0)
        d_wi8 = d_wi.reshape(D_MODEL, N_DEV, N_IN // N_DEV).transpose(1, 0, 2)
        d_wo8 = d_wo.reshape(N_DEV, D_MODEL // N_DEV, D_MODEL)
        outgoing = [((l, 'ffn1_w_in'), d_win1), ((l, 'ffn1_w_out'), d_wo1), ((l, 'mix_w_in'), d_wi8), ((l, 'mix_w_out'), d_wo8)]
    for ((q, n), _), arrived in zip(outgoing, _exchange([a for _, a in outgoing], gather=False, name="l0_grad_exchange")):
        recv[q][n] = arrived

    gs = {n: jnp.stack([small_grads[l][n] for l in range(n_layer)]) for n in small_layer}
    gs['norm_g'] = jnp.stack(norm_grads)
    gs['meta_tokens'] = dh[:N_META]
    gpack = _pack([gs[n] for n in SMALL], 8 * LANES)
    gall = _exchange([gpack], gather=True, name="gather_small_grads")[0]
    gsum = _rowwise(lambda parts: _sum_slots(parts), [gall], [], [(LANES, F32)], [], tm=gall.shape[1],
                    name="sum_small_grads")[0]
    gfull = dict(zip(SMALL, _unpack(gsum, [gs[n].shape for n in SMALL])))

    def my_shard(n, a):
        if n not in SMALL_SHARDED:
            return a
        ax = SMALL_SHARDED[n]
        size = a.shape[ax] // N_DEV
        return lax.dynamic_slice_in_dim(a, me * size, size, axis=ax)

    g_loc = [my_shard(n, gfull[n]) for n in SMALL]
    shapes = [W[n].shape for n in SMALL]
    bufs = [_pack(g_loc, 8 * LANES)] + [_pack([D[n] for n in SMALL], 8 * LANES) for D in (W, M, V)]
    d_s, m_s, v_s = _rowwise(_adamw_rows, bufs, [], [(LANES, F32)] * 3, [], tm=bufs[0].shape[0], name="adamw_small")
    out = {'grad': dict(zip(SMALL, g_loc)), 'delta': dict(zip(SMALL, _unpack(d_s, shapes))),
           'm': dict(zip(SMALL, _unpack(m_s, shapes))), 'v': dict(zip(SMALL, _unpack(v_s, shapes)))}

    order = ['ffn1_w_in', 'ffn1_w_out', 'ffn2_w_in', 'ffn2_w_out', 'mix_w_in', 'mix_w_out']
    for idx, n in enumerate(order):
        per_layer = []
        for l in range(n_layer):
            parts = recv[l][n]
            rows, cols = W[n].shape[1], W[n].shape[2]
            per_layer.append(_reduce_adamw(parts.reshape(N_DEV, rows, cols), W[n][l], M[n][l], V[n][l],
                                           name=f"l{l}_adamw_{n}"))
        for q, key in enumerate(('grad', 'delta', 'm', 'v')):
            out[key][n] = jnp.stack([per_layer[l][q] for l in range(n_layer)])

    return (loss, dh[N_META:t_real][None],
            *[out['grad'][n] for n in WEIGHTS], *[out['delta'][n] for n in WEIGHTS],
            *[out['m'][n] for n in WEIGHTS], *[out['v'][n] for n in WEIGHTS])


def kernel(x, meta_tokens, norm_g, ffn1_w_in, ffn1_w_out, ffn2_w_in, ffn2_w_out, mix_w_in, mix_w_out, lru_conv_w, lru_conv_b, lru_wa, lru_ba, lru_wx, lru_bx, lru_lambda, lru_norm_g, sc_conv_w, sc_norm_g, rwkv_mu, rwkv_w0, rwkv_w2, rwkv_a0, rwkv_a2, rwkv_g2, rwkv_k_k, rwkv_k_a, rwkv_r_k, rwkv_lnx_w, rwkv_lnx_b, loss_target, m_meta_tokens, m_norm_g, m_ffn1_w_in, m_ffn1_w_out, m_ffn2_w_in, m_ffn2_w_out, m_mix_w_in, m_mix_w_out, m_lru_conv_w, m_lru_conv_b, m_lru_wa, m_lru_ba, m_lru_wx, m_lru_bx, m_lru_lambda, m_lru_norm_g, m_sc_conv_w, m_sc_norm_g, m_rwkv_mu, m_rwkv_w0, m_rwkv_w2, m_rwkv_a0, m_rwkv_a2, m_rwkv_g2, m_rwkv_k_k, m_rwkv_k_a, m_rwkv_r_k, m_rwkv_lnx_w, m_rwkv_lnx_b, v_meta_tokens, v_norm_g, v_ffn1_w_in, v_ffn1_w_out, v_ffn2_w_in, v_ffn2_w_out, v_mix_w_in, v_mix_w_out, v_lru_conv_w, v_lru_conv_b, v_lru_wa, v_lru_ba, v_lru_wx, v_lru_bx, v_lru_lambda, v_lru_norm_g, v_sc_conv_w, v_sc_norm_g, v_rwkv_mu, v_rwkv_w0, v_rwkv_w2, v_rwkv_a0, v_rwkv_a2, v_rwkv_g2, v_rwkv_k_k, v_rwkv_k_a, v_rwkv_r_k, v_rwkv_lnx_w, v_rwkv_lnx_b):
    w_vals = (meta_tokens, norm_g, ffn1_w_in, ffn1_w_out, ffn2_w_in, ffn2_w_out, mix_w_in, mix_w_out, lru_conv_w, lru_conv_b, lru_wa, lru_ba, lru_wx, lru_bx, lru_lambda, lru_norm_g, sc_conv_w, sc_norm_g, rwkv_mu, rwkv_w0, rwkv_w2, rwkv_a0, rwkv_a2, rwkv_g2, rwkv_k_k, rwkv_k_a, rwkv_r_k, rwkv_lnx_w, rwkv_lnx_b)
    m_vals = (m_meta_tokens, m_norm_g, m_ffn1_w_in, m_ffn1_w_out, m_ffn2_w_in, m_ffn2_w_out, m_mix_w_in, m_mix_w_out, m_lru_conv_w, m_lru_conv_b, m_lru_wa, m_lru_ba, m_lru_wx, m_lru_bx, m_lru_lambda, m_lru_norm_g, m_sc_conv_w, m_sc_norm_g, m_rwkv_mu, m_rwkv_w0, m_rwkv_w2, m_rwkv_a0, m_rwkv_a2, m_rwkv_g2, m_rwkv_k_k, m_rwkv_k_a, m_rwkv_r_k, m_rwkv_lnx_w, m_rwkv_lnx_b)
    v_vals = (v_meta_tokens, v_norm_g, v_ffn1_w_in, v_ffn1_w_out, v_ffn2_w_in, v_ffn2_w_out, v_mix_w_in, v_mix_w_out, v_lru_conv_w, v_lru_conv_b, v_lru_wa, v_lru_ba, v_lru_wx, v_lru_bx, v_lru_lambda, v_lru_norm_g, v_sc_conv_w, v_sc_norm_g, v_rwkv_mu, v_rwkv_w0, v_rwkv_w2, v_rwkv_a0, v_rwkv_a2, v_rwkv_g2, v_rwkv_k_k, v_rwkv_k_a, v_rwkv_r_k, v_rwkv_lnx_w, v_rwkv_lnx_b)
    return _step(dict(zip(WEIGHTS, w_vals)), dict(zip(WEIGHTS, m_vals)), dict(zip(WEIGHTS, v_vals)), x, loss_target)
```

```python
import functools

import jax
import jax.numpy as jnp
from jax import lax
from jax.experimental import pallas as pl
from jax.experimental.pallas import tpu as pltpu

F32 = jnp.float32
BF16 = jnp.bfloat16
HIGHEST = lax.Precision.HIGHEST

N_DEV = 8
MESH_AXES = ("x", "y", "c")
N_META = 16
D_MODEL = 1024
LRU_W = 256
SC_W = 256
RW_W = 512
HEAD = 64
LANES = 128
CHUNK = 64
RW_IN = 1664
N_IN = 2944
FFN_BLK = 704
RMS_EPS = 1e-6
LNX_EPS = 64e-5
LRU_C = 8.0
ADAM_LR, ADAM_B1, ADAM_B2, ADAM_EPS, ADAM_WD, ADAM_STEP = 0.001, 0.9, 0.999, 1e-08, 0.01, 10

WEIGHTS = ['meta_tokens', 'norm_g', 'ffn1_w_in', 'ffn1_w_out', 'ffn2_w_in', 'ffn2_w_out', 'mix_w_in', 'mix_w_out',
           'lru_conv_w', 'lru_conv_b', 'lru_wa', 'lru_ba', 'lru_wx', 'lru_bx', 'lru_lambda', 'lru_norm_g',
           'sc_conv_w', 'sc_norm_g', 'rwkv_mu', 'rwkv_w0', 'rwkv_w2', 'rwkv_a0', 'rwkv_a2', 'rwkv_g2', 'rwkv_k_k',
           'rwkv_k_a', 'rwkv_r_k', 'rwkv_lnx_w', 'rwkv_lnx_b']
BIG = ['ffn1_w_in', 'ffn1_w_out', 'ffn2_w_in', 'ffn2_w_out', 'mix_w_in', 'mix_w_out']
SMALL_SHARDED = {'meta_tokens': 1, 'norm_g': 2, 'lru_conv_w': 2, 'sc_conv_w': 2, 'rwkv_w2': 2, 'rwkv_a2': 2, 'rwkv_g2': 2}
SMALL = [n for n in WEIGHTS if n not in BIG]


def _pick(n, cands):
    for c in cands:
        if n % c == 0:
            return c
    raise ValueError(f"no tile for {n}")


def _rowwise(fn, rows, params, row_outs, acc_outs, *, tm, name):
    nr, npar, nro, nao = len(rows), len(params), len(row_outs), len(acc_outs)
    n_rows = rows[0].shape[-2]
    assert n_rows % tm == 0, (name, n_rows, tm)

    def body(*refs):
        vals = [r[...] for r in refs[:nr + npar]]
        outs = fn(*vals)
        if not isinstance(outs, (tuple, list)):
            outs = (outs,)
        assert len(outs) == nro + nao, (name, len(outs))
        for o_ref, o in zip(refs[nr + npar:nr + npar + nro], outs[:nro]):
            o_ref[...] = o.astype(o_ref.dtype)
        step = pl.program_id(0)
        for a_ref, a in zip(refs[nr + npar + nro:], outs[nro:]):
            @pl.when(step == 0)
            def _(a_ref=a_ref, a=a):
                a_ref[...] = a.astype(F32)

            @pl.when(step > 0)
            def _(a_ref=a_ref, a=a):
                a_ref[...] += a.astype(F32)

    def row_spec(shape):
        if len(shape) == 2:
            return pl.BlockSpec((tm, shape[1]), lambda i: (i, 0))
        return pl.BlockSpec((shape[0], tm, shape[2]), lambda i: (0, i, 0))

    def full_spec(shape):
        nd = len(shape)
        return pl.BlockSpec(tuple(shape), lambda i, nd=nd: (0,) * nd)

    in_specs = [row_spec(r.shape) for r in rows] + [full_spec(p.shape) for p in params]
    out_shape = [jax.ShapeDtypeStruct((n_rows, w), dt) for (w, dt) in row_outs]
    out_shape += [jax.ShapeDtypeStruct(tuple(s), F32) for s in acc_outs]
    out_specs = [row_spec((n_rows, w)) for (w, _) in row_outs] + [full_spec(s) for s in acc_outs]
    res = pl.pallas_call(body, name=name, grid=(n_rows // tm,), in_specs=in_specs, out_specs=out_specs,
                         out_shape=out_shape)(*rows, *params)
    return tuple(res)


def _mm(a3, b3, *, trans_b, tm, tn, out_dtype, name):
    nj, m, kb = a3.shape
    n = b3.shape[1] if trans_b else b3.shape[2]
    dims = (((1,), (1,)), ((), ())) if trans_b else (((1,), (0,)), ((), ()))

    def body(a_ref, b_ref, o_ref):
        acc = lax.dot_general(a_ref[0], b_ref[0], dims, preferred_element_type=F32)
        for j in range(1, nj):
            acc = acc + lax.dot_general(a_ref[j], b_ref[j], dims, preferred_element_type=F32)
        o_ref[...] = acc.astype(o_ref.dtype)

    if trans_b:
        b_spec = pl.BlockSpec((nj, tn, kb), lambda i, c: (0, c, 0))
    else:
        b_spec = pl.BlockSpec((nj, kb, tn), lambda i, c: (0, 0, c))
    return pl.pallas_call(
        body, name=name, grid=(m // tm, n // tn),
        in_specs=[pl.BlockSpec((nj, tm, kb), lambda i, c: (0, i, 0)), b_spec],
        out_specs=pl.BlockSpec((tm, tn), lambda i, c: (i, c)),
        out_shape=jax.ShapeDtypeStruct((m, n), out_dtype),
    )(a3, b3)


def _mm_tn(a3, b3, *, tk, name):
    ja, t, ka = a3.shape
    jb, _, n = b3.shape
    nj = max(ja, jb)

    def body(a_ref, b_ref, o_ref):
        o_ref[0] = lax.dot_general(a_ref[0], b_ref[0], (((0,), (0,)), ((), ())),
                                   preferred_element_type=F32).astype(o_ref.dtype)

    return pl.pallas_call(
        body, name=name, grid=(nj, ka // tk),
        in_specs=[pl.BlockSpec((1, t, tk), (lambda j, c: (j, 0, c)) if ja > 1 else (lambda j, c: (0, 0, c))),
                  pl.BlockSpec((1, t, n), (lambda j, c: (j, 0, 0)) if jb > 1 else (lambda j, c: (0, 0, 0)))],
        out_specs=pl.BlockSpec((1, tk, n), lambda j, c: (j, c, 0)),
        out_shape=jax.ShapeDtypeStruct((nj, ka, n), BF16),
    )(a3, b3)


def _ffn_in(a, w24, *, tm, name):
    t, d = a.shape
    nb, fb = w24.shape[1], w24.shape[3]

    def body(a_ref, w_ref, gu_ref, s_ref):
        x = a_ref[...]
        g = jnp.dot(x, w_ref[0, 0], preferred_element_type=F32)
        u = jnp.dot(x, w_ref[1, 0], preferred_element_type=F32)
        gu_ref[0, 0] = g.astype(BF16)
        gu_ref[1, 0] = u.astype(BF16)
        s_ref[0] = (g * jax.nn.sigmoid(g) * u).astype(BF16)

    return pl.pallas_call(
        body, name=name, grid=(nb, t // tm),
        in_specs=[pl.BlockSpec((tm, d), lambda j, i: (i, 0)), pl.BlockSpec((2, 1, d, fb), lambda j, i: (0, j, 0, 0))],
        out_specs=[pl.BlockSpec((2, 1, tm, fb), lambda j, i: (0, j, i, 0)), pl.BlockSpec((1, tm, fb), lambda j, i: (j, i, 0))],
        out_shape=[jax.ShapeDtypeStruct((2, nb, t, fb), BF16), jax.ShapeDtypeStruct((nb, t, fb), BF16)],
    )(a, w24)


def _ffn_dswiglu(df, wo4, gu, *, tm, name):
    t, d = df.shape
    nb, fb = wo4.shape[0], wo4.shape[1]

    def body(df_ref, wo_ref, gu_ref, dg_ref):
        ds = lax.dot_general(df_ref[...], wo_ref[0], (((1,), (1,)), ((), ())), preferred_element_type=F32)
        g = gu_ref[0, 0].astype(F32)
        u = gu_ref[1, 0].astype(F32)
        sig = jax.nn.sigmoid(g)
        dg_ref[0, 0] = (ds * u * sig * (1.0 + g * (1.0 - sig))).astype(BF16)
        dg_ref[1, 0] = (ds * g * sig).astype(BF16)

    return pl.pallas_call(
        body, name=name, grid=(nb, t // tm),
        in_specs=[pl.BlockSpec((tm, d), lambda j, i: (i, 0)), pl.BlockSpec((1, fb, d), lambda j, i: (j, 0, 0)),
                  pl.BlockSpec((2, 1, tm, fb), lambda j, i: (0, j, i, 0))],
        out_specs=pl.BlockSpec((2, 1, tm, fb), lambda j, i: (0, j, i, 0)),
        out_shape=jax.ShapeDtypeStruct((2, nb, t, fb), BF16),
    )(df, wo4, gu)


def _rms(x, g):
    return x * lax.rsqrt(jnp.mean(x * x, axis=-1, keepdims=True) + RMS_EPS) * g


def _rms_bwd(x, g, dy):
    rstd = lax.rsqrt(jnp.mean(x * x, axis=-1, keepdims=True) + RMS_EPS)
    xh = x * rstd
    dxh = dy * g
    dx = rstd * (dxh - xh * jnp.mean(dxh * xh, axis=-1, keepdims=True))
    return dx, jnp.sum(dy * xh, axis=0, keepdims=True)


def _seg_sum_impl(x, bd):
    parts = [jnp.dot(x[:, q * LANES:(q + 1) * LANES], bd, preferred_element_type=F32, precision=HIGHEST)
             for q in range(x.shape[1] // LANES)]
    return parts[0] if len(parts) == 1 else jnp.concatenate(parts, axis=1)


@jax.custom_vjp
def _seg_sum(x, bd):
    return _seg_sum_impl(x, bd)


def _seg_sum_fwd(x, bd):
    return _seg_sum_impl(x, bd), bd


def _seg_sum_bwd(bd, ct):
    return _seg_sum_impl(ct, bd), jnp.zeros_like(bd)


_seg_sum.defvjp(_seg_sum_fwd, _seg_sum_bwd)


def _group_rms(y, g, bd):
    return y * lax.rsqrt(_seg_sum(y * y, bd) * (1.0 / HEAD) + RMS_EPS) * g


def _expm1(x):
    return jnp.where(jnp.abs(x) < 1e-2, x * (1.0 + x * (0.5 + x * (1.0 / 6.0))), jnp.exp(x) - 1.0)


def _lru_pre(x0, x1, x2, x3, cw0, cw1, cw2, cw3, cb, wa, ba, wx, bx, lam):
    u = x3 * cw0 + x2 * cw1 + x1 * cw2 + x0 * cw3 + cb
    r = jax.nn.sigmoid(jnp.dot(u, wa, preferred_element_type=F32, precision=HIGHEST) + ba)
    i = jax.nn.sigmoid(jnp.dot(u, wx, preferred_element_type=F32, precision=HIGHEST) + bx)
    log_a = -LRU_C * r * jax.nn.softplus(-lam)
    return jnp.exp(log_a), jnp.sqrt(-_expm1(2.0 * log_a)) * (i * u)


def _lru_post(bd, gate, hs, ng):
    return _group_rms(jax.nn.gelu(gate) * hs, ng, bd)


def _sc_fwd(bd, b, c0, x0, c1, x1, c2, x2, w0, w1, w2, ng):
    return _group_rms(b * (w0 * (c2 * x2) + w1 * (c1 * x1) + w2 * (c0 * x0)), ng, bd)


def _rw_pre(bd, zr, zk, zv, zt, sr, sk, sv, st, mur, muk, muv, mut, w0, w2p, a0, a2p, g2p, k_k, k_a):
    r, k, v, tail = zr + (sr - zr) * mur, zk + (sk - zk) * muk, zv + (sv - zv) * muv, zt + (st - zt) * mut
    lane = lax.broadcasted_iota(jnp.int32, tail.shape, 1)
    act = jnp.where(lane < 32, jnp.tanh(tail), jnp.where(lane < 64, tail, jax.nn.sigmoid(tail)))
    dot = functools.partial(jnp.dot, preferred_element_type=F32, precision=HIGHEST)
    w_log = -jax.nn.softplus(-(w0 + dot(act, w2p))) - 0.5
    w = jnp.exp(-jnp.exp(w_log))
    a = jax.nn.sigmoid(a0 + dot(act, a2p))
    g = dot(act, g2p)
    kk = k * k_k
    k2 = k * (1.0 + (a - 1.0) * k_a)
    kkn = kk * lax.rsqrt(jnp.maximum(_seg_sum(kk * kk, bd), 1e-24))
    return r, w, k2, -kkn, kkn * a, v, g


def _rw_post(bd, y, r, k2, v, g, lnw, lnb, r_k):
    mean = _seg_sum(y, bd) * (1.0 / HEAD)
    yc = y - mean
    var = _seg_sum(yc * yc, bd) * (1.0 / HEAD)
    yn = yc * lax.rsqrt(var + LNX_EPS) * lnw + lnb
    return (yn + _seg_sum(r * k2 * r_k, bd) * v) * g


def _vjp_rows(fwd, n_static, n_in, n_ct):
    def fn(*args):
        static, prim, cts = args[:n_static], args[n_static:n_static + n_in], args[n_static + n_in:]
        assert len(cts) == n_ct
        _, vjp = jax.vjp(functools.partial(fwd, *static), *prim)
        return vjp(cts[0] if n_ct == 1 else tuple(cts))
    return fn


def _exchange_copies(x_refs, o_refs, sems, gather):
    send_sems, recv_sems, local_sems = sems
    mx, my, mc = lax.axis_index("x"), lax.axis_index("y"), lax.axis_index("c")
    me = 4 * mx + 2 * my + mc
    local, sends, recvs = [], [], []
    for k in range(len(x_refs)):
        local.append(pltpu.make_async_copy(x_refs[k] if gather else x_refs[k].at[me], o_refs[k].at[me], local_sems.at[k]))
    for d in range(1, N_DEV):
        px, py, pc = mx ^ ((d >> 2) & 1), my ^ ((d >> 1) & 1), mc ^ (d & 1)
        peer = 4 * px + 2 * py + pc
        for k in range(len(x_refs)):
            src = x_refs[k] if gather else x_refs[k].at[peer]
            common = dict(src_ref=src, send_sem=send_sems.at[k, d - 1], recv_sem=recv_sems.at[k, d - 1],
                          device_id=(px, py, pc), device_id_type=pl.DeviceIdType.MESH)
            sends.append(pltpu.make_async_remote_copy(dst_ref=o_refs[k].at[me], **common))
            recvs.append(pltpu.make_async_remote_copy(dst_ref=o_refs[k].at[peer], **common))
    return local, sends, recvs


def _gather2_copies(x_refs, o_refs, sems):
    send_sems, recv_sems, local_sems = sems
    mx, my, mc = lax.axis_index("x"), lax.axis_index("y"), lax.axis_index("c")
    sibling = (mx, my, 1 - mc)
    chips = [(1 - mx, my), (mx, 1 - my), (1 - mx, 1 - my)]

    def slot(px, py, pc):
        return 4 * px + 2 * py + pc

    out = dict(local=[], first=[], first_recv=[], ici_recv=[], passed=[], passed_recv=[])
    for k in range(len(x_refs)):
        def copy(sem, src, dst_slot, to, k=k):
            return pltpu.make_async_remote_copy(src_ref=src, dst_ref=o_refs[k].at[dst_slot], send_sem=send_sems.at[k, sem],
                                                recv_sem=recv_sems.at[k, sem], device_id=to, device_id_type=pl.DeviceIdType.MESH)
        me = slot(mx, my, mc)
        out['local'].append(pltpu.make_async_copy(x_refs[k], o_refs[k].at[me], local_sems.at[k]))
        out['first'].append(copy(0, x_refs[k], me, sibling))
        out['first_recv'].append(copy(0, x_refs[k], slot(mx, my, 1 - mc), sibling))
        for j, (px, py) in enumerate(chips):
            out['first'].append(copy(1 + j, x_refs[k], me, (px, py, mc)))
            out['ici_recv'].append(copy(1 + j, x_refs[k], slot(px, py, mc), (px, py, mc)))
            out['passed'].append(copy(4 + j, o_refs[k].at[slot(px, py, mc)], slot(px, py, mc), sibling))
            out['passed_recv'].append(copy(4 + j, x_refs[k], slot(px, py, 1 - mc), sibling))
    return out


def _exchange_start(x_refs, o_refs, sems, gather):
    if gather:
        cps = _gather2_copies(x_refs, o_refs, sems)
        for cp in cps['local'] + cps['first']:
            cp.start()
        return
    local, sends, _ = _exchange_copies(x_refs, o_refs, sems, gather)
    for cp in local + sends:
        cp.start()


def _exchange_wait(x_refs, o_refs, sems, gather):
    if gather:
        cps = _gather2_copies(x_refs, o_refs, sems)
        for arrived, onward in zip(cps['ici_recv'], cps['passed']):
            arrived.wait_recv()
            onward.start()
        for cp in cps['first'] + cps['passed']:
            cp.wait_send()
        for cp in cps['first_recv'] + cps['passed_recv']:
            cp.wait_recv()
        for cp in cps['local']:
            cp.wait()
        return
    local, sends, recvs = _exchange_copies(x_refs, o_refs, sems, gather)
    for cp in sends:
        cp.wait_send()
    for cp in recvs:
        cp.wait_recv()
    for cp in local:
        cp.wait()


def _exchange_out_shape(xs, gather):
    return [jax.ShapeDtypeStruct(((N_DEV,) + x.shape) if gather else x.shape, x.dtype) for x in xs]


def _exchange_sems(n):
    return [pltpu.SemaphoreType.DMA((n, N_DEV - 1)), pltpu.SemaphoreType.DMA((n, N_DEV - 1)), pltpu.SemaphoreType.DMA((n,))]


SUBLANES = 8


def _store_row(ref, i, cols, row):
    base = pl.multiple_of((i // SUBLANES) * SUBLANES, SUBLANES)
    sub = lax.broadcasted_iota(jnp.int32, (SUBLANES, row.shape[1]), 0)
    ref[pl.ds(base, SUBLANES), cols] = jnp.where(sub == i % SUBLANES, row, ref[pl.ds(base, SUBLANES), cols])


def _tile_scan(a, b, reverse):
    sub = lax.broadcasted_iota(jnp.int32, a.shape, 0)
    for sh in (1, 2, 4):
        if reverse:
            live = sub < SUBLANES - sh
            a_s, b_s = pltpu.roll(a, SUBLANES - sh, 0), pltpu.roll(b, SUBLANES - sh, 0)
        else:
            live = sub >= sh
            a_s, b_s = pltpu.roll(a, sh, 0), pltpu.roll(b, sh, 0)
        b = jnp.where(live, a * b_s, 0.0) + b
        a = jnp.where(live, a * a_s, a)
    return a, b


def _lru_scan(a, b, name):
    t, w = a.shape

    def body(a_ref, b_ref, h_ref):
        def tile(j, h):
            rows = pl.ds(pl.multiple_of(j * SUBLANES, SUBLANES), SUBLANES)
            ca, cb = _tile_scan(a_ref[rows, :], b_ref[rows, :], False)
            out = ca * h + cb
            h_ref[rows, :] = out
            return out[SUBLANES - 1:SUBLANES]
        lax.fori_loop(0, t // SUBLANES, tile, jnp.zeros((1, w), F32))

    return pl.pallas_call(body, name=name, out_shape=jax.ShapeDtypeStruct((t, w), F32))(a, b)


def _lru_scan_bwd(a_next, h_prev, dhs, name):
    t, w = dhs.shape

    def body(a_ref, h_ref, dh_ref, da_ref, db_ref):
        def tile(n, lam):
            rows = pl.ds(pl.multiple_of((t // SUBLANES - 1 - n) * SUBLANES, SUBLANES), SUBLANES)
            ca, cb = _tile_scan(a_ref[rows, :], dh_ref[rows, :], True)
            out = ca * lam + cb
            db_ref[rows, :] = out
            da_ref[rows, :] = out * h_ref[rows, :]
            return out[0:1]
        lax.fori_loop(0, t // SUBLANES, tile, jnp.zeros((1, w), F32))

    return pl.pallas_call(body, name=name, out_shape=[jax.ShapeDtypeStruct((t, w), F32)] * 2)(a_next, h_prev, dhs)


N_PAIR = RW_W // LANES
UNROLL = 2


def _bcast_cols(v, bd2, name):
    t = v.shape[0]

    def body(v_ref, bd_ref, o_ref):
        bdv = bd_ref[...]
        sub = lax.broadcasted_iota(jnp.int32, (HEAD, LANES), 0)
        own = lax.broadcasted_iota(jnp.int32, (HEAD, LANES), 1) % HEAD == sub
        for i in range(CHUNK):
            row = v_ref[i:i + 1, :]
            sums = _group_sums([jnp.where(own, row[:, p * LANES:(p + 1) * LANES], 0.0) for p in range(N_PAIR)], bdv)
            for p in range(N_PAIR):
                o_ref[i, p] = sums[p * HEAD:(p + 1) * HEAD]

    return pl.pallas_call(
        body, name=name, grid=(t // CHUNK,),
        in_specs=[pl.BlockSpec((CHUNK, RW_W), lambda i: (i, 0)), pl.BlockSpec((2 * LANES, LANES), lambda i: (0, 0))],
        out_specs=pl.BlockSpec((CHUNK, N_PAIR, HEAD, LANES), lambda i: (i, 0, 0, 0)),
        out_shape=jax.ShapeDtypeStruct((t, N_PAIR, HEAD, LANES), F32),
    )(v, bd2)


def _cols_to_rows(cols_ref, rows_ref):
    lane = lax.broadcasted_iota(jnp.int32, (CHUNK, LANES), 1)
    for p in range(N_PAIR):
        tile = cols_ref[p]
        sq = jnp.concatenate([tile, jnp.zeros_like(tile)], axis=0).T
        rows_ref[:, p * LANES:(p + 1) * LANES] = jnp.where(lane < HEAD, sq[0:CHUNK], pltpu.roll(sq[CHUNK:2 * CHUNK], HEAD, 1))


def _stacked_bf16(bd):
    return jnp.concatenate([bd, bd], axis=0).astype(BF16)


def _group_sums(prods, bd2):
    x = jnp.concatenate(prods, axis=0)
    hi = x.astype(BF16)
    lo = (x - hi.astype(F32)).astype(BF16)
    return jnp.dot(jnp.concatenate([hi, lo], axis=1), bd2, preferred_element_type=F32)


def _rw_scan(r, r_prev, w, k, c, c_next, b, vb, bd2, name, ride=None):
    t = w.shape[0]
    nch = t // CHUNK
    ride_xs, ride_gather = ride if ride is not None else ([], False)
    n_ride = len(ride_xs)

    def body(*refs):
        r_ref, rp_ref, w_ref, k_ref, c_ref, cn_ref, b_ref, vb_ref, bd_ref = refs[:9]
        x_refs = refs[9:9 + n_ride]
        spre_ref, y_ref = refs[9 + n_ride:11 + n_ride]
        o_refs = refs[11 + n_ride:11 + 2 * n_ride]
        s_ref, wc_ref, wr_ref, bc_ref, kc_ref, br_ref, kr_ref, yt_ref = refs[11 + 2 * n_ride:19 + 2 * n_ride]
        sems = refs[19 + 2 * n_ride:]

        @pl.when(pl.program_id(0) == 0)
        def _():
            s_ref[...] = jnp.zeros_like(s_ref)
            if n_ride:
                _exchange_start(x_refs, o_refs, sems, ride_gather)

        yt_ref[...] = jnp.zeros_like(yt_ref)
        bdv = bd_ref[...]
        lane = lax.broadcasted_iota(jnp.int32, (HEAD, LANES), 1) % CHUNK

        wv, cn, rv, bv, kv = w_ref[...], cn_ref[...], r_ref[...], b_ref[...], k_ref[...]
        wc_ref[...] = wv * cn
        wr_ref[...] = wv * rv
        for ref, x in ((bc_ref, bv * cn), (kc_ref, kv * cn), (br_ref, bv * rv), (kr_ref, kv * rv)):
            sums = _group_sums([x[:, q * LANES:(q + 1) * LANES] for q in range(N_PAIR)], bdv)
            for q in range(N_PAIR):
                ref[:, q * LANES:(q + 1) * LANES] = sums[q * CHUNK:(q + 1) * CHUNK]

        def cut(ref, i):
            x = ref[pl.ds(i, 1), :]
            return [x[:, p * LANES:(p + 1) * LANES] for p in range(N_PAIR)]

        def two_steps(j, st):
            i0 = 2 * j
            i1 = i0 + 1
            c0, wc0, rp0, wr0 = cut(c_ref, i0), cut(wc_ref, i0), cut(rp_ref, i0), cut(wr_ref, i0)
            w0, b0, k0, w1, b1, k1 = cut(w_ref, i0), cut(b_ref, i0), cut(k_ref, i0), cut(w_ref, i1), cut(b_ref, i1), cut(k_ref, i1)
            bc0, kc0, br0, kr0 = cut(bc_ref, i0), cut(kc_ref, i0), cut(br_ref, i0), cut(kr_ref, i0)
            pairs = range(N_PAIR)
            red = _group_sums([st[p] * c0[p] for p in pairs] + [st[p] * wc0[p] for p in pairs], bdv)
            out = _group_sums([st[p] * rp0[p] for p in pairs] + [st[p] * wr0[p] for p in pairs], bdv)
            new = []
            for p in pairs:
                v0, v1 = vb_ref[i0, p], vb_ref[i1, p]
                sa0 = red[p * HEAD:(p + 1) * HEAD]
                sa1 = red[(N_PAIR + p) * HEAD:(N_PAIR + p + 1) * HEAD] + sa0 * bc0[p] + v0 * kc0[p]
                y_before = out[p * HEAD:(p + 1) * HEAD]
                y0 = out[(N_PAIR + p) * HEAD:(N_PAIR + p + 1) * HEAD] + sa0 * br0[p] + v0 * kr0[p]
                spre_ref[i0, p] = st[p]
                s1 = st[p] * w0[p] + sa0 * b0[p] + v0 * k0[p]
                spre_ref[i1, p] = s1
                new.append(s1 * w1[p] + sa1 * b1[p] + v1 * k1[p])
                yt_ref[p] = jnp.where(lane == i0, y_before, jnp.where(lane == i1, y0, yt_ref[p]))
            return tuple(new)

        st = lax.fori_loop(0, CHUNK // 2, two_steps, tuple(s_ref[p] for p in range(N_PAIR)))
        for p in range(N_PAIR):
            s_ref[p] = st[p]
        _cols_to_rows(yt_ref, y_ref)

        if n_ride:
            @pl.when(pl.program_id(0) == nch - 1)
            def _():
                _exchange_wait(x_refs, o_refs, sems, ride_gather)

    row = pl.BlockSpec((CHUNK, RW_W), lambda i: (i, 0))
    big = pl.BlockSpec((CHUNK, N_PAIR, HEAD, LANES), lambda i: (i, 0, 0, 0))
    any_spec = pl.BlockSpec(memory_space=pl.ANY)
    return pl.pallas_call(
        body, name=name, grid=(nch,),
        in_specs=[row] * 7 + [big, pl.BlockSpec((2 * LANES, LANES), lambda i: (0, 0))] + [any_spec] * n_ride,
        out_specs=[big, row] + [any_spec] * n_ride,
        out_shape=[jax.ShapeDtypeStruct((t, N_PAIR, HEAD, LANES), F32), jax.ShapeDtypeStruct((t, RW_W), F32)]
        + _exchange_out_shape(ride_xs, ride_gather),
        scratch_shapes=[pltpu.VMEM((N_PAIR, HEAD, LANES), F32)] + [pltpu.VMEM((CHUNK, RW_W), F32)] * 6
        + [pltpu.VMEM((N_PAIR, HEAD, LANES), F32)] + (_exchange_sems(n_ride) if n_ride else []),
    )(r, r_prev, w, k, c, c_next, b, vb, bd2, *ride_xs)


def _rw_scan_bwd(r, w, k, c, b, vb, dyb, spre, bd, name, ride=None):
    t = r.shape[0]
    nch = t // CHUNK
    ride_xs, ride_gather = ride if ride is not None else ([], False)
    n_ride = len(ride_xs)
    n_pre = 8

    def body(*refs):
        r_ref, w_ref, k_ref, c_ref, b_ref, vb_ref, dyb_ref, spre_ref, bd_ref = refs[:9]
        x_refs = refs[9:9 + n_ride]
        dr_ref, dw_ref, dk_ref, dc_ref, db_ref, dv_ref = refs[9 + n_ride:15 + n_ride]
        o_refs = refs[15 + n_ride:15 + 2 * n_ride]
        g_ref, snext_ref, dvt_ref = refs[15 + 2 * n_ride:18 + 2 * n_ride]
        pre = refs[18 + 2 * n_ride:18 + 2 * n_ride + n_pre]
        sems = refs[18 + 2 * n_ride + n_pre:]
        wb_ref, wk_ref, rb_ref, rk_ref, rwb_ref, cb_ref, rwk_ref, ck_ref = pre[:8]

        @pl.when(pl.program_id(0) == 0)
        def _():
            g_ref[...] = jnp.zeros_like(g_ref)
            snext_ref[...] = jnp.zeros_like(snext_ref)
            if n_ride:
                _exchange_start(x_refs, o_refs, sems, ride_gather)

        for ref in (dr_ref, dw_ref, dk_ref, dc_ref, db_ref, dvt_ref):
            ref[...] = jnp.zeros_like(ref)
        bdv = bd_ref[...]
        bd1 = bdv[0:LANES]
        lane = lax.broadcasted_iota(jnp.int32, (HEAD, LANES), 1) % CHUNK

        rv, wv, kv, cv, bv = r_ref[...], w_ref[...], k_ref[...], c_ref[...], b_ref[...]
        b_b, k_b = pltpu.roll(bv, 1, 0), pltpu.roll(kv, 1, 0)
        wb_ref[...] = wv * b_b
        wk_ref[...] = wv * k_b
        rw = rv * wv
        for ref, x in ((rb_ref, rv * bv), (rk_ref, rv * kv), (rwb_ref, rw * b_b), (cb_ref, cv * b_b), (rwk_ref, rw * k_b),
                       (ck_ref, cv * k_b)):
            sums = _group_sums([x[:, q * LANES:(q + 1) * LANES] for q in range(N_PAIR)], bdv)
            for q in range(N_PAIR):
                ref[:, q * LANES:(q + 1) * LANES] = sums[q * CHUNK:(q + 1) * CHUNK]

        def sum0(x):
            return jnp.sum(x, axis=0, keepdims=True)

        def cut(ref, i):
            x = ref[pl.ds(i, 1), :]
            return [x[:, p * LANES:(p + 1) * LANES] for p in range(N_PAIR)]

        def sums_bf16(prods):
            return jnp.dot(jnp.concatenate(prods, axis=0).astype(BF16), bd1, preferred_element_type=F32)

        def two_steps(n, gs):
            ia = CHUNK - 1 - 2 * n
            ib = ia - 1
            r_a, w_a, k_a, c_a, b_a = [cut(ref, ia) for ref in (r_ref, w_ref, k_ref, c_ref, b_ref)]
            r_b, w_b, c_b = [cut(ref, ib) for ref in (r_ref, w_ref, c_ref)]
            wb, wk, rb_a, rk_a, rwb, cb, rwk, ck = [cut(ref, ia) for ref in pre[:8]]
            rb_b, rk_b = cut(rb_ref, ib), cut(rk_ref, ib)
            pairs = range(N_PAIR)
            sp_a = [spre_ref[ia, p] for p in pairs]
            sp_b = [spre_ref[ib, p] for p in pairs]
            dy_a = [dyb_ref[ia, p] for p in pairs]
            dy_b = [dyb_ref[ib, p] for p in pairs]
            chain = _group_sums([gs[p] * b_a[p] for p in pairs] + [gs[p] * wb[p] for p in pairs], bdv)
            off = sums_bf16([gs[p] * k_a[p] for p in pairs] + [gs[p] * wk[p] for p in pairs]
                            + [sp_a[p] * c_a[p] for p in pairs] + [sp_b[p] * c_b[p] for p in pairs])
            new = []
            for p in pairs:
                def part(x, q, p=p):
                    return x[(q * N_PAIR + p) * HEAD:(q * N_PAIR + p + 1) * HEAD]
                dsa_a = part(chain, 0) + dy_a[p] * rb_a[p]
                dv_a = part(off, 0) + dy_a[p] * rk_a[p]
                dsa_b = part(chain, 1) + dy_a[p] * rwb[p] + dsa_a * cb[p] + dy_b[p] * rb_b[p]
                dv_b = part(off, 1) + dy_a[p] * rwk[p] + dsa_a * ck[p] + dy_b[p] * rk_b[p]
                sa_a, sa_b = part(off, 2), part(off, 3)
                g_a = gs[p] + dy_a[p] * r_a[p]
                g_mid = g_a * w_a[p] + dsa_a * c_a[p]
                g_b = g_mid + dy_b[p] * r_b[p]
                new.append(g_b * w_b[p] + dsa_b * c_b[p])
                cols = pl.ds(p * LANES, LANES)
                for i, dy, s_post, s_pre, g, sa, dsa in ((ia, dy_a[p], snext_ref[p], sp_a[p], g_a, sa_a, dsa_a),
                                                         (ib, dy_b[p], sp_a[p], sp_b[p], g_b, sa_b, dsa_b)):
                    _store_row(dr_ref, i, cols, sum0(s_post * dy))
                    _store_row(dw_ref, i, cols, sum0(g * s_pre))
                    _store_row(db_ref, i, cols, sum0(g * sa))
                    _store_row(dk_ref, i, cols, sum0(g * vb_ref[i, p]))
                    _store_row(dc_ref, i, cols, sum0(s_pre * dsa))
                dvt_ref[p] = jnp.where(lane == ia, dv_a, jnp.where(lane == ib, dv_b, dvt_ref[p]))
                snext_ref[p] = sp_b[p]
            return tuple(new)

        gs = lax.fori_loop(0, CHUNK // 2, two_steps, tuple(g_ref[p] for p in range(N_PAIR)))
        for p in range(N_PAIR):
            g_ref[p] = gs[p]
        _cols_to_rows(dvt_ref, dv_ref)

        if n_ride:
            @pl.when(pl.program_id(0) == nch - 1)
            def _():
                _exchange_wait(x_refs, o_refs, sems, ride_gather)

    row = pl.BlockSpec((CHUNK, RW_W), lambda i: (nch - 1 - i, 0))
    big = pl.BlockSpec((CHUNK, N_PAIR, HEAD, LANES), lambda i: (nch - 1 - i, 0, 0, 0))
    any_spec = pl.BlockSpec(memory_space=pl.ANY)
    return pl.pallas_call(
        body, name=name, grid=(nch,),
        in_specs=[row] * 5 + [big, big, big, pl.BlockSpec((2 * LANES, LANES), lambda i: (0, 0))] + [any_spec] * n_ride,
        out_specs=[row] * 6 + [any_spec] * n_ride,
        out_shape=[jax.ShapeDtypeStruct((t, RW_W), F32)] * 6 + _exchange_out_shape(ride_xs, ride_gather),
        scratch_shapes=[pltpu.VMEM((N_PAIR, HEAD, LANES), F32)] * 3
        + [pltpu.VMEM((CHUNK, RW_W), F32)] * n_pre + (_exchange_sems(n_ride) if n_ride else []),
    )(r, w, k, c, b, vb, dyb, spre, bd, *ride_xs)


def _exchange(xs, *, gather, name, gather_too=()):
    n, n2 = len(xs), len(gather_too)

    def body(*refs):
        x1, x2 = refs[:n], refs[n:n + n2]
        o1, o2 = refs[n + n2:2 * n + n2], refs[2 * n + n2:2 * (n + n2)]
        sems1, sems2 = refs[2 * (n + n2):2 * (n + n2) + 3], refs[2 * (n + n2) + 3:]
        _exchange_start(x1, o1, sems1, gather)
        if n2:
            _exchange_start(x2, o2, sems2, True)
            _exchange_wait(x2, o2, sems2, True)
        _exchange_wait(x1, o1, sems1, gather)

    any_spec = pl.BlockSpec(memory_space=pl.ANY)
    return pl.pallas_call(
        body, name=name, in_specs=[any_spec] * (n + n2), out_specs=[any_spec] * (n + n2),
        out_shape=_exchange_out_shape(xs, gather) + _exchange_out_shape(list(gather_too), True),
        scratch_shapes=_exchange_sems(n) + (_exchange_sems(n2) if n2 else []),
    )(*xs, *gather_too)


def _adamw_rows(g, w, m, v):
    m = ADAM_B1 * m + (1.0 - ADAM_B1) * g
    v = ADAM_B2 * v + (1.0 - ADAM_B2) * (g * g)
    m_hat = m / (1.0 - ADAM_B1 ** ADAM_STEP)
    v_hat = v / (1.0 - ADAM_B2 ** ADAM_STEP)
    return -ADAM_LR * (m_hat / (jnp.sqrt(v_hat) + ADAM_EPS) + ADAM_WD * w), m, v


def _sum_slots(parts):
    g = parts[0].astype(F32)
    for q in range(1, N_DEV):
        g = g + parts[q].astype(F32)
    return g


def _reduce_adamw(parts, w, m, v, name):
    rows, cols = w.shape

    def fn(parts, w, m, v):
        g = _sum_slots(parts)
        return (g,) + _adamw_rows(g, w, m, v)

    return _rowwise(fn, [parts, w, m, v], [], [(cols, F32)] * 4, [], tm=_pick(rows, (256, 128, 64, 32, 16, 8)), name=name)


def _shift(x, n):
    return jnp.pad(x, ((n, 0), (0, 0)))[:-n]


def _unshift(x, n):
    return jnp.pad(x, ((0, n), (0, 0)))[n:]


def _add_n(xs, *, tm, name):
    def fn(*vals):
        s = vals[0]
        for x in vals[1:]:
            s = s + x
        return s
    return _rowwise(fn, xs, [], [(xs[0].shape[1], F32)], [], tm=tm, name=name)[0]


def _norm_fwd(h, g, *, tm, name):
    return _rowwise(lambda x, gg: _rms(x, gg), [h], [g], [(h.shape[1], BF16)], [], tm=tm, name=name)[0]


def _res_norm_fwd(h, f, g, scale, *, tm, name):
    return _rowwise(lambda hh, ff, gg: hh + scale * _rms(ff, gg), [h, f], [g], [(h.shape[1], F32)], [], tm=tm, name=name)[0]


def _norm_bwd(x, g, dy, scale, res, out_dtype, *, tm, name):
    if res is None:
        def fn(xx, dd, gg):
            dx, dg = _rms_bwd(xx, gg, dd * scale)
            return dx, dg
        rows = [x, dy]
    else:
        def fn(xx, dd, rr, gg):
            dx, dg = _rms_bwd(xx, gg, dd * scale)
            return dx + rr, dg
        rows = [x, dy, res]
    return _rowwise(fn, rows, [g], [(x.shape[1], out_dtype)], [(1, x.shape[1])], tm=tm, name=name)


def _ffn_fwd(h, g_pre, g_post, w24, wo4, tiles, tag):
    tb, ts = tiles
    a = _norm_fwd(h, g_pre, tm=ts, name=f"{tag}_norm")
    gu, s4 = _ffn_in(a, w24, tm=tb, name=f"{tag}_in")
    f = _mm(s4, wo4, trans_b=False, tm=tb, tn=D_MODEL, out_dtype=F32, name=f"{tag}_out")
    h_new = _res_norm_fwd(h, f, g_post, 0.5, tm=ts, name=f"{tag}_res")
    return h_new, (h, a, gu, s4, f)


def _ffn_bwd(dh_new, res, g_pre, g_post, w24, wo4, tiles, tag):
    tb, ts = tiles
    h, a, gu, s4, f = res
    t = h.shape[0]
    df, dg_post = _norm_bwd(f, g_post, dh_new, 0.5, None, BF16, tm=ts, name=f"{tag}_dres")
    dgu = _ffn_dswiglu(df, wo4, gu, tm=tb, name=f"{tag}_dswiglu")
    d_wo = _mm_tn(s4, df[None], tk=FFN_BLK, name=f"{tag}_dwout")
    dgu8 = dgu.reshape(2 * w24.shape[1], t, FFN_BLK)
    w8 = w24.reshape(2 * w24.shape[1], D_MODEL, FFN_BLK)
    da = _mm(dgu8, w8, trans_b=True, tm=tb, tn=D_MODEL // 2, out_dtype=F32, name=f"{tag}_da")
    d_win = _mm_tn(a[None], dgu8, tk=D_MODEL, name=f"{tag}_dwin")
    dh, dg_pre = _norm_bwd(h, g_pre, da, 1.0, dh_new, F32, tm=ts, name=f"{tag}_dnorm")
    return dh, dg_pre, dg_post, d_win, d_wo.reshape(N_DEV, -1, D_MODEL)


def _blockdiag(w4):
    n, b, _ = w4.shape
    eye = jnp.eye(n, dtype=w4.dtype)
    return (eye[:, None, :, None] * w4[:, :, None, :]).reshape(n * b, n * b)


def _blockdiag_grad(d):
    n = d.shape[0] // HEAD
    x = d.reshape(n, HEAD, n, HEAD)
    return jnp.stack([x[i, :, i, :] for i in range(n)])


def _row(v):
    return v.reshape(1, -1)


def _mixer_fwd(h, g_pre, g_post, wi, wo, P, bd, tiles, tag, ride=None):
    tb, ts = tiles
    a = _norm_fwd(h, g_pre, tm=ts, name=f"{tag}_norm")
    p = _mm(a[None], wi[None], trans_b=False, tm=tb, tn=N_IN, out_dtype=F32, name=f"{tag}_in")
    lx, lg = p[:, 0:256], p[:, 256:512]
    sb, scc, sx = p[:, 512:768], p[:, 768:1024], p[:, 1024:1280]
    z = p[:, 1280:]
    lxs = [lx, _shift(lx, 1), _shift(lx, 2), _shift(lx, 3)]
    cw = [_row(P['lru_conv_w'][kk]) for kk in range(4)]
    lru_par = cw + [_row(P['lru_conv_b']), _blockdiag(P['lru_wa']), _row(P['lru_ba']), _blockdiag(P['lru_wx']),
                    _row(P['lru_bx']), _row(P['lru_lambda'])]
    la, lb = _rowwise(_lru_pre, lxs, lru_par, [(LRU_W, F32)] * 2, [], tm=ts, name=f"{tag}_lru_pre")
    hs = _lru_scan(la, lb, name=f"{tag}_lru_scan")
    y_lru = _rowwise(lambda gg, hh, ng, b_: _lru_post(b_, gg, hh, ng), [lg, hs], [_row(P['lru_norm_g']), bd],
                     [(LRU_W, F32)], [], tm=ts, name=f"{tag}_lru_post")[0]
    sc_rows = [sb, scc, sx, _shift(scc, 1), _shift(sx, 1), _shift(scc, 2), _shift(sx, 2)]
    sc_par = [_row(P['sc_conv_w'][kk]) for kk in range(3)] + [_row(P['sc_norm_g'])]
    y_sc = _rowwise(lambda *v: _sc_fwd(v[-1], *v[:-1]), sc_rows, sc_par + [bd], [(SC_W, F32)], [], tm=ts,
                    name=f"{tag}_sc")[0]
    cuts = (0, RW_W, 2 * RW_W, 3 * RW_W, RW_IN)
    zs = [z[:, cuts[q]:cuts[q + 1]] for q in range(4)]
    z_rows = zs + [_shift(q, 1) for q in zs]
    pad = lambda m, lo: jnp.pad(m, ((lo, LANES - lo - m.shape[0]), (0, 0)))
    rw_par = [_row(P['rwkv_mu'][cuts[q]:cuts[q + 1]]) for q in range(4)]
    rw_par += [_row(P['rwkv_w0']), pad(P['rwkv_w2'], 0), _row(P['rwkv_a0']), pad(P['rwkv_a2'], 32),
               pad(P['rwkv_g2'], 64), _row(P['rwkv_k_k']), _row(P['rwkv_k_a'])]
    r, w, k2, c, b, v, g = _rowwise(lambda *vv: _rw_pre(vv[-1], *vv[:-1]), z_rows, rw_par + [bd], [(RW_W, F32)] * 7, [],
                                    tm=ts, name=f"{tag}_rw_pre")
    vb = _bcast_cols(v, _stacked_bf16(bd), name=f"{tag}_rw_vcols")
    spre, yt, *rode = _rw_scan(r, _shift(r, 1), w, k2, c, _unshift(c, 1), b, vb, _stacked_bf16(bd), name=f"{tag}_rw_scan",
                               ride=ride)
    y = _unshift(yt, 1)
    post_par =[_row(P['rwkv_lnx_w']), _row(P['rwkv_lnx_b']), _row(P['rwkv_r_k'])]
    y_rw = _rowwise(lambda *vv: _rw_post(vv[-1], *vv[:-1]), [y, r, k2, v, g], post_par + [bd], [(RW_W, F32)], [], tm=ts,
                    name=f"{tag}_rw_post")[0]
    ycat = jnp.concatenate([y_lru, y_sc, y_rw], axis=1).astype(BF16)
    m = _mm(ycat[None], wo[None], trans_b=False, tm=tb, tn=D_MODEL, out_dtype=F32, name=f"{tag}_out")
    h_new = _res_norm_fwd(h, m, g_post, 1.0, tm=ts, name=f"{tag}_res")
    res = dict(h=h, a=a, m=m, ycat=ycat, lxs=lxs, lru_par=lru_par, lg=lg, la=la, hs=hs, sc_rows=sc_rows, sc_par=sc_par,
               z_rows=z_rows, rw_par=rw_par, r=r, w=w, k2=k2, c=c, b=b, v=v, g=g, vb=vb, spre=spre, y=y, post_par=post_par)
    return h_new, res, rode


def _mixer_bwd(dh_new, R, g_pre, g_post, wi, wo, P, bd, tiles, tag, ride=None):
    tb, ts = tiles
    dm, dg_post = _norm_bwd(R['m'], g_post, dh_new, 1.0, None, BF16, tm=ts, name=f"{tag}_dres")
    dycat = _mm(dm[None], wo[None], trans_b=True, tm=tb, tn=D_MODEL, out_dtype=F32, name=f"{tag}_dycat")
    d_wo = _mm_tn(R['ycat'][None], dm[None], tk=D_MODEL // 2, name=f"{tag}_dwout")[0]
    dy_lru, dy_sc, dy_rw = dycat[:, 0:256], dycat[:, 256:512], dycat[:, 512:]
    G = {}
    d_lg, d_hs, G['lru_norm_g'] = _rowwise(
        lambda gg, hh, ct, ng, b_: _vjp_rows(_lru_post, 1, 3, 1)(b_, gg, hh, ng, ct),
        [R['lg'], R['hs'], dy_lru], [_row(P['lru_norm_g']), bd], [(LRU_W, F32)] * 2, [(1, LRU_W)], tm=ts,
        name=f"{tag}_lru_dpost")
    d_la, d_lb = _lru_scan_bwd(_unshift(R['la'], 1), _shift(R['hs'], 1), d_hs, name=f"{tag}_lru_dscan")

    def lru_pre_bwd(x0, x1, x2, x3, ca, cb_, *par):
        return _vjp_rows(_lru_pre, 0, 14, 2)(x0, x1, x2, x3, *par, ca, cb_)

    par_shapes = [tuple(q.shape) for q in R['lru_par']]
    outs = _rowwise(lru_pre_bwd, R['lxs'] + [d_la, d_lb], R['lru_par'], [(LRU_W, F32)] * 4, par_shapes, tm=ts,
                    name=f"{tag}_lru_dpre")
    dxs, dpar = outs[:4], outs[4:]
    d_lx = _add_n([dxs[0], _unshift(dxs[1], 1), _unshift(dxs[2], 2), _unshift(dxs[3], 3)], tm=ts, name=f"{tag}_lru_dx")
    G['lru_conv_w'] = jnp.concatenate(dpar[0:4], axis=0)
    G['lru_conv_b'] = dpar[4][0]
    G['lru_wa'] = _blockdiag_grad(dpar[5])
    G['lru_ba'] = dpar[6][0]
    G['lru_wx'] = _blockdiag_grad(dpar[7])
    G['lru_bx'] = dpar[8][0]
    G['lru_lambda'] = dpar[9][0]
    G['lru_norm_g'] = G['lru_norm_g'][0]

    def sc_bwd(*vv):
        rows7, ct, par4, b_ = vv[:7], vv[7], vv[8:12], vv[12]
        return _vjp_rows(_sc_fwd, 1, 11, 1)(b_, *rows7, *par4, ct)

    outs = _rowwise(sc_bwd, R['sc_rows'] + [dy_sc], R['sc_par'] + [bd], [(SC_W, F32)] * 7, [(1, SC_W)] * 4, tm=ts,
                    name=f"{tag}_sc_bwd")
    d_sb = outs[0]
    d_sc = _add_n([outs[1], _unshift(outs[3], 1), _unshift(outs[5], 2)], tm=ts, name=f"{tag}_sc_dc")
    d_sx = _add_n([outs[2], _unshift(outs[4], 1), _unshift(outs[6], 2)], tm=ts, name=f"{tag}_sc_dx")
    G['sc_conv_w'] = jnp.concatenate(outs[7:10], axis=0)
    G['sc_norm_g'] = outs[10][0]

    def rw_post_bwd(*vv):
        rows5, ct, par3, b_ = vv[:5], vv[5], vv[6:9], vv[9]
        return _vjp_rows(_rw_post, 1, 8, 1)(b_, *rows5, *par3, ct)

    outs = _rowwise(rw_post_bwd, [R['y'], R['r'], R['k2'], R['v'], R['g'], dy_rw], R['post_par'] + [bd],
                    [(RW_W, F32)] * 5, [(1, RW_W)] * 3, tm=ts, name=f"{tag}_rw_dpost")
    d_y, dr_p, dk_p, dv_p, d_g = outs[:5]
    G['rwkv_lnx_w'], G['rwkv_lnx_b'], G['rwkv_r_k'] = outs[5][0], outs[6][0], outs[7][0]
    dyb = _bcast_cols(d_y, _stacked_bf16(bd), name=f"{tag}_rw_dycols")
    dr_s, d_w, dk_s, d_c, d_b, dvt, *rode = _rw_scan_bwd(R['r'], R['w'], R['k2'], R['c'], R['b'], R['vb'], dyb, R['spre'],
                                                        _stacked_bf16(bd), name=f"{tag}_rw_dscan", ride=ride)
    dv_s = dvt

    def rw_pre_bwd(*vv):
        zrows = vv[0:8]
        dr1, dr2, dw_, dk1, dk2_, dc_, db_, dv1, dv2, dg_ = vv[8:18]
        par, b_ = vv[18:29], vv[29]
        return _vjp_rows(_rw_pre, 1, 19, 7)(b_, *zrows, *par, dr1 + dr2, dw_, dk1 + dk2_, dc_, db_, dv1 + dv2, dg_)

    par_shapes = [tuple(q.shape) for q in R['rw_par']]
    widths = [(q.shape[1], F32) for q in R['z_rows']]
    outs = _rowwise(rw_pre_bwd, R['z_rows'] + [dr_p, dr_s, d_w, dk_p, dk_s, d_c, d_b, dv_p, dv_s, d_g],
                    R['rw_par'] + [bd], widths, par_shapes, tm=ts, name=f"{tag}_rw_dpre")
    d_z = _add_n([jnp.concatenate(outs[0:4], axis=1), _unshift(jnp.concatenate(outs[4:8], axis=1), 1)], tm=ts,
                 name=f"{tag}_rw_dz")
    dpar = outs[8:]
    G['rwkv_mu'] = jnp.concatenate([q[0] for q in dpar[0:4]])
    G['rwkv_w0'], G['rwkv_a0'] = dpar[4][0], dpar[6][0]
    G['rwkv_w2'], G['rwkv_a2'], G['rwkv_g2'] = dpar[5][0:32], dpar[7][32:64], dpar[8][64:128]
    G['rwkv_k_k'], G['rwkv_k_a'] = dpar[9][0], dpar[10][0]

    dp = jnp.concatenate([d_lx, d_lg, d_sb, d_sc, d_sx, d_z], axis=1).astype(BF16)
    da = _mm(dp[None], wi[None], trans_b=True, tm=tb, tn=D_MODEL, out_dtype=F32, name=f"{tag}_da")
    d_wi = _mm_tn(R['a'][None], dp[None], tk=D_MODEL // 2, name=f"{tag}_dwin")[0]
    dh, dg_pre = _norm_bwd(R['h'], g_pre, da, 1.0, dh_new, F32, tm=ts, name=f"{tag}_dnorm")
    return dh, dg_pre, dg_post, d_wi, d_wo, G, rode


def _loss_rows(h, tgt, n_seq, *, tm, name):
    d = h.shape[1]

    def body(h_ref, t_ref, dh_ref, l_ref):
        i = pl.program_id(0)
        row = lax.broadcasted_iota(jnp.int32, (tm, 1), 0) + i * tm
        live = (row >= N_META) & (row < N_META + n_seq)
        e = jnp.where(live, h_ref[...] - t_ref[...], 0.0)
        dh_ref[...] = e * (1.0 / d)
        part = 0.5 * jnp.sum(jnp.sum(e * e, axis=1, keepdims=True) * (1.0 / d), axis=0, keepdims=True)

        @pl.when(i == 0)
        def _():
            l_ref[...] = part

        @pl.when(i > 0)
        def _():
            l_ref[...] += part

    blk = pl.BlockSpec((tm, d), lambda i: (i, 0))
    return pl.pallas_call(body, name=name, grid=(h.shape[0] // tm,), in_specs=[blk, blk],
                          out_specs=[blk, pl.BlockSpec((1, 1), lambda i: (0, 0))],
                          out_shape=[jax.ShapeDtypeStruct(h.shape, F32), jax.ShapeDtypeStruct((1, 1), F32)])(h, tgt)


def _pack(arrs, mult):
    flat = jnp.concatenate([a.reshape(-1).astype(F32) for a in arrs])
    n = flat.shape[0]
    tot = -(-n // mult) * mult
    return jnp.pad(flat, (0, tot - n)).reshape(-1, LANES)


def _unpack(buf, shapes):
    flat = buf.reshape(-1)
    out, off = [], 0
    for s in shapes:
        n = 1
        for q in s:
            n *= q
        out.append(flat[off:off + n].reshape(s))
        off += n
    return out


def _step(W, M, V, x, loss_target):
    n_seq = x.shape[1]
    t_real = N_META + n_seq
    t = (t_real // CHUNK + 1) * CHUNK
    tiles = (_pick(t, (704, 512, 256, 128, 64)), _pick(t, (192, 128, 64)))
    me = 4 * lax.axis_index("x") + 2 * lax.axis_index("y") + lax.axis_index("c")
    n_layer = W['norm_g'].shape[0]

    small_sh = list(SMALL_SHARDED)
    packed = _pack([W[n] for n in small_sh], 8 * LANES)
    early = ['ffn1_w_in', 'ffn1_w_out', 'mix_w_in', 'mix_w_out']
    late = ['ffn2_w_in', 'ffn2_w_out']
    gathered = _exchange([W[n][0].astype(BF16) for n in early] + [packed], gather=True, name="gather_weights")
    big8 = [dict(zip(early, gathered[:-1]))] + [{} for _ in range(n_layer - 1)]
    pieces = [_unpack(gathered[-1][q], [W[n].shape for n in small_sh]) for q in range(N_DEV)]
    full = {n: W[n] for n in SMALL if n not in SMALL_SHARDED}
    for idx, n in enumerate(small_sh):
        full[n] = jnp.concatenate([pieces[q][idx] for q in range(N_DEV)], axis=SMALL_SHARDED[n])

    def ffn_weights(l, which):
        w24 = big8[l][f'{which}_w_in'].reshape(2, N_DEV // 2, D_MODEL, FFN_BLK)
        wo4 = big8[l][f'{which}_w_out'].reshape(N_DEV // 2, FFN_BLK, D_MODEL)
        return w24, wo4

    def mixer_weights(l):
        return big8[l]['mix_w_in'].transpose(1, 0, 2).reshape(D_MODEL, N_IN), big8[l]['mix_w_out'].reshape(D_MODEL, D_MODEL)

    bd = jnp.kron(jnp.eye(LANES // HEAD, dtype=F32), jnp.ones((HEAD, HEAD), F32))
    small_layer = [n for n in SMALL if n not in ('meta_tokens', 'norm_g')]

    h = jnp.concatenate([full['meta_tokens'], x[0], jnp.zeros((t - t_real, D_MODEL), F32)], axis=0)
    saved = []
    for l in range(n_layer):
        ng = [_row(full['norm_g'][l, q]) for q in range(6)]
        P = {n: full[n][l] for n in small_layer}
        w1 = ffn_weights(l, 'ffn1')
        h, r1 = _ffn_fwd(h, ng[0], ng[1], w1[0], w1[1], tiles, f"l{l}_ffn1")
        riders = [(l, n) for n in late] + ([(l + 1, n) for n in BIG] if l + 1 < n_layer else [])
        wm = mixer_weights(l)
        h, r2, rode = _mixer_fwd(h, ng[2], ng[3], wm[0], wm[1], P, bd, tiles, f"l{l}_mix",
                                 ride=([W[n][q].astype(BF16) for q, n in riders], True))
        for (q, n), arrived in zip(riders, rode):
            big8[q][n] = arrived
        w2 = ffn_weights(l, 'ffn2')
        h, r3 = _ffn_fwd(h, ng[4], ng[5], w2[0], w2[1], tiles, f"l{l}_ffn2")
        saved.append(((w1[0], w1[1], w2[0], w2[1], wm[0], wm[1]), ng, P, r1, r2, r3))

    tgt = jnp.pad(loss_target[0], ((N_META, t - t_real), (0, 0)))
    dh, loss_part = _loss_rows(h, tgt, n_seq, tm=tiles[1], name="loss")
    loss = lax.psum(loss_part[0, 0], MESH_AXES)

    small_grads = [None] * n_layer
    norm_grads = [None] * n_layer
    recv = [{} for _ in range(n_layer)]
    outgoing = []
    for l in reversed(range(n_layer)):
        lw, ng, P, r1, r2, r3 = saved[l]
        dh, g4, g5, d_win2, d_wo2 = _ffn_bwd(dh, r3, ng[4], ng[5], lw[2], lw[3], tiles, f"l{l}_ffn2")
        outgoing += [((l, 'ffn2_w_in'), d_win2), ((l, 'ffn2_w_out'), d_wo2)]
        dh, g2, g3, d_wi, d_wo, G, rode = _mixer_bwd(dh, r2, ng[2], ng[3], lw[4], lw[5], P, bd, tiles, f"l{l}_mix",
                                                     ride=([a for _, a in outgoing], False))
        for ((q, n), _), arrived in zip(outgoing, rode):
            recv[q][n] = arrived
        dh, g0, g1, d_win1, d_wo1 = _ffn_bwd(dh, r1, ng[0], ng[1], lw[0], lw[1], tiles, f"l{l}_ffn1")
        small_grads[l] = G
        norm_grads[l] = jnp.concatenate([g0, g1, g2, g3, g4, g5], axis=0)
        d_wi8 = d_wi.reshape(D_MODEL, N_DEV, N_IN // N_DEV).transpose(1, 0, 2)
        d_wo8 = d_wo.reshape(N_DEV, D_MODEL // N_DEV, D_MODEL)
        outgoing = [((l, 'ffn1_w_in'), d_win1), ((l, 'ffn1_w_out'), d_wo1), ((l, 'mix_w_in'), d_wi8), ((l, 'mix_w_out'), d_wo8)]
    gs = {n: jnp.stack([small_grads[l][n] for l in range(n_layer)]) for n in small_layer}
    gs['norm_g'] = jnp.stack(norm_grads)
    gs['meta_tokens'] = dh[:N_META]
    gpack = _pack([gs[n] for n in SMALL], 8 * LANES)
    *last, gall = _exchange([a for _, a in outgoing], gather=False, name="last_grad_exchange", gather_too=[gpack])
    for ((q, n), _), arrived in zip(outgoing, last):
        recv[q][n] = arrived
    gsum =_rowwise(lambda parts: _sum_slots(parts), [gall], [], [(LANES, F32)], [], tm=gall.shape[1],
                    name="sum_small_grads")[0]
    gfull = dict(zip(SMALL, _unpack(gsum, [gs[n].shape for n in SMALL])))

    def my_shard(n, a):
        if n not in SMALL_SHARDED:
            return a
        ax = SMALL_SHARDED[n]
        size = a.shape[ax] // N_DEV
        return lax.dynamic_slice_in_dim(a, me * size, size, axis=ax)

    g_loc = [my_shard(n, gfull[n]) for n in SMALL]
    shapes = [W[n].shape for n in SMALL]
    bufs = [_pack(g_loc, 8 * LANES)] + [_pack([D[n] for n in SMALL], 8 * LANES) for D in (W, M, V)]
    d_s, m_s, v_s = _rowwise(_adamw_rows, bufs, [], [(LANES, F32)] * 3, [], tm=bufs[0].shape[0], name="adamw_small")
    out = {'grad': dict(zip(SMALL, g_loc)), 'delta': dict(zip(SMALL, _unpack(d_s, shapes))),
           'm': dict(zip(SMALL, _unpack(m_s, shapes))), 'v': dict(zip(SMALL, _unpack(v_s, shapes)))}

    order = ['ffn1_w_in', 'ffn1_w_out', 'ffn2_w_in', 'ffn2_w_out', 'mix_w_in', 'mix_w_out']
    for idx, n in enumerate(order):
        per_layer = []
        for l in range(n_layer):
            parts = recv[l][n]
            rows, cols = W[n].shape[1], W[n].shape[2]
            per_layer.append(_reduce_adamw(parts.reshape(N_DEV, rows, cols), W[n][l], M[n][l], V[n][l],
                                           name=f"l{l}_adamw_{n}"))
        for q, key in enumerate(('grad', 'delta', 'm', 'v')):
            out[key][n] = jnp.stack([per_layer[l][q] for l in range(n_layer)])

    return (loss, dh[N_META:t_real][None],
            *[out['grad'][n] for n in WEIGHTS], *[out['delta'][n] for n in WEIGHTS],
            *[out['m'][n] for n in WEIGHTS], *[out['v'][n] for n in WEIGHTS])


def kernel(x, meta_tokens, norm_g, ffn1_w_in, ffn1_w_out, ffn2_w_in, ffn2_w_out, mix_w_in, mix_w_out, lru_conv_w, lru_conv_b, lru_wa, lru_ba, lru_wx, lru_bx, lru_lambda, lru_norm_g, sc_conv_w, sc_norm_g, rwkv_mu, rwkv_w0, rwkv_w2, rwkv_a0, rwkv_a2, rwkv_g2, rwkv_k_k, rwkv_k_a, rwkv_r_k, rwkv_lnx_w, rwkv_lnx_b, loss_target, m_meta_tokens, m_norm_g, m_ffn1_w_in, m_ffn1_w_out, m_ffn2_w_in, m_ffn2_w_out, m_mix_w_in, m_mix_w_out, m_lru_conv_w, m_lru_conv_b, m_lru_wa, m_lru_ba, m_lru_wx, m_lru_bx, m_lru_lambda, m_lru_norm_g, m_sc_conv_w, m_sc_norm_g, m_rwkv_mu, m_rwkv_w0, m_rwkv_w2, m_rwkv_a0, m_rwkv_a2, m_rwkv_g2, m_rwkv_k_k, m_rwkv_k_a, m_rwkv_r_k, m_rwkv_lnx_w, m_rwkv_lnx_b, v_meta_tokens, v_norm_g, v_ffn1_w_in, v_ffn1_w_out, v_ffn2_w_in, v_ffn2_w_out, v_mix_w_in, v_mix_w_out, v_lru_conv_w, v_lru_conv_b, v_lru_wa, v_lru_ba, v_lru_wx, v_lru_bx, v_lru_lambda, v_lru_norm_g, v_sc_conv_w, v_sc_norm_g, v_rwkv_mu, v_rwkv_w0, v_rwkv_w2, v_rwkv_a0, v_rwkv_a2, v_rwkv_g2, v_rwkv_k_k, v_rwkv_k_a, v_rwkv_r_k, v_rwkv_lnx_w, v_rwkv_lnx_b):
    w_vals = (meta_tokens, norm_g, ffn1_w_in, ffn1_w_out, ffn2_w_in, ffn2_w_out, mix_w_in, mix_w_out, lru_conv_w, lru_conv_b, lru_wa, lru_ba, lru_wx, lru_bx, lru_lambda, lru_norm_g, sc_conv_w, sc_norm_g, rwkv_mu, rwkv_w0, rwkv_w2, rwkv_a0, rwkv_a2, rwkv_g2, rwkv_k_k, rwkv_k_a, rwkv_r_k, rwkv_lnx_w, rwkv_lnx_b)
    m_vals = (m_meta_tokens, m_norm_g, m_ffn1_w_in, m_ffn1_w_out, m_ffn2_w_in, m_ffn2_w_out, m_mix_w_in, m_mix_w_out, m_lru_conv_w, m_lru_conv_b, m_lru_wa, m_lru_ba, m_lru_wx, m_lru_bx, m_lru_lambda, m_lru_norm_g, m_sc_conv_w, m_sc_norm_g, m_rwkv_mu, m_rwkv_w0, m_rwkv_w2, m_rwkv_a0, m_rwkv_a2, m_rwkv_g2, m_rwkv_k_k, m_rwkv_k_a, m_rwkv_r_k, m_rwkv_lnx_w, m_rwkv_lnx_b)
    v_vals = (v_meta_tokens, v_norm_g, v_ffn1_w_in, v_ffn1_w_out, v_ffn2_w_in, v_ffn2_w_out, v_mix_w_in, v_mix_w_out, v_lru_conv_w, v_lru_conv_b, v_lru_wa, v_lru_ba, v_lru_wx, v_lru_bx, v_lru_lambda, v_lru_norm_g, v_sc_conv_w, v_sc_norm_g, v_rwkv_mu, v_rwkv_w0, v_rwkv_w2, v_rwkv_a0, v_rwkv_a2, v_rwkv_g2, v_rwkv_k_k, v_rwkv_k_a, v_rwkv_r_k, v_rwkv_lnx_w, v_rwkv_lnx_b)
    return _step(dict(zip(WEIGHTS, w_vals)), dict(zip(WEIGHTS, m_vals)), dict(zip(WEIGHTS, v_vals)), x, loss_target)
```

```python
import functools

import jax
import jax.numpy as jnp
from jax import lax
from jax.experimental import pallas as pl
from jax.experimental.pallas import tpu as pltpu

F32 = jnp.float32
BF16 = jnp.bfloat16
HIGHEST = lax.Precision.HIGHEST

N_DEV = 8
MESH_AXES = ("x", "y", "c")
N_META = 16
D_MODEL = 1024
LRU_W = 256
SC_W = 256
RW_W = 512
HEAD = 64
LANES = 128
CHUNK = 64
RW_IN = 1664
N_IN = 2944
FFN_BLK = 704
RMS_EPS = 1e-6
LNX_EPS = 64e-5
LRU_C = 8.0
ADAM_LR, ADAM_B1, ADAM_B2, ADAM_EPS, ADAM_WD, ADAM_STEP = 0.001, 0.9, 0.999, 1e-08, 0.01, 10

WEIGHTS = ['meta_tokens', 'norm_g', 'ffn1_w_in', 'ffn1_w_out', 'ffn2_w_in', 'ffn2_w_out', 'mix_w_in', 'mix_w_out',
           'lru_conv_w', 'lru_conv_b', 'lru_wa', 'lru_ba', 'lru_wx', 'lru_bx', 'lru_lambda', 'lru_norm_g',
           'sc_conv_w', 'sc_norm_g', 'rwkv_mu', 'rwkv_w0', 'rwkv_w2', 'rwkv_a0', 'rwkv_a2', 'rwkv_g2', 'rwkv_k_k',
           'rwkv_k_a', 'rwkv_r_k', 'rwkv_lnx_w', 'rwkv_lnx_b']
BIG = ['ffn1_w_in', 'ffn1_w_out', 'ffn2_w_in', 'ffn2_w_out', 'mix_w_in', 'mix_w_out']
SMALL_SHARDED = {'meta_tokens': 1, 'norm_g': 2, 'lru_conv_w': 2, 'sc_conv_w': 2, 'rwkv_w2': 2, 'rwkv_a2': 2, 'rwkv_g2': 2}
SMALL = [n for n in WEIGHTS if n not in BIG]


def _pick(n, cands):
    for c in cands:
        if n % c == 0:
            return c
    raise ValueError(f"no tile for {n}")


def _rowwise(fn, rows, params, row_outs, acc_outs, *, tm, name):
    nr, npar, nro, nao = len(rows), len(params), len(row_outs), len(acc_outs)
    n_rows = rows[0].shape[-2]
    assert n_rows % tm == 0, (name, n_rows, tm)

    def body(*refs):
        vals = [r[...] for r in refs[:nr + npar]]
        outs = fn(*vals)
        if not isinstance(outs, (tuple, list)):
            outs = (outs,)
        assert len(outs) == nro + nao, (name, len(outs))
        for o_ref, o in zip(refs[nr + npar:nr + npar + nro], outs[:nro]):
            o_ref[...] = o.astype(o_ref.dtype)
        step = pl.program_id(0)
        for a_ref, a in zip(refs[nr + npar + nro:], outs[nro:]):
            @pl.when(step == 0)
            def _(a_ref=a_ref, a=a):
                a_ref[...] = a.astype(F32)

            @pl.when(step > 0)
            def _(a_ref=a_ref, a=a):
                a_ref[...] += a.astype(F32)

    def row_spec(shape):
        if len(shape) == 2:
            return pl.BlockSpec((tm, shape[1]), lambda i: (i, 0))
        return pl.BlockSpec((shape[0], tm, shape[2]), lambda i: (0, i, 0))

    def full_spec(shape):
        nd = len(shape)
        return pl.BlockSpec(tuple(shape), lambda i, nd=nd: (0,) * nd)

    in_specs = [row_spec(r.shape) for r in rows] + [full_spec(p.shape) for p in params]
    out_shape = [jax.ShapeDtypeStruct((n_rows, w), dt) for (w, dt) in row_outs]
    out_shape += [jax.ShapeDtypeStruct(tuple(s), F32) for s in acc_outs]
    out_specs = [row_spec((n_rows, w)) for (w, _) in row_outs] + [full_spec(s) for s in acc_outs]
    res = pl.pallas_call(body, name=name, grid=(n_rows // tm,), in_specs=in_specs, out_specs=out_specs,
                         out_shape=out_shape)(*rows, *params)
    return tuple(res)


def _mm(a3, b3, *, trans_b, tm, tn, out_dtype, name):
    nj, m, kb = a3.shape
    n = b3.shape[1] if trans_b else b3.shape[2]
    dims = (((1,), (1,)), ((), ())) if trans_b else (((1,), (0,)), ((), ()))

    def body(a_ref, b_ref, o_ref):
        acc = lax.dot_general(a_ref[0], b_ref[0], dims, preferred_element_type=F32)
        for j in range(1, nj):
            acc = acc + lax.dot_general(a_ref[j], b_ref[j], dims, preferred_element_type=F32)
        o_ref[...] = acc.astype(o_ref.dtype)

    if trans_b:
        b_spec = pl.BlockSpec((nj, tn, kb), lambda i, c: (0, c, 0))
    else:
        b_spec = pl.BlockSpec((nj, kb, tn), lambda i, c: (0, 0, c))
    return pl.pallas_call(
        body, name=name, grid=(m // tm, n // tn),
        in_specs=[pl.BlockSpec((nj, tm, kb), lambda i, c: (0, i, 0)), b_spec],
        out_specs=pl.BlockSpec((tm, tn), lambda i, c: (i, c)),
        out_shape=jax.ShapeDtypeStruct((m, n), out_dtype),
    )(a3, b3)


def _mm_tn(a3, b3, *, tk, name):
    ja, t, ka = a3.shape
    jb, _, n = b3.shape
    nj = max(ja, jb)

    def body(a_ref, b_ref, o_ref):
        o_ref[0] = lax.dot_general(a_ref[0], b_ref[0], (((0,), (0,)), ((), ())),
                                   preferred_element_type=F32).astype(o_ref.dtype)

    return pl.pallas_call(
        body, name=name, grid=(nj, ka // tk),
        in_specs=[pl.BlockSpec((1, t, tk), (lambda j, c: (j, 0, c)) if ja > 1 else (lambda j, c: (0, 0, c))),
                  pl.BlockSpec((1, t, n), (lambda j, c: (j, 0, 0)) if jb > 1 else (lambda j, c: (0, 0, 0)))],
        out_specs=pl.BlockSpec((1, tk, n), lambda j, c: (j, c, 0)),
        out_shape=jax.ShapeDtypeStruct((nj, ka, n), BF16),
    )(a3, b3)


def _ffn_in(a, w24, *, tm, name):
    t, d = a.shape
    nb, fb = w24.shape[1], w24.shape[3]

    def body(a_ref, w_ref, gu_ref, s_ref):
        x = a_ref[...]
        g = jnp.dot(x, w_ref[0, 0], preferred_element_type=F32)
        u = jnp.dot(x, w_ref[1, 0], preferred_element_type=F32)
        gu_ref[0, 0] = g.astype(BF16)
        gu_ref[1, 0] = u.astype(BF16)
        s_ref[0] = (g * jax.nn.sigmoid(g) * u).astype(BF16)

    return pl.pallas_call(
        body, name=name, grid=(nb, t // tm),
        in_specs=[pl.BlockSpec((tm, d), lambda j, i: (i, 0)), pl.BlockSpec((2, 1, d, fb), lambda j, i: (0, j, 0, 0))],
        out_specs=[pl.BlockSpec((2, 1, tm, fb), lambda j, i: (0, j, i, 0)), pl.BlockSpec((1, tm, fb), lambda j, i: (j, i, 0))],
        out_shape=[jax.ShapeDtypeStruct((2, nb, t, fb), BF16), jax.ShapeDtypeStruct((nb, t, fb), BF16)],
    )(a, w24)


def _ffn_dswiglu(df, wo4, gu, *, tm, name):
    t, d = df.shape
    nb, fb = wo4.shape[0], wo4.shape[1]

    def body(df_ref, wo_ref, gu_ref, dg_ref):
        ds = lax.dot_general(df_ref[...], wo_ref[0], (((1,), (1,)), ((), ())), preferred_element_type=F32)
        g = gu_ref[0, 0].astype(F32)
        u = gu_ref[1, 0].astype(F32)
        sig = jax.nn.sigmoid(g)
        dg_ref[0, 0] = (ds * u * sig * (1.0 + g * (1.0 - sig))).astype(BF16)
        dg_ref[1, 0] = (ds * g * sig).astype(BF16)

    return pl.pallas_call(
        body, name=name, grid=(nb, t // tm),
        in_specs=[pl.BlockSpec((tm, d), lambda j, i: (i, 0)), pl.BlockSpec((1, fb, d), lambda j, i: (j, 0, 0)),
                  pl.BlockSpec((2, 1, tm, fb), lambda j, i: (0, j, i, 0))],
        out_specs=pl.BlockSpec((2, 1, tm, fb), lambda j, i: (0, j, i, 0)),
        out_shape=jax.ShapeDtypeStruct((2, nb, t, fb), BF16),
    )(df, wo4, gu)


def _rms(x, g):
    return x * lax.rsqrt(jnp.mean(x * x, axis=-1, keepdims=True) + RMS_EPS) * g


def _rms_bwd(x, g, dy):
    rstd = lax.rsqrt(jnp.mean(x * x, axis=-1, keepdims=True) + RMS_EPS)
    xh = x * rstd
    dxh = dy * g
    dx = rstd * (dxh - xh * jnp.mean(dxh * xh, axis=-1, keepdims=True))
    return dx, jnp.sum(dy * xh, axis=0, keepdims=True)


def _seg_sum_impl(x, bd):
    parts = [jnp.dot(x[:, q * LANES:(q + 1) * LANES], bd, preferred_element_type=F32, precision=HIGHEST)
             for q in range(x.shape[1] // LANES)]
    return parts[0] if len(parts) == 1 else jnp.concatenate(parts, axis=1)


@jax.custom_vjp
def _seg_sum(x, bd):
    return _seg_sum_impl(x, bd)


def _seg_sum_fwd(x, bd):
    return _seg_sum_impl(x, bd), bd


def _seg_sum_bwd(bd, ct):
    return _seg_sum_impl(ct, bd), jnp.zeros_like(bd)


_seg_sum.defvjp(_seg_sum_fwd, _seg_sum_bwd)


def _group_rms(y, g, bd):
    return y * lax.rsqrt(_seg_sum(y * y, bd) * (1.0 / HEAD) + RMS_EPS) * g


def _expm1(x):
    return jnp.where(jnp.abs(x) < 1e-2, x * (1.0 + x * (0.5 + x * (1.0 / 6.0))), jnp.exp(x) - 1.0)


def _lru_pre(x0, x1, x2, x3, cw0, cw1, cw2, cw3, cb, wa, ba, wx, bx, lam):
    u = x3 * cw0 + x2 * cw1 + x1 * cw2 + x0 * cw3 + cb
    r = jax.nn.sigmoid(jnp.dot(u, wa, preferred_element_type=F32, precision=HIGHEST) + ba)
    i = jax.nn.sigmoid(jnp.dot(u, wx, preferred_element_type=F32, precision=HIGHEST) + bx)
    log_a = -LRU_C * r * jax.nn.softplus(-lam)
    return jnp.exp(log_a), jnp.sqrt(-_expm1(2.0 * log_a)) * (i * u)


def _lru_post(bd, gate, hs, ng):
    return _group_rms(jax.nn.gelu(gate) * hs, ng, bd)


def _sc_fwd(bd, b, c0, x0, c1, x1, c2, x2, w0, w1, w2, ng):
    return _group_rms(b * (w0 * (c2 * x2) + w1 * (c1 * x1) + w2 * (c0 * x0)), ng, bd)


def _rw_pre(bd, zr, zk, zv, zt, sr, sk, sv, st, mur, muk, muv, mut, w0, w2p, a0, a2p, g2p, k_k, k_a):
    r, k, v, tail = zr + (sr - zr) * mur, zk + (sk - zk) * muk, zv + (sv - zv) * muv, zt + (st - zt) * mut
    lane = lax.broadcasted_iota(jnp.int32, tail.shape, 1)
    act = jnp.where(lane < 32, jnp.tanh(tail), jnp.where(lane < 64, tail, jax.nn.sigmoid(tail)))
    dot = functools.partial(jnp.dot, preferred_element_type=F32, precision=HIGHEST)
    w_log = -jax.nn.softplus(-(w0 + dot(act, w2p))) - 0.5
    w = jnp.exp(-jnp.exp(w_log))
    a = jax.nn.sigmoid(a0 + dot(act, a2p))
    g = dot(act, g2p)
    kk = k * k_k
    k2 = k * (1.0 + (a - 1.0) * k_a)
    kkn = kk * lax.rsqrt(jnp.maximum(_seg_sum(kk * kk, bd), 1e-24))
    return r, w, k2, -kkn, kkn * a, v, g


def _rw_post(bd, y, r, k2, v, g, lnw, lnb, r_k):
    mean = _seg_sum(y, bd) * (1.0 / HEAD)
    yc = y - mean
    var = _seg_sum(yc * yc, bd) * (1.0 / HEAD)
    yn = yc * lax.rsqrt(var + LNX_EPS) * lnw + lnb
    return (yn + _seg_sum(r * k2 * r_k, bd) * v) * g


def _vjp_rows(fwd, n_static, n_in, n_ct):
    def fn(*args):
        static, prim, cts = args[:n_static], args[n_static:n_static + n_in], args[n_static + n_in:]
        assert len(cts) == n_ct
        _, vjp = jax.vjp(functools.partial(fwd, *static), *prim)
        return vjp(cts[0] if n_ct == 1 else tuple(cts))
    return fn


def _all_to_all_copies(x_refs, o_refs, sems):
    send_sems, recv_sems, local_sems = sems
    mx, my, mc = lax.axis_index("x"), lax.axis_index("y"), lax.axis_index("c")
    me = 4 * mx + 2 * my + mc
    local, sends, recvs = [], [], []
    for k in range(len(x_refs)):
        local.append(pltpu.make_async_copy(x_refs[k].at[me], o_refs[k].at[me], local_sems.at[k]))
    for d in range(1, N_DEV):
        px, py, pc = mx ^ ((d >> 2) & 1), my ^ ((d >> 1) & 1), mc ^ (d & 1)
        peer = 4 * px + 2 * py + pc
        for k in range(len(x_refs)):
            common = dict(src_ref=x_refs[k].at[peer], send_sem=send_sems.at[k, d - 1], recv_sem=recv_sems.at[k, d - 1],
                          device_id=(px, py, pc), device_id_type=pl.DeviceIdType.MESH)
            sends.append(pltpu.make_async_remote_copy(dst_ref=o_refs[k].at[me], **common))
            recvs.append(pltpu.make_async_remote_copy(dst_ref=o_refs[k].at[peer], **common))
    return local, sends, recvs


def _gather2_copies(x_refs, o_refs, sems):
    send_sems, recv_sems, local_sems = sems
    mx, my, mc = lax.axis_index("x"), lax.axis_index("y"), lax.axis_index("c")
    sibling = (mx, my, 1 - mc)
    chips = [(1 - mx, my), (mx, 1 - my), (1 - mx, 1 - my)]

    def slot(px, py, pc):
        return 4 * px + 2 * py + pc

    out = dict(local=[], first=[], first_recv=[], ici_recv=[], passed=[], passed_recv=[])
    for k in range(len(x_refs)):
        def copy(sem, src, dst_slot, to, k=k):
            return pltpu.make_async_remote_copy(src_ref=src, dst_ref=o_refs[k].at[dst_slot], send_sem=send_sems.at[k, sem],
                                                recv_sem=recv_sems.at[k, sem], device_id=to, device_id_type=pl.DeviceIdType.MESH)
        me = slot(mx, my, mc)
        out['local'].append(pltpu.make_async_copy(x_refs[k], o_refs[k].at[me], local_sems.at[k]))
        out['first'].append(copy(0, x_refs[k], me, sibling))
        out['first_recv'].append(copy(0, x_refs[k], slot(mx, my, 1 - mc), sibling))
        for j, (px, py) in enumerate(chips):
            out['first'].append(copy(1 + j, x_refs[k], me, (px, py, mc)))
            out['ici_recv'].append(copy(1 + j, x_refs[k], slot(px, py, mc), (px, py, mc)))
            out['passed'].append(copy(4 + j, o_refs[k].at[slot(px, py, mc)], slot(px, py, mc), sibling))
            out['passed_recv'].append(copy(4 + j, x_refs[k], slot(px, py, 1 - mc), sibling))
    return out


def _exchange_start(x_refs, o_refs, sems, gather):
    if gather:
        cps = _gather2_copies(x_refs, o_refs, sems)
        for cp in cps['local'] + cps['first']:
            cp.start()
        return
    local, sends, _ = _all_to_all_copies(x_refs, o_refs, sems)
    for cp in local + sends:
        cp.start()


def _exchange_wait(x_refs, o_refs, sems, gather):
    if gather:
        cps = _gather2_copies(x_refs, o_refs, sems)
        for arrived, onward in zip(cps['ici_recv'], cps['passed']):
            arrived.wait_recv()
            onward.start()
        for cp in cps['first'] + cps['passed']:
            cp.wait_send()
        for cp in cps['first_recv'] + cps['passed_recv']:
            cp.wait_recv()
        for cp in cps['local']:
            cp.wait()
        return
    local, sends, recvs = _all_to_all_copies(x_refs, o_refs, sems)
    for cp in sends:
        cp.wait_send()
    for cp in recvs:
        cp.wait_recv()
    for cp in local:
        cp.wait()


def _exchange_out_shape(xs, gather):
    return [jax.ShapeDtypeStruct(((N_DEV,) + x.shape) if gather else x.shape, x.dtype) for x in xs]


def _exchange_sems(n):
    return [pltpu.SemaphoreType.DMA((n, N_DEV - 1)), pltpu.SemaphoreType.DMA((n, N_DEV - 1)), pltpu.SemaphoreType.DMA((n,))]


SUBLANES = 8


def _store_row(ref, i, cols, row):
    base = pl.multiple_of((i // SUBLANES) * SUBLANES, SUBLANES)
    sub = lax.broadcasted_iota(jnp.int32, (SUBLANES, row.shape[1]), 0)
    ref[pl.ds(base, SUBLANES), cols] = jnp.where(sub == i % SUBLANES, row, ref[pl.ds(base, SUBLANES), cols])


def _tile_scan(a, b, reverse):
    sub = lax.broadcasted_iota(jnp.int32, a.shape, 0)
    for sh in (1, 2, 4):
        if reverse:
            live = sub < SUBLANES - sh
            a_s, b_s = pltpu.roll(a, SUBLANES - sh, 0), pltpu.roll(b, SUBLANES - sh, 0)
        else:
            live = sub >= sh
            a_s, b_s = pltpu.roll(a, sh, 0), pltpu.roll(b, sh, 0)
        b = jnp.where(live, a * b_s, 0.0) + b
        a = jnp.where(live, a * a_s, a)
    return a, b


def _lru_scan(a, b, name):
    t, w = a.shape

    def body(a_ref, b_ref, h_ref):
        def tile(j, h):
            rows = pl.ds(pl.multiple_of(j * SUBLANES, SUBLANES), SUBLANES)
            ca, cb = _tile_scan(a_ref[rows, :], b_ref[rows, :], False)
            out = ca * h + cb
            h_ref[rows, :] = out
            return out[SUBLANES - 1:SUBLANES]
        lax.fori_loop(0, t // SUBLANES, tile, jnp.zeros((1, w), F32))

    return pl.pallas_call(body, name=name, out_shape=jax.ShapeDtypeStruct((t, w), F32))(a, b)


def _lru_scan_bwd(a_next, h_prev, dhs, name):
    t, w = dhs.shape

    def body(a_ref, h_ref, dh_ref, da_ref, db_ref):
        def tile(n, lam):
            rows = pl.ds(pl.multiple_of((t // SUBLANES - 1 - n) * SUBLANES, SUBLANES), SUBLANES)
            ca, cb = _tile_scan(a_ref[rows, :], dh_ref[rows, :], True)
            out = ca * lam + cb
            db_ref[rows, :] = out
            da_ref[rows, :] = out * h_ref[rows, :]
            return out[0:1]
        lax.fori_loop(0, t // SUBLANES, tile, jnp.zeros((1, w), F32))

    return pl.pallas_call(body, name=name, out_shape=[jax.ShapeDtypeStruct((t, w), F32)] * 2)(a_next, h_prev, dhs)


N_PAIR = RW_W // LANES


def _bcast_cols(v, bd2, name):
    t = v.shape[0]

    def body(v_ref, bd_ref, o_ref):
        bdv = bd_ref[...]
        sub = lax.broadcasted_iota(jnp.int32, (HEAD, LANES), 0)
        own = lax.broadcasted_iota(jnp.int32, (HEAD, LANES), 1) % HEAD == sub
        for i in range(CHUNK):
            row = v_ref[i:i + 1, :]
            sums = _group_sums([jnp.where(own, row[:, p * LANES:(p + 1) * LANES], 0.0) for p in range(N_PAIR)], bdv)
            for p in range(N_PAIR):
                o_ref[i, p] = sums[p * HEAD:(p + 1) * HEAD]

    return pl.pallas_call(
        body, name=name, grid=(t // CHUNK,),
        in_specs=[pl.BlockSpec((CHUNK, RW_W), lambda i: (i, 0)), pl.BlockSpec((2 * LANES, LANES), lambda i: (0, 0))],
        out_specs=pl.BlockSpec((CHUNK, N_PAIR, HEAD, LANES), lambda i: (i, 0, 0, 0)),
        out_shape=jax.ShapeDtypeStruct((t, N_PAIR, HEAD, LANES), F32),
    )(v, bd2)


def _cols_to_rows(cols_ref, rows_ref):
    lane = lax.broadcasted_iota(jnp.int32, (CHUNK, LANES), 1)
    for p in range(N_PAIR):
        tile = cols_ref[p]
        sq = jnp.concatenate([tile, jnp.zeros_like(tile)], axis=0).T
        rows_ref[:, p * LANES:(p + 1) * LANES] = jnp.where(lane < HEAD, sq[0:CHUNK], pltpu.roll(sq[CHUNK:2 * CHUNK], HEAD, 1))


def _stacked_bf16(bd):
    return jnp.concatenate([bd, bd], axis=0).astype(BF16)


def _group_sums(prods, bd2):
    x = jnp.concatenate(prods, axis=0)
    hi = x.astype(BF16)
    lo = (x - hi.astype(F32)).astype(BF16)
    return jnp.dot(jnp.concatenate([hi, lo], axis=1), bd2, preferred_element_type=F32)


def _rw_scan(r, r_prev, w, k, c, c_next, b, vb, bd2, name, ride=None):
    t = w.shape[0]
    nch = t // CHUNK
    ride_xs, ride_gather = ride if ride is not None else ([], False)
    n_ride = len(ride_xs)

    def body(*refs):
        r_ref, rp_ref, w_ref, k_ref, c_ref, cn_ref, b_ref, vb_ref, bd_ref = refs[:9]
        x_refs = refs[9:9 + n_ride]
        spre_ref, y_ref = refs[9 + n_ride:11 + n_ride]
        o_refs = refs[11 + n_ride:11 + 2 * n_ride]
        s_ref, wc_ref, wr_ref, bc_ref, kc_ref, br_ref, kr_ref, yt_ref = refs[11 + 2 * n_ride:19 + 2 * n_ride]
        sems = refs[19 + 2 * n_ride:]

        @pl.when(pl.program_id(0) == 0)
        def _():
            s_ref[...] = jnp.zeros_like(s_ref)
            if n_ride:
                _exchange_start(x_refs, o_refs, sems, ride_gather)

        yt_ref[...] = jnp.zeros_like(yt_ref)
        bdv = bd_ref[...]
        lane = lax.broadcasted_iota(jnp.int32, (HEAD, LANES), 1) % CHUNK

        wv, cn, rv, bv, kv = w_ref[...], cn_ref[...], r_ref[...], b_ref[...], k_ref[...]
        wc_ref[...] = wv * cn
        wr_ref[...] = wv * rv
        for ref, x in ((bc_ref, bv * cn), (kc_ref, kv * cn), (br_ref, bv * rv), (kr_ref, kv * rv)):
            sums = _group_sums([x[:, q * LANES:(q + 1) * LANES] for q in range(N_PAIR)], bdv)
            for q in range(N_PAIR):
                ref[:, q * LANES:(q + 1) * LANES] = sums[q * CHUNK:(q + 1) * CHUNK]

        def cut(ref, i):
            x = ref[pl.ds(i, 1), :]
            return [x[:, p * LANES:(p + 1) * LANES] for p in range(N_PAIR)]

        def two_steps(j, st):
            i0 = 2 * j
            i1 = i0 + 1
            c0, wc0, rp0, wr0 = cut(c_ref, i0), cut(wc_ref, i0), cut(rp_ref, i0), cut(wr_ref, i0)
            w0, b0, k0, w1, b1, k1 = cut(w_ref, i0), cut(b_ref, i0), cut(k_ref, i0), cut(w_ref, i1), cut(b_ref, i1), cut(k_ref, i1)
            bc0, kc0, br0, kr0 = cut(bc_ref, i0), cut(kc_ref, i0), cut(br_ref, i0), cut(kr_ref, i0)
            pairs = range(N_PAIR)
            red = _group_sums([st[p] * c0[p] for p in pairs] + [st[p] * wc0[p] for p in pairs], bdv)
            out = _group_sums([st[p] * rp0[p] for p in pairs] + [st[p] * wr0[p] for p in pairs], bdv)
            new = []
            for p in pairs:
                v0, v1 = vb_ref[i0, p], vb_ref[i1, p]
                sa0 = red[p * HEAD:(p + 1) * HEAD]
                sa1 = red[(N_PAIR + p) * HEAD:(N_PAIR + p + 1) * HEAD] + sa0 * bc0[p] + v0 * kc0[p]
                y_before = out[p * HEAD:(p + 1) * HEAD]
                y0 = out[(N_PAIR + p) * HEAD:(N_PAIR + p + 1) * HEAD] + sa0 * br0[p] + v0 * kr0[p]
                spre_ref[i0, p] = st[p]
                s1 = st[p] * w0[p] + sa0 * b0[p] + v0 * k0[p]
                spre_ref[i1, p] = s1
                new.append(s1 * w1[p] + sa1 * b1[p] + v1 * k1[p])
                yt_ref[p] = jnp.where(lane == i0, y_before, jnp.where(lane == i1, y0, yt_ref[p]))
            return tuple(new)

        st = lax.fori_loop(0, CHUNK // 2, two_steps, tuple(s_ref[p] for p in range(N_PAIR)))
        for p in range(N_PAIR):
            s_ref[p] = st[p]
        _cols_to_rows(yt_ref, y_ref)

        if n_ride:
            @pl.when(pl.program_id(0) == nch - 1)
            def _():
                _exchange_wait(x_refs, o_refs, sems, ride_gather)

    row = pl.BlockSpec((CHUNK, RW_W), lambda i: (i, 0))
    big = pl.BlockSpec((CHUNK, N_PAIR, HEAD, LANES), lambda i: (i, 0, 0, 0))
    any_spec = pl.BlockSpec(memory_space=pl.ANY)
    return pl.pallas_call(
        body, name=name, grid=(nch,),
        in_specs=[row] * 7 + [big, pl.BlockSpec((2 * LANES, LANES), lambda i: (0, 0))] + [any_spec] * n_ride,
        out_specs=[big, row] + [any_spec] * n_ride,
        out_shape=[jax.ShapeDtypeStruct((t, N_PAIR, HEAD, LANES), F32), jax.ShapeDtypeStruct((t, RW_W), F32)]
        + _exchange_out_shape(ride_xs, ride_gather),
        scratch_shapes=[pltpu.VMEM((N_PAIR, HEAD, LANES), F32)] + [pltpu.VMEM((CHUNK, RW_W), F32)] * 6
        + [pltpu.VMEM((N_PAIR, HEAD, LANES), F32)] + (_exchange_sems(n_ride) if n_ride else []),
    )(r, r_prev, w, k, c, c_next, b, vb, bd2, *ride_xs)


def _rw_scan_bwd(r, w, k, c, b, vb, dyb, spre, bd, name, ride=None):
    t = r.shape[0]
    nch = t // CHUNK
    ride_xs, ride_gather = ride if ride is not None else ([], False)
    n_ride = len(ride_xs)
    n_pre = 8

    def body(*refs):
        r_ref, w_ref, k_ref, c_ref, b_ref, vb_ref, dyb_ref, spre_ref, bd_ref = refs[:9]
        x_refs = refs[9:9 + n_ride]
        dr_ref, dw_ref, dk_ref, dc_ref, db_ref, dv_ref = refs[9 + n_ride:15 + n_ride]
        o_refs = refs[15 + n_ride:15 + 2 * n_ride]
        g_ref, snext_ref, dvt_ref = refs[15 + 2 * n_ride:18 + 2 * n_ride]
        pre = refs[18 + 2 * n_ride:18 + 2 * n_ride + n_pre]
        sems = refs[18 + 2 * n_ride + n_pre:]
        wb_ref, wk_ref, rb_ref, rk_ref, rwb_ref, cb_ref, rwk_ref, ck_ref = pre[:8]

        @pl.when(pl.program_id(0) == 0)
        def _():
            g_ref[...] = jnp.zeros_like(g_ref)
            snext_ref[...] = jnp.zeros_like(snext_ref)
            if n_ride:
                _exchange_start(x_refs, o_refs, sems, ride_gather)

        for ref in (dr_ref, dw_ref, dk_ref, dc_ref, db_ref, dvt_ref):
            ref[...] = jnp.zeros_like(ref)
        bdv = bd_ref[...]
        bd1 = bdv[0:LANES]
        lane = lax.broadcasted_iota(jnp.int32, (HEAD, LANES), 1) % CHUNK

        rv, wv, kv, cv, bv = r_ref[...], w_ref[...], k_ref[...], c_ref[...], b_ref[...]
        b_b, k_b = pltpu.roll(bv, 1, 0), pltpu.roll(kv, 1, 0)
        wb_ref[...] = wv * b_b
        wk_ref[...] = wv * k_b
        rw = rv * wv
        for ref, x in ((rb_ref, rv * bv), (rk_ref, rv * kv), (rwb_ref, rw * b_b), (cb_ref, cv * b_b), (rwk_ref, rw * k_b),
                       (ck_ref, cv * k_b)):
            sums = _group_sums([x[:, q * LANES:(q + 1) * LANES] for q in range(N_PAIR)], bdv)
            for q in range(N_PAIR):
                ref[:, q * LANES:(q + 1) * LANES] = sums[q * CHUNK:(q + 1) * CHUNK]

        def sum0(x):
            return jnp.sum(x, axis=0, keepdims=True)

        def cut(ref, i):
            x = ref[pl.ds(i, 1), :]
            return [x[:, p * LANES:(p + 1) * LANES] for p in range(N_PAIR)]

        def sums_bf16(prods):
            return jnp.dot(jnp.concatenate(prods, axis=0).astype(BF16), bd1, preferred_element_type=F32)

        def two_steps(n, gs):
            ia = CHUNK - 1 - 2 * n
            ib = ia - 1
            r_a, w_a, k_a, c_a, b_a = [cut(ref, ia) for ref in (r_ref, w_ref, k_ref, c_ref, b_ref)]
            r_b, w_b, c_b = [cut(ref, ib) for ref in (r_ref, w_ref, c_ref)]
            wb, wk, rb_a, rk_a, rwb, cb, rwk, ck = [cut(ref, ia) for ref in pre[:8]]
            rb_b, rk_b = cut(rb_ref, ib), cut(rk_ref, ib)
            pairs = range(N_PAIR)
            sp_a = [spre_ref[ia, p] for p in pairs]
            sp_b = [spre_ref[ib, p] for p in pairs]
            dy_a = [dyb_ref[ia, p] for p in pairs]
            dy_b = [dyb_ref[ib, p] for p in pairs]
            chain = _group_sums([gs[p] * b_a[p] for p in pairs] + [gs[p] * wb[p] for p in pairs], bdv)
            off = sums_bf16([gs[p] * k_a[p] for p in pairs] + [gs[p] * wk[p] for p in pairs]
                            + [sp_a[p] * c_a[p] for p in pairs] + [sp_b[p] * c_b[p] for p in pairs])
            new = []
            for p in pairs:
                def part(x, q, p=p):
                    return x[(q * N_PAIR + p) * HEAD:(q * N_PAIR + p + 1) * HEAD]
                dsa_a = part(chain, 0) + dy_a[p] * rb_a[p]
                dv_a = part(off, 0) + dy_a[p] * rk_a[p]
                dsa_b = part(chain, 1) + dy_a[p] * rwb[p] + dsa_a * cb[p] + dy_b[p] * rb_b[p]
                dv_b = part(off, 1) + dy_a[p] * rwk[p] + dsa_a * ck[p] + dy_b[p] * rk_b[p]
                sa_a, sa_b = part(off, 2), part(off, 3)
                g_a = gs[p] + dy_a[p] * r_a[p]
                g_mid = g_a * w_a[p] + dsa_a * c_a[p]
                g_b = g_mid + dy_b[p] * r_b[p]
                new.append(g_b * w_b[p] + dsa_b * c_b[p])
                cols = pl.ds(p * LANES, LANES)
                for i, dy, s_post, s_pre, g, sa, dsa in ((ia, dy_a[p], snext_ref[p], sp_a[p], g_a, sa_a, dsa_a),
                                                         (ib, dy_b[p], sp_a[p], sp_b[p], g_b, sa_b, dsa_b)):
                    _store_row(dr_ref, i, cols, sum0(s_post * dy))
                    _store_row(dw_ref, i, cols, sum0(g * s_pre))
                    _store_row(db_ref, i, cols, sum0(g * sa))
                    _store_row(dk_ref, i, cols, sum0(g * vb_ref[i, p]))
                    _store_row(dc_ref, i, cols, sum0(s_pre * dsa))
                dvt_ref[p] = jnp.where(lane == ia, dv_a, jnp.where(lane == ib, dv_b, dvt_ref[p]))
                snext_ref[p] = sp_b[p]
            return tuple(new)

        gs = lax.fori_loop(0, CHUNK // 2, two_steps, tuple(g_ref[p] for p in range(N_PAIR)))
        for p in range(N_PAIR):
            g_ref[p] = gs[p]
        _cols_to_rows(dvt_ref, dv_ref)

        if n_ride:
            @pl.when(pl.program_id(0) == nch - 1)
            def _():
                _exchange_wait(x_refs, o_refs, sems, ride_gather)

    row = pl.BlockSpec((CHUNK, RW_W), lambda i: (nch - 1 - i, 0))
    big = pl.BlockSpec((CHUNK, N_PAIR, HEAD, LANES), lambda i: (nch - 1 - i, 0, 0, 0))
    any_spec = pl.BlockSpec(memory_space=pl.ANY)
    return pl.pallas_call(
        body, name=name, grid=(nch,),
        in_specs=[row] * 5 + [big, big, big, pl.BlockSpec((2 * LANES, LANES), lambda i: (0, 0))] + [any_spec] * n_ride,
        out_specs=[row] * 6 + [any_spec] * n_ride,
        out_shape=[jax.ShapeDtypeStruct((t, RW_W), F32)] * 6 + _exchange_out_shape(ride_xs, ride_gather),
        scratch_shapes=[pltpu.VMEM((N_PAIR, HEAD, LANES), F32)] * 3
        + [pltpu.VMEM((CHUNK, RW_W), F32)] * n_pre + (_exchange_sems(n_ride) if n_ride else []),
    )(r, w, k, c, b, vb, dyb, spre, bd, *ride_xs)


def _exchange(xs, *, gather, name, gather_too=()):
    n, n2 = len(xs), len(gather_too)

    def body(*refs):
        x1, x2 = refs[:n], refs[n:n + n2]
        o1, o2 = refs[n + n2:2 * n + n2], refs[2 * n + n2:2 * (n + n2)]
        sems1, sems2 = refs[2 * (n + n2):2 * (n + n2) + 3], refs[2 * (n + n2) + 3:]
        _exchange_start(x1, o1, sems1, gather)
        if n2:
            _exchange_start(x2, o2, sems2, True)
            _exchange_wait(x2, o2, sems2, True)
        _exchange_wait(x1, o1, sems1, gather)

    any_spec = pl.BlockSpec(memory_space=pl.ANY)
    return pl.pallas_call(
        body, name=name, in_specs=[any_spec] * (n + n2), out_specs=[any_spec] * (n + n2),
        out_shape=_exchange_out_shape(xs, gather) + _exchange_out_shape(list(gather_too), True),
        scratch_shapes=_exchange_sems(n) + (_exchange_sems(n2) if n2 else []),
    )(*xs, *gather_too)


def _adamw_rows(g, w, m, v):
    m = ADAM_B1 * m + (1.0 - ADAM_B1) * g
    v = ADAM_B2 * v + (1.0 - ADAM_B2) * (g * g)
    m_hat = m / (1.0 - ADAM_B1 ** ADAM_STEP)
    v_hat = v / (1.0 - ADAM_B2 ** ADAM_STEP)
    return -ADAM_LR * (m_hat / (jnp.sqrt(v_hat) + ADAM_EPS) + ADAM_WD * w), m, v


def _sum_slots(parts):
    g = parts[0].astype(F32)
    for q in range(1, N_DEV):
        g = g + parts[q].astype(F32)
    return g


def _reduce_adamw(parts, w, m, v, name):
    rows, cols = w.shape

    def fn(parts, w, m, v):
        g = _sum_slots(parts)
        return (g,) + _adamw_rows(g, w, m, v)

    return _rowwise(fn, [parts, w, m, v], [], [(cols, F32)] * 4, [], tm=_pick(rows, (256, 128, 64, 32, 16, 8)), name=name)


def _shift(x, n):
    return jnp.pad(x, ((n, 0), (0, 0)))[:-n]


def _unshift(x, n):
    return jnp.pad(x, ((0, n), (0, 0)))[n:]


def _add_n(xs, *, tm, name):
    def fn(*vals):
        s = vals[0]
        for x in vals[1:]:
            s = s + x
        return s
    return _rowwise(fn, xs, [], [(xs[0].shape[1], F32)], [], tm=tm, name=name)[0]


def _norm_fwd(h, g, *, tm, name):
    return _rowwise(lambda x, gg: _rms(x, gg), [h], [g], [(h.shape[1], BF16)], [], tm=tm, name=name)[0]


def _res_norm_fwd(h, f, g, scale, *, tm, name):
    return _rowwise(lambda hh, ff, gg: hh + scale * _rms(ff, gg), [h, f], [g], [(h.shape[1], F32)], [], tm=tm, name=name)[0]


def _norm_bwd(x, g, dy, scale, res, out_dtype, *, tm, name):
    if res is None:
        def fn(xx, dd, gg):
            dx, dg = _rms_bwd(xx, gg, dd * scale)
            return dx, dg
        rows = [x, dy]
    else:
        def fn(xx, dd, rr, gg):
            dx, dg = _rms_bwd(xx, gg, dd * scale)
            return dx + rr, dg
        rows = [x, dy, res]
    return _rowwise(fn, rows, [g], [(x.shape[1], out_dtype)], [(1, x.shape[1])], tm=tm, name=name)


def _ffn_fwd(h, g_pre, g_post, w24, wo4, tiles, tag):
    tb, ts = tiles
    a = _norm_fwd(h, g_pre, tm=ts, name=f"{tag}_norm")
    gu, s4 = _ffn_in(a, w24, tm=tb, name=f"{tag}_in")
    f = _mm(s4, wo4, trans_b=False, tm=tb, tn=D_MODEL, out_dtype=F32, name=f"{tag}_out")
    h_new = _res_norm_fwd(h, f, g_post, 0.5, tm=ts, name=f"{tag}_res")
    return h_new, (h, a, gu, s4, f)


def _ffn_bwd(dh_new, res, g_pre, g_post, w24, wo4, tiles, tag):
    tb, ts = tiles
    h, a, gu, s4, f = res
    t = h.shape[0]
    df, dg_post = _norm_bwd(f, g_post, dh_new, 0.5, None, BF16, tm=ts, name=f"{tag}_dres")
    dgu = _ffn_dswiglu(df, wo4, gu, tm=tb, name=f"{tag}_dswiglu")
    d_wo = _mm_tn(s4, df[None], tk=FFN_BLK, name=f"{tag}_dwout")
    dgu8 = dgu.reshape(2 * w24.shape[1], t, FFN_BLK)
    w8 = w24.reshape(2 * w24.shape[1], D_MODEL, FFN_BLK)
    da = _mm(dgu8, w8, trans_b=True, tm=tb, tn=D_MODEL // 2, out_dtype=F32, name=f"{tag}_da")
    d_win = _mm_tn(a[None], dgu8, tk=D_MODEL, name=f"{tag}_dwin")
    dh, dg_pre = _norm_bwd(h, g_pre, da, 1.0, dh_new, F32, tm=ts, name=f"{tag}_dnorm")
    return dh, dg_pre, dg_post, d_win, d_wo.reshape(N_DEV, -1, D_MODEL)


def _blockdiag(w4):
    n, b, _ = w4.shape
    eye = jnp.eye(n, dtype=w4.dtype)
    return (eye[:, None, :, None] * w4[:, :, None, :]).reshape(n * b, n * b)


def _blockdiag_grad(d):
    n = d.shape[0] // HEAD
    x = d.reshape(n, HEAD, n, HEAD)
    return jnp.stack([x[i, :, i, :] for i in range(n)])


def _row(v):
    return v.reshape(1, -1)


def _mixer_fwd(h, g_pre, g_post, wi, wo, P, bd, tiles, tag, ride=None):
    tb, ts = tiles
    a = _norm_fwd(h, g_pre, tm=ts, name=f"{tag}_norm")
    p = _mm(a[None], wi[None], trans_b=False, tm=tb, tn=N_IN, out_dtype=F32, name=f"{tag}_in")
    lx, lg = p[:, 0:256], p[:, 256:512]
    sb, scc, sx = p[:, 512:768], p[:, 768:1024], p[:, 1024:1280]
    z = p[:, 1280:]
    lxs = [lx, _shift(lx, 1), _shift(lx, 2), _shift(lx, 3)]
    cw = [_row(P['lru_conv_w'][kk]) for kk in range(4)]
    lru_par = cw + [_row(P['lru_conv_b']), _blockdiag(P['lru_wa']), _row(P['lru_ba']), _blockdiag(P['lru_wx']),
                    _row(P['lru_bx']), _row(P['lru_lambda'])]
    la, lb = _rowwise(_lru_pre, lxs, lru_par, [(LRU_W, F32)] * 2, [], tm=ts, name=f"{tag}_lru_pre")
    hs = _lru_scan(la, lb, name=f"{tag}_lru_scan")
    y_lru = _rowwise(lambda gg, hh, ng, b_: _lru_post(b_, gg, hh, ng), [lg, hs], [_row(P['lru_norm_g']), bd],
                     [(LRU_W, F32)], [], tm=ts, name=f"{tag}_lru_post")[0]
    sc_rows = [sb, scc, sx, _shift(scc, 1), _shift(sx, 1), _shift(scc, 2), _shift(sx, 2)]
    sc_par = [_row(P['sc_conv_w'][kk]) for kk in range(3)] + [_row(P['sc_norm_g'])]
    y_sc = _rowwise(lambda *v: _sc_fwd(v[-1], *v[:-1]), sc_rows, sc_par + [bd], [(SC_W, F32)], [], tm=ts,
                    name=f"{tag}_sc")[0]
    cuts = (0, RW_W, 2 * RW_W, 3 * RW_W, RW_IN)
    zs = [z[:, cuts[q]:cuts[q + 1]] for q in range(4)]
    z_rows = zs + [_shift(q, 1) for q in zs]
    pad = lambda m, lo: jnp.pad(m, ((lo, LANES - lo - m.shape[0]), (0, 0)))
    rw_par = [_row(P['rwkv_mu'][cuts[q]:cuts[q + 1]]) for q in range(4)]
    rw_par += [_row(P['rwkv_w0']), pad(P['rwkv_w2'], 0), _row(P['rwkv_a0']), pad(P['rwkv_a2'], 32),
               pad(P['rwkv_g2'], 64), _row(P['rwkv_k_k']), _row(P['rwkv_k_a'])]
    r, w, k2, c, b, v, g = _rowwise(lambda *vv: _rw_pre(vv[-1], *vv[:-1]), z_rows, rw_par + [bd], [(RW_W, F32)] * 7, [],
                                    tm=ts, name=f"{tag}_rw_pre")
    vb = _bcast_cols(v, _stacked_bf16(bd), name=f"{tag}_rw_vcols")
    spre, yt, *rode = _rw_scan(r, _shift(r, 1), w, k2, c, _unshift(c, 1), b, vb, _stacked_bf16(bd), name=f"{tag}_rw_scan",
                               ride=ride)
    y = _unshift(yt, 1)
    post_par =[_row(P['rwkv_lnx_w']), _row(P['rwkv_lnx_b']), _row(P['rwkv_r_k'])]
    y_rw = _rowwise(lambda *vv: _rw_post(vv[-1], *vv[:-1]), [y, r, k2, v, g], post_par + [bd], [(RW_W, F32)], [], tm=ts,
                    name=f"{tag}_rw_post")[0]
    ycat = jnp.concatenate([y_lru, y_sc, y_rw], axis=1).astype(BF16)
    m = _mm(ycat[None], wo[None], trans_b=False, tm=tb, tn=D_MODEL, out_dtype=F32, name=f"{tag}_out")
    h_new = _res_norm_fwd(h, m, g_post, 1.0, tm=ts, name=f"{tag}_res")
    res = dict(h=h, a=a, m=m, ycat=ycat, lxs=lxs, lru_par=lru_par, lg=lg, la=la, hs=hs, sc_rows=sc_rows, sc_par=sc_par,
               z_rows=z_rows, rw_par=rw_par, r=r, w=w, k2=k2, c=c, b=b, v=v, g=g, vb=vb, spre=spre, y=y, post_par=post_par)
    return h_new, res, rode


def _mixer_bwd(dh_new, R, g_pre, g_post, wi, wo, P, bd, tiles, tag, ride=None):
    tb, ts = tiles
    dm, dg_post = _norm_bwd(R['m'], g_post, dh_new, 1.0, None, BF16, tm=ts, name=f"{tag}_dres")
    dycat = _mm(dm[None], wo[None], trans_b=True, tm=tb, tn=D_MODEL, out_dtype=F32, name=f"{tag}_dycat")
    d_wo = _mm_tn(R['ycat'][None], dm[None], tk=D_MODEL // 2, name=f"{tag}_dwout")[0]
    dy_lru, dy_sc, dy_rw = dycat[:, 0:256], dycat[:, 256:512], dycat[:, 512:]
    G = {}
    d_lg, d_hs, G['lru_norm_g'] = _rowwise(
        lambda gg, hh, ct, ng, b_: _vjp_rows(_lru_post, 1, 3, 1)(b_, gg, hh, ng, ct),
        [R['lg'], R['hs'], dy_lru], [_row(P['lru_norm_g']), bd], [(LRU_W, F32)] * 2, [(1, LRU_W)], tm=ts,
        name=f"{tag}_lru_dpost")
    d_la, d_lb = _lru_scan_bwd(_unshift(R['la'], 1), _shift(R['hs'], 1), d_hs, name=f"{tag}_lru_dscan")

    def lru_pre_bwd(x0, x1, x2, x3, ca, cb_, *par):
        return _vjp_rows(_lru_pre, 0, 14, 2)(x0, x1, x2, x3, *par, ca, cb_)

    par_shapes = [tuple(q.shape) for q in R['lru_par']]
    outs = _rowwise(lru_pre_bwd, R['lxs'] + [d_la, d_lb], R['lru_par'], [(LRU_W, F32)] * 4, par_shapes, tm=ts,
                    name=f"{tag}_lru_dpre")
    dxs, dpar = outs[:4], outs[4:]
    d_lx = _add_n([dxs[0], _unshift(dxs[1], 1), _unshift(dxs[2], 2), _unshift(dxs[3], 3)], tm=ts, name=f"{tag}_lru_dx")
    G['lru_conv_w'] = jnp.concatenate(dpar[0:4], axis=0)
    G['lru_conv_b'] = dpar[4][0]
    G['lru_wa'] = _blockdiag_grad(dpar[5])
    G['lru_ba'] = dpar[6][0]
    G['lru_wx'] = _blockdiag_grad(dpar[7])
    G['lru_bx'] = dpar[8][0]
    G['lru_lambda'] = dpar[9][0]
    G['lru_norm_g'] = G['lru_norm_g'][0]

    def sc_bwd(*vv):
        rows7, ct, par4, b_ = vv[:7], vv[7], vv[8:12], vv[12]
        return _vjp_rows(_sc_fwd, 1, 11, 1)(b_, *rows7, *par4, ct)

    outs = _rowwise(sc_bwd, R['sc_rows'] + [dy_sc], R['sc_par'] + [bd], [(SC_W, F32)] * 7, [(1, SC_W)] * 4, tm=ts,
                    name=f"{tag}_sc_bwd")
    d_sb = outs[0]
    d_sc = _add_n([outs[1], _unshift(outs[3], 1), _unshift(outs[5], 2)], tm=ts, name=f"{tag}_sc_dc")
    d_sx = _add_n([outs[2], _unshift(outs[4], 1), _unshift(outs[6], 2)], tm=ts, name=f"{tag}_sc_dx")
    G['sc_conv_w'] = jnp.concatenate(outs[7:10], axis=0)
    G['sc_norm_g'] = outs[10][0]

    def rw_post_bwd(*vv):
        rows5, ct, par3, b_ = vv[:5], vv[5], vv[6:9], vv[9]
        return _vjp_rows(_rw_post, 1, 8, 1)(b_, *rows5, *par3, ct)

    outs = _rowwise(rw_post_bwd, [R['y'], R['r'], R['k2'], R['v'], R['g'], dy_rw], R['post_par'] + [bd],
                    [(RW_W, F32)] * 5, [(1, RW_W)] * 3, tm=ts, name=f"{tag}_rw_dpost")
    d_y, dr_p, dk_p, dv_p, d_g = outs[:5]
    G['rwkv_lnx_w'], G['rwkv_lnx_b'], G['rwkv_r_k'] = outs[5][0], outs[6][0], outs[7][0]
    dyb = _bcast_cols(d_y, _stacked_bf16(bd), name=f"{tag}_rw_dycols")
    dr_s, d_w, dk_s, d_c, d_b, dvt, *rode = _rw_scan_bwd(R['r'], R['w'], R['k2'], R['c'], R['b'], R['vb'], dyb, R['spre'],
                                                        _stacked_bf16(bd), name=f"{tag}_rw_dscan", ride=ride)
    dv_s = dvt

    def rw_pre_bwd(*vv):
        zrows = vv[0:8]
        dr1, dr2, dw_, dk1, dk2_, dc_, db_, dv1, dv2, dg_ = vv[8:18]
        par, b_ = vv[18:29], vv[29]
        return _vjp_rows(_rw_pre, 1, 19, 7)(b_, *zrows, *par, dr1 + dr2, dw_, dk1 + dk2_, dc_, db_, dv1 + dv2, dg_)

    par_shapes = [tuple(q.shape) for q in R['rw_par']]
    widths = [(q.shape[1], F32) for q in R['z_rows']]
    outs = _rowwise(rw_pre_bwd, R['z_rows'] + [dr_p, dr_s, d_w, dk_p, dk_s, d_c, d_b, dv_p, dv_s, d_g],
                    R['rw_par'] + [bd], widths, par_shapes, tm=ts, name=f"{tag}_rw_dpre")
    d_z = _add_n([jnp.concatenate(outs[0:4], axis=1), _unshift(jnp.concatenate(outs[4:8], axis=1), 1)], tm=ts,
                 name=f"{tag}_rw_dz")
    dpar = outs[8:]
    G['rwkv_mu'] = jnp.concatenate([q[0] for q in dpar[0:4]])
    G['rwkv_w0'], G['rwkv_a0'] = dpar[4][0], dpar[6][0]
    G['rwkv_w2'], G['rwkv_a2'], G['rwkv_g2'] = dpar[5][0:32], dpar[7][32:64], dpar[8][64:128]
    G['rwkv_k_k'], G['rwkv_k_a'] = dpar[9][0], dpar[10][0]

    dp = jnp.concatenate([d_lx, d_lg, d_sb, d_sc, d_sx, d_z], axis=1).astype(BF16)
    da = _mm(dp[None], wi[None], trans_b=True, tm=tb, tn=D_MODEL, out_dtype=F32, name=f"{tag}_da")
    d_wi = _mm_tn(R['a'][None], dp[None], tk=D_MODEL // 2, name=f"{tag}_dwin")[0]
    dh, dg_pre = _norm_bwd(R['h'], g_pre, da, 1.0, dh_new, F32, tm=ts, name=f"{tag}_dnorm")
    return dh, dg_pre, dg_post, d_wi, d_wo, G, rode


def _loss_rows(h, tgt, n_seq, *, tm, name):
    d = h.shape[1]

    def body(h_ref, t_ref, dh_ref, l_ref):
        i = pl.program_id(0)
        row = lax.broadcasted_iota(jnp.int32, (tm, 1), 0) + i * tm
        live = (row >= N_META) & (row < N_META + n_seq)
        e = jnp.where(live, h_ref[...] - t_ref[...], 0.0)
        dh_ref[...] = e * (1.0 / d)
        part = 0.5 * jnp.sum(jnp.sum(e * e, axis=1, keepdims=True) * (1.0 / d), axis=0, keepdims=True)

        @pl.when(i == 0)
        def _():
            l_ref[...] = part

        @pl.when(i > 0)
        def _():
            l_ref[...] += part

    blk = pl.BlockSpec((tm, d), lambda i: (i, 0))
    return pl.pallas_call(body, name=name, grid=(h.shape[0] // tm,), in_specs=[blk, blk],
                          out_specs=[blk, pl.BlockSpec((1, 1), lambda i: (0, 0))],
                          out_shape=[jax.ShapeDtypeStruct(h.shape, F32), jax.ShapeDtypeStruct((1, 1), F32)])(h, tgt)


def _pack(arrs, mult):
    flat = jnp.concatenate([a.reshape(-1).astype(F32) for a in arrs])
    n = flat.shape[0]
    tot = -(-n // mult) * mult
    return jnp.pad(flat, (0, tot - n)).reshape(-1, LANES)


def _unpack(buf, shapes):
    flat = buf.reshape(-1)
    out, off = [], 0
    for s in shapes:
        n = 1
        for q in s:
            n *= q
        out.append(flat[off:off + n].reshape(s))
        off += n
    return out


def _step(W, M, V, x, loss_target):
    n_seq = x.shape[1]
    t_real = N_META + n_seq
    t = (t_real // CHUNK + 1) * CHUNK
    tiles = (_pick(t, (704, 512, 256, 128, 64)), _pick(t, (192, 128, 64)))
    me = 4 * lax.axis_index("x") + 2 * lax.axis_index("y") + lax.axis_index("c")
    n_layer = W['norm_g'].shape[0]

    small_sh = list(SMALL_SHARDED)
    packed = _pack([W[n] for n in small_sh], 8 * LANES)
    early = ['ffn1_w_in', 'ffn1_w_out', 'mix_w_in', 'mix_w_out']
    late = ['ffn2_w_in', 'ffn2_w_out']
    gathered = _exchange([W[n][0].astype(BF16) for n in early] + [packed], gather=True, name="gather_weights")
    big8 = [dict(zip(early, gathered[:-1]))] + [{} for _ in range(n_layer - 1)]
    pieces = [_unpack(gathered[-1][q], [W[n].shape for n in small_sh]) for q in range(N_DEV)]
    full = {n: W[n] for n in SMALL if n not in SMALL_SHARDED}
    for idx, n in enumerate(small_sh):
        full[n] = jnp.concatenate([pieces[q][idx] for q in range(N_DEV)], axis=SMALL_SHARDED[n])

    def ffn_weights(l, which):
        w24 = big8[l][f'{which}_w_in'].reshape(2, N_DEV // 2, D_MODEL, FFN_BLK)
        wo4 = big8[l][f'{which}_w_out'].reshape(N_DEV // 2, FFN_BLK, D_MODEL)
        return w24, wo4

    def mixer_weights(l):
        return big8[l]['mix_w_in'].transpose(1, 0, 2).reshape(D_MODEL, N_IN), big8[l]['mix_w_out'].reshape(D_MODEL, D_MODEL)

    bd = jnp.kron(jnp.eye(LANES // HEAD, dtype=F32), jnp.ones((HEAD, HEAD), F32))
    small_layer = [n for n in SMALL if n not in ('meta_tokens', 'norm_g')]

    h = jnp.concatenate([full['meta_tokens'], x[0], jnp.zeros((t - t_real, D_MODEL), F32)], axis=0)
    saved = []
    for l in range(n_layer):
        ng = [_row(full['norm_g'][l, q]) for q in range(6)]
        P = {n: full[n][l] for n in small_layer}
        w1 = ffn_weights(l, 'ffn1')
        h, r1 = _ffn_fwd(h, ng[0], ng[1], w1[0], w1[1], tiles, f"l{l}_ffn1")
        riders = [(l, n) for n in late] + ([(l + 1, n) for n in early] if l + 1 < n_layer else [])
        wm = mixer_weights(l)
        h, r2, rode = _mixer_fwd(h, ng[2], ng[3], wm[0], wm[1], P, bd, tiles, f"l{l}_mix",
                                 ride=([W[n][q].astype(BF16) for q, n in riders], True))
        for (q, n), arrived in zip(riders, rode):
            big8[q][n] = arrived
        w2 = ffn_weights(l, 'ffn2')
        h, r3 = _ffn_fwd(h, ng[4], ng[5], w2[0], w2[1], tiles, f"l{l}_ffn2")
        saved.append(((w1[0], w1[1], w2[0], w2[1], wm[0], wm[1]), ng, P, r1, r2, r3))

    tgt = jnp.pad(loss_target[0], ((N_META, t - t_real), (0, 0)))
    dh, loss_part = _loss_rows(h, tgt, n_seq, tm=tiles[1], name="loss")
    loss = lax.psum(loss_part[0, 0], MESH_AXES)

    small_grads = [None] * n_layer
    norm_grads = [None] * n_layer
    recv = [{} for _ in range(n_layer)]
    outgoing = []
    for l in reversed(range(n_layer)):
        lw, ng, P, r1, r2, r3 = saved[l]
        dh, g4, g5, d_win2, d_wo2 = _ffn_bwd(dh, r3, ng[4], ng[5], lw[2], lw[3], tiles, f"l{l}_ffn2")
        outgoing += [((l, 'ffn2_w_in'), d_win2), ((l, 'ffn2_w_out'), d_wo2)]
        dh, g2, g3, d_wi, d_wo, G, rode = _mixer_bwd(dh, r2, ng[2], ng[3], lw[4], lw[5], P, bd, tiles, f"l{l}_mix",
                                                     ride=([a for _, a in outgoing], False))
        for ((q, n), _), arrived in zip(outgoing, rode):
            recv[q][n] = arrived
        dh, g0, g1, d_win1, d_wo1 = _ffn_bwd(dh, r1, ng[0], ng[1], lw[0], lw[1], tiles, f"l{l}_ffn1")
        small_grads[l] = G
        norm_grads[l] = jnp.concatenate([g0, g1, g2, g3, g4, g5], axis=0)
        d_wi8 = d_wi.reshape(D_MODEL, N_DEV, N_IN // N_DEV).transpose(1, 0, 2)
        d_wo8 = d_wo.reshape(N_DEV, D_MODEL // N_DEV, D_MODEL)
        outgoing = [((l, 'ffn1_w_in'), d_win1), ((l, 'ffn1_w_out'), d_wo1), ((l, 'mix_w_in'), d_wi8), ((l, 'mix_w_out'), d_wo8)]
    gs = {n: jnp.stack([small_grads[l][n] for l in range(n_layer)]) for n in small_layer}
    gs['norm_g'] = jnp.stack(norm_grads)
    gs['meta_tokens'] = dh[:N_META]
    gpack = _pack([gs[n] for n in SMALL], 8 * LANES)
    *last, gall = _exchange([a for _, a in outgoing], gather=False, name="last_grad_exchange", gather_too=[gpack])
    for ((q, n), _), arrived in zip(outgoing, last):
        recv[q][n] = arrived
    gsum =_rowwise(lambda parts: _sum_slots(parts), [gall], [], [(LANES, F32)], [], tm=gall.shape[1],
                    name="sum_small_grads")[0]
    gfull = dict(zip(SMALL, _unpack(gsum, [gs[n].shape for n in SMALL])))

    def my_shard(n, a):
        if n not in SMALL_SHARDED:
            return a
        ax = SMALL_SHARDED[n]
        size = a.shape[ax] // N_DEV
        return lax.dynamic_slice_in_dim(a, me * size, size, axis=ax)

    g_loc = [my_shard(n, gfull[n]) for n in SMALL]
    shapes = [W[n].shape for n in SMALL]
    bufs = [_pack(g_loc, 8 * LANES)] + [_pack([D[n] for n in SMALL], 8 * LANES) for D in (W, M, V)]
    d_s, m_s, v_s = _rowwise(_adamw_rows, bufs, [], [(LANES, F32)] * 3, [], tm=bufs[0].shape[0], name="adamw_small")
    out = {'grad': dict(zip(SMALL, g_loc)), 'delta': dict(zip(SMALL, _unpack(d_s, shapes))),
           'm': dict(zip(SMALL, _unpack(m_s, shapes))), 'v': dict(zip(SMALL, _unpack(v_s, shapes)))}

    order = ['ffn1_w_in', 'ffn1_w_out', 'ffn2_w_in', 'ffn2_w_out', 'mix_w_in', 'mix_w_out']
    for idx, n in enumerate(order):
        per_layer = []
        for l in range(n_layer):
            parts = recv[l][n]
            rows, cols = W[n].shape[1], W[n].shape[2]
            per_layer.append(_reduce_adamw(parts.reshape(N_DEV, rows, cols), W[n][l], M[n][l], V[n][l],
                                           name=f"l{l}_adamw_{n}"))
        for q, key in enumerate(('grad', 'delta', 'm', 'v')):
            out[key][n] = jnp.stack([per_layer[l][q] for l in range(n_layer)])

    return (loss, dh[N_META:t_real][None],
            *[out['grad'][n] for n in WEIGHTS], *[out['delta'][n] for n in WEIGHTS],
            *[out['m'][n] for n in WEIGHTS], *[out['v'][n] for n in WEIGHTS])


def kernel(x, meta_tokens, norm_g, ffn1_w_in, ffn1_w_out, ffn2_w_in, ffn2_w_out, mix_w_in, mix_w_out, lru_conv_w, lru_conv_b, lru_wa, lru_ba, lru_wx, lru_bx, lru_lambda, lru_norm_g, sc_conv_w, sc_norm_g, rwkv_mu, rwkv_w0, rwkv_w2, rwkv_a0, rwkv_a2, rwkv_g2, rwkv_k_k, rwkv_k_a, rwkv_r_k, rwkv_lnx_w, rwkv_lnx_b, loss_target, m_meta_tokens, m_norm_g, m_ffn1_w_in, m_ffn1_w_out, m_ffn2_w_in, m_ffn2_w_out, m_mix_w_in, m_mix_w_out, m_lru_conv_w, m_lru_conv_b, m_lru_wa, m_lru_ba, m_lru_wx, m_lru_bx, m_lru_lambda, m_lru_norm_g, m_sc_conv_w, m_sc_norm_g, m_rwkv_mu, m_rwkv_w0, m_rwkv_w2, m_rwkv_a0, m_rwkv_a2, m_rwkv_g2, m_rwkv_k_k, m_rwkv_k_a, m_rwkv_r_k, m_rwkv_lnx_w, m_rwkv_lnx_b, v_meta_tokens, v_norm_g, v_ffn1_w_in, v_ffn1_w_out, v_ffn2_w_in, v_ffn2_w_out, v_mix_w_in, v_mix_w_out, v_lru_conv_w, v_lru_conv_b, v_lru_wa, v_lru_ba, v_lru_wx, v_lru_bx, v_lru_lambda, v_lru_norm_g, v_sc_conv_w, v_sc_norm_g, v_rwkv_mu, v_rwkv_w0, v_rwkv_w2, v_rwkv_a0, v_rwkv_a2, v_rwkv_g2, v_rwkv_k_k, v_rwkv_k_a, v_rwkv_r_k, v_rwkv_lnx_w, v_rwkv_lnx_b):
    w_vals = (meta_tokens, norm_g, ffn1_w_in, ffn1_w_out, ffn2_w_in, ffn2_w_out, mix_w_in, mix_w_out, lru_conv_w, lru_conv_b, lru_wa, lru_ba, lru_wx, lru_bx, lru_lambda, lru_norm_g, sc_conv_w, sc_norm_g, rwkv_mu, rwkv_w0, rwkv_w2, rwkv_a0, rwkv_a2, rwkv_g2, rwkv_k_k, rwkv_k_a, rwkv_r_k, rwkv_lnx_w, rwkv_lnx_b)
    m_vals = (m_meta_tokens, m_norm_g, m_ffn1_w_in, m_ffn1_w_out, m_ffn2_w_in, m_ffn2_w_out, m_mix_w_in, m_mix_w_out, m_lru_conv_w, m_lru_conv_b, m_lru_wa, m_lru_ba, m_lru_wx, m_lru_bx, m_lru_lambda, m_lru_norm_g, m_sc_conv_w, m_sc_norm_g, m_rwkv_mu, m_rwkv_w0, m_rwkv_w2, m_rwkv_a0, m_rwkv_a2, m_rwkv_g2, m_rwkv_k_k, m_rwkv_k_a, m_rwkv_r_k, m_rwkv_lnx_w, m_rwkv_lnx_b)
    v_vals = (v_meta_tokens, v_norm_g, v_ffn1_w_in, v_ffn1_w_out, v_ffn2_w_in, v_ffn2_w_out, v_mix_w_in, v_mix_w_out, v_lru_conv_w, v_lru_conv_b, v_lru_wa, v_lru_ba, v_lru_wx, v_lru_bx, v_lru_lambda, v_lru_norm_g, v_sc_conv_w, v_sc_norm_g, v_rwkv_mu, v_rwkv_w0, v_rwkv_w2, v_rwkv_a0, v_rwkv_a2, v_rwkv_g2, v_rwkv_k_k, v_rwkv_k_a, v_rwkv_r_k, v_rwkv_lnx_w, v_rwkv_lnx_b)
    return _step(dict(zip(WEIGHTS, w_vals)), dict(zip(WEIGHTS, m_vals)), dict(zip(WEIGHTS, v_vals)), x, loss_target)
```

```python
import functools

import jax
import jax.numpy as jnp
from jax import lax
from jax.experimental import pallas as pl
from jax.experimental.pallas import tpu as pltpu

F32 = jnp.float32
BF16 = jnp.bfloat16
PARAM_DOT = lax.Precision.DEFAULT

N_DEV = 8
MESH_AXES = ("x", "y", "c")
N_META = 16
D_MODEL = 1024
LRU_W = 256
SC_W = 256
RW_W = 512
HEAD = 64
LANES = 128
CHUNK = 64
RW_IN = 1664
N_IN = 2944
FFN_BLK = 704
RMS_EPS = 1e-6
LNX_EPS = 64e-5
LRU_C = 8.0
ADAM_LR, ADAM_B1, ADAM_B2, ADAM_EPS, ADAM_WD, ADAM_STEP = 0.001, 0.9, 0.999, 1e-08, 0.01, 10

WEIGHTS = ['meta_tokens', 'norm_g', 'ffn1_w_in', 'ffn1_w_out', 'ffn2_w_in', 'ffn2_w_out', 'mix_w_in', 'mix_w_out',
           'lru_conv_w', 'lru_conv_b', 'lru_wa', 'lru_ba', 'lru_wx', 'lru_bx', 'lru_lambda', 'lru_norm_g',
           'sc_conv_w', 'sc_norm_g', 'rwkv_mu', 'rwkv_w0', 'rwkv_w2', 'rwkv_a0', 'rwkv_a2', 'rwkv_g2', 'rwkv_k_k',
           'rwkv_k_a', 'rwkv_r_k', 'rwkv_lnx_w', 'rwkv_lnx_b']
BIG = ['ffn1_w_in', 'ffn1_w_out', 'ffn2_w_in', 'ffn2_w_out', 'mix_w_in', 'mix_w_out']
SMALL_SHARDED = {'meta_tokens': 1, 'norm_g': 2, 'lru_conv_w': 2, 'sc_conv_w': 2, 'rwkv_w2': 2, 'rwkv_a2': 2, 'rwkv_g2': 2}
SMALL = [n for n in WEIGHTS if n not in BIG]


def _pick(n, cands):
    for c in cands:
        if n % c == 0:
            return c
    raise ValueError(f"no tile for {n}")


def _rowwise(fn, rows, params, row_outs, acc_outs, *, tm, name):
    nr, npar, nro, nao = len(rows), len(params), len(row_outs), len(acc_outs)
    n_rows = rows[0].shape[-2]
    assert n_rows % tm == 0, (name, n_rows, tm)

    def body(*refs):
        vals = [r[...] for r in refs[:nr + npar]]
        outs = fn(*vals)
        if not isinstance(outs, (tuple, list)):
            outs = (outs,)
        assert len(outs) == nro + nao, (name, len(outs))
        for o_ref, o in zip(refs[nr + npar:nr + npar + nro], outs[:nro]):
            o_ref[...] = o.astype(o_ref.dtype)
        step = pl.program_id(0)
        for a_ref, a in zip(refs[nr + npar + nro:], outs[nro:]):
            @pl.when(step == 0)
            def _(a_ref=a_ref, a=a):
                a_ref[...] = a.astype(F32)

            @pl.when(step > 0)
            def _(a_ref=a_ref, a=a):
                a_ref[...] += a.astype(F32)

    def row_spec(shape):
        if len(shape) == 2:
            return pl.BlockSpec((tm, shape[1]), lambda i: (i, 0))
        return pl.BlockSpec((shape[0], tm, shape[2]), lambda i: (0, i, 0))

    def full_spec(shape):
        nd = len(shape)
        return pl.BlockSpec(tuple(shape), lambda i, nd=nd: (0,) * nd)

    in_specs = [row_spec(r.shape) for r in rows] + [full_spec(p.shape) for p in params]
    out_shape = [jax.ShapeDtypeStruct((n_rows, w), dt) for (w, dt) in row_outs]
    out_shape += [jax.ShapeDtypeStruct(tuple(s), F32) for s in acc_outs]
    out_specs = [row_spec((n_rows, w)) for (w, _) in row_outs] + [full_spec(s) for s in acc_outs]
    res = pl.pallas_call(body, name=name, grid=(n_rows // tm,), in_specs=in_specs, out_specs=out_specs,
                         out_shape=out_shape)(*rows, *params)
    return tuple(res)


def _mm(a3, b3, *, trans_b, tm, tn, out_dtype, name):
    nj, m, kb = a3.shape
    n = b3.shape[1] if trans_b else b3.shape[2]
    dims = (((1,), (1,)), ((), ())) if trans_b else (((1,), (0,)), ((), ()))

    def body(a_ref, b_ref, o_ref):
        acc = lax.dot_general(a_ref[0], b_ref[0], dims, preferred_element_type=F32)
        for j in range(1, nj):
            acc = acc + lax.dot_general(a_ref[j], b_ref[j], dims, preferred_element_type=F32)
        o_ref[...] = acc.astype(o_ref.dtype)

    if trans_b:
        b_spec = pl.BlockSpec((nj, tn, kb), lambda i, c: (0, c, 0))
    else:
        b_spec = pl.BlockSpec((nj, kb, tn), lambda i, c: (0, 0, c))
    return pl.pallas_call(
        body, name=name, grid=(m // tm, n // tn),
        in_specs=[pl.BlockSpec((nj, tm, kb), lambda i, c: (0, i, 0)), b_spec],
        out_specs=pl.BlockSpec((tm, tn), lambda i, c: (i, c)),
        out_shape=jax.ShapeDtypeStruct((m, n), out_dtype),
    )(a3, b3)


def _mm_tn(a3, b3, *, tk, name):
    ja, t, ka = a3.shape
    jb, _, n = b3.shape
    nj = max(ja, jb)

    def body(a_ref, b_ref, o_ref):
        o_ref[0] = lax.dot_general(a_ref[0], b_ref[0], (((0,), (0,)), ((), ())),
                                   preferred_element_type=F32).astype(o_ref.dtype)

    return pl.pallas_call(
        body, name=name, grid=(nj, ka // tk),
        in_specs=[pl.BlockSpec((1, t, tk), (lambda j, c: (j, 0, c)) if ja > 1 else (lambda j, c: (0, 0, c))),
                  pl.BlockSpec((1, t, n), (lambda j, c: (j, 0, 0)) if jb > 1 else (lambda j, c: (0, 0, 0)))],
        out_specs=pl.BlockSpec((1, tk, n), lambda j, c: (j, c, 0)),
        out_shape=jax.ShapeDtypeStruct((nj, ka, n), BF16),
    )(a3, b3)


def _ffn_in(a, w24, *, tm, name):
    t, d = a.shape
    nb, fb = w24.shape[1], w24.shape[3]

    def body(a_ref, w_ref, gu_ref, s_ref):
        x = a_ref[...]
        g = jnp.dot(x, w_ref[0, 0], preferred_element_type=F32)
        u = jnp.dot(x, w_ref[1, 0], preferred_element_type=F32)
        gu_ref[0, 0] = g.astype(BF16)
        gu_ref[1, 0] = u.astype(BF16)
        s_ref[0] = (g * jax.nn.sigmoid(g) * u).astype(BF16)

    return pl.pallas_call(
        body, name=name, grid=(nb, t // tm),
        in_specs=[pl.BlockSpec((tm, d), lambda j, i: (i, 0)), pl.BlockSpec((2, 1, d, fb), lambda j, i: (0, j, 0, 0))],
        out_specs=[pl.BlockSpec((2, 1, tm, fb), lambda j, i: (0, j, i, 0)), pl.BlockSpec((1, tm, fb), lambda j, i: (j, i, 0))],
        out_shape=[jax.ShapeDtypeStruct((2, nb, t, fb), BF16), jax.ShapeDtypeStruct((nb, t, fb), BF16)],
    )(a, w24)


def _ffn_dswiglu(df, wo4, gu, *, tm, name):
    t, d = df.shape
    nb, fb = wo4.shape[0], wo4.shape[1]

    def body(df_ref, wo_ref, gu_ref, dg_ref):
        ds = lax.dot_general(df_ref[...], wo_ref[0], (((1,), (1,)), ((), ())), preferred_element_type=F32)
        g = gu_ref[0, 0].astype(F32)
        u = gu_ref[1, 0].astype(F32)
        sig = jax.nn.sigmoid(g)
        dg_ref[0, 0] = (ds * u * sig * (1.0 + g * (1.0 - sig))).astype(BF16)
        dg_ref[1, 0] = (ds * g * sig).astype(BF16)

    return pl.pallas_call(
        body, name=name, grid=(nb, t // tm),
        in_specs=[pl.BlockSpec((tm, d), lambda j, i: (i, 0)), pl.BlockSpec((1, fb, d), lambda j, i: (j, 0, 0)),
                  pl.BlockSpec((2, 1, tm, fb), lambda j, i: (0, j, i, 0))],
        out_specs=pl.BlockSpec((2, 1, tm, fb), lambda j, i: (0, j, i, 0)),
        out_shape=jax.ShapeDtypeStruct((2, nb, t, fb), BF16),
    )(df, wo4, gu)


def _rms(x, g):
    return x * lax.rsqrt(jnp.mean(x * x, axis=-1, keepdims=True) + RMS_EPS) * g


def _rms_bwd(x, g, dy):
    rstd = lax.rsqrt(jnp.mean(x * x, axis=-1, keepdims=True) + RMS_EPS)
    xh = x * rstd
    dxh = dy * g
    dx = rstd * (dxh - xh * jnp.mean(dxh * xh, axis=-1, keepdims=True))
    return dx, jnp.sum(dy * xh, axis=0, keepdims=True)


def _seg_sum_impl(x, bd):
    bd2 = jnp.concatenate([bd, bd], axis=0).astype(BF16)
    hi = x.astype(BF16)
    lo = (x - hi.astype(F32)).astype(BF16)
    parts = [jnp.dot(jnp.concatenate([hi[:, q * LANES:(q + 1) * LANES], lo[:, q * LANES:(q + 1) * LANES]], axis=1), bd2,
                     preferred_element_type=F32) for q in range(x.shape[1] // LANES)]
    return parts[0] if len(parts) == 1 else jnp.concatenate(parts, axis=1)


@jax.custom_vjp
def _seg_sum(x, bd):
    return _seg_sum_impl(x, bd)


def _seg_sum_fwd(x, bd):
    return _seg_sum_impl(x, bd), bd


def _seg_sum_bwd(bd, ct):
    return _seg_sum_impl(ct, bd), jnp.zeros_like(bd)


_seg_sum.defvjp(_seg_sum_fwd, _seg_sum_bwd)


def _group_rms(y, g, bd):
    return y * lax.rsqrt(_seg_sum(y * y, bd) * (1.0 / HEAD) + RMS_EPS) * g


def _expm1(x):
    return jnp.where(jnp.abs(x) < 1e-2, x * (1.0 + x * (0.5 + x * (1.0 / 6.0))), jnp.exp(x) - 1.0)


def _lru_pre(x0, x1, x2, x3, cw0, cw1, cw2, cw3, cb, wa, ba, wx, bx, lam):
    u = x3 * cw0 + x2 * cw1 + x1 * cw2 + x0 * cw3 + cb
    r = jax.nn.sigmoid(jnp.dot(u, wa, preferred_element_type=F32, precision=PARAM_DOT) + ba)
    i = jax.nn.sigmoid(jnp.dot(u, wx, preferred_element_type=F32, precision=PARAM_DOT) + bx)
    log_a = -LRU_C * r * jax.nn.softplus(-lam)
    return jnp.exp(log_a), jnp.sqrt(-_expm1(2.0 * log_a)) * (i * u)


def _lru_post(bd, gate, hs, ng):
    return _group_rms(jax.nn.gelu(gate) * hs, ng, bd)


def _sc_fwd(bd, b, c0, x0, c1, x1, c2, x2, w0, w1, w2, ng):
    return _group_rms(b * (w0 * (c2 * x2) + w1 * (c1 * x1) + w2 * (c0 * x0)), ng, bd)


def _rw_pre(bd, zr, zk, zv, zt, sr, sk, sv, st, mur, muk, muv, mut, w0, w2p, a0, a2p, g2p, k_k, k_a):
    r, k, v, tail = zr + (sr - zr) * mur, zk + (sk - zk) * muk, zv + (sv - zv) * muv, zt + (st - zt) * mut
    lane = lax.broadcasted_iota(jnp.int32, tail.shape, 1)
    act = jnp.where(lane < 32, jnp.tanh(tail), jnp.where(lane < 64, tail, jax.nn.sigmoid(tail)))
    dot = functools.partial(jnp.dot, preferred_element_type=F32, precision=PARAM_DOT)
    w_log = -jax.nn.softplus(-(w0 + dot(act, w2p))) - 0.5
    w = jnp.exp(-jnp.exp(w_log))
    a = jax.nn.sigmoid(a0 + dot(act, a2p))
    g = dot(act, g2p)
    kk = k * k_k
    k2 = k * (1.0 + (a - 1.0) * k_a)
    kkn = kk * lax.rsqrt(jnp.maximum(_seg_sum(kk * kk, bd), 1e-24))
    return r, w, k2, -kkn, kkn * a, v, g


def _rw_post(bd, y, r, k2, v, g, lnw, lnb, r_k):
    mean = _seg_sum(y, bd) * (1.0 / HEAD)
    yc = y - mean
    var = _seg_sum(yc * yc, bd) * (1.0 / HEAD)
    yn = yc * lax.rsqrt(var + LNX_EPS) * lnw + lnb
    return (yn + _seg_sum(r * k2 * r_k, bd) * v) * g


def _vjp_rows(fwd, n_static, n_in, n_ct):
    def fn(*args):
        static, prim, cts = args[:n_static], args[n_static:n_static + n_in], args[n_static + n_in:]
        assert len(cts) == n_ct
        _, vjp = jax.vjp(functools.partial(fwd, *static), *prim)
        return vjp(cts[0] if n_ct == 1 else tuple(cts))
    return fn


def _all_to_all_copies(x_refs, o_refs, sems):
    send_sems, recv_sems, local_sems = sems
    mx, my, mc = lax.axis_index("x"), lax.axis_index("y"), lax.axis_index("c")
    me = 4 * mx + 2 * my + mc
    local, sends, recvs = [], [], []
    for k in range(len(x_refs)):
        local.append(pltpu.make_async_copy(x_refs[k].at[me], o_refs[k].at[me], local_sems.at[k]))
    for d in range(1, N_DEV):
        px, py, pc = mx ^ ((d >> 2) & 1), my ^ ((d >> 1) & 1), mc ^ (d & 1)
        peer = 4 * px + 2 * py + pc
        for k in range(len(x_refs)):
            common = dict(src_ref=x_refs[k].at[peer], send_sem=send_sems.at[k, d - 1], recv_sem=recv_sems.at[k, d - 1],
                          device_id=(px, py, pc), device_id_type=pl.DeviceIdType.MESH)
            sends.append(pltpu.make_async_remote_copy(dst_ref=o_refs[k].at[me], **common))
            recvs.append(pltpu.make_async_remote_copy(dst_ref=o_refs[k].at[peer], **common))
    return local, sends, recvs


def _gather2_copies(x_refs, o_refs, sems):
    send_sems, recv_sems, local_sems = sems
    mx, my, mc = lax.axis_index("x"), lax.axis_index("y"), lax.axis_index("c")
    sibling = (mx, my, 1 - mc)
    chips = [(1 - mx, my), (mx, 1 - my), (1 - mx, 1 - my)]

    def slot(px, py, pc):
        return 4 * px + 2 * py + pc

    out = dict(local=[], first=[], first_recv=[], ici_recv=[], passed=[], passed_recv=[])
    for k in range(len(x_refs)):
        def copy(sem, src, dst_slot, to, k=k):
            return pltpu.make_async_remote_copy(src_ref=src, dst_ref=o_refs[k].at[dst_slot], send_sem=send_sems.at[k, sem],
                                                recv_sem=recv_sems.at[k, sem], device_id=to, device_id_type=pl.DeviceIdType.MESH)
        me = slot(mx, my, mc)
        out['local'].append(pltpu.make_async_copy(x_refs[k], o_refs[k].at[me], local_sems.at[k]))
        out['first'].append(copy(0, x_refs[k], me, sibling))
        out['first_recv'].append(copy(0, x_refs[k], slot(mx, my, 1 - mc), sibling))
        for j, (px, py) in enumerate(chips):
            out['first'].append(copy(1 + j, x_refs[k], me, (px, py, mc)))
            out['ici_recv'].append(copy(1 + j, x_refs[k], slot(px, py, mc), (px, py, mc)))
            out['passed'].append(copy(4 + j, o_refs[k].at[slot(px, py, mc)], slot(px, py, mc), sibling))
            out['passed_recv'].append(copy(4 + j, x_refs[k], slot(px, py, 1 - mc), sibling))
    return out


def _exchange_start(x_refs, o_refs, sems, gather):
    if gather:
        cps = _gather2_copies(x_refs, o_refs, sems)
        for cp in cps['local'] + cps['first']:
            cp.start()
        return
    local, sends, _ = _all_to_all_copies(x_refs, o_refs, sems)
    for cp in local + sends:
        cp.start()


def _exchange_wait(x_refs, o_refs, sems, gather):
    if gather:
        cps = _gather2_copies(x_refs, o_refs, sems)
        for arrived, onward in zip(cps['ici_recv'], cps['passed']):
            arrived.wait_recv()
            onward.start()
        for cp in cps['first'] + cps['passed']:
            cp.wait_send()
        for cp in cps['first_recv'] + cps['passed_recv']:
            cp.wait_recv()
        for cp in cps['local']:
            cp.wait()
        return
    local, sends, recvs = _all_to_all_copies(x_refs, o_refs, sems)
    for cp in sends:
        cp.wait_send()
    for cp in recvs:
        cp.wait_recv()
    for cp in local:
        cp.wait()


def _exchange_out_shape(xs, gather):
    return [jax.ShapeDtypeStruct(((N_DEV,) + x.shape) if gather else x.shape, x.dtype) for x in xs]


def _exchange_sems(n):
    return [pltpu.SemaphoreType.DMA((n, N_DEV - 1)), pltpu.SemaphoreType.DMA((n, N_DEV - 1)), pltpu.SemaphoreType.DMA((n,))]


SUBLANES = 8


def _store_row(ref, i, cols, row):
    base = pl.multiple_of((i // SUBLANES) * SUBLANES, SUBLANES)
    sub = lax.broadcasted_iota(jnp.int32, (SUBLANES, row.shape[1]), 0)
    ref[pl.ds(base, SUBLANES), cols] = jnp.where(sub == i % SUBLANES, row, ref[pl.ds(base, SUBLANES), cols])


def _tile_scan(a, b, reverse):
    sub = lax.broadcasted_iota(jnp.int32, a.shape, 0)
    for sh in (1, 2, 4):
        if reverse:
            live = sub < SUBLANES - sh
            a_s, b_s = pltpu.roll(a, SUBLANES - sh, 0), pltpu.roll(b, SUBLANES - sh, 0)
        else:
            live = sub >= sh
            a_s, b_s = pltpu.roll(a, sh, 0), pltpu.roll(b, sh, 0)
        b = jnp.where(live, a * b_s, 0.0) + b
        a = jnp.where(live, a * a_s, a)
    return a, b


def _lru_scan(a, b, name):
    t, w = a.shape

    def body(a_ref, b_ref, h_ref):
        def tile(j, h):
            rows = pl.ds(pl.multiple_of(j * SUBLANES, SUBLANES), SUBLANES)
            ca, cb = _tile_scan(a_ref[rows, :], b_ref[rows, :], False)
            out = ca * h + cb
            h_ref[rows, :] = out
            return out[SUBLANES - 1:SUBLANES]
        lax.fori_loop(0, t // SUBLANES, tile, jnp.zeros((1, w), F32))

    return pl.pallas_call(body, name=name, out_shape=jax.ShapeDtypeStruct((t, w), F32))(a, b)


def _lru_scan_bwd(a_next, h_prev, dhs, name):
    t, w = dhs.shape

    def body(a_ref, h_ref, dh_ref, da_ref, db_ref):
        def tile(n, lam):
            rows = pl.ds(pl.multiple_of((t // SUBLANES - 1 - n) * SUBLANES, SUBLANES), SUBLANES)
            ca, cb = _tile_scan(a_ref[rows, :], dh_ref[rows, :], True)
            out = ca * lam + cb
            db_ref[rows, :] = out
            da_ref[rows, :] = out * h_ref[rows, :]
            return out[0:1]
        lax.fori_loop(0, t // SUBLANES, tile, jnp.zeros((1, w), F32))

    return pl.pallas_call(body, name=name, out_shape=[jax.ShapeDtypeStruct((t, w), F32)] * 2)(a_next, h_prev, dhs)


N_PAIR = RW_W // LANES


def _bcast_cols(v, bd2, name):
    t = v.shape[0]

    def body(v_ref, bd_ref, o_ref):
        bdv = bd_ref[...]
        sub = lax.broadcasted_iota(jnp.int32, (HEAD, LANES), 0)
        own = lax.broadcasted_iota(jnp.int32, (HEAD, LANES), 1) % HEAD == sub
        for i in range(CHUNK):
            row = v_ref[i:i + 1, :]
            sums = _group_sums([jnp.where(own, row[:, p * LANES:(p + 1) * LANES], 0.0) for p in range(N_PAIR)], bdv)
            for p in range(N_PAIR):
                o_ref[i, p] = sums[p * HEAD:(p + 1) * HEAD]

    return pl.pallas_call(
        body, name=name, grid=(t // CHUNK,),
        in_specs=[pl.BlockSpec((CHUNK, RW_W), lambda i: (i, 0)), pl.BlockSpec((2 * LANES, LANES), lambda i: (0, 0))],
        out_specs=pl.BlockSpec((CHUNK, N_PAIR, HEAD, LANES), lambda i: (i, 0, 0, 0)),
        out_shape=jax.ShapeDtypeStruct((t, N_PAIR, HEAD, LANES), F32),
    )(v, bd2)


def _cols_to_rows(cols_ref, rows_ref):
    lane = lax.broadcasted_iota(jnp.int32, (CHUNK, LANES), 1)
    for p in range(N_PAIR):
        tile = cols_ref[p]
        sq = jnp.concatenate([tile, jnp.zeros_like(tile)], axis=0).T
        rows_ref[:, p * LANES:(p + 1) * LANES] = jnp.where(lane < HEAD, sq[0:CHUNK], pltpu.roll(sq[CHUNK:2 * CHUNK], HEAD, 1))


def _stacked_bf16(bd):
    return jnp.concatenate([bd, bd], axis=0).astype(BF16)


def _group_sums(prods, bd2):
    x = jnp.concatenate(prods, axis=0)
    hi = x.astype(BF16)
    lo = (x - hi.astype(F32)).astype(BF16)
    return jnp.dot(jnp.concatenate([hi, lo], axis=1), bd2, preferred_element_type=F32)


def _rw_scan(r, r_prev, w, k, c, c_next, b, vb, bd2, name, ride=None):
    t = w.shape[0]
    nch = t // CHUNK
    ride_xs, ride_gather = ride if ride is not None else ([], False)
    n_ride = len(ride_xs)

    def body(*refs):
        r_ref, rp_ref, w_ref, k_ref, c_ref, cn_ref, b_ref, vb_ref, bd_ref = refs[:9]
        x_refs = refs[9:9 + n_ride]
        spre_ref, y_ref = refs[9 + n_ride:11 + n_ride]
        o_refs = refs[11 + n_ride:11 + 2 * n_ride]
        s_ref, wc_ref, wr_ref, bc_ref, kc_ref, br_ref, kr_ref, yt_ref = refs[11 + 2 * n_ride:19 + 2 * n_ride]
        sems = refs[19 + 2 * n_ride:]

        @pl.when(pl.program_id(0) == 0)
        def _():
            s_ref[...] = jnp.zeros_like(s_ref)
            if n_ride:
                _exchange_start(x_refs, o_refs, sems, ride_gather)

        yt_ref[...] = jnp.zeros_like(yt_ref)
        bdv = bd_ref[...]
        lane = lax.broadcasted_iota(jnp.int32, (HEAD, LANES), 1) % CHUNK

        wv, cn, rv, bv, kv = w_ref[...], cn_ref[...], r_ref[...], b_ref[...], k_ref[...]
        wc_ref[...] = wv * cn
        wr_ref[...] = wv * rv
        for ref, x in ((bc_ref, bv * cn), (kc_ref, kv * cn), (br_ref, bv * rv), (kr_ref, kv * rv)):
            sums = _group_sums([x[:, q * LANES:(q + 1) * LANES] for q in range(N_PAIR)], bdv)
            for q in range(N_PAIR):
                ref[:, q * LANES:(q + 1) * LANES] = sums[q * CHUNK:(q + 1) * CHUNK]

        def cut(ref, i):
            x = ref[pl.ds(i, 1), :]
            return [x[:, p * LANES:(p + 1) * LANES] for p in range(N_PAIR)]

        def two_steps(j, st):
            i0 = 2 * j
            i1 = i0 + 1
            c0, wc0, rp0, wr0 = cut(c_ref, i0), cut(wc_ref, i0), cut(rp_ref, i0), cut(wr_ref, i0)
            w0, b0, k0, w1, b1, k1 = cut(w_ref, i0), cut(b_ref, i0), cut(k_ref, i0), cut(w_ref, i1), cut(b_ref, i1), cut(k_ref, i1)
            bc0, kc0, br0, kr0 = cut(bc_ref, i0), cut(kc_ref, i0), cut(br_ref, i0), cut(kr_ref, i0)
            pairs = range(N_PAIR)
            red = _group_sums([st[p] * c0[p] for p in pairs] + [st[p] * wc0[p] for p in pairs], bdv)
            out = _group_sums([st[p] * rp0[p] for p in pairs] + [st[p] * wr0[p] for p in pairs], bdv)
            new = []
            for p in pairs:
                v0, v1 = vb_ref[i0, p], vb_ref[i1, p]
                sa0 = red[p * HEAD:(p + 1) * HEAD]
                sa1 = red[(N_PAIR + p) * HEAD:(N_PAIR + p + 1) * HEAD] + sa0 * bc0[p] + v0 * kc0[p]
                y_before = out[p * HEAD:(p + 1) * HEAD]
                y0 = out[(N_PAIR + p) * HEAD:(N_PAIR + p + 1) * HEAD] + sa0 * br0[p] + v0 * kr0[p]
                spre_ref[i0, p] = st[p]
                s1 = st[p] * w0[p] + sa0 * b0[p] + v0 * k0[p]
                spre_ref[i1, p] = s1
                new.append(s1 * w1[p] + sa1 * b1[p] + v1 * k1[p])
                yt_ref[p] = jnp.where(lane == i0, y_before, jnp.where(lane == i1, y0, yt_ref[p]))
            return tuple(new)

        st = lax.fori_loop(0, CHUNK // 2, two_steps, tuple(s_ref[p] for p in range(N_PAIR)))
        for p in range(N_PAIR):
            s_ref[p] = st[p]
        _cols_to_rows(yt_ref, y_ref)

        if n_ride:
            @pl.when(pl.program_id(0) == nch - 1)
            def _():
                _exchange_wait(x_refs, o_refs, sems, ride_gather)

    row = pl.BlockSpec((CHUNK, RW_W), lambda i: (i, 0))
    big = pl.BlockSpec((CHUNK, N_PAIR, HEAD, LANES), lambda i: (i, 0, 0, 0))
    any_spec = pl.BlockSpec(memory_space=pl.ANY)
    return pl.pallas_call(
        body, name=name, grid=(nch,),
        in_specs=[row] * 7 + [big, pl.BlockSpec((2 * LANES, LANES), lambda i: (0, 0))] + [any_spec] * n_ride,
        out_specs=[big, row] + [any_spec] * n_ride,
        out_shape=[jax.ShapeDtypeStruct((t, N_PAIR, HEAD, LANES), F32), jax.ShapeDtypeStruct((t, RW_W), F32)]
        + _exchange_out_shape(ride_xs, ride_gather),
        scratch_shapes=[pltpu.VMEM((N_PAIR, HEAD, LANES), F32)] + [pltpu.VMEM((CHUNK, RW_W), F32)] * 6
        + [pltpu.VMEM((N_PAIR, HEAD, LANES), F32)] + (_exchange_sems(n_ride) if n_ride else []),
    )(r, r_prev, w, k, c, c_next, b, vb, bd2, *ride_xs)


def _rw_scan_bwd(r, w, k, c, b, vb, dyb, spre, bd, name, ride=None):
    t = r.shape[0]
    nch = t // CHUNK
    ride_xs, ride_gather = ride if ride is not None else ([], False)
    n_ride = len(ride_xs)
    n_pre = 8

    def body(*refs):
        r_ref, w_ref, k_ref, c_ref, b_ref, vb_ref, dyb_ref, spre_ref, bd_ref = refs[:9]
        x_refs = refs[9:9 + n_ride]
        dr_ref, dw_ref, dk_ref, dc_ref, db_ref, dv_ref = refs[9 + n_ride:15 + n_ride]
        o_refs = refs[15 + n_ride:15 + 2 * n_ride]
        g_ref, snext_ref, dvt_ref = refs[15 + 2 * n_ride:18 + 2 * n_ride]
        pre = refs[18 + 2 * n_ride:18 + 2 * n_ride + n_pre]
        sems = refs[18 + 2 * n_ride + n_pre:]
        wb_ref, wk_ref, rb_ref, rk_ref, rwb_ref, cb_ref, rwk_ref, ck_ref = pre[:8]

        @pl.when(pl.program_id(0) == 0)
        def _():
            g_ref[...] = jnp.zeros_like(g_ref)
            snext_ref[...] = jnp.zeros_like(snext_ref)
            if n_ride:
                _exchange_start(x_refs, o_refs, sems, ride_gather)

        for ref in (dr_ref, dw_ref, dk_ref, dc_ref, db_ref, dvt_ref):
            ref[...] = jnp.zeros_like(ref)
        bdv = bd_ref[...]
        bd1 = bdv[0:LANES]
        lane = lax.broadcasted_iota(jnp.int32, (HEAD, LANES), 1) % CHUNK

        rv, wv, kv, cv, bv = r_ref[...], w_ref[...], k_ref[...], c_ref[...], b_ref[...]
        b_b, k_b = pltpu.roll(bv, 1, 0), pltpu.roll(kv, 1, 0)
        wb_ref[...] = wv * b_b
        wk_ref[...] = wv * k_b
        rw = rv * wv
        for ref, x in ((rb_ref, rv * bv), (rk_ref, rv * kv), (rwb_ref, rw * b_b), (cb_ref, cv * b_b), (rwk_ref, rw * k_b),
                       (ck_ref, cv * k_b)):
            sums = _group_sums([x[:, q * LANES:(q + 1) * LANES] for q in range(N_PAIR)], bdv)
            for q in range(N_PAIR):
                ref[:, q * LANES:(q + 1) * LANES] = sums[q * CHUNK:(q + 1) * CHUNK]

        def sum0(x):
            return jnp.sum(x, axis=0, keepdims=True)

        def cut(ref, i):
            x = ref[pl.ds(i, 1), :]
            return [x[:, p * LANES:(p + 1) * LANES] for p in range(N_PAIR)]

        def sums_bf16(prods):
            return jnp.dot(jnp.concatenate(prods, axis=0).astype(BF16), bd1, preferred_element_type=F32)

        def two_steps(n, gs):
            ia = CHUNK - 1 - 2 * n
            ib = ia - 1
            r_a, w_a, k_a, c_a, b_a = [cut(ref, ia) for ref in (r_ref, w_ref, k_ref, c_ref, b_ref)]
            r_b, w_b, c_b = [cut(ref, ib) for ref in (r_ref, w_ref, c_ref)]
            wb, wk, rb_a, rk_a, rwb, cb, rwk, ck = [cut(ref, ia) for ref in pre[:8]]
            rb_b, rk_b = cut(rb_ref, ib), cut(rk_ref, ib)
            pairs = range(N_PAIR)
            sp_a = [spre_ref[ia, p] for p in pairs]
            sp_b = [spre_ref[ib, p] for p in pairs]
            dy_a = [dyb_ref[ia, p] for p in pairs]
            dy_b = [dyb_ref[ib, p] for p in pairs]
            chain = _group_sums([gs[p] * b_a[p] for p in pairs] + [gs[p] * wb[p] for p in pairs], bdv)
            off = sums_bf16([gs[p] * k_a[p] for p in pairs] + [gs[p] * wk[p] for p in pairs]
                            + [sp_a[p] * c_a[p] for p in pairs] + [sp_b[p] * c_b[p] for p in pairs])
            new = []
            for p in pairs:
                def part(x, q, p=p):
                    return x[(q * N_PAIR + p) * HEAD:(q * N_PAIR + p + 1) * HEAD]
                dsa_a = part(chain, 0) + dy_a[p] * rb_a[p]
                dv_a = part(off, 0) + dy_a[p] * rk_a[p]
                dsa_b = part(chain, 1) + dy_a[p] * rwb[p] + dsa_a * cb[p] + dy_b[p] * rb_b[p]
                dv_b = part(off, 1) + dy_a[p] * rwk[p] + dsa_a * ck[p] + dy_b[p] * rk_b[p]
                sa_a, sa_b = part(off, 2), part(off, 3)
                g_a = gs[p] + dy_a[p] * r_a[p]
                g_mid = g_a * w_a[p] + dsa_a * c_a[p]
                g_b = g_mid + dy_b[p] * r_b[p]
                new.append(g_b * w_b[p] + dsa_b * c_b[p])
                cols = pl.ds(p * LANES, LANES)
                for i, dy, s_post, s_pre, g, sa, dsa in ((ia, dy_a[p], snext_ref[p], sp_a[p], g_a, sa_a, dsa_a),
                                                         (ib, dy_b[p], sp_a[p], sp_b[p], g_b, sa_b, dsa_b)):
                    _store_row(dr_ref, i, cols, sum0(s_post * dy))
                    _store_row(dw_ref, i, cols, sum0(g * s_pre))
                    _store_row(db_ref, i, cols, sum0(g * sa))
                    _store_row(dk_ref, i, cols, sum0(g * vb_ref[i, p]))
                    _store_row(dc_ref, i, cols, sum0(s_pre * dsa))
                dvt_ref[p] = jnp.where(lane == ia, dv_a, jnp.where(lane == ib, dv_b, dvt_ref[p]))
                snext_ref[p] = sp_b[p]
            return tuple(new)

        gs = lax.fori_loop(0, CHUNK // 2, two_steps, tuple(g_ref[p] for p in range(N_PAIR)))
        for p in range(N_PAIR):
            g_ref[p] = gs[p]
        _cols_to_rows(dvt_ref, dv_ref)

        if n_ride:
            @pl.when(pl.program_id(0) == nch - 1)
            def _():
                _exchange_wait(x_refs, o_refs, sems, ride_gather)

    row = pl.BlockSpec((CHUNK, RW_W), lambda i: (nch - 1 - i, 0))
    big = pl.BlockSpec((CHUNK, N_PAIR, HEAD, LANES), lambda i: (nch - 1 - i, 0, 0, 0))
    any_spec = pl.BlockSpec(memory_space=pl.ANY)
    return pl.pallas_call(
        body, name=name, grid=(nch,),
        in_specs=[row] * 5 + [big, big, big, pl.BlockSpec((2 * LANES, LANES), lambda i: (0, 0))] + [any_spec] * n_ride,
        out_specs=[row] * 6 + [any_spec] * n_ride,
        out_shape=[jax.ShapeDtypeStruct((t, RW_W), F32)] * 6 + _exchange_out_shape(ride_xs, ride_gather),
        scratch_shapes=[pltpu.VMEM((N_PAIR, HEAD, LANES), F32)] * 3
        + [pltpu.VMEM((CHUNK, RW_W), F32)] * n_pre + (_exchange_sems(n_ride) if n_ride else []),
    )(r, w, k, c, b, vb, dyb, spre, bd, *ride_xs)


def _exchange(xs, *, gather, name, gather_too=()):
    n, n2 = len(xs), len(gather_too)

    def body(*refs):
        x1, x2 = refs[:n], refs[n:n + n2]
        o1, o2 = refs[n + n2:2 * n + n2], refs[2 * n + n2:2 * (n + n2)]
        sems1, sems2 = refs[2 * (n + n2):2 * (n + n2) + 3], refs[2 * (n + n2) + 3:]
        _exchange_start(x1, o1, sems1, gather)
        if n2:
            _exchange_start(x2, o2, sems2, True)
            _exchange_wait(x2, o2, sems2, True)
        _exchange_wait(x1, o1, sems1, gather)

    any_spec = pl.BlockSpec(memory_space=pl.ANY)
    return pl.pallas_call(
        body, name=name, in_specs=[any_spec] * (n + n2), out_specs=[any_spec] * (n + n2),
        out_shape=_exchange_out_shape(xs, gather) + _exchange_out_shape(list(gather_too), True),
        scratch_shapes=_exchange_sems(n) + (_exchange_sems(n2) if n2 else []),
    )(*xs, *gather_too)


def _adamw_rows(g, w, m, v):
    m = ADAM_B1 * m + (1.0 - ADAM_B1) * g
    v = ADAM_B2 * v + (1.0 - ADAM_B2) * (g * g)
    m_hat = m / (1.0 - ADAM_B1 ** ADAM_STEP)
    v_hat = v / (1.0 - ADAM_B2 ** ADAM_STEP)
    return -ADAM_LR * (m_hat / (jnp.sqrt(v_hat) + ADAM_EPS) + ADAM_WD * w), m, v


def _sum_slots(parts):
    g = parts[0].astype(F32)
    for q in range(1, N_DEV):
        g = g + parts[q].astype(F32)
    return g


def _reduce_adamw(parts, w, m, v, name):
    rows, cols = w.shape

    def fn(parts, w, m, v):
        g = _sum_slots(parts)
        return (g,) + _adamw_rows(g, w, m, v)

    return _rowwise(fn, [parts, w, m, v], [], [(cols, F32)] * 4, [], tm=_pick(rows, (256, 128, 64, 32, 16, 8)), name=name)


def _shift(x, n):
    return jnp.pad(x, ((n, 0), (0, 0)))[:-n]


def _unshift(x, n):
    return jnp.pad(x, ((0, n), (0, 0)))[n:]


def _add_n(xs, *, tm, name):
    def fn(*vals):
        s = vals[0]
        for x in vals[1:]:
            s = s + x
        return s
    return _rowwise(fn, xs, [], [(xs[0].shape[1], F32)], [], tm=tm, name=name)[0]


def _norm_fwd(h, g, *, tm, name):
    return _rowwise(lambda x, gg: _rms(x, gg), [h], [g], [(h.shape[1], BF16)], [], tm=tm, name=name)[0]


def _res_norm_fwd(h, f, g, scale, *, tm, name):
    return _rowwise(lambda hh, ff, gg: hh + scale * _rms(ff, gg), [h, f], [g], [(h.shape[1], F32)], [], tm=tm, name=name)[0]


def _norm_bwd(x, g, dy, scale, res, out_dtype, *, tm, name):
    if res is None:
        def fn(xx, dd, gg):
            dx, dg = _rms_bwd(xx, gg, dd * scale)
            return dx, dg
        rows = [x, dy]
    else:
        def fn(xx, dd, rr, gg):
            dx, dg = _rms_bwd(xx, gg, dd * scale)
            return dx + rr, dg
        rows = [x, dy, res]
    return _rowwise(fn, rows, [g], [(x.shape[1], out_dtype)], [(1, x.shape[1])], tm=tm, name=name)


def _ffn_fwd(h, g_pre, g_post, w24, wo4, tiles, tag):
    tb, ts = tiles
    a = _norm_fwd(h, g_pre, tm=ts, name=f"{tag}_norm")
    gu, s4 = _ffn_in(a, w24, tm=tb, name=f"{tag}_in")
    f = _mm(s4, wo4, trans_b=False, tm=tb, tn=D_MODEL, out_dtype=F32, name=f"{tag}_out")
    h_new = _res_norm_fwd(h, f, g_post, 0.5, tm=ts, name=f"{tag}_res")
    return h_new, (h, a, gu, s4, f)


def _ffn_bwd(dh_new, res, g_pre, g_post, w24, wo4, tiles, tag):
    tb, ts = tiles
    h, a, gu, s4, f = res
    t = h.shape[0]
    df, dg_post = _norm_bwd(f, g_post, dh_new, 0.5, None, BF16, tm=ts, name=f"{tag}_dres")
    dgu = _ffn_dswiglu(df, wo4, gu, tm=tb, name=f"{tag}_dswiglu")
    d_wo = _mm_tn(s4, df[None], tk=FFN_BLK, name=f"{tag}_dwout")
    dgu8 = dgu.reshape(2 * w24.shape[1], t, FFN_BLK)
    w8 = w24.reshape(2 * w24.shape[1], D_MODEL, FFN_BLK)
    da = _mm(dgu8, w8, trans_b=True, tm=tb, tn=D_MODEL // 2, out_dtype=F32, name=f"{tag}_da")
    d_win = _mm_tn(a[None], dgu8, tk=D_MODEL, name=f"{tag}_dwin")
    dh, dg_pre = _norm_bwd(h, g_pre, da, 1.0, dh_new, F32, tm=ts, name=f"{tag}_dnorm")
    return dh, dg_pre, dg_post, d_win, d_wo.reshape(N_DEV, -1, D_MODEL)


def _blockdiag(w4):
    n, b, _ = w4.shape
    eye = jnp.eye(n, dtype=w4.dtype)
    return (eye[:, None, :, None] * w4[:, :, None, :]).reshape(n * b, n * b)


def _blockdiag_grad(d):
    n = d.shape[0] // HEAD
    x = d.reshape(n, HEAD, n, HEAD)
    return jnp.stack([x[i, :, i, :] for i in range(n)])


def _row(v):
    return v.reshape(1, -1)


def _mixer_fwd(h, g_pre, g_post, wi, wo, P, bd, tiles, tag, ride=None):
    tb, ts = tiles
    a = _norm_fwd(h, g_pre, tm=ts, name=f"{tag}_norm")
    p = _mm(a[None], wi[None], trans_b=False, tm=tb, tn=N_IN, out_dtype=F32, name=f"{tag}_in")
    lx, lg = p[:, 0:256], p[:, 256:512]
    sb, scc, sx = p[:, 512:768], p[:, 768:1024], p[:, 1024:1280]
    z = p[:, 1280:]
    lxs = [lx, _shift(lx, 1), _shift(lx, 2), _shift(lx, 3)]
    cw = [_row(P['lru_conv_w'][kk]) for kk in range(4)]
    lru_par = cw + [_row(P['lru_conv_b']), _blockdiag(P['lru_wa']), _row(P['lru_ba']), _blockdiag(P['lru_wx']),
                    _row(P['lru_bx']), _row(P['lru_lambda'])]
    la, lb = _rowwise(_lru_pre, lxs, lru_par, [(LRU_W, F32)] * 2, [], tm=ts, name=f"{tag}_lru_pre")
    hs = _lru_scan(la, lb, name=f"{tag}_lru_scan")
    y_lru = _rowwise(lambda gg, hh, ng, b_: _lru_post(b_, gg, hh, ng), [lg, hs], [_row(P['lru_norm_g']), bd],
                     [(LRU_W, F32)], [], tm=ts, name=f"{tag}_lru_post")[0]
    sc_rows = [sb, scc, sx, _shift(scc, 1), _shift(sx, 1), _shift(scc, 2), _shift(sx, 2)]
    sc_par = [_row(P['sc_conv_w'][kk]) for kk in range(3)] + [_row(P['sc_norm_g'])]
    y_sc = _rowwise(lambda *v: _sc_fwd(v[-1], *v[:-1]), sc_rows, sc_par + [bd], [(SC_W, F32)], [], tm=ts,
                    name=f"{tag}_sc")[0]
    cuts = (0, RW_W, 2 * RW_W, 3 * RW_W, RW_IN)
    zs = [z[:, cuts[q]:cuts[q + 1]] for q in range(4)]
    z_rows = zs + [_shift(q, 1) for q in zs]
    pad = lambda m, lo: jnp.pad(m, ((lo, LANES - lo - m.shape[0]), (0, 0)))
    rw_par = [_row(P['rwkv_mu'][cuts[q]:cuts[q + 1]]) for q in range(4)]
    rw_par += [_row(P['rwkv_w0']), pad(P['rwkv_w2'], 0), _row(P['rwkv_a0']), pad(P['rwkv_a2'], 32),
               pad(P['rwkv_g2'], 64), _row(P['rwkv_k_k']), _row(P['rwkv_k_a'])]
    r, w, k2, c, b, v, g = _rowwise(lambda *vv: _rw_pre(vv[-1], *vv[:-1]), z_rows, rw_par + [bd], [(RW_W, F32)] * 7, [],
                                    tm=ts, name=f"{tag}_rw_pre")
    vb = _bcast_cols(v, _stacked_bf16(bd), name=f"{tag}_rw_vcols")
    spre, yt, *rode = _rw_scan(r, _shift(r, 1), w, k2, c, _unshift(c, 1), b, vb, _stacked_bf16(bd), name=f"{tag}_rw_scan",
                               ride=ride)
    y = _unshift(yt, 1)
    post_par =[_row(P['rwkv_lnx_w']), _row(P['rwkv_lnx_b']), _row(P['rwkv_r_k'])]
    y_rw = _rowwise(lambda *vv: _rw_post(vv[-1], *vv[:-1]), [y, r, k2, v, g], post_par + [bd], [(RW_W, F32)], [], tm=ts,
                    name=f"{tag}_rw_post")[0]
    ycat = jnp.concatenate([y_lru, y_sc, y_rw], axis=1).astype(BF16)
    m = _mm(ycat[None], wo[None], trans_b=False, tm=tb, tn=D_MODEL, out_dtype=F32, name=f"{tag}_out")
    h_new = _res_norm_fwd(h, m, g_post, 1.0, tm=ts, name=f"{tag}_res")
    res = dict(h=h, a=a, m=m, ycat=ycat, lxs=lxs, lru_par=lru_par, lg=lg, la=la, hs=hs, sc_rows=sc_rows, sc_par=sc_par,
               z_rows=z_rows, rw_par=rw_par, r=r, w=w, k2=k2, c=c, b=b, v=v, g=g, vb=vb, spre=spre, y=y, post_par=post_par)
    return h_new, res, rode


def _mixer_bwd(dh_new, R, g_pre, g_post, wi, wo, P, bd, tiles, tag, ride=None):
    tb, ts = tiles
    dm, dg_post = _norm_bwd(R['m'], g_post, dh_new, 1.0, None, BF16, tm=ts, name=f"{tag}_dres")
    dycat = _mm(dm[None], wo[None], trans_b=True, tm=tb, tn=D_MODEL, out_dtype=F32, name=f"{tag}_dycat")
    d_wo = _mm_tn(R['ycat'][None], dm[None], tk=D_MODEL // 2, name=f"{tag}_dwout")[0]
    dy_lru, dy_sc, dy_rw = dycat[:, 0:256], dycat[:, 256:512], dycat[:, 512:]
    G = {}
    d_lg, d_hs, G['lru_norm_g'] = _rowwise(
        lambda gg, hh, ct, ng, b_: _vjp_rows(_lru_post, 1, 3, 1)(b_, gg, hh, ng, ct),
        [R['lg'], R['hs'], dy_lru], [_row(P['lru_norm_g']), bd], [(LRU_W, F32)] * 2, [(1, LRU_W)], tm=ts,
        name=f"{tag}_lru_dpost")
    d_la, d_lb = _lru_scan_bwd(_unshift(R['la'], 1), _shift(R['hs'], 1), d_hs, name=f"{tag}_lru_dscan")

    def lru_pre_bwd(x0, x1, x2, x3, ca, cb_, *par):
        return _vjp_rows(_lru_pre, 0, 14, 2)(x0, x1, x2, x3, *par, ca, cb_)

    par_shapes = [tuple(q.shape) for q in R['lru_par']]
    outs = _rowwise(lru_pre_bwd, R['lxs'] + [d_la, d_lb], R['lru_par'], [(LRU_W, F32)] * 4, par_shapes, tm=ts,
                    name=f"{tag}_lru_dpre")
    dxs, dpar = outs[:4], outs[4:]
    d_lx = _add_n([dxs[0], _unshift(dxs[1], 1), _unshift(dxs[2], 2), _unshift(dxs[3], 3)], tm=ts, name=f"{tag}_lru_dx")
    G['lru_conv_w'] = jnp.concatenate(dpar[0:4], axis=0)
    G['lru_conv_b'] = dpar[4][0]
    G['lru_wa'] = _blockdiag_grad(dpar[5])
    G['lru_ba'] = dpar[6][0]
    G['lru_wx'] = _blockdiag_grad(dpar[7])
    G['lru_bx'] = dpar[8][0]
    G['lru_lambda'] = dpar[9][0]
    G['lru_norm_g'] = G['lru_norm_g'][0]

    def sc_bwd(*vv):
        rows7, ct, par4, b_ = vv[:7], vv[7], vv[8:12], vv[12]
        return _vjp_rows(_sc_fwd, 1, 11, 1)(b_, *rows7, *par4, ct)

    outs = _rowwise(sc_bwd, R['sc_rows'] + [dy_sc], R['sc_par'] + [bd], [(SC_W, F32)] * 7, [(1, SC_W)] * 4, tm=ts,
                    name=f"{tag}_sc_bwd")
    d_sb = outs[0]
    d_sc = _add_n([outs[1], _unshift(outs[3], 1), _unshift(outs[5], 2)], tm=ts, name=f"{tag}_sc_dc")
    d_sx = _add_n([outs[2], _unshift(outs[4], 1), _unshift(outs[6], 2)], tm=ts, name=f"{tag}_sc_dx")
    G['sc_conv_w'] = jnp.concatenate(outs[7:10], axis=0)
    G['sc_norm_g'] = outs[10][0]

    def rw_post_bwd(*vv):
        rows5, ct, par3, b_ = vv[:5], vv[5], vv[6:9], vv[9]
        return _vjp_rows(_rw_post, 1, 8, 1)(b_, *rows5, *par3, ct)

    outs = _rowwise(rw_post_bwd, [R['y'], R['r'], R['k2'], R['v'], R['g'], dy_rw], R['post_par'] + [bd],
                    [(RW_W, F32)] * 5, [(1, RW_W)] * 3, tm=ts, name=f"{tag}_rw_dpost")
    d_y, dr_p, dk_p, dv_p, d_g = outs[:5]
    G['rwkv_lnx_w'], G['rwkv_lnx_b'], G['rwkv_r_k'] = outs[5][0], outs[6][0], outs[7][0]
    dyb = _bcast_cols(d_y, _stacked_bf16(bd), name=f"{tag}_rw_dycols")
    dr_s, d_w, dk_s, d_c, d_b, dvt, *rode = _rw_scan_bwd(R['r'], R['w'], R['k2'], R['c'], R['b'], R['vb'], dyb, R['spre'],
                                                        _stacked_bf16(bd), name=f"{tag}_rw_dscan", ride=ride)
    dv_s = dvt

    def rw_pre_bwd(*vv):
        zrows = vv[0:8]
        dr1, dr2, dw_, dk1, dk2_, dc_, db_, dv1, dv2, dg_ = vv[8:18]
        par, b_ = vv[18:29], vv[29]
        return _vjp_rows(_rw_pre, 1, 19, 7)(b_, *zrows, *par, dr1 + dr2, dw_, dk1 + dk2_, dc_, db_, dv1 + dv2, dg_)

    par_shapes = [tuple(q.shape) for q in R['rw_par']]
    widths = [(q.shape[1], F32) for q in R['z_rows']]
    outs = _rowwise(rw_pre_bwd, R['z_rows'] + [dr_p, dr_s, d_w, dk_p, dk_s, d_c, d_b, dv_p, dv_s, d_g],
                    R['rw_par'] + [bd], widths, par_shapes, tm=ts, name=f"{tag}_rw_dpre")
    d_z = _add_n([jnp.concatenate(outs[0:4], axis=1), _unshift(jnp.concatenate(outs[4:8], axis=1), 1)], tm=ts,
                 name=f"{tag}_rw_dz")
    dpar = outs[8:]
    G['rwkv_mu'] = jnp.concatenate([q[0] for q in dpar[0:4]])
    G['rwkv_w0'], G['rwkv_a0'] = dpar[4][0], dpar[6][0]
    G['rwkv_w2'], G['rwkv_a2'], G['rwkv_g2'] = dpar[5][0:32], dpar[7][32:64], dpar[8][64:128]
    G['rwkv_k_k'], G['rwkv_k_a'] = dpar[9][0], dpar[10][0]

    dp = jnp.concatenate([d_lx, d_lg, d_sb, d_sc, d_sx, d_z], axis=1).astype(BF16)
    da = _mm(dp[None], wi[None], trans_b=True, tm=tb, tn=D_MODEL, out_dtype=F32, name=f"{tag}_da")
    d_wi = _mm_tn(R['a'][None], dp[None], tk=D_MODEL // 2, name=f"{tag}_dwin")[0]
    dh, dg_pre = _norm_bwd(R['h'], g_pre, da, 1.0, dh_new, F32, tm=ts, name=f"{tag}_dnorm")
    return dh, dg_pre, dg_post, d_wi, d_wo, G, rode


def _loss_rows(h, tgt, n_seq, *, tm, name):
    d = h.shape[1]

    def body(h_ref, t_ref, dh_ref, l_ref):
        i = pl.program_id(0)
        row = lax.broadcasted_iota(jnp.int32, (tm, 1), 0) + i * tm
        live = (row >= N_META) & (row < N_META + n_seq)
        e = jnp.where(live, h_ref[...] - t_ref[...], 0.0)
        dh_ref[...] = e * (1.0 / d)
        part = 0.5 * jnp.sum(jnp.sum(e * e, axis=1, keepdims=True) * (1.0 / d), axis=0, keepdims=True)

        @pl.when(i == 0)
        def _():
            l_ref[...] = part

        @pl.when(i > 0)
        def _():
            l_ref[...] += part

    blk = pl.BlockSpec((tm, d), lambda i: (i, 0))
    return pl.pallas_call(body, name=name, grid=(h.shape[0] // tm,), in_specs=[blk, blk],
                          out_specs=[blk, pl.BlockSpec((1, 1), lambda i: (0, 0))],
                          out_shape=[jax.ShapeDtypeStruct(h.shape, F32), jax.ShapeDtypeStruct((1, 1), F32)])(h, tgt)


def _pack(arrs, mult):
    flat = jnp.concatenate([a.reshape(-1).astype(F32) for a in arrs])
    n = flat.shape[0]
    tot = -(-n // mult) * mult
    return jnp.pad(flat, (0, tot - n)).reshape(-1, LANES)


def _unpack(buf, shapes):
    flat = buf.reshape(-1)
    out, off = [], 0
    for s in shapes:
        n = 1
        for q in s:
            n *= q
        out.append(flat[off:off + n].reshape(s))
        off += n
    return out


def _step(W, M, V, x, loss_target):
    n_seq = x.shape[1]
    t_real = N_META + n_seq
    t = (t_real // CHUNK + 1) * CHUNK
    tiles = (_pick(t, (704, 512, 256, 128, 64)), _pick(t, (192, 128, 64)))
    me = 4 * lax.axis_index("x") + 2 * lax.axis_index("y") + lax.axis_index("c")
    n_layer = W['norm_g'].shape[0]

    small_sh = list(SMALL_SHARDED)
    packed = _pack([W[n] for n in small_sh], 8 * LANES)
    early = ['ffn1_w_in', 'ffn1_w_out', 'mix_w_in', 'mix_w_out']
    late = ['ffn2_w_in', 'ffn2_w_out']
    gathered = _exchange([W[n][0].astype(BF16) for n in early] + [packed], gather=True, name="gather_weights")
    big8 = [dict(zip(early, gathered[:-1]))] + [{} for _ in range(n_layer - 1)]
    pieces = [_unpack(gathered[-1][q], [W[n].shape for n in small_sh]) for q in range(N_DEV)]
    full = {n: W[n] for n in SMALL if n not in SMALL_SHARDED}
    for idx, n in enumerate(small_sh):
        full[n] = jnp.concatenate([pieces[q][idx] for q in range(N_DEV)], axis=SMALL_SHARDED[n])

    def ffn_weights(l, which):
        w24 = big8[l][f'{which}_w_in'].reshape(2, N_DEV // 2, D_MODEL, FFN_BLK)
        wo4 = big8[l][f'{which}_w_out'].reshape(N_DEV // 2, FFN_BLK, D_MODEL)
        return w24, wo4

    def mixer_weights(l):
        return big8[l]['mix_w_in'].transpose(1, 0, 2).reshape(D_MODEL, N_IN), big8[l]['mix_w_out'].reshape(D_MODEL, D_MODEL)

    bd = jnp.kron(jnp.eye(LANES // HEAD, dtype=F32), jnp.ones((HEAD, HEAD), F32))
    small_layer = [n for n in SMALL if n not in ('meta_tokens', 'norm_g')]

    h = jnp.concatenate([full['meta_tokens'], x[0], jnp.zeros((t - t_real, D_MODEL), F32)], axis=0)
    saved = []
    for l in range(n_layer):
        ng = [_row(full['norm_g'][l, q]) for q in range(6)]
        P = {n: full[n][l] for n in small_layer}
        w1 = ffn_weights(l, 'ffn1')
        h, r1 = _ffn_fwd(h, ng[0], ng[1], w1[0], w1[1], tiles, f"l{l}_ffn1")
        riders = [(l, n) for n in late] + ([(l + 1, n) for n in early] if l + 1 < n_layer else [])
        wm = mixer_weights(l)
        h, r2, rode = _mixer_fwd(h, ng[2], ng[3], wm[0], wm[1], P, bd, tiles, f"l{l}_mix",
                                 ride=([W[n][q].astype(BF16) for q, n in riders], True))
        for (q, n), arrived in zip(riders, rode):
            big8[q][n] = arrived
        w2 = ffn_weights(l, 'ffn2')
        h, r3 = _ffn_fwd(h, ng[4], ng[5], w2[0], w2[1], tiles, f"l{l}_ffn2")
        saved.append(((w1[0], w1[1], w2[0], w2[1], wm[0], wm[1]), ng, P, r1, r2, r3))

    tgt = jnp.pad(loss_target[0], ((N_META, t - t_real), (0, 0)))
    dh, loss_part = _loss_rows(h, tgt, n_seq, tm=tiles[1], name="loss")
    loss = lax.psum(loss_part[0, 0], MESH_AXES)

    small_grads = [None] * n_layer
    norm_grads = [None] * n_layer
    recv = [{} for _ in range(n_layer)]
    outgoing = []
    for l in reversed(range(n_layer)):
        lw, ng, P, r1, r2, r3 = saved[l]
        dh, g4, g5, d_win2, d_wo2 = _ffn_bwd(dh, r3, ng[4], ng[5], lw[2], lw[3], tiles, f"l{l}_ffn2")
        outgoing += [((l, 'ffn2_w_in'), d_win2), ((l, 'ffn2_w_out'), d_wo2)]
        dh, g2, g3, d_wi, d_wo, G, rode = _mixer_bwd(dh, r2, ng[2], ng[3], lw[4], lw[5], P, bd, tiles, f"l{l}_mix",
                                                     ride=([a for _, a in outgoing], False))
        for ((q, n), _), arrived in zip(outgoing, rode):
            recv[q][n] = arrived
        dh, g0, g1, d_win1, d_wo1 = _ffn_bwd(dh, r1, ng[0], ng[1], lw[0], lw[1], tiles, f"l{l}_ffn1")
        small_grads[l] = G
        norm_grads[l] = jnp.concatenate([g0, g1, g2, g3, g4, g5], axis=0)
        d_wi8 = d_wi.reshape(D_MODEL, N_DEV, N_IN // N_DEV).transpose(1, 0, 2)
        d_wo8 = d_wo.reshape(N_DEV, D_MODEL // N_DEV, D_MODEL)
        outgoing = [((l, 'ffn1_w_in'), d_win1), ((l, 'ffn1_w_out'), d_wo1), ((l, 'mix_w_in'), d_wi8), ((l, 'mix_w_out'), d_wo8)]
    gs = {n: jnp.stack([small_grads[l][n] for l in range(n_layer)]) for n in small_layer}
    gs['norm_g'] = jnp.stack(norm_grads)
    gs['meta_tokens'] = dh[:N_META]
    gpack = _pack([gs[n] for n in SMALL], 8 * LANES)
    *last, gall = _exchange([a for _, a in outgoing], gather=False, name="last_grad_exchange", gather_too=[gpack])
    for ((q, n), _), arrived in zip(outgoing, last):
        recv[q][n] = arrived
    gsum =_rowwise(lambda parts: _sum_slots(parts), [gall], [], [(LANES, F32)], [], tm=gall.shape[1],
                    name="sum_small_grads")[0]
    gfull = dict(zip(SMALL, _unpack(gsum, [gs[n].shape for n in SMALL])))

    def my_shard(n, a):
        if n not in SMALL_SHARDED:
            return a
        ax = SMALL_SHARDED[n]
        size = a.shape[ax] // N_DEV
        return lax.dynamic_slice_in_dim(a, me * size, size, axis=ax)

    g_loc = [my_shard(n, gfull[n]) for n in SMALL]
    shapes = [W[n].shape for n in SMALL]
    bufs = [_pack(g_loc, 8 * LANES)] + [_pack([D[n] for n in SMALL], 8 * LANES) for D in (W, M, V)]
    d_s, m_s, v_s = _rowwise(_adamw_rows, bufs, [], [(LANES, F32)] * 3, [], tm=bufs[0].shape[0], name="adamw_small")
    out = {'grad': dict(zip(SMALL, g_loc)), 'delta': dict(zip(SMALL, _unpack(d_s, shapes))),
           'm': dict(zip(SMALL, _unpack(m_s, shapes))), 'v': dict(zip(SMALL, _unpack(v_s, shapes)))}

    order = ['ffn1_w_in', 'ffn1_w_out', 'ffn2_w_in', 'ffn2_w_out', 'mix_w_in', 'mix_w_out']
    for idx, n in enumerate(order):
        per_layer = []
        for l in range(n_layer):
            parts = recv[l][n]
            rows, cols = W[n].shape[1], W[n].shape[2]
            per_layer.append(_reduce_adamw(parts.reshape(N_DEV, rows, cols), W[n][l], M[n][l], V[n][l],
                                           name=f"l{l}_adamw_{n}"))
        for q, key in enumerate(('grad', 'delta', 'm', 'v')):
            out[key][n] = jnp.stack([per_layer[l][q] for l in range(n_layer)])

    return (loss, dh[N_META:t_real][None],
            *[out['grad'][n] for n in WEIGHTS], *[out['delta'][n] for n in WEIGHTS],
            *[out['m'][n] for n in WEIGHTS], *[out['v'][n] for n in WEIGHTS])


def kernel(x, meta_tokens, norm_g, ffn1_w_in, ffn1_w_out, ffn2_w_in, ffn2_w_out, mix_w_in, mix_w_out, lru_conv_w, lru_conv_b, lru_wa, lru_ba, lru_wx, lru_bx, lru_lambda, lru_norm_g, sc_conv_w, sc_norm_g, rwkv_mu, rwkv_w0, rwkv_w2, rwkv_a0, rwkv_a2, rwkv_g2, rwkv_k_k, rwkv_k_a, rwkv_r_k, rwkv_lnx_w, rwkv_lnx_b, loss_target, m_meta_tokens, m_norm_g, m_ffn1_w_in, m_ffn1_w_out, m_ffn2_w_in, m_ffn2_w_out, m_mix_w_in, m_mix_w_out, m_lru_conv_w, m_lru_conv_b, m_lru_wa, m_lru_ba, m_lru_wx, m_lru_bx, m_lru_lambda, m_lru_norm_g, m_sc_conv_w, m_sc_norm_g, m_rwkv_mu, m_rwkv_w0, m_rwkv_w2, m_rwkv_a0, m_rwkv_a2, m_rwkv_g2, m_rwkv_k_k, m_rwkv_k_a, m_rwkv_r_k, m_rwkv_lnx_w, m_rwkv_lnx_b, v_meta_tokens, v_norm_g, v_ffn1_w_in, v_ffn1_w_out, v_ffn2_w_in, v_ffn2_w_out, v_mix_w_in, v_mix_w_out, v_lru_conv_w, v_lru_conv_b, v_lru_wa, v_lru_ba, v_lru_wx, v_lru_bx, v_lru_lambda, v_lru_norm_g, v_sc_conv_w, v_sc_norm_g, v_rwkv_mu, v_rwkv_w0, v_rwkv_w2, v_rwkv_a0, v_rwkv_a2, v_rwkv_g2, v_rwkv_k_k, v_rwkv_k_a, v_rwkv_r_k, v_rwkv_lnx_w, v_rwkv_lnx_b):
    w_vals = (meta_tokens, norm_g, ffn1_w_in, ffn1_w_out, ffn2_w_in, ffn2_w_out, mix_w_in, mix_w_out, lru_conv_w, lru_conv_b, lru_wa, lru_ba, lru_wx, lru_bx, lru_lambda, lru_norm_g, sc_conv_w, sc_norm_g, rwkv_mu, rwkv_w0, rwkv_w2, rwkv_a0, rwkv_a2, rwkv_g2, rwkv_k_k, rwkv_k_a, rwkv_r_k, rwkv_lnx_w, rwkv_lnx_b)
    m_vals = (m_meta_tokens, m_norm_g, m_ffn1_w_in, m_ffn1_w_out, m_ffn2_w_in, m_ffn2_w_out, m_mix_w_in, m_mix_w_out, m_lru_conv_w, m_lru_conv_b, m_lru_wa, m_lru_ba, m_lru_wx, m_lru_bx, m_lru_lambda, m_lru_norm_g, m_sc_conv_w, m_sc_norm_g, m_rwkv_mu, m_rwkv_w0, m_rwkv_w2, m_rwkv_a0, m_rwkv_a2, m_rwkv_g2, m_rwkv_k_k, m_rwkv_k_a, m_rwkv_r_k, m_rwkv_lnx_w, m_rwkv_lnx_b)
    v_vals = (v_meta_tokens, v_norm_g, v_ffn1_w_in, v_ffn1_w_out, v_ffn2_w_in, v_ffn2_w_out, v_mix_w_in, v_mix_w_out, v_lru_conv_w, v_lru_conv_b, v_lru_wa, v_lru_ba, v_lru_wx, v_lru_bx, v_lru_lambda, v_lru_norm_g, v_sc_conv_w, v_sc_norm_g, v_rwkv_mu, v_rwkv_w0, v_rwkv_w2, v_rwkv_a0, v_rwkv_a2, v_rwkv_g2, v_rwkv_k_k, v_rwkv_k_a, v_rwkv_r_k, v_rwkv_lnx_w, v_rwkv_lnx_b)
    return _step(dict(zip(WEIGHTS, w_vals)), dict(zip(WEIGHTS, m_vals)), dict(zip(WEIGHTS, v_vals)), x, loss_target)
```

```python
import functools

import jax
import jax.numpy as jnp
from jax import lax
from jax.experimental import pallas as pl
from jax.experimental.pallas import tpu as pltpu

F32 = jnp.float32
BF16 = jnp.bfloat16
PARAM_DOT = lax.Precision.DEFAULT

N_DEV = 8
MESH_AXES = ("x", "y", "c")
N_META = 16
D_MODEL = 1024
LRU_W = 256
SC_W = 256
RW_W = 512
HEAD = 64
LANES = 128
CHUNK = 64
RW_IN = 1664
N_IN = 2944
FFN_BLK = 704
RMS_EPS = 1e-6
LNX_EPS = 64e-5
LRU_C = 8.0
ADAM_LR, ADAM_B1, ADAM_B2, ADAM_EPS, ADAM_WD, ADAM_STEP = 0.001, 0.9, 0.999, 1e-08, 0.01, 10

WEIGHTS = ['meta_tokens', 'norm_g', 'ffn1_w_in', 'ffn1_w_out', 'ffn2_w_in', 'ffn2_w_out', 'mix_w_in', 'mix_w_out',
           'lru_conv_w', 'lru_conv_b', 'lru_wa', 'lru_ba', 'lru_wx', 'lru_bx', 'lru_lambda', 'lru_norm_g',
           'sc_conv_w', 'sc_norm_g', 'rwkv_mu', 'rwkv_w0', 'rwkv_w2', 'rwkv_a0', 'rwkv_a2', 'rwkv_g2', 'rwkv_k_k',
           'rwkv_k_a', 'rwkv_r_k', 'rwkv_lnx_w', 'rwkv_lnx_b']
BIG = ['ffn1_w_in', 'ffn1_w_out', 'ffn2_w_in', 'ffn2_w_out', 'mix_w_in', 'mix_w_out']
SMALL_SHARDED = {'meta_tokens': 1, 'norm_g': 2, 'lru_conv_w': 2, 'sc_conv_w': 2, 'rwkv_w2': 2, 'rwkv_a2': 2, 'rwkv_g2': 2}
SMALL = [n for n in WEIGHTS if n not in BIG]


def _pick(n, cands):
    for c in cands:
        if n % c == 0:
            return c
    raise ValueError(f"no tile for {n}")


def _rowwise(fn, rows, params, row_outs, acc_outs, *, tm, name):
    nr, npar, nro, nao = len(rows), len(params), len(row_outs), len(acc_outs)
    n_rows = rows[0].shape[-2]
    assert n_rows % tm == 0, (name, n_rows, tm)

    def body(*refs):
        vals = [r[...] for r in refs[:nr + npar]]
        outs = fn(*vals)
        if not isinstance(outs, (tuple, list)):
            outs = (outs,)
        assert len(outs) == nro + nao, (name, len(outs))
        for o_ref, o in zip(refs[nr + npar:nr + npar + nro], outs[:nro]):
            o_ref[...] = o.astype(o_ref.dtype)
        step = pl.program_id(0)
        for a_ref, a in zip(refs[nr + npar + nro:], outs[nro:]):
            @pl.when(step == 0)
            def _(a_ref=a_ref, a=a):
                a_ref[...] = a.astype(F32)

            @pl.when(step > 0)
            def _(a_ref=a_ref, a=a):
                a_ref[...] += a.astype(F32)

    def row_spec(shape):
        if len(shape) == 2:
            return pl.BlockSpec((tm, shape[1]), lambda i: (i, 0))
        return pl.BlockSpec((shape[0], tm, shape[2]), lambda i: (0, i, 0))

    def full_spec(shape):
        nd = len(shape)
        return pl.BlockSpec(tuple(shape), lambda i, nd=nd: (0,) * nd)

    in_specs = [row_spec(r.shape) for r in rows] + [full_spec(p.shape) for p in params]
    out_shape = [jax.ShapeDtypeStruct((n_rows, w), dt) for (w, dt) in row_outs]
    out_shape += [jax.ShapeDtypeStruct(tuple(s), F32) for s in acc_outs]
    out_specs = [row_spec((n_rows, w)) for (w, _) in row_outs] + [full_spec(s) for s in acc_outs]
    res = pl.pallas_call(body, name=name, grid=(n_rows // tm,), in_specs=in_specs, out_specs=out_specs,
                         out_shape=out_shape)(*rows, *params)
    return tuple(res)


def _mm(a3, b3, *, trans_b, tm, tn, out_dtype, name):
    nj, m, kb = a3.shape
    n = b3.shape[1] if trans_b else b3.shape[2]
    dims = (((1,), (1,)), ((), ())) if trans_b else (((1,), (0,)), ((), ()))

    def body(a_ref, b_ref, o_ref):
        acc = lax.dot_general(a_ref[0], b_ref[0], dims, preferred_element_type=F32)
        for j in range(1, nj):
            acc = acc + lax.dot_general(a_ref[j], b_ref[j], dims, preferred_element_type=F32)
        o_ref[...] = acc.astype(o_ref.dtype)

    if trans_b:
        b_spec = pl.BlockSpec((nj, tn, kb), lambda i, c: (0, c, 0))
    else:
        b_spec = pl.BlockSpec((nj, kb, tn), lambda i, c: (0, 0, c))
    return pl.pallas_call(
        body, name=name, grid=(m // tm, n // tn),
        in_specs=[pl.BlockSpec((nj, tm, kb), lambda i, c: (0, i, 0)), b_spec],
        out_specs=pl.BlockSpec((tm, tn), lambda i, c: (i, c)),
        out_shape=jax.ShapeDtypeStruct((m, n), out_dtype),
    )(a3, b3)


def _mm_tn(a3, b3, *, tk, name):
    ja, t, ka = a3.shape
    jb, _, n = b3.shape
    nj = max(ja, jb)

    def body(a_ref, b_ref, o_ref):
        o_ref[0] = lax.dot_general(a_ref[0], b_ref[0], (((0,), (0,)), ((), ())),
                                   preferred_element_type=F32).astype(o_ref.dtype)

    return pl.pallas_call(
        body, name=name, grid=(nj, ka // tk),
        in_specs=[pl.BlockSpec((1, t, tk), (lambda j, c: (j, 0, c)) if ja > 1 else (lambda j, c: (0, 0, c))),
                  pl.BlockSpec((1, t, n), (lambda j, c: (j, 0, 0)) if jb > 1 else (lambda j, c: (0, 0, 0)))],
        out_specs=pl.BlockSpec((1, tk, n), lambda j, c: (j, c, 0)),
        out_shape=jax.ShapeDtypeStruct((nj, ka, n), BF16),
    )(a3, b3)


def _ffn_in(a, w24, *, tm, name):
    t, d = a.shape
    nb, fb = w24.shape[1], w24.shape[3]

    def body(a_ref, w_ref, gu_ref, s_ref):
        x = a_ref[...]
        g = jnp.dot(x, w_ref[0, 0], preferred_element_type=F32)
        u = jnp.dot(x, w_ref[1, 0], preferred_element_type=F32)
        gu_ref[0, 0] = g.astype(BF16)
        gu_ref[1, 0] = u.astype(BF16)
        s_ref[0] = (g * jax.nn.sigmoid(g) * u).astype(BF16)

    return pl.pallas_call(
        body, name=name, grid=(nb, t // tm),
        in_specs=[pl.BlockSpec((tm, d), lambda j, i: (i, 0)), pl.BlockSpec((2, 1, d, fb), lambda j, i: (0, j, 0, 0))],
        out_specs=[pl.BlockSpec((2, 1, tm, fb), lambda j, i: (0, j, i, 0)), pl.BlockSpec((1, tm, fb), lambda j, i: (j, i, 0))],
        out_shape=[jax.ShapeDtypeStruct((2, nb, t, fb), BF16), jax.ShapeDtypeStruct((nb, t, fb), BF16)],
    )(a, w24)


def _ffn_dswiglu(df, wo4, gu, *, tm, name):
    t, d = df.shape
    nb, fb = wo4.shape[0], wo4.shape[1]

    def body(df_ref, wo_ref, gu_ref, dg_ref):
        ds = lax.dot_general(df_ref[...], wo_ref[0], (((1,), (1,)), ((), ())), preferred_element_type=F32)
        g = gu_ref[0, 0].astype(F32)
        u = gu_ref[1, 0].astype(F32)
        sig = jax.nn.sigmoid(g)
        dg_ref[0, 0] = (ds * u * sig * (1.0 + g * (1.0 - sig))).astype(BF16)
        dg_ref[1, 0] = (ds * g * sig).astype(BF16)

    return pl.pallas_call(
        body, name=name, grid=(nb, t // tm),
        in_specs=[pl.BlockSpec((tm, d), lambda j, i: (i, 0)), pl.BlockSpec((1, fb, d), lambda j, i: (j, 0, 0)),
                  pl.BlockSpec((2, 1, tm, fb), lambda j, i: (0, j, i, 0))],
        out_specs=pl.BlockSpec((2, 1, tm, fb), lambda j, i: (0, j, i, 0)),
        out_shape=jax.ShapeDtypeStruct((2, nb, t, fb), BF16),
    )(df, wo4, gu)


def _rms(x, g):
    return x * lax.rsqrt(jnp.mean(x * x, axis=-1, keepdims=True) + RMS_EPS) * g


def _rms_bwd(x, g, dy):
    rstd = lax.rsqrt(jnp.mean(x * x, axis=-1, keepdims=True) + RMS_EPS)
    xh = x * rstd
    dxh = dy * g
    dx = rstd * (dxh - xh * jnp.mean(dxh * xh, axis=-1, keepdims=True))
    return dx, jnp.sum(dy * xh, axis=0, keepdims=True)


def _seg_sum_impl(x, bd):
    bd2 = jnp.concatenate([bd, bd], axis=0).astype(BF16)
    hi = x.astype(BF16)
    lo = (x - hi.astype(F32)).astype(BF16)
    parts = [jnp.dot(jnp.concatenate([hi[:, q * LANES:(q + 1) * LANES], lo[:, q * LANES:(q + 1) * LANES]], axis=1), bd2,
                     preferred_element_type=F32) for q in range(x.shape[1] // LANES)]
    return parts[0] if len(parts) == 1 else jnp.concatenate(parts, axis=1)


@jax.custom_vjp
def _seg_sum(x, bd):
    return _seg_sum_impl(x, bd)


def _seg_sum_fwd(x, bd):
    return _seg_sum_impl(x, bd), bd


def _seg_sum_bwd(bd, ct):
    return _seg_sum_impl(ct, bd), jnp.zeros_like(bd)


_seg_sum.defvjp(_seg_sum_fwd, _seg_sum_bwd)


def _group_rms(y, g, bd):
    return y * lax.rsqrt(_seg_sum(y * y, bd) * (1.0 / HEAD) + RMS_EPS) * g


def _expm1(x):
    return jnp.where(jnp.abs(x) < 1e-2, x * (1.0 + x * (0.5 + x * (1.0 / 6.0))), jnp.exp(x) - 1.0)


def _lru_pre(x0, x1, x2, x3, cw0, cw1, cw2, cw3, cb, wa, ba, wx, bx, lam):
    u = x3 * cw0 + x2 * cw1 + x1 * cw2 + x0 * cw3 + cb
    r = jax.nn.sigmoid(jnp.dot(u, wa, preferred_element_type=F32, precision=PARAM_DOT) + ba)
    i = jax.nn.sigmoid(jnp.dot(u, wx, preferred_element_type=F32, precision=PARAM_DOT) + bx)
    log_a = -LRU_C * r * jax.nn.softplus(-lam)
    return jnp.exp(log_a), jnp.sqrt(-_expm1(2.0 * log_a)) * (i * u)


def _lru_post(bd, gate, hs, ng):
    return _group_rms(jax.nn.gelu(gate) * hs, ng, bd)


def _sc_fwd(bd, b, c0, x0, c1, x1, c2, x2, w0, w1, w2, ng):
    return _group_rms(b * (w0 * (c2 * x2) + w1 * (c1 * x1) + w2 * (c0 * x0)), ng, bd)


def _rw_pre(bd, zr, zk, zv, zt, sr, sk, sv, st, mur, muk, muv, mut, w0, w2p, a0, a2p, g2p, k_k, k_a):
    r, k, v, tail = zr + (sr - zr) * mur, zk + (sk - zk) * muk, zv + (sv - zv) * muv, zt + (st - zt) * mut
    lane = lax.broadcasted_iota(jnp.int32, tail.shape, 1)
    act = jnp.where(lane < 32, jnp.tanh(tail), jnp.where(lane < 64, tail, jax.nn.sigmoid(tail)))
    dot = functools.partial(jnp.dot, preferred_element_type=F32, precision=PARAM_DOT)
    w_log = -jax.nn.softplus(-(w0 + dot(act, w2p))) - 0.5
    w = jnp.exp(-jnp.exp(w_log))
    a = jax.nn.sigmoid(a0 + dot(act, a2p))
    g = dot(act, g2p)
    kk = k * k_k
    k2 = k * (1.0 + (a - 1.0) * k_a)
    kkn = kk * lax.rsqrt(jnp.maximum(_seg_sum(kk * kk, bd), 1e-24))
    return r, w, k2, -kkn, kkn * a, v, g


def _rw_post(bd, y, r, k2, v, g, lnw, lnb, r_k):
    mean = _seg_sum(y, bd) * (1.0 / HEAD)
    yc = y - mean
    var = _seg_sum(yc * yc, bd) * (1.0 / HEAD)
    yn = yc * lax.rsqrt(var + LNX_EPS) * lnw + lnb
    return (yn + _seg_sum(r * k2 * r_k, bd) * v) * g


def _vjp_rows(fwd, n_static, n_in, n_ct):
    def fn(*args):
        static, prim, cts = args[:n_static], args[n_static:n_static + n_in], args[n_static + n_in:]
        assert len(cts) == n_ct
        _, vjp = jax.vjp(functools.partial(fwd, *static), *prim)
        return vjp(cts[0] if n_ct == 1 else tuple(cts))
    return fn


def _all_to_all_copies(x_refs, o_refs, sems):
    send_sems, recv_sems, local_sems = sems
    mx, my, mc = lax.axis_index("x"), lax.axis_index("y"), lax.axis_index("c")
    me = 4 * mx + 2 * my + mc
    local, sends, recvs = [], [], []
    for k in range(len(x_refs)):
        local.append(pltpu.make_async_copy(x_refs[k].at[me], o_refs[k].at[me], local_sems.at[k]))
    for d in range(1, N_DEV):
        px, py, pc = mx ^ ((d >> 2) & 1), my ^ ((d >> 1) & 1), mc ^ (d & 1)
        peer = 4 * px + 2 * py + pc
        for k in range(len(x_refs)):
            common = dict(src_ref=x_refs[k].at[peer], send_sem=send_sems.at[k, d - 1], recv_sem=recv_sems.at[k, d - 1],
                          device_id=(px, py, pc), device_id_type=pl.DeviceIdType.MESH)
            sends.append(pltpu.make_async_remote_copy(dst_ref=o_refs[k].at[me], **common))
            recvs.append(pltpu.make_async_remote_copy(dst_ref=o_refs[k].at[peer], **common))
    return local, sends, recvs


def _gather2_copies(x_refs, o_refs, sems):
    send_sems, recv_sems, local_sems = sems
    mx, my, mc = lax.axis_index("x"), lax.axis_index("y"), lax.axis_index("c")
    sibling = (mx, my, 1 - mc)
    chips = [(1 - mx, my), (mx, 1 - my), (1 - mx, 1 - my)]

    def slot(px, py, pc):
        return 4 * px + 2 * py + pc

    out = dict(local=[], first=[], first_recv=[], ici_recv=[], passed=[], passed_recv=[])
    for k in range(len(x_refs)):
        def copy(sem, src, dst_slot, to, k=k):
            return pltpu.make_async_remote_copy(src_ref=src, dst_ref=o_refs[k].at[dst_slot], send_sem=send_sems.at[k, sem],
                                                recv_sem=recv_sems.at[k, sem], device_id=to, device_id_type=pl.DeviceIdType.MESH)
        me = slot(mx, my, mc)
        out['local'].append(pltpu.make_async_copy(x_refs[k], o_refs[k].at[me], local_sems.at[k]))
        out['first'].append(copy(0, x_refs[k], me, sibling))
        out['first_recv'].append(copy(0, x_refs[k], slot(mx, my, 1 - mc), sibling))
        for j, (px, py) in enumerate(chips):
            out['first'].append(copy(1 + j, x_refs[k], me, (px, py, mc)))
            out['ici_recv'].append(copy(1 + j, x_refs[k], slot(px, py, mc), (px, py, mc)))
            out['passed'].append(copy(4 + j, o_refs[k].at[slot(px, py, mc)], slot(px, py, mc), sibling))
            out['passed_recv'].append(copy(4 + j, x_refs[k], slot(px, py, 1 - mc), sibling))
    return out


def _exchange_start(x_refs, o_refs, sems, gather):
    if gather:
        cps = _gather2_copies(x_refs, o_refs, sems)
        for cp in cps['local'] + cps['first']:
            cp.start()
        return
    local, sends, _ = _all_to_all_copies(x_refs, o_refs, sems)
    for cp in local + sends:
        cp.start()


def _exchange_wait(x_refs, o_refs, sems, gather):
    if gather:
        cps = _gather2_copies(x_refs, o_refs, sems)
        for arrived, onward in zip(cps['ici_recv'], cps['passed']):
            arrived.wait_recv()
            onward.start()
        for cp in cps['first'] + cps['passed']:
            cp.wait_send()
        for cp in cps['first_recv'] + cps['passed_recv']:
            cp.wait_recv()
        for cp in cps['local']:
            cp.wait()
        return
    local, sends, recvs = _all_to_all_copies(x_refs, o_refs, sems)
    for cp in sends:
        cp.wait_send()
    for cp in recvs:
        cp.wait_recv()
    for cp in local:
        cp.wait()


def _exchange_out_shape(xs, gather):
    return [jax.ShapeDtypeStruct(((N_DEV,) + x.shape) if gather else x.shape, x.dtype) for x in xs]


def _exchange_sems(n):
    return [pltpu.SemaphoreType.DMA((n, N_DEV - 1)), pltpu.SemaphoreType.DMA((n, N_DEV - 1)), pltpu.SemaphoreType.DMA((n,))]


SUBLANES = 8


def _store_row(ref, i, cols, row):
    base = pl.multiple_of((i // SUBLANES) * SUBLANES, SUBLANES)
    sub = lax.broadcasted_iota(jnp.int32, (SUBLANES, row.shape[1]), 0)
    ref[pl.ds(base, SUBLANES), cols] = jnp.where(sub == i % SUBLANES, row, ref[pl.ds(base, SUBLANES), cols])


def _tile_scan(a, b, reverse):
    sub = lax.broadcasted_iota(jnp.int32, a.shape, 0)
    for sh in (1, 2, 4):
        if reverse:
            live = sub < SUBLANES - sh
            a_s, b_s = pltpu.roll(a, SUBLANES - sh, 0), pltpu.roll(b, SUBLANES - sh, 0)
        else:
            live = sub >= sh
            a_s, b_s = pltpu.roll(a, sh, 0), pltpu.roll(b, sh, 0)
        b = jnp.where(live, a * b_s, 0.0) + b
        a = jnp.where(live, a * a_s, a)
    return a, b


def _lru_scan(a, b, name):
    t, w = a.shape

    def body(a_ref, b_ref, h_ref):
        def tile(j, h):
            rows = pl.ds(pl.multiple_of(j * SUBLANES, SUBLANES), SUBLANES)
            ca, cb = _tile_scan(a_ref[rows, :], b_ref[rows, :], False)
            out = ca * h + cb
            h_ref[rows, :] = out
            return out[SUBLANES - 1:SUBLANES]
        lax.fori_loop(0, t // SUBLANES, tile, jnp.zeros((1, w), F32))

    return pl.pallas_call(body, name=name, out_shape=jax.ShapeDtypeStruct((t, w), F32))(a, b)


def _lru_scan_bwd(a_next, h_prev, dhs, name):
    t, w = dhs.shape

    def body(a_ref, h_ref, dh_ref, da_ref, db_ref):
        def tile(n, lam):
            rows = pl.ds(pl.multiple_of((t // SUBLANES - 1 - n) * SUBLANES, SUBLANES), SUBLANES)
            ca, cb = _tile_scan(a_ref[rows, :], dh_ref[rows, :], True)
            out = ca * lam + cb
            db_ref[rows, :] = out
            da_ref[rows, :] = out * h_ref[rows, :]
            return out[0:1]
        lax.fori_loop(0, t // SUBLANES, tile, jnp.zeros((1, w), F32))

    return pl.pallas_call(body, name=name, out_shape=[jax.ShapeDtypeStruct((t, w), F32)] * 2)(a_next, h_prev, dhs)


N_PAIR = RW_W // LANES


def _bcast_cols(v, bd2, name):
    t = v.shape[0]

    def body(v_ref, bd_ref, o_ref):
        bdv = bd_ref[...]
        sub = lax.broadcasted_iota(jnp.int32, (HEAD, LANES), 0)
        own = lax.broadcasted_iota(jnp.int32, (HEAD, LANES), 1) % HEAD == sub
        for i in range(CHUNK):
            row = v_ref[i:i + 1, :]
            sums = _group_sums([jnp.where(own, row[:, p * LANES:(p + 1) * LANES], 0.0) for p in range(N_PAIR)], bdv)
            for p in range(N_PAIR):
                o_ref[i, p] = sums[p * HEAD:(p + 1) * HEAD]

    return pl.pallas_call(
        body, name=name, grid=(t // CHUNK,),
        in_specs=[pl.BlockSpec((CHUNK, RW_W), lambda i: (i, 0)), pl.BlockSpec((2 * LANES, LANES), lambda i: (0, 0))],
        out_specs=pl.BlockSpec((CHUNK, N_PAIR, HEAD, LANES), lambda i: (i, 0, 0, 0)),
        out_shape=jax.ShapeDtypeStruct((t, N_PAIR, HEAD, LANES), F32),
    )(v, bd2)


def _cols_to_rows(cols_ref, rows_ref):
    lane = lax.broadcasted_iota(jnp.int32, (CHUNK, LANES), 1)
    for p in range(N_PAIR):
        tile = cols_ref[p]
        sq = jnp.concatenate([tile, jnp.zeros_like(tile)], axis=0).T
        rows_ref[:, p * LANES:(p + 1) * LANES] = jnp.where(lane < HEAD, sq[0:CHUNK], pltpu.roll(sq[CHUNK:2 * CHUNK], HEAD, 1))


def _stacked_bf16(bd):
    return jnp.concatenate([bd, bd], axis=0).astype(BF16)


def _group_sums(prods, bd2):
    x = jnp.concatenate(prods, axis=0)
    hi = x.astype(BF16)
    lo = (x - hi.astype(F32)).astype(BF16)
    return jnp.dot(jnp.concatenate([hi, lo], axis=1), bd2, preferred_element_type=F32)


def _rw_scan(r, r_prev, w, k, c, c_next, b, vb, bd2, name, ride=None):
    t = w.shape[0]
    nch = t // CHUNK
    ride_xs, ride_gather = ride if ride is not None else ([], False)
    n_ride = len(ride_xs)

    def body(*refs):
        r_ref, rp_ref, w_ref, k_ref, c_ref, cn_ref, b_ref, vb_ref, bd_ref = refs[:9]
        x_refs = refs[9:9 + n_ride]
        spre_ref, y_ref = refs[9 + n_ride:11 + n_ride]
        o_refs = refs[11 + n_ride:11 + 2 * n_ride]
        s_ref, wc_ref, wr_ref, bc_ref, kc_ref, br_ref, kr_ref, yt_ref = refs[11 + 2 * n_ride:19 + 2 * n_ride]
        sems = refs[19 + 2 * n_ride:]

        @pl.when(pl.program_id(0) == 0)
        def _():
            s_ref[...] = jnp.zeros_like(s_ref)
            if n_ride:
                _exchange_start(x_refs, o_refs, sems, ride_gather)

        yt_ref[...] = jnp.zeros_like(yt_ref)
        bdv = bd_ref[...]
        lane = lax.broadcasted_iota(jnp.int32, (HEAD, LANES), 1) % CHUNK

        wv, cn, rv, bv, kv = w_ref[...], cn_ref[...], r_ref[...], b_ref[...], k_ref[...]
        wc_ref[...] = wv * cn
        wr_ref[...] = wv * rv
        for ref, x in ((bc_ref, bv * cn), (kc_ref, kv * cn), (br_ref, bv * rv), (kr_ref, kv * rv)):
            sums = _group_sums([x[:, q * LANES:(q + 1) * LANES] for q in range(N_PAIR)], bdv)
            for q in range(N_PAIR):
                ref[:, q * LANES:(q + 1) * LANES] = sums[q * CHUNK:(q + 1) * CHUNK]

        def cut(ref, i):
            x = ref[pl.ds(i, 1), :]
            return [x[:, p * LANES:(p + 1) * LANES] for p in range(N_PAIR)]

        def two_steps(j, st):
            i0 = 2 * j
            i1 = i0 + 1
            c0, wc0, rp0, wr0 = cut(c_ref, i0), cut(wc_ref, i0), cut(rp_ref, i0), cut(wr_ref, i0)
            w0, b0, k0, w1, b1, k1 = cut(w_ref, i0), cut(b_ref, i0), cut(k_ref, i0), cut(w_ref, i1), cut(b_ref, i1), cut(k_ref, i1)
            bc0, kc0, br0, kr0 = cut(bc_ref, i0), cut(kc_ref, i0), cut(br_ref, i0), cut(kr_ref, i0)
            pairs = range(N_PAIR)
            red = _group_sums([st[p] * c0[p] for p in pairs] + [st[p] * wc0[p] for p in pairs], bdv)
            out = _group_sums([st[p] * rp0[p] for p in pairs] + [st[p] * wr0[p] for p in pairs], bdv)
            new = []
            for p in pairs:
                v0, v1 = vb_ref[i0, p], vb_ref[i1, p]
                sa0 = red[p * HEAD:(p + 1) * HEAD]
                sa1 = red[(N_PAIR + p) * HEAD:(N_PAIR + p + 1) * HEAD] + sa0 * bc0[p] + v0 * kc0[p]
                y_before = out[p * HEAD:(p + 1) * HEAD]
                y0 = out[(N_PAIR + p) * HEAD:(N_PAIR + p + 1) * HEAD] + sa0 * br0[p] + v0 * kr0[p]
                spre_ref[i0, p] = st[p]
                s1 = st[p] * w0[p] + sa0 * b0[p] + v0 * k0[p]
                spre_ref[i1, p] = s1
                new.append(s1 * w1[p] + sa1 * b1[p] + v1 * k1[p])
                yt_ref[p] = jnp.where(lane == i0, y_before, jnp.where(lane == i1, y0, yt_ref[p]))
            return tuple(new)

        st = lax.fori_loop(0, CHUNK // 2, two_steps, tuple(s_ref[p] for p in range(N_PAIR)))
        for p in range(N_PAIR):
            s_ref[p] = st[p]
        _cols_to_rows(yt_ref, y_ref)

        if n_ride:
            @pl.when(pl.program_id(0) == nch - 1)
            def _():
                _exchange_wait(x_refs, o_refs, sems, ride_gather)

    row = pl.BlockSpec((CHUNK, RW_W), lambda i: (i, 0))
    big = pl.BlockSpec((CHUNK, N_PAIR, HEAD, LANES), lambda i: (i, 0, 0, 0))
    any_spec = pl.BlockSpec(memory_space=pl.ANY)
    return pl.pallas_call(
        body, name=name, grid=(nch,),
        in_specs=[row] * 7 + [big, pl.BlockSpec((2 * LANES, LANES), lambda i: (0, 0))] + [any_spec] * n_ride,
        out_specs=[big, row] + [any_spec] * n_ride,
        out_shape=[jax.ShapeDtypeStruct((t, N_PAIR, HEAD, LANES), F32), jax.ShapeDtypeStruct((t, RW_W), F32)]
        + _exchange_out_shape(ride_xs, ride_gather),
        scratch_shapes=[pltpu.VMEM((N_PAIR, HEAD, LANES), F32)] + [pltpu.VMEM((CHUNK, RW_W), F32)] * 6
        + [pltpu.VMEM((N_PAIR, HEAD, LANES), F32)] + (_exchange_sems(n_ride) if n_ride else []),
    )(r, r_prev, w, k, c, c_next, b, vb, bd2, *ride_xs)


def _rw_scan_bwd(r, w, k, c, b, vb, dyb, spre, bd, name, ride=None):
    t = r.shape[0]
    nch = t // CHUNK
    ride_xs, ride_gather = ride if ride is not None else ([], False)
    n_ride = len(ride_xs)
    n_pre = 8

    def body(*refs):
        r_ref, w_ref, k_ref, c_ref, b_ref, vb_ref, dyb_ref, spre_ref, bd_ref = refs[:9]
        x_refs = refs[9:9 + n_ride]
        dr_ref, dw_ref, dk_ref, dc_ref, db_ref, dv_ref = refs[9 + n_ride:15 + n_ride]
        o_refs = refs[15 + n_ride:15 + 2 * n_ride]
        g_ref, snext_ref, dvt_ref = refs[15 + 2 * n_ride:18 + 2 * n_ride]
        pre = refs[18 + 2 * n_ride:18 + 2 * n_ride + n_pre]
        sems = refs[18 + 2 * n_ride + n_pre:]
        wb_ref, wk_ref, rb_ref, rk_ref, rwb_ref, cb_ref, rwk_ref, ck_ref = pre[:8]

        @pl.when(pl.program_id(0) == 0)
        def _():
            g_ref[...] = jnp.zeros_like(g_ref)
            snext_ref[...] = jnp.zeros_like(snext_ref)
            if n_ride:
                _exchange_start(x_refs, o_refs, sems, ride_gather)

        for ref in (dr_ref, dw_ref, dk_ref, dc_ref, db_ref, dvt_ref):
            ref[...] = jnp.zeros_like(ref)
        bdv = bd_ref[...]
        bd1 = bdv[0:LANES]
        lane = lax.broadcasted_iota(jnp.int32, (HEAD, LANES), 1) % CHUNK

        rv, wv, kv, cv, bv = r_ref[...], w_ref[...], k_ref[...], c_ref[...], b_ref[...]
        b_b, k_b = pltpu.roll(bv, 1, 0), pltpu.roll(kv, 1, 0)
        wb_ref[...] = wv * b_b
        wk_ref[...] = wv * k_b
        rw = rv * wv
        for ref, x in ((rb_ref, rv * bv), (rk_ref, rv * kv), (rwb_ref, rw * b_b), (cb_ref, cv * b_b), (rwk_ref, rw * k_b),
                       (ck_ref, cv * k_b)):
            sums = _group_sums([x[:, q * LANES:(q + 1) * LANES] for q in range(N_PAIR)], bdv)
            for q in range(N_PAIR):
                ref[:, q * LANES:(q + 1) * LANES] = sums[q * CHUNK:(q + 1) * CHUNK]

        def sum0(x):
            return jnp.sum(x, axis=0, keepdims=True)

        def cut(ref, i):
            x = ref[pl.ds(i, 1), :]
            return [x[:, p * LANES:(p + 1) * LANES] for p in range(N_PAIR)]

        def sums_bf16(prods):
            return jnp.dot(jnp.concatenate(prods, axis=0).astype(BF16), bd1, preferred_element_type=F32)

        def two_steps(n, gs):
            ia = CHUNK - 1 - 2 * n
            ib = ia - 1
            r_a, w_a, k_a, c_a, b_a = [cut(ref, ia) for ref in (r_ref, w_ref, k_ref, c_ref, b_ref)]
            r_b, w_b, c_b = [cut(ref, ib) for ref in (r_ref, w_ref, c_ref)]
            wb, wk, rb_a, rk_a, rwb, cb, rwk, ck = [cut(ref, ia) for ref in pre[:8]]
            rb_b, rk_b = cut(rb_ref, ib), cut(rk_ref, ib)
            pairs = range(N_PAIR)
            sp_a = [spre_ref[ia, p] for p in pairs]
            sp_b = [spre_ref[ib, p] for p in pairs]
            dy_a = [dyb_ref[ia, p] for p in pairs]
            dy_b = [dyb_ref[ib, p] for p in pairs]
            chain = _group_sums([gs[p] * b_a[p] for p in pairs] + [gs[p] * wb[p] for p in pairs], bdv)
            off = sums_bf16([gs[p] * k_a[p] for p in pairs] + [gs[p] * wk[p] for p in pairs]
                            + [sp_a[p] * c_a[p] for p in pairs] + [sp_b[p] * c_b[p] for p in pairs])
            new = []
            for p in pairs:
                def part(x, q, p=p):
                    return x[(q * N_PAIR + p) * HEAD:(q * N_PAIR + p + 1) * HEAD]
                dsa_a = part(chain, 0) + dy_a[p] * rb_a[p]
                dv_a = part(off, 0) + dy_a[p] * rk_a[p]
                dsa_b = part(chain, 1) + dy_a[p] * rwb[p] + dsa_a * cb[p] + dy_b[p] * rb_b[p]
                dv_b = part(off, 1) + dy_a[p] * rwk[p] + dsa_a * ck[p] + dy_b[p] * rk_b[p]
                sa_a, sa_b = part(off, 2), part(off, 3)
                g_a = gs[p] + dy_a[p] * r_a[p]
                g_mid = g_a * w_a[p] + dsa_a * c_a[p]
                g_b = g_mid + dy_b[p] * r_b[p]
                new.append(g_b * w_b[p] + dsa_b * c_b[p])
                cols = pl.ds(p * LANES, LANES)
                for i, dy, s_post, s_pre, g, sa, dsa in ((ia, dy_a[p], snext_ref[p], sp_a[p], g_a, sa_a, dsa_a),
                                                         (ib, dy_b[p], sp_a[p], sp_b[p], g_b, sa_b, dsa_b)):
                    _store_row(dr_ref, i, cols, sum0(s_post * dy))
                    _store_row(dw_ref, i, cols, sum0(g * s_pre))
                    _store_row(db_ref, i, cols, sum0(g * sa))
                    _store_row(dk_ref, i, cols, sum0(g * vb_ref[i, p]))
                    _store_row(dc_ref, i, cols, sum0(s_pre * dsa))
                dvt_ref[p] = jnp.where(lane == ia, dv_a, jnp.where(lane == ib, dv_b, dvt_ref[p]))
                snext_ref[p] = sp_b[p]
            return tuple(new)

        gs = lax.fori_loop(0, CHUNK // 2, two_steps, tuple(g_ref[p] for p in range(N_PAIR)))
        for p in range(N_PAIR):
            g_ref[p] = gs[p]
        _cols_to_rows(dvt_ref, dv_ref)

        if n_ride:
            @pl.when(pl.program_id(0) == nch - 1)
            def _():
                _exchange_wait(x_refs, o_refs, sems, ride_gather)

    row = pl.BlockSpec((CHUNK, RW_W), lambda i: (nch - 1 - i, 0))
    big = pl.BlockSpec((CHUNK, N_PAIR, HEAD, LANES), lambda i: (nch - 1 - i, 0, 0, 0))
    any_spec = pl.BlockSpec(memory_space=pl.ANY)
    return pl.pallas_call(
        body, name=name, grid=(nch,),
        in_specs=[row] * 5 + [big, big, big, pl.BlockSpec((2 * LANES, LANES), lambda i: (0, 0))] + [any_spec] * n_ride,
        out_specs=[row] * 6 + [any_spec] * n_ride,
        out_shape=[jax.ShapeDtypeStruct((t, RW_W), F32)] * 6 + _exchange_out_shape(ride_xs, ride_gather),
        scratch_shapes=[pltpu.VMEM((N_PAIR, HEAD, LANES), F32)] * 3
        + [pltpu.VMEM((CHUNK, RW_W), F32)] * n_pre + (_exchange_sems(n_ride) if n_ride else []),
    )(r, w, k, c, b, vb, dyb, spre, bd, *ride_xs)


def _exchange(xs, *, gather, name, gather_too=()):
    n, n2 = len(xs), len(gather_too)

    def body(*refs):
        x1, x2 = refs[:n], refs[n:n + n2]
        o1, o2 = refs[n + n2:2 * n + n2], refs[2 * n + n2:2 * (n + n2)]
        sems1, sems2 = refs[2 * (n + n2):2 * (n + n2) + 3], refs[2 * (n + n2) + 3:]
        _exchange_start(x1, o1, sems1, gather)
        if n2:
            _exchange_start(x2, o2, sems2, True)
            _exchange_wait(x2, o2, sems2, True)
        _exchange_wait(x1, o1, sems1, gather)

    any_spec = pl.BlockSpec(memory_space=pl.ANY)
    return pl.pallas_call(
        body, name=name, in_specs=[any_spec] * (n + n2), out_specs=[any_spec] * (n + n2),
        out_shape=_exchange_out_shape(xs, gather) + _exchange_out_shape(list(gather_too), True),
        scratch_shapes=_exchange_sems(n) + (_exchange_sems(n2) if n2 else []),
    )(*xs, *gather_too)


def _adamw_rows(g, w, m, v):
    m = ADAM_B1 * m + (1.0 - ADAM_B1) * g
    v = ADAM_B2 * v + (1.0 - ADAM_B2) * (g * g)
    m_hat = m / (1.0 - ADAM_B1 ** ADAM_STEP)
    v_hat = v / (1.0 - ADAM_B2 ** ADAM_STEP)
    return -ADAM_LR * (m_hat / (jnp.sqrt(v_hat) + ADAM_EPS) + ADAM_WD * w), m, v


def _sum_slots(parts):
    g = parts[0].astype(F32)
    for q in range(1, N_DEV):
        g = g + parts[q].astype(F32)
    return g


def _reduce_adamw(parts, w, m, v, name):
    rows, cols = w.shape

    def fn(parts, w, m, v):
        g = _sum_slots(parts)
        return (g,) + _adamw_rows(g, w, m, v)

    return _rowwise(fn, [parts, w, m, v], [], [(cols, F32)] * 4, [], tm=_pick(rows, (256, 128, 64, 32, 16, 8)), name=name)


def _shift(x, n):
    return jnp.pad(x, ((n, 0), (0, 0)))[:-n]


def _unshift(x, n):
    return jnp.pad(x, ((0, n), (0, 0)))[n:]


def _add_n(xs, *, tm, name):
    def fn(*vals):
        s = vals[0]
        for x in vals[1:]:
            s = s + x
        return s
    return _rowwise(fn, xs, [], [(xs[0].shape[1], F32)], [], tm=tm, name=name)[0]


def _norm_fwd(h, g, *, tm, name):
    return _rowwise(lambda x, gg: _rms(x, gg), [h], [g], [(h.shape[1], BF16)], [], tm=tm, name=name)[0]


def _res_norm_fwd(h, f, g, scale, *, tm, name):
    return _rowwise(lambda hh, ff, gg: hh + scale * _rms(ff, gg), [h, f], [g], [(h.shape[1], F32)], [], tm=tm, name=name)[0]


def _norm_bwd(x, g, dy, scale, res, out_dtype, *, tm, name):
    if res is None:
        def fn(xx, dd, gg):
            dx, dg = _rms_bwd(xx, gg, dd * scale)
            return dx, dg
        rows = [x, dy]
    else:
        def fn(xx, dd, rr, gg):
            dx, dg = _rms_bwd(xx, gg, dd * scale)
            return dx + rr, dg
        rows = [x, dy, res]
    return _rowwise(fn, rows, [g], [(x.shape[1], out_dtype)], [(1, x.shape[1])], tm=tm, name=name)


def _ffn_fwd(h, g_pre, g_post, w24, wo4, tiles, tag):
    tb, ts = tiles
    a = _norm_fwd(h, g_pre, tm=ts, name=f"{tag}_norm")
    gu, s4 = _ffn_in(a, w24, tm=tb, name=f"{tag}_in")
    f = _mm(s4, wo4, trans_b=False, tm=tb, tn=D_MODEL, out_dtype=F32, name=f"{tag}_out")
    h_new = _res_norm_fwd(h, f, g_post, 0.5, tm=ts, name=f"{tag}_res")
    return h_new, (h, a, gu, s4, f)


def _ffn_bwd(dh_new, res, g_pre, g_post, w24, wo4, tiles, tag):
    tb, ts = tiles
    h, a, gu, s4, f = res
    t = h.shape[0]
    df, dg_post = _norm_bwd(f, g_post, dh_new, 0.5, None, BF16, tm=ts, name=f"{tag}_dres")
    dgu = _ffn_dswiglu(df, wo4, gu, tm=tb, name=f"{tag}_dswiglu")
    d_wo = _mm_tn(s4, df[None], tk=FFN_BLK, name=f"{tag}_dwout")
    dgu8 = dgu.reshape(2 * w24.shape[1], t, FFN_BLK)
    w8 = w24.reshape(2 * w24.shape[1], D_MODEL, FFN_BLK)
    da = _mm(dgu8, w8, trans_b=True, tm=tb, tn=D_MODEL // 2, out_dtype=F32, name=f"{tag}_da")
    d_win = _mm_tn(a[None], dgu8, tk=D_MODEL, name=f"{tag}_dwin")
    dh, dg_pre = _norm_bwd(h, g_pre, da, 1.0, dh_new, F32, tm=ts, name=f"{tag}_dnorm")
    return dh, dg_pre, dg_post, d_win, d_wo.reshape(N_DEV, -1, D_MODEL)


def _blockdiag(w4):
    n, b, _ = w4.shape
    eye = jnp.eye(n, dtype=w4.dtype)
    return (eye[:, None, :, None] * w4[:, :, None, :]).reshape(n * b, n * b)


def _blockdiag_grad(d):
    n = d.shape[0] // HEAD
    x = d.reshape(n, HEAD, n, HEAD)
    return jnp.stack([x[i, :, i, :] for i in range(n)])


def _row(v):
    return v.reshape(1, -1)


def _mixer_fwd(h, g_pre, g_post, wi, wo, P, bd, tiles, tag, ride=None):
    tb, ts = tiles
    a = _norm_fwd(h, g_pre, tm=ts, name=f"{tag}_norm")
    p = _mm(a[None], wi[None], trans_b=False, tm=tb, tn=N_IN, out_dtype=F32, name=f"{tag}_in")
    lx, lg = p[:, 0:256], p[:, 256:512]
    sb, scc, sx = p[:, 512:768], p[:, 768:1024], p[:, 1024:1280]
    z = p[:, 1280:]
    lxs = [lx, _shift(lx, 1), _shift(lx, 2), _shift(lx, 3)]
    cw = [_row(P['lru_conv_w'][kk]) for kk in range(4)]
    lru_par = cw + [_row(P['lru_conv_b']), _blockdiag(P['lru_wa']), _row(P['lru_ba']), _blockdiag(P['lru_wx']),
                    _row(P['lru_bx']), _row(P['lru_lambda'])]
    la, lb = _rowwise(_lru_pre, lxs, lru_par, [(LRU_W, F32)] * 2, [], tm=ts, name=f"{tag}_lru_pre")
    hs = _lru_scan(la, lb, name=f"{tag}_lru_scan")
    y_lru = _rowwise(lambda gg, hh, ng, b_: _lru_post(b_, gg, hh, ng), [lg, hs], [_row(P['lru_norm_g']), bd],
                     [(LRU_W, F32)], [], tm=ts, name=f"{tag}_lru_post")[0]
    sc_rows = [sb, scc, sx, _shift(scc, 1), _shift(sx, 1), _shift(scc, 2), _shift(sx, 2)]
    sc_par = [_row(P['sc_conv_w'][kk]) for kk in range(3)] + [_row(P['sc_norm_g'])]
    y_sc = _rowwise(lambda *v: _sc_fwd(v[-1], *v[:-1]), sc_rows, sc_par + [bd], [(SC_W, F32)], [], tm=ts,
                    name=f"{tag}_sc")[0]
    cuts = (0, RW_W, 2 * RW_W, 3 * RW_W, RW_IN)
    zs = [z[:, cuts[q]:cuts[q + 1]] for q in range(4)]
    z_rows = zs + [_shift(q, 1) for q in zs]
    pad = lambda m, lo: jnp.pad(m, ((lo, LANES - lo - m.shape[0]), (0, 0)))
    rw_par = [_row(P['rwkv_mu'][cuts[q]:cuts[q + 1]]) for q in range(4)]
    rw_par += [_row(P['rwkv_w0']), pad(P['rwkv_w2'], 0), _row(P['rwkv_a0']), pad(P['rwkv_a2'], 32),
               pad(P['rwkv_g2'], 64), _row(P['rwkv_k_k']), _row(P['rwkv_k_a'])]
    r, w, k2, c, b, v, g = _rowwise(lambda *vv: _rw_pre(vv[-1], *vv[:-1]), z_rows, rw_par + [bd], [(RW_W, F32)] * 7, [],
                                    tm=ts, name=f"{tag}_rw_pre")
    vb = _bcast_cols(v, _stacked_bf16(bd), name=f"{tag}_rw_vcols")
    spre, yt, *rode = _rw_scan(r, _shift(r, 1), w, k2, c, _unshift(c, 1), b, vb, _stacked_bf16(bd), name=f"{tag}_rw_scan",
                               ride=ride)
    y = _unshift(yt, 1)
    post_par =[_row(P['rwkv_lnx_w']), _row(P['rwkv_lnx_b']), _row(P['rwkv_r_k'])]
    y_rw = _rowwise(lambda *vv: _rw_post(vv[-1], *vv[:-1]), [y, r, k2, v, g], post_par + [bd], [(RW_W, F32)], [], tm=ts,
                    name=f"{tag}_rw_post")[0]
    ycat = jnp.concatenate([y_lru, y_sc, y_rw], axis=1).astype(BF16)
    m = _mm(ycat[None], wo[None], trans_b=False, tm=tb, tn=D_MODEL, out_dtype=F32, name=f"{tag}_out")
    h_new = _res_norm_fwd(h, m, g_post, 1.0, tm=ts, name=f"{tag}_res")
    res = dict(h=h, a=a, m=m, ycat=ycat, lxs=lxs, lru_par=lru_par, lg=lg, la=la, hs=hs, sc_rows=sc_rows, sc_par=sc_par,
               z_rows=z_rows, rw_par=rw_par, r=r, w=w, k2=k2, c=c, b=b, v=v, g=g, vb=vb, spre=spre, y=y, post_par=post_par)
    return h_new, res, rode


def _mixer_bwd(dh_new, R, g_pre, g_post, wi, wo, P, bd, tiles, tag, ride=None):
    tb, ts = tiles
    dm, dg_post = _norm_bwd(R['m'], g_post, dh_new, 1.0, None, BF16, tm=ts, name=f"{tag}_dres")
    dycat = _mm(dm[None], wo[None], trans_b=True, tm=tb, tn=D_MODEL, out_dtype=F32, name=f"{tag}_dycat")
    d_wo = _mm_tn(R['ycat'][None], dm[None], tk=D_MODEL // 2, name=f"{tag}_dwout")[0]
    dy_lru, dy_sc, dy_rw = dycat[:, 0:256], dycat[:, 256:512], dycat[:, 512:]
    G = {}
    d_lg, d_hs, G['lru_norm_g'] = _rowwise(
        lambda gg, hh, ct, ng, b_: _vjp_rows(_lru_post, 1, 3, 1)(b_, gg, hh, ng, ct),
        [R['lg'], R['hs'], dy_lru], [_row(P['lru_norm_g']), bd], [(LRU_W, F32)] * 2, [(1, LRU_W)], tm=ts,
        name=f"{tag}_lru_dpost")
    d_la, d_lb = _lru_scan_bwd(_unshift(R['la'], 1), _shift(R['hs'], 1), d_hs, name=f"{tag}_lru_dscan")

    def lru_pre_bwd(x0, x1, x2, x3, ca, cb_, *par):
        return _vjp_rows(_lru_pre, 0, 14, 2)(x0, x1, x2, x3, *par, ca, cb_)

    par_shapes = [tuple(q.shape) for q in R['lru_par']]
    outs = _rowwise(lru_pre_bwd, R['lxs'] + [d_la, d_lb], R['lru_par'], [(LRU_W, F32)] * 4, par_shapes, tm=ts,
                    name=f"{tag}_lru_dpre")
    dxs, dpar = outs[:4], outs[4:]
    d_lx = _add_n([dxs[0], _unshift(dxs[1], 1), _unshift(dxs[2], 2), _unshift(dxs[3], 3)], tm=ts, name=f"{tag}_lru_dx")
    G['lru_conv_w'] = jnp.concatenate(dpar[0:4], axis=0)
    G['lru_conv_b'] = dpar[4][0]
    G['lru_wa'] = _blockdiag_grad(dpar[5])
    G['lru_ba'] = dpar[6][0]
    G['lru_wx'] = _blockdiag_grad(dpar[7])
    G['lru_bx'] = dpar[8][0]
    G['lru_lambda'] = dpar[9][0]
    G['lru_norm_g'] = G['lru_norm_g'][0]

    def sc_bwd(*vv):
        rows7, ct, par4, b_ = vv[:7], vv[7], vv[8:12], vv[12]
        return _vjp_rows(_sc_fwd, 1, 11, 1)(b_, *rows7, *par4, ct)

    outs = _rowwise(sc_bwd, R['sc_rows'] + [dy_sc], R['sc_par'] + [bd], [(SC_W, F32)] * 7, [(1, SC_W)] * 4, tm=ts,
                    name=f"{tag}_sc_bwd")
    d_sb = outs[0]
    d_sc = _add_n([outs[1], _unshift(outs[3], 1), _unshift(outs[5], 2)], tm=ts, name=f"{tag}_sc_dc")
    d_sx = _add_n([outs[2], _unshift(outs[4], 1), _unshift(outs[6], 2)], tm=ts, name=f"{tag}_sc_dx")
    G['sc_conv_w'] = jnp.concatenate(outs[7:10], axis=0)
    G['sc_norm_g'] = outs[10][0]

    def rw_post_bwd(*vv):
        rows5, ct, par3, b_ = vv[:5], vv[5], vv[6:9], vv[9]
        return _vjp_rows(_rw_post, 1, 8, 1)(b_, *rows5, *par3, ct)

    outs = _rowwise(rw_post_bwd, [R['y'], R['r'], R['k2'], R['v'], R['g'], dy_rw], R['post_par'] + [bd],
                    [(RW_W, F32)] * 5, [(1, RW_W)] * 3, tm=ts, name=f"{tag}_rw_dpost")
    d_y, dr_p, dk_p, dv_p, d_g = outs[:5]
    G['rwkv_lnx_w'], G['rwkv_lnx_b'], G['rwkv_r_k'] = outs[5][0], outs[6][0], outs[7][0]
    dyb = _bcast_cols(d_y, _stacked_bf16(bd), name=f"{tag}_rw_dycols")
    dr_s, d_w, dk_s, d_c, d_b, dvt, *rode = _rw_scan_bwd(R['r'], R['w'], R['k2'], R['c'], R['b'], R['vb'], dyb, R['spre'],
                                                        _stacked_bf16(bd), name=f"{tag}_rw_dscan", ride=ride)
    dv_s = dvt

    def rw_pre_bwd(*vv):
        zrows = vv[0:8]
        dr1, dr2, dw_, dk1, dk2_, dc_, db_, dv1, dv2, dg_ = vv[8:18]
        par, b_ = vv[18:29], vv[29]
        return _vjp_rows(_rw_pre, 1, 19, 7)(b_, *zrows, *par, dr1 + dr2, dw_, dk1 + dk2_, dc_, db_, dv1 + dv2, dg_)

    par_shapes = [tuple(q.shape) for q in R['rw_par']]
    widths = [(q.shape[1], F32) for q in R['z_rows']]
    outs = _rowwise(rw_pre_bwd, R['z_rows'] + [dr_p, dr_s, d_w, dk_p, dk_s, d_c, d_b, dv_p, dv_s, d_g],
                    R['rw_par'] + [bd], widths, par_shapes, tm=ts, name=f"{tag}_rw_dpre")
    d_z = _add_n([jnp.concatenate(outs[0:4], axis=1), _unshift(jnp.concatenate(outs[4:8], axis=1), 1)], tm=ts,
                 name=f"{tag}_rw_dz")
    dpar = outs[8:]
    G['rwkv_mu'] = jnp.concatenate([q[0] for q in dpar[0:4]])
    G['rwkv_w0'], G['rwkv_a0'] = dpar[4][0], dpar[6][0]
    G['rwkv_w2'], G['rwkv_a2'], G['rwkv_g2'] = dpar[5][0:32], dpar[7][32:64], dpar[8][64:128]
    G['rwkv_k_k'], G['rwkv_k_a'] = dpar[9][0], dpar[10][0]

    dp = jnp.concatenate([d_lx, d_lg, d_sb, d_sc, d_sx, d_z], axis=1).astype(BF16)
    da = _mm(dp[None], wi[None], trans_b=True, tm=tb, tn=D_MODEL, out_dtype=F32, name=f"{tag}_da")
    d_wi = _mm_tn(R['a'][None], dp[None], tk=D_MODEL // 2, name=f"{tag}_dwin")[0]
    dh, dg_pre = _norm_bwd(R['h'], g_pre, da, 1.0, dh_new, F32, tm=ts, name=f"{tag}_dnorm")
    return dh, dg_pre, dg_post, d_wi, d_wo, G, rode


def _loss_rows(h, tgt, n_seq, *, tm, name):
    d = h.shape[1]

    def body(h_ref, t_ref, dh_ref, l_ref):
        i = pl.program_id(0)
        row = lax.broadcasted_iota(jnp.int32, (tm, 1), 0) + i * tm
        live = (row >= N_META) & (row < N_META + n_seq)
        e = jnp.where(live, h_ref[...] - t_ref[...], 0.0)
        dh_ref[...] = e * (1.0 / d)
        part = 0.5 * jnp.sum(jnp.sum(e * e, axis=1, keepdims=True) * (1.0 / d), axis=0, keepdims=True)

        @pl.when(i == 0)
        def _():
            l_ref[...] = part

        @pl.when(i > 0)
        def _():
            l_ref[...] += part

    blk = pl.BlockSpec((tm, d), lambda i: (i, 0))
    return pl.pallas_call(body, name=name, grid=(h.shape[0] // tm,), in_specs=[blk, blk],
                          out_specs=[blk, pl.BlockSpec((1, 1), lambda i: (0, 0))],
                          out_shape=[jax.ShapeDtypeStruct(h.shape, F32), jax.ShapeDtypeStruct((1, 1), F32)])(h, tgt)


def _pack(arrs, mult):
    flat = jnp.concatenate([a.reshape(-1).astype(F32) for a in arrs])
    n = flat.shape[0]
    tot = -(-n // mult) * mult
    return jnp.pad(flat, (0, tot - n)).reshape(-1, LANES)


def _unpack(buf, shapes):
    flat = buf.reshape(-1)
    out, off = [], 0
    for s in shapes:
        n = 1
        for q in s:
            n *= q
        out.append(flat[off:off + n].reshape(s))
        off += n
    return out


def _step(W, M, V, x, loss_target):
    n_seq = x.shape[1]
    t_real = N_META + n_seq
    t = (t_real // CHUNK + 1) * CHUNK
    tiles = (_pick(t, (704, 512, 256, 128, 64)), _pick(t, (352, 192, 128, 64)))
    me = 4 * lax.axis_index("x") + 2 * lax.axis_index("y") + lax.axis_index("c")
    n_layer = W['norm_g'].shape[0]

    small_sh = list(SMALL_SHARDED)
    packed = _pack([W[n] for n in small_sh], 8 * LANES)
    early = ['ffn1_w_in', 'ffn1_w_out', 'mix_w_in', 'mix_w_out']
    late = ['ffn2_w_in', 'ffn2_w_out']
    gathered = _exchange([W[n][0].astype(BF16) for n in early] + [packed], gather=True, name="gather_weights")
    big8 = [dict(zip(early, gathered[:-1]))] + [{} for _ in range(n_layer - 1)]
    pieces = [_unpack(gathered[-1][q], [W[n].shape for n in small_sh]) for q in range(N_DEV)]
    full = {n: W[n] for n in SMALL if n not in SMALL_SHARDED}
    for idx, n in enumerate(small_sh):
        full[n] = jnp.concatenate([pieces[q][idx] for q in range(N_DEV)], axis=SMALL_SHARDED[n])

    def ffn_weights(l, which):
        w24 = big8[l][f'{which}_w_in'].reshape(2, N_DEV // 2, D_MODEL, FFN_BLK)
        wo4 = big8[l][f'{which}_w_out'].reshape(N_DEV // 2, FFN_BLK, D_MODEL)
        return w24, wo4

    def mixer_weights(l):
        return big8[l]['mix_w_in'].transpose(1, 0, 2).reshape(D_MODEL, N_IN), big8[l]['mix_w_out'].reshape(D_MODEL, D_MODEL)

    bd = jnp.kron(jnp.eye(LANES // HEAD, dtype=F32), jnp.ones((HEAD, HEAD), F32))
    small_layer = [n for n in SMALL if n not in ('meta_tokens', 'norm_g')]

    h = jnp.concatenate([full['meta_tokens'], x[0], jnp.zeros((t - t_real, D_MODEL), F32)], axis=0)
    saved = []
    for l in range(n_layer):
        ng = [_row(full['norm_g'][l, q]) for q in range(6)]
        P = {n: full[n][l] for n in small_layer}
        w1 = ffn_weights(l, 'ffn1')
        h, r1 = _ffn_fwd(h, ng[0], ng[1], w1[0], w1[1], tiles, f"l{l}_ffn1")
        riders = [(l, n) for n in late] + ([(l + 1, n) for n in early] if l + 1 < n_layer else [])
        wm = mixer_weights(l)
        h, r2, rode = _mixer_fwd(h, ng[2], ng[3], wm[0], wm[1], P, bd, tiles, f"l{l}_mix",
                                 ride=([W[n][q].astype(BF16) for q, n in riders], True))
        for (q, n), arrived in zip(riders, rode):
            big8[q][n] = arrived
        w2 = ffn_weights(l, 'ffn2')
        h, r3 = _ffn_fwd(h, ng[4], ng[5], w2[0], w2[1], tiles, f"l{l}_ffn2")
        saved.append(((w1[0], w1[1], w2[0], w2[1], wm[0], wm[1]), ng, P, r1, r2, r3))

    tgt = jnp.pad(loss_target[0], ((N_META, t - t_real), (0, 0)))
    dh, loss_part = _loss_rows(h, tgt, n_seq, tm=tiles[1], name="loss")
    loss = lax.psum(loss_part[0, 0], MESH_AXES)

    small_grads = [None] * n_layer
    norm_grads = [None] * n_layer
    recv = [{} for _ in range(n_layer)]
    outgoing = []
    for l in reversed(range(n_layer)):
        lw, ng, P, r1, r2, r3 = saved[l]
        dh, g4, g5, d_win2, d_wo2 = _ffn_bwd(dh, r3, ng[4], ng[5], lw[2], lw[3], tiles, f"l{l}_ffn2")
        outgoing += [((l, 'ffn2_w_in'), d_win2), ((l, 'ffn2_w_out'), d_wo2)]
        dh, g2, g3, d_wi, d_wo, G, rode = _mixer_bwd(dh, r2, ng[2], ng[3], lw[4], lw[5], P, bd, tiles, f"l{l}_mix",
                                                     ride=([a for _, a in outgoing], False))
        for ((q, n), _), arrived in zip(outgoing, rode):
            recv[q][n] = arrived
        dh, g0, g1, d_win1, d_wo1 = _ffn_bwd(dh, r1, ng[0], ng[1], lw[0], lw[1], tiles, f"l{l}_ffn1")
        small_grads[l] = G
        norm_grads[l] = jnp.concatenate([g0, g1, g2, g3, g4, g5], axis=0)
        d_wi8 = d_wi.reshape(D_MODEL, N_DEV, N_IN // N_DEV).transpose(1, 0, 2)
        d_wo8 = d_wo.reshape(N_DEV, D_MODEL // N_DEV, D_MODEL)
        outgoing = [((l, 'ffn1_w_in'), d_win1), ((l, 'ffn1_w_out'), d_wo1), ((l, 'mix_w_in'), d_wi8), ((l, 'mix_w_out'), d_wo8)]
    gs = {n: jnp.stack([small_grads[l][n] for l in range(n_layer)]) for n in small_layer}
    gs['norm_g'] = jnp.stack(norm_grads)
    gs['meta_tokens'] = dh[:N_META]
    gpack = _pack([gs[n] for n in SMALL], 8 * LANES)
    *last, gall = _exchange([a for _, a in outgoing], gather=False, name="last_grad_exchange", gather_too=[gpack])
    for ((q, n), _), arrived in zip(outgoing, last):
        recv[q][n] = arrived
    gsum =_rowwise(lambda parts: _sum_slots(parts), [gall], [], [(LANES, F32)], [], tm=gall.shape[1],
                    name="sum_small_grads")[0]
    gfull = dict(zip(SMALL, _unpack(gsum, [gs[n].shape for n in SMALL])))

    def my_shard(n, a):
        if n not in SMALL_SHARDED:
            return a
        ax = SMALL_SHARDED[n]
        size = a.shape[ax] // N_DEV
        return lax.dynamic_slice_in_dim(a, me * size, size, axis=ax)

    g_loc = [my_shard(n, gfull[n]) for n in SMALL]
    shapes = [W[n].shape for n in SMALL]
    bufs = [_pack(g_loc, 8 * LANES)] + [_pack([D[n] for n in SMALL], 8 * LANES) for D in (W, M, V)]
    d_s, m_s, v_s = _rowwise(_adamw_rows, bufs, [], [(LANES, F32)] * 3, [], tm=bufs[0].shape[0], name="adamw_small")
    out = {'grad': dict(zip(SMALL, g_loc)), 'delta': dict(zip(SMALL, _unpack(d_s, shapes))),
           'm': dict(zip(SMALL, _unpack(m_s, shapes))), 'v': dict(zip(SMALL, _unpack(v_s, shapes)))}

    order = ['ffn1_w_in', 'ffn1_w_out', 'ffn2_w_in', 'ffn2_w_out', 'mix_w_in', 'mix_w_out']
    for idx, n in enumerate(order):
        per_layer = []
        for l in range(n_layer):
            parts = recv[l][n]
            rows, cols = W[n].shape[1], W[n].shape[2]
            per_layer.append(_reduce_adamw(parts.reshape(N_DEV, rows, cols), W[n][l], M[n][l], V[n][l],
                                           name=f"l{l}_adamw_{n}"))
        for q, key in enumerate(('grad', 'delta', 'm', 'v')):
            out[key][n] = jnp.stack([per_layer[l][q] for l in range(n_layer)])

    return (loss, dh[N_META:t_real][None],
            *[out['grad'][n] for n in WEIGHTS], *[out['delta'][n] for n in WEIGHTS],
            *[out['m'][n] for n in WEIGHTS], *[out['v'][n] for n in WEIGHTS])


def kernel(x, meta_tokens, norm_g, ffn1_w_in, ffn1_w_out, ffn2_w_in, ffn2_w_out, mix_w_in, mix_w_out, lru_conv_w, lru_conv_b, lru_wa, lru_ba, lru_wx, lru_bx, lru_lambda, lru_norm_g, sc_conv_w, sc_norm_g, rwkv_mu, rwkv_w0, rwkv_w2, rwkv_a0, rwkv_a2, rwkv_g2, rwkv_k_k, rwkv_k_a, rwkv_r_k, rwkv_lnx_w, rwkv_lnx_b, loss_target, m_meta_tokens, m_norm_g, m_ffn1_w_in, m_ffn1_w_out, m_ffn2_w_in, m_ffn2_w_out, m_mix_w_in, m_mix_w_out, m_lru_conv_w, m_lru_conv_b, m_lru_wa, m_lru_ba, m_lru_wx, m_lru_bx, m_lru_lambda, m_lru_norm_g, m_sc_conv_w, m_sc_norm_g, m_rwkv_mu, m_rwkv_w0, m_rwkv_w2, m_rwkv_a0, m_rwkv_a2, m_rwkv_g2, m_rwkv_k_k, m_rwkv_k_a, m_rwkv_r_k, m_rwkv_lnx_w, m_rwkv_lnx_b, v_meta_tokens, v_norm_g, v_ffn1_w_in, v_ffn1_w_out, v_ffn2_w_in, v_ffn2_w_out, v_mix_w_in, v_mix_w_out, v_lru_conv_w, v_lru_conv_b, v_lru_wa, v_lru_ba, v_lru_wx, v_lru_bx, v_lru_lambda, v_lru_norm_g, v_sc_conv_w, v_sc_norm_g, v_rwkv_mu, v_rwkv_w0, v_rwkv_w2, v_rwkv_a0, v_rwkv_a2, v_rwkv_g2, v_rwkv_k_k, v_rwkv_k_a, v_rwkv_r_k, v_rwkv_lnx_w, v_rwkv_lnx_b):
    w_vals = (meta_tokens, norm_g, ffn1_w_in, ffn1_w_out, ffn2_w_in, ffn2_w_out, mix_w_in, mix_w_out, lru_conv_w, lru_conv_b, lru_wa, lru_ba, lru_wx, lru_bx, lru_lambda, lru_norm_g, sc_conv_w, sc_norm_g, rwkv_mu, rwkv_w0, rwkv_w2, rwkv_a0, rwkv_a2, rwkv_g2, rwkv_k_k, rwkv_k_a, rwkv_r_k, rwkv_lnx_w, rwkv_lnx_b)
    m_vals = (m_meta_tokens, m_norm_g, m_ffn1_w_in, m_ffn1_w_out, m_ffn2_w_in, m_ffn2_w_out, m_mix_w_in, m_mix_w_out, m_lru_conv_w, m_lru_conv_b, m_lru_wa, m_lru_ba, m_lru_wx, m_lru_bx, m_lru_lambda, m_lru_norm_g, m_sc_conv_w, m_sc_norm_g, m_rwkv_mu, m_rwkv_w0, m_rwkv_w2, m_rwkv_a0, m_rwkv_a2, m_rwkv_g2, m_rwkv_k_k, m_rwkv_k_a, m_rwkv_r_k, m_rwkv_lnx_w, m_rwkv_lnx_b)
    v_vals = (v_meta_tokens, v_norm_g, v_ffn1_w_in, v_ffn1_w_out, v_ffn2_w_in, v_ffn2_w_out, v_mix_w_in, v_mix_w_out, v_lru_conv_w, v_lru_conv_b, v_lru_wa, v_lru_ba, v_lru_wx, v_lru_bx, v_lru_lambda, v_lru_norm_g, v_sc_conv_w, v_sc_norm_g, v_rwkv_mu, v_rwkv_w0, v_rwkv_w2, v_rwkv_a0, v_rwkv_a2, v_rwkv_g2, v_rwkv_k_k, v_rwkv_k_a, v_rwkv_r_k, v_rwkv_lnx_w, v_rwkv_lnx_b)
    return _step(dict(zip(WEIGHTS, w_vals)), dict(zip(WEIGHTS, m_vals)), dict(zip(WEIGHTS, v_vals)), x, loss_target)
```

```python
import functools

import jax
import jax.numpy as jnp
from jax import lax
from jax.experimental import pallas as pl
from jax.experimental.pallas import tpu as pltpu

F32 = jnp.float32
BF16 = jnp.bfloat16
PARAM_DOT = lax.Precision.DEFAULT

N_DEV = 8
MESH_AXES = ("x", "y", "c")
N_META = 16
D_MODEL = 1024
LRU_W = 256
SC_W = 256
RW_W = 512
HEAD = 64
LANES = 128
CHUNK = 64
RW_IN = 1664
N_IN = 2944
FFN_BLK = 704
RMS_EPS = 1e-6
LNX_EPS = 64e-5
LRU_C = 8.0
ADAM_LR, ADAM_B1, ADAM_B2, ADAM_EPS, ADAM_WD, ADAM_STEP = 0.001, 0.9, 0.999, 1e-08, 0.01, 10

WEIGHTS = ['meta_tokens', 'norm_g', 'ffn1_w_in', 'ffn1_w_out', 'ffn2_w_in', 'ffn2_w_out', 'mix_w_in', 'mix_w_out',
           'lru_conv_w', 'lru_conv_b', 'lru_wa', 'lru_ba', 'lru_wx', 'lru_bx', 'lru_lambda', 'lru_norm_g',
           'sc_conv_w', 'sc_norm_g', 'rwkv_mu', 'rwkv_w0', 'rwkv_w2', 'rwkv_a0', 'rwkv_a2', 'rwkv_g2', 'rwkv_k_k',
           'rwkv_k_a', 'rwkv_r_k', 'rwkv_lnx_w', 'rwkv_lnx_b']
BIG = ['ffn1_w_in', 'ffn1_w_out', 'ffn2_w_in', 'ffn2_w_out', 'mix_w_in', 'mix_w_out']
SMALL_SHARDED = {'meta_tokens': 1, 'norm_g': 2, 'lru_conv_w': 2, 'sc_conv_w': 2, 'rwkv_w2': 2, 'rwkv_a2': 2, 'rwkv_g2': 2}
SMALL = [n for n in WEIGHTS if n not in BIG]


def _pick(n, cands):
    for c in cands:
        if n % c == 0:
            return c
    raise ValueError(f"no tile for {n}")


def _rowwise(fn, rows, params, row_outs, acc_outs, *, tm, name):
    nr, npar, nro, nao = len(rows), len(params), len(row_outs), len(acc_outs)
    n_rows = rows[0].shape[-2]
    assert n_rows % tm == 0, (name, n_rows, tm)

    def body(*refs):
        vals = [r[...] for r in refs[:nr + npar]]
        outs = fn(*vals)
        if not isinstance(outs, (tuple, list)):
            outs = (outs,)
        assert len(outs) == nro + nao, (name, len(outs))
        for o_ref, o in zip(refs[nr + npar:nr + npar + nro], outs[:nro]):
            o_ref[...] = o.astype(o_ref.dtype)
        step = pl.program_id(0)
        for a_ref, a in zip(refs[nr + npar + nro:], outs[nro:]):
            @pl.when(step == 0)
            def _(a_ref=a_ref, a=a):
                a_ref[...] = a.astype(F32)

            @pl.when(step > 0)
            def _(a_ref=a_ref, a=a):
                a_ref[...] += a.astype(F32)

    def row_spec(shape):
        if len(shape) == 2:
            return pl.BlockSpec((tm, shape[1]), lambda i: (i, 0))
        return pl.BlockSpec((shape[0], tm, shape[2]), lambda i: (0, i, 0))

    def full_spec(shape):
        nd = len(shape)
        return pl.BlockSpec(tuple(shape), lambda i, nd=nd: (0,) * nd)

    in_specs = [row_spec(r.shape) for r in rows] + [full_spec(p.shape) for p in params]
    out_shape = [jax.ShapeDtypeStruct((n_rows, w), dt) for (w, dt) in row_outs]
    out_shape += [jax.ShapeDtypeStruct(tuple(s), F32) for s in acc_outs]
    out_specs = [row_spec((n_rows, w)) for (w, _) in row_outs] + [full_spec(s) for s in acc_outs]
    res = pl.pallas_call(body, name=name, grid=(n_rows // tm,), in_specs=in_specs, out_specs=out_specs,
                         out_shape=out_shape)(*rows, *params)
    return tuple(res)


def _mm(a3, b3, *, trans_b, tm, tn, out_dtype, name):
    nj, m, kb = a3.shape
    n = b3.shape[1] if trans_b else b3.shape[2]
    dims = (((1,), (1,)), ((), ())) if trans_b else (((1,), (0,)), ((), ()))

    def body(a_ref, b_ref, o_ref):
        acc = lax.dot_general(a_ref[0], b_ref[0], dims, preferred_element_type=F32)
        for j in range(1, nj):
            acc = acc + lax.dot_general(a_ref[j], b_ref[j], dims, preferred_element_type=F32)
        o_ref[...] = acc.astype(o_ref.dtype)

    if trans_b:
        b_spec = pl.BlockSpec((nj, tn, kb), lambda i, c: (0, c, 0))
    else:
        b_spec = pl.BlockSpec((nj, kb, tn), lambda i, c: (0, 0, c))
    return pl.pallas_call(
        body, name=name, grid=(m // tm, n // tn),
        in_specs=[pl.BlockSpec((nj, tm, kb), lambda i, c: (0, i, 0)), b_spec],
        out_specs=pl.BlockSpec((tm, tn), lambda i, c: (i, c)),
        out_shape=jax.ShapeDtypeStruct((m, n), out_dtype),
    )(a3, b3)


def _mm_tn(a3, b3, *, tk, name):
    ja, t, ka = a3.shape
    jb, _, n = b3.shape
    nj = max(ja, jb)

    def body(a_ref, b_ref, o_ref):
        o_ref[0] = lax.dot_general(a_ref[0], b_ref[0], (((0,), (0,)), ((), ())),
                                   preferred_element_type=F32).astype(o_ref.dtype)

    return pl.pallas_call(
        body, name=name, grid=(nj, ka // tk),
        in_specs=[pl.BlockSpec((1, t, tk), (lambda j, c: (j, 0, c)) if ja > 1 else (lambda j, c: (0, 0, c))),
                  pl.BlockSpec((1, t, n), (lambda j, c: (j, 0, 0)) if jb > 1 else (lambda j, c: (0, 0, 0)))],
        out_specs=pl.BlockSpec((1, tk, n), lambda j, c: (j, c, 0)),
        out_shape=jax.ShapeDtypeStruct((nj, ka, n), BF16),
    )(a3, b3)


def _ffn_in(a, w24, *, tm, name):
    t, d = a.shape
    nb, fb = w24.shape[1], w24.shape[3]

    def body(a_ref, w_ref, gu_ref, s_ref):
        x = a_ref[...]
        g = jnp.dot(x, w_ref[0, 0], preferred_element_type=F32)
        u = jnp.dot(x, w_ref[1, 0], preferred_element_type=F32)
        gu_ref[0, 0] = g.astype(BF16)
        gu_ref[1, 0] = u.astype(BF16)
        s_ref[0] = (g * jax.nn.sigmoid(g) * u).astype(BF16)

    return pl.pallas_call(
        body, name=name, grid=(nb, t // tm),
        in_specs=[pl.BlockSpec((tm, d), lambda j, i: (i, 0)), pl.BlockSpec((2, 1, d, fb), lambda j, i: (0, j, 0, 0))],
        out_specs=[pl.BlockSpec((2, 1, tm, fb), lambda j, i: (0, j, i, 0)), pl.BlockSpec((1, tm, fb), lambda j, i: (j, i, 0))],
        out_shape=[jax.ShapeDtypeStruct((2, nb, t, fb), BF16), jax.ShapeDtypeStruct((nb, t, fb), BF16)],
    )(a, w24)


def _ffn_dswiglu(df, wo4, gu, *, tm, name):
    t, d = df.shape
    nb, fb = wo4.shape[0], wo4.shape[1]

    def body(df_ref, wo_ref, gu_ref, dg_ref):
        ds = lax.dot_general(df_ref[...], wo_ref[0], (((1,), (1,)), ((), ())), preferred_element_type=F32)
        g = gu_ref[0, 0].astype(F32)
        u = gu_ref[1, 0].astype(F32)
        sig = jax.nn.sigmoid(g)
        dg_ref[0, 0] = (ds * u * sig * (1.0 + g * (1.0 - sig))).astype(BF16)
        dg_ref[1, 0] = (ds * g * sig).astype(BF16)

    return pl.pallas_call(
        body, name=name, grid=(nb, t // tm),
        in_specs=[pl.BlockSpec((tm, d), lambda j, i: (i, 0)), pl.BlockSpec((1, fb, d), lambda j, i: (j, 0, 0)),
                  pl.BlockSpec((2, 1, tm, fb), lambda j, i: (0, j, i, 0))],
        out_specs=pl.BlockSpec((2, 1, tm, fb), lambda j, i: (0, j, i, 0)),
        out_shape=jax.ShapeDtypeStruct((2, nb, t, fb), BF16),
    )(df, wo4, gu)


def _rms(x, g):
    return x * lax.rsqrt(jnp.mean(x * x, axis=-1, keepdims=True) + RMS_EPS) * g


def _rms_bwd(x, g, dy):
    rstd = lax.rsqrt(jnp.mean(x * x, axis=-1, keepdims=True) + RMS_EPS)
    xh = x * rstd
    dxh = dy * g
    dx = rstd * (dxh - xh * jnp.mean(dxh * xh, axis=-1, keepdims=True))
    return dx, jnp.sum(dy * xh, axis=0, keepdims=True)


def _seg_sum_impl(x, bd):
    bd2 = jnp.concatenate([bd, bd], axis=0).astype(BF16)
    hi = x.astype(BF16)
    lo = (x - hi.astype(F32)).astype(BF16)
    parts = [jnp.dot(jnp.concatenate([hi[:, q * LANES:(q + 1) * LANES], lo[:, q * LANES:(q + 1) * LANES]], axis=1), bd2,
                     preferred_element_type=F32) for q in range(x.shape[1] // LANES)]
    return parts[0] if len(parts) == 1 else jnp.concatenate(parts, axis=1)


@jax.custom_vjp
def _seg_sum(x, bd):
    return _seg_sum_impl(x, bd)


def _seg_sum_fwd(x, bd):
    return _seg_sum_impl(x, bd), bd


def _seg_sum_bwd(bd, ct):
    return _seg_sum_impl(ct, bd), jnp.zeros_like(bd)


_seg_sum.defvjp(_seg_sum_fwd, _seg_sum_bwd)


def _group_rms(y, g, bd):
    return y * lax.rsqrt(_seg_sum(y * y, bd) * (1.0 / HEAD) + RMS_EPS) * g


def _expm1(x):
    return jnp.where(jnp.abs(x) < 1e-2, x * (1.0 + x * (0.5 + x * (1.0 / 6.0))), jnp.exp(x) - 1.0)


def _lru_pre(x0, x1, x2, x3, cw0, cw1, cw2, cw3, cb, wa, ba, wx, bx, lam):
    u = x3 * cw0 + x2 * cw1 + x1 * cw2 + x0 * cw3 + cb
    r = jax.nn.sigmoid(jnp.dot(u, wa, preferred_element_type=F32, precision=PARAM_DOT) + ba)
    i = jax.nn.sigmoid(jnp.dot(u, wx, preferred_element_type=F32, precision=PARAM_DOT) + bx)
    log_a = -LRU_C * r * jax.nn.softplus(-lam)
    return jnp.exp(log_a), jnp.sqrt(-_expm1(2.0 * log_a)) * (i * u)


def _lru_post(bd, gate, hs, ng):
    return _group_rms(jax.nn.gelu(gate) * hs, ng, bd)


def _sc_fwd(bd, b, c0, x0, c1, x1, c2, x2, w0, w1, w2, ng):
    return _group_rms(b * (w0 * (c2 * x2) + w1 * (c1 * x1) + w2 * (c0 * x0)), ng, bd)


def _rw_pre(bd, zr, zk, zv, zt, sr, sk, sv, st, mur, muk, muv, mut, w0, w2p, a0, a2p, g2p, k_k, k_a):
    r, k, v, tail = zr + (sr - zr) * mur, zk + (sk - zk) * muk, zv + (sv - zv) * muv, zt + (st - zt) * mut
    lane = lax.broadcasted_iota(jnp.int32, tail.shape, 1)
    act = jnp.where(lane < 32, jnp.tanh(tail), jnp.where(lane < 64, tail, jax.nn.sigmoid(tail)))
    dot = functools.partial(jnp.dot, preferred_element_type=F32, precision=PARAM_DOT)
    w_log = -jax.nn.softplus(-(w0 + dot(act, w2p))) - 0.5
    w = jnp.exp(-jnp.exp(w_log))
    a = jax.nn.sigmoid(a0 + dot(act, a2p))
    g = dot(act, g2p)
    kk = k * k_k
    k2 = k * (1.0 + (a - 1.0) * k_a)
    kkn = kk * lax.rsqrt(jnp.maximum(_seg_sum(kk * kk, bd), 1e-24))
    return r, w, k2, -kkn, kkn * a, v, g


def _rw_post(bd, y, r, k2, v, g, lnw, lnb, r_k):
    mean = _seg_sum(y, bd) * (1.0 / HEAD)
    yc = y - mean
    var = _seg_sum(yc * yc, bd) * (1.0 / HEAD)
    yn = yc * lax.rsqrt(var + LNX_EPS) * lnw + lnb
    return (yn + _seg_sum(r * k2 * r_k, bd) * v) * g


def _vjp_rows(fwd, n_static, n_in, n_ct):
    def fn(*args):
        static, prim, cts = args[:n_static], args[n_static:n_static + n_in], args[n_static + n_in:]
        assert len(cts) == n_ct
        _, vjp = jax.vjp(functools.partial(fwd, *static), *prim)
        return vjp(cts[0] if n_ct == 1 else tuple(cts))
    return fn


def _all_to_all_copies(x_refs, o_refs, sems):
    send_sems, recv_sems, local_sems = sems
    mx, my, mc = lax.axis_index("x"), lax.axis_index("y"), lax.axis_index("c")
    me = 4 * mx + 2 * my + mc
    local, sends, recvs = [], [], []
    for k in range(len(x_refs)):
        local.append(pltpu.make_async_copy(x_refs[k].at[me], o_refs[k].at[me], local_sems.at[k]))
    for d in range(1, N_DEV):
        px, py, pc = mx ^ ((d >> 2) & 1), my ^ ((d >> 1) & 1), mc ^ (d & 1)
        peer = 4 * px + 2 * py + pc
        for k in range(len(x_refs)):
            common = dict(src_ref=x_refs[k].at[peer], send_sem=send_sems.at[k, d - 1], recv_sem=recv_sems.at[k, d - 1],
                          device_id=(px, py, pc), device_id_type=pl.DeviceIdType.MESH)
            sends.append(pltpu.make_async_remote_copy(dst_ref=o_refs[k].at[me], **common))
            recvs.append(pltpu.make_async_remote_copy(dst_ref=o_refs[k].at[peer], **common))
    return local, sends, recvs


def _gather2_copies(x_refs, o_refs, sems):
    send_sems, recv_sems, local_sems = sems
    mx, my, mc = lax.axis_index("x"), lax.axis_index("y"), lax.axis_index("c")
    sibling = (mx, my, 1 - mc)
    chips = [(1 - mx, my), (mx, 1 - my), (1 - mx, 1 - my)]

    def slot(px, py, pc):
        return 4 * px + 2 * py + pc

    out = dict(local=[], first=[], first_recv=[], ici_recv=[], passed=[], passed_recv=[])
    for k in range(len(x_refs)):
        def copy(sem, src, dst_slot, to, k=k):
            return pltpu.make_async_remote_copy(src_ref=src, dst_ref=o_refs[k].at[dst_slot], send_sem=send_sems.at[k, sem],
                                                recv_sem=recv_sems.at[k, sem], device_id=to, device_id_type=pl.DeviceIdType.MESH)
        me = slot(mx, my, mc)
        out['local'].append(pltpu.make_async_copy(x_refs[k], o_refs[k].at[me], local_sems.at[k]))
        out['first'].append(copy(0, x_refs[k], me, sibling))
        out['first_recv'].append(copy(0, x_refs[k], slot(mx, my, 1 - mc), sibling))
        for j, (px, py) in enumerate(chips):
            out['first'].append(copy(1 + j, x_refs[k], me, (px, py, mc)))
            out['ici_recv'].append(copy(1 + j, x_refs[k], slot(px, py, mc), (px, py, mc)))
            out['passed'].append(copy(4 + j, o_refs[k].at[slot(px, py, mc)], slot(px, py, mc), sibling))
            out['passed_recv'].append(copy(4 + j, x_refs[k], slot(px, py, 1 - mc), sibling))
    return out


def _exchange_start(x_refs, o_refs, sems, gather):
    if gather:
        cps = _gather2_copies(x_refs, o_refs, sems)
        for cp in cps['local'] + cps['first']:
            cp.start()
        return
    local, sends, _ = _all_to_all_copies(x_refs, o_refs, sems)
    for cp in local + sends:
        cp.start()


def _exchange_wait(x_refs, o_refs, sems, gather):
    if gather:
        cps = _gather2_copies(x_refs, o_refs, sems)
        for arrived, onward in zip(cps['ici_recv'], cps['passed']):
            arrived.wait_recv()
            onward.start()
        for cp in cps['first'] + cps['passed']:
            cp.wait_send()
        for cp in cps['first_recv'] + cps['passed_recv']:
            cp.wait_recv()
        for cp in cps['local']:
            cp.wait()
        return
    local, sends, recvs = _all_to_all_copies(x_refs, o_refs, sems)
    for cp in sends:
        cp.wait_send()
    for cp in recvs:
        cp.wait_recv()
    for cp in local:
        cp.wait()


def _exchange_out_shape(xs, gather):
    return [jax.ShapeDtypeStruct(((N_DEV,) + x.shape) if gather else x.shape, x.dtype) for x in xs]


def _exchange_sems(n):
    return [pltpu.SemaphoreType.DMA((n, N_DEV - 1)), pltpu.SemaphoreType.DMA((n, N_DEV - 1)), pltpu.SemaphoreType.DMA((n,))]


SUBLANES = 8


def _store_row(ref, i, cols, row):
    base = pl.multiple_of((i // SUBLANES) * SUBLANES, SUBLANES)
    sub = lax.broadcasted_iota(jnp.int32, (SUBLANES, row.shape[1]), 0)
    ref[pl.ds(base, SUBLANES), cols] = jnp.where(sub == i % SUBLANES, row, ref[pl.ds(base, SUBLANES), cols])


def _tile_scan(a, b, reverse):
    sub = lax.broadcasted_iota(jnp.int32, a.shape, 0)
    for sh in (1, 2, 4):
        if reverse:
            live = sub < SUBLANES - sh
            a_s, b_s = pltpu.roll(a, SUBLANES - sh, 0), pltpu.roll(b, SUBLANES - sh, 0)
        else:
            live = sub >= sh
            a_s, b_s = pltpu.roll(a, sh, 0), pltpu.roll(b, sh, 0)
        b = jnp.where(live, a * b_s, 0.0) + b
        a = jnp.where(live, a * a_s, a)
    return a, b


def _lru_scan(a, b, name):
    t, w = a.shape

    def body(a_ref, b_ref, h_ref):
        def tile(j, h):
            rows = pl.ds(pl.multiple_of(j * SUBLANES, SUBLANES), SUBLANES)
            ca, cb = _tile_scan(a_ref[rows, :], b_ref[rows, :], False)
            out = ca * h + cb
            h_ref[rows, :] = out
            return out[SUBLANES - 1:SUBLANES]
        lax.fori_loop(0, t // SUBLANES, tile, jnp.zeros((1, w), F32))

    return pl.pallas_call(body, name=name, out_shape=jax.ShapeDtypeStruct((t, w), F32))(a, b)


def _lru_scan_bwd(a_next, h_prev, dhs, name):
    t, w = dhs.shape

    def body(a_ref, h_ref, dh_ref, da_ref, db_ref):
        def tile(n, lam):
            rows = pl.ds(pl.multiple_of((t // SUBLANES - 1 - n) * SUBLANES, SUBLANES), SUBLANES)
            ca, cb = _tile_scan(a_ref[rows, :], dh_ref[rows, :], True)
            out = ca * lam + cb
            db_ref[rows, :] = out
            da_ref[rows, :] = out * h_ref[rows, :]
            return out[0:1]
        lax.fori_loop(0, t // SUBLANES, tile, jnp.zeros((1, w), F32))

    return pl.pallas_call(body, name=name, out_shape=[jax.ShapeDtypeStruct((t, w), F32)] * 2)(a_next, h_prev, dhs)


N_PAIR = RW_W // LANES


def _bcast_cols(v, bd2, name, single_pass=False):
    t = v.shape[0]

    def body(v_ref, bd_ref, o_ref):
        bdv = bd_ref[...]
        sub = lax.broadcasted_iota(jnp.int32, (HEAD, LANES), 0)
        own = lax.broadcasted_iota(jnp.int32, (HEAD, LANES), 1) % HEAD == sub
        for i in range(CHUNK):
            row = v_ref[i:i + 1, :]
            prods = [jnp.where(own, row[:, p * LANES:(p + 1) * LANES], 0.0) for p in range(N_PAIR)]
            if single_pass:
                sums = jnp.dot(jnp.concatenate(prods, axis=0).astype(BF16), bdv[0:LANES], preferred_element_type=F32)
            else:
                sums = _group_sums(prods, bdv)
            for p in range(N_PAIR):
                o_ref[i, p] = sums[p * HEAD:(p + 1) * HEAD]

    return pl.pallas_call(
        body, name=name, grid=(t // CHUNK,),
        in_specs=[pl.BlockSpec((CHUNK, RW_W), lambda i: (i, 0)), pl.BlockSpec((2 * LANES, LANES), lambda i: (0, 0))],
        out_specs=pl.BlockSpec((CHUNK, N_PAIR, HEAD, LANES), lambda i: (i, 0, 0, 0)),
        out_shape=jax.ShapeDtypeStruct((t, N_PAIR, HEAD, LANES), F32),
    )(v, bd2)


def _cols_to_rows(cols_ref, rows_ref):
    lane = lax.broadcasted_iota(jnp.int32, (CHUNK, LANES), 1)
    for p in range(N_PAIR):
        tile = cols_ref[p]
        sq = jnp.concatenate([tile, jnp.zeros_like(tile)], axis=0).T
        rows_ref[:, p * LANES:(p + 1) * LANES] = jnp.where(lane < HEAD, sq[0:CHUNK], pltpu.roll(sq[CHUNK:2 * CHUNK], HEAD, 1))


def _stacked_bf16(bd):
    return jnp.concatenate([bd, bd], axis=0).astype(BF16)


def _group_sums(prods, bd2):
    x = jnp.concatenate(prods, axis=0)
    hi = x.astype(BF16)
    lo = (x - hi.astype(F32)).astype(BF16)
    return jnp.dot(jnp.concatenate([hi, lo], axis=1), bd2, preferred_element_type=F32)


def _rw_scan(r, r_prev, w, k, c, c_next, b, vb, bd2, name, ride=None):
    t = w.shape[0]
    nch = t // CHUNK
    ride_xs, ride_gather = ride if ride is not None else ([], False)
    n_ride = len(ride_xs)

    def body(*refs):
        r_ref, rp_ref, w_ref, k_ref, c_ref, cn_ref, b_ref, vb_ref, bd_ref = refs[:9]
        x_refs = refs[9:9 + n_ride]
        spre_ref, y_ref = refs[9 + n_ride:11 + n_ride]
        o_refs = refs[11 + n_ride:11 + 2 * n_ride]
        s_ref, wc_ref, wr_ref, bc_ref, kc_ref, br_ref, kr_ref, yt_ref = refs[11 + 2 * n_ride:19 + 2 * n_ride]
        sems = refs[19 + 2 * n_ride:]

        @pl.when(pl.program_id(0) == 0)
        def _():
            s_ref[...] = jnp.zeros_like(s_ref)
            if n_ride:
                _exchange_start(x_refs, o_refs, sems, ride_gather)

        yt_ref[...] = jnp.zeros_like(yt_ref)
        bdv = bd_ref[...]
        lane = lax.broadcasted_iota(jnp.int32, (HEAD, LANES), 1) % CHUNK

        wv, cn, rv, bv, kv = w_ref[...], cn_ref[...], r_ref[...], b_ref[...], k_ref[...]
        wc_ref[...] = wv * cn
        wr_ref[...] = wv * rv
        for ref, x in ((bc_ref, bv * cn), (kc_ref, kv * cn), (br_ref, bv * rv), (kr_ref, kv * rv)):
            sums = _group_sums([x[:, q * LANES:(q + 1) * LANES] for q in range(N_PAIR)], bdv)
            for q in range(N_PAIR):
                ref[:, q * LANES:(q + 1) * LANES] = sums[q * CHUNK:(q + 1) * CHUNK]

        def cut(ref, i):
            x = ref[pl.ds(i, 1), :]
            return [x[:, p * LANES:(p + 1) * LANES] for p in range(N_PAIR)]

        def two_steps(j, st):
            i0 = 2 * j
            i1 = i0 + 1
            c0, wc0, rp0, wr0 = cut(c_ref, i0), cut(wc_ref, i0), cut(rp_ref, i0), cut(wr_ref, i0)
            w0, b0, k0, w1, b1, k1 = cut(w_ref, i0), cut(b_ref, i0), cut(k_ref, i0), cut(w_ref, i1), cut(b_ref, i1), cut(k_ref, i1)
            bc0, kc0, br0, kr0 = cut(bc_ref, i0), cut(kc_ref, i0), cut(br_ref, i0), cut(kr_ref, i0)
            pairs = range(N_PAIR)
            red = _group_sums([st[p] * c0[p] for p in pairs] + [st[p] * wc0[p] for p in pairs], bdv)
            out = _group_sums([st[p] * rp0[p] for p in pairs] + [st[p] * wr0[p] for p in pairs], bdv)
            new = []
            for p in pairs:
                v0, v1 = vb_ref[i0, p], vb_ref[i1, p]
                sa0 = red[p * HEAD:(p + 1) * HEAD]
                sa1 = red[(N_PAIR + p) * HEAD:(N_PAIR + p + 1) * HEAD] + sa0 * bc0[p] + v0 * kc0[p]
                y_before = out[p * HEAD:(p + 1) * HEAD]
                y0 = out[(N_PAIR + p) * HEAD:(N_PAIR + p + 1) * HEAD] + sa0 * br0[p] + v0 * kr0[p]
                spre_ref[i0, p] = st[p]
                s1 = st[p] * w0[p] + sa0 * b0[p] + v0 * k0[p]
                spre_ref[i1, p] = s1
                new.append(s1 * w1[p] + sa1 * b1[p] + v1 * k1[p])
                yt_ref[p] = jnp.where(lane == i0, y_before, jnp.where(lane == i1, y0, yt_ref[p]))
            return tuple(new)

        st = lax.fori_loop(0, CHUNK // 2, two_steps, tuple(s_ref[p] for p in range(N_PAIR)))
        for p in range(N_PAIR):
            s_ref[p] = st[p]
        _cols_to_rows(yt_ref, y_ref)

        if n_ride:
            @pl.when(pl.program_id(0) == nch - 1)
            def _():
                _exchange_wait(x_refs, o_refs, sems, ride_gather)

    row = pl.BlockSpec((CHUNK, RW_W), lambda i: (i, 0))
    big = pl.BlockSpec((CHUNK, N_PAIR, HEAD, LANES), lambda i: (i, 0, 0, 0))
    any_spec = pl.BlockSpec(memory_space=pl.ANY)
    return pl.pallas_call(
        body, name=name, grid=(nch,),
        in_specs=[row] * 7 + [big, pl.BlockSpec((2 * LANES, LANES), lambda i: (0, 0))] + [any_spec] * n_ride,
        out_specs=[big, row] + [any_spec] * n_ride,
        out_shape=[jax.ShapeDtypeStruct((t, N_PAIR, HEAD, LANES), F32), jax.ShapeDtypeStruct((t, RW_W), F32)]
        + _exchange_out_shape(ride_xs, ride_gather),
        scratch_shapes=[pltpu.VMEM((N_PAIR, HEAD, LANES), F32)] + [pltpu.VMEM((CHUNK, RW_W), F32)] * 6
        + [pltpu.VMEM((N_PAIR, HEAD, LANES), F32)] + (_exchange_sems(n_ride) if n_ride else []),
    )(r, r_prev, w, k, c, c_next, b, vb, bd2, *ride_xs)


def _rw_scan_bwd(r, w, k, c, b, vb, dyb, spre, bd, name, ride=None):
    t = r.shape[0]
    nch = t // CHUNK
    ride_xs, ride_gather = ride if ride is not None else ([], False)
    n_ride = len(ride_xs)
    n_pre = 8

    def body(*refs):
        r_ref, w_ref, k_ref, c_ref, b_ref, vb_ref, dyb_ref, spre_ref, bd_ref = refs[:9]
        x_refs = refs[9:9 + n_ride]
        dr_ref, dw_ref, dk_ref, dc_ref, db_ref, dv_ref = refs[9 + n_ride:15 + n_ride]
        o_refs = refs[15 + n_ride:15 + 2 * n_ride]
        g_ref, snext_ref, dvt_ref = refs[15 + 2 * n_ride:18 + 2 * n_ride]
        pre = refs[18 + 2 * n_ride:18 + 2 * n_ride + n_pre]
        sems = refs[18 + 2 * n_ride + n_pre:]
        wb_ref, wk_ref, rb_ref, rk_ref, rwb_ref, cb_ref, rwk_ref, ck_ref = pre[:8]

        @pl.when(pl.program_id(0) == 0)
        def _():
            g_ref[...] = jnp.zeros_like(g_ref)
            snext_ref[...] = jnp.zeros_like(snext_ref)
            if n_ride:
                _exchange_start(x_refs, o_refs, sems, ride_gather)

        for ref in (dr_ref, dw_ref, dk_ref, dc_ref, db_ref, dvt_ref):
            ref[...] = jnp.zeros_like(ref)
        bdv = bd_ref[...]
        bd1 = bdv[0:LANES]
        lane = lax.broadcasted_iota(jnp.int32, (HEAD, LANES), 1) % CHUNK

        rv, wv, kv, cv, bv = r_ref[...], w_ref[...], k_ref[...], c_ref[...], b_ref[...]
        b_b, k_b = pltpu.roll(bv, 1, 0), pltpu.roll(kv, 1, 0)
        wb_ref[...] = wv * b_b
        wk_ref[...] = wv * k_b
        rw = rv * wv
        for ref, x in ((rb_ref, rv * bv), (rk_ref, rv * kv), (rwb_ref, rw * b_b), (cb_ref, cv * b_b), (rwk_ref, rw * k_b),
                       (ck_ref, cv * k_b)):
            sums = _group_sums([x[:, q * LANES:(q + 1) * LANES] for q in range(N_PAIR)], bdv)
            for q in range(N_PAIR):
                ref[:, q * LANES:(q + 1) * LANES] = sums[q * CHUNK:(q + 1) * CHUNK]

        def sum0(x):
            return jnp.sum(x, axis=0, keepdims=True)

        def cut(ref, i):
            x = ref[pl.ds(i, 1), :]
            return [x[:, p * LANES:(p + 1) * LANES] for p in range(N_PAIR)]

        def sums_bf16(prods):
            return jnp.dot(jnp.concatenate(prods, axis=0).astype(BF16), bd1, preferred_element_type=F32)

        def two_steps(n, gs):
            ia = CHUNK - 1 - 2 * n
            ib = ia - 1
            r_a, w_a, k_a, c_a, b_a = [cut(ref, ia) for ref in (r_ref, w_ref, k_ref, c_ref, b_ref)]
            r_b, w_b, c_b = [cut(ref, ib) for ref in (r_ref, w_ref, c_ref)]
            wb, wk, rb_a, rk_a, rwb, cb, rwk, ck = [cut(ref, ia) for ref in pre[:8]]
            rb_b, rk_b = cut(rb_ref, ib), cut(rk_ref, ib)
            pairs = range(N_PAIR)
            sp_a = [spre_ref[ia, p] for p in pairs]
            sp_b = [spre_ref[ib, p] for p in pairs]
            dy_a = [dyb_ref[ia, p] for p in pairs]
            dy_b = [dyb_ref[ib, p] for p in pairs]
            chain = _group_sums([gs[p] * b_a[p] for p in pairs] + [gs[p] * wb[p] for p in pairs], bdv)
            off = sums_bf16([gs[p] * k_a[p] for p in pairs] + [gs[p] * wk[p] for p in pairs]
                            + [sp_a[p] * c_a[p] for p in pairs] + [sp_b[p] * c_b[p] for p in pairs])
            new = []
            for p in pairs:
                def part(x, q, p=p):
                    return x[(q * N_PAIR + p) * HEAD:(q * N_PAIR + p + 1) * HEAD]
                dsa_a = part(chain, 0) + dy_a[p] * rb_a[p]
                dv_a = part(off, 0) + dy_a[p] * rk_a[p]
                dsa_b = part(chain, 1) + dy_a[p] * rwb[p] + dsa_a * cb[p] + dy_b[p] * rb_b[p]
                dv_b = part(off, 1) + dy_a[p] * rwk[p] + dsa_a * ck[p] + dy_b[p] * rk_b[p]
                sa_a, sa_b = part(off, 2), part(off, 3)
                g_a = gs[p] + dy_a[p] * r_a[p]
                g_mid = g_a * w_a[p] + dsa_a * c_a[p]
                g_b = g_mid + dy_b[p] * r_b[p]
                new.append(g_b * w_b[p] + dsa_b * c_b[p])
                cols = pl.ds(p * LANES, LANES)
                for i, dy, s_post, s_pre, g, sa, dsa in ((ia, dy_a[p], snext_ref[p], sp_a[p], g_a, sa_a, dsa_a),
                                                         (ib, dy_b[p], sp_a[p], sp_b[p], g_b, sa_b, dsa_b)):
                    _store_row(dr_ref, i, cols, sum0(s_post * dy))
                    _store_row(dw_ref, i, cols, sum0(g * s_pre))
                    _store_row(db_ref, i, cols, sum0(g * sa))
                    _store_row(dk_ref, i, cols, sum0(g * vb_ref[i, p]))
                    _store_row(dc_ref, i, cols, sum0(s_pre * dsa))
                dvt_ref[p] = jnp.where(lane == ia, dv_a, jnp.where(lane == ib, dv_b, dvt_ref[p]))
                snext_ref[p] = sp_b[p]
            return tuple(new)

        gs = lax.fori_loop(0, CHUNK // 2, two_steps, tuple(g_ref[p] for p in range(N_PAIR)))
        for p in range(N_PAIR):
            g_ref[p] = gs[p]
        _cols_to_rows(dvt_ref, dv_ref)

        if n_ride:
            @pl.when(pl.program_id(0) == nch - 1)
            def _():
                _exchange_wait(x_refs, o_refs, sems, ride_gather)

    row = pl.BlockSpec((CHUNK, RW_W), lambda i: (nch - 1 - i, 0))
    big = pl.BlockSpec((CHUNK, N_PAIR, HEAD, LANES), lambda i: (nch - 1 - i, 0, 0, 0))
    any_spec = pl.BlockSpec(memory_space=pl.ANY)
    return pl.pallas_call(
        body, name=name, grid=(nch,),
        in_specs=[row] * 5 + [big, big, big, pl.BlockSpec((2 * LANES, LANES), lambda i: (0, 0))] + [any_spec] * n_ride,
        out_specs=[row] * 6 + [any_spec] * n_ride,
        out_shape=[jax.ShapeDtypeStruct((t, RW_W), F32)] * 6 + _exchange_out_shape(ride_xs, ride_gather),
        scratch_shapes=[pltpu.VMEM((N_PAIR, HEAD, LANES), F32)] * 3
        + [pltpu.VMEM((CHUNK, RW_W), F32)] * n_pre + (_exchange_sems(n_ride) if n_ride else []),
    )(r, w, k, c, b, vb, dyb, spre, bd, *ride_xs)


def _exchange(xs, *, gather, name, gather_too=()):
    n, n2 = len(xs), len(gather_too)

    def body(*refs):
        x1, x2 = refs[:n], refs[n:n + n2]
        o1, o2 = refs[n + n2:2 * n + n2], refs[2 * n + n2:2 * (n + n2)]
        sems1, sems2 = refs[2 * (n + n2):2 * (n + n2) + 3], refs[2 * (n + n2) + 3:]
        _exchange_start(x1, o1, sems1, gather)
        if n2:
            _exchange_start(x2, o2, sems2, True)
            _exchange_wait(x2, o2, sems2, True)
        _exchange_wait(x1, o1, sems1, gather)

    any_spec = pl.BlockSpec(memory_space=pl.ANY)
    return pl.pallas_call(
        body, name=name, in_specs=[any_spec] * (n + n2), out_specs=[any_spec] * (n + n2),
        out_shape=_exchange_out_shape(xs, gather) + _exchange_out_shape(list(gather_too), True),
        scratch_shapes=_exchange_sems(n) + (_exchange_sems(n2) if n2 else []),
    )(*xs, *gather_too)


def _adamw_rows(g, w, m, v):
    m = ADAM_B1 * m + (1.0 - ADAM_B1) * g
    v = ADAM_B2 * v + (1.0 - ADAM_B2) * (g * g)
    m_hat = m / (1.0 - ADAM_B1 ** ADAM_STEP)
    v_hat = v / (1.0 - ADAM_B2 ** ADAM_STEP)
    return -ADAM_LR * (m_hat / (jnp.sqrt(v_hat) + ADAM_EPS) + ADAM_WD * w), m, v


def _sum_slots(parts):
    g = parts[0].astype(F32)
    for q in range(1, N_DEV):
        g = g + parts[q].astype(F32)
    return g


def _reduce_adamw(parts, w, m, v, name):
    rows, cols = w.shape

    def fn(parts, w, m, v):
        g = _sum_slots(parts)
        return (g,) + _adamw_rows(g, w, m, v)

    return _rowwise(fn, [parts, w, m, v], [], [(cols, F32)] * 4, [], tm=_pick(rows, (256, 128, 64, 32, 16, 8)), name=name)


def _shift(x, n):
    return jnp.pad(x, ((n, 0), (0, 0)))[:-n]


def _unshift(x, n):
    return jnp.pad(x, ((0, n), (0, 0)))[n:]


def _add_n(xs, *, tm, name):
    def fn(*vals):
        s = vals[0]
        for x in vals[1:]:
            s = s + x
        return s
    return _rowwise(fn, xs, [], [(xs[0].shape[1], F32)], [], tm=tm, name=name)[0]


def _norm_fwd(h, g, *, tm, name):
    return _rowwise(lambda x, gg: _rms(x, gg), [h], [g], [(h.shape[1], BF16)], [], tm=tm, name=name)[0]


def _res_norm_fwd(h, f, g, scale, *, tm, name):
    return _rowwise(lambda hh, ff, gg: hh + scale * _rms(ff, gg), [h, f], [g], [(h.shape[1], F32)], [], tm=tm, name=name)[0]


def _norm_bwd(x, g, dy, scale, res, out_dtype, *, tm, name):
    if res is None:
        def fn(xx, dd, gg):
            dx, dg = _rms_bwd(xx, gg, dd * scale)
            return dx, dg
        rows = [x, dy]
    else:
        def fn(xx, dd, rr, gg):
            dx, dg = _rms_bwd(xx, gg, dd * scale)
            return dx + rr, dg
        rows = [x, dy, res]
    return _rowwise(fn, rows, [g], [(x.shape[1], out_dtype)], [(1, x.shape[1])], tm=tm, name=name)


def _ffn_fwd(h, g_pre, g_post, w24, wo4, tiles, tag):
    tb, ts = tiles
    a = _norm_fwd(h, g_pre, tm=ts, name=f"{tag}_norm")
    gu, s4 = _ffn_in(a, w24, tm=tb, name=f"{tag}_in")
    f = _mm(s4, wo4, trans_b=False, tm=tb, tn=D_MODEL, out_dtype=F32, name=f"{tag}_out")
    h_new = _res_norm_fwd(h, f, g_post, 0.5, tm=ts, name=f"{tag}_res")
    return h_new, (h, a, gu, s4, f)


def _ffn_bwd(dh_new, res, g_pre, g_post, w24, wo4, tiles, tag):
    tb, ts = tiles
    h, a, gu, s4, f = res
    t = h.shape[0]
    df, dg_post = _norm_bwd(f, g_post, dh_new, 0.5, None, BF16, tm=ts, name=f"{tag}_dres")
    dgu = _ffn_dswiglu(df, wo4, gu, tm=tb, name=f"{tag}_dswiglu")
    d_wo = _mm_tn(s4, df[None], tk=FFN_BLK, name=f"{tag}_dwout")
    dgu8 = dgu.reshape(2 * w24.shape[1], t, FFN_BLK)
    w8 = w24.reshape(2 * w24.shape[1], D_MODEL, FFN_BLK)
    da = _mm(dgu8, w8, trans_b=True, tm=tb, tn=D_MODEL // 2, out_dtype=F32, name=f"{tag}_da")
    d_win = _mm_tn(a[None], dgu8, tk=D_MODEL, name=f"{tag}_dwin")
    dh, dg_pre = _norm_bwd(h, g_pre, da, 1.0, dh_new, F32, tm=ts, name=f"{tag}_dnorm")
    return dh, dg_pre, dg_post, d_win, d_wo.reshape(N_DEV, -1, D_MODEL)


def _blockdiag(w4):
    n, b, _ = w4.shape
    eye = jnp.eye(n, dtype=w4.dtype)
    return (eye[:, None, :, None] * w4[:, :, None, :]).reshape(n * b, n * b)


def _blockdiag_grad(d):
    n = d.shape[0] // HEAD
    x = d.reshape(n, HEAD, n, HEAD)
    return jnp.stack([x[i, :, i, :] for i in range(n)])


def _row(v):
    return v.reshape(1, -1)


def _mixer_fwd(h, g_pre, g_post, wi, wo, P, bd, tiles, tag, ride=None):
    tb, ts = tiles
    a = _norm_fwd(h, g_pre, tm=ts, name=f"{tag}_norm")
    p = _mm(a[None], wi[None], trans_b=False, tm=tb, tn=N_IN, out_dtype=F32, name=f"{tag}_in")
    lx, lg = p[:, 0:256], p[:, 256:512]
    sb, scc, sx = p[:, 512:768], p[:, 768:1024], p[:, 1024:1280]
    z = p[:, 1280:]
    lxs = [lx, _shift(lx, 1), _shift(lx, 2), _shift(lx, 3)]
    cw = [_row(P['lru_conv_w'][kk]) for kk in range(4)]
    lru_par = cw + [_row(P['lru_conv_b']), _blockdiag(P['lru_wa']), _row(P['lru_ba']), _blockdiag(P['lru_wx']),
                    _row(P['lru_bx']), _row(P['lru_lambda'])]
    la, lb = _rowwise(_lru_pre, lxs, lru_par, [(LRU_W, F32)] * 2, [], tm=ts, name=f"{tag}_lru_pre")
    hs = _lru_scan(la, lb, name=f"{tag}_lru_scan")
    y_lru = _rowwise(lambda gg, hh, ng, b_: _lru_post(b_, gg, hh, ng), [lg, hs], [_row(P['lru_norm_g']), bd],
                     [(LRU_W, F32)], [], tm=ts, name=f"{tag}_lru_post")[0]
    sc_rows = [sb, scc, sx, _shift(scc, 1), _shift(sx, 1), _shift(scc, 2), _shift(sx, 2)]
    sc_par = [_row(P['sc_conv_w'][kk]) for kk in range(3)] + [_row(P['sc_norm_g'])]
    y_sc = _rowwise(lambda *v: _sc_fwd(v[-1], *v[:-1]), sc_rows, sc_par + [bd], [(SC_W, F32)], [], tm=ts,
                    name=f"{tag}_sc")[0]
    cuts = (0, RW_W, 2 * RW_W, 3 * RW_W, RW_IN)
    zs = [z[:, cuts[q]:cuts[q + 1]] for q in range(4)]
    z_rows = zs + [_shift(q, 1) for q in zs]
    pad = lambda m, lo: jnp.pad(m, ((lo, LANES - lo - m.shape[0]), (0, 0)))
    rw_par = [_row(P['rwkv_mu'][cuts[q]:cuts[q + 1]]) for q in range(4)]
    rw_par += [_row(P['rwkv_w0']), pad(P['rwkv_w2'], 0), _row(P['rwkv_a0']), pad(P['rwkv_a2'], 32),
               pad(P['rwkv_g2'], 64), _row(P['rwkv_k_k']), _row(P['rwkv_k_a'])]
    r, w, k2, c, b, v, g = _rowwise(lambda *vv: _rw_pre(vv[-1], *vv[:-1]), z_rows, rw_par + [bd], [(RW_W, F32)] * 7, [],
                                    tm=ts, name=f"{tag}_rw_pre")
    vb = _bcast_cols(v, _stacked_bf16(bd), name=f"{tag}_rw_vcols")
    spre, yt, *rode = _rw_scan(r, _shift(r, 1), w, k2, c, _unshift(c, 1), b, vb, _stacked_bf16(bd), name=f"{tag}_rw_scan",
                               ride=ride)
    y = _unshift(yt, 1)
    post_par =[_row(P['rwkv_lnx_w']), _row(P['rwkv_lnx_b']), _row(P['rwkv_r_k'])]
    y_rw = _rowwise(lambda *vv: _rw_post(vv[-1], *vv[:-1]), [y, r, k2, v, g], post_par + [bd], [(RW_W, F32)], [], tm=ts,
                    name=f"{tag}_rw_post")[0]
    ycat = jnp.concatenate([y_lru, y_sc, y_rw], axis=1).astype(BF16)
    m = _mm(ycat[None], wo[None], trans_b=False, tm=tb, tn=D_MODEL, out_dtype=F32, name=f"{tag}_out")
    h_new = _res_norm_fwd(h, m, g_post, 1.0, tm=ts, name=f"{tag}_res")
    res = dict(h=h, a=a, m=m, ycat=ycat, lxs=lxs, lru_par=lru_par, lg=lg, la=la, hs=hs, sc_rows=sc_rows, sc_par=sc_par,
               z_rows=z_rows, rw_par=rw_par, r=r, w=w, k2=k2, c=c, b=b, v=v, g=g, vb=vb, spre=spre, y=y, post_par=post_par)
    return h_new, res, rode


def _mixer_bwd(dh_new, R, g_pre, g_post, wi, wo, P, bd, tiles, tag, ride=None):
    tb, ts = tiles
    dm, dg_post = _norm_bwd(R['m'], g_post, dh_new, 1.0, None, BF16, tm=ts, name=f"{tag}_dres")
    dycat = _mm(dm[None], wo[None], trans_b=True, tm=tb, tn=D_MODEL, out_dtype=F32, name=f"{tag}_dycat")
    d_wo = _mm_tn(R['ycat'][None], dm[None], tk=D_MODEL // 2, name=f"{tag}_dwout")[0]
    dy_lru, dy_sc, dy_rw = dycat[:, 0:256], dycat[:, 256:512], dycat[:, 512:]
    G = {}
    d_lg, d_hs, G['lru_norm_g'] = _rowwise(
        lambda gg, hh, ct, ng, b_: _vjp_rows(_lru_post, 1, 3, 1)(b_, gg, hh, ng, ct),
        [R['lg'], R['hs'], dy_lru], [_row(P['lru_norm_g']), bd], [(LRU_W, F32)] * 2, [(1, LRU_W)], tm=ts,
        name=f"{tag}_lru_dpost")
    d_la, d_lb = _lru_scan_bwd(_unshift(R['la'], 1), _shift(R['hs'], 1), d_hs, name=f"{tag}_lru_dscan")

    def lru_pre_bwd(x0, x1, x2, x3, ca, cb_, *par):
        return _vjp_rows(_lru_pre, 0, 14, 2)(x0, x1, x2, x3, *par, ca, cb_)

    par_shapes = [tuple(q.shape) for q in R['lru_par']]
    outs = _rowwise(lru_pre_bwd, R['lxs'] + [d_la, d_lb], R['lru_par'], [(LRU_W, F32)] * 4, par_shapes, tm=ts,
                    name=f"{tag}_lru_dpre")
    dxs, dpar = outs[:4], outs[4:]
    d_lx = _add_n([dxs[0], _unshift(dxs[1], 1), _unshift(dxs[2], 2), _unshift(dxs[3], 3)], tm=ts, name=f"{tag}_lru_dx")
    G['lru_conv_w'] = jnp.concatenate(dpar[0:4], axis=0)
    G['lru_conv_b'] = dpar[4][0]
    G['lru_wa'] = _blockdiag_grad(dpar[5])
    G['lru_ba'] = dpar[6][0]
    G['lru_wx'] = _blockdiag_grad(dpar[7])
    G['lru_bx'] = dpar[8][0]
    G['lru_lambda'] = dpar[9][0]
    G['lru_norm_g'] = G['lru_norm_g'][0]

    def sc_bwd(*vv):
        rows7, ct, par4, b_ = vv[:7], vv[7], vv[8:12], vv[12]
        return _vjp_rows(_sc_fwd, 1, 11, 1)(b_, *rows7, *par4, ct)

    outs = _rowwise(sc_bwd, R['sc_rows'] + [dy_sc], R['sc_par'] + [bd], [(SC_W, F32)] * 7, [(1, SC_W)] * 4, tm=ts,
                    name=f"{tag}_sc_bwd")
    d_sb = outs[0]
    d_sc = _add_n([outs[1], _unshift(outs[3], 1), _unshift(outs[5], 2)], tm=ts, name=f"{tag}_sc_dc")
    d_sx = _add_n([outs[2], _unshift(outs[4], 1), _unshift(outs[6], 2)], tm=ts, name=f"{tag}_sc_dx")
    G['sc_conv_w'] = jnp.concatenate(outs[7:10], axis=0)
    G['sc_norm_g'] = outs[10][0]

    def rw_post_bwd(*vv):
        rows5, ct, par3, b_ = vv[:5], vv[5], vv[6:9], vv[9]
        return _vjp_rows(_rw_post, 1, 8, 1)(b_, *rows5, *par3, ct)

    outs = _rowwise(rw_post_bwd, [R['y'], R['r'], R['k2'], R['v'], R['g'], dy_rw], R['post_par'] + [bd],
                    [(RW_W, F32)] * 5, [(1, RW_W)] * 3, tm=ts, name=f"{tag}_rw_dpost")
    d_y, dr_p, dk_p, dv_p, d_g = outs[:5]
    G['rwkv_lnx_w'], G['rwkv_lnx_b'], G['rwkv_r_k'] = outs[5][0], outs[6][0], outs[7][0]
    dyb = _bcast_cols(d_y, _stacked_bf16(bd), name=f"{tag}_rw_dycols", single_pass=True)
    dr_s, d_w, dk_s, d_c, d_b, dvt, *rode = _rw_scan_bwd(R['r'], R['w'], R['k2'], R['c'], R['b'], R['vb'], dyb, R['spre'],
                                                        _stacked_bf16(bd), name=f"{tag}_rw_dscan", ride=ride)
    dv_s = dvt

    def rw_pre_bwd(*vv):
        zrows = vv[0:8]
        dr1, dr2, dw_, dk1, dk2_, dc_, db_, dv1, dv2, dg_ = vv[8:18]
        par, b_ = vv[18:29], vv[29]
        return _vjp_rows(_rw_pre, 1, 19, 7)(b_, *zrows, *par, dr1 + dr2, dw_, dk1 + dk2_, dc_, db_, dv1 + dv2, dg_)

    par_shapes = [tuple(q.shape) for q in R['rw_par']]
    widths = [(q.shape[1], F32) for q in R['z_rows']]
    outs = _rowwise(rw_pre_bwd, R['z_rows'] + [dr_p, dr_s, d_w, dk_p, dk_s, d_c, d_b, dv_p, dv_s, d_g],
                    R['rw_par'] + [bd], widths, par_shapes, tm=ts, name=f"{tag}_rw_dpre")
    d_z = _add_n([jnp.concatenate(outs[0:4], axis=1), _unshift(jnp.concatenate(outs[4:8], axis=1), 1)], tm=ts,
                 name=f"{tag}_rw_dz")
    dpar = outs[8:]
    G['rwkv_mu'] = jnp.concatenate([q[0] for q in dpar[0:4]])
    G['rwkv_w0'], G['rwkv_a0'] = dpar[4][0], dpar[6][0]
    G['rwkv_w2'], G['rwkv_a2'], G['rwkv_g2'] = dpar[5][0:32], dpar[7][32:64], dpar[8][64:128]
    G['rwkv_k_k'], G['rwkv_k_a'] = dpar[9][0], dpar[10][0]

    dp = jnp.concatenate([d_lx, d_lg, d_sb, d_sc, d_sx, d_z], axis=1).astype(BF16)
    da = _mm(dp[None], wi[None], trans_b=True, tm=tb, tn=D_MODEL, out_dtype=F32, name=f"{tag}_da")
    d_wi = _mm_tn(R['a'][None], dp[None], tk=D_MODEL // 2, name=f"{tag}_dwin")[0]
    dh, dg_pre = _norm_bwd(R['h'], g_pre, da, 1.0, dh_new, F32, tm=ts, name=f"{tag}_dnorm")
    return dh, dg_pre, dg_post, d_wi, d_wo, G, rode


def _loss_rows(h, tgt, n_seq, *, tm, name):
    d = h.shape[1]

    def body(h_ref, t_ref, dh_ref, l_ref):
        i = pl.program_id(0)
        row = lax.broadcasted_iota(jnp.int32, (tm, 1), 0) + i * tm
        live = (row >= N_META) & (row < N_META + n_seq)
        e = jnp.where(live, h_ref[...] - t_ref[...], 0.0)
        dh_ref[...] = e * (1.0 / d)
        part = 0.5 * jnp.sum(jnp.sum(e * e, axis=1, keepdims=True) * (1.0 / d), axis=0, keepdims=True)

        @pl.when(i == 0)
        def _():
            l_ref[...] = part

        @pl.when(i > 0)
        def _():
            l_ref[...] += part

    blk = pl.BlockSpec((tm, d), lambda i: (i, 0))
    return pl.pallas_call(body, name=name, grid=(h.shape[0] // tm,), in_specs=[blk, blk],
                          out_specs=[blk, pl.BlockSpec((1, 1), lambda i: (0, 0))],
                          out_shape=[jax.ShapeDtypeStruct(h.shape, F32), jax.ShapeDtypeStruct((1, 1), F32)])(h, tgt)


def _pack(arrs, mult):
    flat = jnp.concatenate([a.reshape(-1).astype(F32) for a in arrs])
    n = flat.shape[0]
    tot = -(-n // mult) * mult
    return jnp.pad(flat, (0, tot - n)).reshape(-1, LANES)


def _unpack(buf, shapes):
    flat = buf.reshape(-1)
    out, off = [], 0
    for s in shapes:
        n = 1
        for q in s:
            n *= q
        out.append(flat[off:off + n].reshape(s))
        off += n
    return out


def _step(W, M, V, x, loss_target):
    n_seq = x.shape[1]
    t_real = N_META + n_seq
    t = (t_real // CHUNK + 1) * CHUNK
    tiles = (_pick(t, (704, 512, 256, 128, 64)), _pick(t, (352, 192, 128, 64)))
    me = 4 * lax.axis_index("x") + 2 * lax.axis_index("y") + lax.axis_index("c")
    n_layer = W['norm_g'].shape[0]

    small_sh = list(SMALL_SHARDED)
    packed = _pack([W[n] for n in small_sh], 8 * LANES)
    early = ['ffn1_w_in', 'ffn1_w_out', 'mix_w_in', 'mix_w_out']
    late = ['ffn2_w_in', 'ffn2_w_out']
    gathered = _exchange([W[n][0].astype(BF16) for n in early] + [packed], gather=True, name="gather_weights")
    big8 = [dict(zip(early, gathered[:-1]))] + [{} for _ in range(n_layer - 1)]
    pieces = [_unpack(gathered[-1][q], [W[n].shape for n in small_sh]) for q in range(N_DEV)]
    full = {n: W[n] for n in SMALL if n not in SMALL_SHARDED}
    for idx, n in enumerate(small_sh):
        full[n] = jnp.concatenate([pieces[q][idx] for q in range(N_DEV)], axis=SMALL_SHARDED[n])

    def ffn_weights(l, which):
        w24 = big8[l][f'{which}_w_in'].reshape(2, N_DEV // 2, D_MODEL, FFN_BLK)
        wo4 = big8[l][f'{which}_w_out'].reshape(N_DEV // 2, FFN_BLK, D_MODEL)
        return w24, wo4

    def mixer_weights(l):
        return big8[l]['mix_w_in'].transpose(1, 0, 2).reshape(D_MODEL, N_IN), big8[l]['mix_w_out'].reshape(D_MODEL, D_MODEL)

    bd = jnp.kron(jnp.eye(LANES // HEAD, dtype=F32), jnp.ones((HEAD, HEAD), F32))
    small_layer = [n for n in SMALL if n not in ('meta_tokens', 'norm_g')]

    h = jnp.concatenate([full['meta_tokens'], x[0], jnp.zeros((t - t_real, D_MODEL), F32)], axis=0)
    saved = []
    for l in range(n_layer):
        ng = [_row(full['norm_g'][l, q]) for q in range(6)]
        P = {n: full[n][l] for n in small_layer}
        w1 = ffn_weights(l, 'ffn1')
        h, r1 = _ffn_fwd(h, ng[0], ng[1], w1[0], w1[1], tiles, f"l{l}_ffn1")
        riders = [(l, n) for n in late] + ([(l + 1, n) for n in early] if l + 1 < n_layer else [])
        wm = mixer_weights(l)
        h, r2, rode = _mixer_fwd(h, ng[2], ng[3], wm[0], wm[1], P, bd, tiles, f"l{l}_mix",
                                 ride=([W[n][q].astype(BF16) for q, n in riders], True))
        for (q, n), arrived in zip(riders, rode):
            big8[q][n] = arrived
        w2 = ffn_weights(l, 'ffn2')
        h, r3 = _ffn_fwd(h, ng[4], ng[5], w2[0], w2[1], tiles, f"l{l}_ffn2")
        saved.append(((w1[0], w1[1], w2[0], w2[1], wm[0], wm[1]), ng, P, r1, r2, r3))

    tgt = jnp.pad(loss_target[0], ((N_META, t - t_real), (0, 0)))
    dh, loss_part = _loss_rows(h, tgt, n_seq, tm=tiles[1], name="loss")
    loss = lax.psum(loss_part[0, 0], MESH_AXES)

    small_grads = [None] * n_layer
    norm_grads = [None] * n_layer
    recv = [{} for _ in range(n_layer)]
    outgoing = []
    for l in reversed(range(n_layer)):
        lw, ng, P, r1, r2, r3 = saved[l]
        dh, g4, g5, d_win2, d_wo2 = _ffn_bwd(dh, r3, ng[4], ng[5], lw[2], lw[3], tiles, f"l{l}_ffn2")
        outgoing += [((l, 'ffn2_w_in'), d_win2), ((l, 'ffn2_w_out'), d_wo2)]
        dh, g2, g3, d_wi, d_wo, G, rode = _mixer_bwd(dh, r2, ng[2], ng[3], lw[4], lw[5], P, bd, tiles, f"l{l}_mix",
                                                     ride=([a for _, a in outgoing], False))
        for ((q, n), _), arrived in zip(outgoing, rode):
            recv[q][n] = arrived
        dh, g0, g1, d_win1, d_wo1 = _ffn_bwd(dh, r1, ng[0], ng[1], lw[0], lw[1], tiles, f"l{l}_ffn1")
        small_grads[l] = G
        norm_grads[l] = jnp.concatenate([g0, g1, g2, g3, g4, g5], axis=0)
        d_wi8 = d_wi.reshape(D_MODEL, N_DEV, N_IN // N_DEV).transpose(1, 0, 2)
        d_wo8 = d_wo.reshape(N_DEV, D_MODEL // N_DEV, D_MODEL)
        outgoing = [((l, 'ffn1_w_in'), d_win1), ((l, 'ffn1_w_out'), d_wo1), ((l, 'mix_w_in'), d_wi8), ((l, 'mix_w_out'), d_wo8)]
    gs = {n: jnp.stack([small_grads[l][n] for l in range(n_layer)]) for n in small_layer}
    gs['norm_g'] = jnp.stack(norm_grads)
    gs['meta_tokens'] = dh[:N_META]
    gpack = _pack([gs[n] for n in SMALL], 8 * LANES)
    *last, gall = _exchange([a for _, a in outgoing], gather=False, name="last_grad_exchange", gather_too=[gpack])
    for ((q, n), _), arrived in zip(outgoing, last):
        recv[q][n] = arrived
    gsum =_rowwise(lambda parts: _sum_slots(parts), [gall], [], [(LANES, F32)], [], tm=gall.shape[1],
                    name="sum_small_grads")[0]
    gfull = dict(zip(SMALL, _unpack(gsum, [gs[n].shape for n in SMALL])))

    def my_shard(n, a):
        if n not in SMALL_SHARDED:
            return a
        ax = SMALL_SHARDED[n]
        size = a.shape[ax] // N_DEV
        return lax.dynamic_slice_in_dim(a, me * size, size, axis=ax)

    g_loc = [my_shard(n, gfull[n]) for n in SMALL]
    shapes = [W[n].shape for n in SMALL]
    bufs = [_pack(g_loc, 8 * LANES)] + [_pack([D[n] for n in SMALL], 8 * LANES) for D in (W, M, V)]
    d_s, m_s, v_s = _rowwise(_adamw_rows, bufs, [], [(LANES, F32)] * 3, [], tm=bufs[0].shape[0], name="adamw_small")
    out = {'grad': dict(zip(SMALL, g_loc)), 'delta': dict(zip(SMALL, _unpack(d_s, shapes))),
           'm': dict(zip(SMALL, _unpack(m_s, shapes))), 'v': dict(zip(SMALL, _unpack(v_s, shapes)))}

    order = ['ffn1_w_in', 'ffn1_w_out', 'ffn2_w_in', 'ffn2_w_out', 'mix_w_in', 'mix_w_out']
    for idx, n in enumerate(order):
        per_layer = []
        for l in range(n_layer):
            parts = recv[l][n]
            rows, cols = W[n].shape[1], W[n].shape[2]
            per_layer.append(_reduce_adamw(parts.reshape(N_DEV, rows, cols), W[n][l], M[n][l], V[n][l],
                                           name=f"l{l}_adamw_{n}"))
        for q, key in enumerate(('grad', 'delta', 'm', 'v')):
            out[key][n] = jnp.stack([per_layer[l][q] for l in range(n_layer)])

    return (loss, dh[N_META:t_real][None],
            *[out['grad'][n] for n in WEIGHTS], *[out['delta'][n] for n in WEIGHTS],
            *[out['m'][n] for n in WEIGHTS], *[out['v'][n] for n in WEIGHTS])


def kernel(x, meta_tokens, norm_g, ffn1_w_in, ffn1_w_out, ffn2_w_in, ffn2_w_out, mix_w_in, mix_w_out, lru_conv_w, lru_conv_b, lru_wa, lru_ba, lru_wx, lru_bx, lru_lambda, lru_norm_g, sc_conv_w, sc_norm_g, rwkv_mu, rwkv_w0, rwkv_w2, rwkv_a0, rwkv_a2, rwkv_g2, rwkv_k_k, rwkv_k_a, rwkv_r_k, rwkv_lnx_w, rwkv_lnx_b, loss_target, m_meta_tokens, m_norm_g, m_ffn1_w_in, m_ffn1_w_out, m_ffn2_w_in, m_ffn2_w_out, m_mix_w_in, m_mix_w_out, m_lru_conv_w, m_lru_conv_b, m_lru_wa, m_lru_ba, m_lru_wx, m_lru_bx, m_lru_lambda, m_lru_norm_g, m_sc_conv_w, m_sc_norm_g, m_rwkv_mu, m_rwkv_w0, m_rwkv_w2, m_rwkv_a0, m_rwkv_a2, m_rwkv_g2, m_rwkv_k_k, m_rwkv_k_a, m_rwkv_r_k, m_rwkv_lnx_w, m_rwkv_lnx_b, v_meta_tokens, v_norm_g, v_ffn1_w_in, v_ffn1_w_out, v_ffn2_w_in, v_ffn2_w_out, v_mix_w_in, v_mix_w_out, v_lru_conv_w, v_lru_conv_b, v_lru_wa, v_lru_ba, v_lru_wx, v_lru_bx, v_lru_lambda, v_lru_norm_g, v_sc_conv_w, v_sc_norm_g, v_rwkv_mu, v_rwkv_w0, v_rwkv_w2, v_rwkv_a0, v_rwkv_a2, v_rwkv_g2, v_rwkv_k_k, v_rwkv_k_a, v_rwkv_r_k, v_rwkv_lnx_w, v_rwkv_lnx_b):
    w_vals = (meta_tokens, norm_g, ffn1_w_in, ffn1_w_out, ffn2_w_in, ffn2_w_out, mix_w_in, mix_w_out, lru_conv_w, lru_conv_b, lru_wa, lru_ba, lru_wx, lru_bx, lru_lambda, lru_norm_g, sc_conv_w, sc_norm_g, rwkv_mu, rwkv_w0, rwkv_w2, rwkv_a0, rwkv_a2, rwkv_g2, rwkv_k_k, rwkv_k_a, rwkv_r_k, rwkv_lnx_w, rwkv_lnx_b)
    m_vals = (m_meta_tokens, m_norm_g, m_ffn1_w_in, m_ffn1_w_out, m_ffn2_w_in, m_ffn2_w_out, m_mix_w_in, m_mix_w_out, m_lru_conv_w, m_lru_conv_b, m_lru_wa, m_lru_ba, m_lru_wx, m_lru_bx, m_lru_lambda, m_lru_norm_g, m_sc_conv_w, m_sc_norm_g, m_rwkv_mu, m_rwkv_w0, m_rwkv_w2, m_rwkv_a0, m_rwkv_a2, m_rwkv_g2, m_rwkv_k_k, m_rwkv_k_a, m_rwkv_r_k, m_rwkv_lnx_w, m_rwkv_lnx_b)
    v_vals = (v_meta_tokens, v_norm_g, v_ffn1_w_in, v_ffn1_w_out, v_ffn2_w_in, v_ffn2_w_out, v_mix_w_in, v_mix_w_out, v_lru_conv_w, v_lru_conv_b, v_lru_wa, v_lru_ba, v_lru_wx, v_lru_bx, v_lru_lambda, v_lru_norm_g, v_sc_conv_w, v_sc_norm_g, v_rwkv_mu, v_rwkv_w0, v_rwkv_w2, v_rwkv_a0, v_rwkv_a2, v_rwkv_g2, v_rwkv_k_k, v_rwkv_k_a, v_rwkv_r_k, v_rwkv_lnx_w, v_rwkv_lnx_b)
    return _step(dict(zip(WEIGHTS, w_vals)), dict(zip(WEIGHTS, m_vals)), dict(zip(WEIGHTS, v_vals)), x, loss_target)
```

```python
import functools

import jax
import jax.numpy as jnp
from jax import lax
from jax.experimental import pallas as pl
from jax.experimental.pallas import tpu as pltpu

F32 = jnp.float32
BF16 = jnp.bfloat16
PARAM_DOT = lax.Precision.DEFAULT

N_DEV = 8
MESH_AXES = ("x", "y", "c")
N_META = 16
D_MODEL = 1024
LRU_W = 256
SC_W = 256
RW_W = 512
HEAD = 64
LANES = 128
CHUNK = 64
RW_IN = 1664
N_IN = 2944
FFN_BLK = 704
RMS_EPS = 1e-6
LNX_EPS = 64e-5
LRU_C = 8.0
ADAM_LR, ADAM_B1, ADAM_B2, ADAM_EPS, ADAM_WD, ADAM_STEP = 0.001, 0.9, 0.999, 1e-08, 0.01, 10

WEIGHTS = ['meta_tokens', 'norm_g', 'ffn1_w_in', 'ffn1_w_out', 'ffn2_w_in', 'ffn2_w_out', 'mix_w_in', 'mix_w_out',
           'lru_conv_w', 'lru_conv_b', 'lru_wa', 'lru_ba', 'lru_wx', 'lru_bx', 'lru_lambda', 'lru_norm_g',
           'sc_conv_w', 'sc_norm_g', 'rwkv_mu', 'rwkv_w0', 'rwkv_w2', 'rwkv_a0', 'rwkv_a2', 'rwkv_g2', 'rwkv_k_k',
           'rwkv_k_a', 'rwkv_r_k', 'rwkv_lnx_w', 'rwkv_lnx_b']
BIG = ['ffn1_w_in', 'ffn1_w_out', 'ffn2_w_in', 'ffn2_w_out', 'mix_w_in', 'mix_w_out']
SMALL_SHARDED = {'meta_tokens': 1, 'norm_g': 2, 'lru_conv_w': 2, 'sc_conv_w': 2, 'rwkv_w2': 2, 'rwkv_a2': 2, 'rwkv_g2': 2}
SMALL = [n for n in WEIGHTS if n not in BIG]


def _pick(n, cands):
    for c in cands:
        if n % c == 0:
            return c
    raise ValueError(f"no tile for {n}")


def _rowwise(fn, rows, params, row_outs, acc_outs, *, tm, name):
    nr, npar, nro, nao = len(rows), len(params), len(row_outs), len(acc_outs)
    n_rows = rows[0].shape[-2]
    assert n_rows % tm == 0, (name, n_rows, tm)

    def body(*refs):
        vals = [r[...] for r in refs[:nr + npar]]
        outs = fn(*vals)
        if not isinstance(outs, (tuple, list)):
            outs = (outs,)
        assert len(outs) == nro + nao, (name, len(outs))
        for o_ref, o in zip(refs[nr + npar:nr + npar + nro], outs[:nro]):
            o_ref[...] = o.astype(o_ref.dtype)
        step = pl.program_id(0)
        for a_ref, a in zip(refs[nr + npar + nro:], outs[nro:]):
            @pl.when(step == 0)
            def _(a_ref=a_ref, a=a):
                a_ref[...] = a.astype(F32)

            @pl.when(step > 0)
            def _(a_ref=a_ref, a=a):
                a_ref[...] += a.astype(F32)

    def row_spec(shape):
        if len(shape) == 2:
            return pl.BlockSpec((tm, shape[1]), lambda i: (i, 0))
        return pl.BlockSpec((shape[0], tm, shape[2]), lambda i: (0, i, 0))

    def full_spec(shape):
        nd = len(shape)
        return pl.BlockSpec(tuple(shape), lambda i, nd=nd: (0,) * nd)

    in_specs = [row_spec(r.shape) for r in rows] + [full_spec(p.shape) for p in params]
    out_shape = [jax.ShapeDtypeStruct((n_rows, w), dt) for (w, dt) in row_outs]
    out_shape += [jax.ShapeDtypeStruct(tuple(s), F32) for s in acc_outs]
    out_specs = [row_spec((n_rows, w)) for (w, _) in row_outs] + [full_spec(s) for s in acc_outs]
    res = pl.pallas_call(body, name=name, grid=(n_rows // tm,), in_specs=in_specs, out_specs=out_specs,
                         out_shape=out_shape)(*rows, *params)
    return tuple(res)


def _mm(a3, b3, *, trans_b, tm, tn, out_dtype, name):
    nj, m, kb = a3.shape
    n = b3.shape[1] if trans_b else b3.shape[2]
    dims = (((1,), (1,)), ((), ())) if trans_b else (((1,), (0,)), ((), ()))

    def body(a_ref, b_ref, o_ref):
        acc = lax.dot_general(a_ref[0], b_ref[0], dims, preferred_element_type=F32)
        for j in range(1, nj):
            acc = acc + lax.dot_general(a_ref[j], b_ref[j], dims, preferred_element_type=F32)
        o_ref[...] = acc.astype(o_ref.dtype)

    if trans_b:
        b_spec = pl.BlockSpec((nj, tn, kb), lambda i, c: (0, c, 0))
    else:
        b_spec = pl.BlockSpec((nj, kb, tn), lambda i, c: (0, 0, c))
    return pl.pallas_call(
        body, name=name, grid=(m // tm, n // tn),
        in_specs=[pl.BlockSpec((nj, tm, kb), lambda i, c: (0, i, 0)), b_spec],
        out_specs=pl.BlockSpec((tm, tn), lambda i, c: (i, c)),
        out_shape=jax.ShapeDtypeStruct((m, n), out_dtype),
    )(a3, b3)


def _mm_tn(a3, b3, *, tk, name):
    ja, t, ka = a3.shape
    jb, _, n = b3.shape
    nj = max(ja, jb)

    def body(a_ref, b_ref, o_ref):
        o_ref[0] = lax.dot_general(a_ref[0], b_ref[0], (((0,), (0,)), ((), ())),
                                   preferred_element_type=F32).astype(o_ref.dtype)

    return pl.pallas_call(
        body, name=name, grid=(nj, ka // tk),
        in_specs=[pl.BlockSpec((1, t, tk), (lambda j, c: (j, 0, c)) if ja > 1 else (lambda j, c: (0, 0, c))),
                  pl.BlockSpec((1, t, n), (lambda j, c: (j, 0, 0)) if jb > 1 else (lambda j, c: (0, 0, 0)))],
        out_specs=pl.BlockSpec((1, tk, n), lambda j, c: (j, c, 0)),
        out_shape=jax.ShapeDtypeStruct((nj, ka, n), BF16),
    )(a3, b3)


def _ffn_in(a, w24, *, tm, name):
    t, d = a.shape
    nb, fb = w24.shape[1], w24.shape[3]

    def body(a_ref, w_ref, gu_ref, s_ref):
        x = a_ref[...]
        g = jnp.dot(x, w_ref[0, 0], preferred_element_type=F32)
        u = jnp.dot(x, w_ref[1, 0], preferred_element_type=F32)
        gu_ref[0, 0] = g.astype(BF16)
        gu_ref[1, 0] = u.astype(BF16)
        s_ref[0] = (g * jax.nn.sigmoid(g) * u).astype(BF16)

    return pl.pallas_call(
        body, name=name, grid=(nb, t // tm),
        in_specs=[pl.BlockSpec((tm, d), lambda j, i: (i, 0)), pl.BlockSpec((2, 1, d, fb), lambda j, i: (0, j, 0, 0))],
        out_specs=[pl.BlockSpec((2, 1, tm, fb), lambda j, i: (0, j, i, 0)), pl.BlockSpec((1, tm, fb), lambda j, i: (j, i, 0))],
        out_shape=[jax.ShapeDtypeStruct((2, nb, t, fb), BF16), jax.ShapeDtypeStruct((nb, t, fb), BF16)],
    )(a, w24)


def _ffn_dswiglu(df, wo4, gu, *, tm, name):
    t, d = df.shape
    nb, fb = wo4.shape[0], wo4.shape[1]

    def body(df_ref, wo_ref, gu_ref, dg_ref):
        ds = lax.dot_general(df_ref[...], wo_ref[0], (((1,), (1,)), ((), ())), preferred_element_type=F32)
        g = gu_ref[0, 0].astype(F32)
        u = gu_ref[1, 0].astype(F32)
        sig = jax.nn.sigmoid(g)
        dg_ref[0, 0] = (ds * u * sig * (1.0 + g * (1.0 - sig))).astype(BF16)
        dg_ref[1, 0] = (ds * g * sig).astype(BF16)

    return pl.pallas_call(
        body, name=name, grid=(nb, t // tm),
        in_specs=[pl.BlockSpec((tm, d), lambda j, i: (i, 0)), pl.BlockSpec((1, fb, d), lambda j, i: (j, 0, 0)),
                  pl.BlockSpec((2, 1, tm, fb), lambda j, i: (0, j, i, 0))],
        out_specs=pl.BlockSpec((2, 1, tm, fb), lambda j, i: (0, j, i, 0)),
        out_shape=jax.ShapeDtypeStruct((2, nb, t, fb), BF16),
    )(df, wo4, gu)


def _rms(x, g):
    return x * lax.rsqrt(jnp.mean(x * x, axis=-1, keepdims=True) + RMS_EPS) * g


def _rms_bwd(x, g, dy):
    rstd = lax.rsqrt(jnp.mean(x * x, axis=-1, keepdims=True) + RMS_EPS)
    xh = x * rstd
    dxh = dy * g
    dx = rstd * (dxh - xh * jnp.mean(dxh * xh, axis=-1, keepdims=True))
    return dx, jnp.sum(dy * xh, axis=0, keepdims=True)


def _seg_sum_impl(x, bd):
    bd2 = jnp.concatenate([bd, bd], axis=0).astype(BF16)
    hi = x.astype(BF16)
    lo = (x - hi.astype(F32)).astype(BF16)
    parts = [jnp.dot(jnp.concatenate([hi[:, q * LANES:(q + 1) * LANES], lo[:, q * LANES:(q + 1) * LANES]], axis=1), bd2,
                     preferred_element_type=F32) for q in range(x.shape[1] // LANES)]
    return parts[0] if len(parts) == 1 else jnp.concatenate(parts, axis=1)


@jax.custom_vjp
def _seg_sum(x, bd):
    return _seg_sum_impl(x, bd)


def _seg_sum_fwd(x, bd):
    return _seg_sum_impl(x, bd), bd


def _seg_sum_bwd(bd, ct):
    return _seg_sum_impl(ct, bd), jnp.zeros_like(bd)


_seg_sum.defvjp(_seg_sum_fwd, _seg_sum_bwd)


def _group_rms(y, g, bd):
    return y * lax.rsqrt(_seg_sum(y * y, bd) * (1.0 / HEAD) + RMS_EPS) * g


def _expm1(x):
    return jnp.where(jnp.abs(x) < 1e-2, x * (1.0 + x * (0.5 + x * (1.0 / 6.0))), jnp.exp(x) - 1.0)


def _lru_pre(x0, x1, x2, x3, cw0, cw1, cw2, cw3, cb, wa, ba, wx, bx, lam):
    u = x3 * cw0 + x2 * cw1 + x1 * cw2 + x0 * cw3 + cb
    r = jax.nn.sigmoid(jnp.dot(u, wa, preferred_element_type=F32, precision=PARAM_DOT) + ba)
    i = jax.nn.sigmoid(jnp.dot(u, wx, preferred_element_type=F32, precision=PARAM_DOT) + bx)
    log_a = -LRU_C * r * jax.nn.softplus(-lam)
    return jnp.exp(log_a), jnp.sqrt(-_expm1(2.0 * log_a)) * (i * u)


def _lru_post(bd, gate, hs, ng):
    return _group_rms(jax.nn.gelu(gate) * hs, ng, bd)


def _sc_fwd(bd, b, c0, x0, c1, x1, c2, x2, w0, w1, w2, ng):
    return _group_rms(b * (w0 * (c2 * x2) + w1 * (c1 * x1) + w2 * (c0 * x0)), ng, bd)


def _rw_pre(bd, zr, zk, zv, zt, sr, sk, sv, st, mur, muk, muv, mut, w0, w2p, a0, a2p, g2p, k_k, k_a):
    r, k, v, tail = zr + (sr - zr) * mur, zk + (sk - zk) * muk, zv + (sv - zv) * muv, zt + (st - zt) * mut
    lane = lax.broadcasted_iota(jnp.int32, tail.shape, 1)
    act = jnp.where(lane < 32, jnp.tanh(tail), jnp.where(lane < 64, tail, jax.nn.sigmoid(tail)))
    dot = functools.partial(jnp.dot, preferred_element_type=F32, precision=PARAM_DOT)
    w_log = -jax.nn.softplus(-(w0 + dot(act, w2p))) - 0.5
    w = jnp.exp(-jnp.exp(w_log))
    a = jax.nn.sigmoid(a0 + dot(act, a2p))
    g = dot(act, g2p)
    kk = k * k_k
    k2 = k * (1.0 + (a - 1.0) * k_a)
    kkn = kk * lax.rsqrt(jnp.maximum(_seg_sum(kk * kk, bd), 1e-24))
    return r, w, k2, -kkn, kkn * a, v, g


def _rw_post(bd, y, r, k2, v, g, lnw, lnb, r_k):
    mean = _seg_sum(y, bd) * (1.0 / HEAD)
    yc = y - mean
    var = _seg_sum(yc * yc, bd) * (1.0 / HEAD)
    yn = yc * lax.rsqrt(var + LNX_EPS) * lnw + lnb
    return (yn + _seg_sum(r * k2 * r_k, bd) * v) * g


def _vjp_rows(fwd, n_static, n_in, n_ct):
    def fn(*args):
        static, prim, cts = args[:n_static], args[n_static:n_static + n_in], args[n_static + n_in:]
        assert len(cts) == n_ct
        _, vjp = jax.vjp(functools.partial(fwd, *static), *prim)
        return vjp(cts[0] if n_ct == 1 else tuple(cts))
    return fn


def _all_to_all_copies(x_refs, o_refs, sems):
    send_sems, recv_sems, local_sems = sems
    mx, my, mc = lax.axis_index("x"), lax.axis_index("y"), lax.axis_index("c")
    me = 4 * mx + 2 * my + mc
    local, sends, recvs = [], [], []
    for k in range(len(x_refs)):
        local.append(pltpu.make_async_copy(x_refs[k].at[me], o_refs[k].at[me], local_sems.at[k]))
    for d in range(1, N_DEV):
        px, py, pc = mx ^ ((d >> 2) & 1), my ^ ((d >> 1) & 1), mc ^ (d & 1)
        peer = 4 * px + 2 * py + pc
        for k in range(len(x_refs)):
            common = dict(src_ref=x_refs[k].at[peer], send_sem=send_sems.at[k, d - 1], recv_sem=recv_sems.at[k, d - 1],
                          device_id=(px, py, pc), device_id_type=pl.DeviceIdType.MESH)
            sends.append(pltpu.make_async_remote_copy(dst_ref=o_refs[k].at[me], **common))
            recvs.append(pltpu.make_async_remote_copy(dst_ref=o_refs[k].at[peer], **common))
    return local, sends, recvs


def _gather2_copies(x_refs, o_refs, sems):
    send_sems, recv_sems, local_sems = sems
    mx, my, mc = lax.axis_index("x"), lax.axis_index("y"), lax.axis_index("c")
    sibling = (mx, my, 1 - mc)
    chips = [(1 - mx, my), (mx, 1 - my), (1 - mx, 1 - my)]

    def slot(px, py, pc):
        return 4 * px + 2 * py + pc

    out = dict(local=[], first=[], first_recv=[], ici_recv=[], passed=[], passed_recv=[])
    for k in range(len(x_refs)):
        def copy(sem, src, dst_slot, to, k=k):
            return pltpu.make_async_remote_copy(src_ref=src, dst_ref=o_refs[k].at[dst_slot], send_sem=send_sems.at[k, sem],
                                                recv_sem=recv_sems.at[k, sem], device_id=to, device_id_type=pl.DeviceIdType.MESH)
        me = slot(mx, my, mc)
        out['local'].append(pltpu.make_async_copy(x_refs[k], o_refs[k].at[me], local_sems.at[k]))
        out['first'].append(copy(0, x_refs[k], me, sibling))
        out['first_recv'].append(copy(0, x_refs[k], slot(mx, my, 1 - mc), sibling))
        for j, (px, py) in enumerate(chips):
            out['first'].append(copy(1 + j, x_refs[k], me, (px, py, mc)))
            out['ici_recv'].append(copy(1 + j, x_refs[k], slot(px, py, mc), (px, py, mc)))
            out['passed'].append(copy(4 + j, o_refs[k].at[slot(px, py, mc)], slot(px, py, mc), sibling))
            out['passed_recv'].append(copy(4 + j, x_refs[k], slot(px, py, 1 - mc), sibling))
    return out


def _exchange_start(x_refs, o_refs, sems, gather):
    if gather:
        cps = _gather2_copies(x_refs, o_refs, sems)
        for cp in cps['local'] + cps['first']:
            cp.start()
        return
    local, sends, _ = _all_to_all_copies(x_refs, o_refs, sems)
    for cp in local + sends:
        cp.start()


def _exchange_wait(x_refs, o_refs, sems, gather):
    if gather:
        cps = _gather2_copies(x_refs, o_refs, sems)
        for arrived, onward in zip(cps['ici_recv'], cps['passed']):
            arrived.wait_recv()
            onward.start()
        for cp in cps['first'] + cps['passed']:
            cp.wait_send()
        for cp in cps['first_recv'] + cps['passed_recv']:
            cp.wait_recv()
        for cp in cps['local']:
            cp.wait()
        return
    local, sends, recvs = _all_to_all_copies(x_refs, o_refs, sems)
    for cp in sends:
        cp.wait_send()
    for cp in recvs:
        cp.wait_recv()
    for cp in local:
        cp.wait()


def _exchange_out_shape(xs, gather):
    return [jax.ShapeDtypeStruct(((N_DEV,) + x.shape) if gather else x.shape, x.dtype) for x in xs]


def _exchange_sems(n):
    return [pltpu.SemaphoreType.DMA((n, N_DEV - 1)), pltpu.SemaphoreType.DMA((n, N_DEV - 1)), pltpu.SemaphoreType.DMA((n,))]


SUBLANES = 8


def _store_row(ref, i, cols, row):
    base = pl.multiple_of((i // SUBLANES) * SUBLANES, SUBLANES)
    sub = lax.broadcasted_iota(jnp.int32, (SUBLANES, row.shape[1]), 0)
    ref[pl.ds(base, SUBLANES), cols] = jnp.where(sub == i % SUBLANES, row, ref[pl.ds(base, SUBLANES), cols])


def _tile_scan(a, b, reverse):
    sub = lax.broadcasted_iota(jnp.int32, a.shape, 0)
    for sh in (1, 2, 4):
        if reverse:
            live = sub < SUBLANES - sh
            a_s, b_s = pltpu.roll(a, SUBLANES - sh, 0), pltpu.roll(b, SUBLANES - sh, 0)
        else:
            live = sub >= sh
            a_s, b_s = pltpu.roll(a, sh, 0), pltpu.roll(b, sh, 0)
        b = jnp.where(live, a * b_s, 0.0) + b
        a = jnp.where(live, a * a_s, a)
    return a, b


def _lru_scan(a, b, name):
    t, w = a.shape

    def body(a_ref, b_ref, h_ref):
        def tile(j, h):
            rows = pl.ds(pl.multiple_of(j * SUBLANES, SUBLANES), SUBLANES)
            ca, cb = _tile_scan(a_ref[rows, :], b_ref[rows, :], False)
            out = ca * h + cb
            h_ref[rows, :] = out
            return out[SUBLANES - 1:SUBLANES]
        lax.fori_loop(0, t // SUBLANES, tile, jnp.zeros((1, w), F32))

    return pl.pallas_call(body, name=name, out_shape=jax.ShapeDtypeStruct((t, w), F32))(a, b)


def _lru_scan_bwd(a_next, h_prev, dhs, name):
    t, w = dhs.shape

    def body(a_ref, h_ref, dh_ref, da_ref, db_ref):
        def tile(n, lam):
            rows = pl.ds(pl.multiple_of((t // SUBLANES - 1 - n) * SUBLANES, SUBLANES), SUBLANES)
            ca, cb = _tile_scan(a_ref[rows, :], dh_ref[rows, :], True)
            out = ca * lam + cb
            db_ref[rows, :] = out
            da_ref[rows, :] = out * h_ref[rows, :]
            return out[0:1]
        lax.fori_loop(0, t // SUBLANES, tile, jnp.zeros((1, w), F32))

    return pl.pallas_call(body, name=name, out_shape=[jax.ShapeDtypeStruct((t, w), F32)] * 2)(a_next, h_prev, dhs)


N_PAIR = RW_W // LANES


def _bcast_cols(v, bd2, name):
    t = v.shape[0]

    def body(v_ref, bd_ref, o_ref):
        bd1 = bd_ref[...][0:LANES]
        sub = lax.broadcasted_iota(jnp.int32, (HEAD, LANES), 0)
        own = lax.broadcasted_iota(jnp.int32, (HEAD, LANES), 1) % HEAD == sub
        for i in range(CHUNK):
            row = v_ref[i:i + 1, :]
            prods = [jnp.where(own, row[:, p * LANES:(p + 1) * LANES], 0.0) for p in range(N_PAIR)]
            sums = jnp.dot(jnp.concatenate(prods, axis=0).astype(BF16), bd1, preferred_element_type=F32)
            for p in range(N_PAIR):
                o_ref[i, p] = sums[p * HEAD:(p + 1) * HEAD]

    return pl.pallas_call(
        body, name=name, grid=(t // CHUNK,),
        in_specs=[pl.BlockSpec((CHUNK, RW_W), lambda i: (i, 0)), pl.BlockSpec((2 * LANES, LANES), lambda i: (0, 0))],
        out_specs=pl.BlockSpec((CHUNK, N_PAIR, HEAD, LANES), lambda i: (i, 0, 0, 0)),
        out_shape=jax.ShapeDtypeStruct((t, N_PAIR, HEAD, LANES), F32),
    )(v, bd2)


def _cols_to_rows(cols_ref, rows_ref):
    lane = lax.broadcasted_iota(jnp.int32, (CHUNK, LANES), 1)
    for p in range(N_PAIR):
        tile = cols_ref[p]
        sq = jnp.concatenate([tile, jnp.zeros_like(tile)], axis=0).T
        rows_ref[:, p * LANES:(p + 1) * LANES] = jnp.where(lane < HEAD, sq[0:CHUNK], pltpu.roll(sq[CHUNK:2 * CHUNK], HEAD, 1))


def _stacked_bf16(bd):
    return jnp.concatenate([bd, bd], axis=0).astype(BF16)


def _group_sums(prods, bd2):
    x = jnp.concatenate(prods, axis=0)
    hi = x.astype(BF16)
    lo = (x - hi.astype(F32)).astype(BF16)
    return jnp.dot(jnp.concatenate([hi, lo], axis=1), bd2, preferred_element_type=F32)


def _rw_scan(r, r_prev, w, k, c, c_next, b, vb, bd2, name, ride=None):
    t = w.shape[0]
    nch = t // CHUNK
    ride_xs, ride_gather = ride if ride is not None else ([], False)
    n_ride = len(ride_xs)

    def body(*refs):
        r_ref, rp_ref, w_ref, k_ref, c_ref, cn_ref, b_ref, vb_ref, bd_ref = refs[:9]
        x_refs = refs[9:9 + n_ride]
        spre_ref, y_ref = refs[9 + n_ride:11 + n_ride]
        o_refs = refs[11 + n_ride:11 + 2 * n_ride]
        s_ref, wc_ref, wr_ref, bc_ref, kc_ref, br_ref, kr_ref, yt_ref = refs[11 + 2 * n_ride:19 + 2 * n_ride]
        sems = refs[19 + 2 * n_ride:]

        @pl.when(pl.program_id(0) == 0)
        def _():
            s_ref[...] = jnp.zeros_like(s_ref)
            if n_ride:
                _exchange_start(x_refs, o_refs, sems, ride_gather)

        yt_ref[...] = jnp.zeros_like(yt_ref)
        bdv = bd_ref[...]
        lane = lax.broadcasted_iota(jnp.int32, (HEAD, LANES), 1) % CHUNK

        wv, cn, rv, bv, kv = w_ref[...], cn_ref[...], r_ref[...], b_ref[...], k_ref[...]
        wc_ref[...] = wv * cn
        wr_ref[...] = wv * rv
        for ref, x in ((bc_ref, bv * cn), (kc_ref, kv * cn), (br_ref, bv * rv), (kr_ref, kv * rv)):
            sums = _group_sums([x[:, q * LANES:(q + 1) * LANES] for q in range(N_PAIR)], bdv)
            for q in range(N_PAIR):
                ref[:, q * LANES:(q + 1) * LANES] = sums[q * CHUNK:(q + 1) * CHUNK]

        def cut(ref, i):
            x = ref[pl.ds(i, 1), :]
            return [x[:, p * LANES:(p + 1) * LANES] for p in range(N_PAIR)]

        def two_steps(j, st):
            i0 = 2 * j
            i1 = i0 + 1
            c0, wc0, rp0, wr0 = cut(c_ref, i0), cut(wc_ref, i0), cut(rp_ref, i0), cut(wr_ref, i0)
            w0, b0, k0, w1, b1, k1 = cut(w_ref, i0), cut(b_ref, i0), cut(k_ref, i0), cut(w_ref, i1), cut(b_ref, i1), cut(k_ref, i1)
            bc0, kc0, br0, kr0 = cut(bc_ref, i0), cut(kc_ref, i0), cut(br_ref, i0), cut(kr_ref, i0)
            pairs = range(N_PAIR)
            red = _group_sums([st[p] * c0[p] for p in pairs] + [st[p] * wc0[p] for p in pairs], bdv)
            out = _group_sums([st[p] * rp0[p] for p in pairs] + [st[p] * wr0[p] for p in pairs], bdv)
            new = []
            for p in pairs:
                v0, v1 = vb_ref[i0, p], vb_ref[i1, p]
                sa0 = red[p * HEAD:(p + 1) * HEAD]
                sa1 = red[(N_PAIR + p) * HEAD:(N_PAIR + p + 1) * HEAD] + sa0 * bc0[p] + v0 * kc0[p]
                y_before = out[p * HEAD:(p + 1) * HEAD]
                y0 = out[(N_PAIR + p) * HEAD:(N_PAIR + p + 1) * HEAD] + sa0 * br0[p] + v0 * kr0[p]
                spre_ref[i0, p] = st[p]
                s1 = st[p] * w0[p] + sa0 * b0[p] + v0 * k0[p]
                spre_ref[i1, p] = s1
                new.append(s1 * w1[p] + sa1 * b1[p] + v1 * k1[p])
                yt_ref[p] = jnp.where(lane == i0, y_before, jnp.where(lane == i1, y0, yt_ref[p]))
            return tuple(new)

        st = lax.fori_loop(0, CHUNK // 2, two_steps, tuple(s_ref[p] for p in range(N_PAIR)))
        for p in range(N_PAIR):
            s_ref[p] = st[p]
        _cols_to_rows(yt_ref, y_ref)

        if n_ride:
            @pl.when(pl.program_id(0) == nch - 1)
            def _():
                _exchange_wait(x_refs, o_refs, sems, ride_gather)

    row = pl.BlockSpec((CHUNK, RW_W), lambda i: (i, 0))
    big = pl.BlockSpec((CHUNK, N_PAIR, HEAD, LANES), lambda i: (i, 0, 0, 0))
    any_spec = pl.BlockSpec(memory_space=pl.ANY)
    return pl.pallas_call(
        body, name=name, grid=(nch,),
        in_specs=[row] * 7 + [big, pl.BlockSpec((2 * LANES, LANES), lambda i: (0, 0))] + [any_spec] * n_ride,
        out_specs=[big, row] + [any_spec] * n_ride,
        out_shape=[jax.ShapeDtypeStruct((t, N_PAIR, HEAD, LANES), F32), jax.ShapeDtypeStruct((t, RW_W), F32)]
        + _exchange_out_shape(ride_xs, ride_gather),
        scratch_shapes=[pltpu.VMEM((N_PAIR, HEAD, LANES), F32)] + [pltpu.VMEM((CHUNK, RW_W), F32)] * 6
        + [pltpu.VMEM((N_PAIR, HEAD, LANES), F32)] + (_exchange_sems(n_ride) if n_ride else []),
    )(r, r_prev, w, k, c, c_next, b, vb, bd2, *ride_xs)


def _rw_scan_bwd(r, w, k, c, b, vb, dyb, spre, bd, name, ride=None):
    t = r.shape[0]
    nch = t // CHUNK
    ride_xs, ride_gather = ride if ride is not None else ([], False)
    n_ride = len(ride_xs)
    n_pre = 8

    def body(*refs):
        r_ref, w_ref, k_ref, c_ref, b_ref, vb_ref, dyb_ref, spre_ref, bd_ref = refs[:9]
        x_refs = refs[9:9 + n_ride]
        dr_ref, dw_ref, dk_ref, dc_ref, db_ref, dv_ref = refs[9 + n_ride:15 + n_ride]
        o_refs = refs[15 + n_ride:15 + 2 * n_ride]
        g_ref, snext_ref, dvt_ref = refs[15 + 2 * n_ride:18 + 2 * n_ride]
        pre = refs[18 + 2 * n_ride:18 + 2 * n_ride + n_pre]
        sems = refs[18 + 2 * n_ride + n_pre:]
        wb_ref, wk_ref, rb_ref, rk_ref, rwb_ref, cb_ref, rwk_ref, ck_ref = pre[:8]

        @pl.when(pl.program_id(0) == 0)
        def _():
            g_ref[...] = jnp.zeros_like(g_ref)
            snext_ref[...] = jnp.zeros_like(snext_ref)
            if n_ride:
                _exchange_start(x_refs, o_refs, sems, ride_gather)

        for ref in (dr_ref, dw_ref, dk_ref, dc_ref, db_ref, dvt_ref):
            ref[...] = jnp.zeros_like(ref)
        bdv = bd_ref[...]
        bd1 = bdv[0:LANES]
        lane = lax.broadcasted_iota(jnp.int32, (HEAD, LANES), 1) % CHUNK

        rv, wv, kv, cv, bv = r_ref[...], w_ref[...], k_ref[...], c_ref[...], b_ref[...]
        b_b, k_b = pltpu.roll(bv, 1, 0), pltpu.roll(kv, 1, 0)
        wb_ref[...] = wv * b_b
        wk_ref[...] = wv * k_b
        rw = rv * wv
        for ref, x in ((rb_ref, rv * bv), (rk_ref, rv * kv), (rwb_ref, rw * b_b), (cb_ref, cv * b_b), (rwk_ref, rw * k_b),
                       (ck_ref, cv * k_b)):
            sums = _group_sums([x[:, q * LANES:(q + 1) * LANES] for q in range(N_PAIR)], bdv)
            for q in range(N_PAIR):
                ref[:, q * LANES:(q + 1) * LANES] = sums[q * CHUNK:(q + 1) * CHUNK]

        def sum0(x):
            return jnp.sum(x, axis=0, keepdims=True)

        def cut(ref, i):
            x = ref[pl.ds(i, 1), :]
            return [x[:, p * LANES:(p + 1) * LANES] for p in range(N_PAIR)]

        def sums_bf16(prods):
            return jnp.dot(jnp.concatenate(prods, axis=0).astype(BF16), bd1, preferred_element_type=F32)

        def two_steps(n, gs):
            ia = CHUNK - 1 - 2 * n
            ib = ia - 1
            r_a, w_a, k_a, c_a, b_a = [cut(ref, ia) for ref in (r_ref, w_ref, k_ref, c_ref, b_ref)]
            r_b, w_b, c_b = [cut(ref, ib) for ref in (r_ref, w_ref, c_ref)]
            wb, wk, rb_a, rk_a, rwb, cb, rwk, ck = [cut(ref, ia) for ref in pre[:8]]
            rb_b, rk_b = cut(rb_ref, ib), cut(rk_ref, ib)
            pairs = range(N_PAIR)
            sp_a = [spre_ref[ia, p] for p in pairs]
            sp_b = [spre_ref[ib, p] for p in pairs]
            dy_a = [dyb_ref[ia, p] for p in pairs]
            dy_b = [dyb_ref[ib, p] for p in pairs]
            chain = _group_sums([gs[p] * b_a[p] for p in pairs] + [gs[p] * wb[p] for p in pairs], bdv)
            off = sums_bf16([gs[p] * k_a[p] for p in pairs] + [gs[p] * wk[p] for p in pairs]
                            + [sp_a[p] * c_a[p] for p in pairs] + [sp_b[p] * c_b[p] for p in pairs])
            new = []
            for p in pairs:
                def part(x, q, p=p):
                    return x[(q * N_PAIR + p) * HEAD:(q * N_PAIR + p + 1) * HEAD]
                dsa_a = part(chain, 0) + dy_a[p] * rb_a[p]
                dv_a = part(off, 0) + dy_a[p] * rk_a[p]
                dsa_b = part(chain, 1) + dy_a[p] * rwb[p] + dsa_a * cb[p] + dy_b[p] * rb_b[p]
                dv_b = part(off, 1) + dy_a[p] * rwk[p] + dsa_a * ck[p] + dy_b[p] * rk_b[p]
                sa_a, sa_b = part(off, 2), part(off, 3)
                g_a = gs[p] + dy_a[p] * r_a[p]
                g_mid = g_a * w_a[p] + dsa_a * c_a[p]
                g_b = g_mid + dy_b[p] * r_b[p]
                new.append(g_b * w_b[p] + dsa_b * c_b[p])
                cols = pl.ds(p * LANES, LANES)
                for i, dy, s_post, s_pre, g, sa, dsa in ((ia, dy_a[p], snext_ref[p], sp_a[p], g_a, sa_a, dsa_a),
                                                         (ib, dy_b[p], sp_a[p], sp_b[p], g_b, sa_b, dsa_b)):
                    _store_row(dr_ref, i, cols, sum0(s_post * dy))
                    _store_row(dw_ref, i, cols, sum0(g * s_pre))
                    _store_row(db_ref, i, cols, sum0(g * sa))
                    _store_row(dk_ref, i, cols, sum0(g * vb_ref[i, p]))
                    _store_row(dc_ref, i, cols, sum0(s_pre * dsa))
                dvt_ref[p] = jnp.where(lane == ia, dv_a, jnp.where(lane == ib, dv_b, dvt_ref[p]))
                snext_ref[p] = sp_b[p]
            return tuple(new)

        gs = lax.fori_loop(0, CHUNK // 2, two_steps, tuple(g_ref[p] for p in range(N_PAIR)))
        for p in range(N_PAIR):
            g_ref[p] = gs[p]
        _cols_to_rows(dvt_ref, dv_ref)

        if n_ride:
            @pl.when(pl.program_id(0) == nch - 1)
            def _():
                _exchange_wait(x_refs, o_refs, sems, ride_gather)

    row = pl.BlockSpec((CHUNK, RW_W), lambda i: (nch - 1 - i, 0))
    big = pl.BlockSpec((CHUNK, N_PAIR, HEAD, LANES), lambda i: (nch - 1 - i, 0, 0, 0))
    any_spec = pl.BlockSpec(memory_space=pl.ANY)
    return pl.pallas_call(
        body, name=name, grid=(nch,),
        in_specs=[row] * 5 + [big, big, big, pl.BlockSpec((2 * LANES, LANES), lambda i: (0, 0))] + [any_spec] * n_ride,
        out_specs=[row] * 6 + [any_spec] * n_ride,
        out_shape=[jax.ShapeDtypeStruct((t, RW_W), F32)] * 6 + _exchange_out_shape(ride_xs, ride_gather),
        scratch_shapes=[pltpu.VMEM((N_PAIR, HEAD, LANES), F32)] * 3
        + [pltpu.VMEM((CHUNK, RW_W), F32)] * n_pre + (_exchange_sems(n_ride) if n_ride else []),
    )(r, w, k, c, b, vb, dyb, spre, bd, *ride_xs)


def _exchange(xs, *, gather, name, gather_too=()):
    n, n2 = len(xs), len(gather_too)

    def body(*refs):
        x1, x2 = refs[:n], refs[n:n + n2]
        o1, o2 = refs[n + n2:2 * n + n2], refs[2 * n + n2:2 * (n + n2)]
        sems1, sems2 = refs[2 * (n + n2):2 * (n + n2) + 3], refs[2 * (n + n2) + 3:]
        _exchange_start(x1, o1, sems1, gather)
        if n2:
            _exchange_start(x2, o2, sems2, True)
            _exchange_wait(x2, o2, sems2, True)
        _exchange_wait(x1, o1, sems1, gather)

    any_spec = pl.BlockSpec(memory_space=pl.ANY)
    return pl.pallas_call(
        body, name=name, in_specs=[any_spec] * (n + n2), out_specs=[any_spec] * (n + n2),
        out_shape=_exchange_out_shape(xs, gather) + _exchange_out_shape(list(gather_too), True),
        scratch_shapes=_exchange_sems(n) + (_exchange_sems(n2) if n2 else []),
    )(*xs, *gather_too)


def _adamw_rows(g, w, m, v):
    m = ADAM_B1 * m + (1.0 - ADAM_B1) * g
    v = ADAM_B2 * v + (1.0 - ADAM_B2) * (g * g)
    m_hat = m / (1.0 - ADAM_B1 ** ADAM_STEP)
    v_hat = v / (1.0 - ADAM_B2 ** ADAM_STEP)
    return -ADAM_LR * (m_hat / (jnp.sqrt(v_hat) + ADAM_EPS) + ADAM_WD * w), m, v


def _sum_slots(parts):
    g = parts[0].astype(F32)
    for q in range(1, N_DEV):
        g = g + parts[q].astype(F32)
    return g


def _reduce_adamw(parts, w, m, v, name):
    rows, cols = w.shape

    def fn(parts, w, m, v):
        g = _sum_slots(parts)
        return (g,) + _adamw_rows(g, w, m, v)

    return _rowwise(fn, [parts, w, m, v], [], [(cols, F32)] * 4, [], tm=_pick(rows, (256, 128, 64, 32, 16, 8)), name=name)


def _shift(x, n):
    return jnp.pad(x, ((n, 0), (0, 0)))[:-n]


def _unshift(x, n):
    return jnp.pad(x, ((0, n), (0, 0)))[n:]


def _add_n(xs, *, tm, name):
    def fn(*vals):
        s = vals[0]
        for x in vals[1:]:
            s = s + x
        return s
    return _rowwise(fn, xs, [], [(xs[0].shape[1], F32)], [], tm=tm, name=name)[0]


def _norm_fwd(h, g, *, tm, name):
    return _rowwise(lambda x, gg: _rms(x, gg), [h], [g], [(h.shape[1], BF16)], [], tm=tm, name=name)[0]


def _res_norm_fwd(h, f, g, scale, *, tm, name):
    return _rowwise(lambda hh, ff, gg: hh + scale * _rms(ff, gg), [h, f], [g], [(h.shape[1], F32)], [], tm=tm, name=name)[0]


def _norm_bwd(x, g, dy, scale, res, out_dtype, *, tm, name):
    if res is None:
        def fn(xx, dd, gg):
            dx, dg = _rms_bwd(xx, gg, dd * scale)
            return dx, dg
        rows = [x, dy]
    else:
        def fn(xx, dd, rr, gg):
            dx, dg = _rms_bwd(xx, gg, dd * scale)
            return dx + rr, dg
        rows = [x, dy, res]
    return _rowwise(fn, rows, [g], [(x.shape[1], out_dtype)], [(1, x.shape[1])], tm=tm, name=name)


def _ffn_fwd(h, g_pre, g_post, w24, wo4, tiles, tag):
    tb, ts = tiles
    a = _norm_fwd(h, g_pre, tm=ts, name=f"{tag}_norm")
    gu, s4 = _ffn_in(a, w24, tm=tb, name=f"{tag}_in")
    f = _mm(s4, wo4, trans_b=False, tm=tb, tn=D_MODEL, out_dtype=F32, name=f"{tag}_out")
    h_new = _res_norm_fwd(h, f, g_post, 0.5, tm=ts, name=f"{tag}_res")
    return h_new, (h, a, gu, s4, f)


def _ffn_bwd(dh_new, res, g_pre, g_post, w24, wo4, tiles, tag):
    tb, ts = tiles
    h, a, gu, s4, f = res
    t = h.shape[0]
    df, dg_post = _norm_bwd(f, g_post, dh_new, 0.5, None, BF16, tm=ts, name=f"{tag}_dres")
    dgu = _ffn_dswiglu(df, wo4, gu, tm=tb, name=f"{tag}_dswiglu")
    d_wo = _mm_tn(s4, df[None], tk=FFN_BLK, name=f"{tag}_dwout")
    dgu8 = dgu.reshape(2 * w24.shape[1], t, FFN_BLK)
    w8 = w24.reshape(2 * w24.shape[1], D_MODEL, FFN_BLK)
    da = _mm(dgu8, w8, trans_b=True, tm=tb, tn=D_MODEL // 2, out_dtype=F32, name=f"{tag}_da")
    d_win = _mm_tn(a[None], dgu8, tk=D_MODEL, name=f"{tag}_dwin")
    dh, dg_pre = _norm_bwd(h, g_pre, da, 1.0, dh_new, F32, tm=ts, name=f"{tag}_dnorm")
    return dh, dg_pre, dg_post, d_win, d_wo.reshape(N_DEV, -1, D_MODEL)


def _blockdiag(w4):
    n, b, _ = w4.shape
    eye = jnp.eye(n, dtype=w4.dtype)
    return (eye[:, None, :, None] * w4[:, :, None, :]).reshape(n * b, n * b)


def _blockdiag_grad(d):
    n = d.shape[0] // HEAD
    x = d.reshape(n, HEAD, n, HEAD)
    return jnp.stack([x[i, :, i, :] for i in range(n)])


def _row(v):
    return v.reshape(1, -1)


def _mixer_fwd(h, g_pre, g_post, wi, wo, P, bd, tiles, tag, ride=None):
    tb, ts = tiles
    a = _norm_fwd(h, g_pre, tm=ts, name=f"{tag}_norm")
    p = _mm(a[None], wi[None], trans_b=False, tm=tb, tn=N_IN, out_dtype=F32, name=f"{tag}_in")
    lx, lg = p[:, 0:256], p[:, 256:512]
    sb, scc, sx = p[:, 512:768], p[:, 768:1024], p[:, 1024:1280]
    z = p[:, 1280:]
    lxs = [lx, _shift(lx, 1), _shift(lx, 2), _shift(lx, 3)]
    cw = [_row(P['lru_conv_w'][kk]) for kk in range(4)]
    lru_par = cw + [_row(P['lru_conv_b']), _blockdiag(P['lru_wa']), _row(P['lru_ba']), _blockdiag(P['lru_wx']),
                    _row(P['lru_bx']), _row(P['lru_lambda'])]
    la, lb = _rowwise(_lru_pre, lxs, lru_par, [(LRU_W, F32)] * 2, [], tm=ts, name=f"{tag}_lru_pre")
    hs = _lru_scan(la, lb, name=f"{tag}_lru_scan")
    y_lru = _rowwise(lambda gg, hh, ng, b_: _lru_post(b_, gg, hh, ng), [lg, hs], [_row(P['lru_norm_g']), bd],
                     [(LRU_W, F32)], [], tm=ts, name=f"{tag}_lru_post")[0]
    sc_rows = [sb, scc, sx, _shift(scc, 1), _shift(sx, 1), _shift(scc, 2), _shift(sx, 2)]
    sc_par = [_row(P['sc_conv_w'][kk]) for kk in range(3)] + [_row(P['sc_norm_g'])]
    y_sc = _rowwise(lambda *v: _sc_fwd(v[-1], *v[:-1]), sc_rows, sc_par + [bd], [(SC_W, F32)], [], tm=ts,
                    name=f"{tag}_sc")[0]
    cuts = (0, RW_W, 2 * RW_W, 3 * RW_W, RW_IN)
    zs = [z[:, cuts[q]:cuts[q + 1]] for q in range(4)]
    z_rows = zs + [_shift(q, 1) for q in zs]
    pad = lambda m, lo: jnp.pad(m, ((lo, LANES - lo - m.shape[0]), (0, 0)))
    rw_par = [_row(P['rwkv_mu'][cuts[q]:cuts[q + 1]]) for q in range(4)]
    rw_par += [_row(P['rwkv_w0']), pad(P['rwkv_w2'], 0), _row(P['rwkv_a0']), pad(P['rwkv_a2'], 32),
               pad(P['rwkv_g2'], 64), _row(P['rwkv_k_k']), _row(P['rwkv_k_a'])]
    r, w, k2, c, b, v, g = _rowwise(lambda *vv: _rw_pre(vv[-1], *vv[:-1]), z_rows, rw_par + [bd], [(RW_W, F32)] * 7, [],
                                    tm=ts, name=f"{tag}_rw_pre")
    vb = _bcast_cols(v, _stacked_bf16(bd), name=f"{tag}_rw_vcols")
    spre, yt, *rode = _rw_scan(r, _shift(r, 1), w, k2, c, _unshift(c, 1), b, vb, _stacked_bf16(bd), name=f"{tag}_rw_scan",
                               ride=ride)
    y = _unshift(yt, 1)
    post_par =[_row(P['rwkv_lnx_w']), _row(P['rwkv_lnx_b']), _row(P['rwkv_r_k'])]
    y_rw = _rowwise(lambda *vv: _rw_post(vv[-1], *vv[:-1]), [y, r, k2, v, g], post_par + [bd], [(RW_W, F32)], [], tm=ts,
                    name=f"{tag}_rw_post")[0]
    ycat = jnp.concatenate([y_lru, y_sc, y_rw], axis=1).astype(BF16)
    m = _mm(ycat[None], wo[None], trans_b=False, tm=tb, tn=D_MODEL, out_dtype=F32, name=f"{tag}_out")
    h_new = _res_norm_fwd(h, m, g_post, 1.0, tm=ts, name=f"{tag}_res")
    res = dict(h=h, a=a, m=m, ycat=ycat, lxs=lxs, lru_par=lru_par, lg=lg, la=la, hs=hs, sc_rows=sc_rows, sc_par=sc_par,
               z_rows=z_rows, rw_par=rw_par, r=r, w=w, k2=k2, c=c, b=b, v=v, g=g, vb=vb, spre=spre, y=y, post_par=post_par)
    return h_new, res, rode


def _mixer_bwd(dh_new, R, g_pre, g_post, wi, wo, P, bd, tiles, tag, ride=None):
    tb, ts = tiles
    dm, dg_post = _norm_bwd(R['m'], g_post, dh_new, 1.0, None, BF16, tm=ts, name=f"{tag}_dres")
    dycat = _mm(dm[None], wo[None], trans_b=True, tm=tb, tn=D_MODEL, out_dtype=F32, name=f"{tag}_dycat")
    d_wo = _mm_tn(R['ycat'][None], dm[None], tk=D_MODEL // 2, name=f"{tag}_dwout")[0]
    dy_lru, dy_sc, dy_rw = dycat[:, 0:256], dycat[:, 256:512], dycat[:, 512:]
    G = {}
    d_lg, d_hs, G['lru_norm_g'] = _rowwise(
        lambda gg, hh, ct, ng, b_: _vjp_rows(_lru_post, 1, 3, 1)(b_, gg, hh, ng, ct),
        [R['lg'], R['hs'], dy_lru], [_row(P['lru_norm_g']), bd], [(LRU_W, F32)] * 2, [(1, LRU_W)], tm=ts,
        name=f"{tag}_lru_dpost")
    d_la, d_lb = _lru_scan_bwd(_unshift(R['la'], 1), _shift(R['hs'], 1), d_hs, name=f"{tag}_lru_dscan")

    def lru_pre_bwd(x0, x1, x2, x3, ca, cb_, *par):
        return _vjp_rows(_lru_pre, 0, 14, 2)(x0, x1, x2, x3, *par, ca, cb_)

    par_shapes = [tuple(q.shape) for q in R['lru_par']]
    outs = _rowwise(lru_pre_bwd, R['lxs'] + [d_la, d_lb], R['lru_par'], [(LRU_W, F32)] * 4, par_shapes, tm=ts,
                    name=f"{tag}_lru_dpre")
    dxs, dpar = outs[:4], outs[4:]
    d_lx = _add_n([dxs[0], _unshift(dxs[1], 1), _unshift(dxs[2], 2), _unshift(dxs[3], 3)], tm=ts, name=f"{tag}_lru_dx")
    G['lru_conv_w'] = jnp.concatenate(dpar[0:4], axis=0)
    G['lru_conv_b'] = dpar[4][0]
    G['lru_wa'] = _blockdiag_grad(dpar[5])
    G['lru_ba'] = dpar[6][0]
    G['lru_wx'] = _blockdiag_grad(dpar[7])
    G['lru_bx'] = dpar[8][0]
    G['lru_lambda'] = dpar[9][0]
    G['lru_norm_g'] = G['lru_norm_g'][0]

    def sc_bwd(*vv):
        rows7, ct, par4, b_ = vv[:7], vv[7], vv[8:12], vv[12]
        return _vjp_rows(_sc_fwd, 1, 11, 1)(b_, *rows7, *par4, ct)

    outs = _rowwise(sc_bwd, R['sc_rows'] + [dy_sc], R['sc_par'] + [bd], [(SC_W, F32)] * 7, [(1, SC_W)] * 4, tm=ts,
                    name=f"{tag}_sc_bwd")
    d_sb = outs[0]
    d_sc = _add_n([outs[1], _unshift(outs[3], 1), _unshift(outs[5], 2)], tm=ts, name=f"{tag}_sc_dc")
    d_sx = _add_n([outs[2], _unshift(outs[4], 1), _unshift(outs[6], 2)], tm=ts, name=f"{tag}_sc_dx")
    G['sc_conv_w'] = jnp.concatenate(outs[7:10], axis=0)
    G['sc_norm_g'] = outs[10][0]

    def rw_post_bwd(*vv):
        rows5, ct, par3, b_ = vv[:5], vv[5], vv[6:9], vv[9]
        return _vjp_rows(_rw_post, 1, 8, 1)(b_, *rows5, *par3, ct)

    outs = _rowwise(rw_post_bwd, [R['y'], R['r'], R['k2'], R['v'], R['g'], dy_rw], R['post_par'] + [bd],
                    [(RW_W, F32)] * 5, [(1, RW_W)] * 3, tm=ts, name=f"{tag}_rw_dpost")
    d_y, dr_p, dk_p, dv_p, d_g = outs[:5]
    G['rwkv_lnx_w'], G['rwkv_lnx_b'], G['rwkv_r_k'] = outs[5][0], outs[6][0], outs[7][0]
    dyb = _bcast_cols(d_y, _stacked_bf16(bd), name=f"{tag}_rw_dycols")
    dr_s, d_w, dk_s, d_c, d_b, dvt, *rode = _rw_scan_bwd(R['r'], R['w'], R['k2'], R['c'], R['b'], R['vb'], dyb, R['spre'],
                                                        _stacked_bf16(bd), name=f"{tag}_rw_dscan", ride=ride)
    dv_s = dvt

    def rw_pre_bwd(*vv):
        zrows = vv[0:8]
        dr1, dr2, dw_, dk1, dk2_, dc_, db_, dv1, dv2, dg_ = vv[8:18]
        par, b_ = vv[18:29], vv[29]
        return _vjp_rows(_rw_pre, 1, 19, 7)(b_, *zrows, *par, dr1 + dr2, dw_, dk1 + dk2_, dc_, db_, dv1 + dv2, dg_)

    par_shapes = [tuple(q.shape) for q in R['rw_par']]
    widths = [(q.shape[1], F32) for q in R['z_rows']]
    outs = _rowwise(rw_pre_bwd, R['z_rows'] + [dr_p, dr_s, d_w, dk_p, dk_s, d_c, d_b, dv_p, dv_s, d_g],
                    R['rw_par'] + [bd], widths, par_shapes, tm=ts, name=f"{tag}_rw_dpre")
    d_z = _add_n([jnp.concatenate(outs[0:4], axis=1), _unshift(jnp.concatenate(outs[4:8], axis=1), 1)], tm=ts,
                 name=f"{tag}_rw_dz")
    dpar = outs[8:]
    G['rwkv_mu'] = jnp.concatenate([q[0] for q in dpar[0:4]])
    G['rwkv_w0'], G['rwkv_a0'] = dpar[4][0], dpar[6][0]
    G['rwkv_w2'], G['rwkv_a2'], G['rwkv_g2'] = dpar[5][0:32], dpar[7][32:64], dpar[8][64:128]
    G['rwkv_k_k'], G['rwkv_k_a'] = dpar[9][0], dpar[10][0]

    dp = jnp.concatenate([d_lx, d_lg, d_sb, d_sc, d_sx, d_z], axis=1).astype(BF16)
    da = _mm(dp[None], wi[None], trans_b=True, tm=tb, tn=D_MODEL, out_dtype=F32, name=f"{tag}_da")
    d_wi = _mm_tn(R['a'][None], dp[None], tk=D_MODEL // 2, name=f"{tag}_dwin")[0]
    dh, dg_pre = _norm_bwd(R['h'], g_pre, da, 1.0, dh_new, F32, tm=ts, name=f"{tag}_dnorm")
    return dh, dg_pre, dg_post, d_wi, d_wo, G, rode


def _loss_rows(h, tgt, n_seq, *, tm, name):
    d = h.shape[1]

    def body(h_ref, t_ref, dh_ref, l_ref):
        i = pl.program_id(0)
        row = lax.broadcasted_iota(jnp.int32, (tm, 1), 0) + i * tm
        live = (row >= N_META) & (row < N_META + n_seq)
        e = jnp.where(live, h_ref[...] - t_ref[...], 0.0)
        dh_ref[...] = e * (1.0 / d)
        part = 0.5 * jnp.sum(jnp.sum(e * e, axis=1, keepdims=True) * (1.0 / d), axis=0, keepdims=True)

        @pl.when(i == 0)
        def _():
            l_ref[...] = part

        @pl.when(i > 0)
        def _():
            l_ref[...] += part

    blk = pl.BlockSpec((tm, d), lambda i: (i, 0))
    return pl.pallas_call(body, name=name, grid=(h.shape[0] // tm,), in_specs=[blk, blk],
                          out_specs=[blk, pl.BlockSpec((1, 1), lambda i: (0, 0))],
                          out_shape=[jax.ShapeDtypeStruct(h.shape, F32), jax.ShapeDtypeStruct((1, 1), F32)])(h, tgt)


def _pack(arrs, mult):
    flat = jnp.concatenate([a.reshape(-1).astype(F32) for a in arrs])
    n = flat.shape[0]
    tot = -(-n // mult) * mult
    return jnp.pad(flat, (0, tot - n)).reshape(-1, LANES)


def _unpack(buf, shapes):
    flat = buf.reshape(-1)
    out, off = [], 0
    for s in shapes:
        n = 1
        for q in s:
            n *= q
        out.append(flat[off:off + n].reshape(s))
        off += n
    return out


def _step(W, M, V, x, loss_target):
    n_seq = x.shape[1]
    t_real = N_META + n_seq
    t = (t_real // CHUNK + 1) * CHUNK
    tiles = (_pick(t, (704, 512, 256, 128, 64)), _pick(t, (352, 192, 128, 64)))
    me = 4 * lax.axis_index("x") + 2 * lax.axis_index("y") + lax.axis_index("c")
    n_layer = W['norm_g'].shape[0]

    small_sh = list(SMALL_SHARDED)
    packed = _pack([W[n] for n in small_sh], 8 * LANES)
    early = ['ffn1_w_in', 'ffn1_w_out', 'mix_w_in', 'mix_w_out']
    late = ['ffn2_w_in', 'ffn2_w_out']
    gathered = _exchange([W[n][0].astype(BF16) for n in early] + [packed], gather=True, name="gather_weights")
    big8 = [dict(zip(early, gathered[:-1]))] + [{} for _ in range(n_layer - 1)]
    pieces = [_unpack(gathered[-1][q], [W[n].shape for n in small_sh]) for q in range(N_DEV)]
    full = {n: W[n] for n in SMALL if n not in SMALL_SHARDED}
    for idx, n in enumerate(small_sh):
        full[n] = jnp.concatenate([pieces[q][idx] for q in range(N_DEV)], axis=SMALL_SHARDED[n])

    def ffn_weights(l, which):
        w24 = big8[l][f'{which}_w_in'].reshape(2, N_DEV // 2, D_MODEL, FFN_BLK)
        wo4 = big8[l][f'{which}_w_out'].reshape(N_DEV // 2, FFN_BLK, D_MODEL)
        return w24, wo4

    def mixer_weights(l):
        return big8[l]['mix_w_in'].transpose(1, 0, 2).reshape(D_MODEL, N_IN), big8[l]['mix_w_out'].reshape(D_MODEL, D_MODEL)

    bd = jnp.kron(jnp.eye(LANES // HEAD, dtype=F32), jnp.ones((HEAD, HEAD), F32))
    small_layer = [n for n in SMALL if n not in ('meta_tokens', 'norm_g')]

    h = jnp.concatenate([full['meta_tokens'], x[0], jnp.zeros((t - t_real, D_MODEL), F32)], axis=0)
    saved = []
    for l in range(n_layer):
        ng = [_row(full['norm_g'][l, q]) for q in range(6)]
        P = {n: full[n][l] for n in small_layer}
        w1 = ffn_weights(l, 'ffn1')
        h, r1 = _ffn_fwd(h, ng[0], ng[1], w1[0], w1[1], tiles, f"l{l}_ffn1")
        riders = [(l, n) for n in late] + ([(l + 1, n) for n in early] if l + 1 < n_layer else [])
        wm = mixer_weights(l)
        h, r2, rode = _mixer_fwd(h, ng[2], ng[3], wm[0], wm[1], P, bd, tiles, f"l{l}_mix",
                                 ride=([W[n][q].astype(BF16) for q, n in riders], True))
        for (q, n), arrived in zip(riders, rode):
            big8[q][n] = arrived
        w2 = ffn_weights(l, 'ffn2')
        h, r3 = _ffn_fwd(h, ng[4], ng[5], w2[0], w2[1], tiles, f"l{l}_ffn2")
        saved.append(((w1[0], w1[1], w2[0], w2[1], wm[0], wm[1]), ng, P, r1, r2, r3))

    tgt = jnp.pad(loss_target[0], ((N_META, t - t_real), (0, 0)))
    dh, loss_part = _loss_rows(h, tgt, n_seq, tm=tiles[1], name="loss")
    loss = lax.psum(loss_part[0, 0], MESH_AXES)

    small_grads = [None] * n_layer
    norm_grads = [None] * n_layer
    recv = [{} for _ in range(n_layer)]
    outgoing = []
    for l in reversed(range(n_layer)):
        lw, ng, P, r1, r2, r3 = saved[l]
        dh, g4, g5, d_win2, d_wo2 = _ffn_bwd(dh, r3, ng[4], ng[5], lw[2], lw[3], tiles, f"l{l}_ffn2")
        outgoing += [((l, 'ffn2_w_in'), d_win2), ((l, 'ffn2_w_out'), d_wo2)]
        dh, g2, g3, d_wi, d_wo, G, rode = _mixer_bwd(dh, r2, ng[2], ng[3], lw[4], lw[5], P, bd, tiles, f"l{l}_mix",
                                                     ride=([a for _, a in outgoing], False))
        for ((q, n), _), arrived in zip(outgoing, rode):
            recv[q][n] = arrived
        dh, g0, g1, d_win1, d_wo1 = _ffn_bwd(dh, r1, ng[0], ng[1], lw[0], lw[1], tiles, f"l{l}_ffn1")
        small_grads[l] = G
        norm_grads[l] = jnp.concatenate([g0, g1, g2, g3, g4, g5], axis=0)
        d_wi8 = d_wi.reshape(D_MODEL, N_DEV, N_IN // N_DEV).transpose(1, 0, 2)
        d_wo8 = d_wo.reshape(N_DEV, D_MODEL // N_DEV, D_MODEL)
        outgoing = [((l, 'ffn1_w_in'), d_win1), ((l, 'ffn1_w_out'), d_wo1), ((l, 'mix_w_in'), d_wi8), ((l, 'mix_w_out'), d_wo8)]
    gs = {n: jnp.stack([small_grads[l][n] for l in range(n_layer)]) for n in small_layer}
    gs['norm_g'] = jnp.stack(norm_grads)
    gs['meta_tokens'] = dh[:N_META]
    gpack = _pack([gs[n] for n in SMALL], 8 * LANES)
    *last, gall = _exchange([a for _, a in outgoing], gather=False, name="last_grad_exchange", gather_too=[gpack])
    for ((q, n), _), arrived in zip(outgoing, last):
        recv[q][n] = arrived
    gsum =_rowwise(lambda parts: _sum_slots(parts), [gall], [], [(LANES, F32)], [], tm=gall.shape[1],
                    name="sum_small_grads")[0]
    gfull = dict(zip(SMALL, _unpack(gsum, [gs[n].shape for n in SMALL])))

    def my_shard(n, a):
        if n not in SMALL_SHARDED:
            return a
        ax = SMALL_SHARDED[n]
        size = a.shape[ax] // N_DEV
        return lax.dynamic_slice_in_dim(a, me * size, size, axis=ax)

    g_loc = [my_shard(n, gfull[n]) for n in SMALL]
    shapes = [W[n].shape for n in SMALL]
    bufs = [_pack(g_loc, 8 * LANES)] + [_pack([D[n] for n in SMALL], 8 * LANES) for D in (W, M, V)]
    d_s, m_s, v_s = _rowwise(_adamw_rows, bufs, [], [(LANES, F32)] * 3, [], tm=bufs[0].shape[0], name="adamw_small")
    out = {'grad': dict(zip(SMALL, g_loc)), 'delta': dict(zip(SMALL, _unpack(d_s, shapes))),
           'm': dict(zip(SMALL, _unpack(m_s, shapes))), 'v': dict(zip(SMALL, _unpack(v_s, shapes)))}

    order = ['ffn1_w_in', 'ffn1_w_out', 'ffn2_w_in', 'ffn2_w_out', 'mix_w_in', 'mix_w_out']
    for idx, n in enumerate(order):
        per_layer = []
        for l in range(n_layer):
            parts = recv[l][n]
            rows, cols = W[n].shape[1], W[n].shape[2]
            per_layer.append(_reduce_adamw(parts.reshape(N_DEV, rows, cols), W[n][l], M[n][l], V[n][l],
                                           name=f"l{l}_adamw_{n}"))
        for q, key in enumerate(('grad', 'delta', 'm', 'v')):
            out[key][n] = jnp.stack([per_layer[l][q] for l in range(n_layer)])

    return (loss, dh[N_META:t_real][None],
            *[out['grad'][n] for n in WEIGHTS], *[out['delta'][n] for n in WEIGHTS],
            *[out['m'][n] for n in WEIGHTS], *[out['v'][n] for n in WEIGHTS])


def kernel(x, meta_tokens, norm_g, ffn1_w_in, ffn1_w_out, ffn2_w_in, ffn2_w_out, mix_w_in, mix_w_out, lru_conv_w, lru_conv_b, lru_wa, lru_ba, lru_wx, lru_bx, lru_lambda, lru_norm_g, sc_conv_w, sc_norm_g, rwkv_mu, rwkv_w0, rwkv_w2, rwkv_a0, rwkv_a2, rwkv_g2, rwkv_k_k, rwkv_k_a, rwkv_r_k, rwkv_lnx_w, rwkv_lnx_b, loss_target, m_meta_tokens, m_norm_g, m_ffn1_w_in, m_ffn1_w_out, m_ffn2_w_in, m_ffn2_w_out, m_mix_w_in, m_mix_w_out, m_lru_conv_w, m_lru_conv_b, m_lru_wa, m_lru_ba, m_lru_wx, m_lru_bx, m_lru_lambda, m_lru_norm_g, m_sc_conv_w, m_sc_norm_g, m_rwkv_mu, m_rwkv_w0, m_rwkv_w2, m_rwkv_a0, m_rwkv_a2, m_rwkv_g2, m_rwkv_k_k, m_rwkv_k_a, m_rwkv_r_k, m_rwkv_lnx_w, m_rwkv_lnx_b, v_meta_tokens, v_norm_g, v_ffn1_w_in, v_ffn1_w_out, v_ffn2_w_in, v_ffn2_w_out, v_mix_w_in, v_mix_w_out, v_lru_conv_w, v_lru_conv_b, v_lru_wa, v_lru_ba, v_lru_wx, v_lru_bx, v_lru_lambda, v_lru_norm_g, v_sc_conv_w, v_sc_norm_g, v_rwkv_mu, v_rwkv_w0, v_rwkv_w2, v_rwkv_a0, v_rwkv_a2, v_rwkv_g2, v_rwkv_k_k, v_rwkv_k_a, v_rwkv_r_k, v_rwkv_lnx_w, v_rwkv_lnx_b):
    w_vals = (meta_tokens, norm_g, ffn1_w_in, ffn1_w_out, ffn2_w_in, ffn2_w_out, mix_w_in, mix_w_out, lru_conv_w, lru_conv_b, lru_wa, lru_ba, lru_wx, lru_bx, lru_lambda, lru_norm_g, sc_conv_w, sc_norm_g, rwkv_mu, rwkv_w0, rwkv_w2, rwkv_a0, rwkv_a2, rwkv_g2, rwkv_k_k, rwkv_k_a, rwkv_r_k, rwkv_lnx_w, rwkv_lnx_b)
    m_vals = (m_meta_tokens, m_norm_g, m_ffn1_w_in, m_ffn1_w_out, m_ffn2_w_in, m_ffn2_w_out, m_mix_w_in, m_mix_w_out, m_lru_conv_w, m_lru_conv_b, m_lru_wa, m_lru_ba, m_lru_wx, m_lru_bx, m_lru_lambda, m_lru_norm_g, m_sc_conv_w, m_sc_norm_g, m_rwkv_mu, m_rwkv_w0, m_rwkv_w2, m_rwkv_a0, m_rwkv_a2, m_rwkv_g2, m_rwkv_k_k, m_rwkv_k_a, m_rwkv_r_k, m_rwkv_lnx_w, m_rwkv_lnx_b)
    v_vals = (v_meta_tokens, v_norm_g, v_ffn1_w_in, v_ffn1_w_out, v_ffn2_w_in, v_ffn2_w_out, v_mix_w_in, v_mix_w_out, v_lru_conv_w, v_lru_conv_b, v_lru_wa, v_lru_ba, v_lru_wx, v_lru_bx, v_lru_lambda, v_lru_norm_g, v_sc_conv_w, v_sc_norm_g, v_rwkv_mu, v_rwkv_w0, v_rwkv_w2, v_rwkv_a0, v_rwkv_a2, v_rwkv_g2, v_rwkv_k_k, v_rwkv_k_a, v_rwkv_r_k, v_rwkv_lnx_w, v_rwkv_lnx_b)
    return _step(dict(zip(WEIGHTS, w_vals)), dict(zip(WEIGHTS, m_vals)), dict(zip(WEIGHTS, v_vals)), x, loss_target)
```

```python
import functools

import jax
import jax.numpy as jnp
from jax import lax
from jax.experimental import pallas as pl
from jax.experimental.pallas import tpu as pltpu

F32 = jnp.float32
BF16 = jnp.bfloat16
PARAM_DOT = lax.Precision.DEFAULT

N_DEV = 8
MESH_AXES = ("x", "y", "c")
N_META = 16
D_MODEL = 1024
LRU_W = 256
SC_W = 256
RW_W = 512
HEAD = 64
LANES = 128
CHUNK = 64
RW_IN = 1664
N_IN = 2944
FFN_BLK = 704
RMS_EPS = 1e-6
LNX_EPS = 64e-5
LRU_C = 8.0
ADAM_LR, ADAM_B1, ADAM_B2, ADAM_EPS, ADAM_WD, ADAM_STEP = 0.001, 0.9, 0.999, 1e-08, 0.01, 10

WEIGHTS = ['meta_tokens', 'norm_g', 'ffn1_w_in', 'ffn1_w_out', 'ffn2_w_in', 'ffn2_w_out', 'mix_w_in', 'mix_w_out',
           'lru_conv_w', 'lru_conv_b', 'lru_wa', 'lru_ba', 'lru_wx', 'lru_bx', 'lru_lambda', 'lru_norm_g',
           'sc_conv_w', 'sc_norm_g', 'rwkv_mu', 'rwkv_w0', 'rwkv_w2', 'rwkv_a0', 'rwkv_a2', 'rwkv_g2', 'rwkv_k_k',
           'rwkv_k_a', 'rwkv_r_k', 'rwkv_lnx_w', 'rwkv_lnx_b']
BIG = ['ffn1_w_in', 'ffn1_w_out', 'ffn2_w_in', 'ffn2_w_out', 'mix_w_in', 'mix_w_out']
SMALL_SHARDED = {'meta_tokens': 1, 'norm_g': 2, 'lru_conv_w': 2, 'sc_conv_w': 2, 'rwkv_w2': 2, 'rwkv_a2': 2, 'rwkv_g2': 2}
SMALL = [n for n in WEIGHTS if n not in BIG]


def _pick(n, cands):
    for c in cands:
        if n % c == 0:
            return c
    raise ValueError(f"no tile for {n}")


def _rowwise(fn, rows, params, row_outs, acc_outs, *, tm, name):
    nr, npar, nro, nao = len(rows), len(params), len(row_outs), len(acc_outs)
    n_rows = rows[0].shape[-2]
    assert n_rows % tm == 0, (name, n_rows, tm)

    def body(*refs):
        vals = [r[...] for r in refs[:nr + npar]]
        outs = fn(*vals)
        if not isinstance(outs, (tuple, list)):
            outs = (outs,)
        assert len(outs) == nro + nao, (name, len(outs))
        for o_ref, o in zip(refs[nr + npar:nr + npar + nro], outs[:nro]):
            o_ref[...] = o.astype(o_ref.dtype)
        step = pl.program_id(0)
        for a_ref, a in zip(refs[nr + npar + nro:], outs[nro:]):
            @pl.when(step == 0)
            def _(a_ref=a_ref, a=a):
                a_ref[...] = a.astype(F32)

            @pl.when(step > 0)
            def _(a_ref=a_ref, a=a):
                a_ref[...] += a.astype(F32)

    def row_spec(shape):
        if len(shape) == 2:
            return pl.BlockSpec((tm, shape[1]), lambda i: (i, 0))
        return pl.BlockSpec((shape[0], tm, shape[2]), lambda i: (0, i, 0))

    def full_spec(shape):
        nd = len(shape)
        return pl.BlockSpec(tuple(shape), lambda i, nd=nd: (0,) * nd)

    in_specs = [row_spec(r.shape) for r in rows] + [full_spec(p.shape) for p in params]
    out_shape = [jax.ShapeDtypeStruct((n_rows, w), dt) for (w, dt) in row_outs]
    out_shape += [jax.ShapeDtypeStruct(tuple(s), F32) for s in acc_outs]
    out_specs = [row_spec((n_rows, w)) for (w, _) in row_outs] + [full_spec(s) for s in acc_outs]
    res = pl.pallas_call(body, name=name, grid=(n_rows // tm,), in_specs=in_specs, out_specs=out_specs,
                         out_shape=out_shape)(*rows, *params)
    return tuple(res)


def _mm(a3, b3, *, trans_b, tm, tn, out_dtype, name):
    nj, m, kb = a3.shape
    n = b3.shape[1] if trans_b else b3.shape[2]
    dims = (((1,), (1,)), ((), ())) if trans_b else (((1,), (0,)), ((), ()))

    def body(a_ref, b_ref, o_ref):
        acc = lax.dot_general(a_ref[0], b_ref[0], dims, preferred_element_type=F32)
        for j in range(1, nj):
            acc = acc + lax.dot_general(a_ref[j], b_ref[j], dims, preferred_element_type=F32)
        o_ref[...] = acc.astype(o_ref.dtype)

    if trans_b:
        b_spec = pl.BlockSpec((nj, tn, kb), lambda i, c: (0, c, 0))
    else:
        b_spec = pl.BlockSpec((nj, kb, tn), lambda i, c: (0, 0, c))
    return pl.pallas_call(
        body, name=name, grid=(m // tm, n // tn),
        in_specs=[pl.BlockSpec((nj, tm, kb), lambda i, c: (0, i, 0)), b_spec],
        out_specs=pl.BlockSpec((tm, tn), lambda i, c: (i, c)),
        out_shape=jax.ShapeDtypeStruct((m, n), out_dtype),
    )(a3, b3)


def _mm_tn(a3, b3, *, tk, name):
    ja, t, ka = a3.shape
    jb, _, n = b3.shape
    nj = max(ja, jb)

    def body(a_ref, b_ref, o_ref):
        o_ref[0] = lax.dot_general(a_ref[0], b_ref[0], (((0,), (0,)), ((), ())),
                                   preferred_element_type=F32).astype(o_ref.dtype)

    return pl.pallas_call(
        body, name=name, grid=(nj, ka // tk),
        in_specs=[pl.BlockSpec((1, t, tk), (lambda j, c: (j, 0, c)) if ja > 1 else (lambda j, c: (0, 0, c))),
                  pl.BlockSpec((1, t, n), (lambda j, c: (j, 0, 0)) if jb > 1 else (lambda j, c: (0, 0, 0)))],
        out_specs=pl.BlockSpec((1, tk, n), lambda j, c: (j, c, 0)),
        out_shape=jax.ShapeDtypeStruct((nj, ka, n), BF16),
    )(a3, b3)


def _ffn_in(a, w24, *, tm, name):
    t, d = a.shape
    nb, fb = w24.shape[1], w24.shape[3]

    def body(a_ref, w_ref, gu_ref, s_ref):
        x = a_ref[...]
        g = jnp.dot(x, w_ref[0, 0], preferred_element_type=F32)
        u = jnp.dot(x, w_ref[1, 0], preferred_element_type=F32)
        gu_ref[0, 0] = g.astype(BF16)
        gu_ref[1, 0] = u.astype(BF16)
        s_ref[0] = (g * jax.nn.sigmoid(g) * u).astype(BF16)

    return pl.pallas_call(
        body, name=name, grid=(nb, t // tm),
        in_specs=[pl.BlockSpec((tm, d), lambda j, i: (i, 0)), pl.BlockSpec((2, 1, d, fb), lambda j, i: (0, j, 0, 0))],
        out_specs=[pl.BlockSpec((2, 1, tm, fb), lambda j, i: (0, j, i, 0)), pl.BlockSpec((1, tm, fb), lambda j, i: (j, i, 0))],
        out_shape=[jax.ShapeDtypeStruct((2, nb, t, fb), BF16), jax.ShapeDtypeStruct((nb, t, fb), BF16)],
    )(a, w24)


def _ffn_dswiglu(df, wo4, gu, *, tm, name):
    t, d = df.shape
    nb, fb = wo4.shape[0], wo4.shape[1]

    def body(df_ref, wo_ref, gu_ref, dg_ref):
        ds = lax.dot_general(df_ref[...], wo_ref[0], (((1,), (1,)), ((), ())), preferred_element_type=F32)
        g = gu_ref[0, 0].astype(F32)
        u = gu_ref[1, 0].astype(F32)
        sig = jax.nn.sigmoid(g)
        dg_ref[0, 0] = (ds * u * sig * (1.0 + g * (1.0 - sig))).astype(BF16)
        dg_ref[1, 0] = (ds * g * sig).astype(BF16)

    return pl.pallas_call(
        body, name=name, grid=(nb, t // tm),
        in_specs=[pl.BlockSpec((tm, d), lambda j, i: (i, 0)), pl.BlockSpec((1, fb, d), lambda j, i: (j, 0, 0)),
                  pl.BlockSpec((2, 1, tm, fb), lambda j, i: (0, j, i, 0))],
        out_specs=pl.BlockSpec((2, 1, tm, fb), lambda j, i: (0, j, i, 0)),
        out_shape=jax.ShapeDtypeStruct((2, nb, t, fb), BF16),
    )(df, wo4, gu)


def _rms(x, g):
    return x * lax.rsqrt(jnp.mean(x * x, axis=-1, keepdims=True) + RMS_EPS) * g


def _rms_bwd(x, g, dy):
    rstd = lax.rsqrt(jnp.mean(x * x, axis=-1, keepdims=True) + RMS_EPS)
    xh = x * rstd
    dxh = dy * g
    dx = rstd * (dxh - xh * jnp.mean(dxh * xh, axis=-1, keepdims=True))
    return dx, jnp.sum(dy * xh, axis=0, keepdims=True)


def _seg_sum_impl(x, bd):
    bd2 = jnp.concatenate([bd, bd], axis=0).astype(BF16)
    hi = x.astype(BF16)
    lo = (x - hi.astype(F32)).astype(BF16)
    parts = [jnp.dot(jnp.concatenate([hi[:, q * LANES:(q + 1) * LANES], lo[:, q * LANES:(q + 1) * LANES]], axis=1), bd2,
                     preferred_element_type=F32) for q in range(x.shape[1] // LANES)]
    return parts[0] if len(parts) == 1 else jnp.concatenate(parts, axis=1)


@jax.custom_vjp
def _seg_sum(x, bd):
    return _seg_sum_impl(x, bd)


def _seg_sum_fwd(x, bd):
    return _seg_sum_impl(x, bd), bd


def _seg_sum_bwd(bd, ct):
    return _seg_sum_impl(ct, bd), jnp.zeros_like(bd)


_seg_sum.defvjp(_seg_sum_fwd, _seg_sum_bwd)


def _group_rms(y, g, bd):
    return y * lax.rsqrt(_seg_sum(y * y, bd) * (1.0 / HEAD) + RMS_EPS) * g


def _expm1(x):
    return jnp.where(jnp.abs(x) < 1e-2, x * (1.0 + x * (0.5 + x * (1.0 / 6.0))), jnp.exp(x) - 1.0)


def _lru_pre(x0, x1, x2, x3, cw0, cw1, cw2, cw3, cb, wa, ba, wx, bx, lam):
    u = x3 * cw0 + x2 * cw1 + x1 * cw2 + x0 * cw3 + cb
    r = jax.nn.sigmoid(jnp.dot(u, wa, preferred_element_type=F32, precision=PARAM_DOT) + ba)
    i = jax.nn.sigmoid(jnp.dot(u, wx, preferred_element_type=F32, precision=PARAM_DOT) + bx)
    log_a = -LRU_C * r * jax.nn.softplus(-lam)
    return jnp.exp(log_a), jnp.sqrt(-_expm1(2.0 * log_a)) * (i * u)


def _lru_post(bd, gate, hs, ng):
    return _group_rms(jax.nn.gelu(gate) * hs, ng, bd)


def _sc_fwd(bd, b, c0, x0, c1, x1, c2, x2, w0, w1, w2, ng):
    return _group_rms(b * (w0 * (c2 * x2) + w1 * (c1 * x1) + w2 * (c0 * x0)), ng, bd)


def _rw_pre(bd, zr, zk, zv, zt, sr, sk, sv, st, mur, muk, muv, mut, w0, w2p, a0, a2p, g2p, k_k, k_a):
    r, k, v, tail = zr + (sr - zr) * mur, zk + (sk - zk) * muk, zv + (sv - zv) * muv, zt + (st - zt) * mut
    lane = lax.broadcasted_iota(jnp.int32, tail.shape, 1)
    act = jnp.where(lane < 32, jnp.tanh(tail), jnp.where(lane < 64, tail, jax.nn.sigmoid(tail)))
    dot = functools.partial(jnp.dot, preferred_element_type=F32, precision=PARAM_DOT)
    w_log = -jax.nn.softplus(-(w0 + dot(act, w2p))) - 0.5
    w = jnp.exp(-jnp.exp(w_log))
    a = jax.nn.sigmoid(a0 + dot(act, a2p))
    g = dot(act, g2p)
    kk = k * k_k
    k2 = k * (1.0 + (a - 1.0) * k_a)
    kkn = kk * lax.rsqrt(jnp.maximum(_seg_sum(kk * kk, bd), 1e-24))
    return r, w, k2, -kkn, kkn * a, v, g


def _rw_post(bd, y, r, k2, v, g, lnw, lnb, r_k):
    mean = _seg_sum(y, bd) * (1.0 / HEAD)
    yc = y - mean
    var = _seg_sum(yc * yc, bd) * (1.0 / HEAD)
    yn = yc * lax.rsqrt(var + LNX_EPS) * lnw + lnb
    return (yn + _seg_sum(r * k2 * r_k, bd) * v) * g


def _vjp_rows(fwd, n_static, n_in, n_ct):
    def fn(*args):
        static, prim, cts = args[:n_static], args[n_static:n_static + n_in], args[n_static + n_in:]
        assert len(cts) == n_ct
        _, vjp = jax.vjp(functools.partial(fwd, *static), *prim)
        return vjp(cts[0] if n_ct == 1 else tuple(cts))
    return fn


def _all_to_all_copies(x_refs, o_refs, sems):
    send_sems, recv_sems, local_sems = sems
    mx, my, mc = lax.axis_index("x"), lax.axis_index("y"), lax.axis_index("c")
    me = 4 * mx + 2 * my + mc
    local, sends, recvs = [], [], []
    for k in range(len(x_refs)):
        local.append(pltpu.make_async_copy(x_refs[k].at[me], o_refs[k].at[me], local_sems.at[k]))
    for d in range(1, N_DEV):
        px, py, pc = mx ^ ((d >> 2) & 1), my ^ ((d >> 1) & 1), mc ^ (d & 1)
        peer = 4 * px + 2 * py + pc
        for k in range(len(x_refs)):
            common = dict(src_ref=x_refs[k].at[peer], send_sem=send_sems.at[k, d - 1], recv_sem=recv_sems.at[k, d - 1],
                          device_id=(px, py, pc), device_id_type=pl.DeviceIdType.MESH)
            sends.append(pltpu.make_async_remote_copy(dst_ref=o_refs[k].at[me], **common))
            recvs.append(pltpu.make_async_remote_copy(dst_ref=o_refs[k].at[peer], **common))
    return local, sends, recvs


def _gather2_copies(x_refs, o_refs, sems):
    send_sems, recv_sems, local_sems = sems
    mx, my, mc = lax.axis_index("x"), lax.axis_index("y"), lax.axis_index("c")
    sibling = (mx, my, 1 - mc)
    chips = [(1 - mx, my), (mx, 1 - my), (1 - mx, 1 - my)]

    def slot(px, py, pc):
        return 4 * px + 2 * py + pc

    out = dict(local=[], first=[], first_recv=[], ici_recv=[], passed=[], passed_recv=[])
    for k in range(len(x_refs)):
        def copy(sem, src, dst_slot, to, k=k):
            return pltpu.make_async_remote_copy(src_ref=src, dst_ref=o_refs[k].at[dst_slot], send_sem=send_sems.at[k, sem],
                                                recv_sem=recv_sems.at[k, sem], device_id=to, device_id_type=pl.DeviceIdType.MESH)
        me = slot(mx, my, mc)
        out['local'].append(pltpu.make_async_copy(x_refs[k], o_refs[k].at[me], local_sems.at[k]))
        out['first'].append(copy(0, x_refs[k], me, sibling))
        out['first_recv'].append(copy(0, x_refs[k], slot(mx, my, 1 - mc), sibling))
        for j, (px, py) in enumerate(chips):
            out['first'].append(copy(1 + j, x_refs[k], me, (px, py, mc)))
            out['ici_recv'].append(copy(1 + j, x_refs[k], slot(px, py, mc), (px, py, mc)))
            out['passed'].append(copy(4 + j, o_refs[k].at[slot(px, py, mc)], slot(px, py, mc), sibling))
            out['passed_recv'].append(copy(4 + j, x_refs[k], slot(px, py, 1 - mc), sibling))
    return out


def _exchange_start(x_refs, o_refs, sems, gather):
    if gather:
        cps = _gather2_copies(x_refs, o_refs, sems)
        for cp in cps['local'] + cps['first']:
            cp.start()
        return
    local, sends, _ = _all_to_all_copies(x_refs, o_refs, sems)
    for cp in local + sends:
        cp.start()


def _exchange_wait(x_refs, o_refs, sems, gather):
    if gather:
        cps = _gather2_copies(x_refs, o_refs, sems)
        for arrived, onward in zip(cps['ici_recv'], cps['passed']):
            arrived.wait_recv()
            onward.start()
        for cp in cps['first'] + cps['passed']:
            cp.wait_send()
        for cp in cps['first_recv'] + cps['passed_recv']:
            cp.wait_recv()
        for cp in cps['local']:
            cp.wait()
        return
    local, sends, recvs = _all_to_all_copies(x_refs, o_refs, sems)
    for cp in sends:
        cp.wait_send()
    for cp in recvs:
        cp.wait_recv()
    for cp in local:
        cp.wait()


def _exchange_out_shape(xs, gather):
    return [jax.ShapeDtypeStruct(((N_DEV,) + x.shape) if gather else x.shape, x.dtype) for x in xs]


def _exchange_sems(n):
    return [pltpu.SemaphoreType.DMA((n, N_DEV - 1)), pltpu.SemaphoreType.DMA((n, N_DEV - 1)), pltpu.SemaphoreType.DMA((n,))]


SUBLANES = 8


def _store_row(ref, i, cols, row):
    base = pl.multiple_of((i // SUBLANES) * SUBLANES, SUBLANES)
    sub = lax.broadcasted_iota(jnp.int32, (SUBLANES, row.shape[1]), 0)
    ref[pl.ds(base, SUBLANES), cols] = jnp.where(sub == i % SUBLANES, row, ref[pl.ds(base, SUBLANES), cols])


def _tile_scan(a, b, reverse):
    sub = lax.broadcasted_iota(jnp.int32, a.shape, 0)
    for sh in (1, 2, 4):
        if reverse:
            live = sub < SUBLANES - sh
            a_s, b_s = pltpu.roll(a, SUBLANES - sh, 0), pltpu.roll(b, SUBLANES - sh, 0)
        else:
            live = sub >= sh
            a_s, b_s = pltpu.roll(a, sh, 0), pltpu.roll(b, sh, 0)
        b = jnp.where(live, a * b_s, 0.0) + b
        a = jnp.where(live, a * a_s, a)
    return a, b


def _lru_scan(a, b, name):
    t, w = a.shape

    def body(a_ref, b_ref, h_ref):
        def tile(j, h):
            rows = pl.ds(pl.multiple_of(j * SUBLANES, SUBLANES), SUBLANES)
            ca, cb = _tile_scan(a_ref[rows, :], b_ref[rows, :], False)
            out = ca * h + cb
            h_ref[rows, :] = out
            return out[SUBLANES - 1:SUBLANES]
        lax.fori_loop(0, t // SUBLANES, tile, jnp.zeros((1, w), F32))

    return pl.pallas_call(body, name=name, out_shape=jax.ShapeDtypeStruct((t, w), F32))(a, b)


def _lru_scan_bwd(a_next, h_prev, dhs, name):
    t, w = dhs.shape

    def body(a_ref, h_ref, dh_ref, da_ref, db_ref):
        def tile(n, lam):
            rows = pl.ds(pl.multiple_of((t // SUBLANES - 1 - n) * SUBLANES, SUBLANES), SUBLANES)
            ca, cb = _tile_scan(a_ref[rows, :], dh_ref[rows, :], True)
            out = ca * lam + cb
            db_ref[rows, :] = out
            da_ref[rows, :] = out * h_ref[rows, :]
            return out[0:1]
        lax.fori_loop(0, t // SUBLANES, tile, jnp.zeros((1, w), F32))

    return pl.pallas_call(body, name=name, out_shape=[jax.ShapeDtypeStruct((t, w), F32)] * 2)(a_next, h_prev, dhs)


N_PAIR = RW_W // LANES


def _bcast_cols(v, bd2, name):
    t = v.shape[0]

    def body(v_ref, bd_ref, o_ref):
        bd1 = bd_ref[...][0:LANES]
        sub = lax.broadcasted_iota(jnp.int32, (HEAD, LANES), 0)
        own = lax.broadcasted_iota(jnp.int32, (HEAD, LANES), 1) % HEAD == sub
        for i in range(CHUNK):
            row = v_ref[i:i + 1, :]
            prods = [jnp.where(own, row[:, p * LANES:(p + 1) * LANES], 0.0) for p in range(N_PAIR)]
            sums = jnp.dot(jnp.concatenate(prods, axis=0).astype(BF16), bd1, preferred_element_type=F32)
            for p in range(N_PAIR):
                o_ref[i, p] = sums[p * HEAD:(p + 1) * HEAD].astype(BF16)

    return pl.pallas_call(
        body, name=name, grid=(t // CHUNK,),
        in_specs=[pl.BlockSpec((CHUNK, RW_W), lambda i: (i, 0)), pl.BlockSpec((2 * LANES, LANES), lambda i: (0, 0))],
        out_specs=pl.BlockSpec((CHUNK, N_PAIR, HEAD, LANES), lambda i: (i, 0, 0, 0)),
        out_shape=jax.ShapeDtypeStruct((t, N_PAIR, HEAD, LANES), BF16),
    )(v, bd2)


def _cols_to_rows(cols_ref, rows_ref):
    lane = lax.broadcasted_iota(jnp.int32, (CHUNK, LANES), 1)
    for p in range(N_PAIR):
        tile = cols_ref[p]
        sq = jnp.concatenate([tile, jnp.zeros_like(tile)], axis=0).T
        rows_ref[:, p * LANES:(p + 1) * LANES] = jnp.where(lane < HEAD, sq[0:CHUNK], pltpu.roll(sq[CHUNK:2 * CHUNK], HEAD, 1))


def _stacked_bf16(bd):
    return jnp.concatenate([bd, bd], axis=0).astype(BF16)


def _group_sums(prods, bd2):
    x = jnp.concatenate(prods, axis=0)
    hi = x.astype(BF16)
    lo = (x - hi.astype(F32)).astype(BF16)
    return jnp.dot(jnp.concatenate([hi, lo], axis=1), bd2, preferred_element_type=F32)


def _rw_scan(r, r_prev, w, k, c, c_next, b, vb, bd2, name, ride=None):
    t = w.shape[0]
    nch = t // CHUNK
    ride_xs, ride_gather = ride if ride is not None else ([], False)
    n_ride = len(ride_xs)

    def body(*refs):
        r_ref, rp_ref, w_ref, k_ref, c_ref, cn_ref, b_ref, vb_ref, bd_ref = refs[:9]
        x_refs = refs[9:9 + n_ride]
        spre_ref, y_ref = refs[9 + n_ride:11 + n_ride]
        o_refs = refs[11 + n_ride:11 + 2 * n_ride]
        s_ref, wc_ref, wr_ref, bc_ref, kc_ref, br_ref, kr_ref, yt_ref = refs[11 + 2 * n_ride:19 + 2 * n_ride]
        sems = refs[19 + 2 * n_ride:]

        @pl.when(pl.program_id(0) == 0)
        def _():
            s_ref[...] = jnp.zeros_like(s_ref)
            if n_ride:
                _exchange_start(x_refs, o_refs, sems, ride_gather)

        yt_ref[...] = jnp.zeros_like(yt_ref)
        bdv = bd_ref[...]
        lane = lax.broadcasted_iota(jnp.int32, (HEAD, LANES), 1) % CHUNK

        wv, cn, rv, bv, kv = w_ref[...], cn_ref[...], r_ref[...], b_ref[...], k_ref[...]
        wc_ref[...] = wv * cn
        wr_ref[...] = wv * rv
        for ref, x in ((bc_ref, bv * cn), (kc_ref, kv * cn), (br_ref, bv * rv), (kr_ref, kv * rv)):
            sums = _group_sums([x[:, q * LANES:(q + 1) * LANES] for q in range(N_PAIR)], bdv)
            for q in range(N_PAIR):
                ref[:, q * LANES:(q + 1) * LANES] = sums[q * CHUNK:(q + 1) * CHUNK]

        def cut(ref, i):
            x = ref[pl.ds(i, 1), :]
            return [x[:, p * LANES:(p + 1) * LANES] for p in range(N_PAIR)]

        def two_steps(j, st):
            i0 = 2 * j
            i1 = i0 + 1
            c0, wc0, rp0, wr0 = cut(c_ref, i0), cut(wc_ref, i0), cut(rp_ref, i0), cut(wr_ref, i0)
            w0, b0, k0, w1, b1, k1 = cut(w_ref, i0), cut(b_ref, i0), cut(k_ref, i0), cut(w_ref, i1), cut(b_ref, i1), cut(k_ref, i1)
            bc0, kc0, br0, kr0 = cut(bc_ref, i0), cut(kc_ref, i0), cut(br_ref, i0), cut(kr_ref, i0)
            pairs = range(N_PAIR)
            red = _group_sums([st[p] * c0[p] for p in pairs] + [st[p] * wc0[p] for p in pairs], bdv)
            out = _group_sums([st[p] * rp0[p] for p in pairs] + [st[p] * wr0[p] for p in pairs], bdv)
            new = []
            for p in pairs:
                v0, v1 = vb_ref[i0, p].astype(F32), vb_ref[i1, p].astype(F32)
                sa0 = red[p * HEAD:(p + 1) * HEAD]
                sa1 = red[(N_PAIR + p) * HEAD:(N_PAIR + p + 1) * HEAD] + sa0 * bc0[p] + v0 * kc0[p]
                y_before = out[p * HEAD:(p + 1) * HEAD]
                y0 = out[(N_PAIR + p) * HEAD:(N_PAIR + p + 1) * HEAD] + sa0 * br0[p] + v0 * kr0[p]
                spre_ref[i0, p] = st[p]
                s1 = st[p] * w0[p] + sa0 * b0[p] + v0 * k0[p]
                spre_ref[i1, p] = s1
                new.append(s1 * w1[p] + sa1 * b1[p] + v1 * k1[p])
                yt_ref[p] = jnp.where(lane == i0, y_before, jnp.where(lane == i1, y0, yt_ref[p]))
            return tuple(new)

        st = lax.fori_loop(0, CHUNK // 2, two_steps, tuple(s_ref[p] for p in range(N_PAIR)))
        for p in range(N_PAIR):
            s_ref[p] = st[p]
        _cols_to_rows(yt_ref, y_ref)

        if n_ride:
            @pl.when(pl.program_id(0) == nch - 1)
            def _():
                _exchange_wait(x_refs, o_refs, sems, ride_gather)

    row = pl.BlockSpec((CHUNK, RW_W), lambda i: (i, 0))
    big = pl.BlockSpec((CHUNK, N_PAIR, HEAD, LANES), lambda i: (i, 0, 0, 0))
    any_spec = pl.BlockSpec(memory_space=pl.ANY)
    return pl.pallas_call(
        body, name=name, grid=(nch,),
        in_specs=[row] * 7 + [big, pl.BlockSpec((2 * LANES, LANES), lambda i: (0, 0))] + [any_spec] * n_ride,
        out_specs=[big, row] + [any_spec] * n_ride,
        out_shape=[jax.ShapeDtypeStruct((t, N_PAIR, HEAD, LANES), F32), jax.ShapeDtypeStruct((t, RW_W), F32)]
        + _exchange_out_shape(ride_xs, ride_gather),
        scratch_shapes=[pltpu.VMEM((N_PAIR, HEAD, LANES), F32)] + [pltpu.VMEM((CHUNK, RW_W), F32)] * 6
        + [pltpu.VMEM((N_PAIR, HEAD, LANES), F32)] + (_exchange_sems(n_ride) if n_ride else []),
    )(r, r_prev, w, k, c, c_next, b, vb, bd2, *ride_xs)


def _rw_scan_bwd(r, w, k, c, b, vb, dyb, spre, bd, name, ride=None):
    t = r.shape[0]
    nch = t // CHUNK
    ride_xs, ride_gather = ride if ride is not None else ([], False)
    n_ride = len(ride_xs)
    n_pre = 8

    def body(*refs):
        r_ref, w_ref, k_ref, c_ref, b_ref, vb_ref, dyb_ref, spre_ref, bd_ref = refs[:9]
        x_refs = refs[9:9 + n_ride]
        dr_ref, dw_ref, dk_ref, dc_ref, db_ref, dv_ref = refs[9 + n_ride:15 + n_ride]
        o_refs = refs[15 + n_ride:15 + 2 * n_ride]
        g_ref, snext_ref, dvt_ref = refs[15 + 2 * n_ride:18 + 2 * n_ride]
        pre = refs[18 + 2 * n_ride:18 + 2 * n_ride + n_pre]
        sems = refs[18 + 2 * n_ride + n_pre:]
        wb_ref, wk_ref, rb_ref, rk_ref, rwb_ref, cb_ref, rwk_ref, ck_ref = pre[:8]

        @pl.when(pl.program_id(0) == 0)
        def _():
            g_ref[...] = jnp.zeros_like(g_ref)
            snext_ref[...] = jnp.zeros_like(snext_ref)
            if n_ride:
                _exchange_start(x_refs, o_refs, sems, ride_gather)

        for ref in (dr_ref, dw_ref, dk_ref, dc_ref, db_ref, dvt_ref):
            ref[...] = jnp.zeros_like(ref)
        bdv = bd_ref[...]
        bd1 = bdv[0:LANES]
        lane = lax.broadcasted_iota(jnp.int32, (HEAD, LANES), 1) % CHUNK

        rv, wv, kv, cv, bv = r_ref[...], w_ref[...], k_ref[...], c_ref[...], b_ref[...]
        b_b, k_b = pltpu.roll(bv, 1, 0), pltpu.roll(kv, 1, 0)
        wb_ref[...] = wv * b_b
        wk_ref[...] = wv * k_b
        rw = rv * wv
        for ref, x in ((rb_ref, rv * bv), (rk_ref, rv * kv), (rwb_ref, rw * b_b), (cb_ref, cv * b_b), (rwk_ref, rw * k_b),
                       (ck_ref, cv * k_b)):
            sums = _group_sums([x[:, q * LANES:(q + 1) * LANES] for q in range(N_PAIR)], bdv)
            for q in range(N_PAIR):
                ref[:, q * LANES:(q + 1) * LANES] = sums[q * CHUNK:(q + 1) * CHUNK]

        def sum0(x):
            return jnp.sum(x, axis=0, keepdims=True)

        def cut(ref, i):
            x = ref[pl.ds(i, 1), :]
            return [x[:, p * LANES:(p + 1) * LANES] for p in range(N_PAIR)]

        def sums_bf16(prods):
            return jnp.dot(jnp.concatenate(prods, axis=0).astype(BF16), bd1, preferred_element_type=F32)

        def two_steps(n, gs):
            ia = CHUNK - 1 - 2 * n
            ib = ia - 1
            r_a, w_a, k_a, c_a, b_a = [cut(ref, ia) for ref in (r_ref, w_ref, k_ref, c_ref, b_ref)]
            r_b, w_b, c_b = [cut(ref, ib) for ref in (r_ref, w_ref, c_ref)]
            wb, wk, rb_a, rk_a, rwb, cb, rwk, ck = [cut(ref, ia) for ref in pre[:8]]
            rb_b, rk_b = cut(rb_ref, ib), cut(rk_ref, ib)
            pairs = range(N_PAIR)
            sp_a = [spre_ref[ia, p] for p in pairs]
            sp_b = [spre_ref[ib, p] for p in pairs]
            dy_a = [dyb_ref[ia, p].astype(F32) for p in pairs]
            dy_b = [dyb_ref[ib, p].astype(F32) for p in pairs]
            chain = _group_sums([gs[p] * b_a[p] for p in pairs] + [gs[p] * wb[p] for p in pairs], bdv)
            off = sums_bf16([gs[p] * k_a[p] for p in pairs] + [gs[p] * wk[p] for p in pairs]
                            + [sp_a[p] * c_a[p] for p in pairs] + [sp_b[p] * c_b[p] for p in pairs])
            new = []
            for p in pairs:
                def part(x, q, p=p):
                    return x[(q * N_PAIR + p) * HEAD:(q * N_PAIR + p + 1) * HEAD]
                dsa_a = part(chain, 0) + dy_a[p] * rb_a[p]
                dv_a = part(off, 0) + dy_a[p] * rk_a[p]
                dsa_b = part(chain, 1) + dy_a[p] * rwb[p] + dsa_a * cb[p] + dy_b[p] * rb_b[p]
                dv_b = part(off, 1) + dy_a[p] * rwk[p] + dsa_a * ck[p] + dy_b[p] * rk_b[p]
                sa_a, sa_b = part(off, 2), part(off, 3)
                g_a = gs[p] + dy_a[p] * r_a[p]
                g_mid = g_a * w_a[p] + dsa_a * c_a[p]
                g_b = g_mid + dy_b[p] * r_b[p]
                new.append(g_b * w_b[p] + dsa_b * c_b[p])
                cols = pl.ds(p * LANES, LANES)
                for i, dy, s_post, s_pre, g, sa, dsa in ((ia, dy_a[p], snext_ref[p], sp_a[p], g_a, sa_a, dsa_a),
                                                         (ib, dy_b[p], sp_a[p], sp_b[p], g_b, sa_b, dsa_b)):
                    _store_row(dr_ref, i, cols, sum0(s_post * dy))
                    _store_row(dw_ref, i, cols, sum0(g * s_pre))
                    _store_row(db_ref, i, cols, sum0(g * sa))
                    _store_row(dk_ref, i, cols, sum0(g * vb_ref[i, p].astype(F32)))
                    _store_row(dc_ref, i, cols, sum0(s_pre * dsa))
                dvt_ref[p] = jnp.where(lane == ia, dv_a, jnp.where(lane == ib, dv_b, dvt_ref[p]))
                snext_ref[p] = sp_b[p]
            return tuple(new)

        gs = lax.fori_loop(0, CHUNK // 2, two_steps, tuple(g_ref[p] for p in range(N_PAIR)))
        for p in range(N_PAIR):
            g_ref[p] = gs[p]
        _cols_to_rows(dvt_ref, dv_ref)

        if n_ride:
            @pl.when(pl.program_id(0) == nch - 1)
            def _():
                _exchange_wait(x_refs, o_refs, sems, ride_gather)

    row = pl.BlockSpec((CHUNK, RW_W), lambda i: (nch - 1 - i, 0))
    big = pl.BlockSpec((CHUNK, N_PAIR, HEAD, LANES), lambda i: (nch - 1 - i, 0, 0, 0))
    any_spec = pl.BlockSpec(memory_space=pl.ANY)
    return pl.pallas_call(
        body, name=name, grid=(nch,),
        in_specs=[row] * 5 + [big, big, big, pl.BlockSpec((2 * LANES, LANES), lambda i: (0, 0))] + [any_spec] * n_ride,
        out_specs=[row] * 6 + [any_spec] * n_ride,
        out_shape=[jax.ShapeDtypeStruct((t, RW_W), F32)] * 6 + _exchange_out_shape(ride_xs, ride_gather),
        scratch_shapes=[pltpu.VMEM((N_PAIR, HEAD, LANES), F32)] * 3
        + [pltpu.VMEM((CHUNK, RW_W), F32)] * n_pre + (_exchange_sems(n_ride) if n_ride else []),
    )(r, w, k, c, b, vb, dyb, spre, bd, *ride_xs)


def _exchange(xs, *, gather, name, gather_too=()):
    n, n2 = len(xs), len(gather_too)

    def body(*refs):
        x1, x2 = refs[:n], refs[n:n + n2]
        o1, o2 = refs[n + n2:2 * n + n2], refs[2 * n + n2:2 * (n + n2)]
        sems1, sems2 = refs[2 * (n + n2):2 * (n + n2) + 3], refs[2 * (n + n2) + 3:]
        _exchange_start(x1, o1, sems1, gather)
        if n2:
            _exchange_start(x2, o2, sems2, True)
            _exchange_wait(x2, o2, sems2, True)
        _exchange_wait(x1, o1, sems1, gather)

    any_spec = pl.BlockSpec(memory_space=pl.ANY)
    return pl.pallas_call(
        body, name=name, in_specs=[any_spec] * (n + n2), out_specs=[any_spec] * (n + n2),
        out_shape=_exchange_out_shape(xs, gather) + _exchange_out_shape(list(gather_too), True),
        scratch_shapes=_exchange_sems(n) + (_exchange_sems(n2) if n2 else []),
    )(*xs, *gather_too)


def _adamw_rows(g, w, m, v):
    m = ADAM_B1 * m + (1.0 - ADAM_B1) * g
    v = ADAM_B2 * v + (1.0 - ADAM_B2) * (g * g)
    m_hat = m / (1.0 - ADAM_B1 ** ADAM_STEP)
    v_hat = v / (1.0 - ADAM_B2 ** ADAM_STEP)
    return -ADAM_LR * (m_hat / (jnp.sqrt(v_hat) + ADAM_EPS) + ADAM_WD * w), m, v


def _sum_slots(parts):
    g = parts[0].astype(F32)
    for q in range(1, N_DEV):
        g = g + parts[q].astype(F32)
    return g


def _reduce_adamw(parts, w, m, v, name):
    rows, cols = w.shape

    def fn(parts, w, m, v):
        g = _sum_slots(parts)
        return (g,) + _adamw_rows(g, w, m, v)

    return _rowwise(fn, [parts, w, m, v], [], [(cols, F32)] * 4, [], tm=_pick(rows, (256, 128, 64, 32, 16, 8)), name=name)


def _shift(x, n):
    return jnp.pad(x, ((n, 0), (0, 0)))[:-n]


def _unshift(x, n):
    return jnp.pad(x, ((0, n), (0, 0)))[n:]


def _add_n(xs, *, tm, name):
    def fn(*vals):
        s = vals[0]
        for x in vals[1:]:
            s = s + x
        return s
    return _rowwise(fn, xs, [], [(xs[0].shape[1], F32)], [], tm=tm, name=name)[0]


def _norm_fwd(h, g, *, tm, name):
    return _rowwise(lambda x, gg: _rms(x, gg), [h], [g], [(h.shape[1], BF16)], [], tm=tm, name=name)[0]


def _res_norm_fwd(h, f, g, scale, *, tm, name):
    return _rowwise(lambda hh, ff, gg: hh + scale * _rms(ff, gg), [h, f], [g], [(h.shape[1], F32)], [], tm=tm, name=name)[0]


def _norm_bwd(x, g, dy, scale, res, out_dtype, *, tm, name):
    if res is None:
        def fn(xx, dd, gg):
            dx, dg = _rms_bwd(xx, gg, dd * scale)
            return dx, dg
        rows = [x, dy]
    else:
        def fn(xx, dd, rr, gg):
            dx, dg = _rms_bwd(xx, gg, dd * scale)
            return dx + rr, dg
        rows = [x, dy, res]
    return _rowwise(fn, rows, [g], [(x.shape[1], out_dtype)], [(1, x.shape[1])], tm=tm, name=name)


def _ffn_fwd(h, g_pre, g_post, w24, wo4, tiles, tag):
    tb, ts = tiles
    a = _norm_fwd(h, g_pre, tm=ts, name=f"{tag}_norm")
    gu, s4 = _ffn_in(a, w24, tm=tb, name=f"{tag}_in")
    f = _mm(s4, wo4, trans_b=False, tm=tb, tn=D_MODEL, out_dtype=F32, name=f"{tag}_out")
    h_new = _res_norm_fwd(h, f, g_post, 0.5, tm=ts, name=f"{tag}_res")
    return h_new, (h, a, gu, s4, f)


def _ffn_bwd(dh_new, res, g_pre, g_post, w24, wo4, tiles, tag):
    tb, ts = tiles
    h, a, gu, s4, f = res
    t = h.shape[0]
    df, dg_post = _norm_bwd(f, g_post, dh_new, 0.5, None, BF16, tm=ts, name=f"{tag}_dres")
    dgu = _ffn_dswiglu(df, wo4, gu, tm=tb, name=f"{tag}_dswiglu")
    d_wo = _mm_tn(s4, df[None], tk=FFN_BLK, name=f"{tag}_dwout")
    dgu8 = dgu.reshape(2 * w24.shape[1], t, FFN_BLK)
    w8 = w24.reshape(2 * w24.shape[1], D_MODEL, FFN_BLK)
    da = _mm(dgu8, w8, trans_b=True, tm=tb, tn=D_MODEL // 2, out_dtype=F32, name=f"{tag}_da")
    d_win = _mm_tn(a[None], dgu8, tk=D_MODEL, name=f"{tag}_dwin")
    dh, dg_pre = _norm_bwd(h, g_pre, da, 1.0, dh_new, F32, tm=ts, name=f"{tag}_dnorm")
    return dh, dg_pre, dg_post, d_win, d_wo.reshape(N_DEV, -1, D_MODEL)


def _blockdiag(w4):
    n, b, _ = w4.shape
    eye = jnp.eye(n, dtype=w4.dtype)
    return (eye[:, None, :, None] * w4[:, :, None, :]).reshape(n * b, n * b)


def _blockdiag_grad(d):
    n = d.shape[0] // HEAD
    x = d.reshape(n, HEAD, n, HEAD)
    return jnp.stack([x[i, :, i, :] for i in range(n)])


def _row(v):
    return v.reshape(1, -1)


def _mixer_fwd(h, g_pre, g_post, wi, wo, P, bd, tiles, tag, ride=None):
    tb, ts = tiles
    a = _norm_fwd(h, g_pre, tm=ts, name=f"{tag}_norm")
    p = _mm(a[None], wi[None], trans_b=False, tm=tb, tn=N_IN, out_dtype=F32, name=f"{tag}_in")
    lx, lg = p[:, 0:256], p[:, 256:512]
    sb, scc, sx = p[:, 512:768], p[:, 768:1024], p[:, 1024:1280]
    z = p[:, 1280:]
    lxs = [lx, _shift(lx, 1), _shift(lx, 2), _shift(lx, 3)]
    cw = [_row(P['lru_conv_w'][kk]) for kk in range(4)]
    lru_par = cw + [_row(P['lru_conv_b']), _blockdiag(P['lru_wa']), _row(P['lru_ba']), _blockdiag(P['lru_wx']),
                    _row(P['lru_bx']), _row(P['lru_lambda'])]
    la, lb = _rowwise(_lru_pre, lxs, lru_par, [(LRU_W, F32)] * 2, [], tm=ts, name=f"{tag}_lru_pre")
    hs = _lru_scan(la, lb, name=f"{tag}_lru_scan")
    y_lru = _rowwise(lambda gg, hh, ng, b_: _lru_post(b_, gg, hh, ng), [lg, hs], [_row(P['lru_norm_g']), bd],
                     [(LRU_W, F32)], [], tm=ts, name=f"{tag}_lru_post")[0]
    sc_rows = [sb, scc, sx, _shift(scc, 1), _shift(sx, 1), _shift(scc, 2), _shift(sx, 2)]
    sc_par = [_row(P['sc_conv_w'][kk]) for kk in range(3)] + [_row(P['sc_norm_g'])]
    y_sc = _rowwise(lambda *v: _sc_fwd(v[-1], *v[:-1]), sc_rows, sc_par + [bd], [(SC_W, F32)], [], tm=ts,
                    name=f"{tag}_sc")[0]
    cuts = (0, RW_W, 2 * RW_W, 3 * RW_W, RW_IN)
    zs = [z[:, cuts[q]:cuts[q + 1]] for q in range(4)]
    z_rows = zs + [_shift(q, 1) for q in zs]
    pad = lambda m, lo: jnp.pad(m, ((lo, LANES - lo - m.shape[0]), (0, 0)))
    rw_par = [_row(P['rwkv_mu'][cuts[q]:cuts[q + 1]]) for q in range(4)]
    rw_par += [_row(P['rwkv_w0']), pad(P['rwkv_w2'], 0), _row(P['rwkv_a0']), pad(P['rwkv_a2'], 32),
               pad(P['rwkv_g2'], 64), _row(P['rwkv_k_k']), _row(P['rwkv_k_a'])]
    r, w, k2, c, b, v, g = _rowwise(lambda *vv: _rw_pre(vv[-1], *vv[:-1]), z_rows, rw_par + [bd], [(RW_W, F32)] * 7, [],
                                    tm=ts, name=f"{tag}_rw_pre")
    vb = _bcast_cols(v, _stacked_bf16(bd), name=f"{tag}_rw_vcols")
    spre, yt, *rode = _rw_scan(r, _shift(r, 1), w, k2, c, _unshift(c, 1), b, vb, _stacked_bf16(bd), name=f"{tag}_rw_scan",
                               ride=ride)
    y = _unshift(yt, 1)
    post_par =[_row(P['rwkv_lnx_w']), _row(P['rwkv_lnx_b']), _row(P['rwkv_r_k'])]
    y_rw = _rowwise(lambda *vv: _rw_post(vv[-1], *vv[:-1]), [y, r, k2, v, g], post_par + [bd], [(RW_W, F32)], [], tm=ts,
                    name=f"{tag}_rw_post")[0]
    ycat = jnp.concatenate([y_lru, y_sc, y_rw], axis=1).astype(BF16)
    m = _mm(ycat[None], wo[None], trans_b=False, tm=tb, tn=D_MODEL, out_dtype=F32, name=f"{tag}_out")
    h_new = _res_norm_fwd(h, m, g_post, 1.0, tm=ts, name=f"{tag}_res")
    res = dict(h=h, a=a, m=m, ycat=ycat, lxs=lxs, lru_par=lru_par, lg=lg, la=la, hs=hs, sc_rows=sc_rows, sc_par=sc_par,
               z_rows=z_rows, rw_par=rw_par, r=r, w=w, k2=k2, c=c, b=b, v=v, g=g, vb=vb, spre=spre, y=y, post_par=post_par)
    return h_new, res, rode


def _mixer_bwd(dh_new, R, g_pre, g_post, wi, wo, P, bd, tiles, tag, ride=None):
    tb, ts = tiles
    dm, dg_post = _norm_bwd(R['m'], g_post, dh_new, 1.0, None, BF16, tm=ts, name=f"{tag}_dres")
    dycat = _mm(dm[None], wo[None], trans_b=True, tm=tb, tn=D_MODEL, out_dtype=F32, name=f"{tag}_dycat")
    d_wo = _mm_tn(R['ycat'][None], dm[None], tk=D_MODEL // 2, name=f"{tag}_dwout")[0]
    dy_lru, dy_sc, dy_rw = dycat[:, 0:256], dycat[:, 256:512], dycat[:, 512:]
    G = {}
    d_lg, d_hs, G['lru_norm_g'] = _rowwise(
        lambda gg, hh, ct, ng, b_: _vjp_rows(_lru_post, 1, 3, 1)(b_, gg, hh, ng, ct),
        [R['lg'], R['hs'], dy_lru], [_row(P['lru_norm_g']), bd], [(LRU_W, F32)] * 2, [(1, LRU_W)], tm=ts,
        name=f"{tag}_lru_dpost")
    d_la, d_lb = _lru_scan_bwd(_unshift(R['la'], 1), _shift(R['hs'], 1), d_hs, name=f"{tag}_lru_dscan")

    def lru_pre_bwd(x0, x1, x2, x3, ca, cb_, *par):
        return _vjp_rows(_lru_pre, 0, 14, 2)(x0, x1, x2, x3, *par, ca, cb_)

    par_shapes = [tuple(q.shape) for q in R['lru_par']]
    outs = _rowwise(lru_pre_bwd, R['lxs'] + [d_la, d_lb], R['lru_par'], [(LRU_W, F32)] * 4, par_shapes, tm=ts,
                    name=f"{tag}_lru_dpre")
    dxs, dpar = outs[:4], outs[4:]
    d_lx = _add_n([dxs[0], _unshift(dxs[1], 1), _unshift(dxs[2], 2), _unshift(dxs[3], 3)], tm=ts, name=f"{tag}_lru_dx")
    G['lru_conv_w'] = jnp.concatenate(dpar[0:4], axis=0)
    G['lru_conv_b'] = dpar[4][0]
    G['lru_wa'] = _blockdiag_grad(dpar[5])
    G['lru_ba'] = dpar[6][0]
    G['lru_wx'] = _blockdiag_grad(dpar[7])
    G['lru_bx'] = dpar[8][0]
    G['lru_lambda'] = dpar[9][0]
    G['lru_norm_g'] = G['lru_norm_g'][0]

    def sc_bwd(*vv):
        rows7, ct, par4, b_ = vv[:7], vv[7], vv[8:12], vv[12]
        return _vjp_rows(_sc_fwd, 1, 11, 1)(b_, *rows7, *par4, ct)

    outs = _rowwise(sc_bwd, R['sc_rows'] + [dy_sc], R['sc_par'] + [bd], [(SC_W, F32)] * 7, [(1, SC_W)] * 4, tm=ts,
                    name=f"{tag}_sc_bwd")
    d_sb = outs[0]
    d_sc = _add_n([outs[1], _unshift(outs[3], 1), _unshift(outs[5], 2)], tm=ts, name=f"{tag}_sc_dc")
    d_sx = _add_n([outs[2], _unshift(outs[4], 1), _unshift(outs[6], 2)], tm=ts, name=f"{tag}_sc_dx")
    G['sc_conv_w'] = jnp.concatenate(outs[7:10], axis=0)
    G['sc_norm_g'] = outs[10][0]

    def rw_post_bwd(*vv):
        rows5, ct, par3, b_ = vv[:5], vv[5], vv[6:9], vv[9]
        return _vjp_rows(_rw_post, 1, 8, 1)(b_, *rows5, *par3, ct)

    outs = _rowwise(rw_post_bwd, [R['y'], R['r'], R['k2'], R['v'], R['g'], dy_rw], R['post_par'] + [bd],
                    [(RW_W, F32)] * 5, [(1, RW_W)] * 3, tm=ts, name=f"{tag}_rw_dpost")
    d_y, dr_p, dk_p, dv_p, d_g = outs[:5]
    G['rwkv_lnx_w'], G['rwkv_lnx_b'], G['rwkv_r_k'] = outs[5][0], outs[6][0], outs[7][0]
    dyb = _bcast_cols(d_y, _stacked_bf16(bd), name=f"{tag}_rw_dycols")
    dr_s, d_w, dk_s, d_c, d_b, dvt, *rode = _rw_scan_bwd(R['r'], R['w'], R['k2'], R['c'], R['b'], R['vb'], dyb, R['spre'],
                                                        _stacked_bf16(bd), name=f"{tag}_rw_dscan", ride=ride)
    dv_s = dvt

    def rw_pre_bwd(*vv):
        zrows = vv[0:8]
        dr1, dr2, dw_, dk1, dk2_, dc_, db_, dv1, dv2, dg_ = vv[8:18]
        par, b_ = vv[18:29], vv[29]
        return _vjp_rows(_rw_pre, 1, 19, 7)(b_, *zrows, *par, dr1 + dr2, dw_, dk1 + dk2_, dc_, db_, dv1 + dv2, dg_)

    par_shapes = [tuple(q.shape) for q in R['rw_par']]
    widths = [(q.shape[1], F32) for q in R['z_rows']]
    outs = _rowwise(rw_pre_bwd, R['z_rows'] + [dr_p, dr_s, d_w, dk_p, dk_s, d_c, d_b, dv_p, dv_s, d_g],
                    R['rw_par'] + [bd], widths, par_shapes, tm=ts, name=f"{tag}_rw_dpre")
    d_z = _add_n([jnp.concatenate(outs[0:4], axis=1), _unshift(jnp.concatenate(outs[4:8], axis=1), 1)], tm=ts,
                 name=f"{tag}_rw_dz")
    dpar = outs[8:]
    G['rwkv_mu'] = jnp.concatenate([q[0] for q in dpar[0:4]])
    G['rwkv_w0'], G['rwkv_a0'] = dpar[4][0], dpar[6][0]
    G['rwkv_w2'], G['rwkv_a2'], G['rwkv_g2'] = dpar[5][0:32], dpar[7][32:64], dpar[8][64:128]
    G['rwkv_k_k'], G['rwkv_k_a'] = dpar[9][0], dpar[10][0]

    dp = jnp.concatenate([d_lx, d_lg, d_sb, d_sc, d_sx, d_z], axis=1).astype(BF16)
    da = _mm(dp[None], wi[None], trans_b=True, tm=tb, tn=D_MODEL, out_dtype=F32, name=f"{tag}_da")
    d_wi = _mm_tn(R['a'][None], dp[None], tk=D_MODEL // 2, name=f"{tag}_dwin")[0]
    dh, dg_pre = _norm_bwd(R['h'], g_pre, da, 1.0, dh_new, F32, tm=ts, name=f"{tag}_dnorm")
    return dh, dg_pre, dg_post, d_wi, d_wo, G, rode


def _loss_rows(h, tgt, n_seq, *, tm, name):
    d = h.shape[1]

    def body(h_ref, t_ref, dh_ref, l_ref):
        i = pl.program_id(0)
        row = lax.broadcasted_iota(jnp.int32, (tm, 1), 0) + i * tm
        live = (row >= N_META) & (row < N_META + n_seq)
        e = jnp.where(live, h_ref[...] - t_ref[...], 0.0)
        dh_ref[...] = e * (1.0 / d)
        part = 0.5 * jnp.sum(jnp.sum(e * e, axis=1, keepdims=True) * (1.0 / d), axis=0, keepdims=True)

        @pl.when(i == 0)
        def _():
            l_ref[...] = part

        @pl.when(i > 0)
        def _():
            l_ref[...] += part

    blk = pl.BlockSpec((tm, d), lambda i: (i, 0))
    return pl.pallas_call(body, name=name, grid=(h.shape[0] // tm,), in_specs=[blk, blk],
                          out_specs=[blk, pl.BlockSpec((1, 1), lambda i: (0, 0))],
                          out_shape=[jax.ShapeDtypeStruct(h.shape, F32), jax.ShapeDtypeStruct((1, 1), F32)])(h, tgt)


def _pack(arrs, mult):
    flat = jnp.concatenate([a.reshape(-1).astype(F32) for a in arrs])
    n = flat.shape[0]
    tot = -(-n // mult) * mult
    return jnp.pad(flat, (0, tot - n)).reshape(-1, LANES)


def _unpack(buf, shapes):
    flat = buf.reshape(-1)
    out, off = [], 0
    for s in shapes:
        n = 1
        for q in s:
            n *= q
        out.append(flat[off:off + n].reshape(s))
        off += n
    return out


def _step(W, M, V, x, loss_target):
    n_seq = x.shape[1]
    t_real = N_META + n_seq
    t = (t_real // CHUNK + 1) * CHUNK
    tiles = (_pick(t, (704, 512, 256, 128, 64)), _pick(t, (352, 192, 128, 64)))
    me = 4 * lax.axis_index("x") + 2 * lax.axis_index("y") + lax.axis_index("c")
    n_layer = W['norm_g'].shape[0]

    small_sh = list(SMALL_SHARDED)
    packed = _pack([W[n] for n in small_sh], 8 * LANES)
    early = ['ffn1_w_in', 'ffn1_w_out', 'mix_w_in', 'mix_w_out']
    late = ['ffn2_w_in', 'ffn2_w_out']
    gathered = _exchange([W[n][0].astype(BF16) for n in early] + [packed], gather=True, name="gather_weights")
    big8 = [dict(zip(early, gathered[:-1]))] + [{} for _ in range(n_layer - 1)]
    pieces = [_unpack(gathered[-1][q], [W[n].shape for n in small_sh]) for q in range(N_DEV)]
    full = {n: W[n] for n in SMALL if n not in SMALL_SHARDED}
    for idx, n in enumerate(small_sh):
        full[n] = jnp.concatenate([pieces[q][idx] for q in range(N_DEV)], axis=SMALL_SHARDED[n])

    def ffn_weights(l, which):
        w24 = big8[l][f'{which}_w_in'].reshape(2, N_DEV // 2, D_MODEL, FFN_BLK)
        wo4 = big8[l][f'{which}_w_out'].reshape(N_DEV // 2, FFN_BLK, D_MODEL)
        return w24, wo4

    def mixer_weights(l):
        return big8[l]['mix_w_in'].transpose(1, 0, 2).reshape(D_MODEL, N_IN), big8[l]['mix_w_out'].reshape(D_MODEL, D_MODEL)

    bd = jnp.kron(jnp.eye(LANES // HEAD, dtype=F32), jnp.ones((HEAD, HEAD), F32))
    small_layer = [n for n in SMALL if n not in ('meta_tokens', 'norm_g')]

    h = jnp.concatenate([full['meta_tokens'], x[0], jnp.zeros((t - t_real, D_MODEL), F32)], axis=0)
    saved = []
    for l in range(n_layer):
        ng = [_row(full['norm_g'][l, q]) for q in range(6)]
        P = {n: full[n][l] for n in small_layer}
        w1 = ffn_weights(l, 'ffn1')
        h, r1 = _ffn_fwd(h, ng[0], ng[1], w1[0], w1[1], tiles, f"l{l}_ffn1")
        riders = [(l, n) for n in late] + ([(l + 1, n) for n in early] if l + 1 < n_layer else [])
        wm = mixer_weights(l)
        h, r2, rode = _mixer_fwd(h, ng[2], ng[3], wm[0], wm[1], P, bd, tiles, f"l{l}_mix",
                                 ride=([W[n][q].astype(BF16) for q, n in riders], True))
        for (q, n), arrived in zip(riders, rode):
            big8[q][n] = arrived
        w2 = ffn_weights(l, 'ffn2')
        h, r3 = _ffn_fwd(h, ng[4], ng[5], w2[0], w2[1], tiles, f"l{l}_ffn2")
        saved.append(((w1[0], w1[1], w2[0], w2[1], wm[0], wm[1]), ng, P, r1, r2, r3))

    tgt = jnp.pad(loss_target[0], ((N_META, t - t_real), (0, 0)))
    dh, loss_part = _loss_rows(h, tgt, n_seq, tm=tiles[1], name="loss")
    loss = lax.psum(loss_part[0, 0], MESH_AXES)

    small_grads = [None] * n_layer
    norm_grads = [None] * n_layer
    recv = [{} for _ in range(n_layer)]
    outgoing = []
    for l in reversed(range(n_layer)):
        lw, ng, P, r1, r2, r3 = saved[l]
        dh, g4, g5, d_win2, d_wo2 = _ffn_bwd(dh, r3, ng[4], ng[5], lw[2], lw[3], tiles, f"l{l}_ffn2")
        outgoing += [((l, 'ffn2_w_in'), d_win2), ((l, 'ffn2_w_out'), d_wo2)]
        dh, g2, g3, d_wi, d_wo, G, rode = _mixer_bwd(dh, r2, ng[2], ng[3], lw[4], lw[5], P, bd, tiles, f"l{l}_mix",
                                                     ride=([a for _, a in outgoing], False))
        for ((q, n), _), arrived in zip(outgoing, rode):
            recv[q][n] = arrived
        dh, g0, g1, d_win1, d_wo1 = _ffn_bwd(dh, r1, ng[0], ng[1], lw[0], lw[1], tiles, f"l{l}_ffn1")
        small_grads[l] = G
        norm_grads[l] = jnp.concatenate([g0, g1, g2, g3, g4, g5], axis=0)
        d_wi8 = d_wi.reshape(D_MODEL, N_DEV, N_IN // N_DEV).transpose(1, 0, 2)
        d_wo8 = d_wo.reshape(N_DEV, D_MODEL // N_DEV, D_MODEL)
        outgoing = [((l, 'ffn1_w_in'), d_win1), ((l, 'ffn1_w_out'), d_wo1), ((l, 'mix_w_in'), d_wi8), ((l, 'mix_w_out'), d_wo8)]
    gs = {n: jnp.stack([small_grads[l][n] for l in range(n_layer)]) for n in small_layer}
    gs['norm_g'] = jnp.stack(norm_grads)
    gs['meta_tokens'] = dh[:N_META]
    gpack = _pack([gs[n] for n in SMALL], 8 * LANES)
    *last, gall = _exchange([a for _, a in outgoing], gather=False, name="last_grad_exchange", gather_too=[gpack])
    for ((q, n), _), arrived in zip(outgoing, last):
        recv[q][n] = arrived
    gsum =_rowwise(lambda parts: _sum_slots(parts), [gall], [], [(LANES, F32)], [], tm=gall.shape[1],
                    name="sum_small_grads")[0]
    gfull = dict(zip(SMALL, _unpack(gsum, [gs[n].shape for n in SMALL])))

    def my_shard(n, a):
        if n not in SMALL_SHARDED:
            return a
        ax = SMALL_SHARDED[n]
        size = a.shape[ax] // N_DEV
        return lax.dynamic_slice_in_dim(a, me * size, size, axis=ax)

    g_loc = [my_shard(n, gfull[n]) for n in SMALL]
    shapes = [W[n].shape for n in SMALL]
    bufs = [_pack(g_loc, 8 * LANES)] + [_pack([D[n] for n in SMALL], 8 * LANES) for D in (W, M, V)]
    d_s, m_s, v_s = _rowwise(_adamw_rows, bufs, [], [(LANES, F32)] * 3, [], tm=bufs[0].shape[0], name="adamw_small")
    out = {'grad': dict(zip(SMALL, g_loc)), 'delta': dict(zip(SMALL, _unpack(d_s, shapes))),
           'm': dict(zip(SMALL, _unpack(m_s, shapes))), 'v': dict(zip(SMALL, _unpack(v_s, shapes)))}

    order = ['ffn1_w_in', 'ffn1_w_out', 'ffn2_w_in', 'ffn2_w_out', 'mix_w_in', 'mix_w_out']
    for idx, n in enumerate(order):
        per_layer = []
        for l in range(n_layer):
            parts = recv[l][n]
            rows, cols = W[n].shape[1], W[n].shape[2]
            per_layer.append(_reduce_adamw(parts.reshape(N_DEV, rows, cols), W[n][l], M[n][l], V[n][l],
                                           name=f"l{l}_adamw_{n}"))
        for q, key in enumerate(('grad', 'delta', 'm', 'v')):
            out[key][n] = jnp.stack([per_layer[l][q] for l in range(n_layer)])

    return (loss, dh[N_META:t_real][None],
            *[out['grad'][n] for n in WEIGHTS], *[out['delta'][n] for n in WEIGHTS],
            *[out['m'][n] for n in WEIGHTS], *[out['v'][n] for n in WEIGHTS])


def kernel(x, meta_tokens, norm_g, ffn1_w_in, ffn1_w_out, ffn2_w_in, ffn2_w_out, mix_w_in, mix_w_out, lru_conv_w, lru_conv_b, lru_wa, lru_ba, lru_wx, lru_bx, lru_lambda, lru_norm_g, sc_conv_w, sc_norm_g, rwkv_mu, rwkv_w0, rwkv_w2, rwkv_a0, rwkv_a2, rwkv_g2, rwkv_k_k, rwkv_k_a, rwkv_r_k, rwkv_lnx_w, rwkv_lnx_b, loss_target, m_meta_tokens, m_norm_g, m_ffn1_w_in, m_ffn1_w_out, m_ffn2_w_in, m_ffn2_w_out, m_mix_w_in, m_mix_w_out, m_lru_conv_w, m_lru_conv_b, m_lru_wa, m_lru_ba, m_lru_wx, m_lru_bx, m_lru_lambda, m_lru_norm_g, m_sc_conv_w, m_sc_norm_g, m_rwkv_mu, m_rwkv_w0, m_rwkv_w2, m_rwkv_a0, m_rwkv_a2, m_rwkv_g2, m_rwkv_k_k, m_rwkv_k_a, m_rwkv_r_k, m_rwkv_lnx_w, m_rwkv_lnx_b, v_meta_tokens, v_norm_g, v_ffn1_w_in, v_ffn1_w_out, v_ffn2_w_in, v_ffn2_w_out, v_mix_w_in, v_mix_w_out, v_lru_conv_w, v_lru_conv_b, v_lru_wa, v_lru_ba, v_lru_wx, v_lru_bx, v_lru_lambda, v_lru_norm_g, v_sc_conv_w, v_sc_norm_g, v_rwkv_mu, v_rwkv_w0, v_rwkv_w2, v_rwkv_a0, v_rwkv_a2, v_rwkv_g2, v_rwkv_k_k, v_rwkv_k_a, v_rwkv_r_k, v_rwkv_lnx_w, v_rwkv_lnx_b):
    w_vals = (meta_tokens, norm_g, ffn1_w_in, ffn1_w_out, ffn2_w_in, ffn2_w_out, mix_w_in, mix_w_out, lru_conv_w, lru_conv_b, lru_wa, lru_ba, lru_wx, lru_bx, lru_lambda, lru_norm_g, sc_conv_w, sc_norm_g, rwkv_mu, rwkv_w0, rwkv_w2, rwkv_a0, rwkv_a2, rwkv_g2, rwkv_k_k, rwkv_k_a, rwkv_r_k, rwkv_lnx_w, rwkv_lnx_b)
    m_vals = (m_meta_tokens, m_norm_g, m_ffn1_w_in, m_ffn1_w_out, m_ffn2_w_in, m_ffn2_w_out, m_mix_w_in, m_mix_w_out, m_lru_conv_w, m_lru_conv_b, m_lru_wa, m_lru_ba, m_lru_wx, m_lru_bx, m_lru_lambda, m_lru_norm_g, m_sc_conv_w, m_sc_norm_g, m_rwkv_mu, m_rwkv_w0, m_rwkv_w2, m_rwkv_a0, m_rwkv_a2, m_rwkv_g2, m_rwkv_k_k, m_rwkv_k_a, m_rwkv_r_k, m_rwkv_lnx_w, m_rwkv_lnx_b)
    v_vals = (v_meta_tokens, v_norm_g, v_ffn1_w_in, v_ffn1_w_out, v_ffn2_w_in, v_ffn2_w_out, v_mix_w_in, v_mix_w_out, v_lru_conv_w, v_lru_conv_b, v_lru_wa, v_lru_ba, v_lru_wx, v_lru_bx, v_lru_lambda, v_lru_norm_g, v_sc_conv_w, v_sc_norm_g, v_rwkv_mu, v_rwkv_w0, v_rwkv_w2, v_rwkv_a0, v_rwkv_a2, v_rwkv_g2, v_rwkv_k_k, v_rwkv_k_a, v_rwkv_r_k, v_rwkv_lnx_w, v_rwkv_lnx_b)
    return _step(dict(zip(WEIGHTS, w_vals)), dict(zip(WEIGHTS, m_vals)), dict(zip(WEIGHTS, v_vals)), x, loss_target)
```
